```python
import math
import jax, jax.numpy as jnp
from jax import lax
import numpy as np

D_MODEL = 4096
BATCH = 16
SEQ = 2048
DEPTH = 2

N_MIXERS = 2
N_A_LAYERS = (DEPTH + 1) // 2
N_B_LAYERS = DEPTH // 2
HEAD_DIM = 128
ATT_WIDTH = D_MODEL
N_HEADS = ATT_WIDTH // HEAD_DIM
Q_BLOCK = 128
SSM_WIDTH = D_MODEL
GROUP = 16
N_GROUPS = SSM_WIDTH // GROUP
STATE = 64
SCAN_CHUNK = 128
RMS_EPS = 1e-6
DT_MIN = 1e-3
DT_MAX = 1e-1

kernel_name = "hybrid_stickbreak_s5_decoder"


def _rmsnorm(x, g):
    xf = x.astype(jnp.float32)
    r = lax.rsqrt(jnp.mean(xf * xf, axis=-1, keepdims=True) + RMS_EPS)
    return (xf * r).astype(x.dtype) * g


def _stick_breaking_attention(q, k, v):
    S = q.shape[2]
    scale = 1.0 / math.sqrt(q.shape[-1])
    outs = []
    for blk in range(S // Q_BLOCK):
        q0 = blk * Q_BLOCK
        kl = q0 + Q_BLOCK
        qb = q[:, :, q0:kl]
        kb = k[:, :, :kl]
        vb = v[:, :, :kl]
        z = jnp.einsum('bhqd,bhkd->bhqk', qb, kb).astype(jnp.float32) * scale
        qpos = q0 + jnp.arange(Q_BLOCK)[:, None]
        kpos = jnp.arange(kl)[None, :]
        mask = kpos < qpos
        log_1m = jnp.where(mask, jax.nn.log_sigmoid(-z), 0.0)
        suffix = lax.cumsum(log_1m, axis=3, reverse=True) - log_1m
        w = jnp.where(mask, jnp.exp(jax.nn.log_sigmoid(z) + suffix), 0.0)
        outs.append(jnp.einsum('bhqk,bhkd->bhqd', w.astype(vb.dtype), vb))
    return jnp.concatenate(outs, axis=2)


def _attention_layer(x, norm_g, w_in, q_g, k_g, w_out):
    B, S, _ = x.shape
    h = _rmsnorm(x, norm_g)
    proj = h @ w_in
    q, k, v, gate = jnp.split(proj, 4, axis=-1)
    heads = lambda t: t.reshape(B, S, N_HEADS, HEAD_DIM)
    q = _rmsnorm(heads(q), q_g).transpose(0, 2, 1, 3)
    k = _rmsnorm(heads(k), k_g).transpose(0, 2, 1, 3)
    v = heads(v).transpose(0, 2, 1, 3)
    o = _stick_breaking_attention(q, k, v)
    o = o.transpose(0, 2, 1, 3).reshape(B, S, ATT_WIDTH).astype(x.dtype)
    return x + (o * jax.nn.silu(gate)) @ w_out


def _complex_affine_combine(e1, e2):
    a1r, a1i, b1r, b1i = e1
    a2r, a2i, b2r, b2i = e2
    ar = a2r * a1r - a2i * a1i
    ai = a2r * a1i + a2i * a1r
    br = a2r * b1r - a2i * b1i + b2r
    bi = a2r * b1i + a2i * b1r + b2i
    return ar, ai, br, bi


def _s5_scan(u, A_re, A_im, log_dt, B_re, B_im, C_re, C_im, D):
    Bsz, S, _ = u.shape
    f32 = jnp.float32
    A_re, A_im = A_re.astype(f32), A_im.astype(f32)
    B_re, B_im = B_re.astype(f32), B_im.astype(f32)
    C_re, C_im = C_re.astype(f32), C_im.astype(f32)
    dt = jnp.exp(log_dt.astype(f32))[:, None]
    mag = jnp.exp(A_re * dt)
    Ab_re = mag * jnp.cos(A_im * dt)
    Ab_im = mag * jnp.sin(A_im * dt)
    den = A_re * A_re + A_im * A_im
    f_re = ((Ab_re - 1.0) * A_re + Ab_im * A_im) / den
    f_im = (Ab_im * A_re - (Ab_re - 1.0) * A_im) / den
    Bb_re = f_re[..., None] * B_re - f_im[..., None] * B_im
    Bb_im = f_re[..., None] * B_im + f_im[..., None] * B_re
    Dg = D.astype(f32).reshape(N_GROUPS, GROUP)

    nc = S // SCAN_CHUNK
    uc = u.astype(f32).reshape(Bsz, nc, SCAN_CHUNK, N_GROUPS, GROUP).transpose(1, 2, 0, 3, 4)
    a_re_b = jnp.broadcast_to(Ab_re, (SCAN_CHUNK, Bsz, N_GROUPS, STATE))
    a_im_b = jnp.broadcast_to(Ab_im, (SCAN_CHUNK, Bsz, N_GROUPS, STATE))

    def chunk_step(carry, u_c):
        hp_re, hp_im = carry
        bu_re = jnp.einsum('cbgi,gpi->cbgp', u_c, Bb_re)
        bu_im = jnp.einsum('cbgi,gpi->cbgp', u_c, Bb_im)
        ac_re, ac_im, h_re, h_im = lax.associative_scan(
            _complex_affine_combine, (a_re_b, a_im_b, bu_re, bu_im), axis=0)
        h_re, h_im = (h_re + ac_re * hp_re[None] - ac_im * hp_im[None],
                      h_im + ac_re * hp_im[None] + ac_im * hp_re[None])
        y = (jnp.einsum('cbgp,gip->cbgi', h_re, C_re)
             - jnp.einsum('cbgp,gip->cbgi', h_im, C_im)
             + Dg * u_c)
        return (h_re[-1], h_im[-1]), y

    h0 = jnp.zeros((Bsz, N_GROUPS, STATE), f32)
    _, ys = lax.scan(chunk_step, (h0, h0), uc)
    return ys.transpose(2, 0, 1, 3, 4).reshape(Bsz, S, SSM_WIDTH)


def _ssm_layer(x, norm_g, w_in, A_re, A_im, log_dt, B_re, B_im, C_re, C_im, D, glu_w, glu_b, w_out):
    h = _rmsnorm(x, norm_g)
    proj = h @ w_in
    u, gate = jnp.split(proj, 2, axis=-1)
    y = _s5_scan(u, A_re, A_im, log_dt, B_re, B_im, C_re, C_im, D).astype(x.dtype)
    y = jax.nn.gelu(y)
    y = y * jax.nn.sigmoid(y @ glu_w + glu_b)
    return x + (y * jax.nn.silu(gate)) @ w_out


def _fwd_setup_inputs(seed: int = 0) -> dict:
    key = jax.random.key(seed)
    ks = jax.random.split(key, 20)
    n = jax.random.normal
    f32 = jnp.float32
    x = n(ks[0], (BATCH, SEQ, D_MODEL), f32)
    norm_g = 1.0 + 0.02 * n(ks[1], (DEPTH, D_MODEL), f32)
    attn_w_in = n(ks[2], (N_A_LAYERS, D_MODEL, 4 * ATT_WIDTH), f32) * D_MODEL ** -0.5
    attn_q_g = 1.0 + 0.02 * n(ks[3], (N_A_LAYERS, HEAD_DIM), f32)
    attn_k_g = 1.0 + 0.02 * n(ks[4], (N_A_LAYERS, HEAD_DIM), f32)
    attn_w_out = n(ks[5], (N_A_LAYERS, ATT_WIDTH, D_MODEL), f32) * ATT_WIDTH ** -0.5
    ssm_w_in = n(ks[6], (N_B_LAYERS, D_MODEL, 2 * SSM_WIDTH), f32) * D_MODEL ** -0.5
    ssm_A_re = -0.5 + 0.01 * n(ks[7], (N_B_LAYERS, N_GROUPS, STATE), f32)
    ssm_A_im = (math.pi * jnp.arange(STATE, dtype=f32)
                + 0.01 * n(ks[8], (N_B_LAYERS, N_GROUPS, STATE), f32))
    ssm_log_dt = jax.random.uniform(ks[9], (N_B_LAYERS, N_GROUPS), f32,
                                    math.log(DT_MIN), math.log(DT_MAX))
    ssm_B_re = n(ks[10], (N_B_LAYERS, N_GROUPS, STATE, GROUP), f32) * (2 * GROUP) ** -0.5
    ssm_B_im = n(ks[11], (N_B_LAYERS, N_GROUPS, STATE, GROUP), f32) * (2 * GROUP) ** -0.5
    ssm_C_re = n(ks[12], (N_B_LAYERS, N_GROUPS, GROUP, STATE), f32) * STATE ** -0.5
    ssm_C_im = n(ks[13], (N_B_LAYERS, N_GROUPS, GROUP, STATE), f32) * STATE ** -0.5
    ssm_D = n(ks[14], (N_B_LAYERS, SSM_WIDTH), f32)
    ssm_glu_w = n(ks[15], (N_B_LAYERS, SSM_WIDTH, SSM_WIDTH), f32) * SSM_WIDTH ** -0.5
    ssm_glu_b = 0.01 * n(ks[16], (N_B_LAYERS, SSM_WIDTH), f32)
    ssm_w_out = n(ks[17], (N_B_LAYERS, SSM_WIDTH, D_MODEL), f32) * SSM_WIDTH ** -0.5
    return {"x": x, "norm_g": norm_g, "attn_w_in": attn_w_in, "attn_q_g": attn_q_g,
            "attn_k_g": attn_k_g, "attn_w_out": attn_w_out, "ssm_w_in": ssm_w_in,
            "ssm_A_re": ssm_A_re, "ssm_A_im": ssm_A_im, "ssm_log_dt": ssm_log_dt,
            "ssm_B_re": ssm_B_re, "ssm_B_im": ssm_B_im, "ssm_C_re": ssm_C_re,
            "ssm_C_im": ssm_C_im, "ssm_D": ssm_D, "ssm_glu_w": ssm_glu_w,
            "ssm_glu_b": ssm_glu_b, "ssm_w_out": ssm_w_out}


def _fwd_reference(x, norm_g, attn_w_in, attn_q_g, attn_k_g, attn_w_out, ssm_w_in, ssm_A_re, ssm_A_im,
              ssm_log_dt, ssm_B_re, ssm_B_im, ssm_C_re, ssm_C_im, ssm_D, ssm_glu_w, ssm_glu_b,
              ssm_w_out):
    for i in range(DEPTH):
        j = i // N_MIXERS
        if i % N_MIXERS == 0:
            x = _attention_layer(x, norm_g[i], attn_w_in[j], attn_q_g[j], attn_k_g[j], attn_w_out[j])
        else:
            x = _ssm_layer(x, norm_g[i], ssm_w_in[j], ssm_A_re[j], ssm_A_im[j], ssm_log_dt[j],
                           ssm_B_re[j], ssm_B_im[j], ssm_C_re[j], ssm_C_im[j], ssm_D[j],
                           ssm_glu_w[j], ssm_glu_b[j], ssm_w_out[j])
    return x


import jax as _jax
import jax.numpy as _jnp

TWIN_FORMAT = 'train_step'
FWD_PARAMS = ['x', 'norm_g', 'attn_w_in', 'attn_q_g', 'attn_k_g', 'attn_w_out', 'ssm_w_in', 'ssm_A_re', 'ssm_A_im', 'ssm_log_dt', 'ssm_B_re', 'ssm_B_im', 'ssm_C_re', 'ssm_C_im', 'ssm_D', 'ssm_glu_w', 'ssm_glu_b', 'ssm_w_out']
TWIN_WEIGHTS = ['norm_g', 'attn_w_in', 'attn_q_g', 'attn_k_g', 'attn_w_out', 'ssm_w_in', 'ssm_A_re', 'ssm_A_im', 'ssm_log_dt', 'ssm_B_re', 'ssm_B_im', 'ssm_C_re', 'ssm_C_im', 'ssm_D', 'ssm_glu_w', 'ssm_glu_b', 'ssm_w_out']
TWIN_DIFF_INPUT = 'x'
TWIN_INPUTS = ['x', 'norm_g', 'attn_w_in', 'attn_q_g', 'attn_k_g', 'attn_w_out', 'ssm_w_in', 'ssm_A_re', 'ssm_A_im', 'ssm_log_dt', 'ssm_B_re', 'ssm_B_im', 'ssm_C_re', 'ssm_C_im', 'ssm_D', 'ssm_glu_w', 'ssm_glu_b', 'ssm_w_out', 'loss_target', 'm_norm_g', 'm_attn_w_in', 'm_attn_q_g', 'm_attn_k_g', 'm_attn_w_out', 'm_ssm_w_in', 'm_ssm_A_re', 'm_ssm_A_im', 'm_ssm_log_dt', 'm_ssm_B_re', 'm_ssm_B_im', 'm_ssm_C_re', 'm_ssm_C_im', 'm_ssm_D', 'm_ssm_glu_w', 'm_ssm_glu_b', 'm_ssm_w_out', 'v_norm_g', 'v_attn_w_in', 'v_attn_q_g', 'v_attn_k_g', 'v_attn_w_out', 'v_ssm_w_in', 'v_ssm_A_re', 'v_ssm_A_im', 'v_ssm_log_dt', 'v_ssm_B_re', 'v_ssm_B_im', 'v_ssm_C_re', 'v_ssm_C_im', 'v_ssm_D', 'v_ssm_glu_w', 'v_ssm_glu_b', 'v_ssm_w_out']
TWIN_OUTPUTS = ['loss', 'grad_x', 'grad_norm_g', 'grad_attn_w_in', 'grad_attn_q_g', 'grad_attn_k_g', 'grad_attn_w_out', 'grad_ssm_w_in', 'grad_ssm_A_re', 'grad_ssm_A_im', 'grad_ssm_log_dt', 'grad_ssm_B_re', 'grad_ssm_B_im', 'grad_ssm_C_re', 'grad_ssm_C_im', 'grad_ssm_D', 'grad_ssm_glu_w', 'grad_ssm_glu_b', 'grad_ssm_w_out', 'delta_norm_g', 'delta_attn_w_in', 'delta_attn_q_g', 'delta_attn_k_g', 'delta_attn_w_out', 'delta_ssm_w_in', 'delta_ssm_A_re', 'delta_ssm_A_im', 'delta_ssm_log_dt', 'delta_ssm_B_re', 'delta_ssm_B_im', 'delta_ssm_C_re', 'delta_ssm_C_im', 'delta_ssm_D', 'delta_ssm_glu_w', 'delta_ssm_glu_b', 'delta_ssm_w_out', 'new_m_norm_g', 'new_m_attn_w_in', 'new_m_attn_q_g', 'new_m_attn_k_g', 'new_m_attn_w_out', 'new_m_ssm_w_in', 'new_m_ssm_A_re', 'new_m_ssm_A_im', 'new_m_ssm_log_dt', 'new_m_ssm_B_re', 'new_m_ssm_B_im', 'new_m_ssm_C_re', 'new_m_ssm_C_im', 'new_m_ssm_D', 'new_m_ssm_glu_w', 'new_m_ssm_glu_b', 'new_m_ssm_w_out', 'new_v_norm_g', 'new_v_attn_w_in', 'new_v_attn_q_g', 'new_v_attn_k_g', 'new_v_attn_w_out', 'new_v_ssm_w_in', 'new_v_ssm_A_re', 'new_v_ssm_A_im', 'new_v_ssm_log_dt', 'new_v_ssm_B_re', 'new_v_ssm_B_im', 'new_v_ssm_C_re', 'new_v_ssm_C_im', 'new_v_ssm_D', 'new_v_ssm_glu_w', 'new_v_ssm_glu_b', 'new_v_ssm_w_out']
TWIN_LEAF_KINDS = {'loss': 'loss', 'grad_x': 'grad_x', 'grad_norm_g': 'grad_w', 'grad_attn_w_in': 'grad_w', 'grad_attn_q_g': 'grad_w', 'grad_attn_k_g': 'grad_w', 'grad_attn_w_out': 'grad_w', 'grad_ssm_w_in': 'grad_w', 'grad_ssm_A_re': 'grad_w', 'grad_ssm_A_im': 'grad_w', 'grad_ssm_log_dt': 'grad_w', 'grad_ssm_B_re': 'grad_w', 'grad_ssm_B_im': 'grad_w', 'grad_ssm_C_re': 'grad_w', 'grad_ssm_C_im': 'grad_w', 'grad_ssm_D': 'grad_w', 'grad_ssm_glu_w': 'grad_w', 'grad_ssm_glu_b': 'grad_w', 'grad_ssm_w_out': 'grad_w', 'delta_norm_g': 'delta_w', 'delta_attn_w_in': 'delta_w', 'delta_attn_q_g': 'delta_w', 'delta_attn_k_g': 'delta_w', 'delta_attn_w_out': 'delta_w', 'delta_ssm_w_in': 'delta_w', 'delta_ssm_A_re': 'delta_w', 'delta_ssm_A_im': 'delta_w', 'delta_ssm_log_dt': 'delta_w', 'delta_ssm_B_re': 'delta_w', 'delta_ssm_B_im': 'delta_w', 'delta_ssm_C_re': 'delta_w', 'delta_ssm_C_im': 'delta_w', 'delta_ssm_D': 'delta_w', 'delta_ssm_glu_w': 'delta_w', 'delta_ssm_glu_b': 'delta_w', 'delta_ssm_w_out': 'delta_w', 'new_m_norm_g': 'new_m', 'new_m_attn_w_in': 'new_m', 'new_m_attn_q_g': 'new_m', 'new_m_attn_k_g': 'new_m', 'new_m_attn_w_out': 'new_m', 'new_m_ssm_w_in': 'new_m', 'new_m_ssm_A_re': 'new_m', 'new_m_ssm_A_im': 'new_m', 'new_m_ssm_log_dt': 'new_m', 'new_m_ssm_B_re': 'new_m', 'new_m_ssm_B_im': 'new_m', 'new_m_ssm_C_re': 'new_m', 'new_m_ssm_C_im': 'new_m', 'new_m_ssm_D': 'new_m', 'new_m_ssm_glu_w': 'new_m', 'new_m_ssm_glu_b': 'new_m', 'new_m_ssm_w_out': 'new_m', 'new_v_norm_g': 'new_v', 'new_v_attn_w_in': 'new_v', 'new_v_attn_q_g': 'new_v', 'new_v_attn_k_g': 'new_v', 'new_v_attn_w_out': 'new_v', 'new_v_ssm_w_in': 'new_v', 'new_v_ssm_A_re': 'new_v', 'new_v_ssm_A_im': 'new_v', 'new_v_ssm_log_dt': 'new_v', 'new_v_ssm_B_re': 'new_v', 'new_v_ssm_B_im': 'new_v', 'new_v_ssm_C_re': 'new_v', 'new_v_ssm_C_im': 'new_v', 'new_v_ssm_D': 'new_v', 'new_v_ssm_glu_w': 'new_v', 'new_v_ssm_glu_b': 'new_v', 'new_v_ssm_w_out': 'new_v'}


def _forward(args):
    return _fwd_reference(*[args[k] for k in FWD_PARAMS])


def _output_shape():
    def fwd():
        inp = _fwd_setup_inputs(0)
        return _fwd_reference(*[inp[k] for k in FWD_PARAMS])
    out = _jax.eval_shape(fwd)
    return out.shape, out.dtype

N_MICROBATCH = 1
ADAM_LR = 0.001
ADAM_B1 = 0.9
ADAM_B2 = 0.999
ADAM_EPS = 1e-08
ADAM_WD = 0.01
ADAM_STEP = 10
PER_EXAMPLE_BATCH_AXIS = {'x': 0, 'loss_target': 0}
SHARED_INPUTS = []
_WEIGHT_DTYPES = {'norm_g': _jnp.float32, 'attn_w_in': _jnp.float32, 'attn_q_g': _jnp.float32, 'attn_k_g': _jnp.float32, 'attn_w_out': _jnp.float32, 'ssm_w_in': _jnp.float32, 'ssm_A_re': _jnp.float32, 'ssm_A_im': _jnp.float32, 'ssm_log_dt': _jnp.float32, 'ssm_B_re': _jnp.float32, 'ssm_B_im': _jnp.float32, 'ssm_C_re': _jnp.float32, 'ssm_C_im': _jnp.float32, 'ssm_D': _jnp.float32, 'ssm_glu_w': _jnp.float32, 'ssm_glu_b': _jnp.float32, 'ssm_w_out': _jnp.float32}
MOMENT_SCALE = {'norm_g': 1.914189e+00, 'attn_w_in': 3.068136e-02, 'attn_q_g': 5.695070e+00, 'attn_k_g': 5.694600e+00, 'attn_w_out': 3.069879e-02, 'ssm_w_in': 2.758477e-02, 'ssm_A_re': 1.834455e-03, 'ssm_A_im': 1.326875e-03, 'ssm_log_dt': 1.071950e+00, 'ssm_B_re': 1.147805e-03, 'ssm_B_im': 1.166710e-03, 'ssm_C_re': 1.631094e-03, 'ssm_C_im': 1.653269e-03, 'ssm_D': 4.361604e-01, 'ssm_glu_w': 8.760239e-02, 'ssm_glu_b': 2.706320e-01, 'ssm_w_out': 1.966137e-02}


def _to_microbatches(a, axis):
    t = _jnp.moveaxis(a, axis, 0)
    t = t.reshape((N_MICROBATCH, t.shape[0] // N_MICROBATCH) + t.shape[1:])
    return _jnp.moveaxis(t, 1, axis + 1)


def setup_inputs(seed: int = 0) -> dict:
    inp = _fwd_setup_inputs(seed)
    key = _jax.random.fold_in(_jax.random.key(seed), 7919)
    shape, _ = _output_shape()
    out = dict(inp)
    out["loss_target"] = _jax.random.normal(_jax.random.fold_in(key, 0), shape, _jnp.float32)
    for i, name in enumerate(TWIN_WEIGHTS):
        w = inp[name].astype(_jnp.float32)
        if MOMENT_SCALE is None:
            s = _jnp.sqrt(_jnp.mean(_jnp.square(w)) + 1e-30)
        else:
            s = MOMENT_SCALE[name]
        km, kv = _jax.random.split(_jax.random.fold_in(key, i + 1))
        out[name] = w
        out["m_" + name] = s * _jax.random.normal(km, w.shape, _jnp.float32)
        out["v_" + name] = (s * s) * _jax.random.uniform(kv, w.shape, _jnp.float32, 0.5, 1.5)
    if N_MICROBATCH > 1:
        for name, axis in PER_EXAMPLE_BATCH_AXIS.items():
            out[name] = _to_microbatches(out[name], axis)
    return {'x': out['x'], 'norm_g': out['norm_g'], 'attn_w_in': out['attn_w_in'], 'attn_q_g': out['attn_q_g'], 'attn_k_g': out['attn_k_g'], 'attn_w_out': out['attn_w_out'], 'ssm_w_in': out['ssm_w_in'], 'ssm_A_re': out['ssm_A_re'], 'ssm_A_im': out['ssm_A_im'], 'ssm_log_dt': out['ssm_log_dt'], 'ssm_B_re': out['ssm_B_re'], 'ssm_B_im': out['ssm_B_im'], 'ssm_C_re': out['ssm_C_re'], 'ssm_C_im': out['ssm_C_im'], 'ssm_D': out['ssm_D'], 'ssm_glu_w': out['ssm_glu_w'], 'ssm_glu_b': out['ssm_glu_b'], 'ssm_w_out': out['ssm_w_out'], 'loss_target': out['loss_target'], 'm_norm_g': out['m_norm_g'], 'm_attn_w_in': out['m_attn_w_in'], 'm_attn_q_g': out['m_attn_q_g'], 'm_attn_k_g': out['m_attn_k_g'], 'm_attn_w_out': out['m_attn_w_out'], 'm_ssm_w_in': out['m_ssm_w_in'], 'm_ssm_A_re': out['m_ssm_A_re'], 'm_ssm_A_im': out['m_ssm_A_im'], 'm_ssm_log_dt': out['m_ssm_log_dt'], 'm_ssm_B_re': out['m_ssm_B_re'], 'm_ssm_B_im': out['m_ssm_B_im'], 'm_ssm_C_re': out['m_ssm_C_re'], 'm_ssm_C_im': out['m_ssm_C_im'], 'm_ssm_D': out['m_ssm_D'], 'm_ssm_glu_w': out['m_ssm_glu_w'], 'm_ssm_glu_b': out['m_ssm_glu_b'], 'm_ssm_w_out': out['m_ssm_w_out'], 'v_norm_g': out['v_norm_g'], 'v_attn_w_in': out['v_attn_w_in'], 'v_attn_q_g': out['v_attn_q_g'], 'v_attn_k_g': out['v_attn_k_g'], 'v_attn_w_out': out['v_attn_w_out'], 'v_ssm_w_in': out['v_ssm_w_in'], 'v_ssm_A_re': out['v_ssm_A_re'], 'v_ssm_A_im': out['v_ssm_A_im'], 'v_ssm_log_dt': out['v_ssm_log_dt'], 'v_ssm_B_re': out['v_ssm_B_re'], 'v_ssm_B_im': out['v_ssm_B_im'], 'v_ssm_C_re': out['v_ssm_C_re'], 'v_ssm_C_im': out['v_ssm_C_im'], 'v_ssm_D': out['v_ssm_D'], 'v_ssm_glu_w': out['v_ssm_glu_w'], 'v_ssm_glu_b': out['v_ssm_glu_b'], 'v_ssm_w_out': out['v_ssm_w_out']}


def _loss(weights, diff, rest, loss_target):
    with _jax.named_scope("forward"):
        args = {**rest, TWIN_DIFF_INPUT: diff, **{k: w.astype(_WEIGHT_DTYPES[k]) for k, w in weights.items()}}
        y = _forward(args)
    with _jax.named_scope("loss_head"):
        err = _jnp.square(y.astype(_jnp.float32) - loss_target)
        return 0.5 * _jnp.sum(_jnp.mean(err, axis=-1)) if err.ndim else 0.5 * err


def _adamw(w, g, m, v):
    m = ADAM_B1 * m + (1.0 - ADAM_B1) * g
    v = ADAM_B2 * v + (1.0 - ADAM_B2) * _jnp.square(g)
    m_hat = m / (1.0 - ADAM_B1 ** ADAM_STEP)
    v_hat = v / (1.0 - ADAM_B2 ** ADAM_STEP)
    delta = -ADAM_LR * (m_hat / (_jnp.sqrt(v_hat) + ADAM_EPS) + ADAM_WD * w)
    return delta, m, v


def reference(x, norm_g, attn_w_in, attn_q_g, attn_k_g, attn_w_out, ssm_w_in, ssm_A_re, ssm_A_im, ssm_log_dt, ssm_B_re, ssm_B_im, ssm_C_re, ssm_C_im, ssm_D, ssm_glu_w, ssm_glu_b, ssm_w_out, loss_target, m_norm_g, m_attn_w_in, m_attn_q_g, m_attn_k_g, m_attn_w_out, m_ssm_w_in, m_ssm_A_re, m_ssm_A_im, m_ssm_log_dt, m_ssm_B_re, m_ssm_B_im, m_ssm_C_re, m_ssm_C_im, m_ssm_D, m_ssm_glu_w, m_ssm_glu_b, m_ssm_w_out, v_norm_g, v_attn_w_in, v_attn_q_g, v_attn_k_g, v_attn_w_out, v_ssm_w_in, v_ssm_A_re, v_ssm_A_im, v_ssm_log_dt, v_ssm_B_re, v_ssm_B_im, v_ssm_C_re, v_ssm_C_im, v_ssm_D, v_ssm_glu_w, v_ssm_glu_b, v_ssm_w_out):
    given = dict(x=x, norm_g=norm_g, attn_w_in=attn_w_in, attn_q_g=attn_q_g, attn_k_g=attn_k_g, attn_w_out=attn_w_out, ssm_w_in=ssm_w_in, ssm_A_re=ssm_A_re, ssm_A_im=ssm_A_im, ssm_log_dt=ssm_log_dt, ssm_B_re=ssm_B_re, ssm_B_im=ssm_B_im, ssm_C_re=ssm_C_re, ssm_C_im=ssm_C_im, ssm_D=ssm_D, ssm_glu_w=ssm_glu_w, ssm_glu_b=ssm_glu_b, ssm_w_out=ssm_w_out, loss_target=loss_target, m_norm_g=m_norm_g, m_attn_w_in=m_attn_w_in, m_attn_q_g=m_attn_q_g, m_attn_k_g=m_attn_k_g, m_attn_w_out=m_attn_w_out, m_ssm_w_in=m_ssm_w_in, m_ssm_A_re=m_ssm_A_re, m_ssm_A_im=m_ssm_A_im, m_ssm_log_dt=m_ssm_log_dt, m_ssm_B_re=m_ssm_B_re, m_ssm_B_im=m_ssm_B_im, m_ssm_C_re=m_ssm_C_re, m_ssm_C_im=m_ssm_C_im, m_ssm_D=m_ssm_D, m_ssm_glu_w=m_ssm_glu_w, m_ssm_glu_b=m_ssm_glu_b, m_ssm_w_out=m_ssm_w_out, v_norm_g=v_norm_g, v_attn_w_in=v_attn_w_in, v_attn_q_g=v_attn_q_g, v_attn_k_g=v_attn_k_g, v_attn_w_out=v_attn_w_out, v_ssm_w_in=v_ssm_w_in, v_ssm_A_re=v_ssm_A_re, v_ssm_A_im=v_ssm_A_im, v_ssm_log_dt=v_ssm_log_dt, v_ssm_B_re=v_ssm_B_re, v_ssm_B_im=v_ssm_B_im, v_ssm_C_re=v_ssm_C_re, v_ssm_C_im=v_ssm_C_im, v_ssm_D=v_ssm_D, v_ssm_glu_w=v_ssm_glu_w, v_ssm_glu_b=v_ssm_glu_b, v_ssm_w_out=v_ssm_w_out)
    weights = {n: given[n] for n in TWIN_WEIGHTS}
    shared = {n: given[n] for n in SHARED_INPUTS}
    per_example = {n: given[n] for n in ['x']}
    grad_fn = _jax.value_and_grad(_loss, argnums=(0, 1))

    def one_microbatch(ex, loss_target):
        ex = dict(ex)
        diff = ex.pop(TWIN_DIFF_INPUT)
        return grad_fn(weights, diff, {**shared, **ex}, loss_target)

    if N_MICROBATCH == 1:
        loss, (grad_w, grad_x) = one_microbatch(per_example, given["loss_target"])
    else:
        def body(carry, xs):
            loss_sum, grad_sum = carry
            l_k, (gw_k, gx_k) = one_microbatch(xs[0], xs[1])
            with _jax.named_scope("update"):
                return (loss_sum + l_k, _jax.tree.map(_jnp.add, grad_sum, gw_k)), gx_k

        init = (_jnp.zeros((), _jnp.float32), _jax.tree.map(_jnp.zeros_like, weights))
        (loss, grad_w), grad_x = _jax.lax.scan(body, init, (per_example, given["loss_target"]))
    with _jax.named_scope("update"):
        delta_w, new_m, new_v = {}, {}, {}
        for n in TWIN_WEIGHTS:
            delta_w[n], new_m[n], new_v[n] = _adamw(weights[n], grad_w[n], given["m_" + n], given["v_" + n])
    return (loss, grad_x, *[grad_w[n] for n in TWIN_WEIGHTS], *[delta_w[n] for n in TWIN_WEIGHTS],
            *[new_m[n] for n in TWIN_WEIGHTS], *[new_v[n] for n in TWIN_WEIGHTS])
```

```python
import functools
import math

import jax
import jax.numpy as jnp
from jax import lax
from jax.experimental import pallas as pl
from jax.experimental.pallas import tpu as pltpu

F32 = jnp.float32
BF16 = jnp.bfloat16

HEAD_DIM = 128
GROUP = 16
STATE = 64
RMS_EPS = 1e-6
ADAM_LR = 0.001
ADAM_B1 = 0.9
ADAM_B2 = 0.999
ADAM_EPS = 1e-08
ADAM_WD = 0.01
ADAM_STEP = 10

N_CHIPS = 4
N_DEV = 8
SUBLANES = 8
LANES = 128
VMEM_LIMIT = 56 * 1024 * 1024
ROW_BLOCK_ELEMS = 1 << 18
MESH = pl.DeviceIdType.MESH

NN = (((1,), (0,)), ((), ()))
NT = (((1,), (1,)), ((), ()))
TN = (((0,), (0,)), ((), ()))


def _params(*sem):
    return pltpu.CompilerParams(dimension_semantics=sem, vmem_limit_bytes=VMEM_LIMIT)


def _dot(a, b, dims=NN):
    return lax.dot_general(a, b, dims, preferred_element_type=F32)


def _split(a):
    hi = a.astype(BF16)
    lo = (a - hi.astype(F32)).astype(BF16)
    return hi, lo


def _dot3(a, b, dims=NN):
    a1, a2 = _split(a)
    b1, b2 = _split(b)
    return _dot(a1, b1, dims) + _dot(a1, b2, dims) + _dot(a2, b1, dims)


def _sigmoid(x):
    return 1.0 / (1.0 + jnp.exp(-x))


def _silu_parts(x):
    s = _sigmoid(x)
    return x * s, s * (1.0 + x * (1.0 - s))


_GELU_C = math.sqrt(2.0 / math.pi)


def _gelu_parts(x):
    x2 = x * x
    t = jnp.tanh(_GELU_C * (x + 0.044715 * x * x2))
    val = 0.5 * x * (1.0 + t)
    der = 0.5 * (1.0 + t) + 0.5 * x * (1.0 - t * t) * _GELU_C * (1.0 + 3.0 * 0.044715 * x2)
    return val, der


def _matmul(a, b, *, name, ta=False, tb=False, residual=None, out_dtype=F32):
    if ta:
        K, M = a.shape
    else:
        M, K = a.shape
    N = b.shape[0] if tb else b.shape[1]
    tm, tn, tk = min(M, 1024), min(N, 1024), min(K, 512)
    nk = K // tk
    dims = ((((0,) if ta else (1,)), ((1,) if tb else (0,))), ((), ()))

    def body(*refs):
        if residual is None:
            a_ref, b_ref, o_ref, acc = refs
        else:
            a_ref, b_ref, r_ref, o_ref, acc = refs
        k = pl.program_id(2)

        @pl.when(k == 0)
        def _():
            acc[...] = jnp.zeros_like(acc)

        acc[...] += _dot(a_ref[...].astype(BF16), b_ref[...].astype(BF16), dims)

        @pl.when(k == nk - 1)
        def _():
            r = acc[...]
            if residual is not None:
                r = r + r_ref[...].astype(F32)
            o_ref[...] = r.astype(out_dtype)

    a_spec = pl.BlockSpec((tk, tm), lambda i, j, k: (k, i)) if ta else pl.BlockSpec((tm, tk), lambda i, j, k: (i, k))
    b_spec = pl.BlockSpec((tn, tk), lambda i, j, k: (j, k)) if tb else pl.BlockSpec((tk, tn), lambda i, j, k: (k, j))
    in_specs = [a_spec, b_spec]
    args = [a, b]
    if residual is not None:
        in_specs.append(pl.BlockSpec((tm, tn), lambda i, j, k: (i, j)))
        args.append(residual)
    return pl.pallas_call(
        body, name=name, grid=(M // tm, N // tn, nk), in_specs=in_specs,
        out_specs=pl.BlockSpec((tm, tn), lambda i, j, k: (i, j)),
        out_shape=jax.ShapeDtypeStruct((M, N), out_dtype),
        scratch_shapes=[pltpu.VMEM((tm, tn), F32)],
        compiler_params=_params("parallel", "parallel", "arbitrary"),
    )(*args)


def _row_tile(T, C):
    tr = max(SUBLANES, min(T, ROW_BLOCK_ELEMS // C) // SUBLANES * SUBLANES)
    while T % tr:
        tr -= SUBLANES
    return tr


def _rows_call(body, name, T, C, row_ins, full_ins, row_outs, acc_outs=()):
    tr = _row_tile(T, C)
    row_spec = pl.BlockSpec((tr, C), lambda i: (i, 0))
    in_specs = [row_spec] * len(row_ins) + [pl.BlockSpec(f.shape, lambda i, n=f.ndim: (0,) * n) for f in full_ins]
    out_specs = [row_spec] * len(row_outs) + [pl.BlockSpec(s, lambda i, n=len(s): (0,) * n) for s in acc_outs]
    out_shape = [jax.ShapeDtypeStruct((T, C), d) for d in row_outs] + [jax.ShapeDtypeStruct(s, F32) for s in acc_outs]
    return pl.pallas_call(
        body, name=name, grid=(T // tr,), in_specs=in_specs, out_specs=out_specs, out_shape=out_shape,
        compiler_params=_params("arbitrary" if acc_outs else "parallel"),
    )(*row_ins, *full_ins)


def _rmsnorm_fwd(x, g, name):
    T, C = x.shape

    def body(x_ref, g_ref, h_ref):
        xv = x_ref[...]
        r = lax.rsqrt(jnp.mean(xv * xv, axis=-1, keepdims=True) + RMS_EPS)
        h_ref[...] = ((xv * r) * g_ref[...]).astype(BF16)

    return _rows_call(body, name, T, C, [x], [g.reshape(1, C)], [BF16])[0]


def _rmsnorm_bwd(x, g, dh, dres, name):
    T, C = x.shape

    def body(x_ref, dh_ref, dres_ref, g_ref, dx_ref, dxb_ref, dg_ref):
        @pl.when(pl.program_id(0) == 0)
        def _():
            dg_ref[...] = jnp.zeros_like(dg_ref)

        xv = x_ref[...]
        dhv = dh_ref[...]
        r = lax.rsqrt(jnp.mean(xv * xv, axis=-1, keepdims=True) + RMS_EPS)
        xn = xv * r
        dg_ref[...] += jnp.sum(dhv * xn, axis=0, keepdims=True)
        dxn = dhv * g_ref[...]
        dx = dres_ref[...] + r * (dxn - xn * jnp.mean(dxn * xn, axis=-1, keepdims=True))
        dx_ref[...] = dx
        dxb_ref[...] = dx.astype(BF16)

    return _rows_call(body, name, T, C, [x, dh, dres], [g.reshape(1, C)], [F32, BF16], [(1, C)])


def _qknorm_fwd(q, g, name):
    T, C = q.shape
    tr = min(T, 1024)

    def body(q_ref, g_ref, o_ref):
        xv = q_ref[...]
        r = lax.rsqrt(jnp.mean(xv * xv, axis=-1, keepdims=True) + RMS_EPS)
        o_ref[...] = ((xv * r) * g_ref[...]).astype(BF16)

    spec = pl.BlockSpec((tr, HEAD_DIM), lambda i, h: (i, h))
    return pl.pallas_call(
        body, name=name, grid=(T // tr, C // HEAD_DIM),
        in_specs=[spec, pl.BlockSpec((1, HEAD_DIM), lambda i, h: (0, 0))], out_specs=spec,
        out_shape=jax.ShapeDtypeStruct((T, C), BF16), compiler_params=_params("parallel", "parallel"),
    )(q, g.reshape(1, HEAD_DIM))


def _qknorm_bwd(q, g, dqn, name):
    T, C = q.shape
    tr = min(T, 1024)

    def body(q_ref, d_ref, g_ref, dq_ref, dg_ref):
        @pl.when((pl.program_id(0) == 0) & (pl.program_id(1) == 0))
        def _():
            dg_ref[...] = jnp.zeros_like(dg_ref)

        xv = q_ref[...]
        dv = d_ref[...]
        r = lax.rsqrt(jnp.mean(xv * xv, axis=-1, keepdims=True) + RMS_EPS)
        xn = xv * r
        dg_ref[...] += jnp.sum(dv * xn, axis=0, keepdims=True)
        dxn = dv * g_ref[...]
        dq_ref[...] = (r * (dxn - xn * jnp.mean(dxn * xn, axis=-1, keepdims=True))).astype(BF16)

    spec = pl.BlockSpec((tr, HEAD_DIM), lambda i, h: (i, h))
    gspec = pl.BlockSpec((1, HEAD_DIM), lambda i, h: (0, 0))
    return pl.pallas_call(
        body, name=name, grid=(T // tr, C // HEAD_DIM), in_specs=[spec, spec, gspec], out_specs=[spec, gspec],
        out_shape=[jax.ShapeDtypeStruct((T, C), BF16), jax.ShapeDtypeStruct((1, HEAD_DIM), F32)],
        compiler_params=_params("arbitrary", "arbitrary"),
    )(q, dqn, g.reshape(1, HEAD_DIM))


ATT_BLOCK = 128


def _logsig_pair(z):
    sp = jnp.log(1.0 + jnp.exp(-jnp.abs(z)))
    return jnp.minimum(z, 0.0) - sp, jnp.minimum(-z, 0.0) - sp


def _tri(n, strict_upper_src):
    j = lax.broadcasted_iota(jnp.int32, (n, n), 0)
    s = lax.broadcasted_iota(jnp.int32, (n, n), 1)
    if strict_upper_src == "gt":
        m = j > s
    elif strict_upper_src == "le":
        m = j <= s
    else:
        m = j < s
    return jnp.where(m, 1.0, 0.0).astype(BF16)


def _cumdot(x, tri):
    hi, lo = _split(x)
    return _dot(hi, tri) + _dot(lo, tri)


def _attn_fwd(qn, kn, v, Bl, S):
    T, C = qn.shape
    H = C // HEAD_DIM
    tb = min(ATT_BLOCK, S)
    nq = S // tb
    scale = 1.0 / math.sqrt(HEAD_DIM)

    def body(q_ref, k_ref, v_ref, o_ref, bt_ref):
        i = pl.program_id(2)
        qb = q_ref[...]
        tri = _tri(tb, "gt")
        rowpos = lax.broadcasted_iota(jnp.int32, (tb, tb), 0)
        colpos = lax.broadcasted_iota(jnp.int32, (tb, tb), 1)

        def step(n, carry):
            acc, srun = carry
            j = i - n
            rows = pl.ds(pl.multiple_of(j * tb, tb), tb)
            kj = k_ref[rows, :]
            vj = v_ref[rows, :]
            z = _dot(qb, kj, NT) * scale
            mask = (colpos + j * tb) < (rowpos + i * tb)
            a, b = _logsig_pair(z)
            b = jnp.where(mask, b, 0.0)
            suffix = _cumdot(b, tri)
            w = jnp.where(mask, jnp.exp(a + suffix + srun), 0.0)
            acc = acc + _dot(w.astype(BF16), vj)
            srun = srun + jnp.sum(b, axis=-1, keepdims=True)
            return acc, srun

        acc, srun = lax.fori_loop(0, i + 1, step, (jnp.zeros((tb, HEAD_DIM), F32), jnp.zeros((tb, 1), F32)))
        o_ref[...] = acc
        bt_ref[...] = srun

    qspec = pl.BlockSpec((tb, HEAD_DIM), lambda b, h, i: (b * nq + i, h))
    kspec = pl.BlockSpec((S, HEAD_DIM), lambda b, h, i: (b, h))
    btspec = pl.BlockSpec((None, None, tb, 1), lambda b, h, i: (b, h, i, 0))
    return pl.pallas_call(
        body, name="attn_fwd", grid=(Bl, H, nq), in_specs=[qspec, kspec, kspec], out_specs=[qspec, btspec],
        out_shape=[jax.ShapeDtypeStruct((T, C), F32), jax.ShapeDtypeStruct((Bl, H, S, 1), F32)],
        compiler_params=_params("parallel", "parallel", "arbitrary"),
    )(qn, kn, v)


def _attn_bwd(qn, kn, v, do, btot, Bl, S):
    T, C = qn.shape
    H = C // HEAD_DIM
    tb = min(ATT_BLOCK, S)
    nq = S // tb
    scale = 1.0 / math.sqrt(HEAD_DIM)

    def body(q_ref, k_ref, v_ref, do_ref, bt_ref, dq_ref, dk_ref, dv_ref):
        i = pl.program_id(2)

        @pl.when(i == 0)
        def _():
            dk_ref[...] = jnp.zeros_like(dk_ref)
            dv_ref[...] = jnp.zeros_like(dv_ref)

        qb = q_ref[...]
        dob = do_ref[...]
        bt = bt_ref[...]
        tri_le = _tri(tb, "le")
        tri_lt = _tri(tb, "lt")
        rowpos = lax.broadcasted_iota(jnp.int32, (tb, tb), 0)
        colpos = lax.broadcasted_iota(jnp.int32, (tb, tb), 1)

        def step(j, carry):
            dq, pb, pdl = carry
            rows = pl.ds(pl.multiple_of(j * tb, tb), tb)
            kj = k_ref[rows, :]
            vj = v_ref[rows, :]
            z = _dot(qb, kj, NT) * scale
            mask = (colpos + j * tb) < (rowpos + i * tb)
            a, b = _logsig_pair(z)
            b = jnp.where(mask, b, 0.0)
            suffix = bt - pb - _cumdot(b, tri_le)
            w = jnp.where(mask, jnp.exp(a + suffix), 0.0)
            dw = _dot(dob, vj, NT)
            dl = dw * w
            prefix = pdl + _cumdot(dl, tri_lt)
            beta = jnp.exp(a)
            dz = jnp.where(mask, dl * (1.0 - beta) - beta * prefix, 0.0) * scale
            dzb = dz.astype(BF16)
            dq = dq + _dot(dzb, kj)
            dk_ref[rows, :] += _dot(dzb, qb, TN)
            dv_ref[rows, :] += _dot(w.astype(BF16), dob, TN)
            pb = pb + jnp.sum(b, axis=-1, keepdims=True)
            pdl = pdl + jnp.sum(dl, axis=-1, keepdims=True)
            return dq, pb, pdl

        zero = jnp.zeros((tb, 1), F32)
        dq, _, _ = lax.fori_loop(0, i + 1, step, (jnp.zeros((tb, HEAD_DIM), F32), zero, zero))
        dq_ref[...] = dq

    qspec = pl.BlockSpec((tb, HEAD_DIM), lambda b, h, i: (b * nq + i, h))
    kspec = pl.BlockSpec((S, HEAD_DIM), lambda b, h, i: (b, h))
    btspec = pl.BlockSpec((None, None, tb, 1), lambda b, h, i: (b, h, i, 0))
    return pl.pallas_call(
        body, name="attn_bwd", grid=(Bl, H, nq), in_specs=[qspec, kspec, kspec, qspec, btspec],
        out_specs=[qspec, kspec, kspec],
        out_shape=[jax.ShapeDtypeStruct((T, C), F32)] * 3,
        compiler_params=_params("parallel", "parallel", "arbitrary"),
    )(qn, kn, v, do, btot)


SSM_TIME_BLOCK = 512
CHUNK = SUBLANES


def _cmadd(xr, xi, ar, ai, sr, si):
    return xr + ar * sr - ai * si, xi + ar * si + ai * sr


def _chunk_scan(xr, xi, tab_ref, cr, ci, reverse):
    for lvl, d in enumerate((1, 2, 4)):
        shift = (CHUNK - d) if reverse else d
        sr = pltpu.roll(xr, shift, 0)
        si = pltpu.roll(xi, shift, 0)
        ar = tab_ref[pl.ds((2 * lvl) * CHUNK, CHUNK), :]
        ai = tab_ref[pl.ds((2 * lvl + 1) * CHUNK, CHUNK), :]
        xr, xi = _cmadd(xr, xi, ar, ai, sr, si)
    pr = tab_ref[pl.ds(6 * CHUNK, CHUNK), :]
    pi = tab_ref[pl.ds(7 * CHUNK, CHUNK), :]
    return _cmadd(xr, xi, pr, pi, cr, ci)


def _ssm_dims(S, C):
    G = C // GROUP
    GT = min(16, G)
    return G, GT, G // GT, GT * GROUP, GT * STATE, min(SSM_TIME_BLOCK, S)


def _ssm_fwd(u, mats, Bl, S):
    T, C = u.shape
    G, GT, ngt, cw, sw, TB = _ssm_dims(S, C)
    ntb = S // TB
    nch = TB // CHUNK

    def body(u_ref, bre_ref, bim_ref, cre_ref, cim_ref, d_ref, tab_ref, y_ref, hr_ref, hi_ref, car_r, car_i):
        @pl.when(pl.program_id(2) == 0)
        def _():
            car_r[...] = jnp.zeros_like(car_r)
            car_i[...] = jnp.zeros_like(car_i)

        uv = u_ref[...]
        hr_ref[...] = _dot3(uv, bre_ref[...])
        hi_ref[...] = _dot3(uv, bim_ref[...])

        def step(n, carry):
            cr, ci = carry
            rows = pl.ds(pl.multiple_of(n * CHUNK, CHUNK), CHUNK)
            xr, xi = _chunk_scan(hr_ref[rows, :], hi_ref[rows, :], tab_ref, cr, ci, False)
            hr_ref[rows, :] = xr
            hi_ref[rows, :] = xi
            last = (CHUNK - 1, CHUNK)
            return (jnp.broadcast_to(xr[last[0]:last[1], :], xr.shape), jnp.broadcast_to(xi[last[0]:last[1], :], xi.shape))

        cr, ci = lax.fori_loop(0, nch, step, (car_r[...], car_i[...]))
        car_r[...] = cr
        car_i[...] = ci
        y_ref[...] = _dot3(hr_ref[...], cre_ref[...]) - _dot3(hi_ref[...], cim_ref[...]) + d_ref[...] * uv

    uspec = pl.BlockSpec((TB, cw), lambda g, b, t: (b * ntb + t, g))
    hspec = pl.BlockSpec((TB, sw), lambda g, b, t: (b * ntb + t, g))

    def gspec(r, c):
        return pl.BlockSpec((None, r, c), lambda g, b, t: (g, 0, 0))

    return pl.pallas_call(
        body, name="ssm_fwd", grid=(ngt, Bl, ntb),
        in_specs=[uspec, gspec(cw, sw), gspec(cw, sw), gspec(sw, cw), gspec(sw, cw), gspec(1, cw), gspec(8 * CHUNK, sw)],
        out_specs=[uspec, hspec, hspec],
        out_shape=[jax.ShapeDtypeStruct((T, C), F32), jax.ShapeDtypeStruct((T, G * STATE), F32),
                   jax.ShapeDtypeStruct((T, G * STATE), F32)],
        scratch_shapes=[pltpu.VMEM((CHUNK, sw), F32), pltpu.VMEM((CHUNK, sw), F32)],
        compiler_params=_params("parallel", "arbitrary", "arbitrary"),
    )(u, mats["bbd_re"], mats["bbd_im"], mats["cbd_re"], mats["cbd_im"], mats["d"], mats["tab_fwd"])


def _ssm_bwd(u, dy, h_re, h_im, mats, Bl, S):
    T, C = u.shape
    G, GT, ngt, cw, sw, TB = _ssm_dims(S, C)
    ntb = S // TB
    nch = TB // CHUNK
    rpb = TB // CHUNK

    def body(u_ref, dy_ref, hr_ref, hi_ref, hpr_ref, hpi_ref, ctr_ref, cti_ref, btr_ref, bti_ref, d_ref, tab_ref,
             du_ref, dbr_ref, dbi_ref, dcr_ref, dci_ref, dlr_ref, dli_ref, dd_ref, gr_ref, gi_ref, car_r, car_i):
        b = pl.program_id(1)
        t = pl.program_id(2)

        @pl.when((b == 0) & (t == 0))
        def _():
            for ref in (dbr_ref, dbi_ref, dcr_ref, dci_ref, dlr_ref, dli_ref, dd_ref):
                ref[...] = jnp.zeros_like(ref)

        @pl.when(t == 0)
        def _():
            car_r[...] = jnp.zeros_like(car_r)
            car_i[...] = jnp.zeros_like(car_i)

        uv = u_ref[...]
        dyv = dy_ref[...]
        gr_ref[...] = _dot3(dyv, ctr_ref[...])
        gi_ref[...] = -_dot3(dyv, cti_ref[...])
        alive = jnp.where(t == ntb - 1, 0.0, 1.0)
        row0 = lax.broadcasted_iota(jnp.int32, (CHUNK, sw), 0) == 0

        def step(m, carry):
            cr, ci, ar, ai = carry
            n = nch - 1 - m
            rows = pl.ds(pl.multiple_of(n * CHUNK, CHUNK), CHUNK)
            prow = pl.ds(pl.multiple_of(jnp.maximum(n - 1, 0) * CHUNK, CHUNK), CHUNK)
            xr, xi = _chunk_scan(gr_ref[rows, :], gi_ref[rows, :], tab_ref, cr, ci, True)
            gr_ref[rows, :] = xr
            gi_ref[rows, :] = xi
            first = n == 0
            pr = jnp.where(first, hpr_ref[...] * alive, hr_ref[prow, :])
            pi = jnp.where(first, hpi_ref[...] * alive, hi_ref[prow, :])
            sr = jnp.where(row0, pltpu.roll(pr, 1, 0), pltpu.roll(hr_ref[rows, :], 1, 0))
            si = jnp.where(row0, pltpu.roll(pi, 1, 0), pltpu.roll(hi_ref[rows, :], 1, 0))
            ar = ar + xr * sr + xi * si
            ai = ai + xi * sr - xr * si
            return (jnp.broadcast_to(xr[0:1, :], xr.shape), jnp.broadcast_to(xi[0:1, :], xi.shape), ar, ai)

        zero = jnp.zeros((CHUNK, sw), F32)
        cr, ci, ar, ai = lax.fori_loop(0, nch, step, (car_r[...], car_i[...], zero, zero))
        car_r[...] = cr
        car_i[...] = ci
        dlr_ref[...] += ar
        dli_ref[...] += ai
        gr = gr_ref[...]
        gi = gi_ref[...]
        dbr_ref[...] += _dot3(uv, gr, TN)
        dbi_ref[...] += _dot3(uv, gi, TN)
        dcr_ref[...] += _dot3(hr_ref[...], dyv, TN)
        dci_ref[...] -= _dot3(hi_ref[...], dyv, TN)
        dd_ref[...] += jnp.sum(dyv * uv, axis=0, keepdims=True)
        du_ref[...] = _dot3(gr, btr_ref[...]) + _dot3(gi, bti_ref[...]) + d_ref[...] * dyv

    def tblk(b, t):
        return b * ntb + (ntb - 1 - t)

    uspec = pl.BlockSpec((TB, cw), lambda g, b, t: (tblk(b, t), g))
    hspec = pl.BlockSpec((TB, sw), lambda g, b, t: (tblk(b, t), g))
    hpspec = pl.BlockSpec((CHUNK, sw), lambda g, b, t: (jnp.maximum(tblk(b, t) * rpb - 1, 0), g))

    def gspec(r, c):
        return pl.BlockSpec((None, r, c), lambda g, b, t: (g, 0, 0))

    def gshape(r, c):
        return jax.ShapeDtypeStruct((ngt, r, c), F32)

    return pl.pallas_call(
        body, name="ssm_bwd", grid=(ngt, Bl, ntb),
        in_specs=[uspec, uspec, hspec, hspec, hpspec, hpspec, gspec(cw, sw), gspec(cw, sw), gspec(sw, cw), gspec(sw, cw),
                  gspec(1, cw), gspec(8 * CHUNK, sw)],
        out_specs=[uspec, gspec(cw, sw), gspec(cw, sw), gspec(sw, cw), gspec(sw, cw), gspec(CHUNK, sw), gspec(CHUNK, sw),
                   gspec(1, cw)],
        out_shape=[jax.ShapeDtypeStruct((T, C), F32), gshape(cw, sw), gshape(cw, sw), gshape(sw, cw), gshape(sw, cw),
                   gshape(CHUNK, sw), gshape(CHUNK, sw), gshape(1, cw)],
        scratch_shapes=[pltpu.VMEM((TB, sw), F32), pltpu.VMEM((TB, sw), F32), pltpu.VMEM((CHUNK, sw), F32),
                        pltpu.VMEM((CHUNK, sw), F32)],
        compiler_params=_params("arbitrary", "arbitrary", "arbitrary"),
    )(u, dy, h_re, h_im, h_re, h_im, mats["cbdT_re"], mats["cbdT_im"], mats["bbdT_re"], mats["bbdT_im"], mats["d"],
      mats["tab_rev"])


def _zoh(a_re, a_im, log_dt, b_re, b_im):
    dt = jnp.exp(log_dt)[:, None]
    mag = jnp.exp(a_re * dt)
    l_re = mag * jnp.cos(a_im * dt)
    l_im = mag * jnp.sin(a_im * dt)
    den = a_re * a_re + a_im * a_im
    f_re = ((l_re - 1.0) * a_re + l_im * a_im) / den
    f_im = (l_im * a_re - (l_re - 1.0) * a_im) / den
    bb_re = f_re[..., None] * b_re - f_im[..., None] * b_im
    bb_im = f_re[..., None] * b_im + f_im[..., None] * b_re
    return l_re, l_im, bb_re, bb_im


def _ssm_matrices(a_re, a_im, log_dt, b_re, b_im, c_re, c_im, d, S):
    G = a_re.shape[0]
    _, GT, ngt, cw, sw, _ = _ssm_dims(S, G * GROUP)
    _, _, bb_re, bb_im = _zoh(a_re, a_im, log_dt, b_re, b_im)
    eye = jnp.eye(GT, dtype=F32)

    def bd_b(bb):
        return jnp.einsum("tgpi,gh->tgihp", bb.reshape(ngt, GT, STATE, GROUP), eye).reshape(ngt, cw, sw)

    def bd_c(c):
        return jnp.einsum("tgip,gh->tgphi", c.reshape(ngt, GT, GROUP, STATE), eye).reshape(ngt, sw, cw)

    dt = jnp.exp(log_dt)[:, None]

    def power(k, conj):
        mag = jnp.exp(k * a_re * dt)
        ang = k * a_im * dt
        return (mag * jnp.cos(ang)).reshape(ngt, 1, sw), ((-1.0 if conj else 1.0) * mag * jnp.sin(ang)).reshape(ngt, 1, sw)

    r = jnp.arange(CHUNK)[None, :, None]

    def table(reverse):
        parts = []
        for dd in (1, 2, 4):
            pr, pi = power(float(dd), reverse)
            keep = (r <= CHUNK - 1 - dd) if reverse else (r >= dd)
            parts += [jnp.where(keep, pr, 0.0), jnp.where(keep, pi, 0.0)]
        exps = [(CHUNK - k) if reverse else (k + 1) for k in range(CHUNK)]
        pw = [power(float(e), reverse) for e in exps]
        parts += [jnp.concatenate([p[0] for p in pw], axis=1), jnp.concatenate([p[1] for p in pw], axis=1)]
        return jnp.concatenate([jnp.broadcast_to(p, (ngt, CHUNK, sw)) for p in parts], axis=1)

    mats = dict(bbd_re=bd_b(bb_re), bbd_im=bd_b(bb_im), cbd_re=bd_c(c_re), cbd_im=bd_c(c_im),
                d=d.reshape(ngt, 1, cw), tab_fwd=table(False), tab_rev=table(True))
    for k in ("bbd_re", "bbd_im", "cbd_re", "cbd_im"):
        mats[k.replace("bd_", "bdT_")] = jnp.swapaxes(mats[k], 1, 2)
    return mats


def _ssm_unblock(dbr, dbi, dcr, dci, dlr, dli, dd, G):
    ngt = dbr.shape[0]
    GT = G // ngt
    eye = jnp.eye(GT, dtype=F32)

    def ub(x):
        return jnp.einsum("tgihp,gh->tgpi", x.reshape(ngt, GT, GROUP, GT, STATE), eye).reshape(G, STATE, GROUP)

    def uc(x):
        return jnp.einsum("tgphi,gh->tgip", x.reshape(ngt, GT, STATE, GT, GROUP), eye).reshape(G, GROUP, STATE)

    return (dlr.sum(axis=1).reshape(G, STATE), dli.sum(axis=1).reshape(G, STATE), ub(dbr), ub(dbi), uc(dcr), uc(dci),
            dd.reshape(G * GROUP))


def _attn_gate_fwd(o, gate):
    T, C = o.shape

    def body(o_ref, g_ref, out_ref):
        out_ref[...] = (o_ref[...] * _silu_parts(g_ref[...])[0]).astype(BF16)

    return _rows_call(body, "attn_gate_fwd", T, C, [o, gate], [], [BF16])[0]


def _attn_gate_bwd(dog, o, gate):
    T, C = o.shape

    def body(d_ref, o_ref, g_ref, do_ref, dg_ref):
        val, der = _silu_parts(g_ref[...])
        dv = d_ref[...]
        do_ref[...] = (dv * val).astype(BF16)
        dg_ref[...] = (dv * o_ref[...] * der).astype(BF16)

    return _rows_call(body, "attn_gate_bwd", T, C, [dog, o, gate], [], [BF16, BF16])


def _gelu_fwd(y):
    T, C = y.shape

    def body(y_ref, out_ref):
        out_ref[...] = _gelu_parts(y_ref[...])[0].astype(BF16)

    return _rows_call(body, "gelu_fwd", T, C, [y], [], [BF16])[0]


def _glu_fwd(y, gl, gate, glu_b):
    T, C = y.shape

    def body(y_ref, gl_ref, g_ref, b_ref, out_ref):
        yg = _gelu_parts(y_ref[...])[0]
        sg = _sigmoid(gl_ref[...] + b_ref[...])
        out_ref[...] = (yg * sg * _silu_parts(g_ref[...])[0]).astype(BF16)

    return _rows_call(body, "glu_fwd", T, C, [y, gl, gate], [glu_b.reshape(1, C)], [BF16])[0]


def _glu_bwd(dy3, y, gl, gate, glu_b):
    T, C = y.shape

    def body(d_ref, y_ref, gl_ref, g_ref, b_ref, dgl_ref, dgate_ref, t1_ref, db_ref):
        @pl.when(pl.program_id(0) == 0)
        def _():
            db_ref[...] = jnp.zeros_like(db_ref)

        yg = _gelu_parts(y_ref[...])[0]
        sg = _sigmoid(gl_ref[...] + b_ref[...])
        sl, sld = _silu_parts(g_ref[...])
        dv = d_ref[...]
        dy2 = dv * sl
        dgl = dy2 * yg * sg * (1.0 - sg)
        dgl_ref[...] = dgl.astype(BF16)
        dgate_ref[...] = (dv * (yg * sg) * sld).astype(BF16)
        t1_ref[...] = dy2 * sg
        db_ref[...] += jnp.sum(dgl, axis=0, keepdims=True)

    return _rows_call(body, "glu_bwd", T, C, [dy3, y, gl, gate], [glu_b.reshape(1, C)], [BF16, BF16, F32], [(1, C)])


def _gelu_bwd(t1, t2, y):
    T, C = y.shape

    def body(a_ref, b_ref, y_ref, out_ref):
        out_ref[...] = (a_ref[...] + b_ref[...]) * _gelu_parts(y_ref[...])[1]

    return _rows_call(body, "gelu_bwd", T, C, [t1, t2, y], [], [F32])[0]


def _loss_head(x2, target):
    T, C = x2.shape

    def body(x_ref, t_ref, d_ref, db_ref, l_ref):
        @pl.when(pl.program_id(0) == 0)
        def _():
            l_ref[...] = jnp.zeros_like(l_ref)

        e = x_ref[...] - t_ref[...]
        d = e * (1.0 / C)
        d_ref[...] = d
        db_ref[...] = d.astype(BF16)
        l_ref[...] += 0.5 * jnp.sum(jnp.sum(e * e, axis=-1, keepdims=True) * (1.0 / C), axis=0, keepdims=True)

    return _rows_call(body, "loss_head", T, C, [x2, target], [], [F32, BF16], [(1, 1)])


def _adamw(w, g, m, v, name):
    shape = w.shape
    C = shape[-1]
    R = w.size // C
    bc1 = 1.0 - ADAM_B1 ** ADAM_STEP
    bc2 = 1.0 - ADAM_B2 ** ADAM_STEP

    def body(w_ref, g_ref, m_ref, v_ref, d_ref, nm_ref, nv_ref):
        gv = g_ref[...]
        mn = ADAM_B1 * m_ref[...] + (1.0 - ADAM_B1) * gv
        vn = ADAM_B2 * v_ref[...] + (1.0 - ADAM_B2) * (gv * gv)
        d_ref[...] = -ADAM_LR * ((mn / bc1) / (jnp.sqrt(vn / bc2) + ADAM_EPS) + ADAM_WD * w_ref[...])
        nm_ref[...] = mn
        nv_ref[...] = vn

    outs = _rows_call(body, name, R, C, [a.reshape(R, C) for a in (w, g, m, v)], [], [F32, F32, F32])
    return [o.reshape(shape) for o in outs]


def _sum_leading(x, name):
    n, R, C = x.shape
    tr = _row_tile(R, C * n)

    def body(x_ref, o_ref):
        acc = x_ref[0]
        for k in range(1, n):
            acc = acc + x_ref[k]
        o_ref[...] = acc

    return pl.pallas_call(
        body, name=name, grid=(R // tr,), in_specs=[pl.BlockSpec((n, tr, C), lambda i: (0, i, 0))],
        out_specs=pl.BlockSpec((tr, C), lambda i: (i, 0)), out_shape=jax.ShapeDtypeStruct((R, C), F32),
        compiler_params=_params("parallel"),
    )(x)


def _add_halves(g, c, name):
    full, recv = g
    n, R, C = full.shape
    half = R // 2
    tr = _row_tile(half, C)
    nb = half // tr

    def body(c_ref, a_ref, b_ref, o_ref):
        o_ref[...] = a_ref[...] + b_ref[...]

    grid_spec = pltpu.PrefetchScalarGridSpec(
        num_scalar_prefetch=1, grid=(n, nb),
        in_specs=[pl.BlockSpec((None, tr, C), lambda j, i, c_ref: (j, c_ref[0] * nb + i, 0)),
                  pl.BlockSpec((None, tr, C), lambda j, i, c_ref: (j, i, 0))],
        out_specs=pl.BlockSpec((None, tr, C), lambda j, i, c_ref: (j, i, 0)))
    return pl.pallas_call(
        body, name=name, grid_spec=grid_spec, out_shape=jax.ShapeDtypeStruct((n, half, C), F32),
        compiler_params=_params("parallel", "parallel"),
    )(c.reshape(1).astype(jnp.int32), full, recv)


ANY = pl.BlockSpec(memory_space=pl.ANY)


def _position():
    return lax.axis_index("x"), lax.axis_index("y"), lax.axis_index("c")


def _all_gather8(blk, name):
    M, N = blk.shape

    def body(x_ref, out_ref, send_sems, recv_sems, local_sem):
        x, y, c = _position()
        me, sibling = (x, y, c), (x, y, 1 - c)
        chips = [(1 - x, y), (x, 1 - y), (1 - x, 1 - y)]

        def slab(px, py, pc):
            return out_ref.at[4 * px + 2 * py + pc]

        def copy(k, block, to, src=None):
            return pltpu.make_async_remote_copy(
                src_ref=slab(*block) if src is None else src, dst_ref=slab(*block),
                send_sem=send_sems.at[k], recv_sem=recv_sems.at[k], device_id=to, device_id_type=MESH)

        mine = pltpu.make_async_copy(x_ref, slab(*me), local_sem)
        mine.start()
        first = [copy(0, me, sibling, src=x_ref)]
        first += [copy(1 + j, me, (*chip, c), src=x_ref) for j, chip in enumerate(chips)]
        for cp in first:
            cp.start()
        passed = [copy(4 + j, (*chip, c), sibling) for j, chip in enumerate(chips)]
        for j, chip in enumerate(chips):
            copy(1 + j, (*chip, c), me).wait_recv()
            passed[j].start()
        copy(0, sibling, me).wait_recv()
        for j, chip in enumerate(chips):
            copy(4 + j, (*chip, 1 - c), me).wait_recv()
        for cp in first + passed:
            cp.wait_send()
        mine.wait()

    return pl.pallas_call(
        body, name=name, in_specs=[ANY], out_specs=ANY, out_shape=jax.ShapeDtypeStruct((N_DEV, M, N), blk.dtype),
        scratch_shapes=[pltpu.SemaphoreType.DMA((7,)), pltpu.SemaphoreType.DMA((7,)), pltpu.SemaphoreType.DMA],
    )(blk)


def _sibling_send_half(g, name):
    n, R, C = g.shape
    half = R // 2

    def body(g_ref, out_ref, send_sem, recv_sem):
        x, y, c = _position()
        src = g_ref.at[:, pl.ds((1 - c) * half, half), :]
        cp = pltpu.make_async_remote_copy(src_ref=src, dst_ref=out_ref, send_sem=send_sem, recv_sem=recv_sem,
                                          device_id=(x, y, 1 - c), device_id_type=MESH)
        cp.start()
        cp.wait()

    return pl.pallas_call(
        body, name=name, in_specs=[ANY], out_specs=ANY, out_shape=jax.ShapeDtypeStruct((n, half, C), g.dtype),
        scratch_shapes=[pltpu.SemaphoreType.DMA, pltpu.SemaphoreType.DMA],
    )(g)


def _chip_exchange(p, name):
    n, M, C = p.shape

    def body(p_ref, out_ref, send_sems, recv_sems, local_sem):
        x, y, c = _position()
        my = 2 * x + y
        chips = [(1 - x, y), (x, 1 - y), (1 - x, 1 - y)]
        mine = pltpu.make_async_copy(p_ref.at[my], out_ref.at[my], local_sem)
        mine.start()
        copies = []
        for k, (px, py) in enumerate(chips):
            cp = pltpu.make_async_remote_copy(
                src_ref=p_ref.at[2 * px + py], dst_ref=out_ref.at[my], send_sem=send_sems.at[k], recv_sem=recv_sems.at[k],
                device_id=(px, py, c), device_id_type=MESH)
            cp.start()
            copies.append(cp)
        for k, (px, py) in enumerate(chips):
            pltpu.make_async_remote_copy(
                src_ref=p_ref.at[my], dst_ref=out_ref.at[2 * px + py], send_sem=send_sems.at[k], recv_sem=recv_sems.at[k],
                device_id=(px, py, c), device_id_type=MESH).wait_recv()
        for cp in copies:
            cp.wait_send()
        mine.wait()

    return pl.pallas_call(
        body, name=name, in_specs=[ANY], out_specs=ANY, out_shape=jax.ShapeDtypeStruct((n, M, C), p.dtype),
        scratch_shapes=[pltpu.SemaphoreType.DMA((3,)), pltpu.SemaphoreType.DMA((3,)), pltpu.SemaphoreType.DMA],
    )(p)


def _sibling_join(f, name):
    M, C = f.shape

    def body(f_ref, out_ref, send_sem, recv_sem, local_sem):
        x, y, c = _position()
        mine = pltpu.make_async_copy(f_ref, out_ref.at[pl.ds(c * M, M), :], local_sem)
        mine.start()
        cp = pltpu.make_async_remote_copy(src_ref=f_ref, dst_ref=out_ref.at[pl.ds(c * M, M), :], send_sem=send_sem,
                                          recv_sem=recv_sem, device_id=(x, y, 1 - c), device_id_type=MESH)
        cp.start()
        pltpu.make_async_remote_copy(src_ref=f_ref, dst_ref=out_ref.at[pl.ds((1 - c) * M, M), :], send_sem=send_sem,
                                     recv_sem=recv_sem, device_id=(x, y, 1 - c), device_id_type=MESH).wait_recv()
        cp.wait_send()
        mine.wait()

    return pl.pallas_call(
        body, name=name, in_specs=[ANY], out_specs=ANY, out_shape=jax.ShapeDtypeStruct((2 * M, C), f.dtype),
        scratch_shapes=[pltpu.SemaphoreType.DMA, pltpu.SemaphoreType.DMA, pltpu.SemaphoreType.DMA],
    )(f)


def _pack_dense(attn_in, attn_out, ssm_in, glu_w, ssm_out):
    D = attn_in.shape[0]
    return jnp.concatenate([attn_in, attn_out, ssm_in.reshape(D // 2, D), glu_w, ssm_out], axis=0)


def _unpack_dense(p):
    D = p.shape[-1]
    q = D // 4
    o = [0, D, D + q, D + q + D // 2, D + 2 * q + D // 2, D + 3 * q + D // 2]
    lead = p.shape[:-2]
    return (p[..., o[0]:o[1], :], p[..., o[1]:o[2], :], p[..., o[2]:o[3], :].reshape(*lead, D, D // 2),
            p[..., o[3]:o[4], :], p[..., o[4]:o[5], :])


SMALL_REPL = ("norm_g", "attn_q_g", "attn_k_g")


def _pack_small(parts):
    flat = jnp.concatenate([p.reshape(-1) for p in parts])
    pad = (-flat.size) % (2 * SUBLANES * LANES)
    return jnp.pad(flat, (0, pad)).reshape(-1, LANES)


def _unpack_small(buf, shapes):
    flat = buf.reshape(-1)
    out, off = [], 0
    for s in shapes:
        n = math.prod(s)
        out.append(flat[off:off + n].reshape(s))
        off += n
    return out


def _local_step(x, target, norm_g, q_g, k_g, w_attn_in, w_attn_out, w_ssm_in, w_glu, w_ssm_out, ssm_small):
    Bl, S, D = x.shape
    T = Bl * S
    x0 = x.reshape(T, D)
    tgt = target.reshape(T, D)
    a_re, a_im, log_dt, b_re, b_im, c_re, c_im, d_skip, glu_b = ssm_small
    G = a_re.shape[0]
    mats = _ssm_matrices(a_re, a_im, log_dt, b_re, b_im, c_re, c_im, d_skip, S)

    h0 = _rmsnorm_fwd(x0, norm_g[0], "norm0_fwd")
    q, k, v, gate = [_matmul(h0, w_attn_in[j], name=f"attn_in_{j}", out_dtype=(BF16 if j == 2 else F32)) for j in range(4)]
    qn = _qknorm_fwd(q, q_g, "qnorm_fwd")
    kn = _qknorm_fwd(k, k_g, "knorm_fwd")
    o, btot = _attn_fwd(qn, kn, v, Bl, S)
    og = _attn_gate_fwd(o, gate)
    x1 = _matmul(og, w_attn_out, name="attn_out", residual=x0)

    h1 = _rmsnorm_fwd(x1, norm_g[1], "norm1_fwd")
    u = _matmul(h1, w_ssm_in[:, :D], name="ssm_in_u")
    gate2 = _matmul(h1, w_ssm_in[:, D:], name="ssm_in_gate")
    y, hs_re, hs_im = _ssm_fwd(u, mats, Bl, S)
    yg = _gelu_fwd(y)
    gl = _matmul(yg, w_glu, name="glu_mm")
    y3 = _glu_fwd(y, gl, gate2, glu_b)
    x2 = _matmul(y3, w_ssm_out, name="ssm_out", residual=x1)

    dx2, dx2b, loss = _loss_head(x2, tgt)

    dy3 = _matmul(dx2b, w_ssm_out, name="ssm_out_dgrad", tb=True)
    g_ssm_out = _matmul(y3, dx2b, name="ssm_out_wgrad", ta=True)
    dgl, dgate2, t1, dglu_b = _glu_bwd(dy3, y, gl, gate2, glu_b)
    t2 = _matmul(dgl, w_glu, name="glu_dgrad", tb=True)
    g_glu = _matmul(yg, dgl, name="glu_wgrad", ta=True)
    dy = _gelu_bwd(t1, t2, y)
    du, dbr, dbi, dcr, dci, dlr, dli, dd = _ssm_bwd(u, dy, hs_re, hs_im, mats, Bl, S)
    dh1 = _matmul(du, w_ssm_in[:, :D], name="ssm_in_dgrad_u", tb=True)
    dh1 = _matmul(dgate2, w_ssm_in[:, D:], name="ssm_in_dgrad_gate", tb=True, residual=dh1)
    g_ssm_in = jnp.concatenate([_matmul(h1, du, name="ssm_in_wgrad_u", ta=True),
                                _matmul(h1, dgate2, name="ssm_in_wgrad_gate", ta=True)], axis=1)
    dx1, dx1b, dng1 = _rmsnorm_bwd(x1, norm_g[1], dh1, dx2, "norm1_bwd")

    dog = _matmul(dx1b, w_attn_out, name="attn_out_dgrad", tb=True)
    g_attn_out = _matmul(og, dx1b, name="attn_out_wgrad", ta=True)
    do, dgate = _attn_gate_bwd(dog, o, gate)
    dqn, dkn, dv = _attn_bwd(qn, kn, v, do, btot, Bl, S)
    dq, dqg = _qknorm_bwd(q, q_g, dqn, "qnorm_bwd")
    dk, dkg = _qknorm_bwd(k, k_g, dkn, "knorm_bwd")
    dproj = [dq, dk, dv, dgate]
    dh0 = None
    for j in range(4):
        dh0 = _matmul(dproj[j], w_attn_in[j], name=f"attn_in_dgrad_{j}", tb=True, residual=dh0)
    g_attn_in = jnp.stack([_matmul(h0, dproj[j], name=f"attn_in_wgrad_{j}", ta=True) for j in range(4)])
    dx0, _, dng0 = _rmsnorm_bwd(x0, norm_g[0], dh0, dx1, "norm0_bwd")

    dense = (g_attn_in, g_attn_out, g_ssm_in, g_glu, g_ssm_out)
    small = (jnp.concatenate([dng0, dng1], axis=0), dqg, dkg) + _ssm_unblock(dbr, dbi, dcr, dci, dlr, dli, dd, G) + (
        dglu_b.reshape(D),)
    return loss, dx0.reshape(Bl, S, D), dense, small


def _chip_rows(a, chip, n_per):
    return lax.dynamic_slice_in_dim(a, chip * n_per, n_per, axis=0)


def kernel(x, norm_g, attn_w_in, attn_q_g, attn_k_g, attn_w_out, ssm_w_in, ssm_A_re, ssm_A_im, ssm_log_dt, ssm_B_re, ssm_B_im, ssm_C_re, ssm_C_im, ssm_D, ssm_glu_w, ssm_glu_b, ssm_w_out, loss_target, m_norm_g, m_attn_w_in, m_attn_q_g, m_attn_k_g, m_attn_w_out, m_ssm_w_in, m_ssm_A_re, m_ssm_A_im, m_ssm_log_dt, m_ssm_B_re, m_ssm_B_im, m_ssm_C_re, m_ssm_C_im, m_ssm_D, m_ssm_glu_w, m_ssm_glu_b, m_ssm_w_out, v_norm_g, v_attn_w_in, v_attn_q_g, v_attn_k_g, v_attn_w_out, v_ssm_w_in, v_ssm_A_re, v_ssm_A_im, v_ssm_log_dt, v_ssm_B_re, v_ssm_B_im, v_ssm_C_re, v_ssm_C_im, v_ssm_D, v_ssm_glu_w, v_ssm_glu_b, v_ssm_w_out):
    D = x.shape[-1]
    cx, cy, cc = _position()
    chip = 2 * cx + cy
    G = D // GROUP
    Gl = G // N_CHIPS

    dense_local = _pack_dense(attn_w_in[0], attn_w_out[0], ssm_w_in[0], ssm_glu_w[0], ssm_w_out[0]).astype(BF16)
    R = dense_local.shape[0]
    my_half = lax.dynamic_slice_in_dim(dense_local, cc * (R // 2), R // 2, axis=0)
    dense_all = _all_gather8(my_half, "weights_all_gather").reshape(N_CHIPS, R, D)
    wa_in, wa_out, ws_in, w_glu, ws_out = _unpack_dense(dense_all)
    wa_out = wa_out.reshape(D, D)
    ws_in = jnp.swapaxes(ws_in, 0, 1).reshape(D, 2 * D)
    w_glu = w_glu.reshape(D, D)
    ws_out = ws_out.reshape(D, D)

    ssm_local = [ssm_A_re[0], ssm_A_im[0], ssm_log_dt[0], ssm_B_re[0], ssm_B_im[0], ssm_C_re[0], ssm_C_im[0], ssm_D[0],
                 ssm_glu_b[0]]
    small_local = _pack_small(ssm_local)
    half_rows = small_local.shape[0] // 2
    small_half = lax.dynamic_slice_in_dim(small_local, cc * half_rows, half_rows, axis=0)
    small_all = _all_gather8(small_half, "ssm_params_all_gather").reshape(N_CHIPS, 2 * half_rows, LANES)
    per_chip = [_unpack_small(small_all[j], [p.shape for p in ssm_local]) for j in range(N_CHIPS)]
    ssm_full = [jnp.concatenate([per_chip[j][i] for j in range(N_CHIPS)], axis=0) for i in range(len(ssm_local))]

    loss, grad_x, dense, small = _local_step(x, loss_target, norm_g, attn_q_g[0], attn_k_g[0], wa_in, wa_out, ws_in,
                                              w_glu, ws_out, ssm_full)
    loss = lax.psum(loss[0, 0], ("x", "y", "c"))

    g_attn_in, g_attn_out, g_ssm_in, g_glu, g_ssm_out = dense
    q4 = D // N_CHIPS
    packed = jnp.stack([
        _pack_dense(g_attn_in[j], g_attn_out[j * q4:(j + 1) * q4], g_ssm_in[:, j * (D // 2):(j + 1) * (D // 2)],
                    g_glu[j * q4:(j + 1) * q4], g_ssm_out[j * q4:(j + 1) * q4]) for j in range(N_CHIPS)])
    from_sibling = _sibling_send_half(packed, "grads_sibling_half")
    chip_sum = _add_halves((packed, from_sibling), cc, "grads_add_halves")
    gathered = _chip_exchange(chip_sum, "grads_chip_exchange")
    my_sum = _sum_leading(gathered, "grads_sum_chips")
    dense_grad = _sibling_join(my_sum, "grads_sibling_join")
    gd = _unpack_dense(dense_grad)

    small_shapes = [s.shape for s in small]
    small_sum = _sum_leading(_all_gather8(_pack_small(small), "small_grads_all_gather"), "small_grads_sum")
    (dng, dqg, dkg, dl_re, dl_im, dbb_re, dbb_im, dc_re, dc_im, dd_skip, dglu_b) = _unpack_small(small_sum, small_shapes)
    a_re, a_im, log_dt, b_re, b_im = ssm_local[:5]
    _, zoh_vjp = jax.vjp(_zoh, a_re, a_im, log_dt, b_re, b_im)
    da_re, da_im, dlog_dt, db_re, db_im = zoh_vjp((_chip_rows(dl_re, chip, Gl), _chip_rows(dl_im, chip, Gl),
                                                   _chip_rows(dbb_re, chip, Gl), _chip_rows(dbb_im, chip, Gl)))
    grads = {
        "norm_g": dng, "attn_w_in": gd[0][None], "attn_q_g": dqg, "attn_k_g": dkg, "attn_w_out": gd[1][None],
        "ssm_w_in": gd[2][None], "ssm_A_re": da_re[None], "ssm_A_im": da_im[None], "ssm_log_dt": dlog_dt[None],
        "ssm_B_re": db_re[None], "ssm_B_im": db_im[None], "ssm_C_re": _chip_rows(dc_re, chip, Gl)[None],
        "ssm_C_im": _chip_rows(dc_im, chip, Gl)[None], "ssm_D": _chip_rows(dd_skip, chip, q4)[None],
        "ssm_glu_w": gd[3][None], "ssm_glu_b": _chip_rows(dglu_b, chip, q4)[None], "ssm_w_out": gd[4][None],
    }
    weights = dict(norm_g=norm_g, attn_w_in=attn_w_in, attn_q_g=attn_q_g, attn_k_g=attn_k_g, attn_w_out=attn_w_out,
                   ssm_w_in=ssm_w_in, ssm_A_re=ssm_A_re, ssm_A_im=ssm_A_im, ssm_log_dt=ssm_log_dt, ssm_B_re=ssm_B_re,
                   ssm_B_im=ssm_B_im, ssm_C_re=ssm_C_re, ssm_C_im=ssm_C_im, ssm_D=ssm_D, ssm_glu_w=ssm_glu_w,
                   ssm_glu_b=ssm_glu_b, ssm_w_out=ssm_w_out)
    m = dict(norm_g=m_norm_g, attn_w_in=m_attn_w_in, attn_q_g=m_attn_q_g, attn_k_g=m_attn_k_g, attn_w_out=m_attn_w_out,
             ssm_w_in=m_ssm_w_in, ssm_A_re=m_ssm_A_re, ssm_A_im=m_ssm_A_im, ssm_log_dt=m_ssm_log_dt, ssm_B_re=m_ssm_B_re,
             ssm_B_im=m_ssm_B_im, ssm_C_re=m_ssm_C_re, ssm_C_im=m_ssm_C_im, ssm_D=m_ssm_D, ssm_glu_w=m_ssm_glu_w,
             ssm_glu_b=m_ssm_glu_b, ssm_w_out=m_ssm_w_out)
    v = dict(norm_g=v_norm_g, attn_w_in=v_attn_w_in, attn_q_g=v_attn_q_g, attn_k_g=v_attn_k_g, attn_w_out=v_attn_w_out,
             ssm_w_in=v_ssm_w_in, ssm_A_re=v_ssm_A_re, ssm_A_im=v_ssm_A_im, ssm_log_dt=v_ssm_log_dt, ssm_B_re=v_ssm_B_re,
             ssm_B_im=v_ssm_B_im, ssm_C_re=v_ssm_C_re, ssm_C_im=v_ssm_C_im, ssm_D=v_ssm_D, ssm_glu_w=v_ssm_glu_w,
             ssm_glu_b=v_ssm_glu_b, ssm_w_out=v_ssm_w_out)
    names = list(weights)
    dense_names = ("attn_w_in", "attn_w_out", "ssm_w_in", "ssm_glu_w", "ssm_w_out")
    delta, new_m, new_v = {}, {}, {}
    for n in dense_names:
        delta[n], new_m[n], new_v[n] = _adamw(weights[n], grads[n], m[n], v[n], "adamw_" + n)
    small_names = [n for n in names if n not in dense_names]
    small_shapes = [weights[n].shape for n in small_names]
    packs = [_pack_small([d[n] for n in small_names]) for d in (weights, grads, m, v)]
    outs = _adamw(*packs, "adamw_small")
    for res, out in zip((delta, new_m, new_v), outs):
        for n, val in zip(small_names, _unpack_small(out, small_shapes)):
            res[n] = val
    return (loss, grad_x, *[grads[n] for n in names], *[delta[n] for n in names], *[new_m[n] for n in names],
            *[new_v[n] for n in names])
```

```python
import functools
import math

import jax
import jax.numpy as jnp
from jax import lax
from jax.experimental import pallas as pl
from jax.experimental.pallas import tpu as pltpu

F32 = jnp.float32
BF16 = jnp.bfloat16

HEAD_DIM = 128
GROUP = 16
STATE = 64
RMS_EPS = 1e-6
ADAM_LR = 0.001
ADAM_B1 = 0.9
ADAM_B2 = 0.999
ADAM_EPS = 1e-08
ADAM_WD = 0.01
ADAM_STEP = 10

N_CHIPS = 4
N_DEV = 8
SUBLANES = 8
LANES = 128
VMEM_LIMIT = 56 * 1024 * 1024
ROW_BLOCK_ELEMS = 1 << 18
MESH = pl.DeviceIdType.MESH

NN = (((1,), (0,)), ((), ()))
NT = (((1,), (1,)), ((), ()))
TN = (((0,), (0,)), ((), ()))


def _params(*sem):
    return pltpu.CompilerParams(dimension_semantics=sem, vmem_limit_bytes=VMEM_LIMIT)


def _dot(a, b, dims=NN):
    return lax.dot_general(a, b, dims, preferred_element_type=F32)


def _split(a):
    hi = a.astype(BF16)
    lo = (a - hi.astype(F32)).astype(BF16)
    return hi, lo


def _dot3(a, b, dims=NN):
    a1, a2 = _split(a)
    b1, b2 = _split(b)
    return _dot(a1, b1, dims) + _dot(a1, b2, dims) + _dot(a2, b1, dims)


def _sigmoid(x):
    return 1.0 / (1.0 + jnp.exp(-x))


def _silu_parts(x):
    s = _sigmoid(x)
    return x * s, s * (1.0 + x * (1.0 - s))


_GELU_C = math.sqrt(2.0 / math.pi)


def _gelu_parts(x):
    x2 = x * x
    t = jnp.tanh(_GELU_C * (x + 0.044715 * x * x2))
    val = 0.5 * x * (1.0 + t)
    der = 0.5 * (1.0 + t) + 0.5 * x * (1.0 - t * t) * _GELU_C * (1.0 + 3.0 * 0.044715 * x2)
    return val, der


def _matmul(a, b, *, name, ta=False, tb=False, residual=None, out_dtype=F32):
    if ta:
        K, M = a.shape
    else:
        M, K = a.shape
    N = b.shape[0] if tb else b.shape[1]
    tm, tn, tk = min(M, 1024), min(N, 1024), min(K, 1024)
    nk = K // tk
    dims = ((((0,) if ta else (1,)), ((1,) if tb else (0,))), ((), ()))

    def body(*refs):
        if residual is None:
            a_ref, b_ref, o_ref, acc = refs
        else:
            a_ref, b_ref, r_ref, o_ref, acc = refs
        k = pl.program_id(2)

        @pl.when(k == 0)
        def _():
            acc[...] = jnp.zeros_like(acc)

        acc[...] += _dot(a_ref[...].astype(BF16), b_ref[...].astype(BF16), dims)

        @pl.when(k == nk - 1)
        def _():
            r = acc[...]
            if residual is not None:
                r = r + r_ref[...].astype(F32)
            o_ref[...] = r.astype(out_dtype)

    a_spec = pl.BlockSpec((tk, tm), lambda i, j, k: (k, i)) if ta else pl.BlockSpec((tm, tk), lambda i, j, k: (i, k))
    b_spec = pl.BlockSpec((tn, tk), lambda i, j, k: (j, k)) if tb else pl.BlockSpec((tk, tn), lambda i, j, k: (k, j))
    in_specs = [a_spec, b_spec]
    args = [a, b]
    if residual is not None:
        in_specs.append(pl.BlockSpec((tm, tn), lambda i, j, k: (i, j)))
        args.append(residual)
    return pl.pallas_call(
        body, name=name, grid=(M // tm, N // tn, nk), in_specs=in_specs,
        out_specs=pl.BlockSpec((tm, tn), lambda i, j, k: (i, j)),
        out_shape=jax.ShapeDtypeStruct((M, N), out_dtype),
        scratch_shapes=[pltpu.VMEM((tm, tn), F32)],
        compiler_params=_params("parallel", "parallel", "arbitrary"),
    )(*args)


def _row_tile(T, C):
    tr = max(SUBLANES, min(T, ROW_BLOCK_ELEMS // C) // SUBLANES * SUBLANES)
    while T % tr:
        tr -= SUBLANES
    return tr


def _rows_call(body, name, T, C, row_ins, full_ins, row_outs, acc_outs=()):
    tr = _row_tile(T, C)
    row_spec = pl.BlockSpec((tr, C), lambda i: (i, 0))
    in_specs = [row_spec] * len(row_ins) + [pl.BlockSpec(f.shape, lambda i, n=f.ndim: (0,) * n) for f in full_ins]
    out_specs = [row_spec] * len(row_outs) + [pl.BlockSpec(s, lambda i, n=len(s): (0,) * n) for s in acc_outs]
    out_shape = [jax.ShapeDtypeStruct((T, C), d) for d in row_outs] + [jax.ShapeDtypeStruct(s, F32) for s in acc_outs]
    return pl.pallas_call(
        body, name=name, grid=(T // tr,), in_specs=in_specs, out_specs=out_specs, out_shape=out_shape,
        compiler_params=_params("arbitrary" if acc_outs else "parallel"),
    )(*row_ins, *full_ins)


def _rmsnorm_fwd(x, g, name):
    T, C = x.shape

    def body(x_ref, g_ref, h_ref):
        xv = x_ref[...]
        r = lax.rsqrt(jnp.mean(xv * xv, axis=-1, keepdims=True) + RMS_EPS)
        h_ref[...] = ((xv * r) * g_ref[...]).astype(BF16)

    return _rows_call(body, name, T, C, [x], [g.reshape(1, C)], [BF16])[0]


def _rmsnorm_bwd(x, g, dh, dres, name):
    T, C = x.shape

    def body(x_ref, dh_ref, dres_ref, g_ref, dx_ref, dxb_ref, dg_ref):
        @pl.when(pl.program_id(0) == 0)
        def _():
            dg_ref[...] = jnp.zeros_like(dg_ref)

        xv = x_ref[...]
        dhv = dh_ref[...]
        r = lax.rsqrt(jnp.mean(xv * xv, axis=-1, keepdims=True) + RMS_EPS)
        xn = xv * r
        dg_ref[...] += jnp.sum(dhv * xn, axis=0, keepdims=True)
        dxn = dhv * g_ref[...]
        dx = dres_ref[...] + r * (dxn - xn * jnp.mean(dxn * xn, axis=-1, keepdims=True))
        dx_ref[...] = dx
        dxb_ref[...] = dx.astype(BF16)

    return _rows_call(body, name, T, C, [x, dh, dres], [g.reshape(1, C)], [F32, BF16], [(1, C)])


def _qknorm_fwd(q, g, name):
    T, C = q.shape
    tr = min(T, 1024)

    def body(q_ref, g_ref, o_ref):
        xv = q_ref[...]
        r = lax.rsqrt(jnp.mean(xv * xv, axis=-1, keepdims=True) + RMS_EPS)
        o_ref[...] = ((xv * r) * g_ref[...]).astype(BF16)

    spec = pl.BlockSpec((tr, HEAD_DIM), lambda i, h: (i, h))
    return pl.pallas_call(
        body, name=name, grid=(T // tr, C // HEAD_DIM),
        in_specs=[spec, pl.BlockSpec((1, HEAD_DIM), lambda i, h: (0, 0))], out_specs=spec,
        out_shape=jax.ShapeDtypeStruct((T, C), BF16), compiler_params=_params("parallel", "parallel"),
    )(q, g.reshape(1, HEAD_DIM))


def _qknorm_bwd(q, g, dqn, name):
    T, C = q.shape
    tr = min(T, 1024)

    def body(q_ref, d_ref, g_ref, dq_ref, dg_ref):
        @pl.when((pl.program_id(0) == 0) & (pl.program_id(1) == 0))
        def _():
            dg_ref[...] = jnp.zeros_like(dg_ref)

        xv = q_ref[...]
        dv = d_ref[...]
        r = lax.rsqrt(jnp.mean(xv * xv, axis=-1, keepdims=True) + RMS_EPS)
        xn = xv * r
        dg_ref[...] += jnp.sum(dv * xn, axis=0, keepdims=True)
        dxn = dv * g_ref[...]
        dq_ref[...] = (r * (dxn - xn * jnp.mean(dxn * xn, axis=-1, keepdims=True))).astype(BF16)

    spec = pl.BlockSpec((tr, HEAD_DIM), lambda i, h: (i, h))
    gspec = pl.BlockSpec((1, HEAD_DIM), lambda i, h: (0, 0))
    return pl.pallas_call(
        body, name=name, grid=(T // tr, C // HEAD_DIM), in_specs=[spec, spec, gspec], out_specs=[spec, gspec],
        out_shape=[jax.ShapeDtypeStruct((T, C), BF16), jax.ShapeDtypeStruct((1, HEAD_DIM), F32)],
        compiler_params=_params("arbitrary", "arbitrary"),
    )(q, dqn, g.reshape(1, HEAD_DIM))


ATT_TQ = 512
ATT_TK = 256


def _logsig_pair(z):
    sp = jnp.log(1.0 + jnp.exp(-jnp.abs(z)))
    return jnp.minimum(z, 0.0) - sp, jnp.minimum(-z, 0.0) - sp


def _tri(n, strict_upper_src):
    j = lax.broadcasted_iota(jnp.int32, (n, n), 0)
    s = lax.broadcasted_iota(jnp.int32, (n, n), 1)
    if strict_upper_src == "gt":
        m = j > s
    elif strict_upper_src == "le":
        m = j <= s
    else:
        m = j < s
    return jnp.where(m, 1.0, 0.0).astype(BF16)


def _cumdot(x, tri):
    hi, lo = _split(x)
    return _dot(hi, tri) + _dot(lo, tri)


def _attn_tiles(S):
    tq, tk = min(ATT_TQ, S), min(ATT_TK, S)
    return tq, tk, S // tq, tq // tk


def _attn_fwd(qn, kn, v, Bl, S):
    T, C = qn.shape
    H = C // HEAD_DIM
    tq, tk, nq, kpq = _attn_tiles(S)
    scale = 1.0 / math.sqrt(HEAD_DIM)

    def body(q_ref, k_ref, v_ref, o_ref, bt_ref):
        i = pl.program_id(2)
        tri = _tri(tk, "gt")
        rowpos = lax.broadcasted_iota(jnp.int32, (tq, tk), 0) + i * tq
        colpos = lax.broadcasted_iota(jnp.int32, (tq, tk), 1)
        o_ref[...] = jnp.zeros_like(o_ref)
        bt_ref[...] = jnp.zeros_like(bt_ref)
        nkb = (i + 1) * kpq

        def step(n, carry):
            j = nkb - 1 - n
            rows = pl.ds(pl.multiple_of(j * tk, tk), tk)
            z = _dot(q_ref[...], k_ref[rows, :], NT) * scale
            mask = (colpos + j * tk) < rowpos
            a, b = _logsig_pair(z)
            b = jnp.where(mask, b, 0.0)
            suffix = _cumdot(b, tri)
            w = jnp.where(mask, jnp.exp(a + suffix + bt_ref[...]), 0.0)
            o_ref[...] += _dot(w.astype(BF16), v_ref[rows, :])
            bt_ref[...] += jnp.sum(b, axis=-1, keepdims=True)
            return carry

        lax.fori_loop(0, nkb, step, 0)

    qspec = pl.BlockSpec((tq, HEAD_DIM), lambda b, h, i: (b * nq + i, h))
    kspec = pl.BlockSpec((S, HEAD_DIM), lambda b, h, i: (b, h))
    btspec = pl.BlockSpec((None, None, tq, 1), lambda b, h, i: (b, h, i, 0))
    return pl.pallas_call(
        body, name="attn_fwd", grid=(Bl, H, nq), in_specs=[qspec, kspec, kspec], out_specs=[qspec, btspec],
        out_shape=[jax.ShapeDtypeStruct((T, C), F32), jax.ShapeDtypeStruct((Bl, H, S, 1), F32)],
        compiler_params=_params("parallel", "parallel", "arbitrary"),
    )(qn, kn, v)


def _attn_bwd(qn, kn, v, do, btot, Bl, S):
    T, C = qn.shape
    H = C // HEAD_DIM
    tq, tk, nq, kpq = _attn_tiles(S)
    scale = 1.0 / math.sqrt(HEAD_DIM)

    def body(q_ref, k_ref, v_ref, do_ref, bt_ref, dq_ref, dk_ref, dv_ref, pb_ref, pdl_ref):
        i = pl.program_id(2)

        @pl.when(i == 0)
        def _():
            dk_ref[...] = jnp.zeros_like(dk_ref)
            dv_ref[...] = jnp.zeros_like(dv_ref)

        tri_le = _tri(tk, "le")
        tri_lt = _tri(tk, "lt")
        rowpos = lax.broadcasted_iota(jnp.int32, (tq, tk), 0) + i * tq
        colpos = lax.broadcasted_iota(jnp.int32, (tq, tk), 1)
        dq_ref[...] = jnp.zeros_like(dq_ref)
        pb_ref[...] = bt_ref[...]
        pdl_ref[...] = jnp.zeros_like(pdl_ref)

        def step(j, carry):
            rows = pl.ds(pl.multiple_of(j * tk, tk), tk)
            kj = k_ref[rows, :]
            z = _dot(q_ref[...], kj, NT) * scale
            mask = (colpos + j * tk) < rowpos
            a, b = _logsig_pair(z)
            b = jnp.where(mask, b, 0.0)
            suffix = pb_ref[...] - _cumdot(b, tri_le)
            w = jnp.where(mask, jnp.exp(a + suffix), 0.0)
            dl = _dot(do_ref[...], v_ref[rows, :], NT) * w
            prefix = pdl_ref[...] + _cumdot(dl, tri_lt)
            beta = jnp.exp(a)
            dzb = (jnp.where(mask, dl * (1.0 - beta) - beta * prefix, 0.0) * scale).astype(BF16)
            dq_ref[...] += _dot(dzb, kj)
            dk_ref[rows, :] += _dot(dzb, q_ref[...], TN)
            dv_ref[rows, :] += _dot(w.astype(BF16), do_ref[...], TN)
            pb_ref[...] -= jnp.sum(b, axis=-1, keepdims=True)
            pdl_ref[...] += jnp.sum(dl, axis=-1, keepdims=True)
            return carry

        lax.fori_loop(0, (i + 1) * kpq, step, 0)

    qspec = pl.BlockSpec((tq, HEAD_DIM), lambda b, h, i: (b * nq + i, h))
    kspec = pl.BlockSpec((S, HEAD_DIM), lambda b, h, i: (b, h))
    btspec = pl.BlockSpec((None, None, tq, 1), lambda b, h, i: (b, h, i, 0))
    return pl.pallas_call(
        body, name="attn_bwd", grid=(Bl, H, nq), in_specs=[qspec, kspec, kspec, qspec, btspec],
        out_specs=[qspec, kspec, kspec],
        out_shape=[jax.ShapeDtypeStruct((T, C), F32)] * 3,
        scratch_shapes=[pltpu.VMEM((tq, 1), F32), pltpu.VMEM((tq, 1), F32)],
        compiler_params=_params("parallel", "parallel", "arbitrary"),
    )(qn, kn, v, do, btot)


SSM_TIME_BLOCK = 512
CHUNK = SUBLANES


def _cmadd(xr, xi, ar, ai, sr, si):
    return xr + ar * sr - ai * si, xi + ar * si + ai * sr


def _chunk_scan(xr, xi, tab_ref, cr, ci, reverse):
    for lvl, d in enumerate((1, 2, 4)):
        shift = (CHUNK - d) if reverse else d
        sr = pltpu.roll(xr, shift, 0)
        si = pltpu.roll(xi, shift, 0)
        ar = tab_ref[pl.ds((2 * lvl) * CHUNK, CHUNK), :]
        ai = tab_ref[pl.ds((2 * lvl + 1) * CHUNK, CHUNK), :]
        xr, xi = _cmadd(xr, xi, ar, ai, sr, si)
    pr = tab_ref[pl.ds(6 * CHUNK, CHUNK), :]
    pi = tab_ref[pl.ds(7 * CHUNK, CHUNK), :]
    return _cmadd(xr, xi, pr, pi, cr, ci)


def _ssm_dims(S, C):
    G = C // GROUP
    GT = min(16, G)
    return G, GT, G // GT, GT * GROUP, GT * STATE, min(SSM_TIME_BLOCK, S)


def _ssm_fwd(u, mats, Bl, S):
    T, C = u.shape
    G, GT, ngt, cw, sw, TB = _ssm_dims(S, C)
    ntb = S // TB
    nch = TB // CHUNK

    def body(u_ref, bre_ref, bim_ref, cre_ref, cim_ref, d_ref, tab_ref, y_ref, hr_ref, hi_ref, car_r, car_i):
        @pl.when(pl.program_id(2) == 0)
        def _():
            car_r[...] = jnp.zeros_like(car_r)
            car_i[...] = jnp.zeros_like(car_i)

        uv = u_ref[...]
        hr_ref[...] = _dot3(uv, bre_ref[...])
        hi_ref[...] = _dot3(uv, bim_ref[...])

        def step(n, carry):
            cr, ci = carry
            rows = pl.ds(pl.multiple_of(n * CHUNK, CHUNK), CHUNK)
            xr, xi = _chunk_scan(hr_ref[rows, :], hi_ref[rows, :], tab_ref, cr, ci, False)
            hr_ref[rows, :] = xr
            hi_ref[rows, :] = xi
            last = (CHUNK - 1, CHUNK)
            return (jnp.broadcast_to(xr[last[0]:last[1], :], xr.shape), jnp.broadcast_to(xi[last[0]:last[1], :], xi.shape))

        cr, ci = lax.fori_loop(0, nch, step, (car_r[...], car_i[...]))
        car_r[...] = cr
        car_i[...] = ci
        y_ref[...] = _dot3(hr_ref[...], cre_ref[...]) - _dot3(hi_ref[...], cim_ref[...]) + d_ref[...] * uv

    uspec = pl.BlockSpec((TB, cw), lambda g, b, t: (b * ntb + t, g))
    hspec = pl.BlockSpec((TB, sw), lambda g, b, t: (b * ntb + t, g))

    def gspec(r, c):
        return pl.BlockSpec((None, r, c), lambda g, b, t: (g, 0, 0))

    return pl.pallas_call(
        body, name="ssm_fwd", grid=(ngt, Bl, ntb),
        in_specs=[uspec, gspec(cw, sw), gspec(cw, sw), gspec(sw, cw), gspec(sw, cw), gspec(1, cw), gspec(8 * CHUNK, sw)],
        out_specs=[uspec, hspec, hspec],
        out_shape=[jax.ShapeDtypeStruct((T, C), F32), jax.ShapeDtypeStruct((T, G * STATE), F32),
                   jax.ShapeDtypeStruct((T, G * STATE), F32)],
        scratch_shapes=[pltpu.VMEM((CHUNK, sw), F32), pltpu.VMEM((CHUNK, sw), F32)],
        compiler_params=_params("parallel", "arbitrary", "arbitrary"),
    )(u, mats["bbd_re"], mats["bbd_im"], mats["cbd_re"], mats["cbd_im"], mats["d"], mats["tab_fwd"])


def _ssm_bwd(u, dy, h_re, h_im, mats, Bl, S):
    T, C = u.shape
    G, GT, ngt, cw, sw, TB = _ssm_dims(S, C)
    ntb = S // TB
    nch = TB // CHUNK
    rpb = TB // CHUNK

    def body(u_ref, dy_ref, hr_ref, hi_ref, hpr_ref, hpi_ref, ctr_ref, cti_ref, btr_ref, bti_ref, d_ref, tab_ref,
             du_ref, dbr_ref, dbi_ref, dcr_ref, dci_ref, dlr_ref, dli_ref, dd_ref, gr_ref, gi_ref, car_r, car_i):
        b = pl.program_id(1)
        t = pl.program_id(2)

        @pl.when((b == 0) & (t == 0))
        def _():
            for ref in (dbr_ref, dbi_ref, dcr_ref, dci_ref, dlr_ref, dli_ref, dd_ref):
                ref[...] = jnp.zeros_like(ref)

        @pl.when(t == 0)
        def _():
            car_r[...] = jnp.zeros_like(car_r)
            car_i[...] = jnp.zeros_like(car_i)

        uv = u_ref[...]
        dyv = dy_ref[...]
        gr_ref[...] = _dot3(dyv, ctr_ref[...])
        gi_ref[...] = -_dot3(dyv, cti_ref[...])
        alive = jnp.where(t == ntb - 1, 0.0, 1.0)
        row0 = lax.broadcasted_iota(jnp.int32, (CHUNK, sw), 0) == 0

        def step(m, carry):
            cr, ci, ar, ai = carry
            n = nch - 1 - m
            rows = pl.ds(pl.multiple_of(n * CHUNK, CHUNK), CHUNK)
            prow = pl.ds(pl.multiple_of(jnp.maximum(n - 1, 0) * CHUNK, CHUNK), CHUNK)
            xr, xi = _chunk_scan(gr_ref[rows, :], gi_ref[rows, :], tab_ref, cr, ci, True)
            gr_ref[rows, :] = xr
            gi_ref[rows, :] = xi
            first = n == 0
            pr = jnp.where(first, hpr_ref[...] * alive, hr_ref[prow, :])
            pi = jnp.where(first, hpi_ref[...] * alive, hi_ref[prow, :])
            sr = jnp.where(row0, pltpu.roll(pr, 1, 0), pltpu.roll(hr_ref[rows, :], 1, 0))
            si = jnp.where(row0, pltpu.roll(pi, 1, 0), pltpu.roll(hi_ref[rows, :], 1, 0))
            ar = ar + xr * sr + xi * si
            ai = ai + xi * sr - xr * si
            return (jnp.broadcast_to(xr[0:1, :], xr.shape), jnp.broadcast_to(xi[0:1, :], xi.shape), ar, ai)

        zero = jnp.zeros((CHUNK, sw), F32)
        cr, ci, ar, ai = lax.fori_loop(0, nch, step, (car_r[...], car_i[...], zero, zero))
        car_r[...] = cr
        car_i[...] = ci
        dlr_ref[...] += ar
        dli_ref[...] += ai
        gr = gr_ref[...]
        gi = gi_ref[...]
        dbr_ref[...] += _dot3(uv, gr, TN)
        dbi_ref[...] += _dot3(uv, gi, TN)
        dcr_ref[...] += _dot3(hr_ref[...], dyv, TN)
        dci_ref[...] -= _dot3(hi_ref[...], dyv, TN)
        dd_ref[...] += jnp.sum(dyv * uv, axis=0, keepdims=True)
        du_ref[...] = _dot3(gr, btr_ref[...]) + _dot3(gi, bti_ref[...]) + d_ref[...] * dyv

    def tblk(b, t):
        return b * ntb + (ntb - 1 - t)

    uspec = pl.BlockSpec((TB, cw), lambda g, b, t: (tblk(b, t), g))
    hspec = pl.BlockSpec((TB, sw), lambda g, b, t: (tblk(b, t), g))
    hpspec = pl.BlockSpec((CHUNK, sw), lambda g, b, t: (jnp.maximum(tblk(b, t) * rpb - 1, 0), g))

    def gspec(r, c):
        return pl.BlockSpec((None, r, c), lambda g, b, t: (g, 0, 0))

    def gshape(r, c):
        return jax.ShapeDtypeStruct((ngt, r, c), F32)

    return pl.pallas_call(
        body, name="ssm_bwd", grid=(ngt, Bl, ntb),
        in_specs=[uspec, uspec, hspec, hspec, hpspec, hpspec, gspec(cw, sw), gspec(cw, sw), gspec(sw, cw), gspec(sw, cw),
                  gspec(1, cw), gspec(8 * CHUNK, sw)],
        out_specs=[uspec, gspec(cw, sw), gspec(cw, sw), gspec(sw, cw), gspec(sw, cw), gspec(CHUNK, sw), gspec(CHUNK, sw),
                   gspec(1, cw)],
        out_shape=[jax.ShapeDtypeStruct((T, C), F32), gshape(cw, sw), gshape(cw, sw), gshape(sw, cw), gshape(sw, cw),
                   gshape(CHUNK, sw), gshape(CHUNK, sw), gshape(1, cw)],
        scratch_shapes=[pltpu.VMEM((TB, sw), F32), pltpu.VMEM((TB, sw), F32), pltpu.VMEM((CHUNK, sw), F32),
                        pltpu.VMEM((CHUNK, sw), F32)],
        compiler_params=_params("arbitrary", "arbitrary", "arbitrary"),
    )(u, dy, h_re, h_im, h_re, h_im, mats["cbdT_re"], mats["cbdT_im"], mats["bbdT_re"], mats["bbdT_im"], mats["d"],
      mats["tab_rev"])


def _zoh(a_re, a_im, log_dt, b_re, b_im):
    dt = jnp.exp(log_dt)[:, None]
    mag = jnp.exp(a_re * dt)
    l_re = mag * jnp.cos(a_im * dt)
    l_im = mag * jnp.sin(a_im * dt)
    den = a_re * a_re + a_im * a_im
    f_re = ((l_re - 1.0) * a_re + l_im * a_im) / den
    f_im = (l_im * a_re - (l_re - 1.0) * a_im) / den
    bb_re = f_re[..., None] * b_re - f_im[..., None] * b_im
    bb_im = f_re[..., None] * b_im + f_im[..., None] * b_re
    return l_re, l_im, bb_re, bb_im


def _ssm_matrices(a_re, a_im, log_dt, b_re, b_im, c_re, c_im, d, S):
    G = a_re.shape[0]
    _, GT, ngt, cw, sw, _ = _ssm_dims(S, G * GROUP)
    _, _, bb_re, bb_im = _zoh(a_re, a_im, log_dt, b_re, b_im)
    eye = jnp.eye(GT, dtype=F32)

    def bd_b(bb):
        return jnp.einsum("tgpi,gh->tgihp", bb.reshape(ngt, GT, STATE, GROUP), eye).reshape(ngt, cw, sw)

    def bd_c(c):
        return jnp.einsum("tgip,gh->tgphi", c.reshape(ngt, GT, GROUP, STATE), eye).reshape(ngt, sw, cw)

    dt = jnp.exp(log_dt)[:, None]

    def power(k, conj):
        mag = jnp.exp(k * a_re * dt)
        ang = k * a_im * dt
        return (mag * jnp.cos(ang)).reshape(ngt, 1, sw), ((-1.0 if conj else 1.0) * mag * jnp.sin(ang)).reshape(ngt, 1, sw)

    r = jnp.arange(CHUNK)[None, :, None]

    def table(reverse):
        parts = []
        for dd in (1, 2, 4):
            pr, pi = power(float(dd), reverse)
            keep = (r <= CHUNK - 1 - dd) if reverse else (r >= dd)
            parts += [jnp.where(keep, pr, 0.0), jnp.where(keep, pi, 0.0)]
        exps = [(CHUNK - k) if reverse else (k + 1) for k in range(CHUNK)]
        pw = [power(float(e), reverse) for e in exps]
        parts += [jnp.concatenate([p[0] for p in pw], axis=1), jnp.concatenate([p[1] for p in pw], axis=1)]
        return jnp.concatenate([jnp.broadcast_to(p, (ngt, CHUNK, sw)) for p in parts], axis=1)

    mats = dict(bbd_re=bd_b(bb_re), bbd_im=bd_b(bb_im), cbd_re=bd_c(c_re), cbd_im=bd_c(c_im),
                d=d.reshape(ngt, 1, cw), tab_fwd=table(False), tab_rev=table(True))
    for k in ("bbd_re", "bbd_im", "cbd_re", "cbd_im"):
        mats[k.replace("bd_", "bdT_")] = jnp.swapaxes(mats[k], 1, 2)
    return mats


def _ssm_unblock(dbr, dbi, dcr, dci, dlr, dli, dd, G):
    ngt = dbr.shape[0]
    GT = G // ngt
    eye = jnp.eye(GT, dtype=F32)

    def ub(x):
        return jnp.einsum("tgihp,gh->tgpi", x.reshape(ngt, GT, GROUP, GT, STATE), eye).reshape(G, STATE, GROUP)

    def uc(x):
        return jnp.einsum("tgphi,gh->tgip", x.reshape(ngt, GT, STATE, GT, GROUP), eye).reshape(G, GROUP, STATE)

    return (dlr.sum(axis=1).reshape(G, STATE), dli.sum(axis=1).reshape(G, STATE), ub(dbr), ub(dbi), uc(dcr), uc(dci),
            dd.reshape(G * GROUP))


def _attn_gate_fwd(o, gate):
    T, C = o.shape

    def body(o_ref, g_ref, out_ref):
        out_ref[...] = (o_ref[...] * _silu_parts(g_ref[...])[0]).astype(BF16)

    return _rows_call(body, "attn_gate_fwd", T, C, [o, gate], [], [BF16])[0]


def _attn_gate_bwd(dog, o, gate):
    T, C = o.shape

    def body(d_ref, o_ref, g_ref, do_ref, dg_ref):
        val, der = _silu_parts(g_ref[...])
        dv = d_ref[...]
        do_ref[...] = (dv * val).astype(BF16)
        dg_ref[...] = (dv * o_ref[...] * der).astype(BF16)

    return _rows_call(body, "attn_gate_bwd", T, C, [dog, o, gate], [], [BF16, BF16])


def _gelu_fwd(y):
    T, C = y.shape

    def body(y_ref, out_ref):
        out_ref[...] = _gelu_parts(y_ref[...])[0].astype(BF16)

    return _rows_call(body, "gelu_fwd", T, C, [y], [], [BF16])[0]


def _glu_fwd(y, gl, gate, glu_b):
    T, C = y.shape

    def body(y_ref, gl_ref, g_ref, b_ref, out_ref):
        yg = _gelu_parts(y_ref[...])[0]
        sg = _sigmoid(gl_ref[...] + b_ref[...])
        out_ref[...] = (yg * sg * _silu_parts(g_ref[...])[0]).astype(BF16)

    return _rows_call(body, "glu_fwd", T, C, [y, gl, gate], [glu_b.reshape(1, C)], [BF16])[0]


def _glu_bwd(dy3, y, gl, gate, glu_b):
    T, C = y.shape

    def body(d_ref, y_ref, gl_ref, g_ref, b_ref, dgl_ref, dgate_ref, t1_ref, db_ref):
        @pl.when(pl.program_id(0) == 0)
        def _():
            db_ref[...] = jnp.zeros_like(db_ref)

        yg = _gelu_parts(y_ref[...])[0]
        sg = _sigmoid(gl_ref[...] + b_ref[...])
        sl, sld = _silu_parts(g_ref[...])
        dv = d_ref[...]
        dy2 = dv * sl
        dgl = dy2 * yg * sg * (1.0 - sg)
        dgl_ref[...] = dgl.astype(BF16)
        dgate_ref[...] = (dv * (yg * sg) * sld).astype(BF16)
        t1_ref[...] = dy2 * sg
        db_ref[...] += jnp.sum(dgl, axis=0, keepdims=True)

    return _rows_call(body, "glu_bwd", T, C, [dy3, y, gl, gate], [glu_b.reshape(1, C)], [BF16, BF16, F32], [(1, C)])


def _gelu_bwd(t1, t2, y):
    T, C = y.shape

    def body(a_ref, b_ref, y_ref, out_ref):
        out_ref[...] = (a_ref[...] + b_ref[...]) * _gelu_parts(y_ref[...])[1]

    return _rows_call(body, "gelu_bwd", T, C, [t1, t2, y], [], [F32])[0]


def _loss_head(x2, target):
    T, C = x2.shape

    def body(x_ref, t_ref, d_ref, db_ref, l_ref):
        @pl.when(pl.program_id(0) == 0)
        def _():
            l_ref[...] = jnp.zeros_like(l_ref)

        e = x_ref[...] - t_ref[...]
        d = e * (1.0 / C)
        d_ref[...] = d
        db_ref[...] = d.astype(BF16)
        l_ref[...] += 0.5 * jnp.sum(jnp.sum(e * e, axis=-1, keepdims=True) * (1.0 / C), axis=0, keepdims=True)

    return _rows_call(body, "loss_head", T, C, [x2, target], [], [F32, BF16], [(1, 1)])


def _adamw(w, g, m, v, name):
    shape = w.shape
    C = shape[-1]
    R = w.size // C
    bc1 = 1.0 - ADAM_B1 ** ADAM_STEP
    bc2 = 1.0 - ADAM_B2 ** ADAM_STEP

    def body(w_ref, g_ref, m_ref, v_ref, d_ref, nm_ref, nv_ref):
        gv = g_ref[...]
        mn = ADAM_B1 * m_ref[...] + (1.0 - ADAM_B1) * gv
        vn = ADAM_B2 * v_ref[...] + (1.0 - ADAM_B2) * (gv * gv)
        d_ref[...] = -ADAM_LR * ((mn / bc1) / (jnp.sqrt(vn / bc2) + ADAM_EPS) + ADAM_WD * w_ref[...])
        nm_ref[...] = mn
        nv_ref[...] = vn

    outs = _rows_call(body, name, R, C, [a.reshape(R, C) for a in (w, g, m, v)], [], [F32, F32, F32])
    return [o.reshape(shape) for o in outs]


def _sum_leading(x, name):
    n, R, C = x.shape
    tr = _row_tile(R, C * n)

    def body(x_ref, o_ref):
        acc = x_ref[0].astype(F32)
        for k in range(1, n):
            acc = acc + x_ref[k].astype(F32)
        o_ref[...] = acc

    return pl.pallas_call(
        body, name=name, grid=(R // tr,), in_specs=[pl.BlockSpec((n, tr, C), lambda i: (0, i, 0))],
        out_specs=pl.BlockSpec((tr, C), lambda i: (i, 0)), out_shape=jax.ShapeDtypeStruct((R, C), F32),
        compiler_params=_params("parallel"),
    )(x)


def _add_halves(g, c, name):
    full, recv = g
    n, R, C = full.shape
    half = R // 2
    tr = _row_tile(half, C)
    nb = half // tr

    def body(c_ref, a_ref, b_ref, o_ref):
        o_ref[...] = (a_ref[...] + b_ref[...]).astype(BF16)

    grid_spec = pltpu.PrefetchScalarGridSpec(
        num_scalar_prefetch=1, grid=(n, nb),
        in_specs=[pl.BlockSpec((None, tr, C), lambda j, i, c_ref: (j, c_ref[0] * nb + i, 0)),
                  pl.BlockSpec((None, tr, C), lambda j, i, c_ref: (j, i, 0))],
        out_specs=pl.BlockSpec((None, tr, C), lambda j, i, c_ref: (j, i, 0)))
    return pl.pallas_call(
        body, name=name, grid_spec=grid_spec, out_shape=jax.ShapeDtypeStruct((n, half, C), BF16),
        compiler_params=_params("parallel", "parallel"),
    )(c.reshape(1).astype(jnp.int32), full, recv)


ANY = pl.BlockSpec(memory_space=pl.ANY)


def _position():
    return lax.axis_index("x"), lax.axis_index("y"), lax.axis_index("c")


def _all_gather8(blk, name):
    M, N = blk.shape

    def body(x_ref, out_ref, send_sems, recv_sems, local_sem):
        x, y, c = _position()
        me, sibling = (x, y, c), (x, y, 1 - c)
        chips = [(1 - x, y), (x, 1 - y), (1 - x, 1 - y)]

        def slab(px, py, pc):
            return out_ref.at[4 * px + 2 * py + pc]

        def copy(k, block, to, src=None):
            return pltpu.make_async_remote_copy(
                src_ref=slab(*block) if src is None else src, dst_ref=slab(*block),
                send_sem=send_sems.at[k], recv_sem=recv_sems.at[k], device_id=to, device_id_type=MESH)

        mine = pltpu.make_async_copy(x_ref, slab(*me), local_sem)
        mine.start()
        first = [copy(0, me, sibling, src=x_ref)]
        first += [copy(1 + j, me, (*chip, c), src=x_ref) for j, chip in enumerate(chips)]
        for cp in first:
            cp.start()
        passed = [copy(4 + j, (*chip, c), sibling) for j, chip in enumerate(chips)]
        for j, chip in enumerate(chips):
            copy(1 + j, (*chip, c), me).wait_recv()
            passed[j].start()
        copy(0, sibling, me).wait_recv()
        for j, chip in enumerate(chips):
            copy(4 + j, (*chip, 1 - c), me).wait_recv()
        for cp in first + passed:
            cp.wait_send()
        mine.wait()

    return pl.pallas_call(
        body, name=name, in_specs=[ANY], out_specs=ANY, out_shape=jax.ShapeDtypeStruct((N_DEV, M, N), blk.dtype),
        scratch_shapes=[pltpu.SemaphoreType.DMA((7,)), pltpu.SemaphoreType.DMA((7,)), pltpu.SemaphoreType.DMA],
    )(blk)


def _sibling_send_half(g, name):
    n, R, C = g.shape
    half = R // 2

    def body(g_ref, out_ref, send_sem, recv_sem):
        x, y, c = _position()
        src = g_ref.at[:, pl.ds((1 - c) * half, half), :]
        cp = pltpu.make_async_remote_copy(src_ref=src, dst_ref=out_ref, send_sem=send_sem, recv_sem=recv_sem,
                                          device_id=(x, y, 1 - c), device_id_type=MESH)
        cp.start()
        cp.wait()

    return pl.pallas_call(
        body, name=name, in_specs=[ANY], out_specs=ANY, out_shape=jax.ShapeDtypeStruct((n, half, C), g.dtype),
        scratch_shapes=[pltpu.SemaphoreType.DMA, pltpu.SemaphoreType.DMA],
    )(g)


def _chip_exchange(p, name):
    n, M, C = p.shape

    def body(p_ref, out_ref, send_sems, recv_sems, local_sem):
        x, y, c = _position()
        my = 2 * x + y
        chips = [(1 - x, y), (x, 1 - y), (1 - x, 1 - y)]
        mine = pltpu.make_async_copy(p_ref.at[my], out_ref.at[my], local_sem)
        mine.start()
        copies = []
        for k, (px, py) in enumerate(chips):
            cp = pltpu.make_async_remote_copy(
                src_ref=p_ref.at[2 * px + py], dst_ref=out_ref.at[my], send_sem=send_sems.at[k], recv_sem=recv_sems.at[k],
                device_id=(px, py, c), device_id_type=MESH)
            cp.start()
            copies.append(cp)
        for k, (px, py) in enumerate(chips):
            pltpu.make_async_remote_copy(
                src_ref=p_ref.at[my], dst_ref=out_ref.at[2 * px + py], send_sem=send_sems.at[k], recv_sem=recv_sems.at[k],
                device_id=(px, py, c), device_id_type=MESH).wait_recv()
        for cp in copies:
            cp.wait_send()
        mine.wait()

    return pl.pallas_call(
        body, name=name, in_specs=[ANY], out_specs=ANY, out_shape=jax.ShapeDtypeStruct((n, M, C), p.dtype),
        scratch_shapes=[pltpu.SemaphoreType.DMA((3,)), pltpu.SemaphoreType.DMA((3,)), pltpu.SemaphoreType.DMA],
    )(p)


def _sibling_join(f, name):
    M, C = f.shape

    def body(f_ref, out_ref, send_sem, recv_sem, local_sem):
        x, y, c = _position()
        mine = pltpu.make_async_copy(f_ref, out_ref.at[pl.ds(c * M, M), :], local_sem)
        mine.start()
        cp = pltpu.make_async_remote_copy(src_ref=f_ref, dst_ref=out_ref.at[pl.ds(c * M, M), :], send_sem=send_sem,
                                          recv_sem=recv_sem, device_id=(x, y, 1 - c), device_id_type=MESH)
        cp.start()
        pltpu.make_async_remote_copy(src_ref=f_ref, dst_ref=out_ref.at[pl.ds((1 - c) * M, M), :], send_sem=send_sem,
                                     recv_sem=recv_sem, device_id=(x, y, 1 - c), device_id_type=MESH).wait_recv()
        cp.wait_send()
        mine.wait()

    return pl.pallas_call(
        body, name=name, in_specs=[ANY], out_specs=ANY, out_shape=jax.ShapeDtypeStruct((2 * M, C), f.dtype),
        scratch_shapes=[pltpu.SemaphoreType.DMA, pltpu.SemaphoreType.DMA, pltpu.SemaphoreType.DMA],
    )(f)


def _pack_dense(attn_in, attn_out, ssm_in, glu_w, ssm_out):
    D = attn_in.shape[0]
    return jnp.concatenate([attn_in, attn_out, ssm_in.reshape(D // 2, D), glu_w, ssm_out], axis=0)


def _unpack_dense(p):
    D = p.shape[-1]
    q = D // 4
    o = [0, D, D + q, D + q + D // 2, D + 2 * q + D // 2, D + 3 * q + D // 2]
    lead = p.shape[:-2]
    return (p[..., o[0]:o[1], :], p[..., o[1]:o[2], :], p[..., o[2]:o[3], :].reshape(*lead, D, D // 2),
            p[..., o[3]:o[4], :], p[..., o[4]:o[5], :])


SMALL_REPL = ("norm_g", "attn_q_g", "attn_k_g")


def _pack_small(parts):
    flat = jnp.concatenate([p.reshape(-1) for p in parts])
    pad = (-flat.size) % (2 * SUBLANES * LANES)
    return jnp.pad(flat, (0, pad)).reshape(-1, LANES)


def _unpack_small(buf, shapes):
    flat = buf.reshape(-1)
    out, off = [], 0
    for s in shapes:
        n = math.prod(s)
        out.append(flat[off:off + n].reshape(s))
        off += n
    return out


def _local_step(x, target, norm_g, q_g, k_g, w_attn_in, w_attn_out, w_ssm_in, w_glu, w_ssm_out, ssm_small):
    Bl, S, D = x.shape
    T = Bl * S
    x0 = x.reshape(T, D)
    tgt = target.reshape(T, D)
    a_re, a_im, log_dt, b_re, b_im, c_re, c_im, d_skip, glu_b = ssm_small
    G = a_re.shape[0]
    mats = _ssm_matrices(a_re, a_im, log_dt, b_re, b_im, c_re, c_im, d_skip, S)

    h0 = _rmsnorm_fwd(x0, norm_g[0], "norm0_fwd")
    q, k, v, gate = [_matmul(h0, w_attn_in[j], name=f"attn_in_{j}", out_dtype=(BF16 if j == 2 else F32)) for j in range(4)]
    qn = _qknorm_fwd(q, q_g, "qnorm_fwd")
    kn = _qknorm_fwd(k, k_g, "knorm_fwd")
    o, btot = _attn_fwd(qn, kn, v, Bl, S)
    og = _attn_gate_fwd(o, gate)
    x1 = _matmul(og, w_attn_out, name="attn_out", residual=x0)

    h1 = _rmsnorm_fwd(x1, norm_g[1], "norm1_fwd")
    u = _matmul(h1, w_ssm_in[:, :D], name="ssm_in_u")
    gate2 = _matmul(h1, w_ssm_in[:, D:], name="ssm_in_gate")
    y, hs_re, hs_im = _ssm_fwd(u, mats, Bl, S)
    yg = _gelu_fwd(y)
    gl = _matmul(yg, w_glu, name="glu_mm")
    y3 = _glu_fwd(y, gl, gate2, glu_b)
    x2 = _matmul(y3, w_ssm_out, name="ssm_out", residual=x1)

    dx2, dx2b, loss = _loss_head(x2, tgt)

    dy3 = _matmul(dx2b, w_ssm_out, name="ssm_out_dgrad", tb=True)
    g_ssm_out = _matmul(y3, dx2b, name="ssm_out_wgrad", ta=True)
    dgl, dgate2, t1, dglu_b = _glu_bwd(dy3, y, gl, gate2, glu_b)
    t2 = _matmul(dgl, w_glu, name="glu_dgrad", tb=True)
    g_glu = _matmul(yg, dgl, name="glu_wgrad", ta=True)
    dy = _gelu_bwd(t1, t2, y)
    du, dbr, dbi, dcr, dci, dlr, dli, dd = _ssm_bwd(u, dy, hs_re, hs_im, mats, Bl, S)
    dh1 = _matmul(du, w_ssm_in[:, :D], name="ssm_in_dgrad_u", tb=True)
    dh1 = _matmul(dgate2, w_ssm_in[:, D:], name="ssm_in_dgrad_gate", tb=True, residual=dh1)
    g_ssm_in = jnp.concatenate([_matmul(h1, du, name="ssm_in_wgrad_u", ta=True),
                                _matmul(h1, dgate2, name="ssm_in_wgrad_gate", ta=True)], axis=1)
    dx1, dx1b, dng1 = _rmsnorm_bwd(x1, norm_g[1], dh1, dx2, "norm1_bwd")

    dog = _matmul(dx1b, w_attn_out, name="attn_out_dgrad", tb=True)
    g_attn_out = _matmul(og, dx1b, name="attn_out_wgrad", ta=True)
    do, dgate = _attn_gate_bwd(dog, o, gate)
    dqn, dkn, dv = _attn_bwd(qn, kn, v, do, btot, Bl, S)
    dq, dqg = _qknorm_bwd(q, q_g, dqn, "qnorm_bwd")
    dk, dkg = _qknorm_bwd(k, k_g, dkn, "knorm_bwd")
    dproj = [dq, dk, dv, dgate]
    dh0 = None
    for j in range(4):
        dh0 = _matmul(dproj[j], w_attn_in[j], name=f"attn_in_dgrad_{j}", tb=True, residual=dh0)
    g_attn_in = jnp.stack([_matmul(h0, dproj[j], name=f"attn_in_wgrad_{j}", ta=True) for j in range(4)])
    dx0, _, dng0 = _rmsnorm_bwd(x0, norm_g[0], dh0, dx1, "norm0_bwd")

    dense = (g_attn_in, g_attn_out, g_ssm_in, g_glu, g_ssm_out)
    small = (jnp.concatenate([dng0, dng1], axis=0), dqg, dkg) + _ssm_unblock(dbr, dbi, dcr, dci, dlr, dli, dd, G) + (
        dglu_b.reshape(D),)
    return loss, dx0.reshape(Bl, S, D), dense, small


def _chip_rows(a, chip, n_per):
    return lax.dynamic_slice_in_dim(a, chip * n_per, n_per, axis=0)


def kernel(x, norm_g, attn_w_in, attn_q_g, attn_k_g, attn_w_out, ssm_w_in, ssm_A_re, ssm_A_im, ssm_log_dt, ssm_B_re, ssm_B_im, ssm_C_re, ssm_C_im, ssm_D, ssm_glu_w, ssm_glu_b, ssm_w_out, loss_target, m_norm_g, m_attn_w_in, m_attn_q_g, m_attn_k_g, m_attn_w_out, m_ssm_w_in, m_ssm_A_re, m_ssm_A_im, m_ssm_log_dt, m_ssm_B_re, m_ssm_B_im, m_ssm_C_re, m_ssm_C_im, m_ssm_D, m_ssm_glu_w, m_ssm_glu_b, m_ssm_w_out, v_norm_g, v_attn_w_in, v_attn_q_g, v_attn_k_g, v_attn_w_out, v_ssm_w_in, v_ssm_A_re, v_ssm_A_im, v_ssm_log_dt, v_ssm_B_re, v_ssm_B_im, v_ssm_C_re, v_ssm_C_im, v_ssm_D, v_ssm_glu_w, v_ssm_glu_b, v_ssm_w_out):
    D = x.shape[-1]
    cx, cy, cc = _position()
    chip = 2 * cx + cy
    G = D // GROUP
    Gl = G // N_CHIPS

    dense_local = _pack_dense(attn_w_in[0], attn_w_out[0], ssm_w_in[0], ssm_glu_w[0], ssm_w_out[0]).astype(BF16)
    R = dense_local.shape[0]
    my_half = lax.dynamic_slice_in_dim(dense_local, cc * (R // 2), R // 2, axis=0)
    dense_all = _all_gather8(my_half, "weights_all_gather").reshape(N_CHIPS, R, D)
    wa_in, wa_out, ws_in, w_glu, ws_out = _unpack_dense(dense_all)
    wa_out = wa_out.reshape(D, D)
    ws_in = jnp.swapaxes(ws_in, 0, 1).reshape(D, 2 * D)
    w_glu = w_glu.reshape(D, D)
    ws_out = ws_out.reshape(D, D)

    ssm_local = [ssm_A_re[0], ssm_A_im[0], ssm_log_dt[0], ssm_B_re[0], ssm_B_im[0], ssm_C_re[0], ssm_C_im[0], ssm_D[0],
                 ssm_glu_b[0]]
    small_local = _pack_small(ssm_local)
    half_rows = small_local.shape[0] // 2
    small_half = lax.dynamic_slice_in_dim(small_local, cc * half_rows, half_rows, axis=0)
    small_all = _all_gather8(small_half, "ssm_params_all_gather").reshape(N_CHIPS, 2 * half_rows, LANES)
    per_chip = [_unpack_small(small_all[j], [p.shape for p in ssm_local]) for j in range(N_CHIPS)]
    ssm_full = [jnp.concatenate([per_chip[j][i] for j in range(N_CHIPS)], axis=0) for i in range(len(ssm_local))]

    loss, grad_x, dense, small = _local_step(x, loss_target, norm_g, attn_q_g[0], attn_k_g[0], wa_in, wa_out, ws_in,
                                              w_glu, ws_out, ssm_full)
    loss = lax.psum(loss[0, 0], ("x", "y", "c"))

    g_attn_in, g_attn_out, g_ssm_in, g_glu, g_ssm_out = dense
    q4 = D // N_CHIPS
    packed = jnp.stack([
        _pack_dense(g_attn_in[j], g_attn_out[j * q4:(j + 1) * q4], g_ssm_in[:, j * (D // 2):(j + 1) * (D // 2)],
                    g_glu[j * q4:(j + 1) * q4], g_ssm_out[j * q4:(j + 1) * q4]) for j in range(N_CHIPS)])
    from_sibling = _sibling_send_half(packed, "grads_sibling_half")
    chip_sum = _add_halves((packed, from_sibling), cc, "grads_add_halves")
    gathered = _chip_exchange(chip_sum, "grads_chip_exchange")
    my_sum = _sum_leading(gathered, "grads_sum_chips")
    dense_grad = _sibling_join(my_sum, "grads_sibling_join")
    gd = _unpack_dense(dense_grad)

    small_shapes = [s.shape for s in small]
    small_sum = _sum_leading(_all_gather8(_pack_small(small), "small_grads_all_gather"), "small_grads_sum")
    (dng, dqg, dkg, dl_re, dl_im, dbb_re, dbb_im, dc_re, dc_im, dd_skip, dglu_b) = _unpack_small(small_sum, small_shapes)
    a_re, a_im, log_dt, b_re, b_im = ssm_local[:5]
    _, zoh_vjp = jax.vjp(_zoh, a_re, a_im, log_dt, b_re, b_im)
    da_re, da_im, dlog_dt, db_re, db_im = zoh_vjp((_chip_rows(dl_re, chip, Gl), _chip_rows(dl_im, chip, Gl),
                                                   _chip_rows(dbb_re, chip, Gl), _chip_rows(dbb_im, chip, Gl)))
    grads = {
        "norm_g": dng, "attn_w_in": gd[0][None], "attn_q_g": dqg, "attn_k_g": dkg, "attn_w_out": gd[1][None],
        "ssm_w_in": gd[2][None], "ssm_A_re": da_re[None], "ssm_A_im": da_im[None], "ssm_log_dt": dlog_dt[None],
        "ssm_B_re": db_re[None], "ssm_B_im": db_im[None], "ssm_C_re": _chip_rows(dc_re, chip, Gl)[None],
        "ssm_C_im": _chip_rows(dc_im, chip, Gl)[None], "ssm_D": _chip_rows(dd_skip, chip, q4)[None],
        "ssm_glu_w": gd[3][None], "ssm_glu_b": _chip_rows(dglu_b, chip, q4)[None], "ssm_w_out": gd[4][None],
    }
    weights = dict(norm_g=norm_g, attn_w_in=attn_w_in, attn_q_g=attn_q_g, attn_k_g=attn_k_g, attn_w_out=attn_w_out,
                   ssm_w_in=ssm_w_in, ssm_A_re=ssm_A_re, ssm_A_im=ssm_A_im, ssm_log_dt=ssm_log_dt, ssm_B_re=ssm_B_re,
                   ssm_B_im=ssm_B_im, ssm_C_re=ssm_C_re, ssm_C_im=ssm_C_im, ssm_D=ssm_D, ssm_glu_w=ssm_glu_w,
                   ssm_glu_b=ssm_glu_b, ssm_w_out=ssm_w_out)
    m = dict(norm_g=m_norm_g, attn_w_in=m_attn_w_in, attn_q_g=m_attn_q_g, attn_k_g=m_attn_k_g, attn_w_out=m_attn_w_out,
             ssm_w_in=m_ssm_w_in, ssm_A_re=m_ssm_A_re, ssm_A_im=m_ssm_A_im, ssm_log_dt=m_ssm_log_dt, ssm_B_re=m_ssm_B_re,
             ssm_B_im=m_ssm_B_im, ssm_C_re=m_ssm_C_re, ssm_C_im=m_ssm_C_im, ssm_D=m_ssm_D, ssm_glu_w=m_ssm_glu_w,
             ssm_glu_b=m_ssm_glu_b, ssm_w_out=m_ssm_w_out)
    v = dict(norm_g=v_norm_g, attn_w_in=v_attn_w_in, attn_q_g=v_attn_q_g, attn_k_g=v_attn_k_g, attn_w_out=v_attn_w_out,
             ssm_w_in=v_ssm_w_in, ssm_A_re=v_ssm_A_re, ssm_A_im=v_ssm_A_im, ssm_log_dt=v_ssm_log_dt, ssm_B_re=v_ssm_B_re,
             ssm_B_im=v_ssm_B_im, ssm_C_re=v_ssm_C_re, ssm_C_im=v_ssm_C_im, ssm_D=v_ssm_D, ssm_glu_w=v_ssm_glu_w,
             ssm_glu_b=v_ssm_glu_b, ssm_w_out=v_ssm_w_out)
    names = list(weights)
    dense_names = ("attn_w_in", "attn_w_out", "ssm_w_in", "ssm_glu_w", "ssm_w_out")
    delta, new_m, new_v = {}, {}, {}
    for n in dense_names:
        delta[n], new_m[n], new_v[n] = _adamw(weights[n], grads[n], m[n], v[n], "adamw_" + n)
    small_names = [n for n in names if n not in dense_names]
    small_shapes = [weights[n].shape for n in small_names]
    packs = [_pack_small([d[n] for n in small_names]) for d in (weights, grads, m, v)]
    outs = _adamw(*packs, "adamw_small")
    for res, out in zip((delta, new_m, new_v), outs):
        for n, val in zip(small_names, _unpack_small(out, small_shapes)):
            res[n] = val
    return (loss, grad_x, *[grads[n] for n in names], *[delta[n] for n in names], *[new_m[n] for n in names],
            *[new_v[n] for n in names])
```

```python
import functools
import math

import jax
import jax.numpy as jnp
from jax import lax
from jax.experimental import pallas as pl
from jax.experimental.pallas import tpu as pltpu

F32 = jnp.float32
BF16 = jnp.bfloat16

HEAD_DIM = 128
GROUP = 16
STATE = 64
RMS_EPS = 1e-6
ADAM_LR = 0.001
ADAM_B1 = 0.9
ADAM_B2 = 0.999
ADAM_EPS = 1e-08
ADAM_WD = 0.01
ADAM_STEP = 10

N_CHIPS = 4
N_DEV = 8
SUBLANES = 8
LANES = 128
VMEM_LIMIT = 56 * 1024 * 1024
ROW_BLOCK_ELEMS = 1 << 18
MESH = pl.DeviceIdType.MESH

NN = (((1,), (0,)), ((), ()))
NT = (((1,), (1,)), ((), ()))
TN = (((0,), (0,)), ((), ()))


def _params(*sem):
    return pltpu.CompilerParams(dimension_semantics=sem, vmem_limit_bytes=VMEM_LIMIT)


def _dot(a, b, dims=NN):
    return lax.dot_general(a, b, dims, preferred_element_type=F32)


def _split(a):
    hi = a.astype(BF16)
    lo = (a - hi.astype(F32)).astype(BF16)
    return hi, lo


def _dot3(a, b, dims=NN):
    a1, a2 = _split(a)
    b1, b2 = _split(b)
    return _dot(a1, b1, dims) + _dot(a1, b2, dims) + _dot(a2, b1, dims)


def _sigmoid(x):
    return 1.0 / (1.0 + jnp.exp(-x))


def _silu_parts(x):
    s = _sigmoid(x)
    return x * s, s * (1.0 + x * (1.0 - s))


_GELU_C = math.sqrt(2.0 / math.pi)


def _gelu_parts(x):
    x2 = x * x
    t = jnp.tanh(_GELU_C * (x + 0.044715 * x * x2))
    val = 0.5 * x * (1.0 + t)
    der = 0.5 * (1.0 + t) + 0.5 * x * (1.0 - t * t) * _GELU_C * (1.0 + 3.0 * 0.044715 * x2)
    return val, der


def _matmul(a, b, *, name, ta=False, tb=False, residual=None, out_dtype=F32):
    if ta:
        K, M = a.shape
    else:
        M, K = a.shape
    N = b.shape[0] if tb else b.shape[1]
    tm, tn, tk = min(M, 1024), min(N, 1024), min(K, 1024)
    nk = K // tk
    dims = ((((0,) if ta else (1,)), ((1,) if tb else (0,))), ((), ()))

    def body(*refs):
        if residual is None:
            a_ref, b_ref, o_ref, acc = refs
        else:
            a_ref, b_ref, r_ref, o_ref, acc = refs
        k = pl.program_id(2)

        @pl.when(k == 0)
        def _():
            acc[...] = jnp.zeros_like(acc)

        acc[...] += _dot(a_ref[...].astype(BF16), b_ref[...].astype(BF16), dims)

        @pl.when(k == nk - 1)
        def _():
            r = acc[...]
            if residual is not None:
                r = r + r_ref[...].astype(F32)
            o_ref[...] = r.astype(out_dtype)

    a_spec = pl.BlockSpec((tk, tm), lambda i, j, k: (k, i)) if ta else pl.BlockSpec((tm, tk), lambda i, j, k: (i, k))
    b_spec = pl.BlockSpec((tn, tk), lambda i, j, k: (j, k)) if tb else pl.BlockSpec((tk, tn), lambda i, j, k: (k, j))
    in_specs = [a_spec, b_spec]
    args = [a, b]
    if residual is not None:
        in_specs.append(pl.BlockSpec((tm, tn), lambda i, j, k: (i, j)))
        args.append(residual)
    return pl.pallas_call(
        body, name=name, grid=(M // tm, N // tn, nk), in_specs=in_specs,
        out_specs=pl.BlockSpec((tm, tn), lambda i, j, k: (i, j)),
        out_shape=jax.ShapeDtypeStruct((M, N), out_dtype),
        scratch_shapes=[pltpu.VMEM((tm, tn), F32)],
        compiler_params=_params("parallel", "parallel", "arbitrary"),
    )(*args)


def _row_tile(T, C):
    tr = max(SUBLANES, min(T, ROW_BLOCK_ELEMS // C) // SUBLANES * SUBLANES)
    while T % tr:
        tr -= SUBLANES
    return tr


def _rows_call(body, name, T, C, row_ins, full_ins, row_outs, acc_outs=()):
    tr = _row_tile(T, C)
    row_spec = pl.BlockSpec((tr, C), lambda i: (i, 0))
    in_specs = [row_spec] * len(row_ins) + [pl.BlockSpec(f.shape, lambda i, n=f.ndim: (0,) * n) for f in full_ins]
    out_specs = [row_spec] * len(row_outs) + [pl.BlockSpec(s, lambda i, n=len(s): (0,) * n) for s in acc_outs]
    out_shape = [jax.ShapeDtypeStruct((T, C), d) for d in row_outs] + [jax.ShapeDtypeStruct(s, F32) for s in acc_outs]
    return pl.pallas_call(
        body, name=name, grid=(T // tr,), in_specs=in_specs, out_specs=out_specs, out_shape=out_shape,
        compiler_params=_params("arbitrary" if acc_outs else "parallel"),
    )(*row_ins, *full_ins)


def _rmsnorm_fwd(x, g, name):
    T, C = x.shape

    def body(x_ref, g_ref, h_ref):
        xv = x_ref[...]
        r = lax.rsqrt(jnp.mean(xv * xv, axis=-1, keepdims=True) + RMS_EPS)
        h_ref[...] = ((xv * r) * g_ref[...]).astype(BF16)

    return _rows_call(body, name, T, C, [x], [g.reshape(1, C)], [BF16])[0]


def _rmsnorm_bwd(x, g, dh, dres, name):
    T, C = x.shape

    def body(x_ref, dh_ref, dres_ref, g_ref, dx_ref, dxb_ref, dg_ref):
        @pl.when(pl.program_id(0) == 0)
        def _():
            dg_ref[...] = jnp.zeros_like(dg_ref)

        xv = x_ref[...]
        dhv = dh_ref[...]
        r = lax.rsqrt(jnp.mean(xv * xv, axis=-1, keepdims=True) + RMS_EPS)
        xn = xv * r
        dg_ref[...] += jnp.sum(dhv * xn, axis=0, keepdims=True)
        dxn = dhv * g_ref[...]
        dx = dres_ref[...] + r * (dxn - xn * jnp.mean(dxn * xn, axis=-1, keepdims=True))
        dx_ref[...] = dx
        dxb_ref[...] = dx.astype(BF16)

    return _rows_call(body, name, T, C, [x, dh, dres], [g.reshape(1, C)], [F32, BF16], [(1, C)])


def _qknorm_fwd(q, g, name):
    T, C = q.shape
    tr = min(T, 1024)

    def body(q_ref, g_ref, o_ref):
        xv = q_ref[...]
        r = lax.rsqrt(jnp.mean(xv * xv, axis=-1, keepdims=True) + RMS_EPS)
        o_ref[...] = ((xv * r) * g_ref[...]).astype(BF16)

    spec = pl.BlockSpec((tr, HEAD_DIM), lambda i, h: (i, h))
    return pl.pallas_call(
        body, name=name, grid=(T // tr, C // HEAD_DIM),
        in_specs=[spec, pl.BlockSpec((1, HEAD_DIM), lambda i, h: (0, 0))], out_specs=spec,
        out_shape=jax.ShapeDtypeStruct((T, C), BF16), compiler_params=_params("parallel", "parallel"),
    )(q, g.reshape(1, HEAD_DIM))


def _qknorm_bwd(q, g, dqn, name):
    T, C = q.shape
    tr = min(T, 1024)

    def body(q_ref, d_ref, g_ref, dq_ref, dg_ref):
        @pl.when((pl.program_id(0) == 0) & (pl.program_id(1) == 0))
        def _():
            dg_ref[...] = jnp.zeros_like(dg_ref)

        xv = q_ref[...]
        dv = d_ref[...]
        r = lax.rsqrt(jnp.mean(xv * xv, axis=-1, keepdims=True) + RMS_EPS)
        xn = xv * r
        dg_ref[...] += jnp.sum(dv * xn, axis=0, keepdims=True)
        dxn = dv * g_ref[...]
        dq_ref[...] = (r * (dxn - xn * jnp.mean(dxn * xn, axis=-1, keepdims=True))).astype(BF16)

    spec = pl.BlockSpec((tr, HEAD_DIM), lambda i, h: (i, h))
    gspec = pl.BlockSpec((1, HEAD_DIM), lambda i, h: (0, 0))
    return pl.pallas_call(
        body, name=name, grid=(T // tr, C // HEAD_DIM), in_specs=[spec, spec, gspec], out_specs=[spec, gspec],
        out_shape=[jax.ShapeDtypeStruct((T, C), BF16), jax.ShapeDtypeStruct((1, HEAD_DIM), F32)],
        compiler_params=_params("arbitrary", "arbitrary"),
    )(q, dqn, g.reshape(1, HEAD_DIM))


ATT_TQ = 512
ATT_TK = 256


def _logsig_pair(z):
    sp = jnp.log(1.0 + jnp.exp(-jnp.abs(z)))
    return jnp.minimum(z, 0.0) - sp, jnp.minimum(-z, 0.0) - sp


def _tri(n, strict_upper_src):
    j = lax.broadcasted_iota(jnp.int32, (n, n), 0)
    s = lax.broadcasted_iota(jnp.int32, (n, n), 1)
    if strict_upper_src == "gt":
        m = j > s
    elif strict_upper_src == "le":
        m = j <= s
    else:
        m = j < s
    return jnp.where(m, 1.0, 0.0).astype(BF16)


def _cumdot(x, tri):
    hi, lo = _split(x)
    return _dot(hi, tri) + _dot(lo, tri)


def _attn_tiles(S):
    tq, tk = min(ATT_TQ, S), min(ATT_TK, S)
    return tq, tk, S // tq, tq // tk


def _attn_fwd(qn, kn, v, Bl, S):
    T, C = qn.shape
    H = C // HEAD_DIM
    tq, tk, nq, kpq = _attn_tiles(S)
    scale = 1.0 / math.sqrt(HEAD_DIM)

    def body(q_ref, k_ref, v_ref, o_ref, bt_ref):
        i = pl.program_id(2)
        tri = _tri(tk, "gt")
        rowpos = lax.broadcasted_iota(jnp.int32, (tq, tk), 0) + i * tq
        colpos = lax.broadcasted_iota(jnp.int32, (tq, tk), 1)
        o_ref[...] = jnp.zeros_like(o_ref)
        bt_ref[...] = jnp.zeros_like(bt_ref)
        nkb = (i + 1) * kpq

        def step(n, carry):
            j = nkb - 1 - n
            rows = pl.ds(pl.multiple_of(j * tk, tk), tk)
            z = _dot(q_ref[...], k_ref[rows, :], NT) * scale
            mask = (colpos + j * tk) < rowpos
            a, b = _logsig_pair(z)
            b = jnp.where(mask, b, 0.0)
            suffix = _cumdot(b, tri)
            w = jnp.where(mask, jnp.exp(a + suffix + bt_ref[...]), 0.0)
            o_ref[...] += _dot(w.astype(BF16), v_ref[rows, :])
            bt_ref[...] += jnp.sum(b, axis=-1, keepdims=True)
            return carry

        lax.fori_loop(0, nkb, step, 0)

    qspec = pl.BlockSpec((tq, HEAD_DIM), lambda b, h, i: (b * nq + i, h))
    kspec = pl.BlockSpec((S, HEAD_DIM), lambda b, h, i: (b, h))
    btspec = pl.BlockSpec((None, None, tq, 1), lambda b, h, i: (b, h, i, 0))
    return pl.pallas_call(
        body, name="attn_fwd", grid=(Bl, H, nq), in_specs=[qspec, kspec, kspec], out_specs=[qspec, btspec],
        out_shape=[jax.ShapeDtypeStruct((T, C), F32), jax.ShapeDtypeStruct((Bl, H, S, 1), F32)],
        compiler_params=_params("parallel", "parallel", "arbitrary"),
    )(qn, kn, v)


def _attn_bwd(qn, kn, v, do, btot, Bl, S):
    T, C = qn.shape
    H = C // HEAD_DIM
    tq, tk, nq, kpq = _attn_tiles(S)
    scale = 1.0 / math.sqrt(HEAD_DIM)

    def body(q_ref, k_ref, v_ref, do_ref, bt_ref, dq_ref, dk_ref, dv_ref, pb_ref, pdl_ref):
        i = pl.program_id(2)

        @pl.when(i == 0)
        def _():
            dk_ref[...] = jnp.zeros_like(dk_ref)
            dv_ref[...] = jnp.zeros_like(dv_ref)

        tri_le = _tri(tk, "le")
        tri_lt = _tri(tk, "lt")
        rowpos = lax.broadcasted_iota(jnp.int32, (tq, tk), 0) + i * tq
        colpos = lax.broadcasted_iota(jnp.int32, (tq, tk), 1)
        dq_ref[...] = jnp.zeros_like(dq_ref)
        pb_ref[...] = bt_ref[...]
        pdl_ref[...] = jnp.zeros_like(pdl_ref)

        def step(j, carry):
            rows = pl.ds(pl.multiple_of(j * tk, tk), tk)
            kj = k_ref[rows, :]
            z = _dot(q_ref[...], kj, NT) * scale
            mask = (colpos + j * tk) < rowpos
            a, b = _logsig_pair(z)
            b = jnp.where(mask, b, 0.0)
            suffix = pb_ref[...] - _cumdot(b, tri_le)
            w = jnp.where(mask, jnp.exp(a + suffix), 0.0)
            dl = _dot(do_ref[...], v_ref[rows, :], NT) * w
            prefix = pdl_ref[...] + _cumdot(dl, tri_lt)
            beta = jnp.exp(a)
            dzb = (jnp.where(mask, dl * (1.0 - beta) - beta * prefix, 0.0) * scale).astype(BF16)
            dq_ref[...] += _dot(dzb, kj)
            dk_ref[rows, :] += _dot(dzb, q_ref[...], TN)
            dv_ref[rows, :] += _dot(w.astype(BF16), do_ref[...], TN)
            pb_ref[...] -= jnp.sum(b, axis=-1, keepdims=True)
            pdl_ref[...] += jnp.sum(dl, axis=-1, keepdims=True)
            return carry

        lax.fori_loop(0, (i + 1) * kpq, step, 0)

    qspec = pl.BlockSpec((tq, HEAD_DIM), lambda b, h, i: (b * nq + i, h))
    kspec = pl.BlockSpec((S, HEAD_DIM), lambda b, h, i: (b, h))
    btspec = pl.BlockSpec((None, None, tq, 1), lambda b, h, i: (b, h, i, 0))
    return pl.pallas_call(
        body, name="attn_bwd", grid=(Bl, H, nq), in_specs=[qspec, kspec, kspec, qspec, btspec],
        out_specs=[qspec, kspec, kspec],
        out_shape=[jax.ShapeDtypeStruct((T, C), F32)] * 3,
        scratch_shapes=[pltpu.VMEM((tq, 1), F32), pltpu.VMEM((tq, 1), F32)],
        compiler_params=_params("parallel", "parallel", "arbitrary"),
    )(qn, kn, v, do, btot)


SSM_TIME_BLOCK = 512
CHUNK = SUBLANES


def _cmadd(xr, xi, ar, ai, sr, si):
    return xr + ar * sr - ai * si, xi + ar * si + ai * sr


def _chunk_scan(xr, xi, tab_ref, cr, ci, reverse):
    for lvl, d in enumerate((1, 2, 4)):
        shift = (CHUNK - d) if reverse else d
        sr = pltpu.roll(xr, shift, 0)
        si = pltpu.roll(xi, shift, 0)
        ar = tab_ref[pl.ds((2 * lvl) * CHUNK, CHUNK), :]
        ai = tab_ref[pl.ds((2 * lvl + 1) * CHUNK, CHUNK), :]
        xr, xi = _cmadd(xr, xi, ar, ai, sr, si)
    pr = tab_ref[pl.ds(6 * CHUNK, CHUNK), :]
    pi = tab_ref[pl.ds(7 * CHUNK, CHUNK), :]
    return _cmadd(xr, xi, pr, pi, cr, ci)


def _ssm_dims(S, C):
    G = C // GROUP
    GT = min(16, G)
    return G, GT, G // GT, GT * GROUP, GT * STATE, min(SSM_TIME_BLOCK, S)


def _ssm_fwd(u, mats, Bl, S):
    T, C = u.shape
    G, GT, ngt, cw, sw, TB = _ssm_dims(S, C)
    ntb = S // TB
    nch = TB // CHUNK

    def body(u_ref, bre_ref, bim_ref, cre_ref, cim_ref, d_ref, tab_ref, y_ref, hr_ref, hi_ref, car_r, car_i):
        @pl.when(pl.program_id(2) == 0)
        def _():
            car_r[...] = jnp.zeros_like(car_r)
            car_i[...] = jnp.zeros_like(car_i)

        uv = u_ref[...]
        hr_ref[...] = _dot3(uv, bre_ref[...])
        hi_ref[...] = _dot3(uv, bim_ref[...])

        def step(n, carry):
            cr, ci = carry
            rows = pl.ds(pl.multiple_of(n * CHUNK, CHUNK), CHUNK)
            xr, xi = _chunk_scan(hr_ref[rows, :], hi_ref[rows, :], tab_ref, cr, ci, False)
            hr_ref[rows, :] = xr
            hi_ref[rows, :] = xi
            last = (CHUNK - 1, CHUNK)
            return (jnp.broadcast_to(xr[last[0]:last[1], :], xr.shape), jnp.broadcast_to(xi[last[0]:last[1], :], xi.shape))

        cr, ci = lax.fori_loop(0, nch, step, (car_r[...], car_i[...]))
        car_r[...] = cr
        car_i[...] = ci
        y_ref[...] = _dot3(hr_ref[...], cre_ref[...]) - _dot3(hi_ref[...], cim_ref[...]) + d_ref[...] * uv

    uspec = pl.BlockSpec((TB, cw), lambda g, b, t: (b * ntb + t, g))
    hspec = pl.BlockSpec((TB, sw), lambda g, b, t: (b * ntb + t, g))

    def gspec(r, c):
        return pl.BlockSpec((None, r, c), lambda g, b, t: (g, 0, 0))

    return pl.pallas_call(
        body, name="ssm_fwd", grid=(ngt, Bl, ntb),
        in_specs=[uspec, gspec(cw, sw), gspec(cw, sw), gspec(sw, cw), gspec(sw, cw), gspec(1, cw), gspec(8 * CHUNK, sw)],
        out_specs=[uspec, hspec, hspec],
        out_shape=[jax.ShapeDtypeStruct((T, C), F32), jax.ShapeDtypeStruct((T, G * STATE), F32),
                   jax.ShapeDtypeStruct((T, G * STATE), F32)],
        scratch_shapes=[pltpu.VMEM((CHUNK, sw), F32), pltpu.VMEM((CHUNK, sw), F32)],
        compiler_params=_params("parallel", "arbitrary", "arbitrary"),
    )(u, mats["bbd_re"], mats["bbd_im"], mats["cbd_re"], mats["cbd_im"], mats["d"], mats["tab_fwd"])


def _ssm_bwd(u, dy, h_re, h_im, mats, Bl, S):
    T, C = u.shape
    G, GT, ngt, cw, sw, TB = _ssm_dims(S, C)
    ntb = S // TB
    nch = TB // CHUNK
    rpb = TB // CHUNK

    def body(u_ref, dy_ref, hr_ref, hi_ref, hpr_ref, hpi_ref, ctr_ref, cti_ref, btr_ref, bti_ref, d_ref, tab_ref,
             du_ref, dbr_ref, dbi_ref, dcr_ref, dci_ref, dlr_ref, dli_ref, dd_ref, gr_ref, gi_ref, car_r, car_i):
        b = pl.program_id(1)
        t = pl.program_id(2)

        @pl.when((b == 0) & (t == 0))
        def _():
            for ref in (dbr_ref, dbi_ref, dcr_ref, dci_ref, dlr_ref, dli_ref, dd_ref):
                ref[...] = jnp.zeros_like(ref)

        @pl.when(t == 0)
        def _():
            car_r[...] = jnp.zeros_like(car_r)
            car_i[...] = jnp.zeros_like(car_i)

        uv = u_ref[...]
        dyv = dy_ref[...]
        gr_ref[...] = _dot3(dyv, ctr_ref[...])
        gi_ref[...] = -_dot3(dyv, cti_ref[...])
        alive = jnp.where(t == ntb - 1, 0.0, 1.0)
        row0 = lax.broadcasted_iota(jnp.int32, (CHUNK, sw), 0) == 0

        def step(m, carry):
            cr, ci, ar, ai = carry
            n = nch - 1 - m
            rows = pl.ds(pl.multiple_of(n * CHUNK, CHUNK), CHUNK)
            prow = pl.ds(pl.multiple_of(jnp.maximum(n - 1, 0) * CHUNK, CHUNK), CHUNK)
            xr, xi = _chunk_scan(gr_ref[rows, :], gi_ref[rows, :], tab_ref, cr, ci, True)
            gr_ref[rows, :] = xr
            gi_ref[rows, :] = xi
            first = n == 0
            pr = jnp.where(first, hpr_ref[...] * alive, hr_ref[prow, :])
            pi = jnp.where(first, hpi_ref[...] * alive, hi_ref[prow, :])
            sr = jnp.where(row0, pltpu.roll(pr, 1, 0), pltpu.roll(hr_ref[rows, :], 1, 0))
            si = jnp.where(row0, pltpu.roll(pi, 1, 0), pltpu.roll(hi_ref[rows, :], 1, 0))
            ar = ar + xr * sr + xi * si
            ai = ai + xi * sr - xr * si
            return (jnp.broadcast_to(xr[0:1, :], xr.shape), jnp.broadcast_to(xi[0:1, :], xi.shape), ar, ai)

        zero = jnp.zeros((CHUNK, sw), F32)
        cr, ci, ar, ai = lax.fori_loop(0, nch, step, (car_r[...], car_i[...], zero, zero))
        car_r[...] = cr
        car_i[...] = ci
        dlr_ref[...] += ar
        dli_ref[...] += ai
        gr = gr_ref[...]
        gi = gi_ref[...]
        dbr_ref[...] += _dot3(uv, gr, TN)
        dbi_ref[...] += _dot3(uv, gi, TN)
        dcr_ref[...] += _dot3(hr_ref[...], dyv, TN)
        dci_ref[...] -= _dot3(hi_ref[...], dyv, TN)
        dd_ref[...] += jnp.sum(dyv * uv, axis=0, keepdims=True)
        du_ref[...] = _dot3(gr, btr_ref[...]) + _dot3(gi, bti_ref[...]) + d_ref[...] * dyv

    def tblk(b, t):
        return b * ntb + (ntb - 1 - t)

    uspec = pl.BlockSpec((TB, cw), lambda g, b, t: (tblk(b, t), g))
    hspec = pl.BlockSpec((TB, sw), lambda g, b, t: (tblk(b, t), g))
    hpspec = pl.BlockSpec((CHUNK, sw), lambda g, b, t: (jnp.maximum(tblk(b, t) * rpb - 1, 0), g))

    def gspec(r, c):
        return pl.BlockSpec((None, r, c), lambda g, b, t: (g, 0, 0))

    def gshape(r, c):
        return jax.ShapeDtypeStruct((ngt, r, c), F32)

    return pl.pallas_call(
        body, name="ssm_bwd", grid=(ngt, Bl, ntb),
        in_specs=[uspec, uspec, hspec, hspec, hpspec, hpspec, gspec(cw, sw), gspec(cw, sw), gspec(sw, cw), gspec(sw, cw),
                  gspec(1, cw), gspec(8 * CHUNK, sw)],
        out_specs=[uspec, gspec(cw, sw), gspec(cw, sw), gspec(sw, cw), gspec(sw, cw), gspec(CHUNK, sw), gspec(CHUNK, sw),
                   gspec(1, cw)],
        out_shape=[jax.ShapeDtypeStruct((T, C), F32), gshape(cw, sw), gshape(cw, sw), gshape(sw, cw), gshape(sw, cw),
                   gshape(CHUNK, sw), gshape(CHUNK, sw), gshape(1, cw)],
        scratch_shapes=[pltpu.VMEM((TB, sw), F32), pltpu.VMEM((TB, sw), F32), pltpu.VMEM((CHUNK, sw), F32),
                        pltpu.VMEM((CHUNK, sw), F32)],
        compiler_params=_params("arbitrary", "arbitrary", "arbitrary"),
    )(u, dy, h_re, h_im, h_re, h_im, mats["cbdT_re"], mats["cbdT_im"], mats["bbdT_re"], mats["bbdT_im"], mats["d"],
      mats["tab_rev"])


def _zoh(a_re, a_im, log_dt, b_re, b_im):
    dt = jnp.exp(log_dt)[:, None]
    mag = jnp.exp(a_re * dt)
    l_re = mag * jnp.cos(a_im * dt)
    l_im = mag * jnp.sin(a_im * dt)
    den = a_re * a_re + a_im * a_im
    f_re = ((l_re - 1.0) * a_re + l_im * a_im) / den
    f_im = (l_im * a_re - (l_re - 1.0) * a_im) / den
    bb_re = f_re[..., None] * b_re - f_im[..., None] * b_im
    bb_im = f_re[..., None] * b_im + f_im[..., None] * b_re
    return l_re, l_im, bb_re, bb_im


def _ssm_matrices(a_re, a_im, log_dt, b_re, b_im, c_re, c_im, d, S):
    G = a_re.shape[0]
    _, GT, ngt, cw, sw, _ = _ssm_dims(S, G * GROUP)
    _, _, bb_re, bb_im = _zoh(a_re, a_im, log_dt, b_re, b_im)
    eye = jnp.eye(GT, dtype=F32)

    def bd_b(bb):
        return jnp.einsum("tgpi,gh->tgihp", bb.reshape(ngt, GT, STATE, GROUP), eye).reshape(ngt, cw, sw)

    def bd_c(c):
        return jnp.einsum("tgip,gh->tgphi", c.reshape(ngt, GT, GROUP, STATE), eye).reshape(ngt, sw, cw)

    dt = jnp.exp(log_dt)[:, None]

    def power(k, conj):
        mag = jnp.exp(k * a_re * dt)
        ang = k * a_im * dt
        return (mag * jnp.cos(ang)).reshape(ngt, 1, sw), ((-1.0 if conj else 1.0) * mag * jnp.sin(ang)).reshape(ngt, 1, sw)

    r = jnp.arange(CHUNK)[None, :, None]

    def table(reverse):
        parts = []
        for dd in (1, 2, 4):
            pr, pi = power(float(dd), reverse)
            keep = (r <= CHUNK - 1 - dd) if reverse else (r >= dd)
            parts += [jnp.where(keep, pr, 0.0), jnp.where(keep, pi, 0.0)]
        exps = [(CHUNK - k) if reverse else (k + 1) for k in range(CHUNK)]
        pw = [power(float(e), reverse) for e in exps]
        parts += [jnp.concatenate([p[0] for p in pw], axis=1), jnp.concatenate([p[1] for p in pw], axis=1)]
        return jnp.concatenate([jnp.broadcast_to(p, (ngt, CHUNK, sw)) for p in parts], axis=1)

    mats = dict(bbd_re=bd_b(bb_re), bbd_im=bd_b(bb_im), cbd_re=bd_c(c_re), cbd_im=bd_c(c_im),
                d=d.reshape(ngt, 1, cw), tab_fwd=table(False), tab_rev=table(True))
    for k in ("bbd_re", "bbd_im", "cbd_re", "cbd_im"):
        mats[k.replace("bd_", "bdT_")] = jnp.swapaxes(mats[k], 1, 2)
    return mats


def _ssm_unblock(dbr, dbi, dcr, dci, dlr, dli, dd, G):
    ngt = dbr.shape[0]
    GT = G // ngt
    eye = jnp.eye(GT, dtype=F32)

    def ub(x):
        return jnp.einsum("tgihp,gh->tgpi", x.reshape(ngt, GT, GROUP, GT, STATE), eye).reshape(G, STATE, GROUP)

    def uc(x):
        return jnp.einsum("tgphi,gh->tgip", x.reshape(ngt, GT, STATE, GT, GROUP), eye).reshape(G, GROUP, STATE)

    return (dlr.sum(axis=1).reshape(G, STATE), dli.sum(axis=1).reshape(G, STATE), ub(dbr), ub(dbi), uc(dcr), uc(dci),
            dd.reshape(G * GROUP))


def _attn_gate_fwd(o, gate):
    T, C = o.shape

    def body(o_ref, g_ref, out_ref):
        out_ref[...] = (o_ref[...] * _silu_parts(g_ref[...])[0]).astype(BF16)

    return _rows_call(body, "attn_gate_fwd", T, C, [o, gate], [], [BF16])[0]


def _attn_gate_bwd(dog, o, gate):
    T, C = o.shape

    def body(d_ref, o_ref, g_ref, do_ref, dg_ref):
        val, der = _silu_parts(g_ref[...])
        dv = d_ref[...]
        do_ref[...] = (dv * val).astype(BF16)
        dg_ref[...] = (dv * o_ref[...] * der).astype(BF16)

    return _rows_call(body, "attn_gate_bwd", T, C, [dog, o, gate], [], [BF16, BF16])


def _gelu_fwd(y):
    T, C = y.shape

    def body(y_ref, out_ref):
        out_ref[...] = _gelu_parts(y_ref[...])[0].astype(BF16)

    return _rows_call(body, "gelu_fwd", T, C, [y], [], [BF16])[0]


def _glu_fwd(y, gl, gate, glu_b):
    T, C = y.shape

    def body(y_ref, gl_ref, g_ref, b_ref, out_ref):
        yg = _gelu_parts(y_ref[...])[0]
        sg = _sigmoid(gl_ref[...] + b_ref[...])
        out_ref[...] = (yg * sg * _silu_parts(g_ref[...])[0]).astype(BF16)

    return _rows_call(body, "glu_fwd", T, C, [y, gl, gate], [glu_b.reshape(1, C)], [BF16])[0]


def _glu_bwd(dy3, y, gl, gate, glu_b):
    T, C = y.shape

    def body(d_ref, y_ref, gl_ref, g_ref, b_ref, dgl_ref, dgate_ref, t1_ref, db_ref):
        @pl.when(pl.program_id(0) == 0)
        def _():
            db_ref[...] = jnp.zeros_like(db_ref)

        yg = _gelu_parts(y_ref[...])[0]
        sg = _sigmoid(gl_ref[...] + b_ref[...])
        sl, sld = _silu_parts(g_ref[...])
        dv = d_ref[...]
        dy2 = dv * sl
        dgl = dy2 * yg * sg * (1.0 - sg)
        dgl_ref[...] = dgl.astype(BF16)
        dgate_ref[...] = (dv * (yg * sg) * sld).astype(BF16)
        t1_ref[...] = dy2 * sg
        db_ref[...] += jnp.sum(dgl, axis=0, keepdims=True)

    return _rows_call(body, "glu_bwd", T, C, [dy3, y, gl, gate], [glu_b.reshape(1, C)], [BF16, BF16, F32], [(1, C)])


def _gelu_bwd(t1, t2, y):
    T, C = y.shape

    def body(a_ref, b_ref, y_ref, out_ref):
        out_ref[...] = (a_ref[...] + b_ref[...]) * _gelu_parts(y_ref[...])[1]

    return _rows_call(body, "gelu_bwd", T, C, [t1, t2, y], [], [F32])[0]


def _loss_head(x2, target):
    T, C = x2.shape

    def body(x_ref, t_ref, d_ref, db_ref, l_ref):
        @pl.when(pl.program_id(0) == 0)
        def _():
            l_ref[...] = jnp.zeros_like(l_ref)

        e = x_ref[...] - t_ref[...]
        d = e * (1.0 / C)
        d_ref[...] = d
        db_ref[...] = d.astype(BF16)
        l_ref[...] += 0.5 * jnp.sum(jnp.sum(e * e, axis=-1, keepdims=True) * (1.0 / C), axis=0, keepdims=True)

    return _rows_call(body, "loss_head", T, C, [x2, target], [], [F32, BF16], [(1, 1)])


def _adamw(w, g, m, v, name):
    shape = w.shape
    C = shape[-1]
    R = w.size // C
    bc1 = 1.0 - ADAM_B1 ** ADAM_STEP
    bc2 = 1.0 - ADAM_B2 ** ADAM_STEP

    def body(w_ref, g_ref, m_ref, v_ref, d_ref, nm_ref, nv_ref):
        gv = g_ref[...]
        mn = ADAM_B1 * m_ref[...] + (1.0 - ADAM_B1) * gv
        vn = ADAM_B2 * v_ref[...] + (1.0 - ADAM_B2) * (gv * gv)
        d_ref[...] = -ADAM_LR * ((mn / bc1) / (jnp.sqrt(vn / bc2) + ADAM_EPS) + ADAM_WD * w_ref[...])
        nm_ref[...] = mn
        nv_ref[...] = vn

    outs = _rows_call(body, name, R, C, [a.reshape(R, C) for a in (w, g, m, v)], [], [F32, F32, F32])
    return [o.reshape(shape) for o in outs]


def _sum_leading(x, name):
    n, R, C = x.shape
    tr = _row_tile(R, C * n)

    def body(x_ref, o_ref):
        acc = x_ref[0].astype(F32)
        for k in range(1, n):
            acc = acc + x_ref[k].astype(F32)
        o_ref[...] = acc

    return pl.pallas_call(
        body, name=name, grid=(R // tr,), in_specs=[pl.BlockSpec((n, tr, C), lambda i: (0, i, 0))],
        out_specs=pl.BlockSpec((tr, C), lambda i: (i, 0)), out_shape=jax.ShapeDtypeStruct((R, C), F32),
        compiler_params=_params("parallel"),
    )(x)


def _add_halves(g, c, name):
    full, recv = g
    n, R, C = full.shape
    half = R // 2
    tr = _row_tile(half, C)
    nb = half // tr

    def body(c_ref, a_ref, b_ref, o_ref):
        o_ref[...] = (a_ref[...] + b_ref[...]).astype(BF16)

    grid_spec = pltpu.PrefetchScalarGridSpec(
        num_scalar_prefetch=1, grid=(n, nb),
        in_specs=[pl.BlockSpec((None, tr, C), lambda j, i, c_ref: (j, c_ref[0] * nb + i, 0)),
                  pl.BlockSpec((None, tr, C), lambda j, i, c_ref: (j, i, 0))],
        out_specs=pl.BlockSpec((None, tr, C), lambda j, i, c_ref: (j, i, 0)))
    return pl.pallas_call(
        body, name=name, grid_spec=grid_spec, out_shape=jax.ShapeDtypeStruct((n, half, C), BF16),
        compiler_params=_params("parallel", "parallel"),
    )(c.reshape(1).astype(jnp.int32), full, recv)


ANY = pl.BlockSpec(memory_space=pl.ANY)


def _position():
    return lax.axis_index("x"), lax.axis_index("y"), lax.axis_index("c")


def _all_gather8(blk, name):
    M, N = blk.shape

    def body(x_ref, out_ref, send_sems, recv_sems, local_sem):
        x, y, c = _position()
        me, sibling = (x, y, c), (x, y, 1 - c)
        chips = [(1 - x, y), (x, 1 - y), (1 - x, 1 - y)]

        def slab(px, py, pc):
            return out_ref.at[4 * px + 2 * py + pc]

        def copy(k, block, to, src=None):
            return pltpu.make_async_remote_copy(
                src_ref=slab(*block) if src is None else src, dst_ref=slab(*block),
                send_sem=send_sems.at[k], recv_sem=recv_sems.at[k], device_id=to, device_id_type=MESH)

        mine = pltpu.make_async_copy(x_ref, slab(*me), local_sem)
        mine.start()
        first = [copy(0, me, sibling, src=x_ref)]
        first += [copy(1 + j, me, (*chip, c), src=x_ref) for j, chip in enumerate(chips)]
        for cp in first:
            cp.start()
        passed = [copy(4 + j, (*chip, c), sibling) for j, chip in enumerate(chips)]
        for j, chip in enumerate(chips):
            copy(1 + j, (*chip, c), me).wait_recv()
            passed[j].start()
        copy(0, sibling, me).wait_recv()
        for j, chip in enumerate(chips):
            copy(4 + j, (*chip, 1 - c), me).wait_recv()
        for cp in first + passed:
            cp.wait_send()
        mine.wait()

    return pl.pallas_call(
        body, name=name, in_specs=[ANY], out_specs=ANY, out_shape=jax.ShapeDtypeStruct((N_DEV, M, N), blk.dtype),
        scratch_shapes=[pltpu.SemaphoreType.DMA((7,)), pltpu.SemaphoreType.DMA((7,)), pltpu.SemaphoreType.DMA],
    )(blk)


def _sibling_send_half(g, name):
    n, R, C = g.shape
    half = R // 2

    def body(g_ref, out_ref, send_sem, recv_sem):
        x, y, c = _position()
        src = g_ref.at[:, pl.ds((1 - c) * half, half), :]
        cp = pltpu.make_async_remote_copy(src_ref=src, dst_ref=out_ref, send_sem=send_sem, recv_sem=recv_sem,
                                          device_id=(x, y, 1 - c), device_id_type=MESH)
        cp.start()
        cp.wait()

    return pl.pallas_call(
        body, name=name, in_specs=[ANY], out_specs=ANY, out_shape=jax.ShapeDtypeStruct((n, half, C), g.dtype),
        scratch_shapes=[pltpu.SemaphoreType.DMA, pltpu.SemaphoreType.DMA],
    )(g)


def _chip_exchange(p, name):
    n, M, C = p.shape

    def body(p_ref, out_ref, send_sems, recv_sems, local_sem):
        x, y, c = _position()
        my = 2 * x + y
        chips = [(1 - x, y), (x, 1 - y), (1 - x, 1 - y)]
        mine = pltpu.make_async_copy(p_ref.at[my], out_ref.at[my], local_sem)
        mine.start()
        copies = []
        for k, (px, py) in enumerate(chips):
            cp = pltpu.make_async_remote_copy(
                src_ref=p_ref.at[2 * px + py], dst_ref=out_ref.at[my], send_sem=send_sems.at[k], recv_sem=recv_sems.at[k],
                device_id=(px, py, c), device_id_type=MESH)
            cp.start()
            copies.append(cp)
        for k, (px, py) in enumerate(chips):
            pltpu.make_async_remote_copy(
                src_ref=p_ref.at[my], dst_ref=out_ref.at[2 * px + py], send_sem=send_sems.at[k], recv_sem=recv_sems.at[k],
                device_id=(px, py, c), device_id_type=MESH).wait_recv()
        for cp in copies:
            cp.wait_send()
        mine.wait()

    return pl.pallas_call(
        body, name=name, in_specs=[ANY], out_specs=ANY, out_shape=jax.ShapeDtypeStruct((n, M, C), p.dtype),
        scratch_shapes=[pltpu.SemaphoreType.DMA((3,)), pltpu.SemaphoreType.DMA((3,)), pltpu.SemaphoreType.DMA],
    )(p)


def _sibling_swap(f, name):
    def body(f_ref, out_ref, send_sem, recv_sem):
        x, y, c = _position()
        cp = pltpu.make_async_remote_copy(src_ref=f_ref, dst_ref=out_ref, send_sem=send_sem, recv_sem=recv_sem,
                                          device_id=(x, y, 1 - c), device_id_type=MESH)
        cp.start()
        cp.wait()

    return pl.pallas_call(
        body, name=name, in_specs=[ANY], out_specs=ANY, out_shape=jax.ShapeDtypeStruct(f.shape, f.dtype),
        scratch_shapes=[pltpu.SemaphoreType.DMA, pltpu.SemaphoreType.DMA],
    )(f)


def _pack_dense(attn_in, attn_out, ssm_in, glu_w, ssm_out):
    D = attn_in.shape[0]
    return jnp.concatenate([attn_in, attn_out, ssm_in.reshape(D // 2, D), glu_w, ssm_out], axis=0)


def _unpack_dense(p):
    D = p.shape[-1]
    q = D // 4
    o = [0, D, D + q, D + q + D // 2, D + 2 * q + D // 2, D + 3 * q + D // 2]
    lead = p.shape[:-2]
    return (p[..., o[0]:o[1], :], p[..., o[1]:o[2], :], p[..., o[2]:o[3], :].reshape(*lead, D, D // 2),
            p[..., o[3]:o[4], :], p[..., o[4]:o[5], :])


SMALL_REPL = ("norm_g", "attn_q_g", "attn_k_g")


def _pack_small(parts):
    flat = jnp.concatenate([p.reshape(-1) for p in parts])
    pad = (-flat.size) % (2 * SUBLANES * LANES)
    return jnp.pad(flat, (0, pad)).reshape(-1, LANES)


def _unpack_small(buf, shapes):
    flat = buf.reshape(-1)
    out, off = [], 0
    for s in shapes:
        n = math.prod(s)
        out.append(flat[off:off + n].reshape(s))
        off += n
    return out


def _local_step(x, target, norm_g, q_g, k_g, w_attn_in, w_attn_out, w_ssm_in, w_glu, w_ssm_out, ssm_small):
    Bl, S, D = x.shape
    T = Bl * S
    x0 = x.reshape(T, D)
    tgt = target.reshape(T, D)
    a_re, a_im, log_dt, b_re, b_im, c_re, c_im, d_skip, glu_b = ssm_small
    G = a_re.shape[0]
    mats = _ssm_matrices(a_re, a_im, log_dt, b_re, b_im, c_re, c_im, d_skip, S)

    h0 = _rmsnorm_fwd(x0, norm_g[0], "norm0_fwd")
    q, k, v, gate = [_matmul(h0, w_attn_in[j], name=f"attn_in_{j}", out_dtype=(BF16 if j == 2 else F32)) for j in range(4)]
    qn = _qknorm_fwd(q, q_g, "qnorm_fwd")
    kn = _qknorm_fwd(k, k_g, "knorm_fwd")
    o, btot = _attn_fwd(qn, kn, v, Bl, S)
    og = _attn_gate_fwd(o, gate)
    x1 = _matmul(og, w_attn_out, name="attn_out", residual=x0)

    h1 = _rmsnorm_fwd(x1, norm_g[1], "norm1_fwd")
    u = _matmul(h1, w_ssm_in[:, :D], name="ssm_in_u")
    gate2 = _matmul(h1, w_ssm_in[:, D:], name="ssm_in_gate")
    y, hs_re, hs_im = _ssm_fwd(u, mats, Bl, S)
    yg = _gelu_fwd(y)
    gl = _matmul(yg, w_glu, name="glu_mm")
    y3 = _glu_fwd(y, gl, gate2, glu_b)
    x2 = _matmul(y3, w_ssm_out, name="ssm_out", residual=x1)

    dx2, dx2b, loss = _loss_head(x2, tgt)

    dy3 = _matmul(dx2b, w_ssm_out, name="ssm_out_dgrad", tb=True)
    g_ssm_out = _matmul(y3, dx2b, name="ssm_out_wgrad", ta=True)
    dgl, dgate2, t1, dglu_b = _glu_bwd(dy3, y, gl, gate2, glu_b)
    t2 = _matmul(dgl, w_glu, name="glu_dgrad", tb=True)
    g_glu = _matmul(yg, dgl, name="glu_wgrad", ta=True)
    dy = _gelu_bwd(t1, t2, y)
    du, dbr, dbi, dcr, dci, dlr, dli, dd = _ssm_bwd(u, dy, hs_re, hs_im, mats, Bl, S)
    dh1 = _matmul(du, w_ssm_in[:, :D], name="ssm_in_dgrad_u", tb=True)
    dh1 = _matmul(dgate2, w_ssm_in[:, D:], name="ssm_in_dgrad_gate", tb=True, residual=dh1)
    g_ssm_in = jnp.concatenate([_matmul(h1, du, name="ssm_in_wgrad_u", ta=True),
                                _matmul(h1, dgate2, name="ssm_in_wgrad_gate", ta=True)], axis=1)
    dx1, dx1b, dng1 = _rmsnorm_bwd(x1, norm_g[1], dh1, dx2, "norm1_bwd")

    dog = _matmul(dx1b, w_attn_out, name="attn_out_dgrad", tb=True)
    g_attn_out = _matmul(og, dx1b, name="attn_out_wgrad", ta=True)
    do, dgate = _attn_gate_bwd(dog, o, gate)
    dqn, dkn, dv = _attn_bwd(qn, kn, v, do, btot, Bl, S)
    dq, dqg = _qknorm_bwd(q, q_g, dqn, "qnorm_bwd")
    dk, dkg = _qknorm_bwd(k, k_g, dkn, "knorm_bwd")
    dproj = [dq, dk, dv, dgate]
    dh0 = None
    for j in range(4):
        dh0 = _matmul(dproj[j], w_attn_in[j], name=f"attn_in_dgrad_{j}", tb=True, residual=dh0)
    g_attn_in = jnp.stack([_matmul(h0, dproj[j], name=f"attn_in_wgrad_{j}", ta=True) for j in range(4)])
    dx0, _, dng0 = _rmsnorm_bwd(x0, norm_g[0], dh0, dx1, "norm0_bwd")

    dense = (g_attn_in, g_attn_out, g_ssm_in, g_glu, g_ssm_out)
    small = (jnp.concatenate([dng0, dng1], axis=0), dqg, dkg) + _ssm_unblock(dbr, dbi, dcr, dci, dlr, dli, dd, G) + (
        dglu_b.reshape(D),)
    return loss, dx0.reshape(Bl, S, D), dense, small


def _chip_rows(a, chip, n_per):
    return lax.dynamic_slice_in_dim(a, chip * n_per, n_per, axis=0)


def kernel(x, norm_g, attn_w_in, attn_q_g, attn_k_g, attn_w_out, ssm_w_in, ssm_A_re, ssm_A_im, ssm_log_dt, ssm_B_re, ssm_B_im, ssm_C_re, ssm_C_im, ssm_D, ssm_glu_w, ssm_glu_b, ssm_w_out, loss_target, m_norm_g, m_attn_w_in, m_attn_q_g, m_attn_k_g, m_attn_w_out, m_ssm_w_in, m_ssm_A_re, m_ssm_A_im, m_ssm_log_dt, m_ssm_B_re, m_ssm_B_im, m_ssm_C_re, m_ssm_C_im, m_ssm_D, m_ssm_glu_w, m_ssm_glu_b, m_ssm_w_out, v_norm_g, v_attn_w_in, v_attn_q_g, v_attn_k_g, v_attn_w_out, v_ssm_w_in, v_ssm_A_re, v_ssm_A_im, v_ssm_log_dt, v_ssm_B_re, v_ssm_B_im, v_ssm_C_re, v_ssm_C_im, v_ssm_D, v_ssm_glu_w, v_ssm_glu_b, v_ssm_w_out):
    D = x.shape[-1]
    cx, cy, cc = _position()
    chip = 2 * cx + cy
    G = D // GROUP
    Gl = G // N_CHIPS

    dense_local = _pack_dense(attn_w_in[0], attn_w_out[0], ssm_w_in[0], ssm_glu_w[0], ssm_w_out[0]).astype(BF16)
    R = dense_local.shape[0]
    my_half = lax.dynamic_slice_in_dim(dense_local, cc * (R // 2), R // 2, axis=0)
    dense_all = _all_gather8(my_half, "weights_all_gather").reshape(N_CHIPS, R, D)
    wa_in, wa_out, ws_in, w_glu, ws_out = _unpack_dense(dense_all)
    wa_out = wa_out.reshape(D, D)
    ws_in = jnp.swapaxes(ws_in, 0, 1).reshape(D, 2 * D)
    w_glu = w_glu.reshape(D, D)
    ws_out = ws_out.reshape(D, D)

    ssm_local = [ssm_A_re[0], ssm_A_im[0], ssm_log_dt[0], ssm_B_re[0], ssm_B_im[0], ssm_C_re[0], ssm_C_im[0], ssm_D[0],
                 ssm_glu_b[0]]
    small_local = _pack_small(ssm_local)
    half_rows = small_local.shape[0] // 2
    small_half = lax.dynamic_slice_in_dim(small_local, cc * half_rows, half_rows, axis=0)
    small_all = _all_gather8(small_half, "ssm_params_all_gather").reshape(N_CHIPS, 2 * half_rows, LANES)
    per_chip = [_unpack_small(small_all[j], [p.shape for p in ssm_local]) for j in range(N_CHIPS)]
    ssm_full = [jnp.concatenate([per_chip[j][i] for j in range(N_CHIPS)], axis=0) for i in range(len(ssm_local))]

    loss, grad_x, dense, small = _local_step(x, loss_target, norm_g, attn_q_g[0], attn_k_g[0], wa_in, wa_out, ws_in,
                                              w_glu, ws_out, ssm_full)
    loss = lax.psum(loss[0, 0], ("x", "y", "c"))

    g_attn_in, g_attn_out, g_ssm_in, g_glu, g_ssm_out = dense
    q4 = D // N_CHIPS
    packed = jnp.stack([
        _pack_dense(g_attn_in[j], g_attn_out[j * q4:(j + 1) * q4], g_ssm_in[:, j * (D // 2):(j + 1) * (D // 2)],
                    g_glu[j * q4:(j + 1) * q4], g_ssm_out[j * q4:(j + 1) * q4]) for j in range(N_CHIPS)])
    from_sibling = _sibling_send_half(packed, "grads_sibling_half")
    chip_sum = _add_halves((packed, from_sibling), cc, "grads_add_halves")
    gathered = _chip_exchange(chip_sum, "grads_chip_exchange")
    my_sum = _sum_leading(gathered, "grads_sum_chips")
    other_sum = _sibling_swap(my_sum, "grads_sibling_swap")
    south = cc == 0
    dense_grad = jnp.concatenate([jnp.where(south, my_sum, other_sum), jnp.where(south, other_sum, my_sum)], axis=0)
    gd = _unpack_dense(dense_grad)

    small_shapes = [s.shape for s in small]
    small_sum = _sum_leading(_all_gather8(_pack_small(small), "small_grads_all_gather"), "small_grads_sum")
    (dng, dqg, dkg, dl_re, dl_im, dbb_re, dbb_im, dc_re, dc_im, dd_skip, dglu_b) = _unpack_small(small_sum, small_shapes)
    a_re, a_im, log_dt, b_re, b_im = ssm_local[:5]
    _, zoh_vjp = jax.vjp(_zoh, a_re, a_im, log_dt, b_re, b_im)
    da_re, da_im, dlog_dt, db_re, db_im = zoh_vjp((_chip_rows(dl_re, chip, Gl), _chip_rows(dl_im, chip, Gl),
                                                   _chip_rows(dbb_re, chip, Gl), _chip_rows(dbb_im, chip, Gl)))
    grads = {
        "norm_g": dng, "attn_w_in": gd[0][None], "attn_q_g": dqg, "attn_k_g": dkg, "attn_w_out": gd[1][None],
        "ssm_w_in": gd[2][None], "ssm_A_re": da_re[None], "ssm_A_im": da_im[None], "ssm_log_dt": dlog_dt[None],
        "ssm_B_re": db_re[None], "ssm_B_im": db_im[None], "ssm_C_re": _chip_rows(dc_re, chip, Gl)[None],
        "ssm_C_im": _chip_rows(dc_im, chip, Gl)[None], "ssm_D": _chip_rows(dd_skip, chip, q4)[None],
        "ssm_glu_w": gd[3][None], "ssm_glu_b": _chip_rows(dglu_b, chip, q4)[None], "ssm_w_out": gd[4][None],
    }
    weights = dict(norm_g=norm_g, attn_w_in=attn_w_in, attn_q_g=attn_q_g, attn_k_g=attn_k_g, attn_w_out=attn_w_out,
                   ssm_w_in=ssm_w_in, ssm_A_re=ssm_A_re, ssm_A_im=ssm_A_im, ssm_log_dt=ssm_log_dt, ssm_B_re=ssm_B_re,
                   ssm_B_im=ssm_B_im, ssm_C_re=ssm_C_re, ssm_C_im=ssm_C_im, ssm_D=ssm_D, ssm_glu_w=ssm_glu_w,
                   ssm_glu_b=ssm_glu_b, ssm_w_out=ssm_w_out)
    m = dict(norm_g=m_norm_g, attn_w_in=m_attn_w_in, attn_q_g=m_attn_q_g, attn_k_g=m_attn_k_g, attn_w_out=m_attn_w_out,
             ssm_w_in=m_ssm_w_in, ssm_A_re=m_ssm_A_re, ssm_A_im=m_ssm_A_im, ssm_log_dt=m_ssm_log_dt, ssm_B_re=m_ssm_B_re,
             ssm_B_im=m_ssm_B_im, ssm_C_re=m_ssm_C_re, ssm_C_im=m_ssm_C_im, ssm_D=m_ssm_D, ssm_glu_w=m_ssm_glu_w,
             ssm_glu_b=m_ssm_glu_b, ssm_w_out=m_ssm_w_out)
    v = dict(norm_g=v_norm_g, attn_w_in=v_attn_w_in, attn_q_g=v_attn_q_g, attn_k_g=v_attn_k_g, attn_w_out=v_attn_w_out,
             ssm_w_in=v_ssm_w_in, ssm_A_re=v_ssm_A_re, ssm_A_im=v_ssm_A_im, ssm_log_dt=v_ssm_log_dt, ssm_B_re=v_ssm_B_re,
             ssm_B_im=v_ssm_B_im, ssm_C_re=v_ssm_C_re, ssm_C_im=v_ssm_C_im, ssm_D=v_ssm_D, ssm_glu_w=v_ssm_glu_w,
             ssm_glu_b=v_ssm_glu_b, ssm_w_out=v_ssm_w_out)
    names = list(weights)
    dense_names = ("attn_w_in", "attn_w_out", "ssm_w_in", "ssm_glu_w", "ssm_w_out")
    delta, new_m, new_v = {}, {}, {}
    for n in dense_names:
        delta[n], new_m[n], new_v[n] = _adamw(weights[n], grads[n], m[n], v[n], "adamw_" + n)
    small_names = [n for n in names if n not in dense_names]
    small_shapes = [weights[n].shape for n in small_names]
    packs = [_pack_small([d[n] for n in small_names]) for d in (weights, grads, m, v)]
    outs = _adamw(*packs, "adamw_small")
    for res, out in zip((delta, new_m, new_v), outs):
        for n, val in zip(small_names, _unpack_small(out, small_shapes)):
            res[n] = val
    return (loss, grad_x, *[grads[n] for n in names], *[delta[n] for n in names], *[new_m[n] for n in names],
            *[new_v[n] for n in names])
```

```python
import functools
import math

import jax
import jax.numpy as jnp
from jax import lax
from jax.experimental import pallas as pl
from jax.experimental.pallas import tpu as pltpu

F32 = jnp.float32
BF16 = jnp.bfloat16

HEAD_DIM = 128
GROUP = 16
STATE = 64
RMS_EPS = 1e-6
ADAM_LR = 0.001
ADAM_B1 = 0.9
ADAM_B2 = 0.999
ADAM_EPS = 1e-08
ADAM_WD = 0.01
ADAM_STEP = 10

N_CHIPS = 4
N_DEV = 8
SUBLANES = 8
LANES = 128
VMEM_LIMIT = 56 * 1024 * 1024
ROW_BLOCK_ELEMS = 1 << 18
MESH = pl.DeviceIdType.MESH

NN = (((1,), (0,)), ((), ()))
NT = (((1,), (1,)), ((), ()))
TN = (((0,), (0,)), ((), ()))


def _params(*sem):
    return pltpu.CompilerParams(dimension_semantics=sem, vmem_limit_bytes=VMEM_LIMIT)


def _dot(a, b, dims=NN):
    return lax.dot_general(a, b, dims, preferred_element_type=F32)


def _split(a):
    hi = a.astype(BF16)
    lo = (a - hi.astype(F32)).astype(BF16)
    return hi, lo


def _dot3(a, b, dims=NN):
    a1, a2 = _split(a)
    b1, b2 = _split(b)
    return _dot(a1, b1, dims) + _dot(a1, b2, dims) + _dot(a2, b1, dims)


def _sigmoid(x):
    return 1.0 / (1.0 + jnp.exp(-x))


def _silu_parts(x):
    s = _sigmoid(x)
    return x * s, s * (1.0 + x * (1.0 - s))


_GELU_C = math.sqrt(2.0 / math.pi)


def _gelu_parts(x):
    x2 = x * x
    t = jnp.tanh(_GELU_C * (x + 0.044715 * x * x2))
    val = 0.5 * x * (1.0 + t)
    der = 0.5 * (1.0 + t) + 0.5 * x * (1.0 - t * t) * _GELU_C * (1.0 + 3.0 * 0.044715 * x2)
    return val, der


class _View:
    def __init__(self, buf, shape, locate, row_tile, col_tile):
        self.buf, self.shape, self.locate, self.row_tile, self.col_tile = buf, shape, locate, row_tile, col_tile

    def spec(self, t0, t1, block_of):
        def index(i, j, k):
            rb, cb = block_of(i, j, k)
            slab, r, c = self.locate(rb * t0, cb * t1)
            return slab, r // t0, c // t1
        return pl.BlockSpec((None, t0, t1), index)


def _operand(x):
    return (x.buf, x.shape, x.row_tile, x.col_tile) if isinstance(x, _View) else (x, x.shape, x.shape[0], x.shape[1])


def _matmul(a, b, *, name, ta=False, tb=False, residual=None, out_dtype=F32, out=None):
    a_arr, a_shape, a_rt, a_ct = _operand(a)
    b_arr, b_shape, b_rt, b_ct = _operand(b)
    (K, M) = a_shape if ta else a_shape[::-1]
    N = b_shape[0] if tb else b_shape[1]
    a_mt, a_kt = (a_ct, a_rt) if ta else (a_rt, a_ct)
    b_nt, b_kt = (b_rt, b_ct) if tb else (b_ct, b_rt)
    tm, tn, tk = min(M, 1024, a_mt), min(N, 1024, b_nt), min(K, 1024, a_kt, b_kt)
    if out is not None:
        tm, tn = min(tm, out.row_tile), min(tn, out.col_tile)
    nk = K // tk
    dims = ((((0,) if ta else (1,)), ((1,) if tb else (0,))), ((), ()))
    n_in = 2 + (residual is not None) + (out is not None)

    def body(*refs):
        a_ref, b_ref = refs[:2]
        r_ref = refs[2] if residual is not None else None
        o_ref, acc = refs[n_in], refs[n_in + 1]
        k = pl.program_id(2)

        @pl.when(k == 0)
        def _():
            acc[...] = jnp.zeros_like(acc)

        acc[...] += _dot(a_ref[...].astype(BF16), b_ref[...].astype(BF16), dims)

        @pl.when(k == nk - 1)
        def _():
            r = acc[...]
            if residual is not None:
                r = r + r_ref[...].astype(F32)
            o_ref[...] = r.astype(out_dtype)

    def spec(x, t0, t1, block_of):
        if isinstance(x, _View):
            return x.spec(t0, t1, block_of)
        return pl.BlockSpec((t0, t1), block_of)

    a_spec = spec(a, tk, tm, lambda i, j, k: (k, i)) if ta else spec(a, tm, tk, lambda i, j, k: (i, k))
    b_spec = spec(b, tn, tk, lambda i, j, k: (j, k)) if tb else spec(b, tk, tn, lambda i, j, k: (k, j))
    in_specs = [a_spec, b_spec]
    args = [a_arr, b_arr]
    if residual is not None:
        in_specs.append(pl.BlockSpec((tm, tn), lambda i, j, k: (i, j)))
        args.append(residual)
    aliases = {}
    if out is None:
        out_spec = pl.BlockSpec((tm, tn), lambda i, j, k: (i, j))
        out_shape = jax.ShapeDtypeStruct((M, N), out_dtype)
    else:
        out_spec = out.spec(tm, tn, lambda i, j, k: (i, j))
        out_shape = jax.ShapeDtypeStruct(out.buf.shape, out.buf.dtype)
        out_dtype = out.buf.dtype
        in_specs.append(pl.BlockSpec(memory_space=pl.ANY))
        args.append(out.buf)
        aliases = {len(args) - 1: 0}
    return pl.pallas_call(
        body, name=name, grid=(M // tm, N // tn, nk), in_specs=in_specs, out_specs=out_spec, out_shape=out_shape,
        scratch_shapes=[pltpu.VMEM((tm, tn), F32)], input_output_aliases=aliases,
        compiler_params=_params("parallel", "parallel", "arbitrary"),
    )(*args)


def _row_tile(T, C):
    tr = max(SUBLANES, min(T, ROW_BLOCK_ELEMS // C) // SUBLANES * SUBLANES)
    while T % tr:
        tr -= SUBLANES
    return tr


def _rows_call(body, name, T, C, row_ins, full_ins, row_outs, acc_outs=()):
    tr = _row_tile(T, C)
    row_spec = pl.BlockSpec((tr, C), lambda i: (i, 0))
    in_specs = [row_spec] * len(row_ins) + [pl.BlockSpec(f.shape, lambda i, n=f.ndim: (0,) * n) for f in full_ins]
    out_specs = [row_spec] * len(row_outs) + [pl.BlockSpec(s, lambda i, n=len(s): (0,) * n) for s in acc_outs]
    out_shape = [jax.ShapeDtypeStruct((T, C), d) for d in row_outs] + [jax.ShapeDtypeStruct(s, F32) for s in acc_outs]
    return pl.pallas_call(
        body, name=name, grid=(T // tr,), in_specs=in_specs, out_specs=out_specs, out_shape=out_shape,
        compiler_params=_params("arbitrary" if acc_outs else "parallel"),
    )(*row_ins, *full_ins)


def _rmsnorm_fwd(x, g, name):
    T, C = x.shape

    def body(x_ref, g_ref, h_ref):
        xv = x_ref[...]
        r = lax.rsqrt(jnp.mean(xv * xv, axis=-1, keepdims=True) + RMS_EPS)
        h_ref[...] = ((xv * r) * g_ref[...]).astype(BF16)

    return _rows_call(body, name, T, C, [x], [g.reshape(1, C)], [BF16])[0]


def _rmsnorm_bwd(x, g, dh, dres, name):
    T, C = x.shape

    def body(x_ref, dh_ref, dres_ref, g_ref, dx_ref, dxb_ref, dg_ref):
        @pl.when(pl.program_id(0) == 0)
        def _():
            dg_ref[...] = jnp.zeros_like(dg_ref)

        xv = x_ref[...]
        dhv = dh_ref[...]
        r = lax.rsqrt(jnp.mean(xv * xv, axis=-1, keepdims=True) + RMS_EPS)
        xn = xv * r
        dg_ref[...] += jnp.sum(dhv * xn, axis=0, keepdims=True)
        dxn = dhv * g_ref[...]
        dx = dres_ref[...] + r * (dxn - xn * jnp.mean(dxn * xn, axis=-1, keepdims=True))
        dx_ref[...] = dx
        dxb_ref[...] = dx.astype(BF16)

    return _rows_call(body, name, T, C, [x, dh, dres], [g.reshape(1, C)], [F32, BF16], [(1, C)])


def _qknorm_fwd(q, g, name):
    T, C = q.shape
    tr = min(T, 1024)

    def body(q_ref, g_ref, o_ref):
        xv = q_ref[...]
        r = lax.rsqrt(jnp.mean(xv * xv, axis=-1, keepdims=True) + RMS_EPS)
        o_ref[...] = ((xv * r) * g_ref[...]).astype(BF16)

    spec = pl.BlockSpec((tr, HEAD_DIM), lambda i, h: (i, h))
    return pl.pallas_call(
        body, name=name, grid=(T // tr, C // HEAD_DIM),
        in_specs=[spec, pl.BlockSpec((1, HEAD_DIM), lambda i, h: (0, 0))], out_specs=spec,
        out_shape=jax.ShapeDtypeStruct((T, C), BF16), compiler_params=_params("parallel", "parallel"),
    )(q, g.reshape(1, HEAD_DIM))


def _qknorm_bwd(q, g, dqn, name):
    T, C = q.shape
    tr = min(T, 1024)

    def body(q_ref, d_ref, g_ref, dq_ref, dg_ref):
        @pl.when((pl.program_id(0) == 0) & (pl.program_id(1) == 0))
        def _():
            dg_ref[...] = jnp.zeros_like(dg_ref)

        xv = q_ref[...]
        dv = d_ref[...]
        r = lax.rsqrt(jnp.mean(xv * xv, axis=-1, keepdims=True) + RMS_EPS)
        xn = xv * r
        dg_ref[...] += jnp.sum(dv * xn, axis=0, keepdims=True)
        dxn = dv * g_ref[...]
        dq_ref[...] = (r * (dxn - xn * jnp.mean(dxn * xn, axis=-1, keepdims=True))).astype(BF16)

    spec = pl.BlockSpec((tr, HEAD_DIM), lambda i, h: (i, h))
    gspec = pl.BlockSpec((1, HEAD_DIM), lambda i, h: (0, 0))
    return pl.pallas_call(
        body, name=name, grid=(T // tr, C // HEAD_DIM), in_specs=[spec, spec, gspec], out_specs=[spec, gspec],
        out_shape=[jax.ShapeDtypeStruct((T, C), BF16), jax.ShapeDtypeStruct((1, HEAD_DIM), F32)],
        compiler_params=_params("arbitrary", "arbitrary"),
    )(q, dqn, g.reshape(1, HEAD_DIM))


ATT_TQ = 512
ATT_TK = 256


def _logsig_pair(z):
    sp = jnp.log(1.0 + jnp.exp(-jnp.abs(z)))
    return jnp.minimum(z, 0.0) - sp, jnp.minimum(-z, 0.0) - sp


def _tri(n, strict_upper_src):
    j = lax.broadcasted_iota(jnp.int32, (n, n), 0)
    s = lax.broadcasted_iota(jnp.int32, (n, n), 1)
    if strict_upper_src == "gt":
        m = j > s
    elif strict_upper_src == "le":
        m = j <= s
    else:
        m = j < s
    return jnp.where(m, 1.0, 0.0).astype(BF16)


def _cumdot(x, tri):
    hi, lo = _split(x)
    return _dot(hi, tri) + _dot(lo, tri)


def _attn_tiles(S):
    tq, tk = min(ATT_TQ, S), min(ATT_TK, S)
    return tq, tk, S // tq, tq // tk


def _attn_fwd(qn, kn, v, Bl, S):
    T, C = qn.shape
    H = C // HEAD_DIM
    tq, tk, nq, kpq = _attn_tiles(S)
    scale = 1.0 / math.sqrt(HEAD_DIM)

    def body(q_ref, k_ref, v_ref, o_ref, bt_ref):
        i = pl.program_id(2)
        tri = _tri(tk, "gt")
        rowpos = lax.broadcasted_iota(jnp.int32, (tq, tk), 0) + i * tq
        colpos = lax.broadcasted_iota(jnp.int32, (tq, tk), 1)
        o_ref[...] = jnp.zeros_like(o_ref)
        bt_ref[...] = jnp.zeros_like(bt_ref)
        nkb = (i + 1) * kpq

        def step(n, carry):
            j = nkb - 1 - n
            rows = pl.ds(pl.multiple_of(j * tk, tk), tk)
            z = _dot(q_ref[...], k_ref[rows, :], NT) * scale
            mask = (colpos + j * tk) < rowpos
            a, b = _logsig_pair(z)
            b = jnp.where(mask, b, 0.0)
            suffix = _cumdot(b, tri)
            w = jnp.where(mask, jnp.exp(a + suffix + bt_ref[...]), 0.0)
            o_ref[...] += _dot(w.astype(BF16), v_ref[rows, :])
            bt_ref[...] += jnp.sum(b, axis=-1, keepdims=True)
            return carry

        lax.fori_loop(0, nkb, step, 0)

    qspec = pl.BlockSpec((tq, HEAD_DIM), lambda b, h, i: (b * nq + i, h))
    kspec = pl.BlockSpec((S, HEAD_DIM), lambda b, h, i: (b, h))
    btspec = pl.BlockSpec((None, None, tq, 1), lambda b, h, i: (b, h, i, 0))
    return pl.pallas_call(
        body, name="attn_fwd", grid=(Bl, H, nq), in_specs=[qspec, kspec, kspec], out_specs=[qspec, btspec],
        out_shape=[jax.ShapeDtypeStruct((T, C), F32), jax.ShapeDtypeStruct((Bl, H, S, 1), F32)],
        compiler_params=_params("parallel", "parallel", "arbitrary"),
    )(qn, kn, v)


def _attn_bwd(qn, kn, v, do, btot, Bl, S):
    T, C = qn.shape
    H = C // HEAD_DIM
    tq, tk, nq, kpq = _attn_tiles(S)
    scale = 1.0 / math.sqrt(HEAD_DIM)

    def body(q_ref, k_ref, v_ref, do_ref, bt_ref, dq_ref, dk_ref, dv_ref, pb_ref, pdl_ref):
        i = pl.program_id(2)

        @pl.when(i == 0)
        def _():
            dk_ref[...] = jnp.zeros_like(dk_ref)
            dv_ref[...] = jnp.zeros_like(dv_ref)

        tri_le = _tri(tk, "le")
        tri_lt = _tri(tk, "lt")
        rowpos = lax.broadcasted_iota(jnp.int32, (tq, tk), 0) + i * tq
        colpos = lax.broadcasted_iota(jnp.int32, (tq, tk), 1)
        dq_ref[...] = jnp.zeros_like(dq_ref)
        pb_ref[...] = bt_ref[...]
        pdl_ref[...] = jnp.zeros_like(pdl_ref)

        def step(j, carry):
            rows = pl.ds(pl.multiple_of(j * tk, tk), tk)
            kj = k_ref[rows, :]
            z = _dot(q_ref[...], kj, NT) * scale
            mask = (colpos + j * tk) < rowpos
            a, b = _logsig_pair(z)
            b = jnp.where(mask, b, 0.0)
            suffix = pb_ref[...] - _cumdot(b, tri_le)
            w = jnp.where(mask, jnp.exp(a + suffix), 0.0)
            dl = _dot(do_ref[...], v_ref[rows, :], NT) * w
            prefix = pdl_ref[...] + _cumdot(dl, tri_lt)
            beta = jnp.exp(a)
            dzb = (jnp.where(mask, dl * (1.0 - beta) - beta * prefix, 0.0) * scale).astype(BF16)
            dq_ref[...] += _dot(dzb, kj)
            dk_ref[rows, :] += _dot(dzb, q_ref[...], TN)
            dv_ref[rows, :] += _dot(w.astype(BF16), do_ref[...], TN)
            pb_ref[...] -= jnp.sum(b, axis=-1, keepdims=True)
            pdl_ref[...] += jnp.sum(dl, axis=-1, keepdims=True)
            return carry

        lax.fori_loop(0, (i + 1) * kpq, step, 0)

    qspec = pl.BlockSpec((tq, HEAD_DIM), lambda b, h, i: (b * nq + i, h))
    kspec = pl.BlockSpec((S, HEAD_DIM), lambda b, h, i: (b, h))
    btspec = pl.BlockSpec((None, None, tq, 1), lambda b, h, i: (b, h, i, 0))
    return pl.pallas_call(
        body, name="attn_bwd", grid=(Bl, H, nq), in_specs=[qspec, kspec, kspec, qspec, btspec],
        out_specs=[qspec, kspec, kspec],
        out_shape=[jax.ShapeDtypeStruct((T, C), F32)] * 3,
        scratch_shapes=[pltpu.VMEM((tq, 1), F32), pltpu.VMEM((tq, 1), F32)],
        compiler_params=_params("parallel", "parallel", "arbitrary"),
    )(qn, kn, v, do, btot)


SSM_TIME_BLOCK = 512
CHUNK = SUBLANES


def _cmadd(xr, xi, ar, ai, sr, si):
    return xr + ar * sr - ai * si, xi + ar * si + ai * sr


def _chunk_scan(xr, xi, tab_ref, cr, ci, reverse):
    for lvl, d in enumerate((1, 2, 4)):
        shift = (CHUNK - d) if reverse else d
        sr = pltpu.roll(xr, shift, 0)
        si = pltpu.roll(xi, shift, 0)
        ar = tab_ref[pl.ds((2 * lvl) * CHUNK, CHUNK), :]
        ai = tab_ref[pl.ds((2 * lvl + 1) * CHUNK, CHUNK), :]
        xr, xi = _cmadd(xr, xi, ar, ai, sr, si)
    pr = tab_ref[pl.ds(6 * CHUNK, CHUNK), :]
    pi = tab_ref[pl.ds(7 * CHUNK, CHUNK), :]
    return _cmadd(xr, xi, pr, pi, cr, ci)


def _ssm_dims(S, C):
    G = C // GROUP
    GT = min(16, G)
    return G, GT, G // GT, GT * GROUP, GT * STATE, min(SSM_TIME_BLOCK, S)


def _ssm_fwd(u, mats, Bl, S):
    T, C = u.shape
    G, GT, ngt, cw, sw, TB = _ssm_dims(S, C)
    ntb = S // TB
    nch = TB // CHUNK

    def body(u_ref, bre_ref, bim_ref, cre_ref, cim_ref, d_ref, tab_ref, y_ref, hr_ref, hi_ref, car_r, car_i):
        @pl.when(pl.program_id(2) == 0)
        def _():
            car_r[...] = jnp.zeros_like(car_r)
            car_i[...] = jnp.zeros_like(car_i)

        uv = u_ref[...]
        hr_ref[...] = _dot3(uv, bre_ref[...])
        hi_ref[...] = _dot3(uv, bim_ref[...])

        def step(n, carry):
            cr, ci = carry
            rows = pl.ds(pl.multiple_of(n * CHUNK, CHUNK), CHUNK)
            xr, xi = _chunk_scan(hr_ref[rows, :], hi_ref[rows, :], tab_ref, cr, ci, False)
            hr_ref[rows, :] = xr
            hi_ref[rows, :] = xi
            last = (CHUNK - 1, CHUNK)
            return (jnp.broadcast_to(xr[last[0]:last[1], :], xr.shape), jnp.broadcast_to(xi[last[0]:last[1], :], xi.shape))

        cr, ci = lax.fori_loop(0, nch, step, (car_r[...], car_i[...]))
        car_r[...] = cr
        car_i[...] = ci
        y_ref[...] = _dot3(hr_ref[...], cre_ref[...]) - _dot3(hi_ref[...], cim_ref[...]) + d_ref[...] * uv

    uspec = pl.BlockSpec((TB, cw), lambda g, b, t: (b * ntb + t, g))
    hspec = pl.BlockSpec((TB, sw), lambda g, b, t: (b * ntb + t, g))

    def gspec(r, c):
        return pl.BlockSpec((None, r, c), lambda g, b, t: (g, 0, 0))

    return pl.pallas_call(
        body, name="ssm_fwd", grid=(ngt, Bl, ntb),
        in_specs=[uspec, gspec(cw, sw), gspec(cw, sw), gspec(sw, cw), gspec(sw, cw), gspec(1, cw), gspec(8 * CHUNK, sw)],
        out_specs=[uspec, hspec, hspec],
        out_shape=[jax.ShapeDtypeStruct((T, C), F32), jax.ShapeDtypeStruct((T, G * STATE), F32),
                   jax.ShapeDtypeStruct((T, G * STATE), F32)],
        scratch_shapes=[pltpu.VMEM((CHUNK, sw), F32), pltpu.VMEM((CHUNK, sw), F32)],
        compiler_params=_params("parallel", "arbitrary", "arbitrary"),
    )(u, mats["bbd_re"], mats["bbd_im"], mats["cbd_re"], mats["cbd_im"], mats["d"], mats["tab_fwd"])


def _ssm_bwd(u, dy, h_re, h_im, mats, Bl, S):
    T, C = u.shape
    G, GT, ngt, cw, sw, TB = _ssm_dims(S, C)
    ntb = S // TB
    nch = TB // CHUNK
    rpb = TB // CHUNK

    def body(u_ref, dy_ref, hr_ref, hi_ref, hpr_ref, hpi_ref, ctr_ref, cti_ref, btr_ref, bti_ref, d_ref, tab_ref,
             du_ref, dbr_ref, dbi_ref, dcr_ref, dci_ref, dlr_ref, dli_ref, dd_ref, gr_ref, gi_ref, car_r, car_i):
        b = pl.program_id(1)
        t = pl.program_id(2)

        @pl.when((b == 0) & (t == 0))
        def _():
            for ref in (dbr_ref, dbi_ref, dcr_ref, dci_ref, dlr_ref, dli_ref, dd_ref):
                ref[...] = jnp.zeros_like(ref)

        @pl.when(t == 0)
        def _():
            car_r[...] = jnp.zeros_like(car_r)
            car_i[...] = jnp.zeros_like(car_i)

        uv = u_ref[...]
        dyv = dy_ref[...]
        gr_ref[...] = _dot3(dyv, ctr_ref[...])
        gi_ref[...] = -_dot3(dyv, cti_ref[...])
        alive = jnp.where(t == ntb - 1, 0.0, 1.0)
        row0 = lax.broadcasted_iota(jnp.int32, (CHUNK, sw), 0) == 0

        def step(m, carry):
            cr, ci, ar, ai = carry
            n = nch - 1 - m
            rows = pl.ds(pl.multiple_of(n * CHUNK, CHUNK), CHUNK)
            prow = pl.ds(pl.multiple_of(jnp.maximum(n - 1, 0) * CHUNK, CHUNK), CHUNK)
            xr, xi = _chunk_scan(gr_ref[rows, :], gi_ref[rows, :], tab_ref, cr, ci, True)
            gr_ref[rows, :] = xr
            gi_ref[rows, :] = xi
            first = n == 0
            pr = jnp.where(first, hpr_ref[...] * alive, hr_ref[prow, :])
            pi = jnp.where(first, hpi_ref[...] * alive, hi_ref[prow, :])
            sr = jnp.where(row0, pltpu.roll(pr, 1, 0), pltpu.roll(hr_ref[rows, :], 1, 0))
            si = jnp.where(row0, pltpu.roll(pi, 1, 0), pltpu.roll(hi_ref[rows, :], 1, 0))
            ar = ar + xr * sr + xi * si
            ai = ai + xi * sr - xr * si
            return (jnp.broadcast_to(xr[0:1, :], xr.shape), jnp.broadcast_to(xi[0:1, :], xi.shape), ar, ai)

        zero = jnp.zeros((CHUNK, sw), F32)
        cr, ci, ar, ai = lax.fori_loop(0, nch, step, (car_r[...], car_i[...], zero, zero))
        car_r[...] = cr
        car_i[...] = ci
        dlr_ref[...] += ar
        dli_ref[...] += ai
        gr = gr_ref[...]
        gi = gi_ref[...]
        dbr_ref[...] += _dot3(uv, gr, TN)
        dbi_ref[...] += _dot3(uv, gi, TN)
        dcr_ref[...] += _dot3(hr_ref[...], dyv, TN)
        dci_ref[...] -= _dot3(hi_ref[...], dyv, TN)
        dd_ref[...] += jnp.sum(dyv * uv, axis=0, keepdims=True)
        du_ref[...] = _dot3(gr, btr_ref[...]) + _dot3(gi, bti_ref[...]) + d_ref[...] * dyv

    def tblk(b, t):
        return b * ntb + (ntb - 1 - t)

    uspec = pl.BlockSpec((TB, cw), lambda g, b, t: (tblk(b, t), g))
    hspec = pl.BlockSpec((TB, sw), lambda g, b, t: (tblk(b, t), g))
    hpspec = pl.BlockSpec((CHUNK, sw), lambda g, b, t: (jnp.maximum(tblk(b, t) * rpb - 1, 0), g))

    def gspec(r, c):
        return pl.BlockSpec((None, r, c), lambda g, b, t: (g, 0, 0))

    def gshape(r, c):
        return jax.ShapeDtypeStruct((ngt, r, c), F32)

    return pl.pallas_call(
        body, name="ssm_bwd", grid=(ngt, Bl, ntb),
        in_specs=[uspec, uspec, hspec, hspec, hpspec, hpspec, gspec(cw, sw), gspec(cw, sw), gspec(sw, cw), gspec(sw, cw),
                  gspec(1, cw), gspec(8 * CHUNK, sw)],
        out_specs=[uspec, gspec(cw, sw), gspec(cw, sw), gspec(sw, cw), gspec(sw, cw), gspec(CHUNK, sw), gspec(CHUNK, sw),
                   gspec(1, cw)],
        out_shape=[jax.ShapeDtypeStruct((T, C), F32), gshape(cw, sw), gshape(cw, sw), gshape(sw, cw), gshape(sw, cw),
                   gshape(CHUNK, sw), gshape(CHUNK, sw), gshape(1, cw)],
        scratch_shapes=[pltpu.VMEM((TB, sw), F32), pltpu.VMEM((TB, sw), F32), pltpu.VMEM((CHUNK, sw), F32),
                        pltpu.VMEM((CHUNK, sw), F32)],
        compiler_params=_params("arbitrary", "arbitrary", "arbitrary"),
    )(u, dy, h_re, h_im, h_re, h_im, mats["cbdT_re"], mats["cbdT_im"], mats["bbdT_re"], mats["bbdT_im"], mats["d"],
      mats["tab_rev"])


def _zoh(a_re, a_im, log_dt, b_re, b_im):
    dt = jnp.exp(log_dt)[:, None]
    mag = jnp.exp(a_re * dt)
    l_re = mag * jnp.cos(a_im * dt)
    l_im = mag * jnp.sin(a_im * dt)
    den = a_re * a_re + a_im * a_im
    f_re = ((l_re - 1.0) * a_re + l_im * a_im) / den
    f_im = (l_im * a_re - (l_re - 1.0) * a_im) / den
    bb_re = f_re[..., None] * b_re - f_im[..., None] * b_im
    bb_im = f_re[..., None] * b_im + f_im[..., None] * b_re
    return l_re, l_im, bb_re, bb_im


def _ssm_matrices(a_re, a_im, log_dt, b_re, b_im, c_re, c_im, d, S):
    G = a_re.shape[0]
    _, GT, ngt, cw, sw, _ = _ssm_dims(S, G * GROUP)
    _, _, bb_re, bb_im = _zoh(a_re, a_im, log_dt, b_re, b_im)
    eye = jnp.eye(GT, dtype=F32)

    def bd_b(bb):
        return jnp.einsum("tgpi,gh->tgihp", bb.reshape(ngt, GT, STATE, GROUP), eye).reshape(ngt, cw, sw)

    def bd_c(c):
        return jnp.einsum("tgip,gh->tgphi", c.reshape(ngt, GT, GROUP, STATE), eye).reshape(ngt, sw, cw)

    dt = jnp.exp(log_dt)[:, None]

    def power(k, conj):
        mag = jnp.exp(k * a_re * dt)
        ang = k * a_im * dt
        return (mag * jnp.cos(ang)).reshape(ngt, 1, sw), ((-1.0 if conj else 1.0) * mag * jnp.sin(ang)).reshape(ngt, 1, sw)

    r = jnp.arange(CHUNK)[None, :, None]

    def table(reverse):
        parts = []
        for dd in (1, 2, 4):
            pr, pi = power(float(dd), reverse)
            keep = (r <= CHUNK - 1 - dd) if reverse else (r >= dd)
            parts += [jnp.where(keep, pr, 0.0), jnp.where(keep, pi, 0.0)]
        exps = [(CHUNK - k) if reverse else (k + 1) for k in range(CHUNK)]
        pw = [power(float(e), reverse) for e in exps]
        parts += [jnp.concatenate([p[0] for p in pw], axis=1), jnp.concatenate([p[1] for p in pw], axis=1)]
        return jnp.concatenate([jnp.broadcast_to(p, (ngt, CHUNK, sw)) for p in parts], axis=1)

    mats = dict(bbd_re=bd_b(bb_re), bbd_im=bd_b(bb_im), cbd_re=bd_c(c_re), cbd_im=bd_c(c_im),
                d=d.reshape(ngt, 1, cw), tab_fwd=table(False), tab_rev=table(True))
    for k in ("bbd_re", "bbd_im", "cbd_re", "cbd_im"):
        mats[k.replace("bd_", "bdT_")] = jnp.swapaxes(mats[k], 1, 2)
    return mats


def _ssm_unblock(dbr, dbi, dcr, dci, dlr, dli, dd, G):
    ngt = dbr.shape[0]
    GT = G // ngt
    eye = jnp.eye(GT, dtype=F32)

    def ub(x):
        return jnp.einsum("tgihp,gh->tgpi", x.reshape(ngt, GT, GROUP, GT, STATE), eye).reshape(G, STATE, GROUP)

    def uc(x):
        return jnp.einsum("tgphi,gh->tgip", x.reshape(ngt, GT, STATE, GT, GROUP), eye).reshape(G, GROUP, STATE)

    return (dlr.sum(axis=1).reshape(G, STATE), dli.sum(axis=1).reshape(G, STATE), ub(dbr), ub(dbi), uc(dcr), uc(dci),
            dd.reshape(G * GROUP))


def _attn_gate_fwd(o, gate):
    T, C = o.shape

    def body(o_ref, g_ref, out_ref):
        out_ref[...] = (o_ref[...] * _silu_parts(g_ref[...])[0]).astype(BF16)

    return _rows_call(body, "attn_gate_fwd", T, C, [o, gate], [], [BF16])[0]


def _attn_gate_bwd(dog, o, gate):
    T, C = o.shape

    def body(d_ref, o_ref, g_ref, do_ref, dg_ref):
        val, der = _silu_parts(g_ref[...])
        dv = d_ref[...]
        do_ref[...] = (dv * val).astype(BF16)
        dg_ref[...] = (dv * o_ref[...] * der).astype(BF16)

    return _rows_call(body, "attn_gate_bwd", T, C, [dog, o, gate], [], [BF16, BF16])


def _gelu_fwd(y):
    T, C = y.shape

    def body(y_ref, out_ref):
        out_ref[...] = _gelu_parts(y_ref[...])[0].astype(BF16)

    return _rows_call(body, "gelu_fwd", T, C, [y], [], [BF16])[0]


def _glu_fwd(y, gl, gate, glu_b):
    T, C = y.shape

    def body(y_ref, gl_ref, g_ref, b_ref, out_ref):
        yg = _gelu_parts(y_ref[...])[0]
        sg = _sigmoid(gl_ref[...] + b_ref[...])
        out_ref[...] = (yg * sg * _silu_parts(g_ref[...])[0]).astype(BF16)

    return _rows_call(body, "glu_fwd", T, C, [y, gl, gate], [glu_b.reshape(1, C)], [BF16])[0]


def _glu_bwd(dy3, y, gl, gate, glu_b):
    T, C = y.shape

    def body(d_ref, y_ref, gl_ref, g_ref, b_ref, dgl_ref, dgate_ref, t1_ref, db_ref):
        @pl.when(pl.program_id(0) == 0)
        def _():
            db_ref[...] = jnp.zeros_like(db_ref)

        yg = _gelu_parts(y_ref[...])[0]
        sg = _sigmoid(gl_ref[...] + b_ref[...])
        sl, sld = _silu_parts(g_ref[...])
        dv = d_ref[...]
        dy2 = dv * sl
        dgl = dy2 * yg * sg * (1.0 - sg)
        dgl_ref[...] = dgl.astype(BF16)
        dgate_ref[...] = (dv * (yg * sg) * sld).astype(BF16)
        t1_ref[...] = dy2 * sg
        db_ref[...] += jnp.sum(dgl, axis=0, keepdims=True)

    return _rows_call(body, "glu_bwd", T, C, [dy3, y, gl, gate], [glu_b.reshape(1, C)], [BF16, BF16, F32], [(1, C)])


def _gelu_bwd(t1, t2, y):
    T, C = y.shape

    def body(a_ref, b_ref, y_ref, out_ref):
        out_ref[...] = (a_ref[...] + b_ref[...]) * _gelu_parts(y_ref[...])[1]

    return _rows_call(body, "gelu_bwd", T, C, [t1, t2, y], [], [F32])[0]


def _loss_head(x2, target):
    T, C = x2.shape

    def body(x_ref, t_ref, d_ref, db_ref, l_ref):
        @pl.when(pl.program_id(0) == 0)
        def _():
            l_ref[...] = jnp.zeros_like(l_ref)

        e = x_ref[...] - t_ref[...]
        d = e * (1.0 / C)
        d_ref[...] = d
        db_ref[...] = d.astype(BF16)
        l_ref[...] += 0.5 * jnp.sum(jnp.sum(e * e, axis=-1, keepdims=True) * (1.0 / C), axis=0, keepdims=True)

    return _rows_call(body, "loss_head", T, C, [x2, target], [], [F32, BF16], [(1, 1)])


def _adamw(w, g, m, v, name):
    shape = w.shape
    C = shape[-1]
    R = w.size // C
    bc1 = 1.0 - ADAM_B1 ** ADAM_STEP
    bc2 = 1.0 - ADAM_B2 ** ADAM_STEP

    def body(w_ref, g_ref, m_ref, v_ref, d_ref, nm_ref, nv_ref):
        gv = g_ref[...]
        mn = ADAM_B1 * m_ref[...] + (1.0 - ADAM_B1) * gv
        vn = ADAM_B2 * v_ref[...] + (1.0 - ADAM_B2) * (gv * gv)
        d_ref[...] = -ADAM_LR * ((mn / bc1) / (jnp.sqrt(vn / bc2) + ADAM_EPS) + ADAM_WD * w_ref[...])
        nm_ref[...] = mn
        nv_ref[...] = vn

    outs = _rows_call(body, name, R, C, [a.reshape(R, C) for a in (w, g, m, v)], [], [F32, F32, F32])
    return [o.reshape(shape) for o in outs]


def _sum_leading(x, name):
    n, R, C = x.shape
    tr = _row_tile(R, C * n)

    def body(x_ref, o_ref):
        acc = x_ref[0].astype(F32)
        for k in range(1, n):
            acc = acc + x_ref[k].astype(F32)
        o_ref[...] = acc

    return pl.pallas_call(
        body, name=name, grid=(R // tr,), in_specs=[pl.BlockSpec((n, tr, C), lambda i: (0, i, 0))],
        out_specs=pl.BlockSpec((tr, C), lambda i: (i, 0)), out_shape=jax.ShapeDtypeStruct((R, C), F32),
        compiler_params=_params("parallel"),
    )(x)


def _add_halves(g, c, name):
    full, recv = g
    n, R, C = full.shape
    half = R // 2
    tr = _row_tile(half, C)
    nb = half // tr

    def body(c_ref, a_ref, b_ref, o_ref):
        o_ref[...] = (a_ref[...] + b_ref[...]).astype(BF16)

    grid_spec = pltpu.PrefetchScalarGridSpec(
        num_scalar_prefetch=1, grid=(n, nb),
        in_specs=[pl.BlockSpec((None, tr, C), lambda j, i, c_ref: (j, c_ref[0] * nb + i, 0)),
                  pl.BlockSpec((None, tr, C), lambda j, i, c_ref: (j, i, 0))],
        out_specs=pl.BlockSpec((None, tr, C), lambda j, i, c_ref: (j, i, 0)))
    return pl.pallas_call(
        body, name=name, grid_spec=grid_spec, out_shape=jax.ShapeDtypeStruct((n, half, C), BF16),
        compiler_params=_params("parallel", "parallel"),
    )(c.reshape(1).astype(jnp.int32), full, recv)


ANY = pl.BlockSpec(memory_space=pl.ANY)


def _position():
    return lax.axis_index("x"), lax.axis_index("y"), lax.axis_index("c")


def _all_gather8(blk, name):
    M, N = blk.shape

    def body(x_ref, out_ref, send_sems, recv_sems, local_sem):
        x, y, c = _position()
        me, sibling = (x, y, c), (x, y, 1 - c)
        chips = [(1 - x, y), (x, 1 - y), (1 - x, 1 - y)]

        def slab(px, py, pc):
            return out_ref.at[4 * px + 2 * py + pc]

        def copy(k, block, to, src=None):
            return pltpu.make_async_remote_copy(
                src_ref=slab(*block) if src is None else src, dst_ref=slab(*block),
                send_sem=send_sems.at[k], recv_sem=recv_sems.at[k], device_id=to, device_id_type=MESH)

        mine = pltpu.make_async_copy(x_ref, slab(*me), local_sem)
        mine.start()
        first = [copy(0, me, sibling, src=x_ref)]
        first += [copy(1 + j, me, (*chip, c), src=x_ref) for j, chip in enumerate(chips)]
        for cp in first:
            cp.start()
        passed = [copy(4 + j, (*chip, c), sibling) for j, chip in enumerate(chips)]
        for j, chip in enumerate(chips):
            copy(1 + j, (*chip, c), me).wait_recv()
            passed[j].start()
        copy(0, sibling, me).wait_recv()
        for j, chip in enumerate(chips):
            copy(4 + j, (*chip, 1 - c), me).wait_recv()
        for cp in first + passed:
            cp.wait_send()
        mine.wait()

    return pl.pallas_call(
        body, name=name, in_specs=[ANY], out_specs=ANY, out_shape=jax.ShapeDtypeStruct((N_DEV, M, N), blk.dtype),
        scratch_shapes=[pltpu.SemaphoreType.DMA((7,)), pltpu.SemaphoreType.DMA((7,)), pltpu.SemaphoreType.DMA],
    )(blk)


def _sibling_send_half(g, name):
    n, R, C = g.shape
    half = R // 2

    def body(g_ref, out_ref, send_sem, recv_sem):
        x, y, c = _position()
        src = g_ref.at[:, pl.ds((1 - c) * half, half), :]
        cp = pltpu.make_async_remote_copy(src_ref=src, dst_ref=out_ref, send_sem=send_sem, recv_sem=recv_sem,
                                          device_id=(x, y, 1 - c), device_id_type=MESH)
        cp.start()
        cp.wait()

    return pl.pallas_call(
        body, name=name, in_specs=[ANY], out_specs=ANY, out_shape=jax.ShapeDtypeStruct((n, half, C), g.dtype),
        scratch_shapes=[pltpu.SemaphoreType.DMA, pltpu.SemaphoreType.DMA],
    )(g)


def _chip_exchange(p, name):
    n, M, C = p.shape

    def body(p_ref, out_ref, send_sems, recv_sems, local_sem):
        x, y, c = _position()
        my = 2 * x + y
        chips = [(1 - x, y), (x, 1 - y), (1 - x, 1 - y)]
        mine = pltpu.make_async_copy(p_ref.at[my], out_ref.at[my], local_sem)
        mine.start()
        copies = []
        for k, (px, py) in enumerate(chips):
            cp = pltpu.make_async_remote_copy(
                src_ref=p_ref.at[2 * px + py], dst_ref=out_ref.at[my], send_sem=send_sems.at[k], recv_sem=recv_sems.at[k],
                device_id=(px, py, c), device_id_type=MESH)
            cp.start()
            copies.append(cp)
        for k, (px, py) in enumerate(chips):
            pltpu.make_async_remote_copy(
                src_ref=p_ref.at[my], dst_ref=out_ref.at[2 * px + py], send_sem=send_sems.at[k], recv_sem=recv_sems.at[k],
                device_id=(px, py, c), device_id_type=MESH).wait_recv()
        for cp in copies:
            cp.wait_send()
        mine.wait()

    return pl.pallas_call(
        body, name=name, in_specs=[ANY], out_specs=ANY, out_shape=jax.ShapeDtypeStruct((n, M, C), p.dtype),
        scratch_shapes=[pltpu.SemaphoreType.DMA((3,)), pltpu.SemaphoreType.DMA((3,)), pltpu.SemaphoreType.DMA],
    )(p)


def _sibling_swap(f, name):
    def body(f_ref, out_ref, send_sem, recv_sem):
        x, y, c = _position()
        cp = pltpu.make_async_remote_copy(src_ref=f_ref, dst_ref=out_ref, send_sem=send_sem, recv_sem=recv_sem,
                                          device_id=(x, y, 1 - c), device_id_type=MESH)
        cp.start()
        cp.wait()

    return pl.pallas_call(
        body, name=name, in_specs=[ANY], out_specs=ANY, out_shape=jax.ShapeDtypeStruct(f.shape, f.dtype),
        scratch_shapes=[pltpu.SemaphoreType.DMA, pltpu.SemaphoreType.DMA],
    )(f)


def _pack_dense(attn_in, attn_out, ssm_in, glu_w, ssm_out):
    hd = attn_in.shape[0] // 2
    return jnp.concatenate([attn_in, attn_out, jnp.concatenate([ssm_in[:hd], ssm_in[hd:]], axis=1), glu_w, ssm_out],
                           axis=0)


def _unpack_dense(p):
    D = p.shape[-1]
    q, hd = D // N_CHIPS, D // 2
    o = [0, D, D + q, D + q + hd, D + 2 * q + hd, D + 3 * q + hd]
    ssm_in = p[o[2]:o[3]]
    return (p[o[0]:o[1]], p[o[1]:o[2]], jnp.concatenate([ssm_in[:, :hd], ssm_in[:, hd:]], axis=0), p[o[3]:o[4]],
            p[o[4]:o[5]])


def _dense_views(buf):
    D = buf.shape[-1]
    q, hd = D // N_CHIPS, D // 2
    o_out, o_in, o_glu, o_sout = D, D + q, D + q + hd, D + 2 * q + hd

    def col_block(j):
        return _View(buf, (D, D), lambda r, c: (j, r, c), D, D)

    def row_sharded(off):
        return _View(buf, (D, D), lambda r, c: (r // q, off + r % q, c), math.gcd(q, off), D)

    def ssm_in(part):
        return _View(buf, (D, D), lambda r, c: (2 * part + c // hd, o_in + r % hd, (r // hd) * hd + c % hd),
                     math.gcd(hd, o_in), hd)

    return dict(attn_in=[col_block(j) for j in range(N_CHIPS)], attn_out=row_sharded(o_out), ssm_in_u=ssm_in(0),
                ssm_in_gate=ssm_in(1), glu=row_sharded(o_glu), ssm_out=row_sharded(o_sout))


SMALL_REPL = ("norm_g", "attn_q_g", "attn_k_g")


def _pack_small(parts):
    flat = jnp.concatenate([p.reshape(-1) for p in parts])
    pad = (-flat.size) % (2 * SUBLANES * LANES)
    return jnp.pad(flat, (0, pad)).reshape(-1, LANES)


def _unpack_small(buf, shapes):
    flat = buf.reshape(-1)
    out, off = [], 0
    for s in shapes:
        n = math.prod(s)
        out.append(flat[off:off + n].reshape(s))
        off += n
    return out


def _local_step(x, target, norm_g, q_g, k_g, w_packed, ssm_small):
    Bl, S, D = x.shape
    T = Bl * S
    x0 = x.reshape(T, D)
    tgt = target.reshape(T, D)
    a_re, a_im, log_dt, b_re, b_im, c_re, c_im, d_skip, glu_b = ssm_small
    G = a_re.shape[0]
    mats = _ssm_matrices(a_re, a_im, log_dt, b_re, b_im, c_re, c_im, d_skip, S)
    w = _dense_views(w_packed)
    grads = lax.empty(w_packed.shape, F32)

    def wgrad(a, b, key, name, j=None):
        view = _dense_views(grads)[key]
        return _matmul(a, b, name=name, ta=True, out=view if j is None else view[j])

    h0 = _rmsnorm_fwd(x0, norm_g[0], "norm0_fwd")
    q, k, v, gate = [_matmul(h0, w["attn_in"][j], name=f"attn_in_{j}", out_dtype=(BF16 if j == 2 else F32))
                     for j in range(4)]
    qn = _qknorm_fwd(q, q_g, "qnorm_fwd")
    kn = _qknorm_fwd(k, k_g, "knorm_fwd")
    o, btot = _attn_fwd(qn, kn, v, Bl, S)
    og = _attn_gate_fwd(o, gate)
    x1 = _matmul(og, w["attn_out"], name="attn_out", residual=x0)

    h1 = _rmsnorm_fwd(x1, norm_g[1], "norm1_fwd")
    u = _matmul(h1, w["ssm_in_u"], name="ssm_in_u")
    gate2 = _matmul(h1, w["ssm_in_gate"], name="ssm_in_gate")
    y, hs_re, hs_im = _ssm_fwd(u, mats, Bl, S)
    yg = _gelu_fwd(y)
    gl = _matmul(yg, w["glu"], name="glu_mm")
    y3 = _glu_fwd(y, gl, gate2, glu_b)
    x2 = _matmul(y3, w["ssm_out"], name="ssm_out", residual=x1)

    dx2, dx2b, loss = _loss_head(x2, tgt)

    dy3 = _matmul(dx2b, w["ssm_out"], name="ssm_out_dgrad", tb=True)
    grads = wgrad(y3, dx2b, "ssm_out", "ssm_out_wgrad")
    dgl, dgate2, t1, dglu_b = _glu_bwd(dy3, y, gl, gate2, glu_b)
    t2 = _matmul(dgl, w["glu"], name="glu_dgrad", tb=True)
    grads = wgrad(yg, dgl, "glu", "glu_wgrad")
    dy = _gelu_bwd(t1, t2, y)
    du, dbr, dbi, dcr, dci, dlr, dli, dd = _ssm_bwd(u, dy, hs_re, hs_im, mats, Bl, S)
    dh1 = _matmul(du, w["ssm_in_u"], name="ssm_in_dgrad_u", tb=True)
    dh1 = _matmul(dgate2, w["ssm_in_gate"], name="ssm_in_dgrad_gate", tb=True, residual=dh1)
    grads = wgrad(h1, du, "ssm_in_u", "ssm_in_wgrad_u")
    grads = wgrad(h1, dgate2, "ssm_in_gate", "ssm_in_wgrad_gate")
    dx1, dx1b, dng1 = _rmsnorm_bwd(x1, norm_g[1], dh1, dx2, "norm1_bwd")

    dog = _matmul(dx1b, w["attn_out"], name="attn_out_dgrad", tb=True)
    grads = wgrad(og, dx1b, "attn_out", "attn_out_wgrad")
    do, dgate = _attn_gate_bwd(dog, o, gate)
    dqn, dkn, dv = _attn_bwd(qn, kn, v, do, btot, Bl, S)
    dq, dqg = _qknorm_bwd(q, q_g, dqn, "qnorm_bwd")
    dk, dkg = _qknorm_bwd(k, k_g, dkn, "knorm_bwd")
    dproj = [dq, dk, dv, dgate]
    dh0 = None
    for j in range(4):
        dh0 = _matmul(dproj[j], w["attn_in"][j], name=f"attn_in_dgrad_{j}", tb=True, residual=dh0)
        grads = wgrad(h0, dproj[j], "attn_in", f"attn_in_wgrad_{j}", j)
    dx0, _, dng0 = _rmsnorm_bwd(x0, norm_g[0], dh0, dx1, "norm0_bwd")

    small = (jnp.concatenate([dng0, dng1], axis=0), dqg, dkg) + _ssm_unblock(dbr, dbi, dcr, dci, dlr, dli, dd, G) + (
        dglu_b.reshape(D),)
    return loss, dx0.reshape(Bl, S, D), grads, small


def _chip_rows(a, chip, n_per):
    return lax.dynamic_slice_in_dim(a, chip * n_per, n_per, axis=0)


def kernel(x, norm_g, attn_w_in, attn_q_g, attn_k_g, attn_w_out, ssm_w_in, ssm_A_re, ssm_A_im, ssm_log_dt, ssm_B_re, ssm_B_im, ssm_C_re, ssm_C_im, ssm_D, ssm_glu_w, ssm_glu_b, ssm_w_out, loss_target, m_norm_g, m_attn_w_in, m_attn_q_g, m_attn_k_g, m_attn_w_out, m_ssm_w_in, m_ssm_A_re, m_ssm_A_im, m_ssm_log_dt, m_ssm_B_re, m_ssm_B_im, m_ssm_C_re, m_ssm_C_im, m_ssm_D, m_ssm_glu_w, m_ssm_glu_b, m_ssm_w_out, v_norm_g, v_attn_w_in, v_attn_q_g, v_attn_k_g, v_attn_w_out, v_ssm_w_in, v_ssm_A_re, v_ssm_A_im, v_ssm_log_dt, v_ssm_B_re, v_ssm_B_im, v_ssm_C_re, v_ssm_C_im, v_ssm_D, v_ssm_glu_w, v_ssm_glu_b, v_ssm_w_out):
    D = x.shape[-1]
    cx, cy, cc = _position()
    chip = 2 * cx + cy
    G = D // GROUP
    Gl = G // N_CHIPS

    dense_local = _pack_dense(attn_w_in[0], attn_w_out[0], ssm_w_in[0], ssm_glu_w[0], ssm_w_out[0]).astype(BF16)
    R = dense_local.shape[0]
    my_half = lax.dynamic_slice_in_dim(dense_local, cc * (R // 2), R // 2, axis=0)
    dense_all = _all_gather8(my_half, "weights_all_gather").reshape(N_CHIPS, R, D)

    ssm_local = [ssm_A_re[0], ssm_A_im[0], ssm_log_dt[0], ssm_B_re[0], ssm_B_im[0], ssm_C_re[0], ssm_C_im[0], ssm_D[0],
                 ssm_glu_b[0]]
    small_local = _pack_small(ssm_local)
    half_rows = small_local.shape[0] // 2
    small_half = lax.dynamic_slice_in_dim(small_local, cc * half_rows, half_rows, axis=0)
    small_all = _all_gather8(small_half, "ssm_params_all_gather").reshape(N_CHIPS, 2 * half_rows, LANES)
    per_chip = [_unpack_small(small_all[j], [p.shape for p in ssm_local]) for j in range(N_CHIPS)]
    ssm_full = [jnp.concatenate([per_chip[j][i] for j in range(N_CHIPS)], axis=0) for i in range(len(ssm_local))]

    loss, grad_x, packed, small = _local_step(x, loss_target, norm_g, attn_q_g[0], attn_k_g[0], dense_all, ssm_full)
    loss = lax.psum(loss[0, 0], ("x", "y", "c"))

    q4 = D // N_CHIPS
    from_sibling = _sibling_send_half(packed, "grads_sibling_half")
    chip_sum = _add_halves((packed, from_sibling), cc, "grads_add_halves")
    gathered = _chip_exchange(chip_sum, "grads_chip_exchange")
    my_sum = _sum_leading(gathered, "grads_sum_chips")
    other_sum = _sibling_swap(my_sum, "grads_sibling_swap")
    south = cc == 0
    dense_grad = jnp.concatenate([jnp.where(south, my_sum, other_sum), jnp.where(south, other_sum, my_sum)], axis=0)
    gd = _unpack_dense(dense_grad)

    small_shapes = [s.shape for s in small]
    small_sum = _sum_leading(_all_gather8(_pack_small(small), "small_grads_all_gather"), "small_grads_sum")
    (dng, dqg, dkg, dl_re, dl_im, dbb_re, dbb_im, dc_re, dc_im, dd_skip, dglu_b) = _unpack_small(small_sum, small_shapes)
    a_re, a_im, log_dt, b_re, b_im = ssm_local[:5]
    _, zoh_vjp = jax.vjp(_zoh, a_re, a_im, log_dt, b_re, b_im)
    da_re, da_im, dlog_dt, db_re, db_im = zoh_vjp((_chip_rows(dl_re, chip, Gl), _chip_rows(dl_im, chip, Gl),
                                                   _chip_rows(dbb_re, chip, Gl), _chip_rows(dbb_im, chip, Gl)))
    grads = {
        "norm_g": dng, "attn_w_in": gd[0][None], "attn_q_g": dqg, "attn_k_g": dkg, "attn_w_out": gd[1][None],
        "ssm_w_in": gd[2][None], "ssm_A_re": da_re[None], "ssm_A_im": da_im[None], "ssm_log_dt": dlog_dt[None],
        "ssm_B_re": db_re[None], "ssm_B_im": db_im[None], "ssm_C_re": _chip_rows(dc_re, chip, Gl)[None],
        "ssm_C_im": _chip_rows(dc_im, chip, Gl)[None], "ssm_D": _chip_rows(dd_skip, chip, q4)[None],
        "ssm_glu_w": gd[3][None], "ssm_glu_b": _chip_rows(dglu_b, chip, q4)[None], "ssm_w_out": gd[4][None],
    }
    weights = dict(norm_g=norm_g, attn_w_in=attn_w_in, attn_q_g=attn_q_g, attn_k_g=attn_k_g, attn_w_out=attn_w_out,
                   ssm_w_in=ssm_w_in, ssm_A_re=ssm_A_re, ssm_A_im=ssm_A_im, ssm_log_dt=ssm_log_dt, ssm_B_re=ssm_B_re,
                   ssm_B_im=ssm_B_im, ssm_C_re=ssm_C_re, ssm_C_im=ssm_C_im, ssm_D=ssm_D, ssm_glu_w=ssm_glu_w,
                   ssm_glu_b=ssm_glu_b, ssm_w_out=ssm_w_out)
    m = dict(norm_g=m_norm_g, attn_w_in=m_attn_w_in, attn_q_g=m_attn_q_g, attn_k_g=m_attn_k_g, attn_w_out=m_attn_w_out,
             ssm_w_in=m_ssm_w_in, ssm_A_re=m_ssm_A_re, ssm_A_im=m_ssm_A_im, ssm_log_dt=m_ssm_log_dt, ssm_B_re=m_ssm_B_re,
             ssm_B_im=m_ssm_B_im, ssm_C_re=m_ssm_C_re, ssm_C_im=m_ssm_C_im, ssm_D=m_ssm_D, ssm_glu_w=m_ssm_glu_w,
             ssm_glu_b=m_ssm_glu_b, ssm_w_out=m_ssm_w_out)
    v = dict(norm_g=v_norm_g, attn_w_in=v_attn_w_in, attn_q_g=v_attn_q_g, attn_k_g=v_attn_k_g, attn_w_out=v_attn_w_out,
             ssm_w_in=v_ssm_w_in, ssm_A_re=v_ssm_A_re, ssm_A_im=v_ssm_A_im, ssm_log_dt=v_ssm_log_dt, ssm_B_re=v_ssm_B_re,
             ssm_B_im=v_ssm_B_im, ssm_C_re=v_ssm_C_re, ssm_C_im=v_ssm_C_im, ssm_D=v_ssm_D, ssm_glu_w=v_ssm_glu_w,
             ssm_glu_b=v_ssm_glu_b, ssm_w_out=v_ssm_w_out)
    names = list(weights)
    dense_names = ("attn_w_in", "attn_w_out", "ssm_w_in", "ssm_glu_w", "ssm_w_out")
    delta, new_m, new_v = {}, {}, {}
    for n in dense_names:
        delta[n], new_m[n], new_v[n] = _adamw(weights[n], grads[n], m[n], v[n], "adamw_" + n)
    small_names = [n for n in names if n not in dense_names]
    small_shapes = [weights[n].shape for n in small_names]
    packs = [_pack_small([d[n] for n in small_names]) for d in (weights, grads, m, v)]
    outs = _adamw(*packs, "adamw_small")
    for res, out in zip((delta, new_m, new_v), outs):
        for n, val in zip(small_names, _unpack_small(out, small_shapes)):
            res[n] = val
    return (loss, grad_x, *[grads[n] for n in names], *[delta[n] for n in names], *[new_m[n] for n in names],
            *[new_v[n] for n in names])
```

```python
import functools
import math

import jax
import jax.numpy as jnp
from jax import lax
from jax.experimental import pallas as pl
from jax.experimental.pallas import tpu as pltpu

F32 = jnp.float32
BF16 = jnp.bfloat16

HEAD_DIM = 128
GROUP = 16
STATE = 64
RMS_EPS = 1e-6
ADAM_LR = 0.001
ADAM_B1 = 0.9
ADAM_B2 = 0.999
ADAM_EPS = 1e-08
ADAM_WD = 0.01
ADAM_STEP = 10

N_CHIPS = 4
N_DEV = 8
SUBLANES = 8
LANES = 128
VMEM_LIMIT = 56 * 1024 * 1024
ROW_BLOCK_ELEMS = 1 << 18
MESH = pl.DeviceIdType.MESH

NN = (((1,), (0,)), ((), ()))
NT = (((1,), (1,)), ((), ()))
TN = (((0,), (0,)), ((), ()))


def _params(*sem):
    return pltpu.CompilerParams(dimension_semantics=sem, vmem_limit_bytes=VMEM_LIMIT)


def _dot(a, b, dims=NN):
    return lax.dot_general(a, b, dims, preferred_element_type=F32)


def _split(a):
    hi = a.astype(BF16)
    lo = (a - hi.astype(F32)).astype(BF16)
    return hi, lo


def _dot3(a, b, dims=NN):
    a1, a2 = _split(a)
    b1, b2 = _split(b)
    return _dot(a1, b1, dims) + _dot(a1, b2, dims) + _dot(a2, b1, dims)


def _sigmoid(x):
    return 1.0 / (1.0 + jnp.exp(-x))


def _silu_parts(x):
    s = _sigmoid(x)
    return x * s, s * (1.0 + x * (1.0 - s))


_GELU_C = math.sqrt(2.0 / math.pi)


def _gelu_parts(x):
    x2 = x * x
    t = jnp.tanh(_GELU_C * (x + 0.044715 * x * x2))
    val = 0.5 * x * (1.0 + t)
    der = 0.5 * (1.0 + t) + 0.5 * x * (1.0 - t * t) * _GELU_C * (1.0 + 3.0 * 0.044715 * x2)
    return val, der


class _View:
    def __init__(self, buf, shape, locate, row_tile, col_tile):
        self.buf, self.shape, self.locate, self.row_tile, self.col_tile = buf, shape, locate, row_tile, col_tile

    def spec(self, t0, t1, block_of):
        def index(i, j, k):
            rb, cb = block_of(i, j, k)
            slab, r, c = self.locate(rb * t0, cb * t1)
            return slab, r // t0, c // t1
        return pl.BlockSpec((None, t0, t1), index)


def _operand(x):
    return (x.buf, x.shape, x.row_tile, x.col_tile) if isinstance(x, _View) else (x, x.shape, x.shape[0], x.shape[1])


def _matmul(a, b, *, name, ta=False, tb=False, residual=None, out_dtype=F32, out=None):
    a_arr, a_shape, a_rt, a_ct = _operand(a)
    b_arr, b_shape, b_rt, b_ct = _operand(b)
    (K, M) = a_shape if ta else a_shape[::-1]
    N = b_shape[0] if tb else b_shape[1]
    a_mt, a_kt = (a_ct, a_rt) if ta else (a_rt, a_ct)
    b_nt, b_kt = (b_rt, b_ct) if tb else (b_ct, b_rt)
    tm, tn, tk = min(M, 1024, a_mt), min(N, 1024, b_nt), min(K, 1024, a_kt, b_kt)
    if out is not None:
        tm, tn = min(tm, out.row_tile), min(tn, out.col_tile)
    nk = K // tk
    dims = ((((0,) if ta else (1,)), ((1,) if tb else (0,))), ((), ()))
    n_in = 2 + (residual is not None) + (out is not None)

    def body(*refs):
        a_ref, b_ref = refs[:2]
        r_ref = refs[2] if residual is not None else None
        o_ref, acc = refs[n_in], refs[n_in + 1]
        k = pl.program_id(2)

        @pl.when(k == 0)
        def _():
            acc[...] = jnp.zeros_like(acc)

        acc[...] += _dot(a_ref[...].astype(BF16), b_ref[...].astype(BF16), dims)

        @pl.when(k == nk - 1)
        def _():
            r = acc[...]
            if residual is not None:
                r = r + r_ref[...].astype(F32)
            o_ref[...] = r.astype(out_dtype)

    def spec(x, t0, t1, block_of):
        if isinstance(x, _View):
            return x.spec(t0, t1, block_of)
        return pl.BlockSpec((t0, t1), block_of)

    a_spec = spec(a, tk, tm, lambda i, j, k: (k, i)) if ta else spec(a, tm, tk, lambda i, j, k: (i, k))
    b_spec = spec(b, tn, tk, lambda i, j, k: (j, k)) if tb else spec(b, tk, tn, lambda i, j, k: (k, j))
    in_specs = [a_spec, b_spec]
    args = [a_arr, b_arr]
    if residual is not None:
        in_specs.append(pl.BlockSpec((tm, tn), lambda i, j, k: (i, j)))
        args.append(residual)
    aliases = {}
    if out is None:
        out_spec = pl.BlockSpec((tm, tn), lambda i, j, k: (i, j))
        out_shape = jax.ShapeDtypeStruct((M, N), out_dtype)
    else:
        out_spec = out.spec(tm, tn, lambda i, j, k: (i, j))
        out_shape = jax.ShapeDtypeStruct(out.buf.shape, out.buf.dtype)
        out_dtype = out.buf.dtype
        in_specs.append(pl.BlockSpec(memory_space=pl.ANY))
        args.append(out.buf)
        aliases = {len(args) - 1: 0}
    return pl.pallas_call(
        body, name=name, grid=(M // tm, N // tn, nk), in_specs=in_specs, out_specs=out_spec, out_shape=out_shape,
        scratch_shapes=[pltpu.VMEM((tm, tn), F32)], input_output_aliases=aliases,
        compiler_params=_params("parallel", "parallel", "arbitrary"),
    )(*args)


def _row_tile(T, C):
    tr = max(SUBLANES, min(T, ROW_BLOCK_ELEMS // C) // SUBLANES * SUBLANES)
    while T % tr:
        tr -= SUBLANES
    return tr


def _rows_call(body, name, T, C, row_ins, full_ins, row_outs, acc_outs=()):
    tr = _row_tile(T, C)
    row_spec = pl.BlockSpec((tr, C), lambda i: (i, 0))
    in_specs = [row_spec] * len(row_ins) + [pl.BlockSpec(f.shape, lambda i, n=f.ndim: (0,) * n) for f in full_ins]
    out_specs = [row_spec] * len(row_outs) + [pl.BlockSpec(s, lambda i, n=len(s): (0,) * n) for s in acc_outs]
    out_shape = [jax.ShapeDtypeStruct((T, C), d) for d in row_outs] + [jax.ShapeDtypeStruct(s, F32) for s in acc_outs]
    return pl.pallas_call(
        body, name=name, grid=(T // tr,), in_specs=in_specs, out_specs=out_specs, out_shape=out_shape,
        compiler_params=_params("arbitrary" if acc_outs else "parallel"),
    )(*row_ins, *full_ins)


def _rmsnorm_fwd(x, g, name):
    T, C = x.shape

    def body(x_ref, g_ref, h_ref):
        xv = x_ref[...]
        r = lax.rsqrt(jnp.mean(xv * xv, axis=-1, keepdims=True) + RMS_EPS)
        h_ref[...] = ((xv * r) * g_ref[...]).astype(BF16)

    return _rows_call(body, name, T, C, [x], [g.reshape(1, C)], [BF16])[0]


def _rmsnorm_bwd(x, g, dh, dres, name):
    T, C = x.shape

    def body(x_ref, dh_ref, dres_ref, g_ref, dx_ref, dxb_ref, dg_ref):
        @pl.when(pl.program_id(0) == 0)
        def _():
            dg_ref[...] = jnp.zeros_like(dg_ref)

        xv = x_ref[...]
        dhv = dh_ref[...]
        r = lax.rsqrt(jnp.mean(xv * xv, axis=-1, keepdims=True) + RMS_EPS)
        xn = xv * r
        dg_ref[...] += jnp.sum(dhv * xn, axis=0, keepdims=True)
        dxn = dhv * g_ref[...]
        dx = dres_ref[...] + r * (dxn - xn * jnp.mean(dxn * xn, axis=-1, keepdims=True))
        dx_ref[...] = dx
        dxb_ref[...] = dx.astype(BF16)

    return _rows_call(body, name, T, C, [x, dh, dres], [g.reshape(1, C)], [F32, BF16], [(1, C)])


def _qknorm_fwd(q, g, name):
    T, C = q.shape
    tr = min(T, 1024)

    def body(q_ref, g_ref, o_ref):
        xv = q_ref[...]
        r = lax.rsqrt(jnp.mean(xv * xv, axis=-1, keepdims=True) + RMS_EPS)
        o_ref[...] = ((xv * r) * g_ref[...]).astype(BF16)

    spec = pl.BlockSpec((tr, HEAD_DIM), lambda i, h: (i, h))
    return pl.pallas_call(
        body, name=name, grid=(T // tr, C // HEAD_DIM),
        in_specs=[spec, pl.BlockSpec((1, HEAD_DIM), lambda i, h: (0, 0))], out_specs=spec,
        out_shape=jax.ShapeDtypeStruct((T, C), BF16), compiler_params=_params("parallel", "parallel"),
    )(q, g.reshape(1, HEAD_DIM))


def _qknorm_bwd(q, g, dqn, name):
    T, C = q.shape
    tr = min(T, 1024)

    def body(q_ref, d_ref, g_ref, dq_ref, dg_ref):
        @pl.when((pl.program_id(0) == 0) & (pl.program_id(1) == 0))
        def _():
            dg_ref[...] = jnp.zeros_like(dg_ref)

        xv = q_ref[...]
        dv = d_ref[...]
        r = lax.rsqrt(jnp.mean(xv * xv, axis=-1, keepdims=True) + RMS_EPS)
        xn = xv * r
        dg_ref[...] += jnp.sum(dv * xn, axis=0, keepdims=True)
        dxn = dv * g_ref[...]
        dq_ref[...] = (r * (dxn - xn * jnp.mean(dxn * xn, axis=-1, keepdims=True))).astype(BF16)

    spec = pl.BlockSpec((tr, HEAD_DIM), lambda i, h: (i, h))
    gspec = pl.BlockSpec((1, HEAD_DIM), lambda i, h: (0, 0))
    return pl.pallas_call(
        body, name=name, grid=(T // tr, C // HEAD_DIM), in_specs=[spec, spec, gspec], out_specs=[spec, gspec],
        out_shape=[jax.ShapeDtypeStruct((T, C), BF16), jax.ShapeDtypeStruct((1, HEAD_DIM), F32)],
        compiler_params=_params("arbitrary", "arbitrary"),
    )(q, dqn, g.reshape(1, HEAD_DIM))


ATT_TQ = 512
ATT_TK = 256


def _logsig_pair(z):
    sp = jnp.log(1.0 + jnp.exp(-jnp.abs(z)))
    return jnp.minimum(z, 0.0) - sp, jnp.minimum(-z, 0.0) - sp


def _tri(n, strict_upper_src):
    j = lax.broadcasted_iota(jnp.int32, (n, n), 0)
    s = lax.broadcasted_iota(jnp.int32, (n, n), 1)
    if strict_upper_src == "gt":
        m = j > s
    elif strict_upper_src == "le":
        m = j <= s
    else:
        m = j < s
    return jnp.where(m, 1.0, 0.0).astype(BF16)


def _cumdot(x, tri):
    hi, lo = _split(x)
    return _dot(hi, tri) + _dot(lo, tri)


def _attn_tiles(S):
    tq, tk = min(ATT_TQ, S), min(ATT_TK, S)
    return tq, tk, S // tq, tq // tk


def _attn_fwd(qn, kn, v, Bl, S, rider=None):
    T, C = qn.shape
    H = C // HEAD_DIM
    tq, tk, nq, kpq = _attn_tiles(S)
    scale = 1.0 / math.sqrt(HEAD_DIM)

    def body(q_ref, k_ref, v_ref, o_ref, bt_ref):
        i = pl.program_id(2)
        tri = _tri(tk, "gt")
        rowpos = lax.broadcasted_iota(jnp.int32, (tq, tk), 0) + i * tq
        colpos = lax.broadcasted_iota(jnp.int32, (tq, tk), 1)
        o_ref[...] = jnp.zeros_like(o_ref)
        bt_ref[...] = jnp.zeros_like(bt_ref)
        nkb = (i + 1) * kpq

        def step(n, carry):
            j = nkb - 1 - n
            rows = pl.ds(pl.multiple_of(j * tk, tk), tk)
            z = _dot(q_ref[...], k_ref[rows, :], NT) * scale
            mask = (colpos + j * tk) < rowpos
            a, b = _logsig_pair(z)
            b = jnp.where(mask, b, 0.0)
            suffix = _cumdot(b, tri)
            w = jnp.where(mask, jnp.exp(a + suffix + bt_ref[...]), 0.0)
            o_ref[...] += _dot(w.astype(BF16), v_ref[rows, :])
            bt_ref[...] += jnp.sum(b, axis=-1, keepdims=True)
            return carry

        lax.fori_loop(0, nkb, step, 0)

    qspec = pl.BlockSpec((tq, HEAD_DIM), lambda b, h, i: (b * nq + i, h))
    kspec = pl.BlockSpec((S, HEAD_DIM), lambda b, h, i: (b, h))
    btspec = pl.BlockSpec((None, None, tq, 1), lambda b, h, i: (b, h, i, 0))
    grid = (Bl, H, nq)
    body, in_specs, out_specs, out_shape, scratch, extra = _ride(
        rider, body, grid, [qspec, kspec, kspec], [qspec, btspec],
        [jax.ShapeDtypeStruct((T, C), F32), jax.ShapeDtypeStruct((Bl, H, S, 1), F32)], [])
    return pl.pallas_call(
        body, name="attn_fwd", grid=grid, in_specs=in_specs, out_specs=out_specs, out_shape=out_shape,
        scratch_shapes=scratch, compiler_params=_params("arbitrary", "arbitrary", "arbitrary"),
    )(qn, kn, v, *extra)


def _attn_bwd(qn, kn, v, do, btot, Bl, S, rider=None):
    T, C = qn.shape
    H = C // HEAD_DIM
    tq, tk, nq, kpq = _attn_tiles(S)
    scale = 1.0 / math.sqrt(HEAD_DIM)

    def body(q_ref, k_ref, v_ref, do_ref, bt_ref, dq_ref, dk_ref, dv_ref, pb_ref, pdl_ref):
        i = pl.program_id(2)

        @pl.when(i == 0)
        def _():
            dk_ref[...] = jnp.zeros_like(dk_ref)
            dv_ref[...] = jnp.zeros_like(dv_ref)

        tri_le = _tri(tk, "le")
        tri_lt = _tri(tk, "lt")
        rowpos = lax.broadcasted_iota(jnp.int32, (tq, tk), 0) + i * tq
        colpos = lax.broadcasted_iota(jnp.int32, (tq, tk), 1)
        dq_ref[...] = jnp.zeros_like(dq_ref)
        pb_ref[...] = bt_ref[...]
        pdl_ref[...] = jnp.zeros_like(pdl_ref)

        def step(j, carry):
            rows = pl.ds(pl.multiple_of(j * tk, tk), tk)
            kj = k_ref[rows, :]
            z = _dot(q_ref[...], kj, NT) * scale
            mask = (colpos + j * tk) < rowpos
            a, b = _logsig_pair(z)
            b = jnp.where(mask, b, 0.0)
            suffix = pb_ref[...] - _cumdot(b, tri_le)
            w = jnp.where(mask, jnp.exp(a + suffix), 0.0)
            dl = _dot(do_ref[...], v_ref[rows, :], NT) * w
            prefix = pdl_ref[...] + _cumdot(dl, tri_lt)
            beta = jnp.exp(a)
            dzb = (jnp.where(mask, dl * (1.0 - beta) - beta * prefix, 0.0) * scale).astype(BF16)
            dq_ref[...] += _dot(dzb, kj)
            dk_ref[rows, :] += _dot(dzb, q_ref[...], TN)
            dv_ref[rows, :] += _dot(w.astype(BF16), do_ref[...], TN)
            pb_ref[...] -= jnp.sum(b, axis=-1, keepdims=True)
            pdl_ref[...] += jnp.sum(dl, axis=-1, keepdims=True)
            return carry

        lax.fori_loop(0, (i + 1) * kpq, step, 0)

    qspec = pl.BlockSpec((tq, HEAD_DIM), lambda b, h, i: (b * nq + i, h))
    kspec = pl.BlockSpec((S, HEAD_DIM), lambda b, h, i: (b, h))
    btspec = pl.BlockSpec((None, None, tq, 1), lambda b, h, i: (b, h, i, 0))
    grid = (Bl, H, nq)
    body, in_specs, out_specs, out_shape, scratch, extra = _ride(
        rider, body, grid, [qspec, kspec, kspec, qspec, btspec], [qspec, kspec, kspec],
        [jax.ShapeDtypeStruct((T, C), F32)] * 3, [pltpu.VMEM((tq, 1), F32), pltpu.VMEM((tq, 1), F32)])
    return pl.pallas_call(
        body, name="attn_bwd", grid=grid, in_specs=in_specs, out_specs=out_specs, out_shape=out_shape,
        scratch_shapes=scratch, compiler_params=_params("arbitrary", "arbitrary", "arbitrary"),
    )(qn, kn, v, do, btot, *extra)


SSM_TIME_BLOCK = 512
CHUNK = SUBLANES


def _cmadd(xr, xi, ar, ai, sr, si):
    return xr + ar * sr - ai * si, xi + ar * si + ai * sr


def _chunk_scan(xr, xi, tab_ref, cr, ci, reverse):
    for lvl, d in enumerate((1, 2, 4)):
        shift = (CHUNK - d) if reverse else d
        sr = pltpu.roll(xr, shift, 0)
        si = pltpu.roll(xi, shift, 0)
        ar = tab_ref[pl.ds((2 * lvl) * CHUNK, CHUNK), :]
        ai = tab_ref[pl.ds((2 * lvl + 1) * CHUNK, CHUNK), :]
        xr, xi = _cmadd(xr, xi, ar, ai, sr, si)
    pr = tab_ref[pl.ds(6 * CHUNK, CHUNK), :]
    pi = tab_ref[pl.ds(7 * CHUNK, CHUNK), :]
    return _cmadd(xr, xi, pr, pi, cr, ci)


def _ssm_dims(S, C):
    G = C // GROUP
    GT = min(16, G)
    return G, GT, G // GT, GT * GROUP, GT * STATE, min(SSM_TIME_BLOCK, S)


def _ssm_fwd(u, mats, Bl, S):
    T, C = u.shape
    G, GT, ngt, cw, sw, TB = _ssm_dims(S, C)
    ntb = S // TB
    nch = TB // CHUNK

    def body(u_ref, bre_ref, bim_ref, cre_ref, cim_ref, d_ref, tab_ref, y_ref, hr_ref, hi_ref, car_r, car_i):
        @pl.when(pl.program_id(2) == 0)
        def _():
            car_r[...] = jnp.zeros_like(car_r)
            car_i[...] = jnp.zeros_like(car_i)

        uv = u_ref[...]
        hr_ref[...] = _dot3(uv, bre_ref[...])
        hi_ref[...] = _dot3(uv, bim_ref[...])

        def step(n, carry):
            cr, ci = carry
            rows = pl.ds(pl.multiple_of(n * CHUNK, CHUNK), CHUNK)
            xr, xi = _chunk_scan(hr_ref[rows, :], hi_ref[rows, :], tab_ref, cr, ci, False)
            hr_ref[rows, :] = xr
            hi_ref[rows, :] = xi
            last = (CHUNK - 1, CHUNK)
            return (jnp.broadcast_to(xr[last[0]:last[1], :], xr.shape), jnp.broadcast_to(xi[last[0]:last[1], :], xi.shape))

        cr, ci = lax.fori_loop(0, nch, step, (car_r[...], car_i[...]))
        car_r[...] = cr
        car_i[...] = ci
        y_ref[...] = _dot3(hr_ref[...], cre_ref[...]) - _dot3(hi_ref[...], cim_ref[...]) + d_ref[...] * uv

    uspec = pl.BlockSpec((TB, cw), lambda g, b, t: (b * ntb + t, g))
    hspec = pl.BlockSpec((TB, sw), lambda g, b, t: (b * ntb + t, g))

    def gspec(r, c):
        return pl.BlockSpec((None, r, c), lambda g, b, t: (g, 0, 0))

    return pl.pallas_call(
        body, name="ssm_fwd", grid=(ngt, Bl, ntb),
        in_specs=[uspec, gspec(cw, sw), gspec(cw, sw), gspec(sw, cw), gspec(sw, cw), gspec(1, cw), gspec(8 * CHUNK, sw)],
        out_specs=[uspec, hspec, hspec],
        out_shape=[jax.ShapeDtypeStruct((T, C), F32), jax.ShapeDtypeStruct((T, G * STATE), F32),
                   jax.ShapeDtypeStruct((T, G * STATE), F32)],
        scratch_shapes=[pltpu.VMEM((CHUNK, sw), F32), pltpu.VMEM((CHUNK, sw), F32)],
        compiler_params=_params("parallel", "arbitrary", "arbitrary"),
    )(u, mats["bbd_re"], mats["bbd_im"], mats["cbd_re"], mats["cbd_im"], mats["d"], mats["tab_fwd"])


def _ssm_bwd(u, dy, h_re, h_im, mats, Bl, S):
    T, C = u.shape
    G, GT, ngt, cw, sw, TB = _ssm_dims(S, C)
    ntb = S // TB
    nch = TB // CHUNK
    rpb = TB // CHUNK

    def body(u_ref, dy_ref, hr_ref, hi_ref, hpr_ref, hpi_ref, ctr_ref, cti_ref, btr_ref, bti_ref, d_ref, tab_ref,
             du_ref, dbr_ref, dbi_ref, dcr_ref, dci_ref, dlr_ref, dli_ref, dd_ref, gr_ref, gi_ref, car_r, car_i):
        b = pl.program_id(1)
        t = pl.program_id(2)

        @pl.when((b == 0) & (t == 0))
        def _():
            for ref in (dbr_ref, dbi_ref, dcr_ref, dci_ref, dlr_ref, dli_ref, dd_ref):
                ref[...] = jnp.zeros_like(ref)

        @pl.when(t == 0)
        def _():
            car_r[...] = jnp.zeros_like(car_r)
            car_i[...] = jnp.zeros_like(car_i)

        uv = u_ref[...]
        dyv = dy_ref[...]
        gr_ref[...] = _dot3(dyv, ctr_ref[...])
        gi_ref[...] = -_dot3(dyv, cti_ref[...])
        alive = jnp.where(t == ntb - 1, 0.0, 1.0)
        row0 = lax.broadcasted_iota(jnp.int32, (CHUNK, sw), 0) == 0

        def step(m, carry):
            cr, ci, ar, ai = carry
            n = nch - 1 - m
            rows = pl.ds(pl.multiple_of(n * CHUNK, CHUNK), CHUNK)
            prow = pl.ds(pl.multiple_of(jnp.maximum(n - 1, 0) * CHUNK, CHUNK), CHUNK)
            xr, xi = _chunk_scan(gr_ref[rows, :], gi_ref[rows, :], tab_ref, cr, ci, True)
            gr_ref[rows, :] = xr
            gi_ref[rows, :] = xi
            first = n == 0
            pr = jnp.where(first, hpr_ref[...] * alive, hr_ref[prow, :])
            pi = jnp.where(first, hpi_ref[...] * alive, hi_ref[prow, :])
            sr = jnp.where(row0, pltpu.roll(pr, 1, 0), pltpu.roll(hr_ref[rows, :], 1, 0))
            si = jnp.where(row0, pltpu.roll(pi, 1, 0), pltpu.roll(hi_ref[rows, :], 1, 0))
            ar = ar + xr * sr + xi * si
            ai = ai + xi * sr - xr * si
            return (jnp.broadcast_to(xr[0:1, :], xr.shape), jnp.broadcast_to(xi[0:1, :], xi.shape), ar, ai)

        zero = jnp.zeros((CHUNK, sw), F32)
        cr, ci, ar, ai = lax.fori_loop(0, nch, step, (car_r[...], car_i[...], zero, zero))
        car_r[...] = cr
        car_i[...] = ci
        dlr_ref[...] += ar
        dli_ref[...] += ai
        gr = gr_ref[...]
        gi = gi_ref[...]
        dbr_ref[...] += _dot3(uv, gr, TN)
        dbi_ref[...] += _dot3(uv, gi, TN)
        dcr_ref[...] += _dot3(hr_ref[...], dyv, TN)
        dci_ref[...] -= _dot3(hi_ref[...], dyv, TN)
        dd_ref[...] += jnp.sum(dyv * uv, axis=0, keepdims=True)
        du_ref[...] = _dot3(gr, btr_ref[...]) + _dot3(gi, bti_ref[...]) + d_ref[...] * dyv

    def tblk(b, t):
        return b * ntb + (ntb - 1 - t)

    uspec = pl.BlockSpec((TB, cw), lambda g, b, t: (tblk(b, t), g))
    hspec = pl.BlockSpec((TB, sw), lambda g, b, t: (tblk(b, t), g))
    hpspec = pl.BlockSpec((CHUNK, sw), lambda g, b, t: (jnp.maximum(tblk(b, t) * rpb - 1, 0), g))

    def gspec(r, c):
        return pl.BlockSpec((None, r, c), lambda g, b, t: (g, 0, 0))

    def gshape(r, c):
        return jax.ShapeDtypeStruct((ngt, r, c), F32)

    return pl.pallas_call(
        body, name="ssm_bwd", grid=(ngt, Bl, ntb),
        in_specs=[uspec, uspec, hspec, hspec, hpspec, hpspec, gspec(cw, sw), gspec(cw, sw), gspec(sw, cw), gspec(sw, cw),
                  gspec(1, cw), gspec(8 * CHUNK, sw)],
        out_specs=[uspec, gspec(cw, sw), gspec(cw, sw), gspec(sw, cw), gspec(sw, cw), gspec(CHUNK, sw), gspec(CHUNK, sw),
                   gspec(1, cw)],
        out_shape=[jax.ShapeDtypeStruct((T, C), F32), gshape(cw, sw), gshape(cw, sw), gshape(sw, cw), gshape(sw, cw),
                   gshape(CHUNK, sw), gshape(CHUNK, sw), gshape(1, cw)],
        scratch_shapes=[pltpu.VMEM((TB, sw), F32), pltpu.VMEM((TB, sw), F32), pltpu.VMEM((CHUNK, sw), F32),
                        pltpu.VMEM((CHUNK, sw), F32)],
        compiler_params=_params("arbitrary", "arbitrary", "arbitrary"),
    )(u, dy, h_re, h_im, h_re, h_im, mats["cbdT_re"], mats["cbdT_im"], mats["bbdT_re"], mats["bbdT_im"], mats["d"],
      mats["tab_rev"])


def _zoh(a_re, a_im, log_dt, b_re, b_im):
    dt = jnp.exp(log_dt)[:, None]
    mag = jnp.exp(a_re * dt)
    l_re = mag * jnp.cos(a_im * dt)
    l_im = mag * jnp.sin(a_im * dt)
    den = a_re * a_re + a_im * a_im
    f_re = ((l_re - 1.0) * a_re + l_im * a_im) / den
    f_im = (l_im * a_re - (l_re - 1.0) * a_im) / den
    bb_re = f_re[..., None] * b_re - f_im[..., None] * b_im
    bb_im = f_re[..., None] * b_im + f_im[..., None] * b_re
    return l_re, l_im, bb_re, bb_im


def _ssm_matrices(a_re, a_im, log_dt, b_re, b_im, c_re, c_im, d, S):
    G = a_re.shape[0]
    _, GT, ngt, cw, sw, _ = _ssm_dims(S, G * GROUP)
    _, _, bb_re, bb_im = _zoh(a_re, a_im, log_dt, b_re, b_im)
    eye = jnp.eye(GT, dtype=F32)

    def bd_b(bb):
        return jnp.einsum("tgpi,gh->tgihp", bb.reshape(ngt, GT, STATE, GROUP), eye).reshape(ngt, cw, sw)

    def bd_c(c):
        return jnp.einsum("tgip,gh->tgphi", c.reshape(ngt, GT, GROUP, STATE), eye).reshape(ngt, sw, cw)

    dt = jnp.exp(log_dt)[:, None]

    def power(k, conj):
        mag = jnp.exp(k * a_re * dt)
        ang = k * a_im * dt
        return (mag * jnp.cos(ang)).reshape(ngt, 1, sw), ((-1.0 if conj else 1.0) * mag * jnp.sin(ang)).reshape(ngt, 1, sw)

    r = jnp.arange(CHUNK)[None, :, None]

    def table(reverse):
        parts = []
        for dd in (1, 2, 4):
            pr, pi = power(float(dd), reverse)
            keep = (r <= CHUNK - 1 - dd) if reverse else (r >= dd)
            parts += [jnp.where(keep, pr, 0.0), jnp.where(keep, pi, 0.0)]
        exps = [(CHUNK - k) if reverse else (k + 1) for k in range(CHUNK)]
        pw = [power(float(e), reverse) for e in exps]
        parts += [jnp.concatenate([p[0] for p in pw], axis=1), jnp.concatenate([p[1] for p in pw], axis=1)]
        return jnp.concatenate([jnp.broadcast_to(p, (ngt, CHUNK, sw)) for p in parts], axis=1)

    mats = dict(bbd_re=bd_b(bb_re), bbd_im=bd_b(bb_im), cbd_re=bd_c(c_re), cbd_im=bd_c(c_im),
                d=d.reshape(ngt, 1, cw), tab_fwd=table(False), tab_rev=table(True))
    for k in ("bbd_re", "bbd_im", "cbd_re", "cbd_im"):
        mats[k.replace("bd_", "bdT_")] = jnp.swapaxes(mats[k], 1, 2)
    return mats


def _ssm_unblock(dbr, dbi, dcr, dci, dlr, dli, dd, G):
    ngt = dbr.shape[0]
    GT = G // ngt
    eye = jnp.eye(GT, dtype=F32)

    def ub(x):
        return jnp.einsum("tgihp,gh->tgpi", x.reshape(ngt, GT, GROUP, GT, STATE), eye).reshape(G, STATE, GROUP)

    def uc(x):
        return jnp.einsum("tgphi,gh->tgip", x.reshape(ngt, GT, STATE, GT, GROUP), eye).reshape(G, GROUP, STATE)

    return (dlr.sum(axis=1).reshape(G, STATE), dli.sum(axis=1).reshape(G, STATE), ub(dbr), ub(dbi), uc(dcr), uc(dci),
            dd.reshape(G * GROUP))


def _attn_gate_fwd(o, gate):
    T, C = o.shape

    def body(o_ref, g_ref, out_ref):
        out_ref[...] = (o_ref[...] * _silu_parts(g_ref[...])[0]).astype(BF16)

    return _rows_call(body, "attn_gate_fwd", T, C, [o, gate], [], [BF16])[0]


def _attn_gate_bwd(dog, o, gate):
    T, C = o.shape

    def body(d_ref, o_ref, g_ref, do_ref, dg_ref):
        val, der = _silu_parts(g_ref[...])
        dv = d_ref[...]
        do_ref[...] = (dv * val).astype(BF16)
        dg_ref[...] = (dv * o_ref[...] * der).astype(BF16)

    return _rows_call(body, "attn_gate_bwd", T, C, [dog, o, gate], [], [BF16, BF16])


def _gelu_fwd(y):
    T, C = y.shape

    def body(y_ref, out_ref):
        out_ref[...] = _gelu_parts(y_ref[...])[0].astype(BF16)

    return _rows_call(body, "gelu_fwd", T, C, [y], [], [BF16])[0]


def _glu_fwd(y, gl, gate, glu_b):
    T, C = y.shape

    def body(y_ref, gl_ref, g_ref, b_ref, out_ref):
        yg = _gelu_parts(y_ref[...])[0]
        sg = _sigmoid(gl_ref[...] + b_ref[...])
        out_ref[...] = (yg * sg * _silu_parts(g_ref[...])[0]).astype(BF16)

    return _rows_call(body, "glu_fwd", T, C, [y, gl, gate], [glu_b.reshape(1, C)], [BF16])[0]


def _glu_bwd(dy3, y, gl, gate, glu_b):
    T, C = y.shape

    def body(d_ref, y_ref, gl_ref, g_ref, b_ref, dgl_ref, dgate_ref, t1_ref, db_ref):
        @pl.when(pl.program_id(0) == 0)
        def _():
            db_ref[...] = jnp.zeros_like(db_ref)

        yg = _gelu_parts(y_ref[...])[0]
        sg = _sigmoid(gl_ref[...] + b_ref[...])
        sl, sld = _silu_parts(g_ref[...])
        dv = d_ref[...]
        dy2 = dv * sl
        dgl = dy2 * yg * sg * (1.0 - sg)
        dgl_ref[...] = dgl.astype(BF16)
        dgate_ref[...] = (dv * (yg * sg) * sld).astype(BF16)
        t1_ref[...] = dy2 * sg
        db_ref[...] += jnp.sum(dgl, axis=0, keepdims=True)

    return _rows_call(body, "glu_bwd", T, C, [dy3, y, gl, gate], [glu_b.reshape(1, C)], [BF16, BF16, F32], [(1, C)])


def _gelu_bwd(t1, t2, y):
    T, C = y.shape

    def body(a_ref, b_ref, y_ref, out_ref):
        out_ref[...] = (a_ref[...] + b_ref[...]) * _gelu_parts(y_ref[...])[1]

    return _rows_call(body, "gelu_bwd", T, C, [t1, t2, y], [], [F32])[0]


def _loss_head(x2, target):
    T, C = x2.shape

    def body(x_ref, t_ref, d_ref, db_ref, l_ref):
        @pl.when(pl.program_id(0) == 0)
        def _():
            l_ref[...] = jnp.zeros_like(l_ref)

        e = x_ref[...] - t_ref[...]
        d = e * (1.0 / C)
        d_ref[...] = d
        db_ref[...] = d.astype(BF16)
        l_ref[...] += 0.5 * jnp.sum(jnp.sum(e * e, axis=-1, keepdims=True) * (1.0 / C), axis=0, keepdims=True)

    return _rows_call(body, "loss_head", T, C, [x2, target], [], [F32, BF16], [(1, 1)])


def _adamw(w, g, m, v, name):
    shape = w.shape
    C = shape[-1]
    R = w.size // C
    bc1 = 1.0 - ADAM_B1 ** ADAM_STEP
    bc2 = 1.0 - ADAM_B2 ** ADAM_STEP

    def body(w_ref, g_ref, m_ref, v_ref, d_ref, nm_ref, nv_ref):
        gv = g_ref[...]
        mn = ADAM_B1 * m_ref[...] + (1.0 - ADAM_B1) * gv
        vn = ADAM_B2 * v_ref[...] + (1.0 - ADAM_B2) * (gv * gv)
        d_ref[...] = -ADAM_LR * ((mn / bc1) / (jnp.sqrt(vn / bc2) + ADAM_EPS) + ADAM_WD * w_ref[...])
        nm_ref[...] = mn
        nv_ref[...] = vn

    outs = _rows_call(body, name, R, C, [a.reshape(R, C) for a in (w, g, m, v)], [], [F32, F32, F32])
    return [o.reshape(shape) for o in outs]


def _sum_leading(x, name):
    n, R, C = x.shape
    tr = _row_tile(R, C * n)

    def body(x_ref, o_ref):
        acc = x_ref[0].astype(F32)
        for k in range(1, n):
            acc = acc + x_ref[k].astype(F32)
        o_ref[...] = acc

    return pl.pallas_call(
        body, name=name, grid=(R // tr,), in_specs=[pl.BlockSpec((n, tr, C), lambda i: (0, i, 0))],
        out_specs=pl.BlockSpec((tr, C), lambda i: (i, 0)), out_shape=jax.ShapeDtypeStruct((R, C), F32),
        compiler_params=_params("parallel"),
    )(x)


def _add_halves(g, c, name):
    full, recv = g
    n, R, C = full.shape
    half = R // 2
    tr = _row_tile(half, C)
    nb = half // tr

    def body(c_ref, a_ref, b_ref, o_ref):
        o_ref[...] = (a_ref[...] + b_ref[...]).astype(BF16)

    grid_spec = pltpu.PrefetchScalarGridSpec(
        num_scalar_prefetch=1, grid=(n, nb),
        in_specs=[pl.BlockSpec((None, tr, C), lambda j, i, c_ref: (j, c_ref[0] * nb + i, 0)),
                  pl.BlockSpec((None, tr, C), lambda j, i, c_ref: (j, i, 0))],
        out_specs=pl.BlockSpec((None, tr, C), lambda j, i, c_ref: (j, i, 0)))
    return pl.pallas_call(
        body, name=name, grid_spec=grid_spec, out_shape=jax.ShapeDtypeStruct((n, half, C), BF16),
        compiler_params=_params("parallel", "parallel"),
    )(c.reshape(1).astype(jnp.int32), full, recv)


ANY = pl.BlockSpec(memory_space=pl.ANY)


def _position():
    return lax.axis_index("x"), lax.axis_index("y"), lax.axis_index("c")


def _all_gather8(blk, name):
    return _standalone(_all_gather_rider(blk), name)[0]


class _Rider:
    def __init__(self, arrays, out_shapes, sems, start, finish):
        self.arrays, self.out_shapes, self.sems, self.start, self.finish = arrays, out_shapes, sems, start, finish


def _standalone(rider, name):
    def body(*refs):
        rider.start(*refs)
        rider.finish(*refs)

    return pl.pallas_call(
        body, name=name, in_specs=[ANY] * len(rider.arrays), out_specs=[ANY] * len(rider.out_shapes),
        out_shape=rider.out_shapes, scratch_shapes=rider.sems,
    )(*rider.arrays)


def _ride(rider, body, grid, in_specs, out_specs, out_shape, scratch):
    if rider is None:
        return body, in_specs, out_specs, out_shape, scratch, []
    ni, no, ns = len(in_specs), len(out_specs), len(scratch)
    ri, ro = len(rider.arrays), len(rider.out_shapes)

    def full(*refs):
        ins, refs = refs[:ni], refs[ni:]
        r_ins, refs = refs[:ri], refs[ri:]
        outs, refs = refs[:no], refs[no:]
        r_outs, refs = refs[:ro], refs[ro:]
        scr, r_sems = refs[:ns], refs[ns:]
        ids = [pl.program_id(a) for a in range(len(grid))]
        first = functools.reduce(jnp.logical_and, [i == 0 for i in ids])
        last = functools.reduce(jnp.logical_and, [i == g - 1 for i, g in zip(ids, grid)])

        @pl.when(first)
        def _():
            rider.start(*r_ins, *r_outs, *r_sems)

        body(*ins, *outs, *scr)

        @pl.when(last)
        def _():
            rider.finish(*r_ins, *r_outs, *r_sems)

    return (full, in_specs + [ANY] * ri, out_specs + [ANY] * ro, out_shape + rider.out_shapes, scratch + rider.sems,
            rider.arrays)


def _all_gather_rider(blk):
    M, N = blk.shape

    def copies(x_ref, out_ref, send_sems, recv_sems, local_sem):
        x, y, c = _position()
        me, sibling = (x, y, c), (x, y, 1 - c)
        chips = [(1 - x, y), (x, 1 - y), (1 - x, 1 - y)]

        def slab(px, py, pc):
            return out_ref.at[4 * px + 2 * py + pc]

        def copy(k, block, to, src=None):
            return pltpu.make_async_remote_copy(
                src_ref=slab(*block) if src is None else src, dst_ref=slab(*block),
                send_sem=send_sems.at[k], recv_sem=recv_sems.at[k], device_id=to, device_id_type=MESH)

        mine = pltpu.make_async_copy(x_ref, slab(*me), local_sem)
        first = [copy(0, me, sibling, src=x_ref)]
        first += [copy(1 + j, me, (*chip, c), src=x_ref) for j, chip in enumerate(chips)]
        passed = [copy(4 + j, (*chip, c), sibling) for j, chip in enumerate(chips)]
        arrivals = [copy(1 + j, (*chip, c), me) for j, chip in enumerate(chips)]
        from_sibling = [copy(0, sibling, me)] + [copy(4 + j, (*chip, 1 - c), me) for j, chip in enumerate(chips)]
        return mine, first, passed, arrivals, from_sibling

    def start(*refs):
        mine, first, _, _, _ = copies(*refs)
        mine.start()
        for cp in first:
            cp.start()

    def finish(*refs):
        mine, first, passed, arrivals, from_sibling = copies(*refs)
        for arrival, onward in zip(arrivals, passed):
            arrival.wait_recv()
            onward.start()
        for cp in from_sibling:
            cp.wait_recv()
        for cp in first + passed:
            cp.wait_send()
        mine.wait()

    return _Rider([blk], [jax.ShapeDtypeStruct((N_DEV, M, N), blk.dtype)],
                  [pltpu.SemaphoreType.DMA((7,)), pltpu.SemaphoreType.DMA((7,)), pltpu.SemaphoreType.DMA], start, finish)


def _sibling_send_half(g, name):
    n, R, C = g.shape
    half = R // 2

    def body(g_ref, out_ref, send_sem, recv_sem):
        x, y, c = _position()
        src = g_ref.at[:, pl.ds((1 - c) * half, half), :]
        cp = pltpu.make_async_remote_copy(src_ref=src, dst_ref=out_ref, send_sem=send_sem, recv_sem=recv_sem,
                                          device_id=(x, y, 1 - c), device_id_type=MESH)
        cp.start()
        cp.wait()

    return pl.pallas_call(
        body, name=name, in_specs=[ANY], out_specs=ANY, out_shape=jax.ShapeDtypeStruct((n, half, C), g.dtype),
        scratch_shapes=[pltpu.SemaphoreType.DMA, pltpu.SemaphoreType.DMA],
    )(g)


def _chip_exchange(p, name):
    return _standalone(_chip_exchange_rider(p), name)[0]


def _chip_exchange_rider(p):
    def copies(p_ref, out_ref, send_sems, recv_sems, local_sem):
        x, y, c = _position()
        my = 2 * x + y
        chips = [(1 - x, y), (x, 1 - y), (1 - x, 1 - y)]
        mine = pltpu.make_async_copy(p_ref.at[my], out_ref.at[my], local_sem)
        sends = [pltpu.make_async_remote_copy(
            src_ref=p_ref.at[2 * px + py], dst_ref=out_ref.at[my], send_sem=send_sems.at[k], recv_sem=recv_sems.at[k],
            device_id=(px, py, c), device_id_type=MESH) for k, (px, py) in enumerate(chips)]
        arrivals = [pltpu.make_async_remote_copy(
            src_ref=p_ref.at[my], dst_ref=out_ref.at[2 * px + py], send_sem=send_sems.at[k], recv_sem=recv_sems.at[k],
            device_id=(px, py, c), device_id_type=MESH) for k, (px, py) in enumerate(chips)]
        return mine, sends, arrivals

    def start(*refs):
        mine, sends, _ = copies(*refs)
        mine.start()
        for cp in sends:
            cp.start()

    def finish(*refs):
        mine, sends, arrivals = copies(*refs)
        for cp in arrivals:
            cp.wait_recv()
        for cp in sends:
            cp.wait_send()
        mine.wait()

    return _Rider([p], [jax.ShapeDtypeStruct(p.shape, p.dtype)],
                  [pltpu.SemaphoreType.DMA((3,)), pltpu.SemaphoreType.DMA((3,)), pltpu.SemaphoreType.DMA], start, finish)


def _sibling_swap(f, name):
    def body(f_ref, out_ref, send_sem, recv_sem):
        x, y, c = _position()
        cp = pltpu.make_async_remote_copy(src_ref=f_ref, dst_ref=out_ref, send_sem=send_sem, recv_sem=recv_sem,
                                          device_id=(x, y, 1 - c), device_id_type=MESH)
        cp.start()
        cp.wait()

    return pl.pallas_call(
        body, name=name, in_specs=[ANY], out_specs=ANY, out_shape=jax.ShapeDtypeStruct(f.shape, f.dtype),
        scratch_shapes=[pltpu.SemaphoreType.DMA, pltpu.SemaphoreType.DMA],
    )(f)


def _pack_rest(attn_out, ssm_in, glu_w, ssm_out):
    hd = ssm_in.shape[0] // 2
    return jnp.concatenate([attn_out, jnp.concatenate([ssm_in[:hd], ssm_in[hd:]], axis=1), glu_w, ssm_out], axis=0)


def _unpack_rest(p):
    D = p.shape[-1]
    q, hd = D // N_CHIPS, D // 2
    o = [0, q, q + hd, 2 * q + hd, 3 * q + hd]
    ssm_in = p[o[1]:o[2]]
    return p[o[0]:o[1]], jnp.concatenate([ssm_in[:, :hd], ssm_in[:, hd:]], axis=0), p[o[2]:o[3]], p[o[3]:o[4]]


def _attn_in_views(buf_a):
    D = buf_a.shape[-1]
    return [_View(buf_a, (D, D), lambda r, c, j=j: (j, r, c), D, D) for j in range(N_CHIPS)]


def _rest_views(buf_b):
    D = buf_b.shape[-1]
    q, hd = D // N_CHIPS, D // 2
    o_out, o_in, o_glu, o_sout = 0, q, q + hd, 2 * q + hd

    def row_sharded(off):
        return _View(buf_b, (D, D), lambda r, c: (r // q, off + r % q, c), math.gcd(q, off), D)

    def ssm_in(part):
        return _View(buf_b, (D, D), lambda r, c: (2 * part + c // hd, o_in + r % hd, (r // hd) * hd + c % hd),
                     math.gcd(hd, o_in), hd)

    return dict(attn_out=row_sharded(o_out), ssm_in_u=ssm_in(0), ssm_in_gate=ssm_in(1), glu=row_sharded(o_glu),
                ssm_out=row_sharded(o_sout))


SMALL_REPL = ("norm_g", "attn_q_g", "attn_k_g")


def _pack_small(parts):
    flat = jnp.concatenate([p.reshape(-1) for p in parts])
    pad = (-flat.size) % (2 * SUBLANES * LANES)
    return jnp.pad(flat, (0, pad)).reshape(-1, LANES)


def _unpack_small(buf, shapes):
    flat = buf.reshape(-1)
    out, off = [], 0
    for s in shapes:
        n = math.prod(s)
        out.append(flat[off:off + n].reshape(s))
        off += n
    return out


def _local_step(x, target, norm_g, q_g, k_g, wa, wb, ssm_small, core=None):
    Bl, S, D = x.shape
    T = Bl * S
    x0 = x.reshape(T, D)
    tgt = target.reshape(T, D)
    a_re, a_im, log_dt, b_re, b_im, c_re, c_im, d_skip, glu_b = ssm_small
    G = a_re.shape[0]
    mats = _ssm_matrices(a_re, a_im, log_dt, b_re, b_im, c_re, c_im, d_skip, S)
    w_in = _attn_in_views(wa)
    rows_b = wb.shape[-2] * (1 if core is None else 2)
    ga = lax.empty(wa.shape, F32)
    gb = lax.empty((N_CHIPS, rows_b, D), F32)

    def wgrad(a, b, key, name):
        return _matmul(a, b, name=name, ta=True, out=_rest_views(gb)[key])

    h0 = _rmsnorm_fwd(x0, norm_g[0], "norm0_fwd")
    q, k, v, gate = [_matmul(h0, w_in[j], name=f"attn_in_{j}", out_dtype=(BF16 if j == 2 else F32)) for j in range(4)]
    qn = _qknorm_fwd(q, q_g, "qnorm_fwd")
    kn = _qknorm_fwd(k, k_g, "knorm_fwd")
    if core is None:
        o, btot = _attn_fwd(qn, kn, v, Bl, S)
    else:
        o, btot, wb = _attn_fwd(qn, kn, v, Bl, S, rider=_all_gather_rider(wb))
        wb = wb.reshape(N_CHIPS, rows_b, D)
    w = _rest_views(wb)
    og = _attn_gate_fwd(o, gate)
    x1 = _matmul(og, w["attn_out"], name="attn_out", residual=x0)

    h1 = _rmsnorm_fwd(x1, norm_g[1], "norm1_fwd")
    u = _matmul(h1, w["ssm_in_u"], name="ssm_in_u")
    gate2 = _matmul(h1, w["ssm_in_gate"], name="ssm_in_gate")
    y, hs_re, hs_im = _ssm_fwd(u, mats, Bl, S)
    yg = _gelu_fwd(y)
    gl = _matmul(yg, w["glu"], name="glu_mm")
    y3 = _glu_fwd(y, gl, gate2, glu_b)
    x2 = _matmul(y3, w["ssm_out"], name="ssm_out", residual=x1)

    dx2, dx2b, loss = _loss_head(x2, tgt)

    dy3 = _matmul(dx2b, w["ssm_out"], name="ssm_out_dgrad", tb=True)
    gb = wgrad(y3, dx2b, "ssm_out", "ssm_out_wgrad")
    dgl, dgate2, t1, dglu_b = _glu_bwd(dy3, y, gl, gate2, glu_b)
    t2 = _matmul(dgl, w["glu"], name="glu_dgrad", tb=True)
    gb = wgrad(yg, dgl, "glu", "glu_wgrad")
    dy = _gelu_bwd(t1, t2, y)
    du, dbr, dbi, dcr, dci, dlr, dli, dd = _ssm_bwd(u, dy, hs_re, hs_im, mats, Bl, S)
    dh1 = _matmul(du, w["ssm_in_u"], name="ssm_in_dgrad_u", tb=True)
    dh1 = _matmul(dgate2, w["ssm_in_gate"], name="ssm_in_dgrad_gate", tb=True, residual=dh1)
    gb = wgrad(h1, du, "ssm_in_u", "ssm_in_wgrad_u")
    gb = wgrad(h1, dgate2, "ssm_in_gate", "ssm_in_wgrad_gate")
    dx1, dx1b, dng1 = _rmsnorm_bwd(x1, norm_g[1], dh1, dx2, "norm1_bwd")

    dog = _matmul(dx1b, w["attn_out"], name="attn_out_dgrad", tb=True)
    gb = wgrad(og, dx1b, "attn_out", "attn_out_wgrad")
    do, dgate = _attn_gate_bwd(dog, o, gate)
    if core is None:
        dqn, dkn, dv = _attn_bwd(qn, kn, v, do, btot, Bl, S)
    else:
        chip_sum_b = _add_halves((gb, _sibling_send_half(gb, "grads_b_sibling_half")), core, "grads_b_add_halves")
        dqn, dkn, dv, gb = _attn_bwd(qn, kn, v, do, btot, Bl, S, rider=_chip_exchange_rider(chip_sum_b))
    dq, dqg = _qknorm_bwd(q, q_g, dqn, "qnorm_bwd")
    dk, dkg = _qknorm_bwd(k, k_g, dkn, "knorm_bwd")
    dproj = [dq, dk, dv, dgate]
    dh0 = None
    for j in range(4):
        dh0 = _matmul(dproj[j], w_in[j], name=f"attn_in_dgrad_{j}", tb=True, residual=dh0)
        ga = _matmul(h0, dproj[j], name=f"attn_in_wgrad_{j}", ta=True, out=_attn_in_views(ga)[j])
    dx0, _, dng0 = _rmsnorm_bwd(x0, norm_g[0], dh0, dx1, "norm0_bwd")

    small = (jnp.concatenate([dng0, dng1], axis=0), dqg, dkg) + _ssm_unblock(dbr, dbi, dcr, dci, dlr, dli, dd, G) + (
        dglu_b.reshape(D),)
    return loss, dx0.reshape(Bl, S, D), ga, gb, small


def _chip_rows(a, chip, n_per):
    return lax.dynamic_slice_in_dim(a, chip * n_per, n_per, axis=0)


def kernel(x, norm_g, attn_w_in, attn_q_g, attn_k_g, attn_w_out, ssm_w_in, ssm_A_re, ssm_A_im, ssm_log_dt, ssm_B_re, ssm_B_im, ssm_C_re, ssm_C_im, ssm_D, ssm_glu_w, ssm_glu_b, ssm_w_out, loss_target, m_norm_g, m_attn_w_in, m_attn_q_g, m_attn_k_g, m_attn_w_out, m_ssm_w_in, m_ssm_A_re, m_ssm_A_im, m_ssm_log_dt, m_ssm_B_re, m_ssm_B_im, m_ssm_C_re, m_ssm_C_im, m_ssm_D, m_ssm_glu_w, m_ssm_glu_b, m_ssm_w_out, v_norm_g, v_attn_w_in, v_attn_q_g, v_attn_k_g, v_attn_w_out, v_ssm_w_in, v_ssm_A_re, v_ssm_A_im, v_ssm_log_dt, v_ssm_B_re, v_ssm_B_im, v_ssm_C_re, v_ssm_C_im, v_ssm_D, v_ssm_glu_w, v_ssm_glu_b, v_ssm_w_out):
    D = x.shape[-1]
    cx, cy, cc = _position()
    chip = 2 * cx + cy
    G = D // GROUP
    Gl = G // N_CHIPS

    def my_half(a):
        return lax.dynamic_slice_in_dim(a, cc * (a.shape[0] // 2), a.shape[0] // 2, axis=0)

    wa = _all_gather8(my_half(attn_w_in[0].astype(BF16)), "attn_in_all_gather").reshape(N_CHIPS, D, D)
    wb_half = my_half(_pack_rest(attn_w_out[0], ssm_w_in[0], ssm_glu_w[0], ssm_w_out[0]).astype(BF16))

    ssm_local = [ssm_A_re[0], ssm_A_im[0], ssm_log_dt[0], ssm_B_re[0], ssm_B_im[0], ssm_C_re[0], ssm_C_im[0], ssm_D[0],
                 ssm_glu_b[0]]
    small_local = _pack_small(ssm_local)
    half_rows = small_local.shape[0] // 2
    small_half = lax.dynamic_slice_in_dim(small_local, cc * half_rows, half_rows, axis=0)
    small_all = _all_gather8(small_half, "ssm_params_all_gather").reshape(N_CHIPS, 2 * half_rows, LANES)
    per_chip = [_unpack_small(small_all[j], [p.shape for p in ssm_local]) for j in range(N_CHIPS)]
    ssm_full = [jnp.concatenate([per_chip[j][i] for j in range(N_CHIPS)], axis=0) for i in range(len(ssm_local))]

    loss, grad_x, ga, gathered_b, small = _local_step(x, loss_target, norm_g, attn_q_g[0], attn_k_g[0], wa, wb_half,
                                                       ssm_full, core=cc)
    loss = lax.psum(loss[0, 0], ("x", "y", "c"))

    q4 = D // N_CHIPS
    chip_sum_a = _add_halves((ga, _sibling_send_half(ga, "grads_a_sibling_half")), cc, "grads_a_add_halves")
    gathered_a = _chip_exchange(chip_sum_a, "grads_a_chip_exchange")
    south = cc == 0

    def both_halves(gathered, tag):
        mine = _sum_leading(gathered, f"grads_{tag}_sum_chips")
        other = _sibling_swap(mine, f"grads_{tag}_sibling_swap")
        return jnp.concatenate([jnp.where(south, mine, other), jnp.where(south, other, mine)], axis=0)

    gd = (both_halves(gathered_a, "a"),) + _unpack_rest(both_halves(gathered_b, "b"))

    small_shapes = [s.shape for s in small]
    small_sum = _sum_leading(_all_gather8(_pack_small(small), "small_grads_all_gather"), "small_grads_sum")
    (dng, dqg, dkg, dl_re, dl_im, dbb_re, dbb_im, dc_re, dc_im, dd_skip, dglu_b) = _unpack_small(small_sum, small_shapes)
    a_re, a_im, log_dt, b_re, b_im = ssm_local[:5]
    _, zoh_vjp = jax.vjp(_zoh, a_re, a_im, log_dt, b_re, b_im)
    da_re, da_im, dlog_dt, db_re, db_im = zoh_vjp((_chip_rows(dl_re, chip, Gl), _chip_rows(dl_im, chip, Gl),
                                                   _chip_rows(dbb_re, chip, Gl), _chip_rows(dbb_im, chip, Gl)))
    grads = {
        "norm_g": dng, "attn_w_in": gd[0][None], "attn_q_g": dqg, "attn_k_g": dkg, "attn_w_out": gd[1][None],
        "ssm_w_in": gd[2][None], "ssm_A_re": da_re[None], "ssm_A_im": da_im[None], "ssm_log_dt": dlog_dt[None],
        "ssm_B_re": db_re[None], "ssm_B_im": db_im[None], "ssm_C_re": _chip_rows(dc_re, chip, Gl)[None],
        "ssm_C_im": _chip_rows(dc_im, chip, Gl)[None], "ssm_D": _chip_rows(dd_skip, chip, q4)[None],
        "ssm_glu_w": gd[3][None], "ssm_glu_b": _chip_rows(dglu_b, chip, q4)[None], "ssm_w_out": gd[4][None],
    }
    weights = dict(norm_g=norm_g, attn_w_in=attn_w_in, attn_q_g=attn_q_g, attn_k_g=attn_k_g, attn_w_out=attn_w_out,
                   ssm_w_in=ssm_w_in, ssm_A_re=ssm_A_re, ssm_A_im=ssm_A_im, ssm_log_dt=ssm_log_dt, ssm_B_re=ssm_B_re,
                   ssm_B_im=ssm_B_im, ssm_C_re=ssm_C_re, ssm_C_im=ssm_C_im, ssm_D=ssm_D, ssm_glu_w=ssm_glu_w,
                   ssm_glu_b=ssm_glu_b, ssm_w_out=ssm_w_out)
    m = dict(norm_g=m_norm_g, attn_w_in=m_attn_w_in, attn_q_g=m_attn_q_g, attn_k_g=m_attn_k_g, attn_w_out=m_attn_w_out,
             ssm_w_in=m_ssm_w_in, ssm_A_re=m_ssm_A_re, ssm_A_im=m_ssm_A_im, ssm_log_dt=m_ssm_log_dt, ssm_B_re=m_ssm_B_re,
             ssm_B_im=m_ssm_B_im, ssm_C_re=m_ssm_C_re, ssm_C_im=m_ssm_C_im, ssm_D=m_ssm_D, ssm_glu_w=m_ssm_glu_w,
             ssm_glu_b=m_ssm_glu_b, ssm_w_out=m_ssm_w_out)
    v = dict(norm_g=v_norm_g, attn_w_in=v_attn_w_in, attn_q_g=v_attn_q_g, attn_k_g=v_attn_k_g, attn_w_out=v_attn_w_out,
             ssm_w_in=v_ssm_w_in, ssm_A_re=v_ssm_A_re, ssm_A_im=v_ssm_A_im, ssm_log_dt=v_ssm_log_dt, ssm_B_re=v_ssm_B_re,
             ssm_B_im=v_ssm_B_im, ssm_C_re=v_ssm_C_re, ssm_C_im=v_ssm_C_im, ssm_D=v_ssm_D, ssm_glu_w=v_ssm_glu_w,
             ssm_glu_b=v_ssm_glu_b, ssm_w_out=v_ssm_w_out)
    names = list(weights)
    dense_names = ("attn_w_in", "attn_w_out", "ssm_w_in", "ssm_glu_w", "ssm_w_out")
    delta, new_m, new_v = {}, {}, {}
    for n in dense_names:
        delta[n], new_m[n], new_v[n] = _adamw(weights[n], grads[n], m[n], v[n], "adamw_" + n)
    small_names = [n for n in names if n not in dense_names]
    small_shapes = [weights[n].shape for n in small_names]
    packs = [_pack_small([d[n] for n in small_names]) for d in (weights, grads, m, v)]
    outs = _adamw(*packs, "adamw_small")
    for res, out in zip((delta, new_m, new_v), outs):
        for n, val in zip(small_names, _unpack_small(out, small_shapes)):
            res[n] = val
    return (loss, grad_x, *[grads[n] for n in names], *[delta[n] for n in names], *[new_m[n] for n in names],
            *[new_v[n] for n in names])
```

```python
import functools
import math

import jax
import jax.numpy as jnp
from jax import lax
from jax.experimental import pallas as pl
from jax.experimental.pallas import tpu as pltpu

F32 = jnp.float32
BF16 = jnp.bfloat16

HEAD_DIM = 128
GROUP = 16
STATE = 64
RMS_EPS = 1e-6
ADAM_LR = 0.001
ADAM_B1 = 0.9
ADAM_B2 = 0.999
ADAM_EPS = 1e-08
ADAM_WD = 0.01
ADAM_STEP = 10

N_CHIPS = 4
N_DEV = 8
SUBLANES = 8
LANES = 128
VMEM_LIMIT = 56 * 1024 * 1024
ROW_BLOCK_ELEMS = 1 << 18
MATMUL_TILE = 1024
MATMUL_PANEL_BYTES = 8 * 1024 * 1024
MESH = pl.DeviceIdType.MESH

NN = (((1,), (0,)), ((), ()))
NT = (((1,), (1,)), ((), ()))
TN = (((0,), (0,)), ((), ()))


def _params(*sem):
    return pltpu.CompilerParams(dimension_semantics=sem, vmem_limit_bytes=VMEM_LIMIT)


def _dot(a, b, dims=NN):
    return lax.dot_general(a, b, dims, preferred_element_type=F32)


def _split(a):
    hi = a.astype(BF16)
    lo = (a - hi.astype(F32)).astype(BF16)
    return hi, lo


def _dot_f32(a, b, dims=NN):
    return _dot(a.astype(BF16), b.astype(BF16), dims)


def _sigmoid(x):
    return 1.0 / (1.0 + jnp.exp(-x))


def _silu_parts(x):
    s = _sigmoid(x)
    return x * s, s * (1.0 + x * (1.0 - s))


_GELU_C = math.sqrt(2.0 / math.pi)


def _gelu_parts(x):
    x2 = x * x
    t = jnp.tanh(_GELU_C * (x + 0.044715 * x * x2))
    val = 0.5 * x * (1.0 + t)
    der = 0.5 * (1.0 + t) + 0.5 * x * (1.0 - t * t) * _GELU_C * (1.0 + 3.0 * 0.044715 * x2)
    return val, der


class _View:
    def __init__(self, buf, shape, locate, row_tile, col_tile):
        self.buf, self.shape, self.locate, self.row_tile, self.col_tile = buf, shape, locate, row_tile, col_tile

    def spec(self, t0, t1, block_of):
        def index(i, j, k):
            rb, cb = block_of(i, j, k)
            slab, r, c = self.locate(rb * t0, cb * t1)
            return slab, r // t0, c // t1
        return pl.BlockSpec((None, t0, t1), index)


def _operand(x):
    return (x.buf, x.shape, x.row_tile, x.col_tile) if isinstance(x, _View) else (x, x.shape, x.shape[0], x.shape[1])


def _matmul(a, b, *, name, ta=False, tb=False, residual=None, out_dtype=F32, out=None):
    a_arr, a_shape, a_rt, a_ct = _operand(a)
    b_arr, b_shape, b_rt, b_ct = _operand(b)
    (K, M) = a_shape if ta else a_shape[::-1]
    N = b_shape[0] if tb else b_shape[1]
    a_mt, a_kt = (a_ct, a_rt) if ta else (a_rt, a_ct)
    b_nt, b_kt = (b_rt, b_ct) if tb else (b_ct, b_rt)
    k_cap = MATMUL_PANEL_BYTES // (MATMUL_TILE * max(a_arr.dtype.itemsize, b_arr.dtype.itemsize))
    tm, tn, tk = min(M, MATMUL_TILE, a_mt), min(N, MATMUL_TILE, b_nt), min(K, k_cap, a_kt, b_kt)
    if out is not None:
        tm, tn = min(tm, out.row_tile), min(tn, out.col_tile)
    nk = K // tk
    dims = ((((0,) if ta else (1,)), ((1,) if tb else (0,))), ((), ()))
    n_in = 2 + (residual is not None) + (out is not None)

    def body(*refs):
        a_ref, b_ref = refs[:2]
        r_ref = refs[2] if residual is not None else None
        o_ref = refs[n_in]

        def finish(r):
            if residual is not None:
                r = r + r_ref[...].astype(F32)
            o_ref[...] = r.astype(out_dtype)

        part = _dot(a_ref[...].astype(BF16), b_ref[...].astype(BF16), dims)
        if nk == 1:
            finish(part)
            return
        acc = refs[n_in + 1]
        k = pl.program_id(2)

        @pl.when(k == 0)
        def _():
            acc[...] = part

        @pl.when(k > 0)
        def _():
            acc[...] += part

        @pl.when(k == nk - 1)
        def _():
            finish(acc[...])

    def spec(x, t0, t1, block_of):
        if isinstance(x, _View):
            return x.spec(t0, t1, block_of)
        return pl.BlockSpec((t0, t1), block_of)

    a_spec = spec(a, tk, tm, lambda i, j, k: (k, i)) if ta else spec(a, tm, tk, lambda i, j, k: (i, k))
    b_spec = spec(b, tn, tk, lambda i, j, k: (j, k)) if tb else spec(b, tk, tn, lambda i, j, k: (k, j))
    in_specs = [a_spec, b_spec]
    args = [a_arr, b_arr]
    if residual is not None:
        in_specs.append(pl.BlockSpec((tm, tn), lambda i, j, k: (i, j)))
        args.append(residual)
    aliases = {}
    if out is None:
        out_spec = pl.BlockSpec((tm, tn), lambda i, j, k: (i, j))
        out_shape = jax.ShapeDtypeStruct((M, N), out_dtype)
    else:
        out_spec = out.spec(tm, tn, lambda i, j, k: (i, j))
        out_shape = jax.ShapeDtypeStruct(out.buf.shape, out.buf.dtype)
        out_dtype = out.buf.dtype
        in_specs.append(pl.BlockSpec(memory_space=pl.ANY))
        args.append(out.buf)
        aliases = {len(args) - 1: 0}
    return pl.pallas_call(
        body, name=name, grid=(M // tm, N // tn, nk), in_specs=in_specs, out_specs=out_spec, out_shape=out_shape,
        scratch_shapes=[pltpu.VMEM((tm, tn), F32)] if nk > 1 else [], input_output_aliases=aliases,
        compiler_params=_params("parallel", "parallel", "arbitrary"),
    )(*args)


def _row_tile(T, C):
    tr = max(SUBLANES, min(T, ROW_BLOCK_ELEMS // C) // SUBLANES * SUBLANES)
    while T % tr:
        tr -= SUBLANES
    return tr


def _rows_call(body, name, T, C, row_ins, full_ins, row_outs, acc_outs=()):
    tr = _row_tile(T, C)
    row_spec = pl.BlockSpec((tr, C), lambda i: (i, 0))
    in_specs = [row_spec] * len(row_ins) + [pl.BlockSpec(f.shape, lambda i, n=f.ndim: (0,) * n) for f in full_ins]
    out_specs = [row_spec] * len(row_outs) + [pl.BlockSpec(s, lambda i, n=len(s): (0,) * n) for s in acc_outs]
    out_shape = [jax.ShapeDtypeStruct((T, C), d) for d in row_outs] + [jax.ShapeDtypeStruct(s, F32) for s in acc_outs]
    return pl.pallas_call(
        body, name=name, grid=(T // tr,), in_specs=in_specs, out_specs=out_specs, out_shape=out_shape,
        compiler_params=_params("arbitrary" if acc_outs else "parallel"),
    )(*row_ins, *full_ins)


def _rmsnorm_fwd(x, g, name):
    T, C = x.shape

    def body(x_ref, g_ref, h_ref):
        xv = x_ref[...]
        r = lax.rsqrt(jnp.mean(xv * xv, axis=-1, keepdims=True) + RMS_EPS)
        h_ref[...] = ((xv * r) * g_ref[...]).astype(BF16)

    return _rows_call(body, name, T, C, [x], [g.reshape(1, C)], [BF16])[0]


def _rmsnorm_bwd(x, g, dh, dres, name):
    T, C = x.shape

    def body(x_ref, dh_ref, dres_ref, g_ref, dx_ref, dxb_ref, dg_ref):
        @pl.when(pl.program_id(0) == 0)
        def _():
            dg_ref[...] = jnp.zeros_like(dg_ref)

        xv = x_ref[...]
        dhv = dh_ref[...]
        r = lax.rsqrt(jnp.mean(xv * xv, axis=-1, keepdims=True) + RMS_EPS)
        xn = xv * r
        dg_ref[...] += jnp.sum(dhv * xn, axis=0, keepdims=True)
        dxn = dhv * g_ref[...]
        dx = dres_ref[...] + r * (dxn - xn * jnp.mean(dxn * xn, axis=-1, keepdims=True))
        dx_ref[...] = dx
        dxb_ref[...] = dx.astype(BF16)

    return _rows_call(body, name, T, C, [x, dh, dres], [g.reshape(1, C)], [F32, BF16], [(1, C)])


def _qknorm_fwd(q, g, name):
    T, C = q.shape
    tr = min(T, 1024)

    def body(q_ref, g_ref, o_ref):
        xv = q_ref[...]
        r = lax.rsqrt(jnp.mean(xv * xv, axis=-1, keepdims=True) + RMS_EPS)
        o_ref[...] = ((xv * r) * g_ref[...]).astype(BF16)

    spec = pl.BlockSpec((tr, HEAD_DIM), lambda i, h: (i, h))
    return pl.pallas_call(
        body, name=name, grid=(T // tr, C // HEAD_DIM),
        in_specs=[spec, pl.BlockSpec((1, HEAD_DIM), lambda i, h: (0, 0))], out_specs=spec,
        out_shape=jax.ShapeDtypeStruct((T, C), BF16), compiler_params=_params("parallel", "parallel"),
    )(q, g.reshape(1, HEAD_DIM))


def _qknorm_bwd(q, g, dqn, name):
    T, C = q.shape
    tr = min(T, 1024)

    def body(q_ref, d_ref, g_ref, dq_ref, dg_ref):
        @pl.when((pl.program_id(0) == 0) & (pl.program_id(1) == 0))
        def _():
            dg_ref[...] = jnp.zeros_like(dg_ref)

        xv = q_ref[...]
        dv = d_ref[...]
        r = lax.rsqrt(jnp.mean(xv * xv, axis=-1, keepdims=True) + RMS_EPS)
        xn = xv * r
        dg_ref[...] += jnp.sum(dv * xn, axis=0, keepdims=True)
        dxn = dv * g_ref[...]
        dq_ref[...] = (r * (dxn - xn * jnp.mean(dxn * xn, axis=-1, keepdims=True))).astype(BF16)

    spec = pl.BlockSpec((tr, HEAD_DIM), lambda i, h: (i, h))
    gspec = pl.BlockSpec((1, HEAD_DIM), lambda i, h: (0, 0))
    return pl.pallas_call(
        body, name=name, grid=(T // tr, C // HEAD_DIM), in_specs=[spec, spec, gspec], out_specs=[spec, gspec],
        out_shape=[jax.ShapeDtypeStruct((T, C), BF16), jax.ShapeDtypeStruct((1, HEAD_DIM), F32)],
        compiler_params=_params("arbitrary", "arbitrary"),
    )(q, dqn, g.reshape(1, HEAD_DIM))


ATT_TQ = 512
ATT_TK = 256


def _logsig_pair(z):
    sp = jnp.log(1.0 + jnp.exp(-jnp.abs(z)))
    return jnp.minimum(z, 0.0) - sp, jnp.minimum(-z, 0.0) - sp


def _tri(n, strict_upper_src):
    j = lax.broadcasted_iota(jnp.int32, (n, n), 0)
    s = lax.broadcasted_iota(jnp.int32, (n, n), 1)
    if strict_upper_src == "gt":
        m = j > s
    elif strict_upper_src == "le":
        m = j <= s
    else:
        m = j < s
    return jnp.where(m, 1.0, 0.0).astype(BF16)


def _cumdot(x, tri):
    hi, lo = _split(x)
    return _dot(hi, tri) + _dot(lo, tri)


def _attn_tiles(S):
    tq, tk = min(ATT_TQ, S), min(ATT_TK, S)
    return tq, tk, S // tq, tq // tk


def _attn_fwd(qn, kn, v, Bl, S, rider=None):
    T, C = qn.shape
    H = C // HEAD_DIM
    tq, tk, nq, kpq = _attn_tiles(S)
    scale = 1.0 / math.sqrt(HEAD_DIM)

    def body(q_ref, k_ref, v_ref, o_ref, bt_ref):
        i = pl.program_id(2)
        tri = _tri(tk, "gt")
        rowpos = lax.broadcasted_iota(jnp.int32, (tq, tk), 0) + i * tq
        colpos = lax.broadcasted_iota(jnp.int32, (tq, tk), 1)
        o_ref[...] = jnp.zeros_like(o_ref)
        bt_ref[...] = jnp.zeros_like(bt_ref)
        nkb = (i + 1) * kpq

        def step(n, carry):
            j = nkb - 1 - n
            rows = pl.ds(pl.multiple_of(j * tk, tk), tk)
            z = _dot(q_ref[...], k_ref[rows, :], NT) * scale
            mask = (colpos + j * tk) < rowpos
            a, b = _logsig_pair(z)
            b = jnp.where(mask, b, 0.0)
            suffix = _cumdot(b, tri)
            w = jnp.where(mask, jnp.exp(a + suffix + bt_ref[...]), 0.0)
            o_ref[...] += _dot(w.astype(BF16), v_ref[rows, :])
            bt_ref[...] += jnp.sum(b, axis=-1, keepdims=True)
            return carry

        lax.fori_loop(0, nkb, step, 0)

    qspec = pl.BlockSpec((tq, HEAD_DIM), lambda b, h, i: (b * nq + i, h))
    kspec = pl.BlockSpec((S, HEAD_DIM), lambda b, h, i: (b, h))
    btspec = pl.BlockSpec((None, None, tq, 1), lambda b, h, i: (b, h, i, 0))
    grid = (Bl, H, nq)
    body, in_specs, out_specs, out_shape, scratch, extra = _ride(
        rider, body, grid, [qspec, kspec, kspec], [qspec, btspec],
        [jax.ShapeDtypeStruct((T, C), F32), jax.ShapeDtypeStruct((Bl, H, S, 1), F32)], [])
    return pl.pallas_call(
        body, name="attn_fwd", grid=grid, in_specs=in_specs, out_specs=out_specs, out_shape=out_shape,
        scratch_shapes=scratch, compiler_params=_params("arbitrary", "arbitrary", "arbitrary"),
    )(qn, kn, v, *extra)


def _attn_bwd(qn, kn, v, do, btot, Bl, S, rider=None):
    T, C = qn.shape
    H = C // HEAD_DIM
    tq, tk, nq, kpq = _attn_tiles(S)
    scale = 1.0 / math.sqrt(HEAD_DIM)

    def body(q_ref, k_ref, v_ref, do_ref, bt_ref, dq_ref, dk_ref, dv_ref, pb_ref, pdl_ref):
        i = pl.program_id(2)

        @pl.when(i == 0)
        def _():
            dk_ref[...] = jnp.zeros_like(dk_ref)
            dv_ref[...] = jnp.zeros_like(dv_ref)

        tri_le = _tri(tk, "le")
        tri_lt = _tri(tk, "lt")
        rowpos = lax.broadcasted_iota(jnp.int32, (tq, tk), 0) + i * tq
        colpos = lax.broadcasted_iota(jnp.int32, (tq, tk), 1)
        dq_ref[...] = jnp.zeros_like(dq_ref)
        pb_ref[...] = bt_ref[...]
        pdl_ref[...] = jnp.zeros_like(pdl_ref)

        def step(j, carry):
            rows = pl.ds(pl.multiple_of(j * tk, tk), tk)
            kj = k_ref[rows, :]
            z = _dot(q_ref[...], kj, NT) * scale
            mask = (colpos + j * tk) < rowpos
            a, b = _logsig_pair(z)
            b = jnp.where(mask, b, 0.0)
            suffix = pb_ref[...] - _cumdot(b, tri_le)
            w = jnp.where(mask, jnp.exp(a + suffix), 0.0)
            dl = _dot(do_ref[...], v_ref[rows, :], NT) * w
            prefix = pdl_ref[...] + _cumdot(dl, tri_lt)
            beta = jnp.exp(a)
            dzb = (jnp.where(mask, dl * (1.0 - beta) - beta * prefix, 0.0) * scale).astype(BF16)
            dq_ref[...] += _dot(dzb, kj)
            dk_ref[rows, :] += _dot(dzb, q_ref[...], TN)
            dv_ref[rows, :] += _dot(w.astype(BF16), do_ref[...], TN)
            pb_ref[...] -= jnp.sum(b, axis=-1, keepdims=True)
            pdl_ref[...] += jnp.sum(dl, axis=-1, keepdims=True)
            return carry

        lax.fori_loop(0, (i + 1) * kpq, step, 0)

    qspec = pl.BlockSpec((tq, HEAD_DIM), lambda b, h, i: (b * nq + i, h))
    kspec = pl.BlockSpec((S, HEAD_DIM), lambda b, h, i: (b, h))
    btspec = pl.BlockSpec((None, None, tq, 1), lambda b, h, i: (b, h, i, 0))
    grid = (Bl, H, nq)
    body, in_specs, out_specs, out_shape, scratch, extra = _ride(
        rider, body, grid, [qspec, kspec, kspec, qspec, btspec], [qspec, kspec, kspec],
        [jax.ShapeDtypeStruct((T, C), F32)] * 3, [pltpu.VMEM((tq, 1), F32), pltpu.VMEM((tq, 1), F32)])
    return pl.pallas_call(
        body, name="attn_bwd", grid=grid, in_specs=in_specs, out_specs=out_specs, out_shape=out_shape,
        scratch_shapes=scratch, compiler_params=_params("arbitrary", "arbitrary", "arbitrary"),
    )(qn, kn, v, do, btot, *extra)


SSM_TIME_BLOCK = 512
CHUNK = SUBLANES


def _cmadd(xr, xi, ar, ai, sr, si):
    return xr + ar * sr - ai * si, xi + ar * si + ai * sr


def _chunk_scan(xr, xi, tab_ref, cr, ci, reverse):
    for lvl, d in enumerate((1, 2, 4)):
        shift = (CHUNK - d) if reverse else d
        sr = pltpu.roll(xr, shift, 0)
        si = pltpu.roll(xi, shift, 0)
        ar = tab_ref[pl.ds((2 * lvl) * CHUNK, CHUNK), :]
        ai = tab_ref[pl.ds((2 * lvl + 1) * CHUNK, CHUNK), :]
        xr, xi = _cmadd(xr, xi, ar, ai, sr, si)
    pr = tab_ref[pl.ds(6 * CHUNK, CHUNK), :]
    pi = tab_ref[pl.ds(7 * CHUNK, CHUNK), :]
    return _cmadd(xr, xi, pr, pi, cr, ci)


def _ssm_dims(S, C):
    G = C // GROUP
    GT = min(16, G)
    return G, GT, G // GT, GT * GROUP, GT * STATE, min(SSM_TIME_BLOCK, S)


def _ssm_fwd(u, mats, Bl, S):
    T, C = u.shape
    G, GT, ngt, cw, sw, TB = _ssm_dims(S, C)
    ntb = S // TB
    nch = TB // CHUNK

    def body(u_ref, bre_ref, bim_ref, cre_ref, cim_ref, d_ref, tab_ref, y_ref, hr_ref, hi_ref, car_r, car_i):
        @pl.when(pl.program_id(2) == 0)
        def _():
            car_r[...] = jnp.zeros_like(car_r)
            car_i[...] = jnp.zeros_like(car_i)

        uv = u_ref[...]
        hr_ref[...] = _dot_f32(uv, bre_ref[...])
        hi_ref[...] = _dot_f32(uv, bim_ref[...])

        def step(n, carry):
            cr, ci = carry
            rows = pl.ds(pl.multiple_of(n * CHUNK, CHUNK), CHUNK)
            xr, xi = _chunk_scan(hr_ref[rows, :], hi_ref[rows, :], tab_ref, cr, ci, False)
            hr_ref[rows, :] = xr
            hi_ref[rows, :] = xi
            last = (CHUNK - 1, CHUNK)
            return (jnp.broadcast_to(xr[last[0]:last[1], :], xr.shape), jnp.broadcast_to(xi[last[0]:last[1], :], xi.shape))

        cr, ci = lax.fori_loop(0, nch, step, (car_r[...], car_i[...]))
        car_r[...] = cr
        car_i[...] = ci
        y_ref[...] = _dot_f32(hr_ref[...], cre_ref[...]) - _dot_f32(hi_ref[...], cim_ref[...]) + d_ref[...] * uv

    uspec = pl.BlockSpec((TB, cw), lambda g, b, t: (b * ntb + t, g))
    hspec = pl.BlockSpec((TB, sw), lambda g, b, t: (b * ntb + t, g))

    def gspec(r, c):
        return pl.BlockSpec((None, r, c), lambda g, b, t: (g, 0, 0))

    return pl.pallas_call(
        body, name="ssm_fwd", grid=(ngt, Bl, ntb),
        in_specs=[uspec, gspec(cw, sw), gspec(cw, sw), gspec(sw, cw), gspec(sw, cw), gspec(1, cw), gspec(8 * CHUNK, sw)],
        out_specs=[uspec, hspec, hspec],
        out_shape=[jax.ShapeDtypeStruct((T, C), F32), jax.ShapeDtypeStruct((T, G * STATE), F32),
                   jax.ShapeDtypeStruct((T, G * STATE), F32)],
        scratch_shapes=[pltpu.VMEM((CHUNK, sw), F32), pltpu.VMEM((CHUNK, sw), F32)],
        compiler_params=_params("parallel", "arbitrary", "arbitrary"),
    )(u, mats["bbd_re"], mats["bbd_im"], mats["cbd_re"], mats["cbd_im"], mats["d"], mats["tab_fwd"])


def _ssm_bwd(u, dy, h_re, h_im, mats, Bl, S):
    T, C = u.shape
    G, GT, ngt, cw, sw, TB = _ssm_dims(S, C)
    ntb = S // TB
    nch = TB // CHUNK
    rpb = TB // CHUNK

    def body(u_ref, dy_ref, hr_ref, hi_ref, hpr_ref, hpi_ref, ctr_ref, cti_ref, btr_ref, bti_ref, d_ref, tab_ref,
             du_ref, dbr_ref, dbi_ref, dcr_ref, dci_ref, dlr_ref, dli_ref, dd_ref, gr_ref, gi_ref, car_r, car_i):
        b = pl.program_id(1)
        t = pl.program_id(2)

        @pl.when((b == 0) & (t == 0))
        def _():
            for ref in (dbr_ref, dbi_ref, dcr_ref, dci_ref, dlr_ref, dli_ref, dd_ref):
                ref[...] = jnp.zeros_like(ref)

        @pl.when(t == 0)
        def _():
            car_r[...] = jnp.zeros_like(car_r)
            car_i[...] = jnp.zeros_like(car_i)

        uv = u_ref[...]
        dyv = dy_ref[...]
        gr_ref[...] = _dot_f32(dyv, ctr_ref[...])
        gi_ref[...] = -_dot_f32(dyv, cti_ref[...])
        alive = jnp.where(t == ntb - 1, 0.0, 1.0)
        row0 = lax.broadcasted_iota(jnp.int32, (CHUNK, sw), 0) == 0

        def step(m, carry):
            cr, ci, ar, ai = carry
            n = nch - 1 - m
            rows = pl.ds(pl.multiple_of(n * CHUNK, CHUNK), CHUNK)
            prow = pl.ds(pl.multiple_of(jnp.maximum(n - 1, 0) * CHUNK, CHUNK), CHUNK)
            xr, xi = _chunk_scan(gr_ref[rows, :], gi_ref[rows, :], tab_ref, cr, ci, True)
            gr_ref[rows, :] = xr
            gi_ref[rows, :] = xi
            first = n == 0
            pr = jnp.where(first, hpr_ref[...] * alive, hr_ref[prow, :])
            pi = jnp.where(first, hpi_ref[...] * alive, hi_ref[prow, :])
            sr = jnp.where(row0, pltpu.roll(pr, 1, 0), pltpu.roll(hr_ref[rows, :], 1, 0))
            si = jnp.where(row0, pltpu.roll(pi, 1, 0), pltpu.roll(hi_ref[rows, :], 1, 0))
            ar = ar + xr * sr + xi * si
            ai = ai + xi * sr - xr * si
            return (jnp.broadcast_to(xr[0:1, :], xr.shape), jnp.broadcast_to(xi[0:1, :], xi.shape), ar, ai)

        zero = jnp.zeros((CHUNK, sw), F32)
        cr, ci, ar, ai = lax.fori_loop(0, nch, step, (car_r[...], car_i[...], zero, zero))
        car_r[...] = cr
        car_i[...] = ci
        dlr_ref[...] += ar
        dli_ref[...] += ai
        gr = gr_ref[...]
        gi = gi_ref[...]
        dbr_ref[...] += _dot_f32(uv, gr, TN)
        dbi_ref[...] += _dot_f32(uv, gi, TN)
        dcr_ref[...] += _dot_f32(hr_ref[...], dyv, TN)
        dci_ref[...] -= _dot_f32(hi_ref[...], dyv, TN)
        dd_ref[...] += jnp.sum(dyv * uv, axis=0, keepdims=True)
        du_ref[...] = (_dot_f32(gr, btr_ref[...]) + _dot_f32(gi, bti_ref[...]) + d_ref[...] * dyv).astype(BF16)

    def tblk(b, t):
        return b * ntb + (ntb - 1 - t)

    uspec = pl.BlockSpec((TB, cw), lambda g, b, t: (tblk(b, t), g))
    hspec = pl.BlockSpec((TB, sw), lambda g, b, t: (tblk(b, t), g))
    hpspec = pl.BlockSpec((CHUNK, sw), lambda g, b, t: (jnp.maximum(tblk(b, t) * rpb - 1, 0), g))

    def gspec(r, c):
        return pl.BlockSpec((None, r, c), lambda g, b, t: (g, 0, 0))

    def gshape(r, c):
        return jax.ShapeDtypeStruct((ngt, r, c), F32)

    return pl.pallas_call(
        body, name="ssm_bwd", grid=(ngt, Bl, ntb),
        in_specs=[uspec, uspec, hspec, hspec, hpspec, hpspec, gspec(cw, sw), gspec(cw, sw), gspec(sw, cw), gspec(sw, cw),
                  gspec(1, cw), gspec(8 * CHUNK, sw)],
        out_specs=[uspec, gspec(cw, sw), gspec(cw, sw), gspec(sw, cw), gspec(sw, cw), gspec(CHUNK, sw), gspec(CHUNK, sw),
                   gspec(1, cw)],
        out_shape=[jax.ShapeDtypeStruct((T, C), BF16), gshape(cw, sw), gshape(cw, sw), gshape(sw, cw), gshape(sw, cw),
                   gshape(CHUNK, sw), gshape(CHUNK, sw), gshape(1, cw)],
        scratch_shapes=[pltpu.VMEM((TB, sw), F32), pltpu.VMEM((TB, sw), F32), pltpu.VMEM((CHUNK, sw), F32),
                        pltpu.VMEM((CHUNK, sw), F32)],
        compiler_params=_params("arbitrary", "arbitrary", "arbitrary"),
    )(u, dy, h_re, h_im, h_re, h_im, mats["cbdT_re"], mats["cbdT_im"], mats["bbdT_re"], mats["bbdT_im"], mats["d"],
      mats["tab_rev"])


def _zoh(a_re, a_im, log_dt, b_re, b_im):
    dt = jnp.exp(log_dt)[:, None]
    mag = jnp.exp(a_re * dt)
    l_re = mag * jnp.cos(a_im * dt)
    l_im = mag * jnp.sin(a_im * dt)
    den = a_re * a_re + a_im * a_im
    f_re = ((l_re - 1.0) * a_re + l_im * a_im) / den
    f_im = (l_im * a_re - (l_re - 1.0) * a_im) / den
    bb_re = f_re[..., None] * b_re - f_im[..., None] * b_im
    bb_im = f_re[..., None] * b_im + f_im[..., None] * b_re
    return l_re, l_im, bb_re, bb_im


def _ssm_matrices(a_re, a_im, log_dt, b_re, b_im, c_re, c_im, d, S):
    G = a_re.shape[0]
    _, GT, ngt, cw, sw, _ = _ssm_dims(S, G * GROUP)
    _, _, bb_re, bb_im = _zoh(a_re, a_im, log_dt, b_re, b_im)
    eye = jnp.eye(GT, dtype=F32)

    def bd_b(bb):
        return jnp.einsum("tgpi,gh->tgihp", bb.reshape(ngt, GT, STATE, GROUP), eye).reshape(ngt, cw, sw)

    def bd_c(c):
        return jnp.einsum("tgip,gh->tgphi", c.reshape(ngt, GT, GROUP, STATE), eye).reshape(ngt, sw, cw)

    dt = jnp.exp(log_dt)[:, None]

    def power(k, conj):
        mag = jnp.exp(k * a_re * dt)
        ang = k * a_im * dt
        return (mag * jnp.cos(ang)).reshape(ngt, 1, sw), ((-1.0 if conj else 1.0) * mag * jnp.sin(ang)).reshape(ngt, 1, sw)

    r = jnp.arange(CHUNK)[None, :, None]

    def table(reverse):
        parts = []
        for dd in (1, 2, 4):
            pr, pi = power(float(dd), reverse)
            keep = (r <= CHUNK - 1 - dd) if reverse else (r >= dd)
            parts += [jnp.where(keep, pr, 0.0), jnp.where(keep, pi, 0.0)]
        exps = [(CHUNK - k) if reverse else (k + 1) for k in range(CHUNK)]
        pw = [power(float(e), reverse) for e in exps]
        parts += [jnp.concatenate([p[0] for p in pw], axis=1), jnp.concatenate([p[1] for p in pw], axis=1)]
        return jnp.concatenate([jnp.broadcast_to(p, (ngt, CHUNK, sw)) for p in parts], axis=1)

    mats = dict(bbd_re=bd_b(bb_re).astype(BF16), bbd_im=bd_b(bb_im).astype(BF16), cbd_re=bd_c(c_re).astype(BF16),
                cbd_im=bd_c(c_im).astype(BF16), d=d.reshape(ngt, 1, cw), tab_fwd=table(False), tab_rev=table(True))
    for k in ("bbd_re", "bbd_im", "cbd_re", "cbd_im"):
        mats[k.replace("bd_", "bdT_")] = jnp.swapaxes(mats[k], 1, 2)
    return mats


def _ssm_unblock(dbr, dbi, dcr, dci, dlr, dli, dd, G):
    ngt = dbr.shape[0]
    GT = G // ngt
    eye = jnp.eye(GT, dtype=F32)

    def ub(x):
        return jnp.einsum("tgihp,gh->tgpi", x.reshape(ngt, GT, GROUP, GT, STATE), eye).reshape(G, STATE, GROUP)

    def uc(x):
        return jnp.einsum("tgphi,gh->tgip", x.reshape(ngt, GT, STATE, GT, GROUP), eye).reshape(G, GROUP, STATE)

    return (dlr.sum(axis=1).reshape(G, STATE), dli.sum(axis=1).reshape(G, STATE), ub(dbr), ub(dbi), uc(dcr), uc(dci),
            dd.reshape(G * GROUP))


def _attn_gate_fwd(o, gate):
    T, C = o.shape

    def body(o_ref, g_ref, out_ref):
        out_ref[...] = (o_ref[...] * _silu_parts(g_ref[...])[0]).astype(BF16)

    return _rows_call(body, "attn_gate_fwd", T, C, [o, gate], [], [BF16])[0]


def _attn_gate_bwd(dog, o, gate):
    T, C = o.shape

    def body(d_ref, o_ref, g_ref, do_ref, dg_ref):
        val, der = _silu_parts(g_ref[...])
        dv = d_ref[...]
        do_ref[...] = (dv * val).astype(BF16)
        dg_ref[...] = (dv * o_ref[...] * der).astype(BF16)

    return _rows_call(body, "attn_gate_bwd", T, C, [dog, o, gate], [], [BF16, BF16])


def _gelu_fwd(y):
    T, C = y.shape

    def body(y_ref, out_ref):
        out_ref[...] = _gelu_parts(y_ref[...])[0].astype(BF16)

    return _rows_call(body, "gelu_fwd", T, C, [y], [], [BF16])[0]


def _glu_fwd(y, gl, gate, glu_b):
    T, C = y.shape

    def body(y_ref, gl_ref, g_ref, b_ref, out_ref):
        yg = _gelu_parts(y_ref[...])[0]
        sg = _sigmoid(gl_ref[...] + b_ref[...])
        out_ref[...] = (yg * sg * _silu_parts(g_ref[...])[0]).astype(BF16)

    return _rows_call(body, "glu_fwd", T, C, [y, gl, gate], [glu_b.reshape(1, C)], [BF16])[0]


def _glu_bwd(dy3, y, gl, gate, glu_b):
    T, C = y.shape

    def body(d_ref, y_ref, gl_ref, g_ref, b_ref, dgl_ref, dgate_ref, t1_ref, db_ref):
        @pl.when(pl.program_id(0) == 0)
        def _():
            db_ref[...] = jnp.zeros_like(db_ref)

        yg = _gelu_parts(y_ref[...])[0]
        sg = _sigmoid(gl_ref[...] + b_ref[...])
        sl, sld = _silu_parts(g_ref[...])
        dv = d_ref[...]
        dy2 = dv * sl
        dgl = dy2 * yg * sg * (1.0 - sg)
        dgl_ref[...] = dgl.astype(BF16)
        dgate_ref[...] = (dv * (yg * sg) * sld).astype(BF16)
        t1_ref[...] = dy2 * sg
        db_ref[...] += jnp.sum(dgl, axis=0, keepdims=True)

    return _rows_call(body, "glu_bwd", T, C, [dy3, y, gl, gate], [glu_b.reshape(1, C)], [BF16, BF16, F32], [(1, C)])


def _gelu_bwd(t1, t2, y):
    T, C = y.shape

    def body(a_ref, b_ref, y_ref, out_ref):
        out_ref[...] = (a_ref[...] + b_ref[...]) * _gelu_parts(y_ref[...])[1]

    return _rows_call(body, "gelu_bwd", T, C, [t1, t2, y], [], [F32])[0]


def _loss_head(x2, target):
    T, C = x2.shape

    def body(x_ref, t_ref, d_ref, db_ref, l_ref):
        @pl.when(pl.program_id(0) == 0)
        def _():
            l_ref[...] = jnp.zeros_like(l_ref)

        e = x_ref[...] - t_ref[...]
        d = e * (1.0 / C)
        d_ref[...] = d
        db_ref[...] = d.astype(BF16)
        l_ref[...] += 0.5 * jnp.sum(jnp.sum(e * e, axis=-1, keepdims=True) * (1.0 / C), axis=0, keepdims=True)

    return _rows_call(body, "loss_head", T, C, [x2, target], [], [F32, BF16], [(1, 1)])


def _adamw(w, g, m, v, name):
    shape = w.shape
    C = shape[-1]
    R = w.size // C
    bc1 = 1.0 - ADAM_B1 ** ADAM_STEP
    bc2 = 1.0 - ADAM_B2 ** ADAM_STEP

    def body(w_ref, g_ref, m_ref, v_ref, d_ref, nm_ref, nv_ref):
        gv = g_ref[...]
        mn = ADAM_B1 * m_ref[...] + (1.0 - ADAM_B1) * gv
        vn = ADAM_B2 * v_ref[...] + (1.0 - ADAM_B2) * (gv * gv)
        d_ref[...] = -ADAM_LR * ((mn / bc1) / (jnp.sqrt(vn / bc2) + ADAM_EPS) + ADAM_WD * w_ref[...])
        nm_ref[...] = mn
        nv_ref[...] = vn

    outs = _rows_call(body, name, R, C, [a.reshape(R, C) for a in (w, g, m, v)], [], [F32, F32, F32])
    return [o.reshape(shape) for o in outs]


def _sum_leading(x, name):
    n, R, C = x.shape
    tr = _row_tile(R, C * n)

    def body(x_ref, o_ref):
        acc = x_ref[0].astype(F32)
        for k in range(1, n):
            acc = acc + x_ref[k].astype(F32)
        o_ref[...] = acc

    return pl.pallas_call(
        body, name=name, grid=(R // tr,), in_specs=[pl.BlockSpec((n, tr, C), lambda i: (0, i, 0))],
        out_specs=pl.BlockSpec((tr, C), lambda i: (i, 0)), out_shape=jax.ShapeDtypeStruct((R, C), F32),
        compiler_params=_params("parallel"),
    )(x)


def _add_halves(g, c, name):
    full, recv = g
    n, R, C = full.shape
    half = R // 2
    tr = _row_tile(half, C)
    nb = half // tr

    def body(c_ref, a_ref, b_ref, o_ref):
        o_ref[...] = (a_ref[...] + b_ref[...]).astype(BF16)

    grid_spec = pltpu.PrefetchScalarGridSpec(
        num_scalar_prefetch=1, grid=(n, nb),
        in_specs=[pl.BlockSpec((None, tr, C), lambda j, i, c_ref: (j, c_ref[0] * nb + i, 0)),
                  pl.BlockSpec((None, tr, C), lambda j, i, c_ref: (j, i, 0))],
        out_specs=pl.BlockSpec((None, tr, C), lambda j, i, c_ref: (j, i, 0)))
    return pl.pallas_call(
        body, name=name, grid_spec=grid_spec, out_shape=jax.ShapeDtypeStruct((n, half, C), BF16),
        compiler_params=_params("parallel", "parallel"),
    )(c.reshape(1).astype(jnp.int32), full, recv)


ANY = pl.BlockSpec(memory_space=pl.ANY)


def _position():
    return lax.axis_index("x"), lax.axis_index("y"), lax.axis_index("c")


def _all_gather8(blk, name):
    return _standalone(_all_gather_rider(blk), name)[0]


class _Rider:
    def __init__(self, arrays, out_shapes, sems, start, finish):
        self.arrays, self.out_shapes, self.sems, self.start, self.finish = arrays, out_shapes, sems, start, finish


def _standalone(rider, name):
    def body(*refs):
        rider.start(*refs)
        rider.finish(*refs)

    return pl.pallas_call(
        body, name=name, in_specs=[ANY] * len(rider.arrays), out_specs=[ANY] * len(rider.out_shapes),
        out_shape=rider.out_shapes, scratch_shapes=rider.sems,
    )(*rider.arrays)


def _ride(rider, body, grid, in_specs, out_specs, out_shape, scratch):
    if rider is None:
        return body, in_specs, out_specs, out_shape, scratch, []
    ni, no, ns = len(in_specs), len(out_specs), len(scratch)
    ri, ro = len(rider.arrays), len(rider.out_shapes)

    def full(*refs):
        ins, refs = refs[:ni], refs[ni:]
        r_ins, refs = refs[:ri], refs[ri:]
        outs, refs = refs[:no], refs[no:]
        r_outs, refs = refs[:ro], refs[ro:]
        scr, r_sems = refs[:ns], refs[ns:]
        ids = [pl.program_id(a) for a in range(len(grid))]
        first = functools.reduce(jnp.logical_and, [i == 0 for i in ids])
        last = functools.reduce(jnp.logical_and, [i == g - 1 for i, g in zip(ids, grid)])

        @pl.when(first)
        def _():
            rider.start(*r_ins, *r_outs, *r_sems)

        body(*ins, *outs, *scr)

        @pl.when(last)
        def _():
            rider.finish(*r_ins, *r_outs, *r_sems)

    return (full, in_specs + [ANY] * ri, out_specs + [ANY] * ro, out_shape + rider.out_shapes, scratch + rider.sems,
            rider.arrays)


def _all_gather_rider(blk):
    M, N = blk.shape

    def copies(x_ref, out_ref, send_sems, recv_sems, local_sem):
        x, y, c = _position()
        me, sibling = (x, y, c), (x, y, 1 - c)
        chips = [(1 - x, y), (x, 1 - y), (1 - x, 1 - y)]

        def slab(px, py, pc):
            return out_ref.at[4 * px + 2 * py + pc]

        def copy(k, block, to, src=None):
            return pltpu.make_async_remote_copy(
                src_ref=slab(*block) if src is None else src, dst_ref=slab(*block),
                send_sem=send_sems.at[k], recv_sem=recv_sems.at[k], device_id=to, device_id_type=MESH)

        mine = pltpu.make_async_copy(x_ref, slab(*me), local_sem)
        first = [copy(0, me, sibling, src=x_ref)]
        first += [copy(1 + j, me, (*chip, c), src=x_ref) for j, chip in enumerate(chips)]
        passed = [copy(4 + j, (*chip, c), sibling) for j, chip in enumerate(chips)]
        arrivals = [copy(1 + j, (*chip, c), me) for j, chip in enumerate(chips)]
        from_sibling = [copy(0, sibling, me)] + [copy(4 + j, (*chip, 1 - c), me) for j, chip in enumerate(chips)]
        return mine, first, passed, arrivals, from_sibling

    def start(*refs):
        mine, first, _, _, _ = copies(*refs)
        mine.start()
        for cp in first:
            cp.start()

    def finish(*refs):
        mine, first, passed, arrivals, from_sibling = copies(*refs)
        for arrival, onward in zip(arrivals, passed):
            arrival.wait_recv()
            onward.start()
        for cp in from_sibling:
            cp.wait_recv()
        for cp in first + passed:
            cp.wait_send()
        mine.wait()

    return _Rider([blk], [jax.ShapeDtypeStruct((N_DEV, M, N), blk.dtype)],
                  [pltpu.SemaphoreType.DMA((7,)), pltpu.SemaphoreType.DMA((7,)), pltpu.SemaphoreType.DMA], start, finish)


def _sibling_send_half(g, name):
    n, R, C = g.shape
    half = R // 2

    def body(g_ref, out_ref, send_sem, recv_sem):
        x, y, c = _position()
        src = g_ref.at[:, pl.ds((1 - c) * half, half), :]
        cp = pltpu.make_async_remote_copy(src_ref=src, dst_ref=out_ref, send_sem=send_sem, recv_sem=recv_sem,
                                          device_id=(x, y, 1 - c), device_id_type=MESH)
        cp.start()
        cp.wait()

    return pl.pallas_call(
        body, name=name, in_specs=[ANY], out_specs=ANY, out_shape=jax.ShapeDtypeStruct((n, half, C), g.dtype),
        scratch_shapes=[pltpu.SemaphoreType.DMA, pltpu.SemaphoreType.DMA],
    )(g)


def _chip_exchange(p, name):
    return _standalone(_chip_exchange_rider(p), name)[0]


def _chip_exchange_rider(p):
    def copies(p_ref, out_ref, send_sems, recv_sems, local_sem):
        x, y, c = _position()
        my = 2 * x + y
        chips = [(1 - x, y), (x, 1 - y), (1 - x, 1 - y)]
        mine = pltpu.make_async_copy(p_ref.at[my], out_ref.at[my], local_sem)
        sends = [pltpu.make_async_remote_copy(
            src_ref=p_ref.at[2 * px + py], dst_ref=out_ref.at[my], send_sem=send_sems.at[k], recv_sem=recv_sems.at[k],
            device_id=(px, py, c), device_id_type=MESH) for k, (px, py) in enumerate(chips)]
        arrivals = [pltpu.make_async_remote_copy(
            src_ref=p_ref.at[my], dst_ref=out_ref.at[2 * px + py], send_sem=send_sems.at[k], recv_sem=recv_sems.at[k],
            device_id=(px, py, c), device_id_type=MESH) for k, (px, py) in enumerate(chips)]
        return mine, sends, arrivals

    def start(*refs):
        mine, sends, _ = copies(*refs)
        mine.start()
        for cp in sends:
            cp.start()

    def finish(*refs):
        mine, sends, arrivals = copies(*refs)
        for cp in arrivals:
            cp.wait_recv()
        for cp in sends:
            cp.wait_send()
        mine.wait()

    return _Rider([p], [jax.ShapeDtypeStruct(p.shape, p.dtype)],
                  [pltpu.SemaphoreType.DMA((3,)), pltpu.SemaphoreType.DMA((3,)), pltpu.SemaphoreType.DMA], start, finish)


def _sibling_swap(f, name):
    def body(f_ref, out_ref, send_sem, recv_sem):
        x, y, c = _position()
        cp = pltpu.make_async_remote_copy(src_ref=f_ref, dst_ref=out_ref, send_sem=send_sem, recv_sem=recv_sem,
                                          device_id=(x, y, 1 - c), device_id_type=MESH)
        cp.start()
        cp.wait()

    return pl.pallas_call(
        body, name=name, in_specs=[ANY], out_specs=ANY, out_shape=jax.ShapeDtypeStruct(f.shape, f.dtype),
        scratch_shapes=[pltpu.SemaphoreType.DMA, pltpu.SemaphoreType.DMA],
    )(f)


def _pack_rest(attn_out, ssm_in, glu_w, ssm_out):
    hd = ssm_in.shape[0] // 2
    return jnp.concatenate([attn_out, jnp.concatenate([ssm_in[:hd], ssm_in[hd:]], axis=1), glu_w, ssm_out], axis=0)


def _unpack_rest(p):
    D = p.shape[-1]
    q, hd = D // N_CHIPS, D // 2
    o = [0, q, q + hd, 2 * q + hd, 3 * q + hd]
    ssm_in = p[o[1]:o[2]]
    return p[o[0]:o[1]], jnp.concatenate([ssm_in[:, :hd], ssm_in[:, hd:]], axis=0), p[o[2]:o[3]], p[o[3]:o[4]]


def _attn_in_views(buf_a):
    D = buf_a.shape[-1]
    return [_View(buf_a, (D, D), lambda r, c, j=j: (j, r, c), D, D) for j in range(N_CHIPS)]


def _rest_views(buf_b):
    D = buf_b.shape[-1]
    q, hd = D // N_CHIPS, D // 2
    o_out, o_in, o_glu, o_sout = 0, q, q + hd, 2 * q + hd

    def row_sharded(off):
        return _View(buf_b, (D, D), lambda r, c: (r // q, off + r % q, c), math.gcd(q, off), D)

    def ssm_in(part):
        return _View(buf_b, (D, D), lambda r, c: (2 * part + c // hd, o_in + r % hd, (r // hd) * hd + c % hd),
                     math.gcd(hd, o_in), hd)

    return dict(attn_out=row_sharded(o_out), ssm_in_u=ssm_in(0), ssm_in_gate=ssm_in(1), glu=row_sharded(o_glu),
                ssm_out=row_sharded(o_sout))


SMALL_REPL = ("norm_g", "attn_q_g", "attn_k_g")


def _pack_small(parts):
    flat = jnp.concatenate([p.reshape(-1) for p in parts])
    pad = (-flat.size) % (2 * SUBLANES * LANES)
    return jnp.pad(flat, (0, pad)).reshape(-1, LANES)


def _unpack_small(buf, shapes):
    flat = buf.reshape(-1)
    out, off = [], 0
    for s in shapes:
        n = math.prod(s)
        out.append(flat[off:off + n].reshape(s))
        off += n
    return out


def _local_step(x, target, norm_g, q_g, k_g, wa, wb, ssm_small, core=None):
    Bl, S, D = x.shape
    T = Bl * S
    x0 = x.reshape(T, D)
    tgt = target.reshape(T, D)
    a_re, a_im, log_dt, b_re, b_im, c_re, c_im, d_skip, glu_b = ssm_small
    G = a_re.shape[0]
    mats = _ssm_matrices(a_re, a_im, log_dt, b_re, b_im, c_re, c_im, d_skip, S)
    w_in = _attn_in_views(wa)
    rows_b = wb.shape[-2] * (1 if core is None else 2)
    ga = lax.empty(wa.shape, F32)
    gb = lax.empty((N_CHIPS, rows_b, D), F32)

    def wgrad(a, b, key, name):
        return _matmul(a, b, name=name, ta=True, out=_rest_views(gb)[key])

    h0 = _rmsnorm_fwd(x0, norm_g[0], "norm0_fwd")
    q, k, v, gate = [_matmul(h0, w_in[j], name=f"attn_in_{j}", out_dtype=(BF16 if j == 2 else F32)) for j in range(4)]
    qn = _qknorm_fwd(q, q_g, "qnorm_fwd")
    kn = _qknorm_fwd(k, k_g, "knorm_fwd")
    if core is None:
        o, btot = _attn_fwd(qn, kn, v, Bl, S)
    else:
        o, btot, wb = _attn_fwd(qn, kn, v, Bl, S, rider=_all_gather_rider(wb))
        wb = wb.reshape(N_CHIPS, rows_b, D)
    w = _rest_views(wb)
    og = _attn_gate_fwd(o, gate)
    x1 = _matmul(og, w["attn_out"], name="attn_out", residual=x0)

    h1 = _rmsnorm_fwd(x1, norm_g[1], "norm1_fwd")
    u = _matmul(h1, w["ssm_in_u"], name="ssm_in_u")
    gate2 = _matmul(h1, w["ssm_in_gate"], name="ssm_in_gate")
    y, hs_re, hs_im = _ssm_fwd(u, mats, Bl, S)
    yg = _gelu_fwd(y)
    gl = _matmul(yg, w["glu"], name="glu_mm")
    y3 = _glu_fwd(y, gl, gate2, glu_b)
    x2 = _matmul(y3, w["ssm_out"], name="ssm_out", residual=x1)

    dx2, dx2b, loss = _loss_head(x2, tgt)

    dy3 = _matmul(dx2b, w["ssm_out"], name="ssm_out_dgrad", tb=True)
    gb = wgrad(y3, dx2b, "ssm_out", "ssm_out_wgrad")
    dgl, dgate2, t1, dglu_b = _glu_bwd(dy3, y, gl, gate2, glu_b)
    t2 = _matmul(dgl, w["glu"], name="glu_dgrad", tb=True)
    gb = wgrad(yg, dgl, "glu", "glu_wgrad")
    dy = _gelu_bwd(t1, t2, y)
    du, dbr, dbi, dcr, dci, dlr, dli, dd = _ssm_bwd(u, dy, hs_re, hs_im, mats, Bl, S)
    dh1 = _matmul(du, w["ssm_in_u"], name="ssm_in_dgrad_u", tb=True)
    dh1 = _matmul(dgate2, w["ssm_in_gate"], name="ssm_in_dgrad_gate", tb=True, residual=dh1)
    gb = wgrad(h1, du, "ssm_in_u", "ssm_in_wgrad_u")
    gb = wgrad(h1, dgate2, "ssm_in_gate", "ssm_in_wgrad_gate")
    dx1, dx1b, dng1 = _rmsnorm_bwd(x1, norm_g[1], dh1, dx2, "norm1_bwd")

    dog = _matmul(dx1b, w["attn_out"], name="attn_out_dgrad", tb=True)
    gb = wgrad(og, dx1b, "attn_out", "attn_out_wgrad")
    do, dgate = _attn_gate_bwd(dog, o, gate)
    if core is None:
        dqn, dkn, dv = _attn_bwd(qn, kn, v, do, btot, Bl, S)
    else:
        chip_sum_b = _add_halves((gb, _sibling_send_half(gb, "grads_b_sibling_half")), core, "grads_b_add_halves")
        dqn, dkn, dv, gb = _attn_bwd(qn, kn, v, do, btot, Bl, S, rider=_chip_exchange_rider(chip_sum_b))
    dq, dqg = _qknorm_bwd(q, q_g, dqn, "qnorm_bwd")
    dk, dkg = _qknorm_bwd(k, k_g, dkn, "knorm_bwd")
    dproj = [dq, dk, dv, dgate]
    dh0 = None
    for j in range(4):
        dh0 = _matmul(dproj[j], w_in[j], name=f"attn_in_dgrad_{j}", tb=True, residual=dh0)
        ga = _matmul(h0, dproj[j], name=f"attn_in_wgrad_{j}", ta=True, out=_attn_in_views(ga)[j])
    dx0, _, dng0 = _rmsnorm_bwd(x0, norm_g[0], dh0, dx1, "norm0_bwd")

    small = (jnp.concatenate([dng0, dng1], axis=0), dqg, dkg) + _ssm_unblock(dbr, dbi, dcr, dci, dlr, dli, dd, G) + (
        dglu_b.reshape(D),)
    return loss, dx0.reshape(Bl, S, D), ga, gb, small


def _chip_rows(a, chip, n_per):
    return lax.dynamic_slice_in_dim(a, chip * n_per, n_per, axis=0)


def kernel(x, norm_g, attn_w_in, attn_q_g, attn_k_g, attn_w_out, ssm_w_in, ssm_A_re, ssm_A_im, ssm_log_dt, ssm_B_re, ssm_B_im, ssm_C_re, ssm_C_im, ssm_D, ssm_glu_w, ssm_glu_b, ssm_w_out, loss_target, m_norm_g, m_attn_w_in, m_attn_q_g, m_attn_k_g, m_attn_w_out, m_ssm_w_in, m_ssm_A_re, m_ssm_A_im, m_ssm_log_dt, m_ssm_B_re, m_ssm_B_im, m_ssm_C_re, m_ssm_C_im, m_ssm_D, m_ssm_glu_w, m_ssm_glu_b, m_ssm_w_out, v_norm_g, v_attn_w_in, v_attn_q_g, v_attn_k_g, v_attn_w_out, v_ssm_w_in, v_ssm_A_re, v_ssm_A_im, v_ssm_log_dt, v_ssm_B_re, v_ssm_B_im, v_ssm_C_re, v_ssm_C_im, v_ssm_D, v_ssm_glu_w, v_ssm_glu_b, v_ssm_w_out):
    D = x.shape[-1]
    cx, cy, cc = _position()
    chip = 2 * cx + cy
    G = D // GROUP
    Gl = G // N_CHIPS

    def my_half(a):
        return lax.dynamic_slice_in_dim(a, cc * (a.shape[0] // 2), a.shape[0] // 2, axis=0)

    wa = _all_gather8(my_half(attn_w_in[0].astype(BF16)), "attn_in_all_gather").reshape(N_CHIPS, D, D)
    wb_half = my_half(_pack_rest(attn_w_out[0], ssm_w_in[0], ssm_glu_w[0], ssm_w_out[0]).astype(BF16))

    ssm_local = [ssm_A_re[0], ssm_A_im[0], ssm_log_dt[0], ssm_B_re[0], ssm_B_im[0], ssm_C_re[0], ssm_C_im[0], ssm_D[0],
                 ssm_glu_b[0]]
    small_local = _pack_small(ssm_local)
    half_rows = small_local.shape[0] // 2
    small_half = lax.dynamic_slice_in_dim(small_local, cc * half_rows, half_rows, axis=0)
    small_all = _all_gather8(small_half, "ssm_params_all_gather").reshape(N_CHIPS, 2 * half_rows, LANES)
    per_chip = [_unpack_small(small_all[j], [p.shape for p in ssm_local]) for j in range(N_CHIPS)]
    ssm_full = [jnp.concatenate([per_chip[j][i] for j in range(N_CHIPS)], axis=0) for i in range(len(ssm_local))]

    loss, grad_x, ga, gathered_b, small = _local_step(x, loss_target, norm_g, attn_q_g[0], attn_k_g[0], wa, wb_half,
                                                       ssm_full, core=cc)
    loss = lax.psum(loss[0, 0], ("x", "y", "c"))

    q4 = D // N_CHIPS
    chip_sum_a = _add_halves((ga, _sibling_send_half(ga, "grads_a_sibling_half")), cc, "grads_a_add_halves")
    gathered_a = _chip_exchange(chip_sum_a, "grads_a_chip_exchange")
    south = cc == 0

    def both_halves(gathered, tag):
        mine = _sum_leading(gathered, f"grads_{tag}_sum_chips")
        other = _sibling_swap(mine, f"grads_{tag}_sibling_swap")
        return jnp.concatenate([jnp.where(south, mine, other), jnp.where(south, other, mine)], axis=0)

    gd = (both_halves(gathered_a, "a"),) + _unpack_rest(both_halves(gathered_b, "b"))

    small_shapes = [s.shape for s in small]
    small_sum = _sum_leading(_all_gather8(_pack_small(small), "small_grads_all_gather"), "small_grads_sum")
    (dng, dqg, dkg, dl_re, dl_im, dbb_re, dbb_im, dc_re, dc_im, dd_skip, dglu_b) = _unpack_small(small_sum, small_shapes)
    a_re, a_im, log_dt, b_re, b_im = ssm_local[:5]
    _, zoh_vjp = jax.vjp(_zoh, a_re, a_im, log_dt, b_re, b_im)
    da_re, da_im, dlog_dt, db_re, db_im = zoh_vjp((_chip_rows(dl_re, chip, Gl), _chip_rows(dl_im, chip, Gl),
                                                   _chip_rows(dbb_re, chip, Gl), _chip_rows(dbb_im, chip, Gl)))
    grads = {
        "norm_g": dng, "attn_w_in": gd[0][None], "attn_q_g": dqg, "attn_k_g": dkg, "attn_w_out": gd[1][None],
        "ssm_w_in": gd[2][None], "ssm_A_re": da_re[None], "ssm_A_im": da_im[None], "ssm_log_dt": dlog_dt[None],
        "ssm_B_re": db_re[None], "ssm_B_im": db_im[None], "ssm_C_re": _chip_rows(dc_re, chip, Gl)[None],
        "ssm_C_im": _chip_rows(dc_im, chip, Gl)[None], "ssm_D": _chip_rows(dd_skip, chip, q4)[None],
        "ssm_glu_w": gd[3][None], "ssm_glu_b": _chip_rows(dglu_b, chip, q4)[None], "ssm_w_out": gd[4][None],
    }
    weights = dict(norm_g=norm_g, attn_w_in=attn_w_in, attn_q_g=attn_q_g, attn_k_g=attn_k_g, attn_w_out=attn_w_out,
                   ssm_w_in=ssm_w_in, ssm_A_re=ssm_A_re, ssm_A_im=ssm_A_im, ssm_log_dt=ssm_log_dt, ssm_B_re=ssm_B_re,
                   ssm_B_im=ssm_B_im, ssm_C_re=ssm_C_re, ssm_C_im=ssm_C_im, ssm_D=ssm_D, ssm_glu_w=ssm_glu_w,
                   ssm_glu_b=ssm_glu_b, ssm_w_out=ssm_w_out)
    m = dict(norm_g=m_norm_g, attn_w_in=m_attn_w_in, attn_q_g=m_attn_q_g, attn_k_g=m_attn_k_g, attn_w_out=m_attn_w_out,
             ssm_w_in=m_ssm_w_in, ssm_A_re=m_ssm_A_re, ssm_A_im=m_ssm_A_im, ssm_log_dt=m_ssm_log_dt, ssm_B_re=m_ssm_B_re,
             ssm_B_im=m_ssm_B_im, ssm_C_re=m_ssm_C_re, ssm_C_im=m_ssm_C_im, ssm_D=m_ssm_D, ssm_glu_w=m_ssm_glu_w,
             ssm_glu_b=m_ssm_glu_b, ssm_w_out=m_ssm_w_out)
    v = dict(norm_g=v_norm_g, attn_w_in=v_attn_w_in, attn_q_g=v_attn_q_g, attn_k_g=v_attn_k_g, attn_w_out=v_attn_w_out,
             ssm_w_in=v_ssm_w_in, ssm_A_re=v_ssm_A_re, ssm_A_im=v_ssm_A_im, ssm_log_dt=v_ssm_log_dt, ssm_B_re=v_ssm_B_re,
             ssm_B_im=v_ssm_B_im, ssm_C_re=v_ssm_C_re, ssm_C_im=v_ssm_C_im, ssm_D=v_ssm_D, ssm_glu_w=v_ssm_glu_w,
             ssm_glu_b=v_ssm_glu_b, ssm_w_out=v_ssm_w_out)
    names = list(weights)
    dense_names = ("attn_w_in", "attn_w_out", "ssm_w_in", "ssm_glu_w", "ssm_w_out")
    delta, new_m, new_v = {}, {}, {}
    for n in dense_names:
        delta[n], new_m[n], new_v[n] = _adamw(weights[n], grads[n], m[n], v[n], "adamw_" + n)
    small_names = [n for n in names if n not in dense_names]
    small_shapes = [weights[n].shape for n in small_names]
    packs = [_pack_small([d[n] for n in small_names]) for d in (weights, grads, m, v)]
    outs = _adamw(*packs, "adamw_small")
    for res, out in zip((delta, new_m, new_v), outs):
        for n, val in zip(small_names, _unpack_small(out, small_shapes)):
            res[n] = val
    return (loss, grad_x, *[grads[n] for n in names], *[delta[n] for n in names], *[new_m[n] for n in names],
            *[new_v[n] for n in names])
```

```python
import functools
import math

import jax
import jax.numpy as jnp
from jax import lax
from jax.experimental import pallas as pl
from jax.experimental.pallas import tpu as pltpu

F32 = jnp.float32
BF16 = jnp.bfloat16

HEAD_DIM = 128
GROUP = 16
STATE = 64
RMS_EPS = 1e-6
ADAM_LR = 0.001
ADAM_B1 = 0.9
ADAM_B2 = 0.999
ADAM_EPS = 1e-08
ADAM_WD = 0.01
ADAM_STEP = 10

N_CHIPS = 4
N_DEV = 8
SUBLANES = 8
LANES = 128
VMEM_LIMIT = 56 * 1024 * 1024
ROW_BLOCK_ELEMS = 1 << 18
MATMUL_TILE = 1024
MATMUL_PANEL_BYTES = 8 * 1024 * 1024
MESH = pl.DeviceIdType.MESH

NN = (((1,), (0,)), ((), ()))
NT = (((1,), (1,)), ((), ()))
TN = (((0,), (0,)), ((), ()))


def _params(*sem):
    return pltpu.CompilerParams(dimension_semantics=sem, vmem_limit_bytes=VMEM_LIMIT)


def _dot(a, b, dims=NN):
    return lax.dot_general(a, b, dims, preferred_element_type=F32)


def _split(a):
    hi = a.astype(BF16)
    lo = (a - hi.astype(F32)).astype(BF16)
    return hi, lo


def _dot_f32(a, b, dims=NN):
    return _dot(a.astype(BF16), b.astype(BF16), dims)


def _sigmoid(x):
    return 1.0 / (1.0 + jnp.exp(-x))


def _silu_parts(x):
    s = _sigmoid(x)
    return x * s, s * (1.0 + x * (1.0 - s))


_GELU_C = math.sqrt(2.0 / math.pi)


def _gelu_parts(x):
    x2 = x * x
    t = jnp.tanh(_GELU_C * (x + 0.044715 * x * x2))
    val = 0.5 * x * (1.0 + t)
    der = 0.5 * (1.0 + t) + 0.5 * x * (1.0 - t * t) * _GELU_C * (1.0 + 3.0 * 0.044715 * x2)
    return val, der


class _View:
    def __init__(self, buf, shape, locate, row_tile, col_tile):
        self.buf, self.shape, self.locate, self.row_tile, self.col_tile = buf, shape, locate, row_tile, col_tile

    def spec(self, t0, t1, block_of):
        def index(i, j, k):
            rb, cb = block_of(i, j, k)
            slab, r, c = self.locate(rb * t0, cb * t1)
            return slab, r // t0, c // t1
        return pl.BlockSpec((None, t0, t1), index)


def _operand(x):
    return (x.buf, x.shape, x.row_tile, x.col_tile) if isinstance(x, _View) else (x, x.shape, x.shape[0], x.shape[1])


def _matmul(a, b, *, name, ta=False, tb=False, residual=None, out_dtype=F32, out=None, rider=None):
    a_arr, a_shape, a_rt, a_ct = _operand(a)
    b_arr, b_shape, b_rt, b_ct = _operand(b)
    (K, M) = a_shape if ta else a_shape[::-1]
    N = b_shape[0] if tb else b_shape[1]
    a_mt, a_kt = (a_ct, a_rt) if ta else (a_rt, a_ct)
    b_nt, b_kt = (b_rt, b_ct) if tb else (b_ct, b_rt)
    k_cap = MATMUL_PANEL_BYTES // (MATMUL_TILE * max(a_arr.dtype.itemsize, b_arr.dtype.itemsize))
    tm, tn, tk = min(M, MATMUL_TILE, a_mt), min(N, MATMUL_TILE, b_nt), min(K, k_cap, a_kt, b_kt)
    if out is not None:
        tm, tn = min(tm, out.row_tile), min(tn, out.col_tile)
    nk = K // tk
    dims = ((((0,) if ta else (1,)), ((1,) if tb else (0,))), ((), ()))
    n_in = 2 + (residual is not None) + (out is not None)

    def body(*refs):
        a_ref, b_ref = refs[:2]
        r_ref = refs[2] if residual is not None else None
        o_ref = refs[n_in]

        def finish(r):
            if residual is not None:
                r = r + r_ref[...].astype(F32)
            o_ref[...] = r.astype(out_dtype)

        part = _dot(a_ref[...].astype(BF16), b_ref[...].astype(BF16), dims)
        if nk == 1:
            finish(part)
            return
        acc = refs[n_in + 1]
        k = pl.program_id(2)

        @pl.when(k == 0)
        def _():
            acc[...] = part

        @pl.when(k > 0)
        def _():
            acc[...] += part

        @pl.when(k == nk - 1)
        def _():
            finish(acc[...])

    def spec(x, t0, t1, block_of):
        if isinstance(x, _View):
            return x.spec(t0, t1, block_of)
        return pl.BlockSpec((t0, t1), block_of)

    a_spec = spec(a, tk, tm, lambda i, j, k: (k, i)) if ta else spec(a, tm, tk, lambda i, j, k: (i, k))
    b_spec = spec(b, tn, tk, lambda i, j, k: (j, k)) if tb else spec(b, tk, tn, lambda i, j, k: (k, j))
    in_specs = [a_spec, b_spec]
    args = [a_arr, b_arr]
    if residual is not None:
        in_specs.append(pl.BlockSpec((tm, tn), lambda i, j, k: (i, j)))
        args.append(residual)
    aliases = {}
    if out is None:
        out_spec = pl.BlockSpec((tm, tn), lambda i, j, k: (i, j))
        out_shape = jax.ShapeDtypeStruct((M, N), out_dtype)
    else:
        out_spec = out.spec(tm, tn, lambda i, j, k: (i, j))
        out_shape = jax.ShapeDtypeStruct(out.buf.shape, out.buf.dtype)
        out_dtype = out.buf.dtype
        in_specs.append(pl.BlockSpec(memory_space=pl.ANY))
        args.append(out.buf)
        aliases = {len(args) - 1: 0}
    grid = (M // tm, N // tn, nk)
    scratch = [pltpu.VMEM((tm, tn), F32)] if nk > 1 else []
    if rider is None:
        return pl.pallas_call(
            body, name=name, grid=grid, in_specs=in_specs, out_specs=out_spec, out_shape=out_shape,
            scratch_shapes=scratch, input_output_aliases=aliases,
            compiler_params=_params("parallel", "parallel", "arbitrary"),
        )(*args)
    body, in_specs, out_specs, out_shape, scratch, extra = _ride(rider, body, grid, in_specs, [out_spec], [out_shape],
                                                                 scratch)
    return pl.pallas_call(
        body, name=name, grid=grid, in_specs=in_specs, out_specs=out_specs, out_shape=out_shape,
        scratch_shapes=scratch, input_output_aliases=aliases,
        compiler_params=_params("arbitrary", "arbitrary", "arbitrary"),
    )(*args, *extra)


def _row_tile(T, C):
    tr = max(SUBLANES, min(T, ROW_BLOCK_ELEMS // C) // SUBLANES * SUBLANES)
    while T % tr:
        tr -= SUBLANES
    return tr


def _rows_call(body, name, T, C, row_ins, full_ins, row_outs, acc_outs=()):
    tr = _row_tile(T, C)
    row_spec = pl.BlockSpec((tr, C), lambda i: (i, 0))
    in_specs = [row_spec] * len(row_ins) + [pl.BlockSpec(f.shape, lambda i, n=f.ndim: (0,) * n) for f in full_ins]
    out_specs = [row_spec] * len(row_outs) + [pl.BlockSpec(s, lambda i, n=len(s): (0,) * n) for s in acc_outs]
    out_shape = [jax.ShapeDtypeStruct((T, C), d) for d in row_outs] + [jax.ShapeDtypeStruct(s, F32) for s in acc_outs]
    return pl.pallas_call(
        body, name=name, grid=(T // tr,), in_specs=in_specs, out_specs=out_specs, out_shape=out_shape,
        compiler_params=_params("arbitrary" if acc_outs else "parallel"),
    )(*row_ins, *full_ins)


def _rmsnorm_fwd(x, g, name):
    T, C = x.shape

    def body(x_ref, g_ref, h_ref):
        xv = x_ref[...]
        r = lax.rsqrt(jnp.mean(xv * xv, axis=-1, keepdims=True) + RMS_EPS)
        h_ref[...] = ((xv * r) * g_ref[...]).astype(BF16)

    return _rows_call(body, name, T, C, [x], [g.reshape(1, C)], [BF16])[0]


def _rmsnorm_bwd(x, g, dh, dres, name):
    T, C = x.shape

    def body(x_ref, dh_ref, dres_ref, g_ref, dx_ref, dxb_ref, dg_ref):
        @pl.when(pl.program_id(0) == 0)
        def _():
            dg_ref[...] = jnp.zeros_like(dg_ref)

        xv = x_ref[...]
        dhv = dh_ref[...]
        r = lax.rsqrt(jnp.mean(xv * xv, axis=-1, keepdims=True) + RMS_EPS)
        xn = xv * r
        dg_ref[...] += jnp.sum(dhv * xn, axis=0, keepdims=True)
        dxn = dhv * g_ref[...]
        dx = dres_ref[...] + r * (dxn - xn * jnp.mean(dxn * xn, axis=-1, keepdims=True))
        dx_ref[...] = dx
        dxb_ref[...] = dx.astype(BF16)

    return _rows_call(body, name, T, C, [x, dh, dres], [g.reshape(1, C)], [F32, BF16], [(1, C)])


def _qknorm_fwd(q, g, name):
    T, C = q.shape
    tr = min(T, 1024)

    def body(q_ref, g_ref, o_ref):
        xv = q_ref[...]
        r = lax.rsqrt(jnp.mean(xv * xv, axis=-1, keepdims=True) + RMS_EPS)
        o_ref[...] = ((xv * r) * g_ref[...]).astype(BF16)

    spec = pl.BlockSpec((tr, HEAD_DIM), lambda i, h: (i, h))
    return pl.pallas_call(
        body, name=name, grid=(T // tr, C // HEAD_DIM),
        in_specs=[spec, pl.BlockSpec((1, HEAD_DIM), lambda i, h: (0, 0))], out_specs=spec,
        out_shape=jax.ShapeDtypeStruct((T, C), BF16), compiler_params=_params("parallel", "parallel"),
    )(q, g.reshape(1, HEAD_DIM))


def _qknorm_bwd(q, g, dqn, name):
    T, C = q.shape
    tr = min(T, 1024)

    def body(q_ref, d_ref, g_ref, dq_ref, dg_ref):
        @pl.when((pl.program_id(0) == 0) & (pl.program_id(1) == 0))
        def _():
            dg_ref[...] = jnp.zeros_like(dg_ref)

        xv = q_ref[...]
        dv = d_ref[...]
        r = lax.rsqrt(jnp.mean(xv * xv, axis=-1, keepdims=True) + RMS_EPS)
        xn = xv * r
        dg_ref[...] += jnp.sum(dv * xn, axis=0, keepdims=True)
        dxn = dv * g_ref[...]
        dq_ref[...] = (r * (dxn - xn * jnp.mean(dxn * xn, axis=-1, keepdims=True))).astype(BF16)

    spec = pl.BlockSpec((tr, HEAD_DIM), lambda i, h: (i, h))
    gspec = pl.BlockSpec((1, HEAD_DIM), lambda i, h: (0, 0))
    return pl.pallas_call(
        body, name=name, grid=(T // tr, C // HEAD_DIM), in_specs=[spec, spec, gspec], out_specs=[spec, gspec],
        out_shape=[jax.ShapeDtypeStruct((T, C), BF16), jax.ShapeDtypeStruct((1, HEAD_DIM), F32)],
        compiler_params=_params("arbitrary", "arbitrary"),
    )(q, dqn, g.reshape(1, HEAD_DIM))


ATT_TQ = 512
ATT_TK = 256


def _logsig_pair(z):
    sp = jnp.log(1.0 + jnp.exp(-jnp.abs(z)))
    return jnp.minimum(z, 0.0) - sp, jnp.minimum(-z, 0.0) - sp


def _tri(n, strict_upper_src):
    j = lax.broadcasted_iota(jnp.int32, (n, n), 0)
    s = lax.broadcasted_iota(jnp.int32, (n, n), 1)
    if strict_upper_src == "gt":
        m = j > s
    elif strict_upper_src == "le":
        m = j <= s
    else:
        m = j < s
    return jnp.where(m, 1.0, 0.0).astype(BF16)


def _cumdot(x, tri):
    hi, lo = _split(x)
    return _dot(hi, tri) + _dot(lo, tri)


def _attn_tiles(S):
    tq, tk = min(ATT_TQ, S), min(ATT_TK, S)
    return tq, tk, S // tq, tq // tk


def _attn_fwd(qn, kn, v, Bl, S, rider=None):
    T, C = qn.shape
    H = C // HEAD_DIM
    tq, tk, nq, kpq = _attn_tiles(S)
    scale = 1.0 / math.sqrt(HEAD_DIM)

    def body(q_ref, k_ref, v_ref, o_ref, bt_ref):
        i = pl.program_id(2)
        tri = _tri(tk, "gt")
        rowpos = lax.broadcasted_iota(jnp.int32, (tq, tk), 0) + i * tq
        colpos = lax.broadcasted_iota(jnp.int32, (tq, tk), 1)
        o_ref[...] = jnp.zeros_like(o_ref)
        bt_ref[...] = jnp.zeros_like(bt_ref)
        nkb = (i + 1) * kpq

        def step(n, carry):
            j = nkb - 1 - n
            rows = pl.ds(pl.multiple_of(j * tk, tk), tk)
            z = _dot(q_ref[...], k_ref[rows, :], NT) * scale
            mask = (colpos + j * tk) < rowpos
            a, b = _logsig_pair(z)
            b = jnp.where(mask, b, 0.0)
            suffix = _cumdot(b, tri)
            w = jnp.where(mask, jnp.exp(a + suffix + bt_ref[...]), 0.0)
            o_ref[...] += _dot(w.astype(BF16), v_ref[rows, :])
            bt_ref[...] += jnp.sum(b, axis=-1, keepdims=True)
            return carry

        lax.fori_loop(0, nkb, step, 0)

    qspec = pl.BlockSpec((tq, HEAD_DIM), lambda b, h, i: (b * nq + i, h))
    kspec = pl.BlockSpec((S, HEAD_DIM), lambda b, h, i: (b, h))
    btspec = pl.BlockSpec((None, None, tq, 1), lambda b, h, i: (b, h, i, 0))
    grid = (Bl, H, nq)
    body, in_specs, out_specs, out_shape, scratch, extra = _ride(
        rider, body, grid, [qspec, kspec, kspec], [qspec, btspec],
        [jax.ShapeDtypeStruct((T, C), F32), jax.ShapeDtypeStruct((Bl, H, S, 1), F32)], [])
    return pl.pallas_call(
        body, name="attn_fwd", grid=grid, in_specs=in_specs, out_specs=out_specs, out_shape=out_shape,
        scratch_shapes=scratch, compiler_params=_params("arbitrary", "arbitrary", "arbitrary"),
    )(qn, kn, v, *extra)


def _attn_bwd(qn, kn, v, do, btot, Bl, S, rider=None):
    T, C = qn.shape
    H = C // HEAD_DIM
    tq, tk, nq, kpq = _attn_tiles(S)
    scale = 1.0 / math.sqrt(HEAD_DIM)

    def body(q_ref, k_ref, v_ref, do_ref, bt_ref, dq_ref, dk_ref, dv_ref, pb_ref, pdl_ref):
        i = pl.program_id(2)

        @pl.when(i == 0)
        def _():
            dk_ref[...] = jnp.zeros_like(dk_ref)
            dv_ref[...] = jnp.zeros_like(dv_ref)

        tri_le = _tri(tk, "le")
        tri_lt = _tri(tk, "lt")
        rowpos = lax.broadcasted_iota(jnp.int32, (tq, tk), 0) + i * tq
        colpos = lax.broadcasted_iota(jnp.int32, (tq, tk), 1)
        dq_ref[...] = jnp.zeros_like(dq_ref)
        pb_ref[...] = bt_ref[...]
        pdl_ref[...] = jnp.zeros_like(pdl_ref)

        def step(j, carry):
            rows = pl.ds(pl.multiple_of(j * tk, tk), tk)
            kj = k_ref[rows, :]
            z = _dot(q_ref[...], kj, NT) * scale
            mask = (colpos + j * tk) < rowpos
            a, b = _logsig_pair(z)
            b = jnp.where(mask, b, 0.0)
            suffix = pb_ref[...] - _cumdot(b, tri_le)
            w = jnp.where(mask, jnp.exp(a + suffix), 0.0)
            dl = _dot(do_ref[...], v_ref[rows, :], NT) * w
            prefix = pdl_ref[...] + _cumdot(dl, tri_lt)
            beta = jnp.exp(a)
            dzb = (jnp.where(mask, dl * (1.0 - beta) - beta * prefix, 0.0) * scale).astype(BF16)
            dq_ref[...] += _dot(dzb, kj)
            dk_ref[rows, :] += _dot(dzb, q_ref[...], TN)
            dv_ref[rows, :] += _dot(w.astype(BF16), do_ref[...], TN)
            pb_ref[...] -= jnp.sum(b, axis=-1, keepdims=True)
            pdl_ref[...] += jnp.sum(dl, axis=-1, keepdims=True)
            return carry

        lax.fori_loop(0, (i + 1) * kpq, step, 0)

    qspec = pl.BlockSpec((tq, HEAD_DIM), lambda b, h, i: (b * nq + i, h))
    kspec = pl.BlockSpec((S, HEAD_DIM), lambda b, h, i: (b, h))
    btspec = pl.BlockSpec((None, None, tq, 1), lambda b, h, i: (b, h, i, 0))
    grid = (Bl, H, nq)
    body, in_specs, out_specs, out_shape, scratch, extra = _ride(
        rider, body, grid, [qspec, kspec, kspec, qspec, btspec], [qspec, kspec, kspec],
        [jax.ShapeDtypeStruct((T, C), F32)] * 3, [pltpu.VMEM((tq, 1), F32), pltpu.VMEM((tq, 1), F32)])
    return pl.pallas_call(
        body, name="attn_bwd", grid=grid, in_specs=in_specs, out_specs=out_specs, out_shape=out_shape,
        scratch_shapes=scratch, compiler_params=_params("arbitrary", "arbitrary", "arbitrary"),
    )(qn, kn, v, do, btot, *extra)


SSM_TIME_BLOCK = 512
CHUNK = SUBLANES


def _cmadd(xr, xi, ar, ai, sr, si):
    return xr + ar * sr - ai * si, xi + ar * si + ai * sr


def _chunk_scan(xr, xi, tab_ref, cr, ci, reverse):
    for lvl, d in enumerate((1, 2, 4)):
        shift = (CHUNK - d) if reverse else d
        sr = pltpu.roll(xr, shift, 0)
        si = pltpu.roll(xi, shift, 0)
        ar = tab_ref[pl.ds((2 * lvl) * CHUNK, CHUNK), :]
        ai = tab_ref[pl.ds((2 * lvl + 1) * CHUNK, CHUNK), :]
        xr, xi = _cmadd(xr, xi, ar, ai, sr, si)
    pr = tab_ref[pl.ds(6 * CHUNK, CHUNK), :]
    pi = tab_ref[pl.ds(7 * CHUNK, CHUNK), :]
    return _cmadd(xr, xi, pr, pi, cr, ci)


def _ssm_dims(S, C):
    G = C // GROUP
    GT = min(16, G)
    return G, GT, G // GT, GT * GROUP, GT * STATE, min(SSM_TIME_BLOCK, S)


def _ssm_fwd(u, mats, Bl, S):
    T, C = u.shape
    G, GT, ngt, cw, sw, TB = _ssm_dims(S, C)
    ntb = S // TB
    nch = TB // CHUNK

    def body(u_ref, bre_ref, bim_ref, cre_ref, cim_ref, d_ref, tab_ref, y_ref, hr_ref, hi_ref, car_r, car_i):
        @pl.when(pl.program_id(2) == 0)
        def _():
            car_r[...] = jnp.zeros_like(car_r)
            car_i[...] = jnp.zeros_like(car_i)

        uv = u_ref[...]
        hr_ref[...] = _dot_f32(uv, bre_ref[...])
        hi_ref[...] = _dot_f32(uv, bim_ref[...])

        def step(n, carry):
            cr, ci = carry
            rows = pl.ds(pl.multiple_of(n * CHUNK, CHUNK), CHUNK)
            xr, xi = _chunk_scan(hr_ref[rows, :], hi_ref[rows, :], tab_ref, cr, ci, False)
            hr_ref[rows, :] = xr
            hi_ref[rows, :] = xi
            last = (CHUNK - 1, CHUNK)
            return (jnp.broadcast_to(xr[last[0]:last[1], :], xr.shape), jnp.broadcast_to(xi[last[0]:last[1], :], xi.shape))

        cr, ci = lax.fori_loop(0, nch, step, (car_r[...], car_i[...]))
        car_r[...] = cr
        car_i[...] = ci
        y_ref[...] = _dot_f32(hr_ref[...], cre_ref[...]) - _dot_f32(hi_ref[...], cim_ref[...]) + d_ref[...] * uv

    uspec = pl.BlockSpec((TB, cw), lambda g, b, t: (b * ntb + t, g))
    hspec = pl.BlockSpec((TB, sw), lambda g, b, t: (b * ntb + t, g))

    def gspec(r, c):
        return pl.BlockSpec((None, r, c), lambda g, b, t: (g, 0, 0))

    return pl.pallas_call(
        body, name="ssm_fwd", grid=(ngt, Bl, ntb),
        in_specs=[uspec, gspec(cw, sw), gspec(cw, sw), gspec(sw, cw), gspec(sw, cw), gspec(1, cw), gspec(8 * CHUNK, sw)],
        out_specs=[uspec, hspec, hspec],
        out_shape=[jax.ShapeDtypeStruct((T, C), F32), jax.ShapeDtypeStruct((T, G * STATE), F32),
                   jax.ShapeDtypeStruct((T, G * STATE), F32)],
        scratch_shapes=[pltpu.VMEM((CHUNK, sw), F32), pltpu.VMEM((CHUNK, sw), F32)],
        compiler_params=_params("parallel", "arbitrary", "arbitrary"),
    )(u, mats["bbd_re"], mats["bbd_im"], mats["cbd_re"], mats["cbd_im"], mats["d"], mats["tab_fwd"])


def _ssm_bwd(u, dy, h_re, h_im, mats, Bl, S):
    T, C = u.shape
    G, GT, ngt, cw, sw, TB = _ssm_dims(S, C)
    ntb = S // TB
    nch = TB // CHUNK
    rpb = TB // CHUNK

    def body(u_ref, dy_ref, hr_ref, hi_ref, hpr_ref, hpi_ref, ctr_ref, cti_ref, btr_ref, bti_ref, d_ref, tab_ref,
             du_ref, dbr_ref, dbi_ref, dcr_ref, dci_ref, dlr_ref, dli_ref, dd_ref, gr_ref, gi_ref, car_r, car_i):
        b = pl.program_id(1)
        t = pl.program_id(2)

        @pl.when((b == 0) & (t == 0))
        def _():
            for ref in (dbr_ref, dbi_ref, dcr_ref, dci_ref, dlr_ref, dli_ref, dd_ref):
                ref[...] = jnp.zeros_like(ref)

        @pl.when(t == 0)
        def _():
            car_r[...] = jnp.zeros_like(car_r)
            car_i[...] = jnp.zeros_like(car_i)

        uv = u_ref[...]
        dyv = dy_ref[...]
        gr_ref[...] = _dot_f32(dyv, ctr_ref[...])
        gi_ref[...] = -_dot_f32(dyv, cti_ref[...])
        alive = jnp.where(t == ntb - 1, 0.0, 1.0)
        row0 = lax.broadcasted_iota(jnp.int32, (CHUNK, sw), 0) == 0

        def step(m, carry):
            cr, ci, ar, ai = carry
            n = nch - 1 - m
            rows = pl.ds(pl.multiple_of(n * CHUNK, CHUNK), CHUNK)
            prow = pl.ds(pl.multiple_of(jnp.maximum(n - 1, 0) * CHUNK, CHUNK), CHUNK)
            xr, xi = _chunk_scan(gr_ref[rows, :], gi_ref[rows, :], tab_ref, cr, ci, True)
            gr_ref[rows, :] = xr
            gi_ref[rows, :] = xi
            first = n == 0
            pr = jnp.where(first, hpr_ref[...] * alive, hr_ref[prow, :])
            pi = jnp.where(first, hpi_ref[...] * alive, hi_ref[prow, :])
            sr = jnp.where(row0, pltpu.roll(pr, 1, 0), pltpu.roll(hr_ref[rows, :], 1, 0))
            si = jnp.where(row0, pltpu.roll(pi, 1, 0), pltpu.roll(hi_ref[rows, :], 1, 0))
            ar = ar + xr * sr + xi * si
            ai = ai + xi * sr - xr * si
            return (jnp.broadcast_to(xr[0:1, :], xr.shape), jnp.broadcast_to(xi[0:1, :], xi.shape), ar, ai)

        zero = jnp.zeros((CHUNK, sw), F32)
        cr, ci, ar, ai = lax.fori_loop(0, nch, step, (car_r[...], car_i[...], zero, zero))
        car_r[...] = cr
        car_i[...] = ci
        dlr_ref[...] += ar
        dli_ref[...] += ai
        gr = gr_ref[...]
        gi = gi_ref[...]
        dbr_ref[...] += _dot_f32(uv, gr, TN)
        dbi_ref[...] += _dot_f32(uv, gi, TN)
        dcr_ref[...] += _dot_f32(hr_ref[...], dyv, TN)
        dci_ref[...] -= _dot_f32(hi_ref[...], dyv, TN)
        dd_ref[...] += jnp.sum(dyv * uv, axis=0, keepdims=True)
        du_ref[...] = (_dot_f32(gr, btr_ref[...]) + _dot_f32(gi, bti_ref[...]) + d_ref[...] * dyv).astype(BF16)

    def tblk(b, t):
        return b * ntb + (ntb - 1 - t)

    uspec = pl.BlockSpec((TB, cw), lambda g, b, t: (tblk(b, t), g))
    hspec = pl.BlockSpec((TB, sw), lambda g, b, t: (tblk(b, t), g))
    hpspec = pl.BlockSpec((CHUNK, sw), lambda g, b, t: (jnp.maximum(tblk(b, t) * rpb - 1, 0), g))

    def gspec(r, c):
        return pl.BlockSpec((None, r, c), lambda g, b, t: (g, 0, 0))

    def gshape(r, c):
        return jax.ShapeDtypeStruct((ngt, r, c), F32)

    return pl.pallas_call(
        body, name="ssm_bwd", grid=(ngt, Bl, ntb),
        in_specs=[uspec, uspec, hspec, hspec, hpspec, hpspec, gspec(cw, sw), gspec(cw, sw), gspec(sw, cw), gspec(sw, cw),
                  gspec(1, cw), gspec(8 * CHUNK, sw)],
        out_specs=[uspec, gspec(cw, sw), gspec(cw, sw), gspec(sw, cw), gspec(sw, cw), gspec(CHUNK, sw), gspec(CHUNK, sw),
                   gspec(1, cw)],
        out_shape=[jax.ShapeDtypeStruct((T, C), BF16), gshape(cw, sw), gshape(cw, sw), gshape(sw, cw), gshape(sw, cw),
                   gshape(CHUNK, sw), gshape(CHUNK, sw), gshape(1, cw)],
        scratch_shapes=[pltpu.VMEM((TB, sw), F32), pltpu.VMEM((TB, sw), F32), pltpu.VMEM((CHUNK, sw), F32),
                        pltpu.VMEM((CHUNK, sw), F32)],
        compiler_params=_params("arbitrary", "arbitrary", "arbitrary"),
    )(u, dy, h_re, h_im, h_re, h_im, mats["cbdT_re"], mats["cbdT_im"], mats["bbdT_re"], mats["bbdT_im"], mats["d"],
      mats["tab_rev"])


def _zoh(a_re, a_im, log_dt, b_re, b_im):
    dt = jnp.exp(log_dt)[:, None]
    mag = jnp.exp(a_re * dt)
    l_re = mag * jnp.cos(a_im * dt)
    l_im = mag * jnp.sin(a_im * dt)
    den = a_re * a_re + a_im * a_im
    f_re = ((l_re - 1.0) * a_re + l_im * a_im) / den
    f_im = (l_im * a_re - (l_re - 1.0) * a_im) / den
    bb_re = f_re[..., None] * b_re - f_im[..., None] * b_im
    bb_im = f_re[..., None] * b_im + f_im[..., None] * b_re
    return l_re, l_im, bb_re, bb_im


def _ssm_matrices(a_re, a_im, log_dt, b_re, b_im, c_re, c_im, d, S):
    G = a_re.shape[0]
    _, GT, ngt, cw, sw, _ = _ssm_dims(S, G * GROUP)
    _, _, bb_re, bb_im = _zoh(a_re, a_im, log_dt, b_re, b_im)
    eye = jnp.eye(GT, dtype=F32)

    def bd_b(bb):
        return jnp.einsum("tgpi,gh->tgihp", bb.reshape(ngt, GT, STATE, GROUP), eye).reshape(ngt, cw, sw)

    def bd_c(c):
        return jnp.einsum("tgip,gh->tgphi", c.reshape(ngt, GT, GROUP, STATE), eye).reshape(ngt, sw, cw)

    dt = jnp.exp(log_dt)[:, None]

    def power(k, conj):
        mag = jnp.exp(k * a_re * dt)
        ang = k * a_im * dt
        return (mag * jnp.cos(ang)).reshape(ngt, 1, sw), ((-1.0 if conj else 1.0) * mag * jnp.sin(ang)).reshape(ngt, 1, sw)

    r = jnp.arange(CHUNK)[None, :, None]

    def table(reverse):
        parts = []
        for dd in (1, 2, 4):
            pr, pi = power(float(dd), reverse)
            keep = (r <= CHUNK - 1 - dd) if reverse else (r >= dd)
            parts += [jnp.where(keep, pr, 0.0), jnp.where(keep, pi, 0.0)]
        exps = [(CHUNK - k) if reverse else (k + 1) for k in range(CHUNK)]
        pw = [power(float(e), reverse) for e in exps]
        parts += [jnp.concatenate([p[0] for p in pw], axis=1), jnp.concatenate([p[1] for p in pw], axis=1)]
        return jnp.concatenate([jnp.broadcast_to(p, (ngt, CHUNK, sw)) for p in parts], axis=1)

    mats = dict(bbd_re=bd_b(bb_re).astype(BF16), bbd_im=bd_b(bb_im).astype(BF16), cbd_re=bd_c(c_re).astype(BF16),
                cbd_im=bd_c(c_im).astype(BF16), d=d.reshape(ngt, 1, cw), tab_fwd=table(False), tab_rev=table(True))
    for k in ("bbd_re", "bbd_im", "cbd_re", "cbd_im"):
        mats[k.replace("bd_", "bdT_")] = jnp.swapaxes(mats[k], 1, 2)
    return mats


def _ssm_unblock(dbr, dbi, dcr, dci, dlr, dli, dd, G):
    ngt = dbr.shape[0]
    GT = G // ngt
    eye = jnp.eye(GT, dtype=F32)

    def ub(x):
        return jnp.einsum("tgihp,gh->tgpi", x.reshape(ngt, GT, GROUP, GT, STATE), eye).reshape(G, STATE, GROUP)

    def uc(x):
        return jnp.einsum("tgphi,gh->tgip", x.reshape(ngt, GT, STATE, GT, GROUP), eye).reshape(G, GROUP, STATE)

    return (dlr.sum(axis=1).reshape(G, STATE), dli.sum(axis=1).reshape(G, STATE), ub(dbr), ub(dbi), uc(dcr), uc(dci),
            dd.reshape(G * GROUP))


def _attn_gate_fwd(o, gate):
    T, C = o.shape

    def body(o_ref, g_ref, out_ref):
        out_ref[...] = (o_ref[...] * _silu_parts(g_ref[...])[0]).astype(BF16)

    return _rows_call(body, "attn_gate_fwd", T, C, [o, gate], [], [BF16])[0]


def _attn_gate_bwd(dog, o, gate):
    T, C = o.shape

    def body(d_ref, o_ref, g_ref, do_ref, dg_ref):
        val, der = _silu_parts(g_ref[...])
        dv = d_ref[...]
        do_ref[...] = (dv * val).astype(BF16)
        dg_ref[...] = (dv * o_ref[...] * der).astype(BF16)

    return _rows_call(body, "attn_gate_bwd", T, C, [dog, o, gate], [], [BF16, BF16])


def _gelu_fwd(y):
    T, C = y.shape

    def body(y_ref, out_ref):
        out_ref[...] = _gelu_parts(y_ref[...])[0].astype(BF16)

    return _rows_call(body, "gelu_fwd", T, C, [y], [], [BF16])[0]


def _glu_fwd(y, gl, gate, glu_b):
    T, C = y.shape

    def body(y_ref, gl_ref, g_ref, b_ref, out_ref):
        yg = _gelu_parts(y_ref[...])[0]
        sg = _sigmoid(gl_ref[...] + b_ref[...])
        out_ref[...] = (yg * sg * _silu_parts(g_ref[...])[0]).astype(BF16)

    return _rows_call(body, "glu_fwd", T, C, [y, gl, gate], [glu_b.reshape(1, C)], [BF16])[0]


def _glu_bwd(dy3, y, gl, gate, glu_b):
    T, C = y.shape

    def body(d_ref, y_ref, gl_ref, g_ref, b_ref, dgl_ref, dgate_ref, t1_ref, db_ref):
        @pl.when(pl.program_id(0) == 0)
        def _():
            db_ref[...] = jnp.zeros_like(db_ref)

        yg = _gelu_parts(y_ref[...])[0]
        sg = _sigmoid(gl_ref[...] + b_ref[...])
        sl, sld = _silu_parts(g_ref[...])
        dv = d_ref[...]
        dy2 = dv * sl
        dgl = dy2 * yg * sg * (1.0 - sg)
        dgl_ref[...] = dgl.astype(BF16)
        dgate_ref[...] = (dv * (yg * sg) * sld).astype(BF16)
        t1_ref[...] = dy2 * sg
        db_ref[...] += jnp.sum(dgl, axis=0, keepdims=True)

    return _rows_call(body, "glu_bwd", T, C, [dy3, y, gl, gate], [glu_b.reshape(1, C)], [BF16, BF16, F32], [(1, C)])


def _gelu_bwd(t1, t2, y):
    T, C = y.shape

    def body(a_ref, b_ref, y_ref, out_ref):
        out_ref[...] = (a_ref[...] + b_ref[...]) * _gelu_parts(y_ref[...])[1]

    return _rows_call(body, "gelu_bwd", T, C, [t1, t2, y], [], [F32])[0]


def _loss_head(x2, target):
    T, C = x2.shape

    def body(x_ref, t_ref, d_ref, db_ref, l_ref):
        @pl.when(pl.program_id(0) == 0)
        def _():
            l_ref[...] = jnp.zeros_like(l_ref)

        e = x_ref[...] - t_ref[...]
        d = e * (1.0 / C)
        d_ref[...] = d
        db_ref[...] = d.astype(BF16)
        l_ref[...] += 0.5 * jnp.sum(jnp.sum(e * e, axis=-1, keepdims=True) * (1.0 / C), axis=0, keepdims=True)

    return _rows_call(body, "loss_head", T, C, [x2, target], [], [F32, BF16], [(1, 1)])


def _adamw(w, g, m, v, name):
    shape = w.shape
    C = shape[-1]
    R = w.size // C
    bc1 = 1.0 - ADAM_B1 ** ADAM_STEP
    bc2 = 1.0 - ADAM_B2 ** ADAM_STEP

    def body(w_ref, g_ref, m_ref, v_ref, d_ref, nm_ref, nv_ref):
        gv = g_ref[...]
        mn = ADAM_B1 * m_ref[...] + (1.0 - ADAM_B1) * gv
        vn = ADAM_B2 * v_ref[...] + (1.0 - ADAM_B2) * (gv * gv)
        d_ref[...] = -ADAM_LR * ((mn / bc1) / (jnp.sqrt(vn / bc2) + ADAM_EPS) + ADAM_WD * w_ref[...])
        nm_ref[...] = mn
        nv_ref[...] = vn

    outs = _rows_call(body, name, R, C, [a.reshape(R, C) for a in (w, g, m, v)], [], [F32, F32, F32])
    return [o.reshape(shape) for o in outs]


def _sum_leading(x, name):
    n, R, C = x.shape
    tr = _row_tile(R, C * n)

    def body(x_ref, o_ref):
        acc = x_ref[0].astype(F32)
        for k in range(1, n):
            acc = acc + x_ref[k].astype(F32)
        o_ref[...] = acc

    return pl.pallas_call(
        body, name=name, grid=(R // tr,), in_specs=[pl.BlockSpec((n, tr, C), lambda i: (0, i, 0))],
        out_specs=pl.BlockSpec((tr, C), lambda i: (i, 0)), out_shape=jax.ShapeDtypeStruct((R, C), F32),
        compiler_params=_params("parallel"),
    )(x)


def _add_halves(g, c, name):
    full, recv = g
    n, R, C = full.shape
    half = R // 2
    tr = _row_tile(half, C)
    nb = half // tr

    def body(c_ref, a_ref, b_ref, o_ref):
        o_ref[...] = (a_ref[...] + b_ref[...]).astype(BF16)

    grid_spec = pltpu.PrefetchScalarGridSpec(
        num_scalar_prefetch=1, grid=(n, nb),
        in_specs=[pl.BlockSpec((None, tr, C), lambda j, i, c_ref: (j, c_ref[0] * nb + i, 0)),
                  pl.BlockSpec((None, tr, C), lambda j, i, c_ref: (j, i, 0))],
        out_specs=pl.BlockSpec((None, tr, C), lambda j, i, c_ref: (j, i, 0)))
    return pl.pallas_call(
        body, name=name, grid_spec=grid_spec, out_shape=jax.ShapeDtypeStruct((n, half, C), BF16),
        compiler_params=_params("parallel", "parallel"),
    )(c.reshape(1).astype(jnp.int32), full, recv)


ANY = pl.BlockSpec(memory_space=pl.ANY)


def _position():
    return lax.axis_index("x"), lax.axis_index("y"), lax.axis_index("c")


def _all_gather8(blk, name):
    return _standalone(_all_gather_rider(blk), name)[0]


class _Rider:
    def __init__(self, arrays, out_shapes, sems, start, finish):
        self.arrays, self.out_shapes, self.sems, self.start, self.finish = arrays, out_shapes, sems, start, finish


def _standalone(rider, name):
    def body(*refs):
        rider.start(*refs)
        rider.finish(*refs)

    return pl.pallas_call(
        body, name=name, in_specs=[ANY] * len(rider.arrays), out_specs=[ANY] * len(rider.out_shapes),
        out_shape=rider.out_shapes, scratch_shapes=rider.sems,
    )(*rider.arrays)


def _ride(rider, body, grid, in_specs, out_specs, out_shape, scratch):
    if rider is None:
        return body, in_specs, out_specs, out_shape, scratch, []
    ni, no, ns = len(in_specs), len(out_specs), len(scratch)
    ri, ro = len(rider.arrays), len(rider.out_shapes)

    def full(*refs):
        ins, refs = refs[:ni], refs[ni:]
        r_ins, refs = refs[:ri], refs[ri:]
        outs, refs = refs[:no], refs[no:]
        r_outs, refs = refs[:ro], refs[ro:]
        scr, r_sems = refs[:ns], refs[ns:]
        ids = [pl.program_id(a) for a in range(len(grid))]
        first = functools.reduce(jnp.logical_and, [i == 0 for i in ids])
        last = functools.reduce(jnp.logical_and, [i == g - 1 for i, g in zip(ids, grid)])

        @pl.when(first)
        def _():
            rider.start(*r_ins, *r_outs, *r_sems)

        body(*ins, *outs, *scr)

        @pl.when(last)
        def _():
            rider.finish(*r_ins, *r_outs, *r_sems)

    return (full, in_specs + [ANY] * ri, out_specs + [ANY] * ro, out_shape + rider.out_shapes, scratch + rider.sems,
            rider.arrays)


def _all_gather_rider(blk):
    M, N = blk.shape

    def copies(x_ref, out_ref, send_sems, recv_sems, local_sem):
        x, y, c = _position()
        me, sibling = (x, y, c), (x, y, 1 - c)
        chips = [(1 - x, y), (x, 1 - y), (1 - x, 1 - y)]

        def slab(px, py, pc):
            return out_ref.at[4 * px + 2 * py + pc]

        def copy(k, block, to, src=None):
            return pltpu.make_async_remote_copy(
                src_ref=slab(*block) if src is None else src, dst_ref=slab(*block),
                send_sem=send_sems.at[k], recv_sem=recv_sems.at[k], device_id=to, device_id_type=MESH)

        mine = pltpu.make_async_copy(x_ref, slab(*me), local_sem)
        first = [copy(0, me, sibling, src=x_ref)]
        first += [copy(1 + j, me, (*chip, c), src=x_ref) for j, chip in enumerate(chips)]
        passed = [copy(4 + j, (*chip, c), sibling) for j, chip in enumerate(chips)]
        arrivals = [copy(1 + j, (*chip, c), me) for j, chip in enumerate(chips)]
        from_sibling = [copy(0, sibling, me)] + [copy(4 + j, (*chip, 1 - c), me) for j, chip in enumerate(chips)]
        return mine, first, passed, arrivals, from_sibling

    def start(*refs):
        mine, first, _, _, _ = copies(*refs)
        mine.start()
        for cp in first:
            cp.start()

    def finish(*refs):
        mine, first, passed, arrivals, from_sibling = copies(*refs)
        for arrival, onward in zip(arrivals, passed):
            arrival.wait_recv()
            onward.start()
        for cp in from_sibling:
            cp.wait_recv()
        for cp in first + passed:
            cp.wait_send()
        mine.wait()

    return _Rider([blk], [jax.ShapeDtypeStruct((N_DEV, M, N), blk.dtype)],
                  [pltpu.SemaphoreType.DMA((7,)), pltpu.SemaphoreType.DMA((7,)), pltpu.SemaphoreType.DMA], start, finish)


def _sibling_half_rider(g):
    n, R, C = g.shape
    half = R // 2

    def copy(g_ref, out_ref, send_sem, recv_sem):
        x, y, c = _position()
        return pltpu.make_async_remote_copy(
            src_ref=g_ref.at[:, pl.ds((1 - c) * half, half), :], dst_ref=out_ref, send_sem=send_sem, recv_sem=recv_sem,
            device_id=(x, y, 1 - c), device_id_type=MESH)

    return _Rider([g], [jax.ShapeDtypeStruct((n, half, C), g.dtype)], [pltpu.SemaphoreType.DMA, pltpu.SemaphoreType.DMA],
                  lambda *refs: copy(*refs).start(), lambda *refs: copy(*refs).wait())


def _sibling_send_half(g, name):
    return _standalone(_sibling_half_rider(g), name)[0]


def _chip_exchange_rider(p, row_off=0, rows=None):
    rows = p.shape[1] if rows is None else rows

    def copies(p_ref, out_ref, send_sems, recv_sems, local_sem):
        x, y, c = _position()
        my = 2 * x + y
        chips = [(1 - x, y), (x, 1 - y), (1 - x, 1 - y)]

        def src(slab):
            return p_ref.at[slab, pl.ds(row_off, rows), :]

        mine = pltpu.make_async_copy(src(my), out_ref.at[my], local_sem)
        sends = [pltpu.make_async_remote_copy(
            src_ref=src(2 * px + py), dst_ref=out_ref.at[my], send_sem=send_sems.at[k], recv_sem=recv_sems.at[k],
            device_id=(px, py, c), device_id_type=MESH) for k, (px, py) in enumerate(chips)]
        arrivals = [pltpu.make_async_remote_copy(
            src_ref=src(my), dst_ref=out_ref.at[2 * px + py], send_sem=send_sems.at[k], recv_sem=recv_sems.at[k],
            device_id=(px, py, c), device_id_type=MESH) for k, (px, py) in enumerate(chips)]
        return mine, sends, arrivals

    def start(*refs):
        mine, sends, _ = copies(*refs)
        mine.start()
        for cp in sends:
            cp.start()

    def finish(*refs):
        mine, sends, arrivals = copies(*refs)
        for cp in arrivals:
            cp.wait_recv()
        for cp in sends:
            cp.wait_send()
        mine.wait()

    return _Rider([p], [jax.ShapeDtypeStruct((p.shape[0], rows, p.shape[2]), p.dtype)],
                  [pltpu.SemaphoreType.DMA((3,)), pltpu.SemaphoreType.DMA((3,)), pltpu.SemaphoreType.DMA], start, finish)


def _sibling_swap(f, name):
    def body(f_ref, out_ref, send_sem, recv_sem):
        x, y, c = _position()
        cp = pltpu.make_async_remote_copy(src_ref=f_ref, dst_ref=out_ref, send_sem=send_sem, recv_sem=recv_sem,
                                          device_id=(x, y, 1 - c), device_id_type=MESH)
        cp.start()
        cp.wait()

    return pl.pallas_call(
        body, name=name, in_specs=[ANY], out_specs=ANY, out_shape=jax.ShapeDtypeStruct(f.shape, f.dtype),
        scratch_shapes=[pltpu.SemaphoreType.DMA, pltpu.SemaphoreType.DMA],
    )(f)


def _pack_rest(attn_out, ssm_in, glu_w, ssm_out):
    hd = ssm_in.shape[0] // 2
    return jnp.concatenate([attn_out, jnp.concatenate([ssm_in[:hd], ssm_in[hd:]], axis=1), glu_w, ssm_out], axis=0)


def _unpack_rest(p):
    D = p.shape[-1]
    q, hd = D // N_CHIPS, D // 2
    o = [0, q, q + hd, 2 * q + hd, 3 * q + hd]
    ssm_in = p[o[1]:o[2]]
    return p[o[0]:o[1]], jnp.concatenate([ssm_in[:, :hd], ssm_in[:, hd:]], axis=0), p[o[2]:o[3]], p[o[3]:o[4]]


def _attn_in_views(buf_a):
    D = buf_a.shape[-1]
    return [_View(buf_a, (D, D), lambda r, c, j=j: (j, r, c), D, D) for j in range(N_CHIPS)]


def _rest_views(buf_b):
    D = buf_b.shape[-1]
    q, hd = D // N_CHIPS, D // 2
    o_out, o_in, o_glu, o_sout = 0, q, q + hd, 2 * q + hd

    def row_sharded(off):
        return _View(buf_b, (D, D), lambda r, c: (r // q, off + r % q, c), math.gcd(q, off), D)

    def ssm_in(part):
        return _View(buf_b, (D, D), lambda r, c: (2 * part + c // hd, o_in + r % hd, (r // hd) * hd + c % hd),
                     math.gcd(hd, o_in), hd)

    return dict(attn_out=row_sharded(o_out), ssm_in_u=ssm_in(0), ssm_in_gate=ssm_in(1), glu=row_sharded(o_glu),
                ssm_out=row_sharded(o_sout))


def _pack_small(parts):
    flat = jnp.concatenate([p.reshape(-1) for p in parts])
    pad = (-flat.size) % (2 * SUBLANES * LANES)
    return jnp.pad(flat, (0, pad)).reshape(-1, LANES)


def _unpack_small(buf, shapes):
    flat = buf.reshape(-1)
    out, off = [], 0
    for s in shapes:
        n = math.prod(s)
        out.append(flat[off:off + n].reshape(s))
        off += n
    return out


def _local_step(x, target, norm_g, q_g, k_g, wa, wb, ssm_small, core=None):
    Bl, S, D = x.shape
    T = Bl * S
    x0 = x.reshape(T, D)
    tgt = target.reshape(T, D)
    a_re, a_im, log_dt, b_re, b_im, c_re, c_im, d_skip, glu_b = ssm_small
    G = a_re.shape[0]
    mats = _ssm_matrices(a_re, a_im, log_dt, b_re, b_im, c_re, c_im, d_skip, S)
    w_in = _attn_in_views(wa)
    rows_b = wb.shape[-2] * (1 if core is None else 2)
    ga = lax.empty(wa.shape, F32)
    gb = lax.empty((N_CHIPS, rows_b, D), F32)

    def wgrad(a, b, key, name):
        return _matmul(a, b, name=name, ta=True, out=_rest_views(gb)[key])

    h0 = _rmsnorm_fwd(x0, norm_g[0], "norm0_fwd")
    q, k, v, gate = [_matmul(h0, w_in[j], name=f"attn_in_{j}", out_dtype=(BF16 if j == 2 else F32)) for j in range(4)]
    qn = _qknorm_fwd(q, q_g, "qnorm_fwd")
    kn = _qknorm_fwd(k, k_g, "knorm_fwd")
    if core is None:
        o, btot = _attn_fwd(qn, kn, v, Bl, S)
    else:
        o, btot, wb = _attn_fwd(qn, kn, v, Bl, S, rider=_all_gather_rider(wb))
        wb = wb.reshape(N_CHIPS, rows_b, D)
    w = _rest_views(wb)
    og = _attn_gate_fwd(o, gate)
    x1 = _matmul(og, w["attn_out"], name="attn_out", residual=x0)

    h1 = _rmsnorm_fwd(x1, norm_g[1], "norm1_fwd")
    u = _matmul(h1, w["ssm_in_u"], name="ssm_in_u")
    gate2 = _matmul(h1, w["ssm_in_gate"], name="ssm_in_gate")
    y, hs_re, hs_im = _ssm_fwd(u, mats, Bl, S)
    yg = _gelu_fwd(y)
    gl = _matmul(yg, w["glu"], name="glu_mm")
    y3 = _glu_fwd(y, gl, gate2, glu_b)
    x2 = _matmul(y3, w["ssm_out"], name="ssm_out", residual=x1)

    dx2, dx2b, loss = _loss_head(x2, tgt)

    dy3 = _matmul(dx2b, w["ssm_out"], name="ssm_out_dgrad", tb=True)
    gb = wgrad(y3, dx2b, "ssm_out", "ssm_out_wgrad")
    dgl, dgate2, t1, dglu_b = _glu_bwd(dy3, y, gl, gate2, glu_b)
    t2 = _matmul(dgl, w["glu"], name="glu_dgrad", tb=True)
    gb = wgrad(yg, dgl, "glu", "glu_wgrad")
    dy = _gelu_bwd(t1, t2, y)
    du, dbr, dbi, dcr, dci, dlr, dli, dd = _ssm_bwd(u, dy, hs_re, hs_im, mats, Bl, S)
    dh1 = _matmul(du, w["ssm_in_u"], name="ssm_in_dgrad_u", tb=True)
    dh1 = _matmul(dgate2, w["ssm_in_gate"], name="ssm_in_dgrad_gate", tb=True, residual=dh1)
    gb = wgrad(h1, du, "ssm_in_u", "ssm_in_wgrad_u")
    gb = wgrad(h1, dgate2, "ssm_in_gate", "ssm_in_wgrad_gate")
    dx1, dx1b, dng1 = _rmsnorm_bwd(x1, norm_g[1], dh1, dx2, "norm1_bwd")

    gb = wgrad(og, dx1b, "attn_out", "attn_out_wgrad")
    if core is None:
        dog = _matmul(dx1b, w["attn_out"], name="attn_out_dgrad", tb=True)
        do, dgate = _attn_gate_bwd(dog, o, gate)
        dqn, dkn, dv = _attn_bwd(qn, kn, v, do, btot, Bl, S)
    else:
        dog, from_sibling = _matmul(dx1b, w["attn_out"], name="attn_out_dgrad", tb=True, rider=_sibling_half_rider(gb))
        do, dgate = _attn_gate_bwd(dog, o, gate)
        chip_sum_b = _add_halves((gb, from_sibling), core, "grads_b_add_halves")
        dqn, dkn, dv, gb = _attn_bwd(qn, kn, v, do, btot, Bl, S, rider=_chip_exchange_rider(chip_sum_b))
    dq, dqg = _qknorm_bwd(q, q_g, dqn, "qnorm_bwd")
    dk, dkg = _qknorm_bwd(k, k_g, dkn, "knorm_bwd")
    dproj = [dq, dk, dv, dgate]
    for j in range(4):
        ga = _matmul(h0, dproj[j], name=f"attn_in_wgrad_{j}", ta=True, out=_attn_in_views(ga)[j])
    dh0 = None
    if core is None:
        for j in range(4):
            dh0 = _matmul(dproj[j], w_in[j], name=f"attn_in_dgrad_{j}", tb=True, residual=dh0)
    else:
        chip_sum_a = _add_halves((ga, _sibling_send_half(ga, "grads_a_sibling_half")), core, "grads_a_add_halves")
        rows = chip_sum_a.shape[1] // 4
        parts = []
        for j in range(4):
            dh0, part = _matmul(dproj[j], w_in[j], name=f"attn_in_dgrad_{j}", tb=True, residual=dh0,
                                rider=_chip_exchange_rider(chip_sum_a, j * rows, rows))
            parts.append(part)
        ga = jnp.concatenate(parts, axis=1)
    dx0, _, dng0 = _rmsnorm_bwd(x0, norm_g[0], dh0, dx1, "norm0_bwd")

    small = (jnp.concatenate([dng0, dng1], axis=0), dqg, dkg) + _ssm_unblock(dbr, dbi, dcr, dci, dlr, dli, dd, G) + (
        dglu_b.reshape(D),)
    return loss, dx0.reshape(Bl, S, D), ga, gb, small


def _chip_rows(a, chip, n_per):
    return lax.dynamic_slice_in_dim(a, chip * n_per, n_per, axis=0)


def kernel(x, norm_g, attn_w_in, attn_q_g, attn_k_g, attn_w_out, ssm_w_in, ssm_A_re, ssm_A_im, ssm_log_dt, ssm_B_re, ssm_B_im, ssm_C_re, ssm_C_im, ssm_D, ssm_glu_w, ssm_glu_b, ssm_w_out, loss_target, m_norm_g, m_attn_w_in, m_attn_q_g, m_attn_k_g, m_attn_w_out, m_ssm_w_in, m_ssm_A_re, m_ssm_A_im, m_ssm_log_dt, m_ssm_B_re, m_ssm_B_im, m_ssm_C_re, m_ssm_C_im, m_ssm_D, m_ssm_glu_w, m_ssm_glu_b, m_ssm_w_out, v_norm_g, v_attn_w_in, v_attn_q_g, v_attn_k_g, v_attn_w_out, v_ssm_w_in, v_ssm_A_re, v_ssm_A_im, v_ssm_log_dt, v_ssm_B_re, v_ssm_B_im, v_ssm_C_re, v_ssm_C_im, v_ssm_D, v_ssm_glu_w, v_ssm_glu_b, v_ssm_w_out):
    D = x.shape[-1]
    cx, cy, cc = _position()
    chip = 2 * cx + cy
    G = D // GROUP
    Gl = G // N_CHIPS

    def my_half(a):
        return lax.dynamic_slice_in_dim(a, cc * (a.shape[0] // 2), a.shape[0] // 2, axis=0)

    wa = _all_gather8(my_half(attn_w_in[0].astype(BF16)), "attn_in_all_gather").reshape(N_CHIPS, D, D)
    wb_half = my_half(_pack_rest(attn_w_out[0], ssm_w_in[0], ssm_glu_w[0], ssm_w_out[0]).astype(BF16))

    ssm_local = [ssm_A_re[0], ssm_A_im[0], ssm_log_dt[0], ssm_B_re[0], ssm_B_im[0], ssm_C_re[0], ssm_C_im[0], ssm_D[0],
                 ssm_glu_b[0]]
    small_local = _pack_small(ssm_local)
    half_rows = small_local.shape[0] // 2
    small_half = lax.dynamic_slice_in_dim(small_local, cc * half_rows, half_rows, axis=0)
    small_all = _all_gather8(small_half, "ssm_params_all_gather").reshape(N_CHIPS, 2 * half_rows, LANES)
    per_chip = [_unpack_small(small_all[j], [p.shape for p in ssm_local]) for j in range(N_CHIPS)]
    ssm_full = [jnp.concatenate([per_chip[j][i] for j in range(N_CHIPS)], axis=0) for i in range(len(ssm_local))]

    loss, grad_x, gathered_a, gathered_b, small = _local_step(x, loss_target, norm_g, attn_q_g[0], attn_k_g[0], wa,
                                                               wb_half, ssm_full, core=cc)
    loss = lax.psum(loss[0, 0], ("x", "y", "c"))

    q4 = D // N_CHIPS
    south = cc == 0

    def both_halves(gathered, tag):
        mine = _sum_leading(gathered, f"grads_{tag}_sum_chips")
        other = _sibling_swap(mine, f"grads_{tag}_sibling_swap")
        return jnp.concatenate([jnp.where(south, mine, other), jnp.where(south, other, mine)], axis=0)

    gd = (both_halves(gathered_a, "a"),) + _unpack_rest(both_halves(gathered_b, "b"))

    small_shapes = [s.shape for s in small]
    small_sum = _sum_leading(_all_gather8(_pack_small(small), "small_grads_all_gather"), "small_grads_sum")
    (dng, dqg, dkg, dl_re, dl_im, dbb_re, dbb_im, dc_re, dc_im, dd_skip, dglu_b) = _unpack_small(small_sum, small_shapes)
    a_re, a_im, log_dt, b_re, b_im = ssm_local[:5]
    _, zoh_vjp = jax.vjp(_zoh, a_re, a_im, log_dt, b_re, b_im)
    da_re, da_im, dlog_dt, db_re, db_im = zoh_vjp((_chip_rows(dl_re, chip, Gl), _chip_rows(dl_im, chip, Gl),
                                                   _chip_rows(dbb_re, chip, Gl), _chip_rows(dbb_im, chip, Gl)))
    grads = {
        "norm_g": dng, "attn_w_in": gd[0][None], "attn_q_g": dqg, "attn_k_g": dkg, "attn_w_out": gd[1][None],
        "ssm_w_in": gd[2][None], "ssm_A_re": da_re[None], "ssm_A_im": da_im[None], "ssm_log_dt": dlog_dt[None],
        "ssm_B_re": db_re[None], "ssm_B_im": db_im[None], "ssm_C_re": _chip_rows(dc_re, chip, Gl)[None],
        "ssm_C_im": _chip_rows(dc_im, chip, Gl)[None], "ssm_D": _chip_rows(dd_skip, chip, q4)[None],
        "ssm_glu_w": gd[3][None], "ssm_glu_b": _chip_rows(dglu_b, chip, q4)[None], "ssm_w_out": gd[4][None],
    }
    weights = dict(norm_g=norm_g, attn_w_in=attn_w_in, attn_q_g=attn_q_g, attn_k_g=attn_k_g, attn_w_out=attn_w_out,
                   ssm_w_in=ssm_w_in, ssm_A_re=ssm_A_re, ssm_A_im=ssm_A_im, ssm_log_dt=ssm_log_dt, ssm_B_re=ssm_B_re,
                   ssm_B_im=ssm_B_im, ssm_C_re=ssm_C_re, ssm_C_im=ssm_C_im, ssm_D=ssm_D, ssm_glu_w=ssm_glu_w,
                   ssm_glu_b=ssm_glu_b, ssm_w_out=ssm_w_out)
    m = dict(norm_g=m_norm_g, attn_w_in=m_attn_w_in, attn_q_g=m_attn_q_g, attn_k_g=m_attn_k_g, attn_w_out=m_attn_w_out,
             ssm_w_in=m_ssm_w_in, ssm_A_re=m_ssm_A_re, ssm_A_im=m_ssm_A_im, ssm_log_dt=m_ssm_log_dt, ssm_B_re=m_ssm_B_re,
             ssm_B_im=m_ssm_B_im, ssm_C_re=m_ssm_C_re, ssm_C_im=m_ssm_C_im, ssm_D=m_ssm_D, ssm_glu_w=m_ssm_glu_w,
             ssm_glu_b=m_ssm_glu_b, ssm_w_out=m_ssm_w_out)
    v = dict(norm_g=v_norm_g, attn_w_in=v_attn_w_in, attn_q_g=v_attn_q_g, attn_k_g=v_attn_k_g, attn_w_out=v_attn_w_out,
             ssm_w_in=v_ssm_w_in, ssm_A_re=v_ssm_A_re, ssm_A_im=v_ssm_A_im, ssm_log_dt=v_ssm_log_dt, ssm_B_re=v_ssm_B_re,
             ssm_B_im=v_ssm_B_im, ssm_C_re=v_ssm_C_re, ssm_C_im=v_ssm_C_im, ssm_D=v_ssm_D, ssm_glu_w=v_ssm_glu_w,
             ssm_glu_b=v_ssm_glu_b, ssm_w_out=v_ssm_w_out)
    names = list(weights)
    dense_names = ("attn_w_in", "attn_w_out", "ssm_w_in", "ssm_glu_w", "ssm_w_out")
    delta, new_m, new_v = {}, {}, {}
    for n in dense_names:
        delta[n], new_m[n], new_v[n] = _adamw(weights[n], grads[n], m[n], v[n], "adamw_" + n)
    small_names = [n for n in names if n not in dense_names]
    small_shapes = [weights[n].shape for n in small_names]
    packs = [_pack_small([d[n] for n in small_names]) for d in (weights, grads, m, v)]
    outs = _adamw(*packs, "adamw_small")
    for res, out in zip((delta, new_m, new_v), outs):
        for n, val in zip(small_names, _unpack_small(out, small_shapes)):
            res[n] = val
    return (loss, grad_x, *[grads[n] for n in names], *[delta[n] for n in names], *[new_m[n] for n in names],
            *[new_v[n] for n in names])
```

```python
import functools
import math

import jax
import jax.numpy as jnp
from jax import lax
from jax.experimental import pallas as pl
from jax.experimental.pallas import tpu as pltpu

F32 = jnp.float32
BF16 = jnp.bfloat16

HEAD_DIM = 128
GROUP = 16
STATE = 64
RMS_EPS = 1e-6
ADAM_LR = 0.001
ADAM_B1 = 0.9
ADAM_B2 = 0.999
ADAM_EPS = 1e-08
ADAM_WD = 0.01
ADAM_STEP = 10

N_CHIPS = 4
N_DEV = 8
SUBLANES = 8
LANES = 128
VMEM_LIMIT = 56 * 1024 * 1024
ROW_BLOCK_ELEMS = 1 << 19
MATMUL_TILE = 1024
MATMUL_PANEL_BYTES = 8 * 1024 * 1024
MESH = pl.DeviceIdType.MESH

NN = (((1,), (0,)), ((), ()))
NT = (((1,), (1,)), ((), ()))
TN = (((0,), (0,)), ((), ()))


def _params(*sem):
    return pltpu.CompilerParams(dimension_semantics=sem, vmem_limit_bytes=VMEM_LIMIT)


def _dot(a, b, dims=NN):
    return lax.dot_general(a, b, dims, preferred_element_type=F32)


def _split(a):
    hi = a.astype(BF16)
    lo = (a - hi.astype(F32)).astype(BF16)
    return hi, lo


def _dot_f32(a, b, dims=NN):
    return _dot(a.astype(BF16), b.astype(BF16), dims)


def _sigmoid(x):
    return 1.0 / (1.0 + jnp.exp(-x))


def _silu_parts(x):
    s = _sigmoid(x)
    return x * s, s * (1.0 + x * (1.0 - s))


_GELU_C = math.sqrt(2.0 / math.pi)


def _gelu_parts(x):
    x2 = x * x
    t = jnp.tanh(_GELU_C * (x + 0.044715 * x * x2))
    val = 0.5 * x * (1.0 + t)
    der = 0.5 * (1.0 + t) + 0.5 * x * (1.0 - t * t) * _GELU_C * (1.0 + 3.0 * 0.044715 * x2)
    return val, der


class _View:
    def __init__(self, buf, shape, locate, row_tile, col_tile):
        self.buf, self.shape, self.locate, self.row_tile, self.col_tile = buf, shape, locate, row_tile, col_tile

    def spec(self, t0, t1, block_of):
        def index(i, j, k):
            rb, cb = block_of(i, j, k)
            slab, r, c = self.locate(rb * t0, cb * t1)
            return slab, r // t0, c // t1
        return pl.BlockSpec((None, t0, t1), index)


def _operand(x):
    return (x.buf, x.shape, x.row_tile, x.col_tile) if isinstance(x, _View) else (x, x.shape, x.shape[0], x.shape[1])


def _matmul(a, b, *, name, ta=False, tb=False, residual=None, out_dtype=F32, out=None, rider=None):
    a_arr, a_shape, a_rt, a_ct = _operand(a)
    b_arr, b_shape, b_rt, b_ct = _operand(b)
    (K, M) = a_shape if ta else a_shape[::-1]
    N = b_shape[0] if tb else b_shape[1]
    a_mt, a_kt = (a_ct, a_rt) if ta else (a_rt, a_ct)
    b_nt, b_kt = (b_rt, b_ct) if tb else (b_ct, b_rt)
    k_cap = MATMUL_PANEL_BYTES // (MATMUL_TILE * max(a_arr.dtype.itemsize, b_arr.dtype.itemsize))
    tm, tn, tk = min(M, MATMUL_TILE, a_mt), min(N, MATMUL_TILE, b_nt), min(K, k_cap, a_kt)
    if out is not None:
        tm, tn = min(tm, out.row_tile), min(tn, out.col_tile)
    pk = min(tk, b_kt)
    pieces = tk // pk
    nk = K // tk
    dims = ((((0,) if ta else (1,)), ((1,) if tb else (0,))), ((), ()))
    n_in = 1 + pieces + (residual is not None) + (out is not None)

    def body(*refs):
        a_ref, b_refs = refs[0], refs[1:1 + pieces]
        r_ref = refs[1 + pieces] if residual is not None else None
        o_ref = refs[n_in]

        def finish(r):
            if residual is not None:
                r = r + r_ref[...].astype(F32)
            o_ref[...] = r.astype(out_dtype)

        part = None
        for p, b_ref in enumerate(b_refs):
            ks = slice(p * pk, (p + 1) * pk)
            a_blk = a_ref[...] if pieces == 1 else (a_ref[ks, :] if ta else a_ref[:, ks])
            term = _dot(a_blk.astype(BF16), b_ref[...].astype(BF16), dims)
            part = term if part is None else part + term
        if nk == 1:
            finish(part)
            return
        acc = refs[n_in + 1]
        k = pl.program_id(2)

        @pl.when(k == 0)
        def _():
            acc[...] = part

        @pl.when(k > 0)
        def _():
            acc[...] += part

        @pl.when(k == nk - 1)
        def _():
            finish(acc[...])

    def spec(x, t0, t1, block_of):
        if isinstance(x, _View):
            return x.spec(t0, t1, block_of)
        return pl.BlockSpec((t0, t1), block_of)

    a_spec = spec(a, tk, tm, lambda i, j, k: (k, i)) if ta else spec(a, tm, tk, lambda i, j, k: (i, k))
    b_specs = [spec(b, tn, pk, lambda i, j, k, p=p: (j, k * pieces + p)) if tb else
               spec(b, pk, tn, lambda i, j, k, p=p: (k * pieces + p, j)) for p in range(pieces)]
    in_specs = [a_spec] + b_specs
    args = [a_arr] + [b_arr] * pieces
    if residual is not None:
        in_specs.append(pl.BlockSpec((tm, tn), lambda i, j, k: (i, j)))
        args.append(residual)
    aliases = {}
    if out is None:
        out_spec = pl.BlockSpec((tm, tn), lambda i, j, k: (i, j))
        out_shape = jax.ShapeDtypeStruct((M, N), out_dtype)
    else:
        out_spec = out.spec(tm, tn, lambda i, j, k: (i, j))
        out_shape = jax.ShapeDtypeStruct(out.buf.shape, out.buf.dtype)
        out_dtype = out.buf.dtype
        in_specs.append(pl.BlockSpec(memory_space=pl.ANY))
        args.append(out.buf)
        aliases = {len(args) - 1: 0}
    grid = (M // tm, N // tn, nk)
    scratch = [pltpu.VMEM((tm, tn), F32)] if nk > 1 else []
    if rider is None:
        return pl.pallas_call(
            body, name=name, grid=grid, in_specs=in_specs, out_specs=out_spec, out_shape=out_shape,
            scratch_shapes=scratch, input_output_aliases=aliases,
            compiler_params=_params("parallel", "parallel", "arbitrary"),
        )(*args)
    body, in_specs, out_specs, out_shape, scratch, extra = _ride(rider, body, grid, in_specs, [out_spec], [out_shape],
                                                                 scratch)
    return pl.pallas_call(
        body, name=name, grid=grid, in_specs=in_specs, out_specs=out_specs, out_shape=out_shape,
        scratch_shapes=scratch, input_output_aliases=aliases,
        compiler_params=_params("arbitrary", "arbitrary", "arbitrary"),
    )(*args, *extra)


def _row_tile(T, C):
    tr = max(SUBLANES, min(T, ROW_BLOCK_ELEMS // C) // SUBLANES * SUBLANES)
    while T % tr:
        tr -= SUBLANES
    return tr


def _rows_call(body, name, T, C, row_ins, full_ins, row_outs, acc_outs=()):
    tr = _row_tile(T, C)
    row_spec = pl.BlockSpec((tr, C), lambda i: (i, 0))
    in_specs = [row_spec] * len(row_ins) + [pl.BlockSpec(f.shape, lambda i, n=f.ndim: (0,) * n) for f in full_ins]
    out_specs = [row_spec] * len(row_outs) + [pl.BlockSpec(s, lambda i, n=len(s): (0,) * n) for s in acc_outs]
    out_shape = [jax.ShapeDtypeStruct((T, C), d) for d in row_outs] + [jax.ShapeDtypeStruct(s, F32) for s in acc_outs]
    return pl.pallas_call(
        body, name=name, grid=(T // tr,), in_specs=in_specs, out_specs=out_specs, out_shape=out_shape,
        compiler_params=_params("arbitrary" if acc_outs else "parallel"),
    )(*row_ins, *full_ins)


def _rmsnorm_fwd(x, g, name):
    T, C = x.shape

    def body(x_ref, g_ref, h_ref):
        xv = x_ref[...]
        r = lax.rsqrt(jnp.mean(xv * xv, axis=-1, keepdims=True) + RMS_EPS)
        h_ref[...] = ((xv * r) * g_ref[...]).astype(BF16)

    return _rows_call(body, name, T, C, [x], [g.reshape(1, C)], [BF16])[0]


def _rmsnorm_bwd(x, g, dh, dres, name):
    T, C = x.shape

    def body(x_ref, dh_ref, dres_ref, g_ref, dx_ref, dxb_ref, dg_ref):
        @pl.when(pl.program_id(0) == 0)
        def _():
            dg_ref[...] = jnp.zeros_like(dg_ref)

        xv = x_ref[...]
        dhv = dh_ref[...]
        r = lax.rsqrt(jnp.mean(xv * xv, axis=-1, keepdims=True) + RMS_EPS)
        xn = xv * r
        dg_ref[...] += jnp.sum(dhv * xn, axis=0, keepdims=True)
        dxn = dhv * g_ref[...]
        dx = dres_ref[...] + r * (dxn - xn * jnp.mean(dxn * xn, axis=-1, keepdims=True))
        dx_ref[...] = dx
        dxb_ref[...] = dx.astype(BF16)

    return _rows_call(body, name, T, C, [x, dh, dres], [g.reshape(1, C)], [F32, BF16], [(1, C)])


def _heads(C):
    return [slice(h * HEAD_DIM, (h + 1) * HEAD_DIM) for h in range(C // HEAD_DIM)]


def _qknorm_fwd(q, g, name):
    T, C = q.shape

    def body(q_ref, g_ref, o_ref):
        for head in _heads(C):
            xv = q_ref[:, head]
            r = lax.rsqrt(jnp.mean(xv * xv, axis=-1, keepdims=True) + RMS_EPS)
            o_ref[:, head] = ((xv * r) * g_ref[...]).astype(BF16)

    return _rows_call(body, name, T, C, [q], [g.reshape(1, HEAD_DIM)], [BF16])[0]


def _qknorm_bwd(q, g, dqn, name):
    T, C = q.shape

    def body(q_ref, d_ref, g_ref, dq_ref, dg_ref):
        @pl.when(pl.program_id(0) == 0)
        def _():
            dg_ref[...] = jnp.zeros_like(dg_ref)

        dg = jnp.zeros((1, HEAD_DIM), F32)
        for head in _heads(C):
            xv = q_ref[:, head]
            dv = d_ref[:, head]
            r = lax.rsqrt(jnp.mean(xv * xv, axis=-1, keepdims=True) + RMS_EPS)
            xn = xv * r
            dg = dg + jnp.sum(dv * xn, axis=0, keepdims=True)
            dxn = dv * g_ref[...]
            dq_ref[:, head] = (r * (dxn - xn * jnp.mean(dxn * xn, axis=-1, keepdims=True))).astype(BF16)
        dg_ref[...] += dg

    return _rows_call(body, name, T, C, [q, dqn], [g.reshape(1, HEAD_DIM)], [BF16], [(1, HEAD_DIM)])


ATT_TQ = 512
ATT_TK = 256


def _logsig_pair(z):
    a = jnp.minimum(z, 0.0) - jnp.log(1.0 + jnp.exp(-jnp.abs(z)))
    return a, a - z


def _tri(n, strict_upper_src):
    j = lax.broadcasted_iota(jnp.int32, (n, n), 0)
    s = lax.broadcasted_iota(jnp.int32, (n, n), 1)
    if strict_upper_src == "gt":
        m = j > s
    elif strict_upper_src == "le":
        m = j <= s
    else:
        m = j < s
    return jnp.where(m, 1.0, 0.0).astype(BF16)


def _cumdot(x, tri):
    hi, lo = _split(x)
    return _dot(hi, tri) + _dot(lo, tri)


def _attn_tiles(S):
    tq, tk = min(ATT_TQ, S), min(ATT_TK, S)
    return tq, tk, S // tq, tq // tk


def _attn_fwd(qn, kn, v, gate, Bl, S, rider=None):
    T, C = qn.shape
    H = C // HEAD_DIM
    tq, tk, nq, kpq = _attn_tiles(S)
    scale = 1.0 / math.sqrt(HEAD_DIM)

    def body(q_ref, k_ref, v_ref, g_ref, o_ref, bt_ref, og_ref):
        i = pl.program_id(2)
        tri = _tri(tk, "gt")
        rowpos = lax.broadcasted_iota(jnp.int32, (tq, tk), 0) + i * tq
        colpos = lax.broadcasted_iota(jnp.int32, (tq, tk), 1)
        o_ref[...] = jnp.zeros_like(o_ref)
        bt_ref[...] = jnp.zeros_like(bt_ref)
        nkb = (i + 1) * kpq

        def step(n, carry):
            j = nkb - 1 - n
            rows = pl.ds(pl.multiple_of(j * tk, tk), tk)
            z = _dot(q_ref[...], k_ref[rows, :], NT) * scale
            mask = (colpos + j * tk) < rowpos
            a, b = _logsig_pair(z)
            b = jnp.where(mask, b, 0.0)
            suffix = _cumdot(b, tri)
            w = jnp.where(mask, jnp.exp(a + suffix + bt_ref[...]), 0.0)
            o_ref[...] += _dot(w.astype(BF16), v_ref[rows, :])
            bt_ref[...] += jnp.sum(b, axis=-1, keepdims=True)
            return carry

        lax.fori_loop(0, nkb, step, 0)
        og_ref[...] = (o_ref[...] * _silu_parts(g_ref[...])[0]).astype(BF16)

    qspec = pl.BlockSpec((tq, HEAD_DIM), lambda b, h, i: (b * nq + i, h))
    kspec = pl.BlockSpec((S, HEAD_DIM), lambda b, h, i: (b, h))
    btspec = pl.BlockSpec((None, None, tq, 1), lambda b, h, i: (b, h, i, 0))
    grid = (Bl, H, nq)
    body, in_specs, out_specs, out_shape, scratch, extra = _ride(
        rider, body, grid, [qspec, kspec, kspec, qspec], [qspec, btspec, qspec],
        [jax.ShapeDtypeStruct((T, C), F32), jax.ShapeDtypeStruct((Bl, H, S, 1), F32),
         jax.ShapeDtypeStruct((T, C), BF16)], [])
    return pl.pallas_call(
        body, name="attn_fwd", grid=grid, in_specs=in_specs, out_specs=out_specs, out_shape=out_shape,
        scratch_shapes=scratch, compiler_params=_params("arbitrary", "arbitrary", "arbitrary"),
    )(qn, kn, v, gate, *extra)


def _attn_bwd(qn, kn, v, do, btot, Bl, S, rider=None):
    T, C = qn.shape
    H = C // HEAD_DIM
    tq, tk, nq, kpq = _attn_tiles(S)
    scale = 1.0 / math.sqrt(HEAD_DIM)

    def body(q_ref, k_ref, v_ref, do_ref, bt_ref, dq_ref, dk_ref, dv_ref, pb_ref, pdl_ref):
        i = pl.program_id(2)

        @pl.when(i == 0)
        def _():
            dk_ref[...] = jnp.zeros_like(dk_ref)
            dv_ref[...] = jnp.zeros_like(dv_ref)

        tri_le = _tri(tk, "le")
        tri_lt = _tri(tk, "lt")
        rowpos = lax.broadcasted_iota(jnp.int32, (tq, tk), 0) + i * tq
        colpos = lax.broadcasted_iota(jnp.int32, (tq, tk), 1)
        dq_ref[...] = jnp.zeros_like(dq_ref)
        pb_ref[...] = bt_ref[...]
        pdl_ref[...] = jnp.zeros_like(pdl_ref)

        def step(j, carry):
            rows = pl.ds(pl.multiple_of(j * tk, tk), tk)
            kj = k_ref[rows, :]
            z = _dot(q_ref[...], kj, NT) * scale
            mask = (colpos + j * tk) < rowpos
            a, b = _logsig_pair(z)
            b = jnp.where(mask, b, 0.0)
            suffix = pb_ref[...] - _cumdot(b, tri_le)
            w = jnp.where(mask, jnp.exp(a + suffix), 0.0)
            dl = _dot(do_ref[...], v_ref[rows, :], NT) * w
            prefix = pdl_ref[...] + _cumdot(dl, tri_lt)
            beta = jnp.exp(a)
            dzb = (jnp.where(mask, dl * (1.0 - beta) - beta * prefix, 0.0) * scale).astype(BF16)
            dq_ref[...] += _dot(dzb, kj)
            dk_ref[rows, :] += _dot(dzb, q_ref[...], TN)
            dv_ref[rows, :] += _dot(w.astype(BF16), do_ref[...], TN)
            pb_ref[...] -= jnp.sum(b, axis=-1, keepdims=True)
            pdl_ref[...] += jnp.sum(dl, axis=-1, keepdims=True)
            return carry

        lax.fori_loop(0, (i + 1) * kpq, step, 0)

    qspec = pl.BlockSpec((tq, HEAD_DIM), lambda b, h, i: (b * nq + i, h))
    kspec = pl.BlockSpec((S, HEAD_DIM), lambda b, h, i: (b, h))
    btspec = pl.BlockSpec((None, None, tq, 1), lambda b, h, i: (b, h, i, 0))
    grid = (Bl, H, nq)
    body, in_specs, out_specs, out_shape, scratch, extra = _ride(
        rider, body, grid, [qspec, kspec, kspec, qspec, btspec], [qspec, kspec, kspec],
        [jax.ShapeDtypeStruct((T, C), F32)] * 3, [pltpu.VMEM((tq, 1), F32), pltpu.VMEM((tq, 1), F32)])
    return pl.pallas_call(
        body, name="attn_bwd", grid=grid, in_specs=in_specs, out_specs=out_specs, out_shape=out_shape,
        scratch_shapes=scratch, compiler_params=_params("arbitrary", "arbitrary", "arbitrary"),
    )(qn, kn, v, do, btot, *extra)


SSM_TIME_BLOCK = 512
CHUNK = SUBLANES


def _cmadd(xr, xi, ar, ai, sr, si):
    return xr + ar * sr - ai * si, xi + ar * si + ai * sr


def _chunk_scan(xr, xi, tab_ref, cr, ci, reverse):
    for lvl, d in enumerate((1, 2, 4)):
        shift = (CHUNK - d) if reverse else d
        sr = pltpu.roll(xr, shift, 0)
        si = pltpu.roll(xi, shift, 0)
        ar = tab_ref[pl.ds((2 * lvl) * CHUNK, CHUNK), :]
        ai = tab_ref[pl.ds((2 * lvl + 1) * CHUNK, CHUNK), :]
        xr, xi = _cmadd(xr, xi, ar, ai, sr, si)
    pr = tab_ref[pl.ds(6 * CHUNK, CHUNK), :]
    pi = tab_ref[pl.ds(7 * CHUNK, CHUNK), :]
    return _cmadd(xr, xi, pr, pi, cr, ci)


def _ssm_dims(S, C):
    G = C // GROUP
    GT = min(16, G)
    return G, GT, G // GT, GT * GROUP, GT * STATE, min(SSM_TIME_BLOCK, S)


def _ssm_fwd(u, mats, Bl, S):
    T, C = u.shape
    G, GT, ngt, cw, sw, TB = _ssm_dims(S, C)
    ntb = S // TB
    nch = TB // CHUNK

    def body(u_ref, bre_ref, bim_ref, cre_ref, cim_ref, d_ref, tab_ref, y_ref, hr_ref, hi_ref, yg_ref, car_r, car_i):
        @pl.when(pl.program_id(2) == 0)
        def _():
            car_r[...] = jnp.zeros_like(car_r)
            car_i[...] = jnp.zeros_like(car_i)

        uv = u_ref[...]
        hr_ref[...] = _dot_f32(uv, bre_ref[...])
        hi_ref[...] = _dot_f32(uv, bim_ref[...])

        def step(n, carry):
            cr, ci = carry
            rows = pl.ds(pl.multiple_of(n * CHUNK, CHUNK), CHUNK)
            xr, xi = _chunk_scan(hr_ref[rows, :], hi_ref[rows, :], tab_ref, cr, ci, False)
            hr_ref[rows, :] = xr
            hi_ref[rows, :] = xi
            last = (CHUNK - 1, CHUNK)
            return (jnp.broadcast_to(xr[last[0]:last[1], :], xr.shape), jnp.broadcast_to(xi[last[0]:last[1], :], xi.shape))

        cr, ci = lax.fori_loop(0, nch, step, (car_r[...], car_i[...]))
        car_r[...] = cr
        car_i[...] = ci
        y = _dot_f32(hr_ref[...], cre_ref[...]) - _dot_f32(hi_ref[...], cim_ref[...]) + d_ref[...] * uv
        y_ref[...] = y
        yg_ref[...] = _gelu_parts(y)[0].astype(BF16)

    uspec = pl.BlockSpec((TB, cw), lambda g, b, t: (b * ntb + t, g))
    hspec = pl.BlockSpec((TB, sw), lambda g, b, t: (b * ntb + t, g))

    def gspec(r, c):
        return pl.BlockSpec((None, r, c), lambda g, b, t: (g, 0, 0))

    return pl.pallas_call(
        body, name="ssm_fwd", grid=(ngt, Bl, ntb),
        in_specs=[uspec, gspec(cw, sw), gspec(cw, sw), gspec(sw, cw), gspec(sw, cw), gspec(1, cw), gspec(8 * CHUNK, sw)],
        out_specs=[uspec, hspec, hspec, uspec],
        out_shape=[jax.ShapeDtypeStruct((T, C), F32), jax.ShapeDtypeStruct((T, G * STATE), F32),
                   jax.ShapeDtypeStruct((T, G * STATE), F32), jax.ShapeDtypeStruct((T, C), BF16)],
        scratch_shapes=[pltpu.VMEM((CHUNK, sw), F32), pltpu.VMEM((CHUNK, sw), F32)],
        compiler_params=_params("parallel", "arbitrary", "arbitrary"),
    )(u, mats["bbd_re"], mats["bbd_im"], mats["cbd_re"], mats["cbd_im"], mats["d"], mats["tab_fwd"])


def _ssm_bwd(u, dy, h_re, h_im, mats, Bl, S):
    T, C = u.shape
    G, GT, ngt, cw, sw, TB = _ssm_dims(S, C)
    ntb = S // TB
    nch = TB // CHUNK
    rpb = TB // CHUNK

    def body(u_ref, dy_ref, hr_ref, hi_ref, hpr_ref, hpi_ref, ctr_ref, cti_ref, btr_ref, bti_ref, d_ref, tab_ref,
             du_ref, dbr_ref, dbi_ref, dcr_ref, dci_ref, dlr_ref, dli_ref, dd_ref, gr_ref, gi_ref, car_r, car_i):
        b = pl.program_id(1)
        t = pl.program_id(2)

        @pl.when((b == 0) & (t == 0))
        def _():
            for ref in (dbr_ref, dbi_ref, dcr_ref, dci_ref, dlr_ref, dli_ref, dd_ref):
                ref[...] = jnp.zeros_like(ref)

        @pl.when(t == 0)
        def _():
            car_r[...] = jnp.zeros_like(car_r)
            car_i[...] = jnp.zeros_like(car_i)

        uv = u_ref[...]
        dyv = dy_ref[...]
        gr_ref[...] = _dot_f32(dyv, ctr_ref[...])
        gi_ref[...] = -_dot_f32(dyv, cti_ref[...])
        alive = jnp.where(t == ntb - 1, 0.0, 1.0)
        row0 = lax.broadcasted_iota(jnp.int32, (CHUNK, sw), 0) == 0

        def step(m, carry):
            cr, ci, ar, ai = carry
            n = nch - 1 - m
            rows = pl.ds(pl.multiple_of(n * CHUNK, CHUNK), CHUNK)
            prow = pl.ds(pl.multiple_of(jnp.maximum(n - 1, 0) * CHUNK, CHUNK), CHUNK)
            xr, xi = _chunk_scan(gr_ref[rows, :], gi_ref[rows, :], tab_ref, cr, ci, True)
            gr_ref[rows, :] = xr
            gi_ref[rows, :] = xi
            first = n == 0
            pr = jnp.where(first, hpr_ref[...] * alive, hr_ref[prow, :])
            pi = jnp.where(first, hpi_ref[...] * alive, hi_ref[prow, :])
            sr = jnp.where(row0, pltpu.roll(pr, 1, 0), pltpu.roll(hr_ref[rows, :], 1, 0))
            si = jnp.where(row0, pltpu.roll(pi, 1, 0), pltpu.roll(hi_ref[rows, :], 1, 0))
            ar = ar + xr * sr + xi * si
            ai = ai + xi * sr - xr * si
            return (jnp.broadcast_to(xr[0:1, :], xr.shape), jnp.broadcast_to(xi[0:1, :], xi.shape), ar, ai)

        zero = jnp.zeros((CHUNK, sw), F32)
        cr, ci, ar, ai = lax.fori_loop(0, nch, step, (car_r[...], car_i[...], zero, zero))
        car_r[...] = cr
        car_i[...] = ci
        dlr_ref[...] += ar
        dli_ref[...] += ai
        gr = gr_ref[...]
        gi = gi_ref[...]
        dbr_ref[...] += _dot_f32(uv, gr, TN)
        dbi_ref[...] += _dot_f32(uv, gi, TN)
        dcr_ref[...] += _dot_f32(hr_ref[...], dyv, TN)
        dci_ref[...] -= _dot_f32(hi_ref[...], dyv, TN)
        dd_ref[...] += jnp.sum(dyv * uv, axis=0, keepdims=True)
        du_ref[...] = (_dot_f32(gr, btr_ref[...]) + _dot_f32(gi, bti_ref[...]) + d_ref[...] * dyv).astype(BF16)

    def tblk(b, t):
        return b * ntb + (ntb - 1 - t)

    uspec = pl.BlockSpec((TB, cw), lambda g, b, t: (tblk(b, t), g))
    hspec = pl.BlockSpec((TB, sw), lambda g, b, t: (tblk(b, t), g))
    hpspec = pl.BlockSpec((CHUNK, sw), lambda g, b, t: (jnp.maximum(tblk(b, t) * rpb - 1, 0), g))

    def gspec(r, c):
        return pl.BlockSpec((None, r, c), lambda g, b, t: (g, 0, 0))

    def gshape(r, c):
        return jax.ShapeDtypeStruct((ngt, r, c), F32)

    return pl.pallas_call(
        body, name="ssm_bwd", grid=(ngt, Bl, ntb),
        in_specs=[uspec, uspec, hspec, hspec, hpspec, hpspec, gspec(cw, sw), gspec(cw, sw), gspec(sw, cw), gspec(sw, cw),
                  gspec(1, cw), gspec(8 * CHUNK, sw)],
        out_specs=[uspec, gspec(cw, sw), gspec(cw, sw), gspec(sw, cw), gspec(sw, cw), gspec(CHUNK, sw), gspec(CHUNK, sw),
                   gspec(1, cw)],
        out_shape=[jax.ShapeDtypeStruct((T, C), BF16), gshape(cw, sw), gshape(cw, sw), gshape(sw, cw), gshape(sw, cw),
                   gshape(CHUNK, sw), gshape(CHUNK, sw), gshape(1, cw)],
        scratch_shapes=[pltpu.VMEM((TB, sw), F32), pltpu.VMEM((TB, sw), F32), pltpu.VMEM((CHUNK, sw), F32),
                        pltpu.VMEM((CHUNK, sw), F32)],
        compiler_params=_params("arbitrary", "arbitrary", "arbitrary"),
    )(u, dy, h_re, h_im, h_re, h_im, mats["cbdT_re"], mats["cbdT_im"], mats["bbdT_re"], mats["bbdT_im"], mats["d"],
      mats["tab_rev"])


def _zoh(a_re, a_im, log_dt, b_re, b_im):
    dt = jnp.exp(log_dt)[:, None]
    mag = jnp.exp(a_re * dt)
    l_re = mag * jnp.cos(a_im * dt)
    l_im = mag * jnp.sin(a_im * dt)
    den = a_re * a_re + a_im * a_im
    f_re = ((l_re - 1.0) * a_re + l_im * a_im) / den
    f_im = (l_im * a_re - (l_re - 1.0) * a_im) / den
    bb_re = f_re[..., None] * b_re - f_im[..., None] * b_im
    bb_im = f_re[..., None] * b_im + f_im[..., None] * b_re
    return l_re, l_im, bb_re, bb_im


def _ssm_matrices(a_re, a_im, log_dt, b_re, b_im, c_re, c_im, d, S):
    G = a_re.shape[0]
    _, GT, ngt, cw, sw, _ = _ssm_dims(S, G * GROUP)
    _, _, bb_re, bb_im = _zoh(a_re, a_im, log_dt, b_re, b_im)
    eye = jnp.eye(GT, dtype=F32)

    def bd_b(bb):
        return jnp.einsum("tgpi,gh->tgihp", bb.reshape(ngt, GT, STATE, GROUP), eye).reshape(ngt, cw, sw)

    def bd_c(c):
        return jnp.einsum("tgip,gh->tgphi", c.reshape(ngt, GT, GROUP, STATE), eye).reshape(ngt, sw, cw)

    dt = jnp.exp(log_dt)[:, None]

    def power(k, conj):
        mag = jnp.exp(k * a_re * dt)
        ang = k * a_im * dt
        return (mag * jnp.cos(ang)).reshape(ngt, 1, sw), ((-1.0 if conj else 1.0) * mag * jnp.sin(ang)).reshape(ngt, 1, sw)

    r = jnp.arange(CHUNK)[None, :, None]

    def table(reverse):
        parts = []
        for dd in (1, 2, 4):
            pr, pi = power(float(dd), reverse)
            keep = (r <= CHUNK - 1 - dd) if reverse else (r >= dd)
            parts += [jnp.where(keep, pr, 0.0), jnp.where(keep, pi, 0.0)]
        exps = [(CHUNK - k) if reverse else (k + 1) for k in range(CHUNK)]
        pw = [power(float(e), reverse) for e in exps]
        parts += [jnp.concatenate([p[0] for p in pw], axis=1), jnp.concatenate([p[1] for p in pw], axis=1)]
        return jnp.concatenate([jnp.broadcast_to(p, (ngt, CHUNK, sw)) for p in parts], axis=1)

    mats = dict(bbd_re=bd_b(bb_re).astype(BF16), bbd_im=bd_b(bb_im).astype(BF16), cbd_re=bd_c(c_re).astype(BF16),
                cbd_im=bd_c(c_im).astype(BF16), d=d.reshape(ngt, 1, cw), tab_fwd=table(False), tab_rev=table(True))
    for k in ("bbd_re", "bbd_im", "cbd_re", "cbd_im"):
        mats[k.replace("bd_", "bdT_")] = jnp.swapaxes(mats[k], 1, 2)
    return mats


def _ssm_unblock(dbr, dbi, dcr, dci, dlr, dli, dd, G):
    ngt = dbr.shape[0]
    GT = G // ngt
    eye = jnp.eye(GT, dtype=F32)

    def ub(x):
        return jnp.einsum("tgihp,gh->tgpi", x.reshape(ngt, GT, GROUP, GT, STATE), eye).reshape(G, STATE, GROUP)

    def uc(x):
        return jnp.einsum("tgphi,gh->tgip", x.reshape(ngt, GT, STATE, GT, GROUP), eye).reshape(G, GROUP, STATE)

    return (dlr.sum(axis=1).reshape(G, STATE), dli.sum(axis=1).reshape(G, STATE), ub(dbr), ub(dbi), uc(dcr), uc(dci),
            dd.reshape(G * GROUP))


def _attn_gate_bwd(dog, o, gate):
    T, C = o.shape

    def body(d_ref, o_ref, g_ref, do_ref, dg_ref):
        val, der = _silu_parts(g_ref[...])
        dv = d_ref[...]
        do_ref[...] = (dv * val).astype(BF16)
        dg_ref[...] = (dv * o_ref[...] * der).astype(BF16)

    return _rows_call(body, "attn_gate_bwd", T, C, [dog, o, gate], [], [BF16, BF16])


def _glu_fwd(y, gl, gate, glu_b):
    T, C = y.shape

    def body(y_ref, gl_ref, g_ref, b_ref, out_ref):
        yg = _gelu_parts(y_ref[...])[0]
        sg = _sigmoid(gl_ref[...] + b_ref[...])
        out_ref[...] = (yg * sg * _silu_parts(g_ref[...])[0]).astype(BF16)

    return _rows_call(body, "glu_fwd", T, C, [y, gl, gate], [glu_b.reshape(1, C)], [BF16])[0]


def _glu_bwd(dy3, y, gl, gate, glu_b):
    T, C = y.shape

    def body(d_ref, y_ref, gl_ref, g_ref, b_ref, dgl_ref, dgate_ref, t1_ref, db_ref):
        @pl.when(pl.program_id(0) == 0)
        def _():
            db_ref[...] = jnp.zeros_like(db_ref)

        yg = _gelu_parts(y_ref[...])[0]
        sg = _sigmoid(gl_ref[...] + b_ref[...])
        sl, sld = _silu_parts(g_ref[...])
        dv = d_ref[...]
        dy2 = dv * sl
        dgl = dy2 * yg * sg * (1.0 - sg)
        dgl_ref[...] = dgl.astype(BF16)
        dgate_ref[...] = (dv * (yg * sg) * sld).astype(BF16)
        t1_ref[...] = dy2 * sg
        db_ref[...] += jnp.sum(dgl, axis=0, keepdims=True)

    return _rows_call(body, "glu_bwd", T, C, [dy3, y, gl, gate], [glu_b.reshape(1, C)], [BF16, BF16, F32], [(1, C)])


def _gelu_bwd(t1, t2, y):
    T, C = y.shape

    def body(a_ref, b_ref, y_ref, out_ref):
        out_ref[...] = (a_ref[...] + b_ref[...]) * _gelu_parts(y_ref[...])[1]

    return _rows_call(body, "gelu_bwd", T, C, [t1, t2, y], [], [F32])[0]


def _loss_head(x2, target):
    T, C = x2.shape

    def body(x_ref, t_ref, d_ref, db_ref, l_ref):
        @pl.when(pl.program_id(0) == 0)
        def _():
            l_ref[...] = jnp.zeros_like(l_ref)

        e = x_ref[...] - t_ref[...]
        d = e * (1.0 / C)
        d_ref[...] = d
        db_ref[...] = d.astype(BF16)
        l_ref[...] += 0.5 * jnp.sum(jnp.sum(e * e, axis=-1, keepdims=True) * (1.0 / C), axis=0, keepdims=True)

    return _rows_call(body, "loss_head", T, C, [x2, target], [], [F32, BF16], [(1, 1)])


def _adamw(w, g, m, v, name):
    shape = w.shape
    C = shape[-1]
    R = w.size // C
    bc1 = 1.0 - ADAM_B1 ** ADAM_STEP
    bc2 = 1.0 - ADAM_B2 ** ADAM_STEP

    def body(w_ref, g_ref, m_ref, v_ref, d_ref, nm_ref, nv_ref):
        gv = g_ref[...]
        mn = ADAM_B1 * m_ref[...] + (1.0 - ADAM_B1) * gv
        vn = ADAM_B2 * v_ref[...] + (1.0 - ADAM_B2) * (gv * gv)
        d_ref[...] = -ADAM_LR * ((mn / bc1) / (jnp.sqrt(vn / bc2) + ADAM_EPS) + ADAM_WD * w_ref[...])
        nm_ref[...] = mn
        nv_ref[...] = vn

    outs = _rows_call(body, name, R, C, [a.reshape(R, C) for a in (w, g, m, v)], [], [F32, F32, F32])
    return [o.reshape(shape) for o in outs]


def _sum_leading(x, name):
    n, R, C = x.shape
    tr = _row_tile(R, C * n)

    def body(x_ref, o_ref):
        acc = x_ref[0].astype(F32)
        for k in range(1, n):
            acc = acc + x_ref[k].astype(F32)
        o_ref[...] = acc

    return pl.pallas_call(
        body, name=name, grid=(R // tr,), in_specs=[pl.BlockSpec((n, tr, C), lambda i: (0, i, 0))],
        out_specs=pl.BlockSpec((tr, C), lambda i: (i, 0)), out_shape=jax.ShapeDtypeStruct((R, C), F32),
        compiler_params=_params("parallel"),
    )(x)


def _add_halves(g, c, name):
    full, recv = g
    n, R, C = full.shape
    half = R // 2
    tr = _row_tile(half, C)
    nb = half // tr

    def body(c_ref, a_ref, b_ref, o_ref):
        o_ref[...] = (a_ref[...] + b_ref[...]).astype(BF16)

    grid_spec = pltpu.PrefetchScalarGridSpec(
        num_scalar_prefetch=1, grid=(n, nb),
        in_specs=[pl.BlockSpec((None, tr, C), lambda j, i, c_ref: (j, c_ref[0] * nb + i, 0)),
                  pl.BlockSpec((None, tr, C), lambda j, i, c_ref: (j, i, 0))],
        out_specs=pl.BlockSpec((None, tr, C), lambda j, i, c_ref: (j, i, 0)))
    return pl.pallas_call(
        body, name=name, grid_spec=grid_spec, out_shape=jax.ShapeDtypeStruct((n, half, C), BF16),
        compiler_params=_params("parallel", "parallel"),
    )(c.reshape(1).astype(jnp.int32), full, recv)


ANY = pl.BlockSpec(memory_space=pl.ANY)


def _position():
    return lax.axis_index("x"), lax.axis_index("y"), lax.axis_index("c")


def _all_gather8(blk, name):
    return _standalone(_all_gather_rider(blk), name)[0]


class _Rider:
    def __init__(self, arrays, out_shapes, sems, start, finish):
        self.arrays, self.out_shapes, self.sems, self.start, self.finish = arrays, out_shapes, sems, start, finish


def _join_riders(*riders):
    def split(refs):
        ins, outs, sems = [], [], []
        for r in riders:
            ins.append(refs[:len(r.arrays)])
            refs = refs[len(r.arrays):]
        for r in riders:
            outs.append(refs[:len(r.out_shapes)])
            refs = refs[len(r.out_shapes):]
        for r in riders:
            sems.append(refs[:len(r.sems)])
            refs = refs[len(r.sems):]
        return [i + o + s for i, o, s in zip(ins, outs, sems)]

    def start(*refs):
        for r, own in zip(riders, split(refs)):
            r.start(*own)

    def finish(*refs):
        for r, own in zip(riders, split(refs)):
            r.finish(*own)

    return _Rider([a for r in riders for a in r.arrays], [s for r in riders for s in r.out_shapes],
                  [s for r in riders for s in r.sems], start, finish)


def _standalone(rider, name):
    def body(*refs):
        rider.start(*refs)
        rider.finish(*refs)

    return pl.pallas_call(
        body, name=name, in_specs=[ANY] * len(rider.arrays), out_specs=[ANY] * len(rider.out_shapes),
        out_shape=rider.out_shapes, scratch_shapes=rider.sems,
    )(*rider.arrays)


def _ride(rider, body, grid, in_specs, out_specs, out_shape, scratch):
    if rider is None:
        return body, in_specs, out_specs, out_shape, scratch, []
    ni, no, ns = len(in_specs), len(out_specs), len(scratch)
    ri, ro = len(rider.arrays), len(rider.out_shapes)

    def full(*refs):
        ins, refs = refs[:ni], refs[ni:]
        r_ins, refs = refs[:ri], refs[ri:]
        outs, refs = refs[:no], refs[no:]
        r_outs, refs = refs[:ro], refs[ro:]
        scr, r_sems = refs[:ns], refs[ns:]
        ids = [pl.program_id(a) for a in range(len(grid))]
        first = functools.reduce(jnp.logical_and, [i == 0 for i in ids])
        last = functools.reduce(jnp.logical_and, [i == g - 1 for i, g in zip(ids, grid)])

        @pl.when(first)
        def _():
            rider.start(*r_ins, *r_outs, *r_sems)

        body(*ins, *outs, *scr)

        @pl.when(last)
        def _():
            rider.finish(*r_ins, *r_outs, *r_sems)

    return (full, in_specs + [ANY] * ri, out_specs + [ANY] * ro, out_shape + rider.out_shapes, scratch + rider.sems,
            rider.arrays)


def _all_gather_rider(blk):
    M, N = blk.shape

    def copies(x_ref, out_ref, send_sems, recv_sems, local_sem):
        x, y, c = _position()
        me, sibling = (x, y, c), (x, y, 1 - c)
        chips = [(1 - x, y), (x, 1 - y), (1 - x, 1 - y)]

        def slab(px, py, pc):
            return out_ref.at[4 * px + 2 * py + pc]

        def copy(k, block, to, src=None):
            return pltpu.make_async_remote_copy(
                src_ref=slab(*block) if src is None else src, dst_ref=slab(*block),
                send_sem=send_sems.at[k], recv_sem=recv_sems.at[k], device_id=to, device_id_type=MESH)

        mine = pltpu.make_async_copy(x_ref, slab(*me), local_sem)
        first = [copy(0, me, sibling, src=x_ref)]
        first += [copy(1 + j, me, (*chip, c), src=x_ref) for j, chip in enumerate(chips)]
        passed = [copy(4 + j, (*chip, c), sibling) for j, chip in enumerate(chips)]
        arrivals = [copy(1 + j, (*chip, c), me) for j, chip in enumerate(chips)]
        from_sibling = [copy(0, sibling, me)] + [copy(4 + j, (*chip, 1 - c), me) for j, chip in enumerate(chips)]
        return mine, first, passed, arrivals, from_sibling

    def start(*refs):
        mine, first, _, _, _ = copies(*refs)
        mine.start()
        for cp in first:
            cp.start()

    def finish(*refs):
        mine, first, passed, arrivals, from_sibling = copies(*refs)
        for arrival, onward in zip(arrivals, passed):
            arrival.wait_recv()
            onward.start()
        for cp in from_sibling:
            cp.wait_recv()
        for cp in first + passed:
            cp.wait_send()
        mine.wait()

    return _Rider([blk], [jax.ShapeDtypeStruct((N_DEV, M, N), blk.dtype)],
                  [pltpu.SemaphoreType.DMA((7,)), pltpu.SemaphoreType.DMA((7,)), pltpu.SemaphoreType.DMA], start, finish)


def _sibling_half_rider(g):
    n, R, C = g.shape
    half = R // 2

    def copy(g_ref, out_ref, send_sem, recv_sem):
        x, y, c = _position()
        return pltpu.make_async_remote_copy(
            src_ref=g_ref.at[:, pl.ds((1 - c) * half, half), :], dst_ref=out_ref, send_sem=send_sem, recv_sem=recv_sem,
            device_id=(x, y, 1 - c), device_id_type=MESH)

    return _Rider([g], [jax.ShapeDtypeStruct((n, half, C), g.dtype)], [pltpu.SemaphoreType.DMA, pltpu.SemaphoreType.DMA],
                  lambda *refs: copy(*refs).start(), lambda *refs: copy(*refs).wait())


def _sibling_send_half(g, name):
    return _standalone(_sibling_half_rider(g), name)[0]


def _chip_exchange_rider(p, row_off=0, rows=None):
    rows = p.shape[1] if rows is None else rows

    def copies(p_ref, out_ref, send_sems, recv_sems, local_sem):
        x, y, c = _position()
        my = 2 * x + y
        chips = [(1 - x, y), (x, 1 - y), (1 - x, 1 - y)]

        def src(slab):
            return p_ref.at[slab, pl.ds(row_off, rows), :]

        mine = pltpu.make_async_copy(src(my), out_ref.at[my], local_sem)
        sends = [pltpu.make_async_remote_copy(
            src_ref=src(2 * px + py), dst_ref=out_ref.at[my], send_sem=send_sems.at[k], recv_sem=recv_sems.at[k],
            device_id=(px, py, c), device_id_type=MESH) for k, (px, py) in enumerate(chips)]
        arrivals = [pltpu.make_async_remote_copy(
            src_ref=src(my), dst_ref=out_ref.at[2 * px + py], send_sem=send_sems.at[k], recv_sem=recv_sems.at[k],
            device_id=(px, py, c), device_id_type=MESH) for k, (px, py) in enumerate(chips)]
        return mine, sends, arrivals

    def start(*refs):
        mine, sends, _ = copies(*refs)
        mine.start()
        for cp in sends:
            cp.start()

    def finish(*refs):
        mine, sends, arrivals = copies(*refs)
        for cp in arrivals:
            cp.wait_recv()
        for cp in sends:
            cp.wait_send()
        mine.wait()

    return _Rider([p], [jax.ShapeDtypeStruct((p.shape[0], rows, p.shape[2]), p.dtype)],
                  [pltpu.SemaphoreType.DMA((3,)), pltpu.SemaphoreType.DMA((3,)), pltpu.SemaphoreType.DMA], start, finish)


def _sibling_swap(f, name):
    def body(f_ref, out_ref, send_sem, recv_sem):
        x, y, c = _position()
        cp = pltpu.make_async_remote_copy(src_ref=f_ref, dst_ref=out_ref, send_sem=send_sem, recv_sem=recv_sem,
                                          device_id=(x, y, 1 - c), device_id_type=MESH)
        cp.start()
        cp.wait()

    return pl.pallas_call(
        body, name=name, in_specs=[ANY], out_specs=ANY, out_shape=jax.ShapeDtypeStruct(f.shape, f.dtype),
        scratch_shapes=[pltpu.SemaphoreType.DMA, pltpu.SemaphoreType.DMA],
    )(f)


def _pack_rest(attn_out, ssm_in, glu_w, ssm_out):
    hd = ssm_in.shape[0] // 2
    return jnp.concatenate([attn_out, jnp.concatenate([ssm_in[:hd], ssm_in[hd:]], axis=1), glu_w, ssm_out], axis=0)


def _unpack_rest(p):
    D = p.shape[-1]
    q, hd = D // N_CHIPS, D // 2
    o = [0, q, q + hd, 2 * q + hd, 3 * q + hd]
    ssm_in = p[o[1]:o[2]]
    return p[o[0]:o[1]], jnp.concatenate([ssm_in[:, :hd], ssm_in[:, hd:]], axis=0), p[o[2]:o[3]], p[o[3]:o[4]]


def _attn_in_views(buf_a):
    D = buf_a.shape[-1]
    return [_View(buf_a, (D, D), lambda r, c, j=j: (j, r, c), D, D) for j in range(N_CHIPS)]


def _rest_views(buf_b):
    D = buf_b.shape[-1]
    q, hd = D // N_CHIPS, D // 2
    o_out, o_in, o_glu, o_sout = 0, q, q + hd, 2 * q + hd

    def row_sharded(off):
        return _View(buf_b, (D, D), lambda r, c: (r // q, off + r % q, c), math.gcd(q, off), D)

    def ssm_in(part):
        return _View(buf_b, (D, D), lambda r, c: (2 * part + c // hd, o_in + r % hd, (r // hd) * hd + c % hd),
                     math.gcd(hd, o_in), hd)

    return dict(attn_out=row_sharded(o_out), ssm_in_u=ssm_in(0), ssm_in_gate=ssm_in(1), glu=row_sharded(o_glu),
                ssm_out=row_sharded(o_sout))


def _pack_small(parts):
    flat = jnp.concatenate([p.reshape(-1) for p in parts])
    pad = (-flat.size) % (2 * SUBLANES * LANES)
    return jnp.pad(flat, (0, pad)).reshape(-1, LANES)


def _unpack_small(buf, shapes):
    flat = buf.reshape(-1)
    out, off = [], 0
    for s in shapes:
        n = math.prod(s)
        out.append(flat[off:off + n].reshape(s))
        off += n
    return out


def _local_step(x, target, norm_g, q_g, k_g, wa, wb, ssm_small, core=None):
    Bl, S, D = x.shape
    T = Bl * S
    x0 = x.reshape(T, D)
    tgt = target.reshape(T, D)
    a_re, a_im, log_dt, b_re, b_im, c_re, c_im, d_skip, glu_b = ssm_small
    G = a_re.shape[0]
    mats = _ssm_matrices(a_re, a_im, log_dt, b_re, b_im, c_re, c_im, d_skip, S)
    w_in = _attn_in_views(wa)
    rows_b = wb.shape[-2] * (1 if core is None else 2)
    ga = lax.empty(wa.shape, F32)
    gb = lax.empty((N_CHIPS, rows_b, D), F32)

    def wgrad(a, b, key, name):
        return _matmul(a, b, name=name, ta=True, out=_rest_views(gb)[key])

    h0 = _rmsnorm_fwd(x0, norm_g[0], "norm0_fwd")
    q, k, v, gate = [_matmul(h0, w_in[j], name=f"attn_in_{j}", out_dtype=(BF16 if j == 2 else F32)) for j in range(4)]
    qn = _qknorm_fwd(q, q_g, "qnorm_fwd")
    kn = _qknorm_fwd(k, k_g, "knorm_fwd")
    if core is None:
        o, btot, og = _attn_fwd(qn, kn, v, gate, Bl, S)
    else:
        o, btot, og, wb = _attn_fwd(qn, kn, v, gate, Bl, S, rider=_all_gather_rider(wb))
        wb = wb.reshape(N_CHIPS, rows_b, D)
    w = _rest_views(wb)
    x1 = _matmul(og, w["attn_out"], name="attn_out", residual=x0)

    h1 = _rmsnorm_fwd(x1, norm_g[1], "norm1_fwd")
    u = _matmul(h1, w["ssm_in_u"], name="ssm_in_u")
    gate2 = _matmul(h1, w["ssm_in_gate"], name="ssm_in_gate")
    y, hs_re, hs_im, yg = _ssm_fwd(u, mats, Bl, S)
    gl = _matmul(yg, w["glu"], name="glu_mm")
    y3 = _glu_fwd(y, gl, gate2, glu_b)
    x2 = _matmul(y3, w["ssm_out"], name="ssm_out", residual=x1)

    dx2, dx2b, loss = _loss_head(x2, tgt)

    dy3 = _matmul(dx2b, w["ssm_out"], name="ssm_out_dgrad", tb=True)
    gb = wgrad(y3, dx2b, "ssm_out", "ssm_out_wgrad")
    dgl, dgate2, t1, dglu_b = _glu_bwd(dy3, y, gl, gate2, glu_b)
    t2 = _matmul(dgl, w["glu"], name="glu_dgrad", tb=True)
    gb = wgrad(yg, dgl, "glu", "glu_wgrad")
    dy = _gelu_bwd(t1, t2, y)
    du, dbr, dbi, dcr, dci, dlr, dli, dd = _ssm_bwd(u, dy, hs_re, hs_im, mats, Bl, S)
    dh1 = _matmul(du, w["ssm_in_u"], name="ssm_in_dgrad_u", tb=True)
    dh1 = _matmul(dgate2, w["ssm_in_gate"], name="ssm_in_dgrad_gate", tb=True, residual=dh1)
    gb = wgrad(h1, du, "ssm_in_u", "ssm_in_wgrad_u")
    gb = wgrad(h1, dgate2, "ssm_in_gate", "ssm_in_wgrad_gate")
    dx1, dx1b, dng1 = _rmsnorm_bwd(x1, norm_g[1], dh1, dx2, "norm1_bwd")
    small_early = (dng1,) + _ssm_unblock(dbr, dbi, dcr, dci, dlr, dli, dd, G) + (dglu_b.reshape(D),)

    gb = wgrad(og, dx1b, "attn_out", "attn_out_wgrad")
    if core is None:
        dog = _matmul(dx1b, w["attn_out"], name="attn_out_dgrad", tb=True)
        do, dgate = _attn_gate_bwd(dog, o, gate)
        dqn, dkn, dv = _attn_bwd(qn, kn, v, do, btot, Bl, S)
    else:
        dog, from_sibling = _matmul(dx1b, w["attn_out"], name="attn_out_dgrad", tb=True, rider=_sibling_half_rider(gb))
        do, dgate = _attn_gate_bwd(dog, o, gate)
        chip_sum_b = _add_halves((gb, from_sibling), core, "grads_b_add_halves")
        dqn, dkn, dv, gb, small_early = _attn_bwd(
            qn, kn, v, do, btot, Bl, S,
            rider=_join_riders(_chip_exchange_rider(chip_sum_b), _all_gather_rider(_pack_small(small_early))))
    dq, dqg = _qknorm_bwd(q, q_g, dqn, "qnorm_bwd")
    dk, dkg = _qknorm_bwd(k, k_g, dkn, "knorm_bwd")
    dproj = [dq, dk, dv, dgate]
    for j in range(4):
        ga = _matmul(h0, dproj[j], name=f"attn_in_wgrad_{j}", ta=True, out=_attn_in_views(ga)[j])
    dh0 = None
    if core is None:
        for j in range(4):
            dh0 = _matmul(dproj[j], w_in[j], name=f"attn_in_dgrad_{j}", tb=True, residual=dh0)
    else:
        chip_sum_a = _add_halves((ga, _sibling_send_half(ga, "grads_a_sibling_half")), core, "grads_a_add_halves")
        rows = chip_sum_a.shape[1] // 4
        parts = []
        for j in range(4):
            dh0, part = _matmul(dproj[j], w_in[j], name=f"attn_in_dgrad_{j}", tb=True, residual=dh0,
                                rider=_chip_exchange_rider(chip_sum_a, j * rows, rows))
            parts.append(part)
        ga = jnp.concatenate(parts, axis=1)
    dx0, _, dng0 = _rmsnorm_bwd(x0, norm_g[0], dh0, dx1, "norm0_bwd")

    return loss, dx0.reshape(Bl, S, D), ga, gb, small_early, (dng0, dqg, dkg)


def _small_early_shapes(D):
    G = D // GROUP
    return [(1, D), (G, STATE), (G, STATE), (G, STATE, GROUP), (G, STATE, GROUP), (G, GROUP, STATE), (G, GROUP, STATE),
            (D,), (D,)]


def _chip_rows(a, chip, n_per):
    return lax.dynamic_slice_in_dim(a, chip * n_per, n_per, axis=0)


def kernel(x, norm_g, attn_w_in, attn_q_g, attn_k_g, attn_w_out, ssm_w_in, ssm_A_re, ssm_A_im, ssm_log_dt, ssm_B_re, ssm_B_im, ssm_C_re, ssm_C_im, ssm_D, ssm_glu_w, ssm_glu_b, ssm_w_out, loss_target, m_norm_g, m_attn_w_in, m_attn_q_g, m_attn_k_g, m_attn_w_out, m_ssm_w_in, m_ssm_A_re, m_ssm_A_im, m_ssm_log_dt, m_ssm_B_re, m_ssm_B_im, m_ssm_C_re, m_ssm_C_im, m_ssm_D, m_ssm_glu_w, m_ssm_glu_b, m_ssm_w_out, v_norm_g, v_attn_w_in, v_attn_q_g, v_attn_k_g, v_attn_w_out, v_ssm_w_in, v_ssm_A_re, v_ssm_A_im, v_ssm_log_dt, v_ssm_B_re, v_ssm_B_im, v_ssm_C_re, v_ssm_C_im, v_ssm_D, v_ssm_glu_w, v_ssm_glu_b, v_ssm_w_out):
    D = x.shape[-1]
    cx, cy, cc = _position()
    chip = 2 * cx + cy
    G = D // GROUP
    Gl = G // N_CHIPS

    def my_half(a):
        return lax.dynamic_slice_in_dim(a, cc * (a.shape[0] // 2), a.shape[0] // 2, axis=0)

    wa = _all_gather8(my_half(attn_w_in[0].astype(BF16)), "attn_in_all_gather").reshape(N_CHIPS, D, D)
    wb_half = my_half(_pack_rest(attn_w_out[0], ssm_w_in[0], ssm_glu_w[0], ssm_w_out[0]).astype(BF16))

    ssm_local = [ssm_A_re[0], ssm_A_im[0], ssm_log_dt[0], ssm_B_re[0], ssm_B_im[0], ssm_C_re[0], ssm_C_im[0], ssm_D[0],
                 ssm_glu_b[0]]
    small_local = _pack_small(ssm_local)
    half_rows = small_local.shape[0] // 2
    small_half = lax.dynamic_slice_in_dim(small_local, cc * half_rows, half_rows, axis=0)
    small_all = _all_gather8(small_half, "ssm_params_all_gather").reshape(N_CHIPS, 2 * half_rows, LANES)
    per_chip = [_unpack_small(small_all[j], [p.shape for p in ssm_local]) for j in range(N_CHIPS)]
    ssm_full = [jnp.concatenate([per_chip[j][i] for j in range(N_CHIPS)], axis=0) for i in range(len(ssm_local))]

    loss, grad_x, gathered_a, gathered_b, small_early, small_late = _local_step(
        x, loss_target, norm_g, attn_q_g[0], attn_k_g[0], wa, wb_half, ssm_full, core=cc)
    loss = lax.psum(loss[0, 0], ("x", "y", "c"))

    q4 = D // N_CHIPS
    south = cc == 0

    def both_halves(gathered, tag):
        mine = _sum_leading(gathered, f"grads_{tag}_sum_chips")
        other = _sibling_swap(mine, f"grads_{tag}_sibling_swap")
        return jnp.concatenate([jnp.where(south, mine, other), jnp.where(south, other, mine)], axis=0)

    gd = (both_halves(gathered_a, "a"),) + _unpack_rest(both_halves(gathered_b, "b"))

    (dng1, dl_re, dl_im, dbb_re, dbb_im, dc_re, dc_im, dd_skip, dglu_b) = _unpack_small(
        _sum_leading(small_early, "small_early_sum"), _small_early_shapes(D))
    late_all = _all_gather8(_pack_small(small_late), "small_late_all_gather")
    dng0, dqg, dkg = _unpack_small(_sum_leading(late_all, "small_late_sum"), [s.shape for s in small_late])
    dng = jnp.concatenate([dng0, dng1], axis=0)
    a_re, a_im, log_dt, b_re, b_im = ssm_local[:5]
    _, zoh_vjp = jax.vjp(_zoh, a_re, a_im, log_dt, b_re, b_im)
    da_re, da_im, dlog_dt, db_re, db_im = zoh_vjp((_chip_rows(dl_re, chip, Gl), _chip_rows(dl_im, chip, Gl),
                                                   _chip_rows(dbb_re, chip, Gl), _chip_rows(dbb_im, chip, Gl)))
    grads = {
        "norm_g": dng, "attn_w_in": gd[0][None], "attn_q_g": dqg, "attn_k_g": dkg, "attn_w_out": gd[1][None],
        "ssm_w_in": gd[2][None], "ssm_A_re": da_re[None], "ssm_A_im": da_im[None], "ssm_log_dt": dlog_dt[None],
        "ssm_B_re": db_re[None], "ssm_B_im": db_im[None], "ssm_C_re": _chip_rows(dc_re, chip, Gl)[None],
        "ssm_C_im": _chip_rows(dc_im, chip, Gl)[None], "ssm_D": _chip_rows(dd_skip, chip, q4)[None],
        "ssm_glu_w": gd[3][None], "ssm_glu_b": _chip_rows(dglu_b, chip, q4)[None], "ssm_w_out": gd[4][None],
    }
    weights = dict(norm_g=norm_g, attn_w_in=attn_w_in, attn_q_g=attn_q_g, attn_k_g=attn_k_g, attn_w_out=attn_w_out,
                   ssm_w_in=ssm_w_in, ssm_A_re=ssm_A_re, ssm_A_im=ssm_A_im, ssm_log_dt=ssm_log_dt, ssm_B_re=ssm_B_re,
                   ssm_B_im=ssm_B_im, ssm_C_re=ssm_C_re, ssm_C_im=ssm_C_im, ssm_D=ssm_D, ssm_glu_w=ssm_glu_w,
                   ssm_glu_b=ssm_glu_b, ssm_w_out=ssm_w_out)
    m = dict(norm_g=m_norm_g, attn_w_in=m_attn_w_in, attn_q_g=m_attn_q_g, attn_k_g=m_attn_k_g, attn_w_out=m_attn_w_out,
             ssm_w_in=m_ssm_w_in, ssm_A_re=m_ssm_A_re, ssm_A_im=m_ssm_A_im, ssm_log_dt=m_ssm_log_dt, ssm_B_re=m_ssm_B_re,
             ssm_B_im=m_ssm_B_im, ssm_C_re=m_ssm_C_re, ssm_C_im=m_ssm_C_im, ssm_D=m_ssm_D, ssm_glu_w=m_ssm_glu_w,
             ssm_glu_b=m_ssm_glu_b, ssm_w_out=m_ssm_w_out)
    v = dict(norm_g=v_norm_g, attn_w_in=v_attn_w_in, attn_q_g=v_attn_q_g, attn_k_g=v_attn_k_g, attn_w_out=v_attn_w_out,
             ssm_w_in=v_ssm_w_in, ssm_A_re=v_ssm_A_re, ssm_A_im=v_ssm_A_im, ssm_log_dt=v_ssm_log_dt, ssm_B_re=v_ssm_B_re,
             ssm_B_im=v_ssm_B_im, ssm_C_re=v_ssm_C_re, ssm_C_im=v_ssm_C_im, ssm_D=v_ssm_D, ssm_glu_w=v_ssm_glu_w,
             ssm_glu_b=v_ssm_glu_b, ssm_w_out=v_ssm_w_out)
    names = list(weights)
    dense_names = ("attn_w_in", "attn_w_out", "ssm_w_in", "ssm_glu_w", "ssm_w_out")
    delta, new_m, new_v = {}, {}, {}
    for n in dense_names:
        delta[n], new_m[n], new_v[n] = _adamw(weights[n], grads[n], m[n], v[n], "adamw_" + n)
    small_names = [n for n in names if n not in dense_names]
    small_shapes = [weights[n].shape for n in small_names]
    packs = [_pack_small([d[n] for n in small_names]) for d in (weights, grads, m, v)]
    outs = _adamw(*packs, "adamw_small")
    for res, out in zip((delta, new_m, new_v), outs):
        for n, val in zip(small_names, _unpack_small(out, small_shapes)):
            res[n] = val
    return (loss, grad_x, *[grads[n] for n in names], *[delta[n] for n in names], *[new_m[n] for n in names],
            *[new_v[n] for n in names])
```

```python
import functools
import math

import jax
import jax.numpy as jnp
from jax import lax
from jax.experimental import pallas as pl
from jax.experimental.pallas import tpu as pltpu

F32 = jnp.float32
BF16 = jnp.bfloat16

HEAD_DIM = 128
GROUP = 16
STATE = 64
RMS_EPS = 1e-6
ADAM_LR = 0.001
ADAM_B1 = 0.9
ADAM_B2 = 0.999
ADAM_EPS = 1e-08
ADAM_WD = 0.01
ADAM_STEP = 10

N_CHIPS = 4
N_DEV = 8
SUBLANES = 8
LANES = 128
VMEM_LIMIT = 56 * 1024 * 1024
ROW_BLOCK_ELEMS = 1 << 19
MATMUL_TILE = 1024
MATMUL_PANEL_BYTES = 8 * 1024 * 1024
MESH = pl.DeviceIdType.MESH

NN = (((1,), (0,)), ((), ()))
NT = (((1,), (1,)), ((), ()))
TN = (((0,), (0,)), ((), ()))


def _params(*sem):
    return pltpu.CompilerParams(dimension_semantics=sem, vmem_limit_bytes=VMEM_LIMIT)


def _dot(a, b, dims=NN):
    return lax.dot_general(a, b, dims, preferred_element_type=F32)


def _split(a):
    hi = a.astype(BF16)
    lo = (a - hi.astype(F32)).astype(BF16)
    return hi, lo


def _dot_f32(a, b, dims=NN):
    return _dot(a.astype(BF16), b.astype(BF16), dims)


def _sigmoid(x):
    return 1.0 / (1.0 + jnp.exp(-x))


def _silu_parts(x):
    s = _sigmoid(x)
    return x * s, s * (1.0 + x * (1.0 - s))


_GELU_C = math.sqrt(2.0 / math.pi)


def _gelu_parts(x):
    x2 = x * x
    t = jnp.tanh(_GELU_C * (x + 0.044715 * x * x2))
    val = 0.5 * x * (1.0 + t)
    der = 0.5 * (1.0 + t) + 0.5 * x * (1.0 - t * t) * _GELU_C * (1.0 + 3.0 * 0.044715 * x2)
    return val, der


class _View:
    def __init__(self, buf, shape, locate, row_tile, col_tile):
        self.buf, self.shape, self.locate, self.row_tile, self.col_tile = buf, shape, locate, row_tile, col_tile

    def spec(self, t0, t1, block_of):
        def index(i, j, k):
            rb, cb = block_of(i, j, k)
            slab, r, c = self.locate(rb * t0, cb * t1)
            return slab, r // t0, c // t1
        return pl.BlockSpec((None, t0, t1), index)


def _operand(x):
    return (x.buf, x.shape, x.row_tile, x.col_tile) if isinstance(x, _View) else (x, x.shape, x.shape[0], x.shape[1])


def _matmul(a, b, *, name, ta=False, tb=False, residual=None, out_dtype=F32, out=None, rider=None):
    a_arr, a_shape, a_rt, a_ct = _operand(a)
    b_arr, b_shape, b_rt, b_ct = _operand(b)
    (K, M) = a_shape if ta else a_shape[::-1]
    N = b_shape[0] if tb else b_shape[1]
    a_mt, a_kt = (a_ct, a_rt) if ta else (a_rt, a_ct)
    b_nt, b_kt = (b_rt, b_ct) if tb else (b_ct, b_rt)
    k_cap = MATMUL_PANEL_BYTES // (MATMUL_TILE * max(a_arr.dtype.itemsize, b_arr.dtype.itemsize))
    tm, tn, tk = min(M, MATMUL_TILE, a_mt), min(N, MATMUL_TILE, b_nt), min(K, k_cap, a_kt)
    if out is not None:
        tm, tn = min(tm, out.row_tile), min(tn, out.col_tile)
    pk = min(tk, b_kt)
    pieces = tk // pk
    nk = K // tk
    dims = ((((0,) if ta else (1,)), ((1,) if tb else (0,))), ((), ()))
    n_in = 1 + pieces + (residual is not None) + (out is not None)

    def body(*refs):
        a_ref, b_refs = refs[0], refs[1:1 + pieces]
        r_ref = refs[1 + pieces] if residual is not None else None
        o_ref = refs[n_in]

        def finish(r):
            if residual is not None:
                r = r + r_ref[...].astype(F32)
            o_ref[...] = r.astype(out_dtype)

        part = None
        for p, b_ref in enumerate(b_refs):
            ks = slice(p * pk, (p + 1) * pk)
            a_blk = a_ref[...] if pieces == 1 else (a_ref[ks, :] if ta else a_ref[:, ks])
            term = _dot(a_blk.astype(BF16), b_ref[...].astype(BF16), dims)
            part = term if part is None else part + term
        if nk == 1:
            finish(part)
            return
        acc = refs[n_in + 1]
        k = pl.program_id(2)

        @pl.when(k == 0)
        def _():
            acc[...] = part

        @pl.when(k > 0)
        def _():
            acc[...] += part

        @pl.when(k == nk - 1)
        def _():
            finish(acc[...])

    def spec(x, t0, t1, block_of):
        if isinstance(x, _View):
            return x.spec(t0, t1, block_of)
        return pl.BlockSpec((t0, t1), block_of)

    a_spec = spec(a, tk, tm, lambda i, j, k: (k, i)) if ta else spec(a, tm, tk, lambda i, j, k: (i, k))
    b_specs = [spec(b, tn, pk, lambda i, j, k, p=p: (j, k * pieces + p)) if tb else
               spec(b, pk, tn, lambda i, j, k, p=p: (k * pieces + p, j)) for p in range(pieces)]
    in_specs = [a_spec] + b_specs
    args = [a_arr] + [b_arr] * pieces
    if residual is not None:
        in_specs.append(pl.BlockSpec((tm, tn), lambda i, j, k: (i, j)))
        args.append(residual)
    aliases = {}
    if out is None:
        out_spec = pl.BlockSpec((tm, tn), lambda i, j, k: (i, j))
        out_shape = jax.ShapeDtypeStruct((M, N), out_dtype)
    else:
        out_spec = out.spec(tm, tn, lambda i, j, k: (i, j))
        out_shape = jax.ShapeDtypeStruct(out.buf.shape, out.buf.dtype)
        out_dtype = out.buf.dtype
        in_specs.append(pl.BlockSpec(memory_space=pl.ANY))
        args.append(out.buf)
        aliases = {len(args) - 1: 0}
    grid = (M // tm, N // tn, nk)
    scratch = [pltpu.VMEM((tm, tn), F32)] if nk > 1 else []
    if rider is None:
        return pl.pallas_call(
            body, name=name, grid=grid, in_specs=in_specs, out_specs=out_spec, out_shape=out_shape,
            scratch_shapes=scratch, input_output_aliases=aliases,
            compiler_params=_params("parallel", "parallel", "arbitrary"),
        )(*args)
    body, in_specs, out_specs, out_shape, scratch, extra = _ride(rider, body, grid, in_specs, [out_spec], [out_shape],
                                                                 scratch)
    return pl.pallas_call(
        body, name=name, grid=grid, in_specs=in_specs, out_specs=out_specs, out_shape=out_shape,
        scratch_shapes=scratch, input_output_aliases=aliases,
        compiler_params=_params("arbitrary", "arbitrary", "arbitrary"),
    )(*args, *extra)


def _row_tile(T, C):
    tr = max(SUBLANES, min(T, ROW_BLOCK_ELEMS // C) // SUBLANES * SUBLANES)
    while T % tr:
        tr -= SUBLANES
    return tr


def _rows_call(body, name, T, C, row_ins, full_ins, row_outs, acc_outs=()):
    tr = _row_tile(T, C)
    row_spec = pl.BlockSpec((tr, C), lambda i: (i, 0))
    in_specs = [row_spec] * len(row_ins) + [pl.BlockSpec(f.shape, lambda i, n=f.ndim: (0,) * n) for f in full_ins]
    out_specs = [row_spec] * len(row_outs) + [pl.BlockSpec(s, lambda i, n=len(s): (0,) * n) for s in acc_outs]
    out_shape = [jax.ShapeDtypeStruct((T, C), d) for d in row_outs] + [jax.ShapeDtypeStruct(s, F32) for s in acc_outs]
    return pl.pallas_call(
        body, name=name, grid=(T // tr,), in_specs=in_specs, out_specs=out_specs, out_shape=out_shape,
        compiler_params=_params("arbitrary" if acc_outs else "parallel"),
    )(*row_ins, *full_ins)


def _rmsnorm_fwd(x, g, name):
    T, C = x.shape

    def body(x_ref, g_ref, h_ref):
        xv = x_ref[...]
        r = lax.rsqrt(jnp.mean(xv * xv, axis=-1, keepdims=True) + RMS_EPS)
        h_ref[...] = ((xv * r) * g_ref[...]).astype(BF16)

    return _rows_call(body, name, T, C, [x], [g.reshape(1, C)], [BF16])[0]


def _rmsnorm_bwd(x, g, dh, dres, name):
    T, C = x.shape

    def body(x_ref, dh_ref, dres_ref, g_ref, dx_ref, dxb_ref, dg_ref):
        @pl.when(pl.program_id(0) == 0)
        def _():
            dg_ref[...] = jnp.zeros_like(dg_ref)

        xv = x_ref[...]
        dhv = dh_ref[...]
        r = lax.rsqrt(jnp.mean(xv * xv, axis=-1, keepdims=True) + RMS_EPS)
        xn = xv * r
        dg_ref[...] += jnp.sum(dhv * xn, axis=0, keepdims=True)
        dxn = dhv * g_ref[...]
        dx = dres_ref[...] + r * (dxn - xn * jnp.mean(dxn * xn, axis=-1, keepdims=True))
        dx_ref[...] = dx
        dxb_ref[...] = dx.astype(BF16)

    return _rows_call(body, name, T, C, [x, dh, dres], [g.reshape(1, C)], [F32, BF16], [(1, C)])


def _heads(C):
    return [slice(h * HEAD_DIM, (h + 1) * HEAD_DIM) for h in range(C // HEAD_DIM)]


def _qknorm_fwd(q, g, name):
    T, C = q.shape

    def body(q_ref, g_ref, o_ref):
        for head in _heads(C):
            xv = q_ref[:, head]
            r = lax.rsqrt(jnp.mean(xv * xv, axis=-1, keepdims=True) + RMS_EPS)
            o_ref[:, head] = ((xv * r) * g_ref[...]).astype(BF16)

    return _rows_call(body, name, T, C, [q], [g.reshape(1, HEAD_DIM)], [BF16])[0]


def _qknorm_bwd(q, g, dqn, name):
    T, C = q.shape

    def body(q_ref, d_ref, g_ref, dq_ref, dg_ref):
        @pl.when(pl.program_id(0) == 0)
        def _():
            dg_ref[...] = jnp.zeros_like(dg_ref)

        dg = jnp.zeros((1, HEAD_DIM), F32)
        for head in _heads(C):
            xv = q_ref[:, head]
            dv = d_ref[:, head]
            r = lax.rsqrt(jnp.mean(xv * xv, axis=-1, keepdims=True) + RMS_EPS)
            xn = xv * r
            dg = dg + jnp.sum(dv * xn, axis=0, keepdims=True)
            dxn = dv * g_ref[...]
            dq_ref[:, head] = (r * (dxn - xn * jnp.mean(dxn * xn, axis=-1, keepdims=True))).astype(BF16)
        dg_ref[...] += dg

    return _rows_call(body, name, T, C, [q, dqn], [g.reshape(1, HEAD_DIM)], [BF16], [(1, HEAD_DIM)])


ATT_TQ = 512
ATT_TK = 256


def _logsig_pair(z):
    a = jnp.minimum(z, 0.0) - jnp.log(1.0 + jnp.exp(-jnp.abs(z)))
    return a, a - z


def _tri(n, strict_upper_src):
    j = lax.broadcasted_iota(jnp.int32, (n, n), 0)
    s = lax.broadcasted_iota(jnp.int32, (n, n), 1)
    if strict_upper_src == "gt":
        m = j > s
    elif strict_upper_src == "le":
        m = j <= s
    else:
        m = j < s
    return jnp.where(m, 1.0, 0.0).astype(BF16)


def _cumdot(x, tri):
    hi, lo = _split(x)
    return _dot(hi, tri) + _dot(lo, tri)


def _attn_tiles(S):
    tq, tk = min(ATT_TQ, S), min(ATT_TK, S)
    return tq, tk, S // tq, tq // tk


def _attn_fwd(qn, kn, v, gate, Bl, S, rider=None):
    T, C = qn.shape
    H = C // HEAD_DIM
    tq, tk, nq, kpq = _attn_tiles(S)
    scale = 1.0 / math.sqrt(HEAD_DIM)

    def body(q_ref, k_ref, v_ref, g_ref, o_ref, bt_ref, og_ref):
        i = pl.program_id(2)
        tri = _tri(tk, "gt")
        rowpos = lax.broadcasted_iota(jnp.int32, (tq, tk), 0) + i * tq
        colpos = lax.broadcasted_iota(jnp.int32, (tq, tk), 1)
        o_ref[...] = jnp.zeros_like(o_ref)
        bt_ref[...] = jnp.zeros_like(bt_ref)

        def group(g, masked):
            blocks = []
            for p in reversed(range(kpq)):
                j = g * kpq + p
                rows = pl.ds(pl.multiple_of(j * tk, tk), tk)
                a, b = _logsig_pair(_dot(q_ref[...], k_ref[rows, :], NT) * scale)
                mask = (colpos + j * tk) < rowpos if masked else None
                if masked:
                    b = jnp.where(mask, b, 0.0)
                blocks.append((rows, a, b, _cumdot(b, tri), mask))
            total = bt_ref[...]
            out = None
            for rows, a, b, suffix, mask in blocks:
                w = jnp.exp(a + suffix + total)
                if masked:
                    w = jnp.where(mask, w, 0.0)
                term = _dot(w.astype(BF16), v_ref[rows, :])
                out = term if out is None else out + term
                total = total + jnp.sum(b, axis=-1, keepdims=True)
            o_ref[...] += out
            bt_ref[...] = total

        group(i, True)

        def step(n, carry):
            group(i - 1 - n, False)
            return carry

        lax.fori_loop(0, i, step, 0)
        og_ref[...] = (o_ref[...] * _silu_parts(g_ref[...])[0]).astype(BF16)

    qspec = pl.BlockSpec((tq, HEAD_DIM), lambda b, h, i: (b * nq + i, h))
    kspec = pl.BlockSpec((S, HEAD_DIM), lambda b, h, i: (b, h))
    btspec = pl.BlockSpec((None, None, tq, 1), lambda b, h, i: (b, h, i, 0))
    grid = (Bl, H, nq)
    body, in_specs, out_specs, out_shape, scratch, extra = _ride(
        rider, body, grid, [qspec, kspec, kspec, qspec], [qspec, btspec, qspec],
        [jax.ShapeDtypeStruct((T, C), F32), jax.ShapeDtypeStruct((Bl, H, S, 1), F32),
         jax.ShapeDtypeStruct((T, C), BF16)], [])
    return pl.pallas_call(
        body, name="attn_fwd", grid=grid, in_specs=in_specs, out_specs=out_specs, out_shape=out_shape,
        scratch_shapes=scratch, compiler_params=_params("arbitrary", "arbitrary", "arbitrary"),
    )(qn, kn, v, gate, *extra)


def _attn_bwd(qn, kn, v, do, btot, Bl, S, rider=None):
    T, C = qn.shape
    H = C // HEAD_DIM
    tq, tk, nq, kpq = _attn_tiles(S)
    scale = 1.0 / math.sqrt(HEAD_DIM)

    def body(q_ref, k_ref, v_ref, do_ref, bt_ref, dq_ref, dk_ref, dv_ref, pb_ref, pdl_ref):
        i = pl.program_id(2)

        @pl.when(i == 0)
        def _():
            dk_ref[...] = jnp.zeros_like(dk_ref)
            dv_ref[...] = jnp.zeros_like(dv_ref)

        tri_le = _tri(tk, "le")
        tri_lt = _tri(tk, "lt")
        rowpos = lax.broadcasted_iota(jnp.int32, (tq, tk), 0) + i * tq
        colpos = lax.broadcasted_iota(jnp.int32, (tq, tk), 1)
        dq_ref[...] = jnp.zeros_like(dq_ref)
        pb_ref[...] = bt_ref[...]
        pdl_ref[...] = jnp.zeros_like(pdl_ref)

        def group(g, masked):
            blocks = []
            for p in range(kpq):
                j = g * kpq + p
                rows = pl.ds(pl.multiple_of(j * tk, tk), tk)
                a, b = _logsig_pair(_dot(q_ref[...], k_ref[rows, :], NT) * scale)
                mask = (colpos + j * tk) < rowpos if masked else None
                if masked:
                    b = jnp.where(mask, b, 0.0)
                blocks.append((rows, a, b, _cumdot(b, tri_le), _dot(do_ref[...], v_ref[rows, :], NT), mask))
            remaining = pb_ref[...]
            swept = pdl_ref[...]
            dq = None
            for rows, a, b, cum, dw, mask in blocks:
                w = jnp.exp(a + (remaining - cum))
                if masked:
                    w = jnp.where(mask, w, 0.0)
                dl = dw * w
                prefix = swept + _cumdot(dl, tri_lt)
                beta = jnp.exp(a)
                dz = dl * (1.0 - beta) - beta * prefix
                if masked:
                    dz = jnp.where(mask, dz, 0.0)
                dzb = (dz * scale).astype(BF16)
                term = _dot(dzb, k_ref[rows, :])
                dq = term if dq is None else dq + term
                dk_ref[rows, :] += _dot(dzb, q_ref[...], TN)
                dv_ref[rows, :] += _dot(w.astype(BF16), do_ref[...], TN)
                remaining = remaining - jnp.sum(b, axis=-1, keepdims=True)
                swept = swept + jnp.sum(dl, axis=-1, keepdims=True)
            dq_ref[...] += dq
            pb_ref[...] = remaining
            pdl_ref[...] = swept

        def step(g, carry):
            group(g, False)
            return carry

        lax.fori_loop(0, i, step, 0)
        group(i, True)

    qspec = pl.BlockSpec((tq, HEAD_DIM), lambda b, h, i: (b * nq + i, h))
    kspec = pl.BlockSpec((S, HEAD_DIM), lambda b, h, i: (b, h))
    btspec = pl.BlockSpec((None, None, tq, 1), lambda b, h, i: (b, h, i, 0))
    grid = (Bl, H, nq)
    body, in_specs, out_specs, out_shape, scratch, extra = _ride(
        rider, body, grid, [qspec, kspec, kspec, qspec, btspec], [qspec, kspec, kspec],
        [jax.ShapeDtypeStruct((T, C), F32)] * 3, [pltpu.VMEM((tq, 1), F32), pltpu.VMEM((tq, 1), F32)])
    return pl.pallas_call(
        body, name="attn_bwd", grid=grid, in_specs=in_specs, out_specs=out_specs, out_shape=out_shape,
        scratch_shapes=scratch, compiler_params=_params("arbitrary", "arbitrary", "arbitrary"),
    )(qn, kn, v, do, btot, *extra)


SSM_TIME_BLOCK = 512
CHUNK = SUBLANES


def _cmadd(xr, xi, ar, ai, sr, si):
    return xr + ar * sr - ai * si, xi + ar * si + ai * sr


def _chunk_scan(xr, xi, tab_ref, cr, ci, reverse):
    for lvl, d in enumerate((1, 2, 4)):
        shift = (CHUNK - d) if reverse else d
        sr = pltpu.roll(xr, shift, 0)
        si = pltpu.roll(xi, shift, 0)
        ar = tab_ref[pl.ds((2 * lvl) * CHUNK, CHUNK), :]
        ai = tab_ref[pl.ds((2 * lvl + 1) * CHUNK, CHUNK), :]
        xr, xi = _cmadd(xr, xi, ar, ai, sr, si)
    pr = tab_ref[pl.ds(6 * CHUNK, CHUNK), :]
    pi = tab_ref[pl.ds(7 * CHUNK, CHUNK), :]
    return _cmadd(xr, xi, pr, pi, cr, ci)


def _ssm_dims(S, C):
    G = C // GROUP
    GT = min(16, G)
    return G, GT, G // GT, GT * GROUP, GT * STATE, min(SSM_TIME_BLOCK, S)


def _ssm_fwd(u, mats, Bl, S):
    T, C = u.shape
    G, GT, ngt, cw, sw, TB = _ssm_dims(S, C)
    ntb = S // TB
    nch = TB // CHUNK

    def body(u_ref, bre_ref, bim_ref, cre_ref, cim_ref, d_ref, tab_ref, y_ref, hr_ref, hi_ref, yg_ref, car_r, car_i):
        @pl.when(pl.program_id(2) == 0)
        def _():
            car_r[...] = jnp.zeros_like(car_r)
            car_i[...] = jnp.zeros_like(car_i)

        uv = u_ref[...]
        hr_ref[...] = _dot_f32(uv, bre_ref[...])
        hi_ref[...] = _dot_f32(uv, bim_ref[...])

        def step(n, carry):
            cr, ci = carry
            rows = pl.ds(pl.multiple_of(n * CHUNK, CHUNK), CHUNK)
            xr, xi = _chunk_scan(hr_ref[rows, :], hi_ref[rows, :], tab_ref, cr, ci, False)
            hr_ref[rows, :] = xr
            hi_ref[rows, :] = xi
            last = (CHUNK - 1, CHUNK)
            return (jnp.broadcast_to(xr[last[0]:last[1], :], xr.shape), jnp.broadcast_to(xi[last[0]:last[1], :], xi.shape))

        cr, ci = lax.fori_loop(0, nch, step, (car_r[...], car_i[...]))
        car_r[...] = cr
        car_i[...] = ci
        y = _dot_f32(hr_ref[...], cre_ref[...]) - _dot_f32(hi_ref[...], cim_ref[...]) + d_ref[...] * uv
        y_ref[...] = y
        yg_ref[...] = _gelu_parts(y)[0].astype(BF16)

    uspec = pl.BlockSpec((TB, cw), lambda g, b, t: (b * ntb + t, g))
    hspec = pl.BlockSpec((TB, sw), lambda g, b, t: (b * ntb + t, g))

    def gspec(r, c):
        return pl.BlockSpec((None, r, c), lambda g, b, t: (g, 0, 0))

    return pl.pallas_call(
        body, name="ssm_fwd", grid=(ngt, Bl, ntb),
        in_specs=[uspec, gspec(cw, sw), gspec(cw, sw), gspec(sw, cw), gspec(sw, cw), gspec(1, cw), gspec(8 * CHUNK, sw)],
        out_specs=[uspec, hspec, hspec, uspec],
        out_shape=[jax.ShapeDtypeStruct((T, C), F32), jax.ShapeDtypeStruct((T, G * STATE), F32),
                   jax.ShapeDtypeStruct((T, G * STATE), F32), jax.ShapeDtypeStruct((T, C), BF16)],
        scratch_shapes=[pltpu.VMEM((CHUNK, sw), F32), pltpu.VMEM((CHUNK, sw), F32)],
        compiler_params=_params("parallel", "arbitrary", "arbitrary"),
    )(u, mats["bbd_re"], mats["bbd_im"], mats["cbd_re"], mats["cbd_im"], mats["d"], mats["tab_fwd"])


def _ssm_bwd(u, dy, h_re, h_im, mats, Bl, S):
    T, C = u.shape
    G, GT, ngt, cw, sw, TB = _ssm_dims(S, C)
    ntb = S // TB
    nch = TB // CHUNK
    rpb = TB // CHUNK

    def body(u_ref, dy_ref, hr_ref, hi_ref, hpr_ref, hpi_ref, ctr_ref, cti_ref, btr_ref, bti_ref, d_ref, tab_ref,
             du_ref, dbr_ref, dbi_ref, dcr_ref, dci_ref, dlr_ref, dli_ref, dd_ref, gr_ref, gi_ref, car_r, car_i):
        b = pl.program_id(1)
        t = pl.program_id(2)

        @pl.when((b == 0) & (t == 0))
        def _():
            for ref in (dbr_ref, dbi_ref, dcr_ref, dci_ref, dlr_ref, dli_ref, dd_ref):
                ref[...] = jnp.zeros_like(ref)

        @pl.when(t == 0)
        def _():
            car_r[...] = jnp.zeros_like(car_r)
            car_i[...] = jnp.zeros_like(car_i)

        uv = u_ref[...]
        dyv = dy_ref[...]
        gr_ref[...] = _dot_f32(dyv, ctr_ref[...])
        gi_ref[...] = -_dot_f32(dyv, cti_ref[...])
        alive = jnp.where(t == ntb - 1, 0.0, 1.0)
        row0 = lax.broadcasted_iota(jnp.int32, (CHUNK, sw), 0) == 0

        def step(m, carry):
            cr, ci, ar, ai = carry
            n = nch - 1 - m
            rows = pl.ds(pl.multiple_of(n * CHUNK, CHUNK), CHUNK)
            prow = pl.ds(pl.multiple_of(jnp.maximum(n - 1, 0) * CHUNK, CHUNK), CHUNK)
            xr, xi = _chunk_scan(gr_ref[rows, :], gi_ref[rows, :], tab_ref, cr, ci, True)
            gr_ref[rows, :] = xr
            gi_ref[rows, :] = xi
            first = n == 0
            pr = jnp.where(first, hpr_ref[...] * alive, hr_ref[prow, :])
            pi = jnp.where(first, hpi_ref[...] * alive, hi_ref[prow, :])
            sr = jnp.where(row0, pltpu.roll(pr, 1, 0), pltpu.roll(hr_ref[rows, :], 1, 0))
            si = jnp.where(row0, pltpu.roll(pi, 1, 0), pltpu.roll(hi_ref[rows, :], 1, 0))
            ar = ar + xr * sr + xi * si
            ai = ai + xi * sr - xr * si
            return (jnp.broadcast_to(xr[0:1, :], xr.shape), jnp.broadcast_to(xi[0:1, :], xi.shape), ar, ai)

        zero = jnp.zeros((CHUNK, sw), F32)
        cr, ci, ar, ai = lax.fori_loop(0, nch, step, (car_r[...], car_i[...], zero, zero))
        car_r[...] = cr
        car_i[...] = ci
        dlr_ref[...] += ar
        dli_ref[...] += ai
        gr = gr_ref[...]
        gi = gi_ref[...]
        dbr_ref[...] += _dot_f32(uv, gr, TN)
        dbi_ref[...] += _dot_f32(uv, gi, TN)
        dcr_ref[...] += _dot_f32(hr_ref[...], dyv, TN)
        dci_ref[...] -= _dot_f32(hi_ref[...], dyv, TN)
        dd_ref[...] += jnp.sum(dyv * uv, axis=0, keepdims=True)
        du_ref[...] = (_dot_f32(gr, btr_ref[...]) + _dot_f32(gi, bti_ref[...]) + d_ref[...] * dyv).astype(BF16)

    def tblk(b, t):
        return b * ntb + (ntb - 1 - t)

    uspec = pl.BlockSpec((TB, cw), lambda g, b, t: (tblk(b, t), g))
    hspec = pl.BlockSpec((TB, sw), lambda g, b, t: (tblk(b, t), g))
    hpspec = pl.BlockSpec((CHUNK, sw), lambda g, b, t: (jnp.maximum(tblk(b, t) * rpb - 1, 0), g))

    def gspec(r, c):
        return pl.BlockSpec((None, r, c), lambda g, b, t: (g, 0, 0))

    def gshape(r, c):
        return jax.ShapeDtypeStruct((ngt, r, c), F32)

    return pl.pallas_call(
        body, name="ssm_bwd", grid=(ngt, Bl, ntb),
        in_specs=[uspec, uspec, hspec, hspec, hpspec, hpspec, gspec(cw, sw), gspec(cw, sw), gspec(sw, cw), gspec(sw, cw),
                  gspec(1, cw), gspec(8 * CHUNK, sw)],
        out_specs=[uspec, gspec(cw, sw), gspec(cw, sw), gspec(sw, cw), gspec(sw, cw), gspec(CHUNK, sw), gspec(CHUNK, sw),
                   gspec(1, cw)],
        out_shape=[jax.ShapeDtypeStruct((T, C), BF16), gshape(cw, sw), gshape(cw, sw), gshape(sw, cw), gshape(sw, cw),
                   gshape(CHUNK, sw), gshape(CHUNK, sw), gshape(1, cw)],
        scratch_shapes=[pltpu.VMEM((TB, sw), F32), pltpu.VMEM((TB, sw), F32), pltpu.VMEM((CHUNK, sw), F32),
                        pltpu.VMEM((CHUNK, sw), F32)],
        compiler_params=_params("arbitrary", "arbitrary", "arbitrary"),
    )(u, dy, h_re, h_im, h_re, h_im, mats["cbdT_re"], mats["cbdT_im"], mats["bbdT_re"], mats["bbdT_im"], mats["d"],
      mats["tab_rev"])


def _zoh(a_re, a_im, log_dt, b_re, b_im):
    dt = jnp.exp(log_dt)[:, None]
    mag = jnp.exp(a_re * dt)
    l_re = mag * jnp.cos(a_im * dt)
    l_im = mag * jnp.sin(a_im * dt)
    den = a_re * a_re + a_im * a_im
    f_re = ((l_re - 1.0) * a_re + l_im * a_im) / den
    f_im = (l_im * a_re - (l_re - 1.0) * a_im) / den
    bb_re = f_re[..., None] * b_re - f_im[..., None] * b_im
    bb_im = f_re[..., None] * b_im + f_im[..., None] * b_re
    return l_re, l_im, bb_re, bb_im


def _ssm_matrices(a_re, a_im, log_dt, b_re, b_im, c_re, c_im, d, S):
    G = a_re.shape[0]
    _, GT, ngt, cw, sw, _ = _ssm_dims(S, G * GROUP)
    _, _, bb_re, bb_im = _zoh(a_re, a_im, log_dt, b_re, b_im)
    eye = jnp.eye(GT, dtype=F32)

    def bd_b(bb):
        return jnp.einsum("tgpi,gh->tgihp", bb.reshape(ngt, GT, STATE, GROUP), eye).reshape(ngt, cw, sw)

    def bd_c(c):
        return jnp.einsum("tgip,gh->tgphi", c.reshape(ngt, GT, GROUP, STATE), eye).reshape(ngt, sw, cw)

    dt = jnp.exp(log_dt)[:, None]

    def power(k, conj):
        mag = jnp.exp(k * a_re * dt)
        ang = k * a_im * dt
        return (mag * jnp.cos(ang)).reshape(ngt, 1, sw), ((-1.0 if conj else 1.0) * mag * jnp.sin(ang)).reshape(ngt, 1, sw)

    r = jnp.arange(CHUNK)[None, :, None]

    def table(reverse):
        parts = []
        for dd in (1, 2, 4):
            pr, pi = power(float(dd), reverse)
            keep = (r <= CHUNK - 1 - dd) if reverse else (r >= dd)
            parts += [jnp.where(keep, pr, 0.0), jnp.where(keep, pi, 0.0)]
        exps = [(CHUNK - k) if reverse else (k + 1) for k in range(CHUNK)]
        pw = [power(float(e), reverse) for e in exps]
        parts += [jnp.concatenate([p[0] for p in pw], axis=1), jnp.concatenate([p[1] for p in pw], axis=1)]
        return jnp.concatenate([jnp.broadcast_to(p, (ngt, CHUNK, sw)) for p in parts], axis=1)

    mats = dict(bbd_re=bd_b(bb_re).astype(BF16), bbd_im=bd_b(bb_im).astype(BF16), cbd_re=bd_c(c_re).astype(BF16),
                cbd_im=bd_c(c_im).astype(BF16), d=d.reshape(ngt, 1, cw), tab_fwd=table(False), tab_rev=table(True))
    for k in ("bbd_re", "bbd_im", "cbd_re", "cbd_im"):
        mats[k.replace("bd_", "bdT_")] = jnp.swapaxes(mats[k], 1, 2)
    return mats


def _ssm_unblock(dbr, dbi, dcr, dci, dlr, dli, dd, G):
    ngt = dbr.shape[0]
    GT = G // ngt
    eye = jnp.eye(GT, dtype=F32)

    def ub(x):
        return jnp.einsum("tgihp,gh->tgpi", x.reshape(ngt, GT, GROUP, GT, STATE), eye).reshape(G, STATE, GROUP)

    def uc(x):
        return jnp.einsum("tgphi,gh->tgip", x.reshape(ngt, GT, STATE, GT, GROUP), eye).reshape(G, GROUP, STATE)

    return (dlr.sum(axis=1).reshape(G, STATE), dli.sum(axis=1).reshape(G, STATE), ub(dbr), ub(dbi), uc(dcr), uc(dci),
            dd.reshape(G * GROUP))


def _attn_gate_bwd(dog, o, gate):
    T, C = o.shape

    def body(d_ref, o_ref, g_ref, do_ref, dg_ref):
        val, der = _silu_parts(g_ref[...])
        dv = d_ref[...]
        do_ref[...] = (dv * val).astype(BF16)
        dg_ref[...] = (dv * o_ref[...] * der).astype(BF16)

    return _rows_call(body, "attn_gate_bwd", T, C, [dog, o, gate], [], [BF16, BF16])


def _glu_fwd(y, gl, gate, glu_b):
    T, C = y.shape

    def body(y_ref, gl_ref, g_ref, b_ref, out_ref):
        yg = _gelu_parts(y_ref[...])[0]
        sg = _sigmoid(gl_ref[...] + b_ref[...])
        out_ref[...] = (yg * sg * _silu_parts(g_ref[...])[0]).astype(BF16)

    return _rows_call(body, "glu_fwd", T, C, [y, gl, gate], [glu_b.reshape(1, C)], [BF16])[0]


def _glu_bwd(dy3, y, gl, gate, glu_b):
    T, C = y.shape

    def body(d_ref, y_ref, gl_ref, g_ref, b_ref, dgl_ref, dgate_ref, t1_ref, db_ref):
        @pl.when(pl.program_id(0) == 0)
        def _():
            db_ref[...] = jnp.zeros_like(db_ref)

        yg = _gelu_parts(y_ref[...])[0]
        sg = _sigmoid(gl_ref[...] + b_ref[...])
        sl, sld = _silu_parts(g_ref[...])
        dv = d_ref[...]
        dy2 = dv * sl
        dgl = dy2 * yg * sg * (1.0 - sg)
        dgl_ref[...] = dgl.astype(BF16)
        dgate_ref[...] = (dv * (yg * sg) * sld).astype(BF16)
        t1_ref[...] = dy2 * sg
        db_ref[...] += jnp.sum(dgl, axis=0, keepdims=True)

    return _rows_call(body, "glu_bwd", T, C, [dy3, y, gl, gate], [glu_b.reshape(1, C)], [BF16, BF16, F32], [(1, C)])


def _gelu_bwd(t1, t2, y):
    T, C = y.shape

    def body(a_ref, b_ref, y_ref, out_ref):
        out_ref[...] = (a_ref[...] + b_ref[...]) * _gelu_parts(y_ref[...])[1]

    return _rows_call(body, "gelu_bwd", T, C, [t1, t2, y], [], [F32])[0]


def _loss_head(x2, target):
    T, C = x2.shape

    def body(x_ref, t_ref, d_ref, db_ref, l_ref):
        @pl.when(pl.program_id(0) == 0)
        def _():
            l_ref[...] = jnp.zeros_like(l_ref)

        e = x_ref[...] - t_ref[...]
        d = e * (1.0 / C)
        d_ref[...] = d
        db_ref[...] = d.astype(BF16)
        l_ref[...] += 0.5 * jnp.sum(jnp.sum(e * e, axis=-1, keepdims=True) * (1.0 / C), axis=0, keepdims=True)

    return _rows_call(body, "loss_head", T, C, [x2, target], [], [F32, BF16], [(1, 1)])


def _adamw(w, g, m, v, name):
    shape = w.shape
    C = shape[-1]
    R = w.size // C
    bc1 = 1.0 - ADAM_B1 ** ADAM_STEP
    bc2 = 1.0 - ADAM_B2 ** ADAM_STEP

    def body(w_ref, g_ref, m_ref, v_ref, d_ref, nm_ref, nv_ref):
        gv = g_ref[...]
        mn = ADAM_B1 * m_ref[...] + (1.0 - ADAM_B1) * gv
        vn = ADAM_B2 * v_ref[...] + (1.0 - ADAM_B2) * (gv * gv)
        d_ref[...] = -ADAM_LR * ((mn / bc1) / (jnp.sqrt(vn / bc2) + ADAM_EPS) + ADAM_WD * w_ref[...])
        nm_ref[...] = mn
        nv_ref[...] = vn

    outs = _rows_call(body, name, R, C, [a.reshape(R, C) for a in (w, g, m, v)], [], [F32, F32, F32])
    return [o.reshape(shape) for o in outs]


def _sum_leading(x, name):
    n, R, C = x.shape
    tr = _row_tile(R, C * n)

    def body(x_ref, o_ref):
        acc = x_ref[0].astype(F32)
        for k in range(1, n):
            acc = acc + x_ref[k].astype(F32)
        o_ref[...] = acc

    return pl.pallas_call(
        body, name=name, grid=(R // tr,), in_specs=[pl.BlockSpec((n, tr, C), lambda i: (0, i, 0))],
        out_specs=pl.BlockSpec((tr, C), lambda i: (i, 0)), out_shape=jax.ShapeDtypeStruct((R, C), F32),
        compiler_params=_params("parallel"),
    )(x)


def _add_halves(g, c, name):
    full, recv = g
    n, R, C = full.shape
    half = R // 2
    tr = _row_tile(half, C)
    nb = half // tr

    def body(c_ref, a_ref, b_ref, o_ref):
        o_ref[...] = (a_ref[...] + b_ref[...]).astype(BF16)

    grid_spec = pltpu.PrefetchScalarGridSpec(
        num_scalar_prefetch=1, grid=(n, nb),
        in_specs=[pl.BlockSpec((None, tr, C), lambda j, i, c_ref: (j, c_ref[0] * nb + i, 0)),
                  pl.BlockSpec((None, tr, C), lambda j, i, c_ref: (j, i, 0))],
        out_specs=pl.BlockSpec((None, tr, C), lambda j, i, c_ref: (j, i, 0)))
    return pl.pallas_call(
        body, name=name, grid_spec=grid_spec, out_shape=jax.ShapeDtypeStruct((n, half, C), BF16),
        compiler_params=_params("parallel", "parallel"),
    )(c.reshape(1).astype(jnp.int32), full, recv)


ANY = pl.BlockSpec(memory_space=pl.ANY)


def _position():
    return lax.axis_index("x"), lax.axis_index("y"), lax.axis_index("c")


def _all_gather8(blk, name):
    return _standalone(_all_gather_rider(blk), name)[0]


class _Rider:
    def __init__(self, arrays, out_shapes, sems, start, finish):
        self.arrays, self.out_shapes, self.sems, self.start, self.finish = arrays, out_shapes, sems, start, finish


def _join_riders(*riders):
    def split(refs):
        ins, outs, sems = [], [], []
        for r in riders:
            ins.append(refs[:len(r.arrays)])
            refs = refs[len(r.arrays):]
        for r in riders:
            outs.append(refs[:len(r.out_shapes)])
            refs = refs[len(r.out_shapes):]
        for r in riders:
            sems.append(refs[:len(r.sems)])
            refs = refs[len(r.sems):]
        return [i + o + s for i, o, s in zip(ins, outs, sems)]

    def start(*refs):
        for r, own in zip(riders, split(refs)):
            r.start(*own)

    def finish(*refs):
        for r, own in zip(riders, split(refs)):
            r.finish(*own)

    return _Rider([a for r in riders for a in r.arrays], [s for r in riders for s in r.out_shapes],
                  [s for r in riders for s in r.sems], start, finish)


def _standalone(rider, name):
    def body(*refs):
        rider.start(*refs)
        rider.finish(*refs)

    return pl.pallas_call(
        body, name=name, in_specs=[ANY] * len(rider.arrays), out_specs=[ANY] * len(rider.out_shapes),
        out_shape=rider.out_shapes, scratch_shapes=rider.sems,
    )(*rider.arrays)


def _ride(rider, body, grid, in_specs, out_specs, out_shape, scratch):
    if rider is None:
        return body, in_specs, out_specs, out_shape, scratch, []
    ni, no, ns = len(in_specs), len(out_specs), len(scratch)
    ri, ro = len(rider.arrays), len(rider.out_shapes)

    def full(*refs):
        ins, refs = refs[:ni], refs[ni:]
        r_ins, refs = refs[:ri], refs[ri:]
        outs, refs = refs[:no], refs[no:]
        r_outs, refs = refs[:ro], refs[ro:]
        scr, r_sems = refs[:ns], refs[ns:]
        ids = [pl.program_id(a) for a in range(len(grid))]
        first = functools.reduce(jnp.logical_and, [i == 0 for i in ids])
        last = functools.reduce(jnp.logical_and, [i == g - 1 for i, g in zip(ids, grid)])

        @pl.when(first)
        def _():
            rider.start(*r_ins, *r_outs, *r_sems)

        body(*ins, *outs, *scr)

        @pl.when(last)
        def _():
            rider.finish(*r_ins, *r_outs, *r_sems)

    return (full, in_specs + [ANY] * ri, out_specs + [ANY] * ro, out_shape + rider.out_shapes, scratch + rider.sems,
            rider.arrays)


def _all_gather_rider(blk):
    M, N = blk.shape

    def copies(x_ref, out_ref, send_sems, recv_sems, local_sem):
        x, y, c = _position()
        me, sibling = (x, y, c), (x, y, 1 - c)
        chips = [(1 - x, y), (x, 1 - y), (1 - x, 1 - y)]

        def slab(px, py, pc):
            return out_ref.at[4 * px + 2 * py + pc]

        def copy(k, block, to, src=None):
            return pltpu.make_async_remote_copy(
                src_ref=slab(*block) if src is None else src, dst_ref=slab(*block),
                send_sem=send_sems.at[k], recv_sem=recv_sems.at[k], device_id=to, device_id_type=MESH)

        mine = pltpu.make_async_copy(x_ref, slab(*me), local_sem)
        first = [copy(0, me, sibling, src=x_ref)]
        first += [copy(1 + j, me, (*chip, c), src=x_ref) for j, chip in enumerate(chips)]
        passed = [copy(4 + j, (*chip, c), sibling) for j, chip in enumerate(chips)]
        arrivals = [copy(1 + j, (*chip, c), me) for j, chip in enumerate(chips)]
        from_sibling = [copy(0, sibling, me)] + [copy(4 + j, (*chip, 1 - c), me) for j, chip in enumerate(chips)]
        return mine, first, passed, arrivals, from_sibling

    def start(*refs):
        mine, first, _, _, _ = copies(*refs)
        mine.start()
        for cp in first:
            cp.start()

    def finish(*refs):
        mine, first, passed, arrivals, from_sibling = copies(*refs)
        for arrival, onward in zip(arrivals, passed):
            arrival.wait_recv()
            onward.start()
        for cp in from_sibling:
            cp.wait_recv()
        for cp in first + passed:
            cp.wait_send()
        mine.wait()

    return _Rider([blk], [jax.ShapeDtypeStruct((N_DEV, M, N), blk.dtype)],
                  [pltpu.SemaphoreType.DMA((7,)), pltpu.SemaphoreType.DMA((7,)), pltpu.SemaphoreType.DMA], start, finish)


def _sibling_half_rider(g):
    n, R, C = g.shape
    half = R // 2

    def copy(g_ref, out_ref, send_sem, recv_sem):
        x, y, c = _position()
        return pltpu.make_async_remote_copy(
            src_ref=g_ref.at[:, pl.ds((1 - c) * half, half), :], dst_ref=out_ref, send_sem=send_sem, recv_sem=recv_sem,
            device_id=(x, y, 1 - c), device_id_type=MESH)

    return _Rider([g], [jax.ShapeDtypeStruct((n, half, C), g.dtype)], [pltpu.SemaphoreType.DMA, pltpu.SemaphoreType.DMA],
                  lambda *refs: copy(*refs).start(), lambda *refs: copy(*refs).wait())


def _sibling_send_half(g, name):
    return _standalone(_sibling_half_rider(g), name)[0]


def _chip_exchange_rider(p, row_off=0, rows=None):
    rows = p.shape[1] if rows is None else rows

    def copies(p_ref, out_ref, send_sems, recv_sems, local_sem):
        x, y, c = _position()
        my = 2 * x + y
        chips = [(1 - x, y), (x, 1 - y), (1 - x, 1 - y)]

        def src(slab):
            return p_ref.at[slab, pl.ds(row_off, rows), :]

        mine = pltpu.make_async_copy(src(my), out_ref.at[my], local_sem)
        sends = [pltpu.make_async_remote_copy(
            src_ref=src(2 * px + py), dst_ref=out_ref.at[my], send_sem=send_sems.at[k], recv_sem=recv_sems.at[k],
            device_id=(px, py, c), device_id_type=MESH) for k, (px, py) in enumerate(chips)]
        arrivals = [pltpu.make_async_remote_copy(
            src_ref=src(my), dst_ref=out_ref.at[2 * px + py], send_sem=send_sems.at[k], recv_sem=recv_sems.at[k],
            device_id=(px, py, c), device_id_type=MESH) for k, (px, py) in enumerate(chips)]
        return mine, sends, arrivals

    def start(*refs):
        mine, sends, _ = copies(*refs)
        mine.start()
        for cp in sends:
            cp.start()

    def finish(*refs):
        mine, sends, arrivals = copies(*refs)
        for cp in arrivals:
            cp.wait_recv()
        for cp in sends:
            cp.wait_send()
        mine.wait()

    return _Rider([p], [jax.ShapeDtypeStruct((p.shape[0], rows, p.shape[2]), p.dtype)],
                  [pltpu.SemaphoreType.DMA((3,)), pltpu.SemaphoreType.DMA((3,)), pltpu.SemaphoreType.DMA], start, finish)


def _sibling_swap(f, name):
    def body(f_ref, out_ref, send_sem, recv_sem):
        x, y, c = _position()
        cp = pltpu.make_async_remote_copy(src_ref=f_ref, dst_ref=out_ref, send_sem=send_sem, recv_sem=recv_sem,
                                          device_id=(x, y, 1 - c), device_id_type=MESH)
        cp.start()
        cp.wait()

    return pl.pallas_call(
        body, name=name, in_specs=[ANY], out_specs=ANY, out_shape=jax.ShapeDtypeStruct(f.shape, f.dtype),
        scratch_shapes=[pltpu.SemaphoreType.DMA, pltpu.SemaphoreType.DMA],
    )(f)


def _pack_rest(attn_out, ssm_in, glu_w, ssm_out):
    hd = ssm_in.shape[0] // 2
    return jnp.concatenate([attn_out, jnp.concatenate([ssm_in[:hd], ssm_in[hd:]], axis=1), glu_w, ssm_out], axis=0)


def _unpack_rest(p):
    D = p.shape[-1]
    q, hd = D // N_CHIPS, D // 2
    o = [0, q, q + hd, 2 * q + hd, 3 * q + hd]
    ssm_in = p[o[1]:o[2]]
    return p[o[0]:o[1]], jnp.concatenate([ssm_in[:, :hd], ssm_in[:, hd:]], axis=0), p[o[2]:o[3]], p[o[3]:o[4]]


def _attn_in_views(buf_a):
    D = buf_a.shape[-1]
    return [_View(buf_a, (D, D), lambda r, c, j=j: (j, r, c), D, D) for j in range(N_CHIPS)]


def _rest_views(buf_b):
    D = buf_b.shape[-1]
    q, hd = D // N_CHIPS, D // 2
    o_out, o_in, o_glu, o_sout = 0, q, q + hd, 2 * q + hd

    def row_sharded(off):
        return _View(buf_b, (D, D), lambda r, c: (r // q, off + r % q, c), math.gcd(q, off), D)

    def ssm_in(part):
        return _View(buf_b, (D, D), lambda r, c: (2 * part + c // hd, o_in + r % hd, (r // hd) * hd + c % hd),
                     math.gcd(hd, o_in), hd)

    return dict(attn_out=row_sharded(o_out), ssm_in_u=ssm_in(0), ssm_in_gate=ssm_in(1), glu=row_sharded(o_glu),
                ssm_out=row_sharded(o_sout))


def _pack_small(parts):
    flat = jnp.concatenate([p.reshape(-1) for p in parts])
    pad = (-flat.size) % (2 * SUBLANES * LANES)
    return jnp.pad(flat, (0, pad)).reshape(-1, LANES)


def _unpack_small(buf, shapes):
    flat = buf.reshape(-1)
    out, off = [], 0
    for s in shapes:
        n = math.prod(s)
        out.append(flat[off:off + n].reshape(s))
        off += n
    return out


def _local_step(x, target, norm_g, q_g, k_g, wa, wb, ssm_small, core=None):
    Bl, S, D = x.shape
    T = Bl * S
    x0 = x.reshape(T, D)
    tgt = target.reshape(T, D)
    a_re, a_im, log_dt, b_re, b_im, c_re, c_im, d_skip, glu_b = ssm_small
    G = a_re.shape[0]
    mats = _ssm_matrices(a_re, a_im, log_dt, b_re, b_im, c_re, c_im, d_skip, S)
    w_in = _attn_in_views(wa)
    rows_b = wb.shape[-2] * (1 if core is None else 2)
    ga = lax.empty(wa.shape, F32)
    gb = lax.empty((N_CHIPS, rows_b, D), F32)

    def wgrad(a, b, key, name):
        return _matmul(a, b, name=name, ta=True, out=_rest_views(gb)[key])

    h0 = _rmsnorm_fwd(x0, norm_g[0], "norm0_fwd")
    q, k, v, gate = [_matmul(h0, w_in[j], name=f"attn_in_{j}", out_dtype=(BF16 if j == 2 else F32)) for j in range(4)]
    qn = _qknorm_fwd(q, q_g, "qnorm_fwd")
    kn = _qknorm_fwd(k, k_g, "knorm_fwd")
    if core is None:
        o, btot, og = _attn_fwd(qn, kn, v, gate, Bl, S)
    else:
        o, btot, og, wb = _attn_fwd(qn, kn, v, gate, Bl, S, rider=_all_gather_rider(wb))
        wb = wb.reshape(N_CHIPS, rows_b, D)
    w = _rest_views(wb)
    x1 = _matmul(og, w["attn_out"], name="attn_out", residual=x0)

    h1 = _rmsnorm_fwd(x1, norm_g[1], "norm1_fwd")
    u = _matmul(h1, w["ssm_in_u"], name="ssm_in_u")
    gate2 = _matmul(h1, w["ssm_in_gate"], name="ssm_in_gate")
    y, hs_re, hs_im, yg = _ssm_fwd(u, mats, Bl, S)
    gl = _matmul(yg, w["glu"], name="glu_mm")
    y3 = _glu_fwd(y, gl, gate2, glu_b)
    x2 = _matmul(y3, w["ssm_out"], name="ssm_out", residual=x1)

    dx2, dx2b, loss = _loss_head(x2, tgt)

    dy3 = _matmul(dx2b, w["ssm_out"], name="ssm_out_dgrad", tb=True)
    gb = wgrad(y3, dx2b, "ssm_out", "ssm_out_wgrad")
    dgl, dgate2, t1, dglu_b = _glu_bwd(dy3, y, gl, gate2, glu_b)
    t2 = _matmul(dgl, w["glu"], name="glu_dgrad", tb=True)
    gb = wgrad(yg, dgl, "glu", "glu_wgrad")
    dy = _gelu_bwd(t1, t2, y)
    du, dbr, dbi, dcr, dci, dlr, dli, dd = _ssm_bwd(u, dy, hs_re, hs_im, mats, Bl, S)
    dh1 = _matmul(du, w["ssm_in_u"], name="ssm_in_dgrad_u", tb=True)
    dh1 = _matmul(dgate2, w["ssm_in_gate"], name="ssm_in_dgrad_gate", tb=True, residual=dh1)
    gb = wgrad(h1, du, "ssm_in_u", "ssm_in_wgrad_u")
    gb = wgrad(h1, dgate2, "ssm_in_gate", "ssm_in_wgrad_gate")
    dx1, dx1b, dng1 = _rmsnorm_bwd(x1, norm_g[1], dh1, dx2, "norm1_bwd")
    small_early = (dng1,) + _ssm_unblock(dbr, dbi, dcr, dci, dlr, dli, dd, G) + (dglu_b.reshape(D),)

    gb = wgrad(og, dx1b, "attn_out", "attn_out_wgrad")
    if core is None:
        dog = _matmul(dx1b, w["attn_out"], name="attn_out_dgrad", tb=True)
        do, dgate = _attn_gate_bwd(dog, o, gate)
        dqn, dkn, dv = _attn_bwd(qn, kn, v, do, btot, Bl, S)
    else:
        dog, from_sibling = _matmul(dx1b, w["attn_out"], name="attn_out_dgrad", tb=True, rider=_sibling_half_rider(gb))
        do, dgate = _attn_gate_bwd(dog, o, gate)
        chip_sum_b = _add_halves((gb, from_sibling), core, "grads_b_add_halves")
        dqn, dkn, dv, gb, small_early = _attn_bwd(
            qn, kn, v, do, btot, Bl, S,
            rider=_join_riders(_chip_exchange_rider(chip_sum_b), _all_gather_rider(_pack_small(small_early))))
    dq, dqg = _qknorm_bwd(q, q_g, dqn, "qnorm_bwd")
    dk, dkg = _qknorm_bwd(k, k_g, dkn, "knorm_bwd")
    dproj = [dq, dk, dv, dgate]
    for j in range(4):
        ga = _matmul(h0, dproj[j], name=f"attn_in_wgrad_{j}", ta=True, out=_attn_in_views(ga)[j])
    dh0 = None
    if core is None:
        for j in range(4):
            dh0 = _matmul(dproj[j], w_in[j], name=f"attn_in_dgrad_{j}", tb=True, residual=dh0)
    else:
        chip_sum_a = _add_halves((ga, _sibling_send_half(ga, "grads_a_sibling_half")), core, "grads_a_add_halves")
        rows = chip_sum_a.shape[1] // 4
        parts = []
        for j in range(4):
            dh0, part = _matmul(dproj[j], w_in[j], name=f"attn_in_dgrad_{j}", tb=True, residual=dh0,
                                rider=_chip_exchange_rider(chip_sum_a, j * rows, rows))
            parts.append(part)
        ga = jnp.concatenate(parts, axis=1)
    dx0, _, dng0 = _rmsnorm_bwd(x0, norm_g[0], dh0, dx1, "norm0_bwd")

    return loss, dx0.reshape(Bl, S, D), ga, gb, small_early, (dng0, dqg, dkg)


def _small_early_shapes(D):
    G = D // GROUP
    return [(1, D), (G, STATE), (G, STATE), (G, STATE, GROUP), (G, STATE, GROUP), (G, GROUP, STATE), (G, GROUP, STATE),
            (D,), (D,)]


def _chip_rows(a, chip, n_per):
    return lax.dynamic_slice_in_dim(a, chip * n_per, n_per, axis=0)


def kernel(x, norm_g, attn_w_in, attn_q_g, attn_k_g, attn_w_out, ssm_w_in, ssm_A_re, ssm_A_im, ssm_log_dt, ssm_B_re, ssm_B_im, ssm_C_re, ssm_C_im, ssm_D, ssm_glu_w, ssm_glu_b, ssm_w_out, loss_target, m_norm_g, m_attn_w_in, m_attn_q_g, m_attn_k_g, m_attn_w_out, m_ssm_w_in, m_ssm_A_re, m_ssm_A_im, m_ssm_log_dt, m_ssm_B_re, m_ssm_B_im, m_ssm_C_re, m_ssm_C_im, m_ssm_D, m_ssm_glu_w, m_ssm_glu_b, m_ssm_w_out, v_norm_g, v_attn_w_in, v_attn_q_g, v_attn_k_g, v_attn_w_out, v_ssm_w_in, v_ssm_A_re, v_ssm_A_im, v_ssm_log_dt, v_ssm_B_re, v_ssm_B_im, v_ssm_C_re, v_ssm_C_im, v_ssm_D, v_ssm_glu_w, v_ssm_glu_b, v_ssm_w_out):
    D = x.shape[-1]
    cx, cy, cc = _position()
    chip = 2 * cx + cy
    G = D // GROUP
    Gl = G // N_CHIPS

    def my_half(a):
        return lax.dynamic_slice_in_dim(a, cc * (a.shape[0] // 2), a.shape[0] // 2, axis=0)

    wa = _all_gather8(my_half(attn_w_in[0].astype(BF16)), "attn_in_all_gather").reshape(N_CHIPS, D, D)
    wb_half = my_half(_pack_rest(attn_w_out[0], ssm_w_in[0], ssm_glu_w[0], ssm_w_out[0]).astype(BF16))

    ssm_local = [ssm_A_re[0], ssm_A_im[0], ssm_log_dt[0], ssm_B_re[0], ssm_B_im[0], ssm_C_re[0], ssm_C_im[0], ssm_D[0],
                 ssm_glu_b[0]]
    small_local = _pack_small(ssm_local)
    half_rows = small_local.shape[0] // 2
    small_half = lax.dynamic_slice_in_dim(small_local, cc * half_rows, half_rows, axis=0)
    small_all = _all_gather8(small_half, "ssm_params_all_gather").reshape(N_CHIPS, 2 * half_rows, LANES)
    per_chip = [_unpack_small(small_all[j], [p.shape for p in ssm_local]) for j in range(N_CHIPS)]
    ssm_full = [jnp.concatenate([per_chip[j][i] for j in range(N_CHIPS)], axis=0) for i in range(len(ssm_local))]

    loss, grad_x, gathered_a, gathered_b, small_early, small_late = _local_step(
        x, loss_target, norm_g, attn_q_g[0], attn_k_g[0], wa, wb_half, ssm_full, core=cc)
    loss = lax.psum(loss[0, 0], ("x", "y", "c"))

    q4 = D // N_CHIPS
    south = cc == 0

    def both_halves(gathered, tag):
        mine = _sum_leading(gathered, f"grads_{tag}_sum_chips")
        other = _sibling_swap(mine, f"grads_{tag}_sibling_swap")
        return jnp.concatenate([jnp.where(south, mine, other), jnp.where(south, other, mine)], axis=0)

    gd = (both_halves(gathered_a, "a"),) + _unpack_rest(both_halves(gathered_b, "b"))

    (dng1, dl_re, dl_im, dbb_re, dbb_im, dc_re, dc_im, dd_skip, dglu_b) = _unpack_small(
        _sum_leading(small_early, "small_early_sum"), _small_early_shapes(D))
    late_all = _all_gather8(_pack_small(small_late), "small_late_all_gather")
    dng0, dqg, dkg = _unpack_small(_sum_leading(late_all, "small_late_sum"), [s.shape for s in small_late])
    dng = jnp.concatenate([dng0, dng1], axis=0)
    a_re, a_im, log_dt, b_re, b_im = ssm_local[:5]
    _, zoh_vjp = jax.vjp(_zoh, a_re, a_im, log_dt, b_re, b_im)
    da_re, da_im, dlog_dt, db_re, db_im = zoh_vjp((_chip_rows(dl_re, chip, Gl), _chip_rows(dl_im, chip, Gl),
                                                   _chip_rows(dbb_re, chip, Gl), _chip_rows(dbb_im, chip, Gl)))
    grads = {
        "norm_g": dng, "attn_w_in": gd[0][None], "attn_q_g": dqg, "attn_k_g": dkg, "attn_w_out": gd[1][None],
        "ssm_w_in": gd[2][None], "ssm_A_re": da_re[None], "ssm_A_im": da_im[None], "ssm_log_dt": dlog_dt[None],
        "ssm_B_re": db_re[None], "ssm_B_im": db_im[None], "ssm_C_re": _chip_rows(dc_re, chip, Gl)[None],
        "ssm_C_im": _chip_rows(dc_im, chip, Gl)[None], "ssm_D": _chip_rows(dd_skip, chip, q4)[None],
        "ssm_glu_w": gd[3][None], "ssm_glu_b": _chip_rows(dglu_b, chip, q4)[None], "ssm_w_out": gd[4][None],
    }
    weights = dict(norm_g=norm_g, attn_w_in=attn_w_in, attn_q_g=attn_q_g, attn_k_g=attn_k_g, attn_w_out=attn_w_out,
                   ssm_w_in=ssm_w_in, ssm_A_re=ssm_A_re, ssm_A_im=ssm_A_im, ssm_log_dt=ssm_log_dt, ssm_B_re=ssm_B_re,
                   ssm_B_im=ssm_B_im, ssm_C_re=ssm_C_re, ssm_C_im=ssm_C_im, ssm_D=ssm_D, ssm_glu_w=ssm_glu_w,
                   ssm_glu_b=ssm_glu_b, ssm_w_out=ssm_w_out)
    m = dict(norm_g=m_norm_g, attn_w_in=m_attn_w_in, attn_q_g=m_attn_q_g, attn_k_g=m_attn_k_g, attn_w_out=m_attn_w_out,
             ssm_w_in=m_ssm_w_in, ssm_A_re=m_ssm_A_re, ssm_A_im=m_ssm_A_im, ssm_log_dt=m_ssm_log_dt, ssm_B_re=m_ssm_B_re,
             ssm_B_im=m_ssm_B_im, ssm_C_re=m_ssm_C_re, ssm_C_im=m_ssm_C_im, ssm_D=m_ssm_D, ssm_glu_w=m_ssm_glu_w,
             ssm_glu_b=m_ssm_glu_b, ssm_w_out=m_ssm_w_out)
    v = dict(norm_g=v_norm_g, attn_w_in=v_attn_w_in, attn_q_g=v_attn_q_g, attn_k_g=v_attn_k_g, attn_w_out=v_attn_w_out,
             ssm_w_in=v_ssm_w_in, ssm_A_re=v_ssm_A_re, ssm_A_im=v_ssm_A_im, ssm_log_dt=v_ssm_log_dt, ssm_B_re=v_ssm_B_re,
             ssm_B_im=v_ssm_B_im, ssm_C_re=v_ssm_C_re, ssm_C_im=v_ssm_C_im, ssm_D=v_ssm_D, ssm_glu_w=v_ssm_glu_w,
             ssm_glu_b=v_ssm_glu_b, ssm_w_out=v_ssm_w_out)
    names = list(weights)
    dense_names = ("attn_w_in", "attn_w_out", "ssm_w_in", "ssm_glu_w", "ssm_w_out")
    delta, new_m, new_v = {}, {}, {}
    for n in dense_names:
        delta[n], new_m[n], new_v[n] = _adamw(weights[n], grads[n], m[n], v[n], "adamw_" + n)
    small_names = [n for n in names if n not in dense_names]
    small_shapes = [weights[n].shape for n in small_names]
    packs = [_pack_small([d[n] for n in small_names]) for d in (weights, grads, m, v)]
    outs = _adamw(*packs, "adamw_small")
    for res, out in zip((delta, new_m, new_v), outs):
        for n, val in zip(small_names, _unpack_small(out, small_shapes)):
            res[n] = val
    return (loss, grad_x, *[grads[n] for n in names], *[delta[n] for n in names], *[new_m[n] for n in names],
            *[new_v[n] for n in names])
```

```python
import functools
import math

import jax
import jax.numpy as jnp
from jax import lax
from jax.experimental import pallas as pl
from jax.experimental.pallas import tpu as pltpu

F32 = jnp.float32
BF16 = jnp.bfloat16

HEAD_DIM = 128
GROUP = 16
STATE = 64
RMS_EPS = 1e-6
ADAM_LR = 0.001
ADAM_B1 = 0.9
ADAM_B2 = 0.999
ADAM_EPS = 1e-08
ADAM_WD = 0.01
ADAM_STEP = 10

N_CHIPS = 4
N_DEV = 8
SUBLANES = 8
LANES = 128
VMEM_LIMIT = 56 * 1024 * 1024
ROW_BLOCK_ELEMS = 1 << 19
MATMUL_TILE = 1024
MATMUL_PANEL_BYTES = 8 * 1024 * 1024
MESH = pl.DeviceIdType.MESH

NN = (((1,), (0,)), ((), ()))
NT = (((1,), (1,)), ((), ()))
TN = (((0,), (0,)), ((), ()))


def _params(*sem):
    return pltpu.CompilerParams(dimension_semantics=sem, vmem_limit_bytes=VMEM_LIMIT)


def _dot(a, b, dims=NN):
    return lax.dot_general(a, b, dims, preferred_element_type=F32)


def _split(a):
    hi = a.astype(BF16)
    lo = (a - hi.astype(F32)).astype(BF16)
    return hi, lo


def _dot_f32(a, b, dims=NN):
    return _dot(a.astype(BF16), b.astype(BF16), dims)


def _sigmoid(x):
    return 1.0 / (1.0 + jnp.exp(-x))


def _silu_parts(x):
    s = _sigmoid(x)
    return x * s, s * (1.0 + x * (1.0 - s))


_GELU_C = math.sqrt(2.0 / math.pi)


def _gelu_parts(x):
    x2 = x * x
    t = jnp.tanh(_GELU_C * (x + 0.044715 * x * x2))
    val = 0.5 * x * (1.0 + t)
    der = 0.5 * (1.0 + t) + 0.5 * x * (1.0 - t * t) * _GELU_C * (1.0 + 3.0 * 0.044715 * x2)
    return val, der


class _View:
    def __init__(self, buf, shape, locate, row_tile, col_tile):
        self.buf, self.shape, self.locate, self.row_tile, self.col_tile = buf, shape, locate, row_tile, col_tile

    def spec(self, t0, t1, block_of):
        def index(i, j, k):
            rb, cb = block_of(i, j, k)
            slab, r, c = self.locate(rb * t0, cb * t1)
            return slab, r // t0, c // t1
        return pl.BlockSpec((None, t0, t1), index)


def _operand(x):
    return (x.buf, x.shape, x.row_tile, x.col_tile) if isinstance(x, _View) else (x, x.shape, x.shape[0], x.shape[1])


def _matmul(a, b, *, name, ta=False, tb=False, residual=None, out_dtype=F32, out=None, rider=None):
    a_arr, a_shape, a_rt, a_ct = _operand(a)
    b_arr, b_shape, b_rt, b_ct = _operand(b)
    (K, M) = a_shape if ta else a_shape[::-1]
    N = b_shape[0] if tb else b_shape[1]
    a_mt, a_kt = (a_ct, a_rt) if ta else (a_rt, a_ct)
    b_nt, b_kt = (b_rt, b_ct) if tb else (b_ct, b_rt)
    k_cap = MATMUL_PANEL_BYTES // (MATMUL_TILE * max(a_arr.dtype.itemsize, b_arr.dtype.itemsize))
    tm, tn, tk = min(M, MATMUL_TILE, a_mt), min(N, MATMUL_TILE, b_nt), min(K, k_cap, a_kt)
    if out is not None:
        tm, tn = min(tm, out.row_tile), min(tn, out.col_tile)
    pk = min(tk, b_kt)
    pieces = tk // pk
    nk = K // tk
    dims = ((((0,) if ta else (1,)), ((1,) if tb else (0,))), ((), ()))
    n_in = 1 + pieces + (residual is not None) + (out is not None)

    def body(*refs):
        a_ref, b_refs = refs[0], refs[1:1 + pieces]
        r_ref = refs[1 + pieces] if residual is not None else None
        o_ref = refs[n_in]

        def finish(r):
            if residual is not None:
                r = r + r_ref[...].astype(F32)
            o_ref[...] = r.astype(out_dtype)

        part = None
        for p, b_ref in enumerate(b_refs):
            ks = slice(p * pk, (p + 1) * pk)
            a_blk = a_ref[...] if pieces == 1 else (a_ref[ks, :] if ta else a_ref[:, ks])
            term = _dot(a_blk.astype(BF16), b_ref[...].astype(BF16), dims)
            part = term if part is None else part + term
        if nk == 1:
            finish(part)
            return
        acc = refs[n_in + 1]
        k = pl.program_id(2)

        @pl.when(k == 0)
        def _():
            acc[...] = part

        @pl.when(k > 0)
        def _():
            acc[...] += part

        @pl.when(k == nk - 1)
        def _():
            finish(acc[...])

    def spec(x, t0, t1, block_of):
        if isinstance(x, _View):
            return x.spec(t0, t1, block_of)
        return pl.BlockSpec((t0, t1), block_of)

    a_spec = spec(a, tk, tm, lambda i, j, k: (k, i)) if ta else spec(a, tm, tk, lambda i, j, k: (i, k))
    b_specs = [spec(b, tn, pk, lambda i, j, k, p=p: (j, k * pieces + p)) if tb else
               spec(b, pk, tn, lambda i, j, k, p=p: (k * pieces + p, j)) for p in range(pieces)]
    in_specs = [a_spec] + b_specs
    args = [a_arr] + [b_arr] * pieces
    if residual is not None:
        in_specs.append(pl.BlockSpec((tm, tn), lambda i, j, k: (i, j)))
        args.append(residual)
    aliases = {}
    if out is None:
        out_spec = pl.BlockSpec((tm, tn), lambda i, j, k: (i, j))
        out_shape = jax.ShapeDtypeStruct((M, N), out_dtype)
    else:
        out_spec = out.spec(tm, tn, lambda i, j, k: (i, j))
        out_shape = jax.ShapeDtypeStruct(out.buf.shape, out.buf.dtype)
        out_dtype = out.buf.dtype
        in_specs.append(pl.BlockSpec(memory_space=pl.ANY))
        args.append(out.buf)
        aliases = {len(args) - 1: 0}
    grid = (M // tm, N // tn, nk)
    scratch = [pltpu.VMEM((tm, tn), F32)] if nk > 1 else []
    if rider is None:
        return pl.pallas_call(
            body, name=name, grid=grid, in_specs=in_specs, out_specs=out_spec, out_shape=out_shape,
            scratch_shapes=scratch, input_output_aliases=aliases,
            compiler_params=_params("parallel", "parallel", "arbitrary"),
        )(*args)
    body, in_specs, out_specs, out_shape, scratch, extra = _ride(rider, body, grid, in_specs, [out_spec], [out_shape],
                                                                 scratch)
    return pl.pallas_call(
        body, name=name, grid=grid, in_specs=in_specs, out_specs=out_specs, out_shape=out_shape,
        scratch_shapes=scratch, input_output_aliases=aliases,
        compiler_params=_params("arbitrary", "arbitrary", "arbitrary"),
    )(*args, *extra)


def _row_tile(T, C):
    tr = max(SUBLANES, min(T, ROW_BLOCK_ELEMS // C) // SUBLANES * SUBLANES)
    while T % tr:
        tr -= SUBLANES
    return tr


def _rows_call(body, name, T, C, row_ins, full_ins, row_outs, acc_outs=()):
    tr = _row_tile(T, C)
    row_spec = pl.BlockSpec((tr, C), lambda i: (i, 0))
    in_specs = [row_spec] * len(row_ins) + [pl.BlockSpec(f.shape, lambda i, n=f.ndim: (0,) * n) for f in full_ins]
    out_specs = [row_spec] * len(row_outs) + [pl.BlockSpec(s, lambda i, n=len(s): (0,) * n) for s in acc_outs]
    out_shape = [jax.ShapeDtypeStruct((T, C), d) for d in row_outs] + [jax.ShapeDtypeStruct(s, F32) for s in acc_outs]
    return pl.pallas_call(
        body, name=name, grid=(T // tr,), in_specs=in_specs, out_specs=out_specs, out_shape=out_shape,
        compiler_params=_params("arbitrary" if acc_outs else "parallel"),
    )(*row_ins, *full_ins)


def _rmsnorm_fwd(x, g, name):
    T, C = x.shape

    def body(x_ref, g_ref, h_ref):
        xv = x_ref[...]
        r = lax.rsqrt(jnp.mean(xv * xv, axis=-1, keepdims=True) + RMS_EPS)
        h_ref[...] = ((xv * r) * g_ref[...]).astype(BF16)

    return _rows_call(body, name, T, C, [x], [g.reshape(1, C)], [BF16])[0]


def _rmsnorm_bwd(x, g, dh, dres, name):
    T, C = x.shape

    def body(x_ref, dh_ref, dres_ref, g_ref, dx_ref, dxb_ref, dg_ref):
        @pl.when(pl.program_id(0) == 0)
        def _():
            dg_ref[...] = jnp.zeros_like(dg_ref)

        xv = x_ref[...]
        dhv = dh_ref[...]
        r = lax.rsqrt(jnp.mean(xv * xv, axis=-1, keepdims=True) + RMS_EPS)
        xn = xv * r
        dg_ref[...] += jnp.sum(dhv * xn, axis=0, keepdims=True)
        dxn = dhv * g_ref[...]
        dx = dres_ref[...] + r * (dxn - xn * jnp.mean(dxn * xn, axis=-1, keepdims=True))
        dx_ref[...] = dx
        dxb_ref[...] = dx.astype(BF16)

    return _rows_call(body, name, T, C, [x, dh, dres], [g.reshape(1, C)], [F32, BF16], [(1, C)])


def _heads(C):
    return [slice(h * HEAD_DIM, (h + 1) * HEAD_DIM) for h in range(C // HEAD_DIM)]


def _qknorm_fwd(q, g, name):
    T, C = q.shape

    def body(q_ref, g_ref, o_ref):
        for head in _heads(C):
            xv = q_ref[:, head]
            r = lax.rsqrt(jnp.mean(xv * xv, axis=-1, keepdims=True) + RMS_EPS)
            o_ref[:, head] = ((xv * r) * g_ref[...]).astype(BF16)

    return _rows_call(body, name, T, C, [q], [g.reshape(1, HEAD_DIM)], [BF16])[0]


def _qknorm_bwd(q, g, dqn, name):
    T, C = q.shape

    def body(q_ref, d_ref, g_ref, dq_ref, dg_ref):
        @pl.when(pl.program_id(0) == 0)
        def _():
            dg_ref[...] = jnp.zeros_like(dg_ref)

        dg = jnp.zeros((1, HEAD_DIM), F32)
        for head in _heads(C):
            xv = q_ref[:, head]
            dv = d_ref[:, head]
            r = lax.rsqrt(jnp.mean(xv * xv, axis=-1, keepdims=True) + RMS_EPS)
            xn = xv * r
            dg = dg + jnp.sum(dv * xn, axis=0, keepdims=True)
            dxn = dv * g_ref[...]
            dq_ref[:, head] = (r * (dxn - xn * jnp.mean(dxn * xn, axis=-1, keepdims=True))).astype(BF16)
        dg_ref[...] += dg

    return _rows_call(body, name, T, C, [q, dqn], [g.reshape(1, HEAD_DIM)], [BF16], [(1, HEAD_DIM)])


ATT_TQ = 512
ATT_TK = 256
ATT_HEADS = 2


def _logsig_pair(z):
    a = jnp.minimum(z, 0.0) - jnp.log(1.0 + jnp.exp(-jnp.abs(z)))
    return a, a - z


def _tri(n, strict_upper_src):
    j = lax.broadcasted_iota(jnp.int32, (n, n), 0)
    s = lax.broadcasted_iota(jnp.int32, (n, n), 1)
    if strict_upper_src == "gt":
        m = j > s
    elif strict_upper_src == "le":
        m = j <= s
    else:
        m = j < s
    return jnp.where(m, 1.0, 0.0).astype(BF16)


def _cumdot(x, tri):
    hi, lo = _split(x)
    return _dot(hi, tri) + _dot(lo, tri)


def _attn_tiles(S):
    tq, tk = min(ATT_TQ, S), min(ATT_TK, S)
    return tq, tk, S // tq, tq // tk


def _attn_fwd(qn, kn, v, gate, Bl, S, rider=None):
    T, C = qn.shape
    H = C // HEAD_DIM
    tq, tk, nq, kpq = _attn_tiles(S)
    hp = min(ATT_HEADS, H)
    scale = 1.0 / math.sqrt(HEAD_DIM)

    def body(q_ref, k_ref, v_ref, g_ref, o_ref, bt_ref, og_ref):
        i = pl.program_id(2)
        tri = _tri(tk, "gt")
        rowpos = lax.broadcasted_iota(jnp.int32, (tq, tk), 0) + i * tq
        colpos = lax.broadcasted_iota(jnp.int32, (tq, tk), 1)
        o_ref[...] = jnp.zeros_like(o_ref)
        bt_ref[...] = jnp.zeros_like(bt_ref)

        def group(g, masked):
            work = []
            for p in reversed(range(kpq)):
                j = g * kpq + p
                rows = pl.ds(pl.multiple_of(j * tk, tk), tk)
                mask = (colpos + j * tk) < rowpos if masked else None
                for head in _heads(hp * HEAD_DIM):
                    a, b = _logsig_pair(_dot(q_ref[:, head], k_ref[rows, head], NT) * scale)
                    if masked:
                        b = jnp.where(mask, b, 0.0)
                    work.append((head, rows, a, b, _cumdot(b, tri), mask))
            for n, head in enumerate(_heads(hp * HEAD_DIM)):
                total = bt_ref[n]
                out = None
                for rows, a, b, suffix, mask in [w[1:] for w in work if w[0] == head]:
                    w = jnp.exp(a + suffix + total)
                    if masked:
                        w = jnp.where(mask, w, 0.0)
                    term = _dot(w.astype(BF16), v_ref[rows, head])
                    out = term if out is None else out + term
                    total = total + jnp.sum(b, axis=-1, keepdims=True)
                o_ref[:, head] += out
                bt_ref[n] = total

        group(i, True)

        def step(n, carry):
            group(i - 1 - n, False)
            return carry

        lax.fori_loop(0, i, step, 0)
        og_ref[...] = (o_ref[...] * _silu_parts(g_ref[...])[0]).astype(BF16)

    qspec = pl.BlockSpec((tq, hp * HEAD_DIM), lambda b, h, i: (b * nq + i, h))
    kspec = pl.BlockSpec((S, hp * HEAD_DIM), lambda b, h, i: (b, h))
    btspec = pl.BlockSpec((None, hp, tq, 1), lambda b, h, i: (b, h, i, 0))
    grid = (Bl, H // hp, nq)
    body, in_specs, out_specs, out_shape, scratch, extra = _ride(
        rider, body, grid, [qspec, kspec, kspec, qspec], [qspec, btspec, qspec],
        [jax.ShapeDtypeStruct((T, C), F32), jax.ShapeDtypeStruct((Bl, H, S, 1), F32),
         jax.ShapeDtypeStruct((T, C), BF16)], [])
    return pl.pallas_call(
        body, name="attn_fwd", grid=grid, in_specs=in_specs, out_specs=out_specs, out_shape=out_shape,
        scratch_shapes=scratch, compiler_params=_params("arbitrary", "arbitrary", "arbitrary"),
    )(qn, kn, v, gate, *extra)


def _attn_bwd(qn, kn, v, do, btot, Bl, S, rider=None):
    T, C = qn.shape
    H = C // HEAD_DIM
    tq, tk, nq, kpq = _attn_tiles(S)
    hp = min(ATT_HEADS, H)
    scale = 1.0 / math.sqrt(HEAD_DIM)

    def body(q_ref, k_ref, v_ref, do_ref, bt_ref, dq_ref, dk_ref, dv_ref, pb_ref, pdl_ref):
        i = pl.program_id(2)

        @pl.when(i == 0)
        def _():
            dk_ref[...] = jnp.zeros_like(dk_ref)
            dv_ref[...] = jnp.zeros_like(dv_ref)

        tri_le = _tri(tk, "le")
        tri_lt = _tri(tk, "lt")
        rowpos = lax.broadcasted_iota(jnp.int32, (tq, tk), 0) + i * tq
        colpos = lax.broadcasted_iota(jnp.int32, (tq, tk), 1)
        dq_ref[...] = jnp.zeros_like(dq_ref)
        pb_ref[...] = bt_ref[...]
        pdl_ref[...] = jnp.zeros_like(pdl_ref)

        def group(g, masked):
            work = []
            for p in range(kpq):
                j = g * kpq + p
                rows = pl.ds(pl.multiple_of(j * tk, tk), tk)
                mask = (colpos + j * tk) < rowpos if masked else None
                for head in _heads(hp * HEAD_DIM):
                    a, b = _logsig_pair(_dot(q_ref[:, head], k_ref[rows, head], NT) * scale)
                    if masked:
                        b = jnp.where(mask, b, 0.0)
                    work.append((head, rows, a, b, _cumdot(b, tri_le), _dot(do_ref[:, head], v_ref[rows, head], NT),
                                 mask))
            for n, head in enumerate(_heads(hp * HEAD_DIM)):
                remaining = pb_ref[n]
                swept = pdl_ref[n]
                dq = None
                for rows, a, b, cum, dw, mask in [w[1:] for w in work if w[0] == head]:
                    w = jnp.exp(a + (remaining - cum))
                    if masked:
                        w = jnp.where(mask, w, 0.0)
                    dl = dw * w
                    prefix = swept + _cumdot(dl, tri_lt)
                    beta = jnp.exp(a)
                    dz = dl * (1.0 - beta) - beta * prefix
                    if masked:
                        dz = jnp.where(mask, dz, 0.0)
                    dzb = (dz * scale).astype(BF16)
                    term = _dot(dzb, k_ref[rows, head])
                    dq = term if dq is None else dq + term
                    dk_ref[rows, head] += _dot(dzb, q_ref[:, head], TN)
                    dv_ref[rows, head] += _dot(w.astype(BF16), do_ref[:, head], TN)
                    remaining = remaining - jnp.sum(b, axis=-1, keepdims=True)
                    swept = swept + jnp.sum(dl, axis=-1, keepdims=True)
                dq_ref[:, head] += dq
                pb_ref[n] = remaining
                pdl_ref[n] = swept

        def step(g, carry):
            group(g, False)
            return carry

        lax.fori_loop(0, i, step, 0)
        group(i, True)

    qspec = pl.BlockSpec((tq, hp * HEAD_DIM), lambda b, h, i: (b * nq + i, h))
    kspec = pl.BlockSpec((S, hp * HEAD_DIM), lambda b, h, i: (b, h))
    btspec = pl.BlockSpec((None, hp, tq, 1), lambda b, h, i: (b, h, i, 0))
    grid = (Bl, H // hp, nq)
    body, in_specs, out_specs, out_shape, scratch, extra = _ride(
        rider, body, grid, [qspec, kspec, kspec, qspec, btspec], [qspec, kspec, kspec],
        [jax.ShapeDtypeStruct((T, C), F32)] * 3, [pltpu.VMEM((hp, tq, 1), F32), pltpu.VMEM((hp, tq, 1), F32)])
    return pl.pallas_call(
        body, name="attn_bwd", grid=grid, in_specs=in_specs, out_specs=out_specs, out_shape=out_shape,
        scratch_shapes=scratch, compiler_params=_params("arbitrary", "arbitrary", "arbitrary"),
    )(qn, kn, v, do, btot, *extra)


SSM_TIME_BLOCK = 512
CHUNK = SUBLANES


def _cmadd(xr, xi, ar, ai, sr, si):
    return xr + ar * sr - ai * si, xi + ar * si + ai * sr


def _chunk_scan(xr, xi, tab_ref, cr, ci, reverse):
    for lvl, d in enumerate((1, 2, 4)):
        shift = (CHUNK - d) if reverse else d
        sr = pltpu.roll(xr, shift, 0)
        si = pltpu.roll(xi, shift, 0)
        ar = tab_ref[pl.ds((2 * lvl) * CHUNK, CHUNK), :]
        ai = tab_ref[pl.ds((2 * lvl + 1) * CHUNK, CHUNK), :]
        xr, xi = _cmadd(xr, xi, ar, ai, sr, si)
    pr = tab_ref[pl.ds(6 * CHUNK, CHUNK), :]
    pi = tab_ref[pl.ds(7 * CHUNK, CHUNK), :]
    return _cmadd(xr, xi, pr, pi, cr, ci)


def _ssm_dims(S, C):
    G = C // GROUP
    GT = min(16, G)
    return G, GT, G // GT, GT * GROUP, GT * STATE, min(SSM_TIME_BLOCK, S)


def _ssm_fwd(u, mats, Bl, S):
    T, C = u.shape
    G, GT, ngt, cw, sw, TB = _ssm_dims(S, C)
    ntb = S // TB
    nch = TB // CHUNK

    def body(u_ref, bre_ref, bim_ref, cre_ref, cim_ref, d_ref, tab_ref, y_ref, hr_ref, hi_ref, yg_ref, car_r, car_i):
        @pl.when(pl.program_id(2) == 0)
        def _():
            car_r[...] = jnp.zeros_like(car_r)
            car_i[...] = jnp.zeros_like(car_i)

        uv = u_ref[...]
        hr_ref[...] = _dot_f32(uv, bre_ref[...])
        hi_ref[...] = _dot_f32(uv, bim_ref[...])

        def step(n, carry):
            cr, ci = carry
            rows = pl.ds(pl.multiple_of(n * CHUNK, CHUNK), CHUNK)
            xr, xi = _chunk_scan(hr_ref[rows, :], hi_ref[rows, :], tab_ref, cr, ci, False)
            hr_ref[rows, :] = xr
            hi_ref[rows, :] = xi
            last = (CHUNK - 1, CHUNK)
            return (jnp.broadcast_to(xr[last[0]:last[1], :], xr.shape), jnp.broadcast_to(xi[last[0]:last[1], :], xi.shape))

        cr, ci = lax.fori_loop(0, nch, step, (car_r[...], car_i[...]))
        car_r[...] = cr
        car_i[...] = ci
        y = _dot_f32(hr_ref[...], cre_ref[...]) - _dot_f32(hi_ref[...], cim_ref[...]) + d_ref[...] * uv
        y_ref[...] = y
        yg_ref[...] = _gelu_parts(y)[0].astype(BF16)

    uspec = pl.BlockSpec((TB, cw), lambda g, b, t: (b * ntb + t, g))
    hspec = pl.BlockSpec((TB, sw), lambda g, b, t: (b * ntb + t, g))

    def gspec(r, c):
        return pl.BlockSpec((None, r, c), lambda g, b, t: (g, 0, 0))

    return pl.pallas_call(
        body, name="ssm_fwd", grid=(ngt, Bl, ntb),
        in_specs=[uspec, gspec(cw, sw), gspec(cw, sw), gspec(sw, cw), gspec(sw, cw), gspec(1, cw), gspec(8 * CHUNK, sw)],
        out_specs=[uspec, hspec, hspec, uspec],
        out_shape=[jax.ShapeDtypeStruct((T, C), F32), jax.ShapeDtypeStruct((T, G * STATE), F32),
                   jax.ShapeDtypeStruct((T, G * STATE), F32), jax.ShapeDtypeStruct((T, C), BF16)],
        scratch_shapes=[pltpu.VMEM((CHUNK, sw), F32), pltpu.VMEM((CHUNK, sw), F32)],
        compiler_params=_params("parallel", "arbitrary", "arbitrary"),
    )(u, mats["bbd_re"], mats["bbd_im"], mats["cbd_re"], mats["cbd_im"], mats["d"], mats["tab_fwd"])


def _ssm_bwd(u, dy, h_re, h_im, mats, Bl, S):
    T, C = u.shape
    G, GT, ngt, cw, sw, TB = _ssm_dims(S, C)
    ntb = S // TB
    nch = TB // CHUNK
    rpb = TB // CHUNK

    def body(u_ref, dy_ref, hr_ref, hi_ref, hpr_ref, hpi_ref, ctr_ref, cti_ref, btr_ref, bti_ref, d_ref, tab_ref,
             du_ref, dbr_ref, dbi_ref, dcr_ref, dci_ref, dlr_ref, dli_ref, dd_ref, gr_ref, gi_ref, car_r, car_i):
        b = pl.program_id(1)
        t = pl.program_id(2)

        @pl.when((b == 0) & (t == 0))
        def _():
            for ref in (dbr_ref, dbi_ref, dcr_ref, dci_ref, dlr_ref, dli_ref, dd_ref):
                ref[...] = jnp.zeros_like(ref)

        @pl.when(t == 0)
        def _():
            car_r[...] = jnp.zeros_like(car_r)
            car_i[...] = jnp.zeros_like(car_i)

        uv = u_ref[...]
        dyv = dy_ref[...]
        gr_ref[...] = _dot_f32(dyv, ctr_ref[...])
        gi_ref[...] = -_dot_f32(dyv, cti_ref[...])
        alive = jnp.where(t == ntb - 1, 0.0, 1.0)
        row0 = lax.broadcasted_iota(jnp.int32, (CHUNK, sw), 0) == 0

        def step(m, carry):
            cr, ci, ar, ai = carry
            n = nch - 1 - m
            rows = pl.ds(pl.multiple_of(n * CHUNK, CHUNK), CHUNK)
            prow = pl.ds(pl.multiple_of(jnp.maximum(n - 1, 0) * CHUNK, CHUNK), CHUNK)
            xr, xi = _chunk_scan(gr_ref[rows, :], gi_ref[rows, :], tab_ref, cr, ci, True)
            gr_ref[rows, :] = xr
            gi_ref[rows, :] = xi
            first = n == 0
            pr = jnp.where(first, hpr_ref[...] * alive, hr_ref[prow, :])
            pi = jnp.where(first, hpi_ref[...] * alive, hi_ref[prow, :])
            sr = jnp.where(row0, pltpu.roll(pr, 1, 0), pltpu.roll(hr_ref[rows, :], 1, 0))
            si = jnp.where(row0, pltpu.roll(pi, 1, 0), pltpu.roll(hi_ref[rows, :], 1, 0))
            ar = ar + xr * sr + xi * si
            ai = ai + xi * sr - xr * si
            return (jnp.broadcast_to(xr[0:1, :], xr.shape), jnp.broadcast_to(xi[0:1, :], xi.shape), ar, ai)

        zero = jnp.zeros((CHUNK, sw), F32)
        cr, ci, ar, ai = lax.fori_loop(0, nch, step, (car_r[...], car_i[...], zero, zero))
        car_r[...] = cr
        car_i[...] = ci
        dlr_ref[...] += ar
        dli_ref[...] += ai
        gr = gr_ref[...]
        gi = gi_ref[...]
        dbr_ref[...] += _dot_f32(uv, gr, TN)
        dbi_ref[...] += _dot_f32(uv, gi, TN)
        dcr_ref[...] += _dot_f32(hr_ref[...], dyv, TN)
        dci_ref[...] -= _dot_f32(hi_ref[...], dyv, TN)
        dd_ref[...] += jnp.sum(dyv * uv, axis=0, keepdims=True)
        du_ref[...] = (_dot_f32(gr, btr_ref[...]) + _dot_f32(gi, bti_ref[...]) + d_ref[...] * dyv).astype(BF16)

    def tblk(b, t):
        return b * ntb + (ntb - 1 - t)

    uspec = pl.BlockSpec((TB, cw), lambda g, b, t: (tblk(b, t), g))
    hspec = pl.BlockSpec((TB, sw), lambda g, b, t: (tblk(b, t), g))
    hpspec = pl.BlockSpec((CHUNK, sw), lambda g, b, t: (jnp.maximum(tblk(b, t) * rpb - 1, 0), g))

    def gspec(r, c):
        return pl.BlockSpec((None, r, c), lambda g, b, t: (g, 0, 0))

    def gshape(r, c):
        return jax.ShapeDtypeStruct((ngt, r, c), F32)

    return pl.pallas_call(
        body, name="ssm_bwd", grid=(ngt, Bl, ntb),
        in_specs=[uspec, uspec, hspec, hspec, hpspec, hpspec, gspec(cw, sw), gspec(cw, sw), gspec(sw, cw), gspec(sw, cw),
                  gspec(1, cw), gspec(8 * CHUNK, sw)],
        out_specs=[uspec, gspec(cw, sw), gspec(cw, sw), gspec(sw, cw), gspec(sw, cw), gspec(CHUNK, sw), gspec(CHUNK, sw),
                   gspec(1, cw)],
        out_shape=[jax.ShapeDtypeStruct((T, C), BF16), gshape(cw, sw), gshape(cw, sw), gshape(sw, cw), gshape(sw, cw),
                   gshape(CHUNK, sw), gshape(CHUNK, sw), gshape(1, cw)],
        scratch_shapes=[pltpu.VMEM((TB, sw), F32), pltpu.VMEM((TB, sw), F32), pltpu.VMEM((CHUNK, sw), F32),
                        pltpu.VMEM((CHUNK, sw), F32)],
        compiler_params=_params("arbitrary", "arbitrary", "arbitrary"),
    )(u, dy, h_re, h_im, h_re, h_im, mats["cbdT_re"], mats["cbdT_im"], mats["bbdT_re"], mats["bbdT_im"], mats["d"],
      mats["tab_rev"])


def _zoh(a_re, a_im, log_dt, b_re, b_im):
    dt = jnp.exp(log_dt)[:, None]
    mag = jnp.exp(a_re * dt)
    l_re = mag * jnp.cos(a_im * dt)
    l_im = mag * jnp.sin(a_im * dt)
    den = a_re * a_re + a_im * a_im
    f_re = ((l_re - 1.0) * a_re + l_im * a_im) / den
    f_im = (l_im * a_re - (l_re - 1.0) * a_im) / den
    bb_re = f_re[..., None] * b_re - f_im[..., None] * b_im
    bb_im = f_re[..., None] * b_im + f_im[..., None] * b_re
    return l_re, l_im, bb_re, bb_im


def _ssm_matrices(a_re, a_im, log_dt, b_re, b_im, c_re, c_im, d, S):
    G = a_re.shape[0]
    _, GT, ngt, cw, sw, _ = _ssm_dims(S, G * GROUP)
    _, _, bb_re, bb_im = _zoh(a_re, a_im, log_dt, b_re, b_im)
    eye = jnp.eye(GT, dtype=F32)

    def bd_b(bb):
        return jnp.einsum("tgpi,gh->tgihp", bb.reshape(ngt, GT, STATE, GROUP), eye).reshape(ngt, cw, sw)

    def bd_c(c):
        return jnp.einsum("tgip,gh->tgphi", c.reshape(ngt, GT, GROUP, STATE), eye).reshape(ngt, sw, cw)

    dt = jnp.exp(log_dt)[:, None]

    def power(k, conj):
        mag = jnp.exp(k * a_re * dt)
        ang = k * a_im * dt
        return (mag * jnp.cos(ang)).reshape(ngt, 1, sw), ((-1.0 if conj else 1.0) * mag * jnp.sin(ang)).reshape(ngt, 1, sw)

    r = jnp.arange(CHUNK)[None, :, None]

    def table(reverse):
        parts = []
        for dd in (1, 2, 4):
            pr, pi = power(float(dd), reverse)
            keep = (r <= CHUNK - 1 - dd) if reverse else (r >= dd)
            parts += [jnp.where(keep, pr, 0.0), jnp.where(keep, pi, 0.0)]
        exps = [(CHUNK - k) if reverse else (k + 1) for k in range(CHUNK)]
        pw = [power(float(e), reverse) for e in exps]
        parts += [jnp.concatenate([p[0] for p in pw], axis=1), jnp.concatenate([p[1] for p in pw], axis=1)]
        return jnp.concatenate([jnp.broadcast_to(p, (ngt, CHUNK, sw)) for p in parts], axis=1)

    mats = dict(bbd_re=bd_b(bb_re).astype(BF16), bbd_im=bd_b(bb_im).astype(BF16), cbd_re=bd_c(c_re).astype(BF16),
                cbd_im=bd_c(c_im).astype(BF16), d=d.reshape(ngt, 1, cw), tab_fwd=table(False), tab_rev=table(True))
    for k in ("bbd_re", "bbd_im", "cbd_re", "cbd_im"):
        mats[k.replace("bd_", "bdT_")] = jnp.swapaxes(mats[k], 1, 2)
    return mats


def _ssm_unblock(dbr, dbi, dcr, dci, dlr, dli, dd, G):
    ngt = dbr.shape[0]
    GT = G // ngt
    eye = jnp.eye(GT, dtype=F32)

    def ub(x):
        return jnp.einsum("tgihp,gh->tgpi", x.reshape(ngt, GT, GROUP, GT, STATE), eye).reshape(G, STATE, GROUP)

    def uc(x):
        return jnp.einsum("tgphi,gh->tgip", x.reshape(ngt, GT, STATE, GT, GROUP), eye).reshape(G, GROUP, STATE)

    return (dlr.sum(axis=1).reshape(G, STATE), dli.sum(axis=1).reshape(G, STATE), ub(dbr), ub(dbi), uc(dcr), uc(dci),
            dd.reshape(G * GROUP))


def _attn_gate_bwd(dog, o, gate):
    T, C = o.shape

    def body(d_ref, o_ref, g_ref, do_ref, dg_ref):
        val, der = _silu_parts(g_ref[...])
        dv = d_ref[...]
        do_ref[...] = (dv * val).astype(BF16)
        dg_ref[...] = (dv * o_ref[...] * der).astype(BF16)

    return _rows_call(body, "attn_gate_bwd", T, C, [dog, o, gate], [], [BF16, BF16])


def _glu_fwd(y, gl, gate, glu_b):
    T, C = y.shape

    def body(y_ref, gl_ref, g_ref, b_ref, out_ref):
        yg = _gelu_parts(y_ref[...])[0]
        sg = _sigmoid(gl_ref[...] + b_ref[...])
        out_ref[...] = (yg * sg * _silu_parts(g_ref[...])[0]).astype(BF16)

    return _rows_call(body, "glu_fwd", T, C, [y, gl, gate], [glu_b.reshape(1, C)], [BF16])[0]


def _glu_bwd(dy3, y, gl, gate, glu_b):
    T, C = y.shape

    def body(d_ref, y_ref, gl_ref, g_ref, b_ref, dgl_ref, dgate_ref, t1_ref, db_ref):
        @pl.when(pl.program_id(0) == 0)
        def _():
            db_ref[...] = jnp.zeros_like(db_ref)

        yg = _gelu_parts(y_ref[...])[0]
        sg = _sigmoid(gl_ref[...] + b_ref[...])
        sl, sld = _silu_parts(g_ref[...])
        dv = d_ref[...]
        dy2 = dv * sl
        dgl = dy2 * yg * sg * (1.0 - sg)
        dgl_ref[...] = dgl.astype(BF16)
        dgate_ref[...] = (dv * (yg * sg) * sld).astype(BF16)
        t1_ref[...] = dy2 * sg
        db_ref[...] += jnp.sum(dgl, axis=0, keepdims=True)

    return _rows_call(body, "glu_bwd", T, C, [dy3, y, gl, gate], [glu_b.reshape(1, C)], [BF16, BF16, F32], [(1, C)])


def _gelu_bwd(t1, t2, y):
    T, C = y.shape

    def body(a_ref, b_ref, y_ref, out_ref):
        out_ref[...] = (a_ref[...] + b_ref[...]) * _gelu_parts(y_ref[...])[1]

    return _rows_call(body, "gelu_bwd", T, C, [t1, t2, y], [], [F32])[0]


def _loss_head(x2, target):
    T, C = x2.shape

    def body(x_ref, t_ref, d_ref, db_ref, l_ref):
        @pl.when(pl.program_id(0) == 0)
        def _():
            l_ref[...] = jnp.zeros_like(l_ref)

        e = x_ref[...] - t_ref[...]
        d = e * (1.0 / C)
        d_ref[...] = d
        db_ref[...] = d.astype(BF16)
        l_ref[...] += 0.5 * jnp.sum(jnp.sum(e * e, axis=-1, keepdims=True) * (1.0 / C), axis=0, keepdims=True)

    return _rows_call(body, "loss_head", T, C, [x2, target], [], [F32, BF16], [(1, 1)])


def _adamw(w, g, m, v, name):
    shape = w.shape
    C = shape[-1]
    R = w.size // C
    bc1 = 1.0 - ADAM_B1 ** ADAM_STEP
    bc2 = 1.0 - ADAM_B2 ** ADAM_STEP

    def body(w_ref, g_ref, m_ref, v_ref, d_ref, nm_ref, nv_ref):
        gv = g_ref[...]
        mn = ADAM_B1 * m_ref[...] + (1.0 - ADAM_B1) * gv
        vn = ADAM_B2 * v_ref[...] + (1.0 - ADAM_B2) * (gv * gv)
        d_ref[...] = -ADAM_LR * ((mn / bc1) / (jnp.sqrt(vn / bc2) + ADAM_EPS) + ADAM_WD * w_ref[...])
        nm_ref[...] = mn
        nv_ref[...] = vn

    outs = _rows_call(body, name, R, C, [a.reshape(R, C) for a in (w, g, m, v)], [], [F32, F32, F32])
    return [o.reshape(shape) for o in outs]


def _sum_leading(x, name):
    n, R, C = x.shape
    tr = _row_tile(R, C * n)

    def body(x_ref, o_ref):
        acc = x_ref[0].astype(F32)
        for k in range(1, n):
            acc = acc + x_ref[k].astype(F32)
        o_ref[...] = acc

    return pl.pallas_call(
        body, name=name, grid=(R // tr,), in_specs=[pl.BlockSpec((n, tr, C), lambda i: (0, i, 0))],
        out_specs=pl.BlockSpec((tr, C), lambda i: (i, 0)), out_shape=jax.ShapeDtypeStruct((R, C), F32),
        compiler_params=_params("parallel"),
    )(x)


def _add_halves(g, c, name):
    full, recv = g
    n, R, C = full.shape
    half = R // 2
    tr = _row_tile(half, C)
    nb = half // tr

    def body(c_ref, a_ref, b_ref, o_ref):
        o_ref[...] = (a_ref[...] + b_ref[...]).astype(BF16)

    grid_spec = pltpu.PrefetchScalarGridSpec(
        num_scalar_prefetch=1, grid=(n, nb),
        in_specs=[pl.BlockSpec((None, tr, C), lambda j, i, c_ref: (j, c_ref[0] * nb + i, 0)),
                  pl.BlockSpec((None, tr, C), lambda j, i, c_ref: (j, i, 0))],
        out_specs=pl.BlockSpec((None, tr, C), lambda j, i, c_ref: (j, i, 0)))
    return pl.pallas_call(
        body, name=name, grid_spec=grid_spec, out_shape=jax.ShapeDtypeStruct((n, half, C), BF16),
        compiler_params=_params("parallel", "parallel"),
    )(c.reshape(1).astype(jnp.int32), full, recv)


ANY = pl.BlockSpec(memory_space=pl.ANY)


def _position():
    return lax.axis_index("x"), lax.axis_index("y"), lax.axis_index("c")


def _all_gather8(blk, name):
    return _standalone(_all_gather_rider(blk), name)[0]


class _Rider:
    def __init__(self, arrays, out_shapes, sems, start, finish):
        self.arrays, self.out_shapes, self.sems, self.start, self.finish = arrays, out_shapes, sems, start, finish


def _join_riders(*riders):
    def split(refs):
        ins, outs, sems = [], [], []
        for r in riders:
            ins.append(refs[:len(r.arrays)])
            refs = refs[len(r.arrays):]
        for r in riders:
            outs.append(refs[:len(r.out_shapes)])
            refs = refs[len(r.out_shapes):]
        for r in riders:
            sems.append(refs[:len(r.sems)])
            refs = refs[len(r.sems):]
        return [i + o + s for i, o, s in zip(ins, outs, sems)]

    def start(*refs):
        for r, own in zip(riders, split(refs)):
            r.start(*own)

    def finish(*refs):
        for r, own in zip(riders, split(refs)):
            r.finish(*own)

    return _Rider([a for r in riders for a in r.arrays], [s for r in riders for s in r.out_shapes],
                  [s for r in riders for s in r.sems], start, finish)


def _standalone(rider, name):
    def body(*refs):
        rider.start(*refs)
        rider.finish(*refs)

    return pl.pallas_call(
        body, name=name, in_specs=[ANY] * len(rider.arrays), out_specs=[ANY] * len(rider.out_shapes),
        out_shape=rider.out_shapes, scratch_shapes=rider.sems,
    )(*rider.arrays)


def _ride(rider, body, grid, in_specs, out_specs, out_shape, scratch):
    if rider is None:
        return body, in_specs, out_specs, out_shape, scratch, []
    ni, no, ns = len(in_specs), len(out_specs), len(scratch)
    ri, ro = len(rider.arrays), len(rider.out_shapes)

    def full(*refs):
        ins, refs = refs[:ni], refs[ni:]
        r_ins, refs = refs[:ri], refs[ri:]
        outs, refs = refs[:no], refs[no:]
        r_outs, refs = refs[:ro], refs[ro:]
        scr, r_sems = refs[:ns], refs[ns:]
        ids = [pl.program_id(a) for a in range(len(grid))]
        first = functools.reduce(jnp.logical_and, [i == 0 for i in ids])
        last = functools.reduce(jnp.logical_and, [i == g - 1 for i, g in zip(ids, grid)])

        @pl.when(first)
        def _():
            rider.start(*r_ins, *r_outs, *r_sems)

        body(*ins, *outs, *scr)

        @pl.when(last)
        def _():
            rider.finish(*r_ins, *r_outs, *r_sems)

    return (full, in_specs + [ANY] * ri, out_specs + [ANY] * ro, out_shape + rider.out_shapes, scratch + rider.sems,
            rider.arrays)


def _all_gather_rider(blk):
    M, N = blk.shape

    def copies(x_ref, out_ref, send_sems, recv_sems, local_sem):
        x, y, c = _position()
        me, sibling = (x, y, c), (x, y, 1 - c)
        chips = [(1 - x, y), (x, 1 - y), (1 - x, 1 - y)]

        def slab(px, py, pc):
            return out_ref.at[4 * px + 2 * py + pc]

        def copy(k, block, to, src=None):
            return pltpu.make_async_remote_copy(
                src_ref=slab(*block) if src is None else src, dst_ref=slab(*block),
                send_sem=send_sems.at[k], recv_sem=recv_sems.at[k], device_id=to, device_id_type=MESH)

        mine = pltpu.make_async_copy(x_ref, slab(*me), local_sem)
        first = [copy(0, me, sibling, src=x_ref)]
        first += [copy(1 + j, me, (*chip, c), src=x_ref) for j, chip in enumerate(chips)]
        passed = [copy(4 + j, (*chip, c), sibling) for j, chip in enumerate(chips)]
        arrivals = [copy(1 + j, (*chip, c), me) for j, chip in enumerate(chips)]
        from_sibling = [copy(0, sibling, me)] + [copy(4 + j, (*chip, 1 - c), me) for j, chip in enumerate(chips)]
        return mine, first, passed, arrivals, from_sibling

    def start(*refs):
        mine, first, _, _, _ = copies(*refs)
        mine.start()
        for cp in first:
            cp.start()

    def finish(*refs):
        mine, first, passed, arrivals, from_sibling = copies(*refs)
        for arrival, onward in zip(arrivals, passed):
            arrival.wait_recv()
            onward.start()
        for cp in from_sibling:
            cp.wait_recv()
        for cp in first + passed:
            cp.wait_send()
        mine.wait()

    return _Rider([blk], [jax.ShapeDtypeStruct((N_DEV, M, N), blk.dtype)],
                  [pltpu.SemaphoreType.DMA((7,)), pltpu.SemaphoreType.DMA((7,)), pltpu.SemaphoreType.DMA], start, finish)


def _sibling_half_rider(g):
    n, R, C = g.shape
    half = R // 2

    def copy(g_ref, out_ref, send_sem, recv_sem):
        x, y, c = _position()
        return pltpu.make_async_remote_copy(
            src_ref=g_ref.at[:, pl.ds((1 - c) * half, half), :], dst_ref=out_ref, send_sem=send_sem, recv_sem=recv_sem,
            device_id=(x, y, 1 - c), device_id_type=MESH)

    return _Rider([g], [jax.ShapeDtypeStruct((n, half, C), g.dtype)], [pltpu.SemaphoreType.DMA, pltpu.SemaphoreType.DMA],
                  lambda *refs: copy(*refs).start(), lambda *refs: copy(*refs).wait())


def _sibling_send_half(g, name):
    return _standalone(_sibling_half_rider(g), name)[0]


def _chip_exchange_rider(p, row_off=0, rows=None):
    rows = p.shape[1] if rows is None else rows

    def copies(p_ref, out_ref, send_sems, recv_sems, local_sem):
        x, y, c = _position()
        my = 2 * x + y
        chips = [(1 - x, y), (x, 1 - y), (1 - x, 1 - y)]

        def src(slab):
            return p_ref.at[slab, pl.ds(row_off, rows), :]

        mine = pltpu.make_async_copy(src(my), out_ref.at[my], local_sem)
        sends = [pltpu.make_async_remote_copy(
            src_ref=src(2 * px + py), dst_ref=out_ref.at[my], send_sem=send_sems.at[k], recv_sem=recv_sems.at[k],
            device_id=(px, py, c), device_id_type=MESH) for k, (px, py) in enumerate(chips)]
        arrivals = [pltpu.make_async_remote_copy(
            src_ref=src(my), dst_ref=out_ref.at[2 * px + py], send_sem=send_sems.at[k], recv_sem=recv_sems.at[k],
            device_id=(px, py, c), device_id_type=MESH) for k, (px, py) in enumerate(chips)]
        return mine, sends, arrivals

    def start(*refs):
        mine, sends, _ = copies(*refs)
        mine.start()
        for cp in sends:
            cp.start()

    def finish(*refs):
        mine, sends, arrivals = copies(*refs)
        for cp in arrivals:
            cp.wait_recv()
        for cp in sends:
            cp.wait_send()
        mine.wait()

    return _Rider([p], [jax.ShapeDtypeStruct((p.shape[0], rows, p.shape[2]), p.dtype)],
                  [pltpu.SemaphoreType.DMA((3,)), pltpu.SemaphoreType.DMA((3,)), pltpu.SemaphoreType.DMA], start, finish)


def _sibling_swap(f, name):
    def body(f_ref, out_ref, send_sem, recv_sem):
        x, y, c = _position()
        cp = pltpu.make_async_remote_copy(src_ref=f_ref, dst_ref=out_ref, send_sem=send_sem, recv_sem=recv_sem,
                                          device_id=(x, y, 1 - c), device_id_type=MESH)
        cp.start()
        cp.wait()

    return pl.pallas_call(
        body, name=name, in_specs=[ANY], out_specs=ANY, out_shape=jax.ShapeDtypeStruct(f.shape, f.dtype),
        scratch_shapes=[pltpu.SemaphoreType.DMA, pltpu.SemaphoreType.DMA],
    )(f)


def _pack_rest(attn_out, ssm_in, glu_w, ssm_out):
    hd = ssm_in.shape[0] // 2
    return jnp.concatenate([attn_out, jnp.concatenate([ssm_in[:hd], ssm_in[hd:]], axis=1), glu_w, ssm_out], axis=0)


def _unpack_rest(p):
    D = p.shape[-1]
    q, hd = D // N_CHIPS, D // 2
    o = [0, q, q + hd, 2 * q + hd, 3 * q + hd]
    ssm_in = p[o[1]:o[2]]
    return p[o[0]:o[1]], jnp.concatenate([ssm_in[:, :hd], ssm_in[:, hd:]], axis=0), p[o[2]:o[3]], p[o[3]:o[4]]


def _attn_in_views(buf_a):
    D = buf_a.shape[-1]
    return [_View(buf_a, (D, D), lambda r, c, j=j: (j, r, c), D, D) for j in range(N_CHIPS)]


def _rest_views(buf_b):
    D = buf_b.shape[-1]
    q, hd = D // N_CHIPS, D // 2
    o_out, o_in, o_glu, o_sout = 0, q, q + hd, 2 * q + hd

    def row_sharded(off):
        return _View(buf_b, (D, D), lambda r, c: (r // q, off + r % q, c), math.gcd(q, off), D)

    def ssm_in(part):
        return _View(buf_b, (D, D), lambda r, c: (2 * part + c // hd, o_in + r % hd, (r // hd) * hd + c % hd),
                     math.gcd(hd, o_in), hd)

    return dict(attn_out=row_sharded(o_out), ssm_in_u=ssm_in(0), ssm_in_gate=ssm_in(1), glu=row_sharded(o_glu),
                ssm_out=row_sharded(o_sout))


def _pack_small(parts):
    flat = jnp.concatenate([p.reshape(-1) for p in parts])
    pad = (-flat.size) % (2 * SUBLANES * LANES)
    return jnp.pad(flat, (0, pad)).reshape(-1, LANES)


def _unpack_small(buf, shapes):
    flat = buf.reshape(-1)
    out, off = [], 0
    for s in shapes:
        n = math.prod(s)
        out.append(flat[off:off + n].reshape(s))
        off += n
    return out


def _local_step(x, target, norm_g, q_g, k_g, wa, wb, ssm_small, core=None):
    Bl, S, D = x.shape
    T = Bl * S
    x0 = x.reshape(T, D)
    tgt = target.reshape(T, D)
    a_re, a_im, log_dt, b_re, b_im, c_re, c_im, d_skip, glu_b = ssm_small
    G = a_re.shape[0]
    mats = _ssm_matrices(a_re, a_im, log_dt, b_re, b_im, c_re, c_im, d_skip, S)
    w_in = _attn_in_views(wa)
    rows_b = wb.shape[-2] * (1 if core is None else 2)
    ga = lax.empty(wa.shape, F32)
    gb = lax.empty((N_CHIPS, rows_b, D), F32)

    def wgrad(a, b, key, name):
        return _matmul(a, b, name=name, ta=True, out=_rest_views(gb)[key])

    h0 = _rmsnorm_fwd(x0, norm_g[0], "norm0_fwd")
    q, k, v, gate = [_matmul(h0, w_in[j], name=f"attn_in_{j}", out_dtype=(BF16 if j == 2 else F32)) for j in range(4)]
    qn = _qknorm_fwd(q, q_g, "qnorm_fwd")
    kn = _qknorm_fwd(k, k_g, "knorm_fwd")
    if core is None:
        o, btot, og = _attn_fwd(qn, kn, v, gate, Bl, S)
    else:
        o, btot, og, wb = _attn_fwd(qn, kn, v, gate, Bl, S, rider=_all_gather_rider(wb))
        wb = wb.reshape(N_CHIPS, rows_b, D)
    w = _rest_views(wb)
    x1 = _matmul(og, w["attn_out"], name="attn_out", residual=x0)

    h1 = _rmsnorm_fwd(x1, norm_g[1], "norm1_fwd")
    u = _matmul(h1, w["ssm_in_u"], name="ssm_in_u")
    gate2 = _matmul(h1, w["ssm_in_gate"], name="ssm_in_gate")
    y, hs_re, hs_im, yg = _ssm_fwd(u, mats, Bl, S)
    gl = _matmul(yg, w["glu"], name="glu_mm")
    y3 = _glu_fwd(y, gl, gate2, glu_b)
    x2 = _matmul(y3, w["ssm_out"], name="ssm_out", residual=x1)

    dx2, dx2b, loss = _loss_head(x2, tgt)

    dy3 = _matmul(dx2b, w["ssm_out"], name="ssm_out_dgrad", tb=True)
    gb = wgrad(y3, dx2b, "ssm_out", "ssm_out_wgrad")
    dgl, dgate2, t1, dglu_b = _glu_bwd(dy3, y, gl, gate2, glu_b)
    t2 = _matmul(dgl, w["glu"], name="glu_dgrad", tb=True)
    gb = wgrad(yg, dgl, "glu", "glu_wgrad")
    dy = _gelu_bwd(t1, t2, y)
    du, dbr, dbi, dcr, dci, dlr, dli, dd = _ssm_bwd(u, dy, hs_re, hs_im, mats, Bl, S)
    dh1 = _matmul(du, w["ssm_in_u"], name="ssm_in_dgrad_u", tb=True)
    dh1 = _matmul(dgate2, w["ssm_in_gate"], name="ssm_in_dgrad_gate", tb=True, residual=dh1)
    gb = wgrad(h1, du, "ssm_in_u", "ssm_in_wgrad_u")
    gb = wgrad(h1, dgate2, "ssm_in_gate", "ssm_in_wgrad_gate")
    dx1, dx1b, dng1 = _rmsnorm_bwd(x1, norm_g[1], dh1, dx2, "norm1_bwd")
    small_early = (dng1,) + _ssm_unblock(dbr, dbi, dcr, dci, dlr, dli, dd, G) + (dglu_b.reshape(D),)

    gb = wgrad(og, dx1b, "attn_out", "attn_out_wgrad")
    if core is None:
        dog = _matmul(dx1b, w["attn_out"], name="attn_out_dgrad", tb=True)
        do, dgate = _attn_gate_bwd(dog, o, gate)
        dqn, dkn, dv = _attn_bwd(qn, kn, v, do, btot, Bl, S)
    else:
        dog, from_sibling = _matmul(dx1b, w["attn_out"], name="attn_out_dgrad", tb=True, rider=_sibling_half_rider(gb))
        do, dgate = _attn_gate_bwd(dog, o, gate)
        chip_sum_b = _add_halves((gb, from_sibling), core, "grads_b_add_halves")
        dqn, dkn, dv, gb, small_early = _attn_bwd(
            qn, kn, v, do, btot, Bl, S,
            rider=_join_riders(_chip_exchange_rider(chip_sum_b), _all_gather_rider(_pack_small(small_early))))
    dq, dqg = _qknorm_bwd(q, q_g, dqn, "qnorm_bwd")
    dk, dkg = _qknorm_bwd(k, k_g, dkn, "knorm_bwd")
    dproj = [dq, dk, dv, dgate]
    for j in range(4):
        ga = _matmul(h0, dproj[j], name=f"attn_in_wgrad_{j}", ta=True, out=_attn_in_views(ga)[j])
    dh0 = None
    if core is None:
        for j in range(4):
            dh0 = _matmul(dproj[j], w_in[j], name=f"attn_in_dgrad_{j}", tb=True, residual=dh0)
    else:
        chip_sum_a = _add_halves((ga, _sibling_send_half(ga, "grads_a_sibling_half")), core, "grads_a_add_halves")
        rows = chip_sum_a.shape[1] // 4
        parts = []
        for j in range(4):
            dh0, part = _matmul(dproj[j], w_in[j], name=f"attn_in_dgrad_{j}", tb=True, residual=dh0,
                                rider=_chip_exchange_rider(chip_sum_a, j * rows, rows))
            parts.append(part)
        ga = jnp.concatenate(parts, axis=1)
    dx0, _, dng0 = _rmsnorm_bwd(x0, norm_g[0], dh0, dx1, "norm0_bwd")

    return loss, dx0.reshape(Bl, S, D), ga, gb, small_early, (dng0, dqg, dkg)


def _small_early_shapes(D):
    G = D // GROUP
    return [(1, D), (G, STATE), (G, STATE), (G, STATE, GROUP), (G, STATE, GROUP), (G, GROUP, STATE), (G, GROUP, STATE),
            (D,), (D,)]


def _chip_rows(a, chip, n_per):
    return lax.dynamic_slice_in_dim(a, chip * n_per, n_per, axis=0)


def kernel(x, norm_g, attn_w_in, attn_q_g, attn_k_g, attn_w_out, ssm_w_in, ssm_A_re, ssm_A_im, ssm_log_dt, ssm_B_re, ssm_B_im, ssm_C_re, ssm_C_im, ssm_D, ssm_glu_w, ssm_glu_b, ssm_w_out, loss_target, m_norm_g, m_attn_w_in, m_attn_q_g, m_attn_k_g, m_attn_w_out, m_ssm_w_in, m_ssm_A_re, m_ssm_A_im, m_ssm_log_dt, m_ssm_B_re, m_ssm_B_im, m_ssm_C_re, m_ssm_C_im, m_ssm_D, m_ssm_glu_w, m_ssm_glu_b, m_ssm_w_out, v_norm_g, v_attn_w_in, v_attn_q_g, v_attn_k_g, v_attn_w_out, v_ssm_w_in, v_ssm_A_re, v_ssm_A_im, v_ssm_log_dt, v_ssm_B_re, v_ssm_B_im, v_ssm_C_re, v_ssm_C_im, v_ssm_D, v_ssm_glu_w, v_ssm_glu_b, v_ssm_w_out):
    D = x.shape[-1]
    cx, cy, cc = _position()
    chip = 2 * cx + cy
    G = D // GROUP
    Gl = G // N_CHIPS

    def my_half(a):
        return lax.dynamic_slice_in_dim(a, cc * (a.shape[0] // 2), a.shape[0] // 2, axis=0)

    wa = _all_gather8(my_half(attn_w_in[0].astype(BF16)), "attn_in_all_gather").reshape(N_CHIPS, D, D)
    wb_half = my_half(_pack_rest(attn_w_out[0], ssm_w_in[0], ssm_glu_w[0], ssm_w_out[0]).astype(BF16))

    ssm_local = [ssm_A_re[0], ssm_A_im[0], ssm_log_dt[0], ssm_B_re[0], ssm_B_im[0], ssm_C_re[0], ssm_C_im[0], ssm_D[0],
                 ssm_glu_b[0]]
    small_local = _pack_small(ssm_local)
    half_rows = small_local.shape[0] // 2
    small_half = lax.dynamic_slice_in_dim(small_local, cc * half_rows, half_rows, axis=0)
    small_all = _all_gather8(small_half, "ssm_params_all_gather").reshape(N_CHIPS, 2 * half_rows, LANES)
    per_chip = [_unpack_small(small_all[j], [p.shape for p in ssm_local]) for j in range(N_CHIPS)]
    ssm_full = [jnp.concatenate([per_chip[j][i] for j in range(N_CHIPS)], axis=0) for i in range(len(ssm_local))]

    loss, grad_x, gathered_a, gathered_b, small_early, small_late = _local_step(
        x, loss_target, norm_g, attn_q_g[0], attn_k_g[0], wa, wb_half, ssm_full, core=cc)
    loss = lax.psum(loss[0, 0], ("x", "y", "c"))

    q4 = D // N_CHIPS
    south = cc == 0

    def both_halves(gathered, tag):
        mine = _sum_leading(gathered, f"grads_{tag}_sum_chips")
        other = _sibling_swap(mine, f"grads_{tag}_sibling_swap")
        return jnp.concatenate([jnp.where(south, mine, other), jnp.where(south, other, mine)], axis=0)

    gd = (both_halves(gathered_a, "a"),) + _unpack_rest(both_halves(gathered_b, "b"))

    (dng1, dl_re, dl_im, dbb_re, dbb_im, dc_re, dc_im, dd_skip, dglu_b) = _unpack_small(
        _sum_leading(small_early, "small_early_sum"), _small_early_shapes(D))
    late_all = _all_gather8(_pack_small(small_late), "small_late_all_gather")
    dng0, dqg, dkg = _unpack_small(_sum_leading(late_all, "small_late_sum"), [s.shape for s in small_late])
    dng = jnp.concatenate([dng0, dng1], axis=0)
    a_re, a_im, log_dt, b_re, b_im = ssm_local[:5]
    _, zoh_vjp = jax.vjp(_zoh, a_re, a_im, log_dt, b_re, b_im)
    da_re, da_im, dlog_dt, db_re, db_im = zoh_vjp((_chip_rows(dl_re, chip, Gl), _chip_rows(dl_im, chip, Gl),
                                                   _chip_rows(dbb_re, chip, Gl), _chip_rows(dbb_im, chip, Gl)))
    grads = {
        "norm_g": dng, "attn_w_in": gd[0][None], "attn_q_g": dqg, "attn_k_g": dkg, "attn_w_out": gd[1][None],
        "ssm_w_in": gd[2][None], "ssm_A_re": da_re[None], "ssm_A_im": da_im[None], "ssm_log_dt": dlog_dt[None],
        "ssm_B_re": db_re[None], "ssm_B_im": db_im[None], "ssm_C_re": _chip_rows(dc_re, chip, Gl)[None],
        "ssm_C_im": _chip_rows(dc_im, chip, Gl)[None], "ssm_D": _chip_rows(dd_skip, chip, q4)[None],
        "ssm_glu_w": gd[3][None], "ssm_glu_b": _chip_rows(dglu_b, chip, q4)[None], "ssm_w_out": gd[4][None],
    }
    weights = dict(norm_g=norm_g, attn_w_in=attn_w_in, attn_q_g=attn_q_g, attn_k_g=attn_k_g, attn_w_out=attn_w_out,
                   ssm_w_in=ssm_w_in, ssm_A_re=ssm_A_re, ssm_A_im=ssm_A_im, ssm_log_dt=ssm_log_dt, ssm_B_re=ssm_B_re,
                   ssm_B_im=ssm_B_im, ssm_C_re=ssm_C_re, ssm_C_im=ssm_C_im, ssm_D=ssm_D, ssm_glu_w=ssm_glu_w,
                   ssm_glu_b=ssm_glu_b, ssm_w_out=ssm_w_out)
    m = dict(norm_g=m_norm_g, attn_w_in=m_attn_w_in, attn_q_g=m_attn_q_g, attn_k_g=m_attn_k_g, attn_w_out=m_attn_w_out,
             ssm_w_in=m_ssm_w_in, ssm_A_re=m_ssm_A_re, ssm_A_im=m_ssm_A_im, ssm_log_dt=m_ssm_log_dt, ssm_B_re=m_ssm_B_re,
             ssm_B_im=m_ssm_B_im, ssm_C_re=m_ssm_C_re, ssm_C_im=m_ssm_C_im, ssm_D=m_ssm_D, ssm_glu_w=m_ssm_glu_w,
             ssm_glu_b=m_ssm_glu_b, ssm_w_out=m_ssm_w_out)
    v = dict(norm_g=v_norm_g, attn_w_in=v_attn_w_in, attn_q_g=v_attn_q_g, attn_k_g=v_attn_k_g, attn_w_out=v_attn_w_out,
             ssm_w_in=v_ssm_w_in, ssm_A_re=v_ssm_A_re, ssm_A_im=v_ssm_A_im, ssm_log_dt=v_ssm_log_dt, ssm_B_re=v_ssm_B_re,
             ssm_B_im=v_ssm_B_im, ssm_C_re=v_ssm_C_re, ssm_C_im=v_ssm_C_im, ssm_D=v_ssm_D, ssm_glu_w=v_ssm_glu_w,
             ssm_glu_b=v_ssm_glu_b, ssm_w_out=v_ssm_w_out)
    names = list(weights)
    dense_names = ("attn_w_in", "attn_w_out", "ssm_w_in", "ssm_glu_w", "ssm_w_out")
    delta, new_m, new_v = {}, {}, {}
    for n in dense_names:
        delta[n], new_m[n], new_v[n] = _adamw(weights[n], grads[n], m[n], v[n], "adamw_" + n)
    small_names = [n for n in names if n not in dense_names]
    small_shapes = [weights[n].shape for n in small_names]
    packs = [_pack_small([d[n] for n in small_names]) for d in (weights, grads, m, v)]
    outs = _adamw(*packs, "adamw_small")
    for res, out in zip((delta, new_m, new_v), outs):
        for n, val in zip(small_names, _unpack_small(out, small_shapes)):
            res[n] = val
    return (loss, grad_x, *[grads[n] for n in names], *[delta[n] for n in names], *[new_m[n] for n in names],
            *[new_v[n] for n in names])
```

```python
import functools
import math

import jax
import jax.numpy as jnp
from jax import lax
from jax.experimental import pallas as pl
from jax.experimental.pallas import tpu as pltpu

F32 = jnp.float32
BF16 = jnp.bfloat16

HEAD_DIM = 128
GROUP = 16
STATE = 64
RMS_EPS = 1e-6
ADAM_LR = 0.001
ADAM_B1 = 0.9
ADAM_B2 = 0.999
ADAM_EPS = 1e-08
ADAM_WD = 0.01
ADAM_STEP = 10

N_CHIPS = 4
N_DEV = 8
SUBLANES = 8
LANES = 128
VMEM_LIMIT = 56 * 1024 * 1024
ROW_BLOCK_ELEMS = 1 << 19
MATMUL_TILE = 1024
MATMUL_PANEL_BYTES = 8 * 1024 * 1024
MESH = pl.DeviceIdType.MESH

NN = (((1,), (0,)), ((), ()))
NT = (((1,), (1,)), ((), ()))
TN = (((0,), (0,)), ((), ()))


def _params(*sem):
    return pltpu.CompilerParams(dimension_semantics=sem, vmem_limit_bytes=VMEM_LIMIT)


def _dot(a, b, dims=NN):
    return lax.dot_general(a, b, dims, preferred_element_type=F32)


def _split(a):
    hi = a.astype(BF16)
    lo = (a - hi.astype(F32)).astype(BF16)
    return hi, lo


def _dot_f32(a, b, dims=NN):
    return _dot(a.astype(BF16), b.astype(BF16), dims)


def _sigmoid(x):
    return 1.0 / (1.0 + jnp.exp(-x))


def _silu_parts(x):
    s = _sigmoid(x)
    return x * s, s * (1.0 + x * (1.0 - s))


_GELU_C = math.sqrt(2.0 / math.pi)


def _gelu_parts(x):
    x2 = x * x
    t = jnp.tanh(_GELU_C * (x + 0.044715 * x * x2))
    val = 0.5 * x * (1.0 + t)
    der = 0.5 * (1.0 + t) + 0.5 * x * (1.0 - t * t) * _GELU_C * (1.0 + 3.0 * 0.044715 * x2)
    return val, der


class _View:
    def __init__(self, buf, shape, locate, row_tile, col_tile):
        self.buf, self.shape, self.locate, self.row_tile, self.col_tile = buf, shape, locate, row_tile, col_tile

    def spec(self, t0, t1, block_of):
        def index(i, j, k):
            rb, cb = block_of(i, j, k)
            slab, r, c = self.locate(rb * t0, cb * t1)
            return slab, r // t0, c // t1
        return pl.BlockSpec((None, t0, t1), index)


def _operand(x):
    return (x.buf, x.shape, x.row_tile, x.col_tile) if isinstance(x, _View) else (x, x.shape, x.shape[0], x.shape[1])


class _Epilogue:
    def __init__(self, fn, tiles, rows, out_dtypes):
        self.fn, self.tiles, self.rows, self.out_dtypes = fn, tiles, rows, out_dtypes


def _matmul(a, b, *, name, ta=False, tb=False, residual=None, out_dtype=F32, out=None, rider=None, epilogue=None):
    if residual is not None:
        epilogue = _Epilogue(lambda r, res: (r + res.astype(F32),), [residual], [], [out_dtype])
    elif epilogue is None:
        epilogue = _Epilogue(lambda r: (r,), [], [], [out_dtype])
    n_extra, n_out = len(epilogue.tiles) + len(epilogue.rows), len(epilogue.out_dtypes)
    a_arr, a_shape, a_rt, a_ct = _operand(a)
    b_arr, b_shape, b_rt, b_ct = _operand(b)
    (K, M) = a_shape if ta else a_shape[::-1]
    N = b_shape[0] if tb else b_shape[1]
    a_mt, a_kt = (a_ct, a_rt) if ta else (a_rt, a_ct)
    b_nt, b_kt = (b_rt, b_ct) if tb else (b_ct, b_rt)
    k_cap = MATMUL_PANEL_BYTES // (MATMUL_TILE * max(a_arr.dtype.itemsize, b_arr.dtype.itemsize))
    n_cap = MATMUL_TILE if n_extra + n_out <= 2 else MATMUL_TILE // 2
    tm, tn, tk = min(M, MATMUL_TILE, a_mt), min(N, n_cap, b_nt), min(K, k_cap, a_kt)
    if out is not None:
        tm, tn = min(tm, out.row_tile), min(tn, out.col_tile)
    pk = min(tk, b_kt)
    pieces = tk // pk
    nk = K // tk
    dims = ((((0,) if ta else (1,)), ((1,) if tb else (0,))), ((), ()))
    n_in = 1 + pieces + n_extra + (out is not None)

    def body(*refs):
        a_ref, b_refs = refs[0], refs[1:1 + pieces]
        e_refs = refs[1 + pieces:1 + pieces + n_extra]
        o_refs = refs[n_in:n_in + n_out]

        def finish(r):
            for o_ref, val in zip(o_refs, epilogue.fn(r, *[e[...] for e in e_refs])):
                o_ref[...] = val.astype(o_ref.dtype)

        part = None
        for p, b_ref in enumerate(b_refs):
            ks = slice(p * pk, (p + 1) * pk)
            a_blk = a_ref[...] if pieces == 1 else (a_ref[ks, :] if ta else a_ref[:, ks])
            term = _dot(a_blk.astype(BF16), b_ref[...].astype(BF16), dims)
            part = term if part is None else part + term
        if nk == 1:
            finish(part)
            return
        acc = refs[n_in + n_out]
        k = pl.program_id(2)

        @pl.when(k == 0)
        def _():
            acc[...] = part

        @pl.when(k > 0)
        def _():
            acc[...] += part

        @pl.when(k == nk - 1)
        def _():
            finish(acc[...])

    def spec(x, t0, t1, block_of):
        if isinstance(x, _View):
            return x.spec(t0, t1, block_of)
        return pl.BlockSpec((t0, t1), block_of)

    a_spec = spec(a, tk, tm, lambda i, j, k: (k, i)) if ta else spec(a, tm, tk, lambda i, j, k: (i, k))
    b_specs = [spec(b, tn, pk, lambda i, j, k, p=p: (j, k * pieces + p)) if tb else
               spec(b, pk, tn, lambda i, j, k, p=p: (k * pieces + p, j)) for p in range(pieces)]
    tile_spec = pl.BlockSpec((tm, tn), lambda i, j, k: (i, j))
    in_specs = [a_spec] + b_specs + [tile_spec] * len(epilogue.tiles) + [
        pl.BlockSpec((1, tn), lambda i, j, k: (0, j))] * len(epilogue.rows)
    args = [a_arr] + [b_arr] * pieces + list(epilogue.tiles) + list(epilogue.rows)
    aliases = {}
    if out is None:
        out_specs = [tile_spec] * n_out
        out_shape = [jax.ShapeDtypeStruct((M, N), d) for d in epilogue.out_dtypes]
    else:
        out_specs = [out.spec(tm, tn, lambda i, j, k: (i, j))]
        out_shape = [jax.ShapeDtypeStruct(out.buf.shape, out.buf.dtype)]
        in_specs.append(pl.BlockSpec(memory_space=pl.ANY))
        args.append(out.buf)
        aliases = {len(args) - 1: 0}
    grid = (M // tm, N // tn, nk)
    scratch = [pltpu.VMEM((tm, tn), F32)] if nk > 1 else []
    body, in_specs, out_specs, out_shape, scratch, extra = _ride(rider, body, grid, in_specs, out_specs, out_shape,
                                                                 scratch)
    results = pl.pallas_call(
        body, name=name, grid=grid, in_specs=in_specs, out_specs=out_specs, out_shape=out_shape,
        scratch_shapes=scratch, input_output_aliases=aliases,
        compiler_params=_params(*(("parallel", "parallel", "arbitrary") if rider is None else ("arbitrary",) * 3)),
    )(*args, *extra)
    return results[0] if len(results) == 1 else results


def _row_tile(T, C):
    tr = max(SUBLANES, min(T, ROW_BLOCK_ELEMS // C) // SUBLANES * SUBLANES)
    while T % tr:
        tr -= SUBLANES
    return tr


def _rows_call(body, name, T, C, row_ins, full_ins, row_outs, acc_outs=()):
    tr = _row_tile(T, C)
    row_spec = pl.BlockSpec((tr, C), lambda i: (i, 0))
    in_specs = [row_spec] * len(row_ins) + [pl.BlockSpec(f.shape, lambda i, n=f.ndim: (0,) * n) for f in full_ins]
    out_specs = [row_spec] * len(row_outs) + [pl.BlockSpec(s, lambda i, n=len(s): (0,) * n) for s in acc_outs]
    out_shape = [jax.ShapeDtypeStruct((T, C), d) for d in row_outs] + [jax.ShapeDtypeStruct(s, F32) for s in acc_outs]
    return pl.pallas_call(
        body, name=name, grid=(T // tr,), in_specs=in_specs, out_specs=out_specs, out_shape=out_shape,
        compiler_params=_params("arbitrary" if acc_outs else "parallel"),
    )(*row_ins, *full_ins)


def _rmsnorm_fwd(x, g, name):
    T, C = x.shape

    def body(x_ref, g_ref, h_ref):
        xv = x_ref[...]
        r = lax.rsqrt(jnp.mean(xv * xv, axis=-1, keepdims=True) + RMS_EPS)
        h_ref[...] = ((xv * r) * g_ref[...]).astype(BF16)

    return _rows_call(body, name, T, C, [x], [g.reshape(1, C)], [BF16])[0]


def _rmsnorm_bwd(x, g, dh, dres, name):
    T, C = x.shape

    def body(x_ref, dh_ref, dres_ref, g_ref, dx_ref, dxb_ref, dg_ref):
        @pl.when(pl.program_id(0) == 0)
        def _():
            dg_ref[...] = jnp.zeros_like(dg_ref)

        xv = x_ref[...]
        dhv = dh_ref[...]
        r = lax.rsqrt(jnp.mean(xv * xv, axis=-1, keepdims=True) + RMS_EPS)
        xn = xv * r
        dg_ref[...] += jnp.sum(dhv * xn, axis=0, keepdims=True)
        dxn = dhv * g_ref[...]
        dx = dres_ref[...] + r * (dxn - xn * jnp.mean(dxn * xn, axis=-1, keepdims=True))
        dx_ref[...] = dx
        dxb_ref[...] = dx.astype(BF16)

    return _rows_call(body, name, T, C, [x, dh, dres], [g.reshape(1, C)], [F32, BF16], [(1, C)])


def _heads(C):
    return [slice(h * HEAD_DIM, (h + 1) * HEAD_DIM) for h in range(C // HEAD_DIM)]


def _qknorm_fwd(q, g, name):
    T, C = q.shape

    def body(q_ref, g_ref, o_ref):
        for head in _heads(C):
            xv = q_ref[:, head]
            r = lax.rsqrt(jnp.mean(xv * xv, axis=-1, keepdims=True) + RMS_EPS)
            o_ref[:, head] = ((xv * r) * g_ref[...]).astype(BF16)

    return _rows_call(body, name, T, C, [q], [g.reshape(1, HEAD_DIM)], [BF16])[0]


def _qknorm_bwd(q, g, dqn, name):
    T, C = q.shape

    def body(q_ref, d_ref, g_ref, dq_ref, dg_ref):
        @pl.when(pl.program_id(0) == 0)
        def _():
            dg_ref[...] = jnp.zeros_like(dg_ref)

        dg = jnp.zeros((1, HEAD_DIM), F32)
        for head in _heads(C):
            xv = q_ref[:, head]
            dv = d_ref[:, head]
            r = lax.rsqrt(jnp.mean(xv * xv, axis=-1, keepdims=True) + RMS_EPS)
            xn = xv * r
            dg = dg + jnp.sum(dv * xn, axis=0, keepdims=True)
            dxn = dv * g_ref[...]
            dq_ref[:, head] = (r * (dxn - xn * jnp.mean(dxn * xn, axis=-1, keepdims=True))).astype(BF16)
        dg_ref[...] += dg

    return _rows_call(body, name, T, C, [q, dqn], [g.reshape(1, HEAD_DIM)], [BF16], [(1, HEAD_DIM)])


ATT_TQ = 512
ATT_TK = 256
ATT_HEADS = 2


def _logsig_pair(z):
    a = jnp.minimum(z, 0.0) - jnp.log(1.0 + jnp.exp(-jnp.abs(z)))
    return a, a - z


def _tri(n, strict_upper_src):
    j = lax.broadcasted_iota(jnp.int32, (n, n), 0)
    s = lax.broadcasted_iota(jnp.int32, (n, n), 1)
    if strict_upper_src == "gt":
        m = j > s
    elif strict_upper_src == "le":
        m = j <= s
    else:
        m = j < s
    return jnp.where(m, 1.0, 0.0).astype(BF16)


def _cumdot(x, tri):
    hi, lo = _split(x)
    return _dot(hi, tri) + _dot(lo, tri)


def _attn_tiles(S):
    tq, tk = min(ATT_TQ, S), min(ATT_TK, S)
    return tq, tk, S // tq, tq // tk


def _attn_fwd(qn, kn, v, gate, Bl, S, rider=None):
    T, C = qn.shape
    H = C // HEAD_DIM
    tq, tk, nq, kpq = _attn_tiles(S)
    hp = min(ATT_HEADS, H)
    scale = 1.0 / math.sqrt(HEAD_DIM)

    def body(q_ref, k_ref, v_ref, g_ref, o_ref, bt_ref, og_ref):
        i = pl.program_id(2)
        tri = _tri(tk, "gt")
        rowpos = lax.broadcasted_iota(jnp.int32, (tq, tk), 0) + i * tq
        colpos = lax.broadcasted_iota(jnp.int32, (tq, tk), 1)
        o_ref[...] = jnp.zeros_like(o_ref)
        bt_ref[...] = jnp.zeros_like(bt_ref)

        def group(g, masked):
            work = []
            for p in reversed(range(kpq)):
                j = g * kpq + p
                rows = pl.ds(pl.multiple_of(j * tk, tk), tk)
                mask = (colpos + j * tk) < rowpos if masked else None
                for head in _heads(hp * HEAD_DIM):
                    a, b = _logsig_pair(_dot(q_ref[:, head], k_ref[rows, head], NT) * scale)
                    if masked:
                        b = jnp.where(mask, b, 0.0)
                    work.append((head, rows, a, b, _cumdot(b, tri), mask))
            for n, head in enumerate(_heads(hp * HEAD_DIM)):
                total = bt_ref[n]
                out = None
                for rows, a, b, suffix, mask in [w[1:] for w in work if w[0] == head]:
                    w = jnp.exp(a + suffix + total)
                    if masked:
                        w = jnp.where(mask, w, 0.0)
                    term = _dot(w.astype(BF16), v_ref[rows, head])
                    out = term if out is None else out + term
                    total = total + jnp.sum(b, axis=-1, keepdims=True)
                o_ref[:, head] += out
                bt_ref[n] = total

        group(i, True)

        def step(n, carry):
            group(i - 1 - n, False)
            return carry

        lax.fori_loop(0, i, step, 0)
        og_ref[...] = (o_ref[...] * _silu_parts(g_ref[...])[0]).astype(BF16)

    qspec = pl.BlockSpec((tq, hp * HEAD_DIM), lambda b, h, i: (b * nq + i, h))
    kspec = pl.BlockSpec((S, hp * HEAD_DIM), lambda b, h, i: (b, h))
    btspec = pl.BlockSpec((None, hp, tq, 1), lambda b, h, i: (b, h, i, 0))
    grid = (Bl, H // hp, nq)
    body, in_specs, out_specs, out_shape, scratch, extra = _ride(
        rider, body, grid, [qspec, kspec, kspec, qspec], [qspec, btspec, qspec],
        [jax.ShapeDtypeStruct((T, C), F32), jax.ShapeDtypeStruct((Bl, H, S, 1), F32),
         jax.ShapeDtypeStruct((T, C), BF16)], [])
    return pl.pallas_call(
        body, name="attn_fwd", grid=grid, in_specs=in_specs, out_specs=out_specs, out_shape=out_shape,
        scratch_shapes=scratch, compiler_params=_params("arbitrary", "arbitrary", "arbitrary"),
    )(qn, kn, v, gate, *extra)


def _attn_bwd(qn, kn, v, do, btot, Bl, S, rider=None):
    T, C = qn.shape
    H = C // HEAD_DIM
    tq, tk, nq, kpq = _attn_tiles(S)
    hp = min(ATT_HEADS, H)
    scale = 1.0 / math.sqrt(HEAD_DIM)

    def body(q_ref, k_ref, v_ref, do_ref, bt_ref, dq_ref, dk_ref, dv_ref, pb_ref, pdl_ref):
        i = pl.program_id(2)

        @pl.when(i == 0)
        def _():
            dk_ref[...] = jnp.zeros_like(dk_ref)
            dv_ref[...] = jnp.zeros_like(dv_ref)

        tri_le = _tri(tk, "le")
        tri_lt = _tri(tk, "lt")
        rowpos = lax.broadcasted_iota(jnp.int32, (tq, tk), 0) + i * tq
        colpos = lax.broadcasted_iota(jnp.int32, (tq, tk), 1)
        dq_ref[...] = jnp.zeros_like(dq_ref)
        pb_ref[...] = bt_ref[...]
        pdl_ref[...] = jnp.zeros_like(pdl_ref)

        def group(g, masked):
            work = []
            for p in range(kpq):
                j = g * kpq + p
                rows = pl.ds(pl.multiple_of(j * tk, tk), tk)
                mask = (colpos + j * tk) < rowpos if masked else None
                for head in _heads(hp * HEAD_DIM):
                    a, b = _logsig_pair(_dot(q_ref[:, head], k_ref[rows, head], NT) * scale)
                    if masked:
                        b = jnp.where(mask, b, 0.0)
                    work.append((head, rows, a, b, _cumdot(b, tri_le), _dot(do_ref[:, head], v_ref[rows, head], NT),
                                 mask))
            for n, head in enumerate(_heads(hp * HEAD_DIM)):
                remaining = pb_ref[n]
                swept = pdl_ref[n]
                dq = None
                for rows, a, b, cum, dw, mask in [w[1:] for w in work if w[0] == head]:
                    w = jnp.exp(a + (remaining - cum))
                    if masked:
                        w = jnp.where(mask, w, 0.0)
                    dl = dw * w
                    prefix = swept + _cumdot(dl, tri_lt)
                    beta = jnp.exp(a)
                    dz = dl * (1.0 - beta) - beta * prefix
                    if masked:
                        dz = jnp.where(mask, dz, 0.0)
                    dzb = (dz * scale).astype(BF16)
                    term = _dot(dzb, k_ref[rows, head])
                    dq = term if dq is None else dq + term
                    dk_ref[rows, head] += _dot(dzb, q_ref[:, head], TN)
                    dv_ref[rows, head] += _dot(w.astype(BF16), do_ref[:, head], TN)
                    remaining = remaining - jnp.sum(b, axis=-1, keepdims=True)
                    swept = swept + jnp.sum(dl, axis=-1, keepdims=True)
                dq_ref[:, head] += dq
                pb_ref[n] = remaining
                pdl_ref[n] = swept

        def step(g, carry):
            group(g, False)
            return carry

        lax.fori_loop(0, i, step, 0)
        group(i, True)

    qspec = pl.BlockSpec((tq, hp * HEAD_DIM), lambda b, h, i: (b * nq + i, h))
    kspec = pl.BlockSpec((S, hp * HEAD_DIM), lambda b, h, i: (b, h))
    btspec = pl.BlockSpec((None, hp, tq, 1), lambda b, h, i: (b, h, i, 0))
    grid = (Bl, H // hp, nq)
    body, in_specs, out_specs, out_shape, scratch, extra = _ride(
        rider, body, grid, [qspec, kspec, kspec, qspec, btspec], [qspec, kspec, kspec],
        [jax.ShapeDtypeStruct((T, C), F32)] * 3, [pltpu.VMEM((hp, tq, 1), F32), pltpu.VMEM((hp, tq, 1), F32)])
    return pl.pallas_call(
        body, name="attn_bwd", grid=grid, in_specs=in_specs, out_specs=out_specs, out_shape=out_shape,
        scratch_shapes=scratch, compiler_params=_params("arbitrary", "arbitrary", "arbitrary"),
    )(qn, kn, v, do, btot, *extra)


SSM_TIME_BLOCK = 512
CHUNK = SUBLANES


def _cmadd(xr, xi, ar, ai, sr, si):
    return xr + ar * sr - ai * si, xi + ar * si + ai * sr


def _chunk_scan(xr, xi, tab_ref, cr, ci, reverse):
    for lvl, d in enumerate((1, 2, 4)):
        shift = (CHUNK - d) if reverse else d
        sr = pltpu.roll(xr, shift, 0)
        si = pltpu.roll(xi, shift, 0)
        ar = tab_ref[pl.ds((2 * lvl) * CHUNK, CHUNK), :]
        ai = tab_ref[pl.ds((2 * lvl + 1) * CHUNK, CHUNK), :]
        xr, xi = _cmadd(xr, xi, ar, ai, sr, si)
    pr = tab_ref[pl.ds(6 * CHUNK, CHUNK), :]
    pi = tab_ref[pl.ds(7 * CHUNK, CHUNK), :]
    return _cmadd(xr, xi, pr, pi, cr, ci)


def _ssm_dims(S, C):
    G = C // GROUP
    GT = min(16, G)
    return G, GT, G // GT, GT * GROUP, GT * STATE, min(SSM_TIME_BLOCK, S)


def _ssm_fwd(u, mats, Bl, S):
    T, C = u.shape
    G, GT, ngt, cw, sw, TB = _ssm_dims(S, C)
    ntb = S // TB
    nch = TB // CHUNK

    def body(u_ref, bre_ref, bim_ref, cre_ref, cim_ref, d_ref, tab_ref, y_ref, hr_ref, hi_ref, yg_ref, car_r, car_i):
        @pl.when(pl.program_id(2) == 0)
        def _():
            car_r[...] = jnp.zeros_like(car_r)
            car_i[...] = jnp.zeros_like(car_i)

        uv = u_ref[...]
        hr_ref[...] = _dot_f32(uv, bre_ref[...])
        hi_ref[...] = _dot_f32(uv, bim_ref[...])

        def step(n, carry):
            cr, ci = carry
            rows = pl.ds(pl.multiple_of(n * CHUNK, CHUNK), CHUNK)
            xr, xi = _chunk_scan(hr_ref[rows, :], hi_ref[rows, :], tab_ref, cr, ci, False)
            hr_ref[rows, :] = xr
            hi_ref[rows, :] = xi
            last = (CHUNK - 1, CHUNK)
            return (jnp.broadcast_to(xr[last[0]:last[1], :], xr.shape), jnp.broadcast_to(xi[last[0]:last[1], :], xi.shape))

        cr, ci = lax.fori_loop(0, nch, step, (car_r[...], car_i[...]))
        car_r[...] = cr
        car_i[...] = ci
        y = _dot_f32(hr_ref[...], cre_ref[...]) - _dot_f32(hi_ref[...], cim_ref[...]) + d_ref[...] * uv
        y_ref[...] = y
        yg_ref[...] = _gelu_parts(y)[0].astype(BF16)

    uspec = pl.BlockSpec((TB, cw), lambda g, b, t: (b * ntb + t, g))
    hspec = pl.BlockSpec((TB, sw), lambda g, b, t: (b * ntb + t, g))

    def gspec(r, c):
        return pl.BlockSpec((None, r, c), lambda g, b, t: (g, 0, 0))

    return pl.pallas_call(
        body, name="ssm_fwd", grid=(ngt, Bl, ntb),
        in_specs=[uspec, gspec(cw, sw), gspec(cw, sw), gspec(sw, cw), gspec(sw, cw), gspec(1, cw), gspec(8 * CHUNK, sw)],
        out_specs=[uspec, hspec, hspec, uspec],
        out_shape=[jax.ShapeDtypeStruct((T, C), F32), jax.ShapeDtypeStruct((T, G * STATE), F32),
                   jax.ShapeDtypeStruct((T, G * STATE), F32), jax.ShapeDtypeStruct((T, C), BF16)],
        scratch_shapes=[pltpu.VMEM((CHUNK, sw), F32), pltpu.VMEM((CHUNK, sw), F32)],
        compiler_params=_params("parallel", "arbitrary", "arbitrary"),
    )(u, mats["bbd_re"], mats["bbd_im"], mats["cbd_re"], mats["cbd_im"], mats["d"], mats["tab_fwd"])


def _ssm_bwd(u, dy, h_re, h_im, mats, Bl, S):
    T, C = u.shape
    G, GT, ngt, cw, sw, TB = _ssm_dims(S, C)
    ntb = S // TB
    nch = TB // CHUNK
    rpb = TB // CHUNK

    def body(u_ref, dy_ref, hr_ref, hi_ref, hpr_ref, hpi_ref, cre_ref, cim_ref, bre_ref, bim_ref, d_ref, tab_ref,
             du_ref, dbr_ref, dbi_ref, dcr_ref, dci_ref, dlr_ref, dli_ref, dd_ref, gr_ref, gi_ref, car_r, car_i):
        b = pl.program_id(1)
        t = pl.program_id(2)

        @pl.when((b == 0) & (t == 0))
        def _():
            for ref in (dbr_ref, dbi_ref, dcr_ref, dci_ref, dlr_ref, dli_ref, dd_ref):
                ref[...] = jnp.zeros_like(ref)

        @pl.when(t == 0)
        def _():
            car_r[...] = jnp.zeros_like(car_r)
            car_i[...] = jnp.zeros_like(car_i)

        uv = u_ref[...]
        dyv = dy_ref[...]
        gr_ref[...] = _dot_f32(dyv, cre_ref[...], NT)
        gi_ref[...] = -_dot_f32(dyv, cim_ref[...], NT)
        alive = jnp.where(t == ntb - 1, 0.0, 1.0)
        row0 = lax.broadcasted_iota(jnp.int32, (CHUNK, sw), 0) == 0

        def step(m, carry):
            cr, ci, ar, ai = carry
            n = nch - 1 - m
            rows = pl.ds(pl.multiple_of(n * CHUNK, CHUNK), CHUNK)
            prow = pl.ds(pl.multiple_of(jnp.maximum(n - 1, 0) * CHUNK, CHUNK), CHUNK)
            xr, xi = _chunk_scan(gr_ref[rows, :], gi_ref[rows, :], tab_ref, cr, ci, True)
            gr_ref[rows, :] = xr
            gi_ref[rows, :] = xi
            first = n == 0
            pr = jnp.where(first, hpr_ref[...] * alive, hr_ref[prow, :])
            pi = jnp.where(first, hpi_ref[...] * alive, hi_ref[prow, :])
            sr = jnp.where(row0, pltpu.roll(pr, 1, 0), pltpu.roll(hr_ref[rows, :], 1, 0))
            si = jnp.where(row0, pltpu.roll(pi, 1, 0), pltpu.roll(hi_ref[rows, :], 1, 0))
            ar = ar + xr * sr + xi * si
            ai = ai + xi * sr - xr * si
            return (jnp.broadcast_to(xr[0:1, :], xr.shape), jnp.broadcast_to(xi[0:1, :], xi.shape), ar, ai)

        zero = jnp.zeros((CHUNK, sw), F32)
        cr, ci, ar, ai = lax.fori_loop(0, nch, step, (car_r[...], car_i[...], zero, zero))
        car_r[...] = cr
        car_i[...] = ci
        dlr_ref[...] += ar
        dli_ref[...] += ai
        gr = gr_ref[...]
        gi = gi_ref[...]
        dbr_ref[...] += _dot_f32(uv, gr, TN)
        dbi_ref[...] += _dot_f32(uv, gi, TN)
        dcr_ref[...] += _dot_f32(hr_ref[...], dyv, TN)
        dci_ref[...] -= _dot_f32(hi_ref[...], dyv, TN)
        dd_ref[...] += jnp.sum(dyv * uv, axis=0, keepdims=True)
        du_ref[...] = (_dot_f32(gr, bre_ref[...], NT) + _dot_f32(gi, bim_ref[...], NT) + d_ref[...] * dyv).astype(BF16)

    def tblk(b, t):
        return b * ntb + (ntb - 1 - t)

    uspec = pl.BlockSpec((TB, cw), lambda g, b, t: (tblk(b, t), g))
    hspec = pl.BlockSpec((TB, sw), lambda g, b, t: (tblk(b, t), g))
    hpspec = pl.BlockSpec((CHUNK, sw), lambda g, b, t: (jnp.maximum(tblk(b, t) * rpb - 1, 0), g))

    def gspec(r, c):
        return pl.BlockSpec((None, r, c), lambda g, b, t: (g, 0, 0))

    def gshape(r, c):
        return jax.ShapeDtypeStruct((ngt, r, c), F32)

    return pl.pallas_call(
        body, name="ssm_bwd", grid=(ngt, Bl, ntb),
        in_specs=[uspec, uspec, hspec, hspec, hpspec, hpspec, gspec(sw, cw), gspec(sw, cw), gspec(cw, sw), gspec(cw, sw),
                  gspec(1, cw), gspec(8 * CHUNK, sw)],
        out_specs=[uspec, gspec(cw, sw), gspec(cw, sw), gspec(sw, cw), gspec(sw, cw), gspec(CHUNK, sw), gspec(CHUNK, sw),
                   gspec(1, cw)],
        out_shape=[jax.ShapeDtypeStruct((T, C), BF16), gshape(cw, sw), gshape(cw, sw), gshape(sw, cw), gshape(sw, cw),
                   gshape(CHUNK, sw), gshape(CHUNK, sw), gshape(1, cw)],
        scratch_shapes=[pltpu.VMEM((TB, sw), F32), pltpu.VMEM((TB, sw), F32), pltpu.VMEM((CHUNK, sw), F32),
                        pltpu.VMEM((CHUNK, sw), F32)],
        compiler_params=_params("arbitrary", "arbitrary", "arbitrary"),
    )(u, dy, h_re, h_im, h_re, h_im, mats["cbd_re"], mats["cbd_im"], mats["bbd_re"], mats["bbd_im"], mats["d"],
      mats["tab_rev"])


def _zoh(a_re, a_im, log_dt, b_re, b_im):
    dt = jnp.exp(log_dt)[:, None]
    mag = jnp.exp(a_re * dt)
    l_re = mag * jnp.cos(a_im * dt)
    l_im = mag * jnp.sin(a_im * dt)
    den = a_re * a_re + a_im * a_im
    f_re = ((l_re - 1.0) * a_re + l_im * a_im) / den
    f_im = (l_im * a_re - (l_re - 1.0) * a_im) / den
    bb_re = f_re[..., None] * b_re - f_im[..., None] * b_im
    bb_im = f_re[..., None] * b_im + f_im[..., None] * b_re
    return l_re, l_im, bb_re, bb_im


def _ssm_matrices(a_re, a_im, log_dt, b_re, b_im, c_re, c_im, d, S):
    G = a_re.shape[0]
    _, GT, ngt, cw, sw, _ = _ssm_dims(S, G * GROUP)
    _, _, bb_re, bb_im = _zoh(a_re, a_im, log_dt, b_re, b_im)
    eye = jnp.eye(GT, dtype=BF16)

    def bd_b(bb):
        return jnp.einsum("tgpi,gh->tgihp", bb.astype(BF16).reshape(ngt, GT, STATE, GROUP), eye).reshape(ngt, cw, sw)

    def bd_c(c):
        return jnp.einsum("tgip,gh->tgphi", c.astype(BF16).reshape(ngt, GT, GROUP, STATE), eye).reshape(ngt, sw, cw)

    dt = jnp.exp(log_dt)[:, None]

    def power(k, conj):
        mag = jnp.exp(k * a_re * dt)
        ang = k * a_im * dt
        return (mag * jnp.cos(ang)).reshape(ngt, 1, sw), ((-1.0 if conj else 1.0) * mag * jnp.sin(ang)).reshape(ngt, 1, sw)

    r = jnp.arange(CHUNK)[None, :, None]

    def table(reverse):
        parts = []
        for dd in (1, 2, 4):
            pr, pi = power(float(dd), reverse)
            keep = (r <= CHUNK - 1 - dd) if reverse else (r >= dd)
            parts += [jnp.where(keep, pr, 0.0), jnp.where(keep, pi, 0.0)]
        exps = [(CHUNK - k) if reverse else (k + 1) for k in range(CHUNK)]
        pw = [power(float(e), reverse) for e in exps]
        parts += [jnp.concatenate([p[0] for p in pw], axis=1), jnp.concatenate([p[1] for p in pw], axis=1)]
        return jnp.concatenate([jnp.broadcast_to(p, (ngt, CHUNK, sw)) for p in parts], axis=1)

    return dict(bbd_re=bd_b(bb_re), bbd_im=bd_b(bb_im), cbd_re=bd_c(c_re), cbd_im=bd_c(c_im), d=d.reshape(ngt, 1, cw),
                tab_fwd=table(False), tab_rev=table(True))


def _ssm_unblock(dbr, dbi, dcr, dci, dlr, dli, dd, G):
    ngt = dbr.shape[0]
    GT = G // ngt
    eye = jnp.eye(GT, dtype=F32)

    def ub(x):
        return jnp.einsum("tgihp,gh->tgpi", x.reshape(ngt, GT, GROUP, GT, STATE), eye).reshape(G, STATE, GROUP)

    def uc(x):
        return jnp.einsum("tgphi,gh->tgip", x.reshape(ngt, GT, STATE, GT, GROUP), eye).reshape(G, GROUP, STATE)

    return (dlr.sum(axis=1).reshape(G, STATE), dli.sum(axis=1).reshape(G, STATE), ub(dbr), ub(dbi), uc(dcr), uc(dci),
            dd.reshape(G * GROUP))


def _attn_gate_bwd(o, gate):
    def fn(dog, o_blk, g_blk):
        val, der = _silu_parts(g_blk)
        return dog * val, dog * o_blk * der

    return _Epilogue(fn, [o, gate], [], [BF16, BF16])


def _glu_fwd(y, gate, glu_b):
    def fn(gl, y_blk, g_blk, b_blk):
        return gl, _gelu_parts(y_blk)[0] * _sigmoid(gl + b_blk) * _silu_parts(g_blk)[0]

    return _Epilogue(fn, [y, gate], [glu_b.reshape(1, -1)], [F32, BF16])


def _glu_bwd(dy3, y, gl, gate, glu_b):
    T, C = y.shape

    def body(d_ref, y_ref, gl_ref, g_ref, b_ref, dgl_ref, dgate_ref, t1_ref, db_ref):
        @pl.when(pl.program_id(0) == 0)
        def _():
            db_ref[...] = jnp.zeros_like(db_ref)

        yg = _gelu_parts(y_ref[...])[0]
        sg = _sigmoid(gl_ref[...] + b_ref[...])
        sl, sld = _silu_parts(g_ref[...])
        dv = d_ref[...]
        dy2 = dv * sl
        dgl = dy2 * yg * sg * (1.0 - sg)
        dgl_ref[...] = dgl.astype(BF16)
        dgate_ref[...] = (dv * (yg * sg) * sld).astype(BF16)
        t1_ref[...] = dy2 * sg
        db_ref[...] += jnp.sum(dgl, axis=0, keepdims=True)

    return _rows_call(body, "glu_bwd", T, C, [dy3, y, gl, gate], [glu_b.reshape(1, C)], [BF16, BF16, F32], [(1, C)])


def _gelu_bwd(t1, y):
    return _Epilogue(lambda t2, t1_blk, y_blk: ((t1_blk + t2) * _gelu_parts(y_blk)[1],), [t1, y], [], [F32])


def _loss_head(x2, target):
    T, C = x2.shape

    def body(x_ref, t_ref, d_ref, db_ref, l_ref):
        @pl.when(pl.program_id(0) == 0)
        def _():
            l_ref[...] = jnp.zeros_like(l_ref)

        e = x_ref[...] - t_ref[...]
        d = e * (1.0 / C)
        d_ref[...] = d
        db_ref[...] = d.astype(BF16)
        l_ref[...] += 0.5 * jnp.sum(jnp.sum(e * e, axis=-1, keepdims=True) * (1.0 / C), axis=0, keepdims=True)

    return _rows_call(body, "loss_head", T, C, [x2, target], [], [F32, BF16], [(1, 1)])


def _adamw(w, g, m, v, name):
    shape = w.shape
    C = shape[-1]
    R = w.size // C
    bc1 = 1.0 - ADAM_B1 ** ADAM_STEP
    bc2 = 1.0 - ADAM_B2 ** ADAM_STEP

    def body(w_ref, g_ref, m_ref, v_ref, d_ref, nm_ref, nv_ref):
        gv = g_ref[...]
        mn = ADAM_B1 * m_ref[...] + (1.0 - ADAM_B1) * gv
        vn = ADAM_B2 * v_ref[...] + (1.0 - ADAM_B2) * (gv * gv)
        d_ref[...] = -ADAM_LR * ((mn / bc1) / (jnp.sqrt(vn / bc2) + ADAM_EPS) + ADAM_WD * w_ref[...])
        nm_ref[...] = mn
        nv_ref[...] = vn

    outs = _rows_call(body, name, R, C, [a.reshape(R, C) for a in (w, g, m, v)], [], [F32, F32, F32])
    return [o.reshape(shape) for o in outs]


def _sum_leading(x, name, half=None):
    n, R, C = x.shape
    tr = _row_tile(R, C * n)
    nb = R // tr

    def body(*refs):
        x_ref, o_ref = refs[-2:]
        acc = x_ref[0].astype(F32)
        for k in range(1, n):
            acc = acc + x_ref[k].astype(F32)
        o_ref[...] = acc

    if half is None:
        return pl.pallas_call(
            body, name=name, grid=(nb,), in_specs=[pl.BlockSpec((n, tr, C), lambda i: (0, i, 0))],
            out_specs=pl.BlockSpec((tr, C), lambda i: (i, 0)), out_shape=jax.ShapeDtypeStruct((R, C), F32),
            compiler_params=_params("parallel"),
        )(x)
    grid_spec = pltpu.PrefetchScalarGridSpec(
        num_scalar_prefetch=1, grid=(nb,), in_specs=[pl.BlockSpec((n, tr, C), lambda i, c_ref: (0, i, 0))],
        out_specs=pl.BlockSpec((tr, C), lambda i, c_ref: (c_ref[0] * nb + i, 0)))
    return pl.pallas_call(
        body, name=name, grid_spec=grid_spec, out_shape=jax.ShapeDtypeStruct((2 * R, C), F32),
        compiler_params=_params("parallel"),
    )(half.reshape(1).astype(jnp.int32), x)


def _add_halves(g, c, name):
    full, recv = g
    n, R, C = full.shape
    half = R // 2
    tr = _row_tile(half, C)
    nb = half // tr

    def body(c_ref, a_ref, b_ref, o_ref):
        o_ref[...] = (a_ref[...] + b_ref[...]).astype(BF16)

    grid_spec = pltpu.PrefetchScalarGridSpec(
        num_scalar_prefetch=1, grid=(n, nb),
        in_specs=[pl.BlockSpec((None, tr, C), lambda j, i, c_ref: (j, c_ref[0] * nb + i, 0)),
                  pl.BlockSpec((None, tr, C), lambda j, i, c_ref: (j, i, 0))],
        out_specs=pl.BlockSpec((None, tr, C), lambda j, i, c_ref: (j, i, 0)))
    return pl.pallas_call(
        body, name=name, grid_spec=grid_spec, out_shape=jax.ShapeDtypeStruct((n, half, C), BF16),
        compiler_params=_params("parallel", "parallel"),
    )(c.reshape(1).astype(jnp.int32), full, recv)


ANY = pl.BlockSpec(memory_space=pl.ANY)


def _position():
    return lax.axis_index("x"), lax.axis_index("y"), lax.axis_index("c")


def _all_gather8(blk, name):
    return _standalone(_all_gather_rider(blk), name)[0]


class _Rider:
    def __init__(self, arrays, out_shapes, sems, start, finish):
        self.arrays, self.out_shapes, self.sems, self.start, self.finish = arrays, out_shapes, sems, start, finish


def _join_riders(*riders):
    def split(refs):
        ins, outs, sems = [], [], []
        for r in riders:
            ins.append(refs[:len(r.arrays)])
            refs = refs[len(r.arrays):]
        for r in riders:
            outs.append(refs[:len(r.out_shapes)])
            refs = refs[len(r.out_shapes):]
        for r in riders:
            sems.append(refs[:len(r.sems)])
            refs = refs[len(r.sems):]
        return [i + o + s for i, o, s in zip(ins, outs, sems)]

    def start(*refs):
        for r, own in zip(riders, split(refs)):
            r.start(*own)

    def finish(*refs):
        for r, own in zip(riders, split(refs)):
            r.finish(*own)

    return _Rider([a for r in riders for a in r.arrays], [s for r in riders for s in r.out_shapes],
                  [s for r in riders for s in r.sems], start, finish)


def _standalone(rider, name):
    def body(*refs):
        rider.start(*refs)
        rider.finish(*refs)

    return pl.pallas_call(
        body, name=name, in_specs=[ANY] * len(rider.arrays), out_specs=[ANY] * len(rider.out_shapes),
        out_shape=rider.out_shapes, scratch_shapes=rider.sems,
    )(*rider.arrays)


def _ride(rider, body, grid, in_specs, out_specs, out_shape, scratch):
    if rider is None:
        return body, in_specs, out_specs, out_shape, scratch, []
    ni, no, ns = len(in_specs), len(out_specs), len(scratch)
    ri, ro = len(rider.arrays), len(rider.out_shapes)

    def full(*refs):
        ins, refs = refs[:ni], refs[ni:]
        r_ins, refs = refs[:ri], refs[ri:]
        outs, refs = refs[:no], refs[no:]
        r_outs, refs = refs[:ro], refs[ro:]
        scr, r_sems = refs[:ns], refs[ns:]
        ids = [pl.program_id(a) for a in range(len(grid))]
        first = functools.reduce(jnp.logical_and, [i == 0 for i in ids])
        last = functools.reduce(jnp.logical_and, [i == g - 1 for i, g in zip(ids, grid)])

        @pl.when(first)
        def _():
            rider.start(*r_ins, *r_outs, *r_sems)

        body(*ins, *outs, *scr)

        @pl.when(last)
        def _():
            rider.finish(*r_ins, *r_outs, *r_sems)

    return (full, in_specs + [ANY] * ri, out_specs + [ANY] * ro, out_shape + rider.out_shapes, scratch + rider.sems,
            rider.arrays)


def _all_gather_rider(blk):
    M, N = blk.shape

    def copies(x_ref, out_ref, send_sems, recv_sems, local_sem):
        x, y, c = _position()
        me, sibling = (x, y, c), (x, y, 1 - c)
        chips = [(1 - x, y), (x, 1 - y), (1 - x, 1 - y)]

        def slab(px, py, pc):
            return out_ref.at[4 * px + 2 * py + pc]

        def copy(k, block, to, src=None):
            return pltpu.make_async_remote_copy(
                src_ref=slab(*block) if src is None else src, dst_ref=slab(*block),
                send_sem=send_sems.at[k], recv_sem=recv_sems.at[k], device_id=to, device_id_type=MESH)

        mine = pltpu.make_async_copy(x_ref, slab(*me), local_sem)
        first = [copy(0, me, sibling, src=x_ref)]
        first += [copy(1 + j, me, (*chip, c), src=x_ref) for j, chip in enumerate(chips)]
        passed = [copy(4 + j, (*chip, c), sibling) for j, chip in enumerate(chips)]
        arrivals = [copy(1 + j, (*chip, c), me) for j, chip in enumerate(chips)]
        from_sibling = [copy(0, sibling, me)] + [copy(4 + j, (*chip, 1 - c), me) for j, chip in enumerate(chips)]
        return mine, first, passed, arrivals, from_sibling

    def start(*refs):
        mine, first, _, _, _ = copies(*refs)
        mine.start()
        for cp in first:
            cp.start()

    def finish(*refs):
        mine, first, passed, arrivals, from_sibling = copies(*refs)
        for arrival, onward in zip(arrivals, passed):
            arrival.wait_recv()
            onward.start()
        for cp in from_sibling:
            cp.wait_recv()
        for cp in first + passed:
            cp.wait_send()
        mine.wait()

    return _Rider([blk], [jax.ShapeDtypeStruct((N_DEV, M, N), blk.dtype)],
                  [pltpu.SemaphoreType.DMA((7,)), pltpu.SemaphoreType.DMA((7,)), pltpu.SemaphoreType.DMA], start, finish)


def _sibling_half_rider(g):
    n, R, C = g.shape
    half = R // 2

    def copy(g_ref, out_ref, send_sem, recv_sem):
        x, y, c = _position()
        return pltpu.make_async_remote_copy(
            src_ref=g_ref.at[:, pl.ds((1 - c) * half, half), :], dst_ref=out_ref, send_sem=send_sem, recv_sem=recv_sem,
            device_id=(x, y, 1 - c), device_id_type=MESH)

    return _Rider([g], [jax.ShapeDtypeStruct((n, half, C), g.dtype)], [pltpu.SemaphoreType.DMA, pltpu.SemaphoreType.DMA],
                  lambda *refs: copy(*refs).start(), lambda *refs: copy(*refs).wait())


def _sibling_send_half(g, name):
    return _standalone(_sibling_half_rider(g), name)[0]


def _chip_exchange_rider(p, row_off=0, rows=None):
    rows = p.shape[1] if rows is None else rows

    def copies(p_ref, out_ref, send_sems, recv_sems, local_sem):
        x, y, c = _position()
        my = 2 * x + y
        chips = [(1 - x, y), (x, 1 - y), (1 - x, 1 - y)]

        def src(slab):
            return p_ref.at[slab, pl.ds(row_off, rows), :]

        mine = pltpu.make_async_copy(src(my), out_ref.at[my], local_sem)
        sends = [pltpu.make_async_remote_copy(
            src_ref=src(2 * px + py), dst_ref=out_ref.at[my], send_sem=send_sems.at[k], recv_sem=recv_sems.at[k],
            device_id=(px, py, c), device_id_type=MESH) for k, (px, py) in enumerate(chips)]
        arrivals = [pltpu.make_async_remote_copy(
            src_ref=src(my), dst_ref=out_ref.at[2 * px + py], send_sem=send_sems.at[k], recv_sem=recv_sems.at[k],
            device_id=(px, py, c), device_id_type=MESH) for k, (px, py) in enumerate(chips)]
        return mine, sends, arrivals

    def start(*refs):
        mine, sends, _ = copies(*refs)
        mine.start()
        for cp in sends:
            cp.start()

    def finish(*refs):
        mine, sends, arrivals = copies(*refs)
        for cp in arrivals:
            cp.wait_recv()
        for cp in sends:
            cp.wait_send()
        mine.wait()

    return _Rider([p], [jax.ShapeDtypeStruct((p.shape[0], rows, p.shape[2]), p.dtype)],
                  [pltpu.SemaphoreType.DMA((3,)), pltpu.SemaphoreType.DMA((3,)), pltpu.SemaphoreType.DMA], start, finish)


def _sibling_fill(buf, name):
    M = buf.shape[0] // 2

    def body(buf_ref, out_ref, send_sem, recv_sem):
        x, y, c = _position()
        mine = pl.ds(c * M, M)
        theirs = pl.ds((1 - c) * M, M)
        pltpu.make_async_remote_copy(src_ref=buf_ref.at[mine], dst_ref=out_ref.at[mine], send_sem=send_sem,
                                     recv_sem=recv_sem, device_id=(x, y, 1 - c), device_id_type=MESH).start()
        pltpu.make_async_remote_copy(src_ref=buf_ref.at[mine], dst_ref=out_ref.at[theirs], send_sem=send_sem,
                                     recv_sem=recv_sem, device_id=(x, y, 1 - c), device_id_type=MESH).wait()

    return pl.pallas_call(
        body, name=name, in_specs=[ANY], out_specs=ANY, out_shape=jax.ShapeDtypeStruct(buf.shape, buf.dtype),
        scratch_shapes=[pltpu.SemaphoreType.DMA, pltpu.SemaphoreType.DMA], input_output_aliases={0: 0},
    )(buf)


def _pack_rest(attn_out, ssm_in, glu_w, ssm_out):
    hd = ssm_in.shape[0] // 2
    return jnp.concatenate([attn_out, jnp.concatenate([ssm_in[:hd], ssm_in[hd:]], axis=1), glu_w, ssm_out], axis=0)


def _unpack_rest(p):
    D = p.shape[-1]
    q, hd = D // N_CHIPS, D // 2
    o = [0, q, q + hd, 2 * q + hd, 3 * q + hd]
    ssm_in = p[o[1]:o[2]]
    return p[o[0]:o[1]], jnp.concatenate([ssm_in[:, :hd], ssm_in[:, hd:]], axis=0), p[o[2]:o[3]], p[o[3]:o[4]]


def _attn_in_views(buf_a):
    D = buf_a.shape[-1]
    return [_View(buf_a, (D, D), lambda r, c, j=j: (j, r, c), D, D) for j in range(N_CHIPS)]


def _rest_views(buf_b):
    D = buf_b.shape[-1]
    q, hd = D // N_CHIPS, D // 2
    o_out, o_in, o_glu, o_sout = 0, q, q + hd, 2 * q + hd

    def row_sharded(off):
        return _View(buf_b, (D, D), lambda r, c: (r // q, off + r % q, c), math.gcd(q, off), D)

    def ssm_in(part):
        return _View(buf_b, (D, D), lambda r, c: (2 * part + c // hd, o_in + r % hd, (r // hd) * hd + c % hd),
                     math.gcd(hd, o_in), hd)

    return dict(attn_out=row_sharded(o_out), ssm_in_u=ssm_in(0), ssm_in_gate=ssm_in(1), glu=row_sharded(o_glu),
                ssm_out=row_sharded(o_sout))


def _pack_small(parts):
    flat = jnp.concatenate([p.reshape(-1) for p in parts])
    pad = (-flat.size) % (2 * SUBLANES * LANES)
    return jnp.pad(flat, (0, pad)).reshape(-1, LANES)


def _unpack_small(buf, shapes):
    flat = buf.reshape(-1)
    out, off = [], 0
    for s in shapes:
        n = math.prod(s)
        out.append(flat[off:off + n].reshape(s))
        off += n
    return out


def _local_step(x, target, norm_g, q_g, k_g, wa, wb, ssm_small, core=None):
    Bl, S, D = x.shape
    T = Bl * S
    x0 = x.reshape(T, D)
    tgt = target.reshape(T, D)
    a_re, a_im, log_dt, b_re, b_im, c_re, c_im, d_skip, glu_b = ssm_small
    G = a_re.shape[0]
    mats = _ssm_matrices(a_re, a_im, log_dt, b_re, b_im, c_re, c_im, d_skip, S)
    w_in = _attn_in_views(wa)
    rows_b = wb.shape[-2] * (1 if core is None else 2)
    ga = lax.empty(wa.shape, F32)
    gb = lax.empty((N_CHIPS, rows_b, D), F32)

    def wgrad(a, b, key, name):
        return _matmul(a, b, name=name, ta=True, out=_rest_views(gb)[key])

    h0 = _rmsnorm_fwd(x0, norm_g[0], "norm0_fwd")
    q, k, v, gate = [_matmul(h0, w_in[j], name=f"attn_in_{j}", out_dtype=(BF16 if j == 2 else F32)) for j in range(4)]
    qn = _qknorm_fwd(q, q_g, "qnorm_fwd")
    kn = _qknorm_fwd(k, k_g, "knorm_fwd")
    if core is None:
        o, btot, og = _attn_fwd(qn, kn, v, gate, Bl, S)
    else:
        o, btot, og, wb = _attn_fwd(qn, kn, v, gate, Bl, S, rider=_all_gather_rider(wb))
        wb = wb.reshape(N_CHIPS, rows_b, D)
    w = _rest_views(wb)
    x1 = _matmul(og, w["attn_out"], name="attn_out", residual=x0)

    h1 = _rmsnorm_fwd(x1, norm_g[1], "norm1_fwd")
    u = _matmul(h1, w["ssm_in_u"], name="ssm_in_u")
    gate2 = _matmul(h1, w["ssm_in_gate"], name="ssm_in_gate")
    y, hs_re, hs_im, yg = _ssm_fwd(u, mats, Bl, S)
    gl, y3 = _matmul(yg, w["glu"], name="glu_mm", epilogue=_glu_fwd(y, gate2, glu_b))
    x2 = _matmul(y3, w["ssm_out"], name="ssm_out", residual=x1)

    dx2, dx2b, loss = _loss_head(x2, tgt)

    dy3 = _matmul(dx2b, w["ssm_out"], name="ssm_out_dgrad", tb=True)
    gb = wgrad(y3, dx2b, "ssm_out", "ssm_out_wgrad")
    dgl, dgate2, t1, dglu_b = _glu_bwd(dy3, y, gl, gate2, glu_b)
    dy = _matmul(dgl, w["glu"], name="glu_dgrad", tb=True, epilogue=_gelu_bwd(t1, y))
    gb = wgrad(yg, dgl, "glu", "glu_wgrad")
    du, dbr, dbi, dcr, dci, dlr, dli, dd = _ssm_bwd(u, dy, hs_re, hs_im, mats, Bl, S)
    dh1 = _matmul(du, w["ssm_in_u"], name="ssm_in_dgrad_u", tb=True)
    dh1 = _matmul(dgate2, w["ssm_in_gate"], name="ssm_in_dgrad_gate", tb=True, residual=dh1)
    gb = wgrad(h1, du, "ssm_in_u", "ssm_in_wgrad_u")
    gb = wgrad(h1, dgate2, "ssm_in_gate", "ssm_in_wgrad_gate")
    dx1, dx1b, dng1 = _rmsnorm_bwd(x1, norm_g[1], dh1, dx2, "norm1_bwd")
    small_early = (dng1,) + _ssm_unblock(dbr, dbi, dcr, dci, dlr, dli, dd, G) + (dglu_b.reshape(D),)

    gb = wgrad(og, dx1b, "attn_out", "attn_out_wgrad")
    if core is None:
        do, dgate = _matmul(dx1b, w["attn_out"], name="attn_out_dgrad", tb=True, epilogue=_attn_gate_bwd(o, gate))
        dqn, dkn, dv = _attn_bwd(qn, kn, v, do, btot, Bl, S)
    else:
        do, dgate, from_sibling = _matmul(dx1b, w["attn_out"], name="attn_out_dgrad", tb=True,
                                          epilogue=_attn_gate_bwd(o, gate), rider=_sibling_half_rider(gb))
        chip_sum_b = _add_halves((gb, from_sibling), core, "grads_b_add_halves")
        dqn, dkn, dv, gb, small_early = _attn_bwd(
            qn, kn, v, do, btot, Bl, S,
            rider=_join_riders(_chip_exchange_rider(chip_sum_b), _all_gather_rider(_pack_small(small_early))))
    dq, dqg = _qknorm_bwd(q, q_g, dqn, "qnorm_bwd")
    dk, dkg = _qknorm_bwd(k, k_g, dkn, "knorm_bwd")
    dproj = [dq, dk, dv, dgate]
    for j in range(4):
        ga = _matmul(h0, dproj[j], name=f"attn_in_wgrad_{j}", ta=True, out=_attn_in_views(ga)[j])
    dh0 = None
    if core is None:
        for j in range(4):
            dh0 = _matmul(dproj[j], w_in[j], name=f"attn_in_dgrad_{j}", tb=True, residual=dh0)
    else:
        chip_sum_a = _add_halves((ga, _sibling_send_half(ga, "grads_a_sibling_half")), core, "grads_a_add_halves")
        rows = chip_sum_a.shape[1] // 4
        parts = []
        for j in range(4):
            dh0, part = _matmul(dproj[j], w_in[j], name=f"attn_in_dgrad_{j}", tb=True, residual=dh0,
                                rider=_chip_exchange_rider(chip_sum_a, j * rows, rows))
            parts.append(part)
        ga = jnp.concatenate(parts, axis=1)
    dx0, _, dng0 = _rmsnorm_bwd(x0, norm_g[0], dh0, dx1, "norm0_bwd")

    return loss, dx0.reshape(Bl, S, D), ga, gb, small_early, (dng0, dqg, dkg)


def _small_early_shapes(D):
    G = D // GROUP
    return [(1, D), (G, STATE), (G, STATE), (G, STATE, GROUP), (G, STATE, GROUP), (G, GROUP, STATE), (G, GROUP, STATE),
            (D,), (D,)]


def _chip_rows(a, chip, n_per):
    return lax.dynamic_slice_in_dim(a, chip * n_per, n_per, axis=0)


def kernel(x, norm_g, attn_w_in, attn_q_g, attn_k_g, attn_w_out, ssm_w_in, ssm_A_re, ssm_A_im, ssm_log_dt, ssm_B_re, ssm_B_im, ssm_C_re, ssm_C_im, ssm_D, ssm_glu_w, ssm_glu_b, ssm_w_out, loss_target, m_norm_g, m_attn_w_in, m_attn_q_g, m_attn_k_g, m_attn_w_out, m_ssm_w_in, m_ssm_A_re, m_ssm_A_im, m_ssm_log_dt, m_ssm_B_re, m_ssm_B_im, m_ssm_C_re, m_ssm_C_im, m_ssm_D, m_ssm_glu_w, m_ssm_glu_b, m_ssm_w_out, v_norm_g, v_attn_w_in, v_attn_q_g, v_attn_k_g, v_attn_w_out, v_ssm_w_in, v_ssm_A_re, v_ssm_A_im, v_ssm_log_dt, v_ssm_B_re, v_ssm_B_im, v_ssm_C_re, v_ssm_C_im, v_ssm_D, v_ssm_glu_w, v_ssm_glu_b, v_ssm_w_out):
    D = x.shape[-1]
    cx, cy, cc = _position()
    chip = 2 * cx + cy
    G = D // GROUP
    Gl = G // N_CHIPS

    def my_half(a):
        return lax.dynamic_slice_in_dim(a, cc * (a.shape[0] // 2), a.shape[0] // 2, axis=0)

    wa = _all_gather8(my_half(attn_w_in[0].astype(BF16)), "attn_in_all_gather").reshape(N_CHIPS, D, D)
    wb_half = my_half(_pack_rest(attn_w_out[0], ssm_w_in[0], ssm_glu_w[0], ssm_w_out[0]).astype(BF16))

    ssm_local = [ssm_A_re[0], ssm_A_im[0], ssm_log_dt[0], ssm_B_re[0], ssm_B_im[0], ssm_C_re[0], ssm_C_im[0], ssm_D[0],
                 ssm_glu_b[0]]
    small_local = _pack_small(ssm_local)
    half_rows = small_local.shape[0] // 2
    small_half = lax.dynamic_slice_in_dim(small_local, cc * half_rows, half_rows, axis=0)
    small_all = _all_gather8(small_half, "ssm_params_all_gather").reshape(N_CHIPS, 2 * half_rows, LANES)
    per_chip = [_unpack_small(small_all[j], [p.shape for p in ssm_local]) for j in range(N_CHIPS)]
    ssm_full = [jnp.concatenate([per_chip[j][i] for j in range(N_CHIPS)], axis=0) for i in range(len(ssm_local))]

    loss, grad_x, gathered_a, gathered_b, small_early, small_late = _local_step(
        x, loss_target, norm_g, attn_q_g[0], attn_k_g[0], wa, wb_half, ssm_full, core=cc)
    loss = lax.psum(loss[0, 0], ("x", "y", "c"))

    q4 = D // N_CHIPS
    def both_halves(gathered, tag):
        return _sibling_fill(_sum_leading(gathered, f"grads_{tag}_sum_chips", half=cc), f"grads_{tag}_sibling_fill")

    gd = (both_halves(gathered_a, "a"),) + _unpack_rest(both_halves(gathered_b, "b"))

    (dng1, dl_re, dl_im, dbb_re, dbb_im, dc_re, dc_im, dd_skip, dglu_b) = _unpack_small(
        _sum_leading(small_early, "small_early_sum"), _small_early_shapes(D))
    late_all = _all_gather8(_pack_small(small_late), "small_late_all_gather")
    dng0, dqg, dkg = _unpack_small(_sum_leading(late_all, "small_late_sum"), [s.shape for s in small_late])
    dng = jnp.concatenate([dng0, dng1], axis=0)
    a_re, a_im, log_dt, b_re, b_im = ssm_local[:5]
    _, zoh_vjp = jax.vjp(_zoh, a_re, a_im, log_dt, b_re, b_im)
    da_re, da_im, dlog_dt, db_re, db_im = zoh_vjp((_chip_rows(dl_re, chip, Gl), _chip_rows(dl_im, chip, Gl),
                                                   _chip_rows(dbb_re, chip, Gl), _chip_rows(dbb_im, chip, Gl)))
    grads = {
        "norm_g": dng, "attn_w_in": gd[0][None], "attn_q_g": dqg, "attn_k_g": dkg, "attn_w_out": gd[1][None],
        "ssm_w_in": gd[2][None], "ssm_A_re": da_re[None], "ssm_A_im": da_im[None], "ssm_log_dt": dlog_dt[None],
        "ssm_B_re": db_re[None], "ssm_B_im": db_im[None], "ssm_C_re": _chip_rows(dc_re, chip, Gl)[None],
        "ssm_C_im": _chip_rows(dc_im, chip, Gl)[None], "ssm_D": _chip_rows(dd_skip, chip, q4)[None],
        "ssm_glu_w": gd[3][None], "ssm_glu_b": _chip_rows(dglu_b, chip, q4)[None], "ssm_w_out": gd[4][None],
    }
    weights = dict(norm_g=norm_g, attn_w_in=attn_w_in, attn_q_g=attn_q_g, attn_k_g=attn_k_g, attn_w_out=attn_w_out,
                   ssm_w_in=ssm_w_in, ssm_A_re=ssm_A_re, ssm_A_im=ssm_A_im, ssm_log_dt=ssm_log_dt, ssm_B_re=ssm_B_re,
                   ssm_B_im=ssm_B_im, ssm_C_re=ssm_C_re, ssm_C_im=ssm_C_im, ssm_D=ssm_D, ssm_glu_w=ssm_glu_w,
                   ssm_glu_b=ssm_glu_b, ssm_w_out=ssm_w_out)
    m = dict(norm_g=m_norm_g, attn_w_in=m_attn_w_in, attn_q_g=m_attn_q_g, attn_k_g=m_attn_k_g, attn_w_out=m_attn_w_out,
             ssm_w_in=m_ssm_w_in, ssm_A_re=m_ssm_A_re, ssm_A_im=m_ssm_A_im, ssm_log_dt=m_ssm_log_dt, ssm_B_re=m_ssm_B_re,
             ssm_B_im=m_ssm_B_im, ssm_C_re=m_ssm_C_re, ssm_C_im=m_ssm_C_im, ssm_D=m_ssm_D, ssm_glu_w=m_ssm_glu_w,
             ssm_glu_b=m_ssm_glu_b, ssm_w_out=m_ssm_w_out)
    v = dict(norm_g=v_norm_g, attn_w_in=v_attn_w_in, attn_q_g=v_attn_q_g, attn_k_g=v_attn_k_g, attn_w_out=v_attn_w_out,
             ssm_w_in=v_ssm_w_in, ssm_A_re=v_ssm_A_re, ssm_A_im=v_ssm_A_im, ssm_log_dt=v_ssm_log_dt, ssm_B_re=v_ssm_B_re,
             ssm_B_im=v_ssm_B_im, ssm_C_re=v_ssm_C_re, ssm_C_im=v_ssm_C_im, ssm_D=v_ssm_D, ssm_glu_w=v_ssm_glu_w,
             ssm_glu_b=v_ssm_glu_b, ssm_w_out=v_ssm_w_out)
    names = list(weights)
    dense_names = ("attn_w_in", "attn_w_out", "ssm_w_in", "ssm_glu_w", "ssm_w_out")
    delta, new_m, new_v = {}, {}, {}
    for n in dense_names:
        delta[n], new_m[n], new_v[n] = _adamw(weights[n], grads[n], m[n], v[n], "adamw_" + n)
    small_names = [n for n in names if n not in dense_names]
    small_shapes = [weights[n].shape for n in small_names]
    packs = [_pack_small([d[n] for n in small_names]) for d in (weights, grads, m, v)]
    outs = _adamw(*packs, "adamw_small")
    for res, out in zip((delta, new_m, new_v), outs):
        for n, val in zip(small_names, _unpack_small(out, small_shapes)):
            res[n] = val
    return (loss, grad_x, *[grads[n] for n in names], *[delta[n] for n in names], *[new_m[n] for n in names],
            *[new_v[n] for n in names])
```

```python
import functools
import math

import jax
import jax.numpy as jnp
from jax import lax
from jax.experimental import pallas as pl
from jax.experimental.pallas import tpu as pltpu

F32 = jnp.float32
BF16 = jnp.bfloat16

HEAD_DIM = 128
GROUP = 16
STATE = 64
RMS_EPS = 1e-6
ADAM_LR = 0.001
ADAM_B1 = 0.9
ADAM_B2 = 0.999
ADAM_EPS = 1e-08
ADAM_WD = 0.01
ADAM_STEP = 10

N_CHIPS = 4
N_DEV = 8
SUBLANES = 8
LANES = 128
VMEM_LIMIT = 56 * 1024 * 1024
ROW_BLOCK_ELEMS = 1 << 19
MATMUL_TILE = 1024
MATMUL_PANEL_BYTES = 8 * 1024 * 1024
MESH = pl.DeviceIdType.MESH

NN = (((1,), (0,)), ((), ()))
NT = (((1,), (1,)), ((), ()))
TN = (((0,), (0,)), ((), ()))


def _params(*sem):
    return pltpu.CompilerParams(dimension_semantics=sem, vmem_limit_bytes=VMEM_LIMIT)


def _dot(a, b, dims=NN):
    return lax.dot_general(a, b, dims, preferred_element_type=F32)


def _split(a):
    hi = a.astype(BF16)
    lo = (a - hi.astype(F32)).astype(BF16)
    return hi, lo


def _dot_f32(a, b, dims=NN):
    return _dot(a.astype(BF16), b.astype(BF16), dims)


def _sigmoid(x):
    return 1.0 / (1.0 + jnp.exp(-x))


def _silu_parts(x):
    s = _sigmoid(x)
    return x * s, s * (1.0 + x * (1.0 - s))


_GELU_C = math.sqrt(2.0 / math.pi)


def _gelu_parts(x):
    x2 = x * x
    t = jnp.tanh(_GELU_C * (x + 0.044715 * x * x2))
    val = 0.5 * x * (1.0 + t)
    der = 0.5 * (1.0 + t) + 0.5 * x * (1.0 - t * t) * _GELU_C * (1.0 + 3.0 * 0.044715 * x2)
    return val, der


class _View:
    def __init__(self, buf, shape, locate, row_tile, col_tile):
        self.buf, self.shape, self.locate, self.row_tile, self.col_tile = buf, shape, locate, row_tile, col_tile

    def spec(self, t0, t1, block_of):
        def index(i, j, k):
            rb, cb = block_of(i, j, k)
            slab, r, c = self.locate(rb * t0, cb * t1)
            return slab, r // t0, c // t1
        return pl.BlockSpec((None, t0, t1), index)


def _operand(x):
    return (x.buf, x.shape, x.row_tile, x.col_tile) if isinstance(x, _View) else (x, x.shape, x.shape[0], x.shape[1])


class _Epilogue:
    def __init__(self, fn, tiles, rows, out_dtypes):
        self.fn, self.tiles, self.rows, self.out_dtypes = fn, tiles, rows, out_dtypes


def _matmul(a, b, *, name, ta=False, tb=False, residual=None, out_dtype=F32, out=None, rider=None, epilogue=None):
    if residual is not None:
        epilogue = _Epilogue(lambda r, res: (r + res.astype(F32),), [residual], [], [out_dtype])
    elif epilogue is None:
        epilogue = _Epilogue(lambda r: (r,), [], [], [out_dtype])
    n_extra, n_out = len(epilogue.tiles) + len(epilogue.rows), len(epilogue.out_dtypes)
    a_arr, a_shape, a_rt, a_ct = _operand(a)
    b_arr, b_shape, b_rt, b_ct = _operand(b)
    (K, M) = a_shape if ta else a_shape[::-1]
    N = b_shape[0] if tb else b_shape[1]
    a_mt, a_kt = (a_ct, a_rt) if ta else (a_rt, a_ct)
    b_nt, b_kt = (b_rt, b_ct) if tb else (b_ct, b_rt)
    k_cap = MATMUL_PANEL_BYTES // (MATMUL_TILE * max(a_arr.dtype.itemsize, b_arr.dtype.itemsize))
    n_cap = MATMUL_TILE if n_extra + n_out <= 2 else MATMUL_TILE // 2
    tm, tn, tk = min(M, MATMUL_TILE, a_mt), min(N, n_cap, b_nt), min(K, k_cap, a_kt)
    if out is not None:
        tm, tn = min(tm, out.row_tile), min(tn, out.col_tile)
    pk = min(tk, b_kt)
    pieces = tk // pk
    nk = K // tk
    dims = ((((0,) if ta else (1,)), ((1,) if tb else (0,))), ((), ()))
    n_in = 1 + pieces + n_extra + (out is not None)

    def body(*refs):
        a_ref, b_refs = refs[0], refs[1:1 + pieces]
        e_refs = refs[1 + pieces:1 + pieces + n_extra]
        o_refs = refs[n_in:n_in + n_out]

        def finish(r):
            for o_ref, val in zip(o_refs, epilogue.fn(r, *[e[...] for e in e_refs])):
                o_ref[...] = val.astype(o_ref.dtype)

        part = None
        for p, b_ref in enumerate(b_refs):
            ks = slice(p * pk, (p + 1) * pk)
            a_blk = a_ref[...] if pieces == 1 else (a_ref[ks, :] if ta else a_ref[:, ks])
            term = _dot(a_blk.astype(BF16), b_ref[...].astype(BF16), dims)
            part = term if part is None else part + term
        if nk == 1:
            finish(part)
            return
        acc = refs[n_in + n_out]
        k = pl.program_id(2)

        @pl.when(k == 0)
        def _():
            acc[...] = part

        @pl.when(k > 0)
        def _():
            acc[...] += part

        @pl.when(k == nk - 1)
        def _():
            finish(acc[...])

    def spec(x, t0, t1, block_of):
        if isinstance(x, _View):
            return x.spec(t0, t1, block_of)
        return pl.BlockSpec((t0, t1), block_of)

    a_spec = spec(a, tk, tm, lambda i, j, k: (k, i)) if ta else spec(a, tm, tk, lambda i, j, k: (i, k))
    b_specs = [spec(b, tn, pk, lambda i, j, k, p=p: (j, k * pieces + p)) if tb else
               spec(b, pk, tn, lambda i, j, k, p=p: (k * pieces + p, j)) for p in range(pieces)]
    tile_spec = pl.BlockSpec((tm, tn), lambda i, j, k: (i, j))
    in_specs = [a_spec] + b_specs + [tile_spec] * len(epilogue.tiles) + [
        pl.BlockSpec((1, tn), lambda i, j, k: (0, j))] * len(epilogue.rows)
    args = [a_arr] + [b_arr] * pieces + list(epilogue.tiles) + list(epilogue.rows)
    aliases = {}
    if out is None:
        out_specs = [tile_spec] * n_out
        out_shape = [jax.ShapeDtypeStruct((M, N), d) for d in epilogue.out_dtypes]
    else:
        out_specs = [out.spec(tm, tn, lambda i, j, k: (i, j))]
        out_shape = [jax.ShapeDtypeStruct(out.buf.shape, out.buf.dtype)]
        in_specs.append(pl.BlockSpec(memory_space=pl.ANY))
        args.append(out.buf)
        aliases = {len(args) - 1: 0}
    grid = (M // tm, N // tn, nk)
    scratch = [pltpu.VMEM((tm, tn), F32)] if nk > 1 else []
    body, in_specs, out_specs, out_shape, scratch, extra = _ride(rider, body, grid, in_specs, out_specs, out_shape,
                                                                 scratch)
    results = pl.pallas_call(
        body, name=name, grid=grid, in_specs=in_specs, out_specs=out_specs, out_shape=out_shape,
        scratch_shapes=scratch, input_output_aliases=aliases,
        compiler_params=_params(*(("parallel", "parallel", "arbitrary") if rider is None else ("arbitrary",) * 3)),
    )(*args, *extra)
    return results[0] if len(results) == 1 else results


def _row_tile(T, C):
    tr = max(SUBLANES, min(T, ROW_BLOCK_ELEMS // C) // SUBLANES * SUBLANES)
    while T % tr:
        tr -= SUBLANES
    return tr


def _rows_call(body, name, T, C, row_ins, full_ins, row_outs, acc_outs=()):
    tr = _row_tile(T, C)
    row_spec = pl.BlockSpec((tr, C), lambda i: (i, 0))
    in_specs = [row_spec] * len(row_ins) + [pl.BlockSpec(f.shape, lambda i, n=f.ndim: (0,) * n) for f in full_ins]
    out_specs = [row_spec] * len(row_outs) + [pl.BlockSpec(s, lambda i, n=len(s): (0,) * n) for s in acc_outs]
    out_shape = [jax.ShapeDtypeStruct((T, C), d) for d in row_outs] + [jax.ShapeDtypeStruct(s, F32) for s in acc_outs]
    return pl.pallas_call(
        body, name=name, grid=(T // tr,), in_specs=in_specs, out_specs=out_specs, out_shape=out_shape,
        compiler_params=_params("arbitrary" if acc_outs else "parallel"),
    )(*row_ins, *full_ins)


def _rmsnorm_fwd(x, g, name):
    T, C = x.shape

    def body(x_ref, g_ref, h_ref):
        xv = x_ref[...]
        r = lax.rsqrt(jnp.mean(xv * xv, axis=-1, keepdims=True) + RMS_EPS)
        h_ref[...] = ((xv * r) * g_ref[...]).astype(BF16)

    return _rows_call(body, name, T, C, [x], [g.reshape(1, C)], [BF16])[0]


def _rmsnorm_bwd(x, g, dh, dres, name):
    T, C = x.shape

    def body(x_ref, dh_ref, dres_ref, g_ref, dx_ref, dxb_ref, dg_ref):
        @pl.when(pl.program_id(0) == 0)
        def _():
            dg_ref[...] = jnp.zeros_like(dg_ref)

        xv = x_ref[...]
        dhv = dh_ref[...]
        r = lax.rsqrt(jnp.mean(xv * xv, axis=-1, keepdims=True) + RMS_EPS)
        xn = xv * r
        dg_ref[...] += jnp.sum(dhv * xn, axis=0, keepdims=True)
        dxn = dhv * g_ref[...]
        dx = dres_ref[...] + r * (dxn - xn * jnp.mean(dxn * xn, axis=-1, keepdims=True))
        dx_ref[...] = dx
        dxb_ref[...] = dx.astype(BF16)

    return _rows_call(body, name, T, C, [x, dh, dres], [g.reshape(1, C)], [F32, BF16], [(1, C)])


def _heads(C):
    return [slice(h * HEAD_DIM, (h + 1) * HEAD_DIM) for h in range(C // HEAD_DIM)]


def _qknorm_fwd(q, g, name):
    T, C = q.shape

    def body(q_ref, g_ref, o_ref):
        for head in _heads(C):
            xv = q_ref[:, head]
            r = lax.rsqrt(jnp.mean(xv * xv, axis=-1, keepdims=True) + RMS_EPS)
            o_ref[:, head] = ((xv * r) * g_ref[...]).astype(BF16)

    return _rows_call(body, name, T, C, [q], [g.reshape(1, HEAD_DIM)], [BF16])[0]


def _qknorm_bwd(q, g, dqn, name):
    T, C = q.shape

    def body(q_ref, d_ref, g_ref, dq_ref, dg_ref):
        @pl.when(pl.program_id(0) == 0)
        def _():
            dg_ref[...] = jnp.zeros_like(dg_ref)

        dg = jnp.zeros((1, HEAD_DIM), F32)
        for head in _heads(C):
            xv = q_ref[:, head]
            dv = d_ref[:, head]
            r = lax.rsqrt(jnp.mean(xv * xv, axis=-1, keepdims=True) + RMS_EPS)
            xn = xv * r
            dg = dg + jnp.sum(dv * xn, axis=0, keepdims=True)
            dxn = dv * g_ref[...]
            dq_ref[:, head] = (r * (dxn - xn * jnp.mean(dxn * xn, axis=-1, keepdims=True))).astype(BF16)
        dg_ref[...] += dg

    return _rows_call(body, name, T, C, [q, dqn], [g.reshape(1, HEAD_DIM)], [BF16], [(1, HEAD_DIM)])


ATT_TQ = 512
ATT_TK = 256
ATT_HEADS = 2


def _logsig_pair(z):
    a = jnp.minimum(z, 0.0) - jnp.log(1.0 + jnp.exp(-jnp.abs(z)))
    return a, a - z


def _tri(n, strict_upper_src):
    j = lax.broadcasted_iota(jnp.int32, (n, n), 0)
    s = lax.broadcasted_iota(jnp.int32, (n, n), 1)
    if strict_upper_src == "gt":
        m = j > s
    elif strict_upper_src == "le":
        m = j <= s
    else:
        m = j < s
    return jnp.where(m, 1.0, 0.0).astype(BF16)


def _cumdot(x, tri):
    hi, lo = _split(x)
    return _dot(hi, tri) + _dot(lo, tri)


def _attn_tiles(S):
    tq, tk = min(ATT_TQ, S), min(ATT_TK, S)
    return tq, tk, S // tq, tq // tk


def _attn_fwd(qn, kn, v, gate, Bl, S, rider=None):
    T, C = qn.shape
    H = C // HEAD_DIM
    tq, tk, nq, kpq = _attn_tiles(S)
    hp = min(ATT_HEADS, H)
    scale = 1.0 / math.sqrt(HEAD_DIM)

    def body(q_ref, k_ref, v_ref, g_ref, o_ref, bt_ref, og_ref):
        i = pl.program_id(2)
        tri = _tri(tk, "gt")
        rowpos = lax.broadcasted_iota(jnp.int32, (tq, tk), 0) + i * tq
        colpos = lax.broadcasted_iota(jnp.int32, (tq, tk), 1)
        o_ref[...] = jnp.zeros_like(o_ref)
        bt_ref[...] = jnp.zeros_like(bt_ref)

        def group(g, masked):
            work = []
            for p in reversed(range(kpq)):
                j = g * kpq + p
                rows = pl.ds(pl.multiple_of(j * tk, tk), tk)
                mask = (colpos + j * tk) < rowpos if masked else None
                for head in _heads(hp * HEAD_DIM):
                    a, b = _logsig_pair(_dot(q_ref[:, head], k_ref[rows, head], NT) * scale)
                    if masked:
                        b = jnp.where(mask, b, 0.0)
                    work.append((head, rows, a, b, _cumdot(b, tri), mask))
            for n, head in enumerate(_heads(hp * HEAD_DIM)):
                total = bt_ref[n]
                out = None
                for rows, a, b, suffix, mask in [w[1:] for w in work if w[0] == head]:
                    w = jnp.exp(a + suffix + total)
                    if masked:
                        w = jnp.where(mask, w, 0.0)
                    term = _dot(w.astype(BF16), v_ref[rows, head])
                    out = term if out is None else out + term
                    total = total + jnp.sum(b, axis=-1, keepdims=True)
                o_ref[:, head] += out
                bt_ref[n] = total

        group(i, True)

        def step(n, carry):
            group(i - 1 - n, False)
            return carry

        lax.fori_loop(0, i, step, 0)
        og_ref[...] = (o_ref[...] * _silu_parts(g_ref[...])[0]).astype(BF16)

    qspec = pl.BlockSpec((tq, hp * HEAD_DIM), lambda b, h, i: (b * nq + i, h))
    kspec = pl.BlockSpec((S, hp * HEAD_DIM), lambda b, h, i: (b, h))
    btspec = pl.BlockSpec((None, hp, tq, 1), lambda b, h, i: (b, h, i, 0))
    grid = (Bl, H // hp, nq)
    body, in_specs, out_specs, out_shape, scratch, extra = _ride(
        rider, body, grid, [qspec, kspec, kspec, qspec], [qspec, btspec, qspec],
        [jax.ShapeDtypeStruct((T, C), F32), jax.ShapeDtypeStruct((Bl, H, S, 1), F32),
         jax.ShapeDtypeStruct((T, C), BF16)], [])
    return pl.pallas_call(
        body, name="attn_fwd", grid=grid, in_specs=in_specs, out_specs=out_specs, out_shape=out_shape,
        scratch_shapes=scratch, compiler_params=_params("arbitrary", "arbitrary", "arbitrary"),
    )(qn, kn, v, gate, *extra)


def _attn_bwd(qn, kn, v, do, btot, Bl, S, rider=None):
    T, C = qn.shape
    H = C // HEAD_DIM
    tq, tk, nq, kpq = _attn_tiles(S)
    hp = min(ATT_HEADS, H)
    scale = 1.0 / math.sqrt(HEAD_DIM)

    def body(q_ref, k_ref, v_ref, do_ref, bt_ref, dq_ref, dk_ref, dvb_ref, pb_ref, pdl_ref, dv_ref):
        i = pl.program_id(2)

        @pl.when(i == 0)
        def _():
            dk_ref[...] = jnp.zeros_like(dk_ref)
            dv_ref[...] = jnp.zeros_like(dv_ref)

        tri_le = _tri(tk, "le")
        tri_lt = _tri(tk, "lt")
        rowpos = lax.broadcasted_iota(jnp.int32, (tq, tk), 0) + i * tq
        colpos = lax.broadcasted_iota(jnp.int32, (tq, tk), 1)
        dq_ref[...] = jnp.zeros_like(dq_ref)
        pb_ref[...] = bt_ref[...]
        pdl_ref[...] = jnp.zeros_like(pdl_ref)

        def group(g, masked):
            work = []
            for p in range(kpq):
                j = g * kpq + p
                rows = pl.ds(pl.multiple_of(j * tk, tk), tk)
                mask = (colpos + j * tk) < rowpos if masked else None
                for head in _heads(hp * HEAD_DIM):
                    a, b = _logsig_pair(_dot(q_ref[:, head], k_ref[rows, head], NT) * scale)
                    if masked:
                        b = jnp.where(mask, b, 0.0)
                    work.append((head, rows, a, b, _cumdot(b, tri_le), _dot(do_ref[:, head], v_ref[rows, head], NT),
                                 mask))
            for n, head in enumerate(_heads(hp * HEAD_DIM)):
                remaining = pb_ref[n]
                swept = pdl_ref[n]
                dq = None
                for rows, a, b, cum, dw, mask in [w[1:] for w in work if w[0] == head]:
                    w = jnp.exp(a + (remaining - cum))
                    if masked:
                        w = jnp.where(mask, w, 0.0)
                    dl = dw * w
                    prefix = swept + _cumdot(dl, tri_lt)
                    beta = jnp.exp(a)
                    dz = dl * (1.0 - beta) - beta * prefix
                    if masked:
                        dz = jnp.where(mask, dz, 0.0)
                    dzb = (dz * scale).astype(BF16)
                    term = _dot(dzb, k_ref[rows, head])
                    dq = term if dq is None else dq + term
                    dk_ref[rows, head] += _dot(dzb, q_ref[:, head], TN)
                    dv_ref[rows, head] += _dot(w.astype(BF16), do_ref[:, head], TN)
                    remaining = remaining - jnp.sum(b, axis=-1, keepdims=True)
                    swept = swept + jnp.sum(dl, axis=-1, keepdims=True)
                dq_ref[:, head] += dq
                pb_ref[n] = remaining
                pdl_ref[n] = swept

        def step(g, carry):
            group(g, False)
            return carry

        lax.fori_loop(0, i, step, 0)
        group(i, True)

        @pl.when(i == nq - 1)
        def _():
            dvb_ref[...] = dv_ref[...].astype(BF16)

    qspec = pl.BlockSpec((tq, hp * HEAD_DIM), lambda b, h, i: (b * nq + i, h))
    kspec = pl.BlockSpec((S, hp * HEAD_DIM), lambda b, h, i: (b, h))
    btspec = pl.BlockSpec((None, hp, tq, 1), lambda b, h, i: (b, h, i, 0))
    grid = (Bl, H // hp, nq)
    body, in_specs, out_specs, out_shape, scratch, extra = _ride(
        rider, body, grid, [qspec, kspec, kspec, qspec, btspec], [qspec, kspec, kspec],
        [jax.ShapeDtypeStruct((T, C), F32), jax.ShapeDtypeStruct((T, C), F32), jax.ShapeDtypeStruct((T, C), BF16)],
        [pltpu.VMEM((hp, tq, 1), F32), pltpu.VMEM((hp, tq, 1), F32), pltpu.VMEM((S, hp * HEAD_DIM), F32)])
    return pl.pallas_call(
        body, name="attn_bwd", grid=grid, in_specs=in_specs, out_specs=out_specs, out_shape=out_shape,
        scratch_shapes=scratch, compiler_params=_params("arbitrary", "arbitrary", "arbitrary"),
    )(qn, kn, v, do, btot, *extra)


SSM_TIME_BLOCK = 512
CHUNK = SUBLANES


def _cmadd(xr, xi, ar, ai, sr, si):
    return xr + ar * sr - ai * si, xi + ar * si + ai * sr


def _chunk_scan(xr, xi, tab_ref, cr, ci, reverse):
    for lvl, d in enumerate((1, 2, 4)):
        shift = (CHUNK - d) if reverse else d
        sr = pltpu.roll(xr, shift, 0)
        si = pltpu.roll(xi, shift, 0)
        ar = tab_ref[pl.ds((2 * lvl) * CHUNK, CHUNK), :]
        ai = tab_ref[pl.ds((2 * lvl + 1) * CHUNK, CHUNK), :]
        xr, xi = _cmadd(xr, xi, ar, ai, sr, si)
    pr = tab_ref[pl.ds(6 * CHUNK, CHUNK), :]
    pi = tab_ref[pl.ds(7 * CHUNK, CHUNK), :]
    return _cmadd(xr, xi, pr, pi, cr, ci)


def _ssm_dims(S, C):
    G = C // GROUP
    GT = min(16, G)
    return G, GT, G // GT, GT * GROUP, GT * STATE, min(SSM_TIME_BLOCK, S)


def _ssm_fwd(u, mats, Bl, S):
    T, C = u.shape
    G, GT, ngt, cw, sw, TB = _ssm_dims(S, C)
    ntb = S // TB
    nch = TB // CHUNK

    def body(u_ref, bre_ref, bim_ref, cre_ref, cim_ref, d_ref, tab_ref, y_ref, hr_ref, hi_ref, yg_ref, car_r, car_i):
        @pl.when(pl.program_id(2) == 0)
        def _():
            car_r[...] = jnp.zeros_like(car_r)
            car_i[...] = jnp.zeros_like(car_i)

        uv = u_ref[...]
        hr_ref[...] = _dot_f32(uv, bre_ref[...])
        hi_ref[...] = _dot_f32(uv, bim_ref[...])

        def step(n, carry):
            cr, ci = carry
            rows = pl.ds(pl.multiple_of(n * CHUNK, CHUNK), CHUNK)
            xr, xi = _chunk_scan(hr_ref[rows, :], hi_ref[rows, :], tab_ref, cr, ci, False)
            hr_ref[rows, :] = xr
            hi_ref[rows, :] = xi
            last = (CHUNK - 1, CHUNK)
            return (jnp.broadcast_to(xr[last[0]:last[1], :], xr.shape), jnp.broadcast_to(xi[last[0]:last[1], :], xi.shape))

        cr, ci = lax.fori_loop(0, nch, step, (car_r[...], car_i[...]))
        car_r[...] = cr
        car_i[...] = ci
        y = _dot_f32(hr_ref[...], cre_ref[...]) - _dot_f32(hi_ref[...], cim_ref[...]) + d_ref[...] * uv
        y_ref[...] = y
        yg_ref[...] = _gelu_parts(y)[0].astype(BF16)

    uspec = pl.BlockSpec((TB, cw), lambda g, b, t: (b * ntb + t, g))
    hspec = pl.BlockSpec((TB, sw), lambda g, b, t: (b * ntb + t, g))

    def gspec(r, c):
        return pl.BlockSpec((None, r, c), lambda g, b, t: (g, 0, 0))

    return pl.pallas_call(
        body, name="ssm_fwd", grid=(ngt, Bl, ntb),
        in_specs=[uspec, gspec(cw, sw), gspec(cw, sw), gspec(sw, cw), gspec(sw, cw), gspec(1, cw), gspec(8 * CHUNK, sw)],
        out_specs=[uspec, hspec, hspec, uspec],
        out_shape=[jax.ShapeDtypeStruct((T, C), F32), jax.ShapeDtypeStruct((T, G * STATE), F32),
                   jax.ShapeDtypeStruct((T, G * STATE), F32), jax.ShapeDtypeStruct((T, C), BF16)],
        scratch_shapes=[pltpu.VMEM((CHUNK, sw), F32), pltpu.VMEM((CHUNK, sw), F32)],
        compiler_params=_params("parallel", "arbitrary", "arbitrary"),
    )(u, mats["bbd_re"], mats["bbd_im"], mats["cbd_re"], mats["cbd_im"], mats["d"], mats["tab_fwd"])


def _ssm_bwd(u, dy, h_re, h_im, mats, Bl, S):
    T, C = u.shape
    G, GT, ngt, cw, sw, TB = _ssm_dims(S, C)
    ntb = S // TB
    nch = TB // CHUNK
    rpb = TB // CHUNK

    def body(u_ref, dy_ref, hr_ref, hi_ref, hpr_ref, hpi_ref, cre_ref, cim_ref, bre_ref, bim_ref, d_ref, tab_ref,
             du_ref, dbr_ref, dbi_ref, dcr_ref, dci_ref, dlr_ref, dli_ref, dd_ref, gr_ref, gi_ref, car_r, car_i):
        b = pl.program_id(1)
        t = pl.program_id(2)

        @pl.when((b == 0) & (t == 0))
        def _():
            for ref in (dbr_ref, dbi_ref, dcr_ref, dci_ref, dlr_ref, dli_ref, dd_ref):
                ref[...] = jnp.zeros_like(ref)

        @pl.when(t == 0)
        def _():
            car_r[...] = jnp.zeros_like(car_r)
            car_i[...] = jnp.zeros_like(car_i)

        uv = u_ref[...]
        dyv = dy_ref[...]
        gr_ref[...] = _dot_f32(dyv, cre_ref[...], NT)
        gi_ref[...] = -_dot_f32(dyv, cim_ref[...], NT)
        alive = jnp.where(t == ntb - 1, 0.0, 1.0)
        row0 = lax.broadcasted_iota(jnp.int32, (CHUNK, sw), 0) == 0

        def step(m, carry):
            cr, ci, ar, ai = carry
            n = nch - 1 - m
            rows = pl.ds(pl.multiple_of(n * CHUNK, CHUNK), CHUNK)
            prow = pl.ds(pl.multiple_of(jnp.maximum(n - 1, 0) * CHUNK, CHUNK), CHUNK)
            xr, xi = _chunk_scan(gr_ref[rows, :], gi_ref[rows, :], tab_ref, cr, ci, True)
            gr_ref[rows, :] = xr
            gi_ref[rows, :] = xi
            first = n == 0
            pr = jnp.where(first, hpr_ref[...] * alive, hr_ref[prow, :])
            pi = jnp.where(first, hpi_ref[...] * alive, hi_ref[prow, :])
            sr = jnp.where(row0, pltpu.roll(pr, 1, 0), pltpu.roll(hr_ref[rows, :], 1, 0))
            si = jnp.where(row0, pltpu.roll(pi, 1, 0), pltpu.roll(hi_ref[rows, :], 1, 0))
            ar = ar + xr * sr + xi * si
            ai = ai + xi * sr - xr * si
            return (jnp.broadcast_to(xr[0:1, :], xr.shape), jnp.broadcast_to(xi[0:1, :], xi.shape), ar, ai)

        zero = jnp.zeros((CHUNK, sw), F32)
        cr, ci, ar, ai = lax.fori_loop(0, nch, step, (car_r[...], car_i[...], zero, zero))
        car_r[...] = cr
        car_i[...] = ci
        dlr_ref[...] += ar
        dli_ref[...] += ai
        gr = gr_ref[...]
        gi = gi_ref[...]
        dbr_ref[...] += _dot_f32(uv, gr, TN)
        dbi_ref[...] += _dot_f32(uv, gi, TN)
        dcr_ref[...] += _dot_f32(hr_ref[...], dyv, TN)
        dci_ref[...] -= _dot_f32(hi_ref[...], dyv, TN)
        dd_ref[...] += jnp.sum(dyv * uv, axis=0, keepdims=True)
        du_ref[...] = (_dot_f32(gr, bre_ref[...], NT) + _dot_f32(gi, bim_ref[...], NT) + d_ref[...] * dyv).astype(BF16)

    def tblk(b, t):
        return b * ntb + (ntb - 1 - t)

    uspec = pl.BlockSpec((TB, cw), lambda g, b, t: (tblk(b, t), g))
    hspec = pl.BlockSpec((TB, sw), lambda g, b, t: (tblk(b, t), g))
    hpspec = pl.BlockSpec((CHUNK, sw), lambda g, b, t: (jnp.maximum(tblk(b, t) * rpb - 1, 0), g))

    def gspec(r, c):
        return pl.BlockSpec((None, r, c), lambda g, b, t: (g, 0, 0))

    def gshape(r, c):
        return jax.ShapeDtypeStruct((ngt, r, c), F32)

    return pl.pallas_call(
        body, name="ssm_bwd", grid=(ngt, Bl, ntb),
        in_specs=[uspec, uspec, hspec, hspec, hpspec, hpspec, gspec(sw, cw), gspec(sw, cw), gspec(cw, sw), gspec(cw, sw),
                  gspec(1, cw), gspec(8 * CHUNK, sw)],
        out_specs=[uspec, gspec(cw, sw), gspec(cw, sw), gspec(sw, cw), gspec(sw, cw), gspec(CHUNK, sw), gspec(CHUNK, sw),
                   gspec(1, cw)],
        out_shape=[jax.ShapeDtypeStruct((T, C), BF16), gshape(cw, sw), gshape(cw, sw), gshape(sw, cw), gshape(sw, cw),
                   gshape(CHUNK, sw), gshape(CHUNK, sw), gshape(1, cw)],
        scratch_shapes=[pltpu.VMEM((TB, sw), F32), pltpu.VMEM((TB, sw), F32), pltpu.VMEM((CHUNK, sw), F32),
                        pltpu.VMEM((CHUNK, sw), F32)],
        compiler_params=_params("arbitrary", "arbitrary", "arbitrary"),
    )(u, dy, h_re, h_im, h_re, h_im, mats["cbd_re"], mats["cbd_im"], mats["bbd_re"], mats["bbd_im"], mats["d"],
      mats["tab_rev"])


def _zoh(a_re, a_im, log_dt, b_re, b_im):
    dt = jnp.exp(log_dt)[:, None]
    mag = jnp.exp(a_re * dt)
    l_re = mag * jnp.cos(a_im * dt)
    l_im = mag * jnp.sin(a_im * dt)
    den = a_re * a_re + a_im * a_im
    f_re = ((l_re - 1.0) * a_re + l_im * a_im) / den
    f_im = (l_im * a_re - (l_re - 1.0) * a_im) / den
    bb_re = f_re[..., None] * b_re - f_im[..., None] * b_im
    bb_im = f_re[..., None] * b_im + f_im[..., None] * b_re
    return l_re, l_im, bb_re, bb_im


def _ssm_matrices(a_re, a_im, log_dt, b_re, b_im, c_re, c_im, d, S):
    G = a_re.shape[0]
    _, GT, ngt, cw, sw, _ = _ssm_dims(S, G * GROUP)
    _, _, bb_re, bb_im = _zoh(a_re, a_im, log_dt, b_re, b_im)
    eye = jnp.eye(GT, dtype=BF16)

    def bd_b(bb):
        return jnp.einsum("tgpi,gh->tgihp", bb.astype(BF16).reshape(ngt, GT, STATE, GROUP), eye).reshape(ngt, cw, sw)

    def bd_c(c):
        return jnp.einsum("tgip,gh->tgphi", c.astype(BF16).reshape(ngt, GT, GROUP, STATE), eye).reshape(ngt, sw, cw)

    dt = jnp.exp(log_dt)[:, None]

    def power(k, conj):
        mag = jnp.exp(k * a_re * dt)
        ang = k * a_im * dt
        return (mag * jnp.cos(ang)).reshape(ngt, 1, sw), ((-1.0 if conj else 1.0) * mag * jnp.sin(ang)).reshape(ngt, 1, sw)

    r = jnp.arange(CHUNK)[None, :, None]

    def table(reverse):
        parts = []
        for dd in (1, 2, 4):
            pr, pi = power(float(dd), reverse)
            keep = (r <= CHUNK - 1 - dd) if reverse else (r >= dd)
            parts += [jnp.where(keep, pr, 0.0), jnp.where(keep, pi, 0.0)]
        exps = [(CHUNK - k) if reverse else (k + 1) for k in range(CHUNK)]
        pw = [power(float(e), reverse) for e in exps]
        parts += [jnp.concatenate([p[0] for p in pw], axis=1), jnp.concatenate([p[1] for p in pw], axis=1)]
        return jnp.concatenate([jnp.broadcast_to(p, (ngt, CHUNK, sw)) for p in parts], axis=1)

    return dict(bbd_re=bd_b(bb_re), bbd_im=bd_b(bb_im), cbd_re=bd_c(c_re), cbd_im=bd_c(c_im), d=d.reshape(ngt, 1, cw),
                tab_fwd=table(False), tab_rev=table(True))


def _ssm_unblock(dbr, dbi, dcr, dci, dlr, dli, dd, G):
    ngt = dbr.shape[0]
    GT = G // ngt
    eye = jnp.eye(GT, dtype=F32)

    def ub(x):
        return jnp.einsum("tgihp,gh->tgpi", x.reshape(ngt, GT, GROUP, GT, STATE), eye).reshape(G, STATE, GROUP)

    def uc(x):
        return jnp.einsum("tgphi,gh->tgip", x.reshape(ngt, GT, STATE, GT, GROUP), eye).reshape(G, GROUP, STATE)

    return (dlr.sum(axis=1).reshape(G, STATE), dli.sum(axis=1).reshape(G, STATE), ub(dbr), ub(dbi), uc(dcr), uc(dci),
            dd.reshape(G * GROUP))


def _attn_gate_bwd(o, gate):
    def fn(dog, o_blk, g_blk):
        val, der = _silu_parts(g_blk)
        return dog * val, dog * o_blk * der

    return _Epilogue(fn, [o, gate], [], [BF16, BF16])


def _glu_fwd(y, gate, glu_b):
    def fn(gl, y_blk, g_blk, b_blk):
        return gl, _gelu_parts(y_blk)[0] * _sigmoid(gl + b_blk) * _silu_parts(g_blk)[0]

    return _Epilogue(fn, [y, gate], [glu_b.reshape(1, -1)], [F32, BF16])


def _glu_bwd(dy3, y, gl, gate, glu_b):
    T, C = y.shape

    def body(d_ref, y_ref, gl_ref, g_ref, b_ref, dgl_ref, dgate_ref, t1_ref, db_ref):
        @pl.when(pl.program_id(0) == 0)
        def _():
            db_ref[...] = jnp.zeros_like(db_ref)

        yg = _gelu_parts(y_ref[...])[0]
        sg = _sigmoid(gl_ref[...] + b_ref[...])
        sl, sld = _silu_parts(g_ref[...])
        dv = d_ref[...]
        dy2 = dv * sl
        dgl = dy2 * yg * sg * (1.0 - sg)
        dgl_ref[...] = dgl.astype(BF16)
        dgate_ref[...] = (dv * (yg * sg) * sld).astype(BF16)
        t1_ref[...] = dy2 * sg
        db_ref[...] += jnp.sum(dgl, axis=0, keepdims=True)

    return _rows_call(body, "glu_bwd", T, C, [dy3, y, gl, gate], [glu_b.reshape(1, C)], [BF16, BF16, F32], [(1, C)])


def _gelu_bwd(t1, y):
    return _Epilogue(lambda t2, t1_blk, y_blk: ((t1_blk + t2) * _gelu_parts(y_blk)[1],), [t1, y], [], [F32])


def _loss_head(x2, target):
    T, C = x2.shape

    def body(x_ref, t_ref, d_ref, db_ref, l_ref):
        @pl.when(pl.program_id(0) == 0)
        def _():
            l_ref[...] = jnp.zeros_like(l_ref)

        e = x_ref[...] - t_ref[...]
        d = e * (1.0 / C)
        d_ref[...] = d
        db_ref[...] = d.astype(BF16)
        l_ref[...] += 0.5 * jnp.sum(jnp.sum(e * e, axis=-1, keepdims=True) * (1.0 / C), axis=0, keepdims=True)

    return _rows_call(body, "loss_head", T, C, [x2, target], [], [F32, BF16], [(1, 1)])


def _adamw(w, g, m, v, name):
    shape = w.shape
    C = shape[-1]
    R = w.size // C
    bc1 = 1.0 - ADAM_B1 ** ADAM_STEP
    bc2 = 1.0 - ADAM_B2 ** ADAM_STEP

    def body(w_ref, g_ref, m_ref, v_ref, d_ref, nm_ref, nv_ref):
        gv = g_ref[...]
        mn = ADAM_B1 * m_ref[...] + (1.0 - ADAM_B1) * gv
        vn = ADAM_B2 * v_ref[...] + (1.0 - ADAM_B2) * (gv * gv)
        d_ref[...] = -ADAM_LR * ((mn / bc1) / (jnp.sqrt(vn / bc2) + ADAM_EPS) + ADAM_WD * w_ref[...])
        nm_ref[...] = mn
        nv_ref[...] = vn

    outs = _rows_call(body, name, R, C, [a.reshape(R, C) for a in (w, g, m, v)], [], [F32, F32, F32])
    return [o.reshape(shape) for o in outs]


def _sum_leading(x, name, half=None):
    n, R, C = x.shape
    tr = _row_tile(R, C * n)
    nb = R // tr

    def body(*refs):
        x_ref, o_ref = refs[-2:]
        acc = x_ref[0].astype(F32)
        for k in range(1, n):
            acc = acc + x_ref[k].astype(F32)
        o_ref[...] = acc

    if half is None:
        return pl.pallas_call(
            body, name=name, grid=(nb,), in_specs=[pl.BlockSpec((n, tr, C), lambda i: (0, i, 0))],
            out_specs=pl.BlockSpec((tr, C), lambda i: (i, 0)), out_shape=jax.ShapeDtypeStruct((R, C), F32),
            compiler_params=_params("parallel"),
        )(x)
    grid_spec = pltpu.PrefetchScalarGridSpec(
        num_scalar_prefetch=1, grid=(nb,), in_specs=[pl.BlockSpec((n, tr, C), lambda i, c_ref: (0, i, 0))],
        out_specs=pl.BlockSpec((tr, C), lambda i, c_ref: (c_ref[0] * nb + i, 0)))
    return pl.pallas_call(
        body, name=name, grid_spec=grid_spec, out_shape=jax.ShapeDtypeStruct((2 * R, C), F32),
        compiler_params=_params("parallel"),
    )(half.reshape(1).astype(jnp.int32), x)


def _add_halves(g, c, name):
    full, recv = g
    n, R, C = full.shape
    half = R // 2
    tr = _row_tile(half, C)
    nb = half // tr

    def body(c_ref, a_ref, b_ref, o_ref):
        o_ref[...] = (a_ref[...] + b_ref[...]).astype(BF16)

    grid_spec = pltpu.PrefetchScalarGridSpec(
        num_scalar_prefetch=1, grid=(n, nb),
        in_specs=[pl.BlockSpec((None, tr, C), lambda j, i, c_ref: (j, c_ref[0] * nb + i, 0)),
                  pl.BlockSpec((None, tr, C), lambda j, i, c_ref: (j, i, 0))],
        out_specs=pl.BlockSpec((None, tr, C), lambda j, i, c_ref: (j, i, 0)))
    return pl.pallas_call(
        body, name=name, grid_spec=grid_spec, out_shape=jax.ShapeDtypeStruct((n, half, C), BF16),
        compiler_params=_params("parallel", "parallel"),
    )(c.reshape(1).astype(jnp.int32), full, recv)


ANY = pl.BlockSpec(memory_space=pl.ANY)


def _position():
    return lax.axis_index("x"), lax.axis_index("y"), lax.axis_index("c")


def _all_gather8(blk, name):
    return _standalone(_all_gather_rider(blk), name)[0]


class _Rider:
    def __init__(self, arrays, out_shapes, sems, start, finish):
        self.arrays, self.out_shapes, self.sems, self.start, self.finish = arrays, out_shapes, sems, start, finish


def _join_riders(*riders):
    def split(refs):
        ins, outs, sems = [], [], []
        for r in riders:
            ins.append(refs[:len(r.arrays)])
            refs = refs[len(r.arrays):]
        for r in riders:
            outs.append(refs[:len(r.out_shapes)])
            refs = refs[len(r.out_shapes):]
        for r in riders:
            sems.append(refs[:len(r.sems)])
            refs = refs[len(r.sems):]
        return [i + o + s for i, o, s in zip(ins, outs, sems)]

    def start(*refs):
        for r, own in zip(riders, split(refs)):
            r.start(*own)

    def finish(*refs):
        for r, own in zip(riders, split(refs)):
            r.finish(*own)

    return _Rider([a for r in riders for a in r.arrays], [s for r in riders for s in r.out_shapes],
                  [s for r in riders for s in r.sems], start, finish)


def _standalone(rider, name):
    def body(*refs):
        rider.start(*refs)
        rider.finish(*refs)

    return pl.pallas_call(
        body, name=name, in_specs=[ANY] * len(rider.arrays), out_specs=[ANY] * len(rider.out_shapes),
        out_shape=rider.out_shapes, scratch_shapes=rider.sems,
    )(*rider.arrays)


def _ride(rider, body, grid, in_specs, out_specs, out_shape, scratch):
    if rider is None:
        return body, in_specs, out_specs, out_shape, scratch, []
    ni, no, ns = len(in_specs), len(out_specs), len(scratch)
    ri, ro = len(rider.arrays), len(rider.out_shapes)

    def full(*refs):
        ins, refs = refs[:ni], refs[ni:]
        r_ins, refs = refs[:ri], refs[ri:]
        outs, refs = refs[:no], refs[no:]
        r_outs, refs = refs[:ro], refs[ro:]
        scr, r_sems = refs[:ns], refs[ns:]
        ids = [pl.program_id(a) for a in range(len(grid))]
        first = functools.reduce(jnp.logical_and, [i == 0 for i in ids])
        last = functools.reduce(jnp.logical_and, [i == g - 1 for i, g in zip(ids, grid)])

        @pl.when(first)
        def _():
            rider.start(*r_ins, *r_outs, *r_sems)

        body(*ins, *outs, *scr)

        @pl.when(last)
        def _():
            rider.finish(*r_ins, *r_outs, *r_sems)

    return (full, in_specs + [ANY] * ri, out_specs + [ANY] * ro, out_shape + rider.out_shapes, scratch + rider.sems,
            rider.arrays)


def _all_gather_rider(blk):
    M, N = blk.shape

    def copies(x_ref, out_ref, send_sems, recv_sems, local_sem):
        x, y, c = _position()
        me, sibling = (x, y, c), (x, y, 1 - c)
        chips = [(1 - x, y), (x, 1 - y), (1 - x, 1 - y)]

        def slab(px, py, pc):
            return out_ref.at[4 * px + 2 * py + pc]

        def copy(k, block, to, src=None):
            return pltpu.make_async_remote_copy(
                src_ref=slab(*block) if src is None else src, dst_ref=slab(*block),
                send_sem=send_sems.at[k], recv_sem=recv_sems.at[k], device_id=to, device_id_type=MESH)

        mine = pltpu.make_async_copy(x_ref, slab(*me), local_sem)
        first = [copy(0, me, sibling, src=x_ref)]
        first += [copy(1 + j, me, (*chip, c), src=x_ref) for j, chip in enumerate(chips)]
        passed = [copy(4 + j, (*chip, c), sibling) for j, chip in enumerate(chips)]
        arrivals = [copy(1 + j, (*chip, c), me) for j, chip in enumerate(chips)]
        from_sibling = [copy(0, sibling, me)] + [copy(4 + j, (*chip, 1 - c), me) for j, chip in enumerate(chips)]
        return mine, first, passed, arrivals, from_sibling

    def start(*refs):
        mine, first, _, _, _ = copies(*refs)
        mine.start()
        for cp in first:
            cp.start()

    def finish(*refs):
        mine, first, passed, arrivals, from_sibling = copies(*refs)
        for arrival, onward in zip(arrivals, passed):
            arrival.wait_recv()
            onward.start()
        for cp in from_sibling:
            cp.wait_recv()
        for cp in first + passed:
            cp.wait_send()
        mine.wait()

    return _Rider([blk], [jax.ShapeDtypeStruct((N_DEV, M, N), blk.dtype)],
                  [pltpu.SemaphoreType.DMA((7,)), pltpu.SemaphoreType.DMA((7,)), pltpu.SemaphoreType.DMA], start, finish)


def _all_gather_forwarding(blk, name):
    M, N = blk.shape
    half = M // 2

    def body(x_ref, out_ref, send_sems, recv_sems, local_sem):
        x, y, c = _position()
        me, sibling = (x, y, c), (x, y, 1 - c)
        xn, yn, dg = (1 - x, y), (x, 1 - y), (1 - x, 1 - y)

        def slab(chip, pc, part=None):
            ref = out_ref.at[4 * chip[0] + 2 * chip[1] + pc]
            return ref if part is None else ref.at[pl.ds(part * half, half), :]

        def copy(k, place, to, src=None):
            return pltpu.make_async_remote_copy(src_ref=place if src is None else src, dst_ref=place,
                                                send_sem=send_sems.at[k], recv_sem=recv_sems.at[k], device_id=to,
                                                device_id_type=MESH)

        mine = pltpu.make_async_copy(x_ref, slab((x, y), c), local_sem)
        own = [copy(0, slab((x, y), c), sibling, src=x_ref), copy(1, slab((x, y), c), (*xn, c), src=x_ref),
               copy(2, slab((x, y), c), (*yn, c), src=x_ref)]
        mine.start()
        for cp in own:
            cp.start()
        copy(1, slab(xn, c), me).wait_recv()
        onward = [copy(3, slab(xn, c, 0), (*yn, c)), copy(5, slab(xn, c), sibling)]
        for cp in onward:
            cp.start()
        copy(2, slab(yn, c), me).wait_recv()
        onward += [copy(4, slab(yn, c, 1), (*xn, c)), copy(6, slab(yn, c), sibling)]
        for cp in onward[2:]:
            cp.start()
        copy(3, slab(dg, c, 0), me).wait_recv()
        onward.append(copy(7, slab(dg, c, 0), sibling))
        onward[-1].start()
        copy(4, slab(dg, c, 1), me).wait_recv()
        onward.append(copy(8, slab(dg, c, 1), sibling))
        onward[-1].start()
        for k, place in ((0, slab((x, y), 1 - c)), (5, slab(xn, 1 - c)), (6, slab(yn, 1 - c)), (7, slab(dg, 1 - c, 0)),
                         (8, slab(dg, 1 - c, 1))):
            copy(k, place, me).wait_recv()
        for cp in own + onward:
            cp.wait_send()
        mine.wait()

    return pl.pallas_call(
        body, name=name, in_specs=[ANY], out_specs=ANY, out_shape=jax.ShapeDtypeStruct((N_DEV, M, N), blk.dtype),
        scratch_shapes=[pltpu.SemaphoreType.DMA((9,)), pltpu.SemaphoreType.DMA((9,)), pltpu.SemaphoreType.DMA],
    )(blk)


def _sibling_half_rider(g):
    n, R, C = g.shape
    half = R // 2

    def copy(g_ref, out_ref, send_sem, recv_sem):
        x, y, c = _position()
        return pltpu.make_async_remote_copy(
            src_ref=g_ref.at[:, pl.ds((1 - c) * half, half), :], dst_ref=out_ref, send_sem=send_sem, recv_sem=recv_sem,
            device_id=(x, y, 1 - c), device_id_type=MESH)

    return _Rider([g], [jax.ShapeDtypeStruct((n, half, C), g.dtype)], [pltpu.SemaphoreType.DMA, pltpu.SemaphoreType.DMA],
                  lambda *refs: copy(*refs).start(), lambda *refs: copy(*refs).wait())


def _sibling_send_half(g, name):
    return _standalone(_sibling_half_rider(g), name)[0]


def _chip_exchange_rider(p, row_off=0, rows=None):
    rows = p.shape[1] if rows is None else rows

    def copies(p_ref, out_ref, send_sems, recv_sems, local_sem):
        x, y, c = _position()
        my = 2 * x + y
        chips = [(1 - x, y), (x, 1 - y), (1 - x, 1 - y)]

        def src(slab):
            return p_ref.at[slab, pl.ds(row_off, rows), :]

        mine = pltpu.make_async_copy(src(my), out_ref.at[my], local_sem)
        sends = [pltpu.make_async_remote_copy(
            src_ref=src(2 * px + py), dst_ref=out_ref.at[my], send_sem=send_sems.at[k], recv_sem=recv_sems.at[k],
            device_id=(px, py, c), device_id_type=MESH) for k, (px, py) in enumerate(chips)]
        arrivals = [pltpu.make_async_remote_copy(
            src_ref=src(my), dst_ref=out_ref.at[2 * px + py], send_sem=send_sems.at[k], recv_sem=recv_sems.at[k],
            device_id=(px, py, c), device_id_type=MESH) for k, (px, py) in enumerate(chips)]
        return mine, sends, arrivals

    def start(*refs):
        mine, sends, _ = copies(*refs)
        mine.start()
        for cp in sends:
            cp.start()

    def finish(*refs):
        mine, sends, arrivals = copies(*refs)
        for cp in arrivals:
            cp.wait_recv()
        for cp in sends:
            cp.wait_send()
        mine.wait()

    return _Rider([p], [jax.ShapeDtypeStruct((p.shape[0], rows, p.shape[2]), p.dtype)],
                  [pltpu.SemaphoreType.DMA((3,)), pltpu.SemaphoreType.DMA((3,)), pltpu.SemaphoreType.DMA], start, finish)


def _sibling_fill(buf, name):
    M = buf.shape[0] // 2

    def body(buf_ref, out_ref, send_sem, recv_sem):
        x, y, c = _position()
        mine = pl.ds(c * M, M)
        theirs = pl.ds((1 - c) * M, M)
        pltpu.make_async_remote_copy(src_ref=buf_ref.at[mine], dst_ref=out_ref.at[mine], send_sem=send_sem,
                                     recv_sem=recv_sem, device_id=(x, y, 1 - c), device_id_type=MESH).start()
        pltpu.make_async_remote_copy(src_ref=buf_ref.at[mine], dst_ref=out_ref.at[theirs], send_sem=send_sem,
                                     recv_sem=recv_sem, device_id=(x, y, 1 - c), device_id_type=MESH).wait()

    return pl.pallas_call(
        body, name=name, in_specs=[ANY], out_specs=ANY, out_shape=jax.ShapeDtypeStruct(buf.shape, buf.dtype),
        scratch_shapes=[pltpu.SemaphoreType.DMA, pltpu.SemaphoreType.DMA], input_output_aliases={0: 0},
    )(buf)


def _pack_rest(attn_out, ssm_in, glu_w, ssm_out):
    hd = ssm_in.shape[0] // 2
    return jnp.concatenate([attn_out, jnp.concatenate([ssm_in[:hd], ssm_in[hd:]], axis=1), glu_w, ssm_out], axis=0)


def _unpack_rest(p):
    D = p.shape[-1]
    q, hd = D // N_CHIPS, D // 2
    o = [0, q, q + hd, 2 * q + hd, 3 * q + hd]
    ssm_in = p[o[1]:o[2]]
    return p[o[0]:o[1]], jnp.concatenate([ssm_in[:, :hd], ssm_in[:, hd:]], axis=0), p[o[2]:o[3]], p[o[3]:o[4]]


def _attn_in_views(buf_a):
    D = buf_a.shape[-1]
    return [_View(buf_a, (D, D), lambda r, c, j=j: (j, r, c), D, D) for j in range(N_CHIPS)]


def _rest_views(buf_b):
    D = buf_b.shape[-1]
    q, hd = D // N_CHIPS, D // 2
    o_out, o_in, o_glu, o_sout = 0, q, q + hd, 2 * q + hd

    def row_sharded(off):
        return _View(buf_b, (D, D), lambda r, c: (r // q, off + r % q, c), math.gcd(q, off), D)

    def ssm_in(part):
        return _View(buf_b, (D, D), lambda r, c: (2 * part + c // hd, o_in + r % hd, (r // hd) * hd + c % hd),
                     math.gcd(hd, o_in), hd)

    return dict(attn_out=row_sharded(o_out), ssm_in_u=ssm_in(0), ssm_in_gate=ssm_in(1), glu=row_sharded(o_glu),
                ssm_out=row_sharded(o_sout))


def _pack_small(parts):
    flat = jnp.concatenate([p.reshape(-1) for p in parts])
    pad = (-flat.size) % (2 * SUBLANES * LANES)
    return jnp.pad(flat, (0, pad)).reshape(-1, LANES)


def _unpack_small(buf, shapes):
    flat = buf.reshape(-1)
    out, off = [], 0
    for s in shapes:
        n = math.prod(s)
        out.append(flat[off:off + n].reshape(s))
        off += n
    return out


def _local_step(x, target, norm_g, q_g, k_g, wa, wb, ssm_small, core=None):
    Bl, S, D = x.shape
    T = Bl * S
    x0 = x.reshape(T, D)
    tgt = target.reshape(T, D)
    a_re, a_im, log_dt, b_re, b_im, c_re, c_im, d_skip, glu_b = ssm_small
    G = a_re.shape[0]
    mats = _ssm_matrices(a_re, a_im, log_dt, b_re, b_im, c_re, c_im, d_skip, S)
    w_in = _attn_in_views(wa)
    rows_b = wb.shape[-2] * (1 if core is None else 2)
    ga = lax.empty(wa.shape, F32)
    gb = lax.empty((N_CHIPS, rows_b, D), F32)

    def wgrad(a, b, key, name):
        return _matmul(a, b, name=name, ta=True, out=_rest_views(gb)[key])

    h0 = _rmsnorm_fwd(x0, norm_g[0], "norm0_fwd")
    q, k, v, gate = [_matmul(h0, w_in[j], name=f"attn_in_{j}", out_dtype=(BF16 if j == 2 else F32)) for j in range(4)]
    qn = _qknorm_fwd(q, q_g, "qnorm_fwd")
    kn = _qknorm_fwd(k, k_g, "knorm_fwd")
    if core is None:
        o, btot, og = _attn_fwd(qn, kn, v, gate, Bl, S)
    else:
        o, btot, og, wb = _attn_fwd(qn, kn, v, gate, Bl, S, rider=_all_gather_rider(wb))
        wb = wb.reshape(N_CHIPS, rows_b, D)
    w = _rest_views(wb)
    x1 = _matmul(og, w["attn_out"], name="attn_out", residual=x0)

    h1 = _rmsnorm_fwd(x1, norm_g[1], "norm1_fwd")
    u = _matmul(h1, w["ssm_in_u"], name="ssm_in_u")
    gate2 = _matmul(h1, w["ssm_in_gate"], name="ssm_in_gate")
    y, hs_re, hs_im, yg = _ssm_fwd(u, mats, Bl, S)
    gl, y3 = _matmul(yg, w["glu"], name="glu_mm", epilogue=_glu_fwd(y, gate2, glu_b))
    x2 = _matmul(y3, w["ssm_out"], name="ssm_out", residual=x1)

    dx2, dx2b, loss = _loss_head(x2, tgt)

    dy3 = _matmul(dx2b, w["ssm_out"], name="ssm_out_dgrad", tb=True)
    gb = wgrad(y3, dx2b, "ssm_out", "ssm_out_wgrad")
    dgl, dgate2, t1, dglu_b = _glu_bwd(dy3, y, gl, gate2, glu_b)
    dy = _matmul(dgl, w["glu"], name="glu_dgrad", tb=True, epilogue=_gelu_bwd(t1, y))
    gb = wgrad(yg, dgl, "glu", "glu_wgrad")
    du, dbr, dbi, dcr, dci, dlr, dli, dd = _ssm_bwd(u, dy, hs_re, hs_im, mats, Bl, S)
    dh1 = _matmul(du, w["ssm_in_u"], name="ssm_in_dgrad_u", tb=True)
    dh1 = _matmul(dgate2, w["ssm_in_gate"], name="ssm_in_dgrad_gate", tb=True, residual=dh1)
    gb = wgrad(h1, du, "ssm_in_u", "ssm_in_wgrad_u")
    gb = wgrad(h1, dgate2, "ssm_in_gate", "ssm_in_wgrad_gate")
    dx1, dx1b, dng1 = _rmsnorm_bwd(x1, norm_g[1], dh1, dx2, "norm1_bwd")
    small_early = (dng1,) + _ssm_unblock(dbr, dbi, dcr, dci, dlr, dli, dd, G) + (dglu_b.reshape(D),)

    gb = wgrad(og, dx1b, "attn_out", "attn_out_wgrad")
    if core is None:
        do, dgate = _matmul(dx1b, w["attn_out"], name="attn_out_dgrad", tb=True, epilogue=_attn_gate_bwd(o, gate))
        dqn, dkn, dv = _attn_bwd(qn, kn, v, do, btot, Bl, S)
    else:
        do, dgate, from_sibling = _matmul(dx1b, w["attn_out"], name="attn_out_dgrad", tb=True,
                                          epilogue=_attn_gate_bwd(o, gate), rider=_sibling_half_rider(gb))
        chip_sum_b = _add_halves((gb, from_sibling), core, "grads_b_add_halves")
        dqn, dkn, dv, gb, small_early = _attn_bwd(
            qn, kn, v, do, btot, Bl, S,
            rider=_join_riders(_chip_exchange_rider(chip_sum_b), _all_gather_rider(_pack_small(small_early))))
    dq, dqg = _qknorm_bwd(q, q_g, dqn, "qnorm_bwd")
    dk, dkg = _qknorm_bwd(k, k_g, dkn, "knorm_bwd")
    dproj = [dq, dk, dv, dgate]
    for j in range(4):
        ga = _matmul(h0, dproj[j], name=f"attn_in_wgrad_{j}", ta=True, out=_attn_in_views(ga)[j])
    dh0 = None
    if core is None:
        for j in range(4):
            dh0 = _matmul(dproj[j], w_in[j], name=f"attn_in_dgrad_{j}", tb=True, residual=dh0)
    else:
        chip_sum_a = _add_halves((ga, _sibling_send_half(ga, "grads_a_sibling_half")), core, "grads_a_add_halves")
        rows = chip_sum_a.shape[1] // 4
        parts = []
        for j in range(4):
            dh0, part = _matmul(dproj[j], w_in[j], name=f"attn_in_dgrad_{j}", tb=True, residual=dh0,
                                rider=_chip_exchange_rider(chip_sum_a, j * rows, rows))
            parts.append(part)
        ga = jnp.concatenate(parts, axis=1)
    dx0, _, dng0 = _rmsnorm_bwd(x0, norm_g[0], dh0, dx1, "norm0_bwd")

    return loss, dx0.reshape(Bl, S, D), ga, gb, small_early, (dng0, dqg, dkg)


def _small_early_shapes(D):
    G = D // GROUP
    return [(1, D), (G, STATE), (G, STATE), (G, STATE, GROUP), (G, STATE, GROUP), (G, GROUP, STATE), (G, GROUP, STATE),
            (D,), (D,)]


def _chip_rows(a, chip, n_per):
    return lax.dynamic_slice_in_dim(a, chip * n_per, n_per, axis=0)


def kernel(x, norm_g, attn_w_in, attn_q_g, attn_k_g, attn_w_out, ssm_w_in, ssm_A_re, ssm_A_im, ssm_log_dt, ssm_B_re, ssm_B_im, ssm_C_re, ssm_C_im, ssm_D, ssm_glu_w, ssm_glu_b, ssm_w_out, loss_target, m_norm_g, m_attn_w_in, m_attn_q_g, m_attn_k_g, m_attn_w_out, m_ssm_w_in, m_ssm_A_re, m_ssm_A_im, m_ssm_log_dt, m_ssm_B_re, m_ssm_B_im, m_ssm_C_re, m_ssm_C_im, m_ssm_D, m_ssm_glu_w, m_ssm_glu_b, m_ssm_w_out, v_norm_g, v_attn_w_in, v_attn_q_g, v_attn_k_g, v_attn_w_out, v_ssm_w_in, v_ssm_A_re, v_ssm_A_im, v_ssm_log_dt, v_ssm_B_re, v_ssm_B_im, v_ssm_C_re, v_ssm_C_im, v_ssm_D, v_ssm_glu_w, v_ssm_glu_b, v_ssm_w_out):
    D = x.shape[-1]
    cx, cy, cc = _position()
    chip = 2 * cx + cy
    G = D // GROUP
    Gl = G // N_CHIPS

    def my_half(a):
        return lax.dynamic_slice_in_dim(a, cc * (a.shape[0] // 2), a.shape[0] // 2, axis=0)

    wa = _all_gather_forwarding(my_half(attn_w_in[0].astype(BF16)), "attn_in_all_gather").reshape(N_CHIPS, D, D)
    wb_half = my_half(_pack_rest(attn_w_out[0], ssm_w_in[0], ssm_glu_w[0], ssm_w_out[0]).astype(BF16))

    ssm_local = [ssm_A_re[0], ssm_A_im[0], ssm_log_dt[0], ssm_B_re[0], ssm_B_im[0], ssm_C_re[0], ssm_C_im[0], ssm_D[0],
                 ssm_glu_b[0]]
    small_local = _pack_small(ssm_local)
    half_rows = small_local.shape[0] // 2
    small_half = lax.dynamic_slice_in_dim(small_local, cc * half_rows, half_rows, axis=0)
    small_all = _all_gather8(small_half, "ssm_params_all_gather").reshape(N_CHIPS, 2 * half_rows, LANES)
    per_chip = [_unpack_small(small_all[j], [p.shape for p in ssm_local]) for j in range(N_CHIPS)]
    ssm_full = [jnp.concatenate([per_chip[j][i] for j in range(N_CHIPS)], axis=0) for i in range(len(ssm_local))]

    loss, grad_x, gathered_a, gathered_b, small_early, small_late = _local_step(
        x, loss_target, norm_g, attn_q_g[0], attn_k_g[0], wa, wb_half, ssm_full, core=cc)
    loss = lax.psum(loss[0, 0], ("x", "y", "c"))

    q4 = D // N_CHIPS
    def both_halves(gathered, tag):
        return _sibling_fill(_sum_leading(gathered, f"grads_{tag}_sum_chips", half=cc), f"grads_{tag}_sibling_fill")

    gd = (both_halves(gathered_a, "a"),) + _unpack_rest(both_halves(gathered_b, "b"))

    (dng1, dl_re, dl_im, dbb_re, dbb_im, dc_re, dc_im, dd_skip, dglu_b) = _unpack_small(
        _sum_leading(small_early, "small_early_sum"), _small_early_shapes(D))
    late_all = _all_gather8(_pack_small(small_late), "small_late_all_gather")
    dng0, dqg, dkg = _unpack_small(_sum_leading(late_all, "small_late_sum"), [s.shape for s in small_late])
    dng = jnp.concatenate([dng0, dng1], axis=0)
    a_re, a_im, log_dt, b_re, b_im = ssm_local[:5]
    _, zoh_vjp = jax.vjp(_zoh, a_re, a_im, log_dt, b_re, b_im)
    da_re, da_im, dlog_dt, db_re, db_im = zoh_vjp((_chip_rows(dl_re, chip, Gl), _chip_rows(dl_im, chip, Gl),
                                                   _chip_rows(dbb_re, chip, Gl), _chip_rows(dbb_im, chip, Gl)))
    grads = {
        "norm_g": dng, "attn_w_in": gd[0][None], "attn_q_g": dqg, "attn_k_g": dkg, "attn_w_out": gd[1][None],
        "ssm_w_in": gd[2][None], "ssm_A_re": da_re[None], "ssm_A_im": da_im[None], "ssm_log_dt": dlog_dt[None],
        "ssm_B_re": db_re[None], "ssm_B_im": db_im[None], "ssm_C_re": _chip_rows(dc_re, chip, Gl)[None],
        "ssm_C_im": _chip_rows(dc_im, chip, Gl)[None], "ssm_D": _chip_rows(dd_skip, chip, q4)[None],
        "ssm_glu_w": gd[3][None], "ssm_glu_b": _chip_rows(dglu_b, chip, q4)[None], "ssm_w_out": gd[4][None],
    }
    weights = dict(norm_g=norm_g, attn_w_in=attn_w_in, attn_q_g=attn_q_g, attn_k_g=attn_k_g, attn_w_out=attn_w_out,
                   ssm_w_in=ssm_w_in, ssm_A_re=ssm_A_re, ssm_A_im=ssm_A_im, ssm_log_dt=ssm_log_dt, ssm_B_re=ssm_B_re,
                   ssm_B_im=ssm_B_im, ssm_C_re=ssm_C_re, ssm_C_im=ssm_C_im, ssm_D=ssm_D, ssm_glu_w=ssm_glu_w,
                   ssm_glu_b=ssm_glu_b, ssm_w_out=ssm_w_out)
    m = dict(norm_g=m_norm_g, attn_w_in=m_attn_w_in, attn_q_g=m_attn_q_g, attn_k_g=m_attn_k_g, attn_w_out=m_attn_w_out,
             ssm_w_in=m_ssm_w_in, ssm_A_re=m_ssm_A_re, ssm_A_im=m_ssm_A_im, ssm_log_dt=m_ssm_log_dt, ssm_B_re=m_ssm_B_re,
             ssm_B_im=m_ssm_B_im, ssm_C_re=m_ssm_C_re, ssm_C_im=m_ssm_C_im, ssm_D=m_ssm_D, ssm_glu_w=m_ssm_glu_w,
             ssm_glu_b=m_ssm_glu_b, ssm_w_out=m_ssm_w_out)
    v = dict(norm_g=v_norm_g, attn_w_in=v_attn_w_in, attn_q_g=v_attn_q_g, attn_k_g=v_attn_k_g, attn_w_out=v_attn_w_out,
             ssm_w_in=v_ssm_w_in, ssm_A_re=v_ssm_A_re, ssm_A_im=v_ssm_A_im, ssm_log_dt=v_ssm_log_dt, ssm_B_re=v_ssm_B_re,
             ssm_B_im=v_ssm_B_im, ssm_C_re=v_ssm_C_re, ssm_C_im=v_ssm_C_im, ssm_D=v_ssm_D, ssm_glu_w=v_ssm_glu_w,
             ssm_glu_b=v_ssm_glu_b, ssm_w_out=v_ssm_w_out)
    names = list(weights)
    dense_names = ("attn_w_in", "attn_w_out", "ssm_w_in", "ssm_glu_w", "ssm_w_out")
    delta, new_m, new_v = {}, {}, {}
    for n in dense_names:
        delta[n], new_m[n], new_v[n] = _adamw(weights[n], grads[n], m[n], v[n], "adamw_" + n)
    small_names = [n for n in names if n not in dense_names]
    small_shapes = [weights[n].shape for n in small_names]
    packs = [_pack_small([d[n] for n in small_names]) for d in (weights, grads, m, v)]
    outs = _adamw(*packs, "adamw_small")
    for res, out in zip((delta, new_m, new_v), outs):
        for n, val in zip(small_names, _unpack_small(out, small_shapes)):
            res[n] = val
    return (loss, grad_x, *[grads[n] for n in names], *[delta[n] for n in names], *[new_m[n] for n in names],
            *[new_v[n] for n in names])
```

```python
import functools
import math

import jax
import jax.numpy as jnp
from jax import lax
from jax.experimental import pallas as pl
from jax.experimental.pallas import tpu as pltpu

F32 = jnp.float32
BF16 = jnp.bfloat16

HEAD_DIM = 128
GROUP = 16
STATE = 64
RMS_EPS = 1e-6
ADAM_LR = 0.001
ADAM_B1 = 0.9
ADAM_B2 = 0.999
ADAM_EPS = 1e-08
ADAM_WD = 0.01
ADAM_STEP = 10

N_CHIPS = 4
N_DEV = 8
SUBLANES = 8
LANES = 128
VMEM_LIMIT = 56 * 1024 * 1024
ROW_BLOCK_ELEMS = 1 << 19
MATMUL_TILE = 1024
MATMUL_PANEL_BYTES = 8 * 1024 * 1024
MESH = pl.DeviceIdType.MESH

NN = (((1,), (0,)), ((), ()))
NT = (((1,), (1,)), ((), ()))
TN = (((0,), (0,)), ((), ()))


def _params(*sem):
    return pltpu.CompilerParams(dimension_semantics=sem, vmem_limit_bytes=VMEM_LIMIT)


def _dot(a, b, dims=NN):
    return lax.dot_general(a, b, dims, preferred_element_type=F32)


def _split(a):
    hi = a.astype(BF16)
    lo = (a - hi.astype(F32)).astype(BF16)
    return hi, lo


def _dot_f32(a, b, dims=NN):
    return _dot(a.astype(BF16), b.astype(BF16), dims)


def _sigmoid(x):
    return 1.0 / (1.0 + jnp.exp(-x))


def _silu_parts(x):
    s = _sigmoid(x)
    return x * s, s * (1.0 + x * (1.0 - s))


_GELU_C = math.sqrt(2.0 / math.pi)


def _gelu_parts(x):
    x2 = x * x
    t = jnp.tanh(_GELU_C * (x + 0.044715 * x * x2))
    val = 0.5 * x * (1.0 + t)
    der = 0.5 * (1.0 + t) + 0.5 * x * (1.0 - t * t) * _GELU_C * (1.0 + 3.0 * 0.044715 * x2)
    return val, der


class _View:
    def __init__(self, buf, shape, locate, row_tile, col_tile):
        self.buf, self.shape, self.locate, self.row_tile, self.col_tile = buf, shape, locate, row_tile, col_tile

    def spec(self, t0, t1, block_of):
        def index(i, j, k):
            rb, cb = block_of(i, j, k)
            slab, r, c = self.locate(rb * t0, cb * t1)
            return slab, r // t0, c // t1
        return pl.BlockSpec((None, t0, t1), index)


def _operand(x):
    return (x.buf, x.shape, x.row_tile, x.col_tile) if isinstance(x, _View) else (x, x.shape, x.shape[0], x.shape[1])


class _Epilogue:
    def __init__(self, fn, tiles, rows, out_dtypes):
        self.fn, self.tiles, self.rows, self.out_dtypes = fn, tiles, rows, out_dtypes


def _matmul(a, b, *, name, ta=False, tb=False, residual=None, out_dtype=F32, out=None, rider=None, epilogue=None):
    if residual is not None:
        epilogue = _Epilogue(lambda r, res: (r + res.astype(F32),), [residual], [], [out_dtype])
    elif epilogue is None:
        epilogue = _Epilogue(lambda r: (r,), [], [], [out_dtype])
    n_extra, n_out = len(epilogue.tiles) + len(epilogue.rows), len(epilogue.out_dtypes)
    a_arr, a_shape, a_rt, a_ct = _operand(a)
    b_arr, b_shape, b_rt, b_ct = _operand(b)
    (K, M) = a_shape if ta else a_shape[::-1]
    N = b_shape[0] if tb else b_shape[1]
    a_mt, a_kt = (a_ct, a_rt) if ta else (a_rt, a_ct)
    b_nt, b_kt = (b_rt, b_ct) if tb else (b_ct, b_rt)
    k_cap = MATMUL_PANEL_BYTES // (MATMUL_TILE * max(a_arr.dtype.itemsize, b_arr.dtype.itemsize))
    n_cap = MATMUL_TILE if n_extra + n_out <= 2 else MATMUL_TILE // 2
    tm, tn, tk = min(M, MATMUL_TILE, a_mt), min(N, n_cap, b_nt), min(K, k_cap, a_kt)
    if out is not None:
        tm, tn = min(tm, out.row_tile), min(tn, out.col_tile)
    pk = min(tk, b_kt)
    pieces = tk // pk
    nk = K // tk
    dims = ((((0,) if ta else (1,)), ((1,) if tb else (0,))), ((), ()))
    n_in = 1 + pieces + n_extra + (out is not None)

    def body(*refs):
        a_ref, b_refs = refs[0], refs[1:1 + pieces]
        e_refs = refs[1 + pieces:1 + pieces + n_extra]
        o_refs = refs[n_in:n_in + n_out]

        def finish(r):
            for o_ref, val in zip(o_refs, epilogue.fn(r, *[e[...] for e in e_refs])):
                o_ref[...] = val.astype(o_ref.dtype)

        part = None
        for p, b_ref in enumerate(b_refs):
            ks = slice(p * pk, (p + 1) * pk)
            a_blk = a_ref[...] if pieces == 1 else (a_ref[ks, :] if ta else a_ref[:, ks])
            term = _dot(a_blk.astype(BF16), b_ref[...].astype(BF16), dims)
            part = term if part is None else part + term
        if nk == 1:
            finish(part)
            return
        acc = refs[n_in + n_out]
        k = pl.program_id(2)

        @pl.when(k == 0)
        def _():
            acc[...] = part

        @pl.when(k > 0)
        def _():
            acc[...] += part

        @pl.when(k == nk - 1)
        def _():
            finish(acc[...])

    def spec(x, t0, t1, block_of):
        if isinstance(x, _View):
            return x.spec(t0, t1, block_of)
        return pl.BlockSpec((t0, t1), block_of)

    a_spec = spec(a, tk, tm, lambda i, j, k: (k, i)) if ta else spec(a, tm, tk, lambda i, j, k: (i, k))
    b_specs = [spec(b, tn, pk, lambda i, j, k, p=p: (j, k * pieces + p)) if tb else
               spec(b, pk, tn, lambda i, j, k, p=p: (k * pieces + p, j)) for p in range(pieces)]
    tile_spec = pl.BlockSpec((tm, tn), lambda i, j, k: (i, j))
    in_specs = [a_spec] + b_specs + [tile_spec] * len(epilogue.tiles) + [
        pl.BlockSpec((1, tn), lambda i, j, k: (0, j))] * len(epilogue.rows)
    args = [a_arr] + [b_arr] * pieces + list(epilogue.tiles) + list(epilogue.rows)
    aliases = {}
    if out is None:
        out_specs = [tile_spec] * n_out
        out_shape = [jax.ShapeDtypeStruct((M, N), d) for d in epilogue.out_dtypes]
    else:
        out_specs = [out.spec(tm, tn, lambda i, j, k: (i, j))]
        out_shape = [jax.ShapeDtypeStruct(out.buf.shape, out.buf.dtype)]
        in_specs.append(pl.BlockSpec(memory_space=pl.ANY))
        args.append(out.buf)
        aliases = {len(args) - 1: 0}
    grid = (M // tm, N // tn, nk)
    scratch = [pltpu.VMEM((tm, tn), F32)] if nk > 1 else []
    body, in_specs, out_specs, out_shape, scratch, extra = _ride(rider, body, grid, in_specs, out_specs, out_shape,
                                                                 scratch)
    results = pl.pallas_call(
        body, name=name, grid=grid, in_specs=in_specs, out_specs=out_specs, out_shape=out_shape,
        scratch_shapes=scratch, input_output_aliases=aliases,
        compiler_params=_params(*(("parallel", "parallel", "arbitrary") if rider is None else ("arbitrary",) * 3)),
    )(*args, *extra)
    return results[0] if len(results) == 1 else results


def _row_tile(T, C):
    tr = max(SUBLANES, min(T, ROW_BLOCK_ELEMS // C) // SUBLANES * SUBLANES)
    while T % tr:
        tr -= SUBLANES
    return tr


def _rows_call(body, name, T, C, row_ins, full_ins, row_outs, acc_outs=(), rider=None):
    tr = _row_tile(T, C)
    row_spec = pl.BlockSpec((tr, C), lambda i: (i, 0))
    in_specs = [row_spec] * len(row_ins) + [pl.BlockSpec(f.shape, lambda i, n=f.ndim: (0,) * n) for f in full_ins]
    out_specs = [row_spec] * len(row_outs) + [pl.BlockSpec(s, lambda i, n=len(s): (0,) * n) for s in acc_outs]
    out_shape = [jax.ShapeDtypeStruct((T, C), d) for d in row_outs] + [jax.ShapeDtypeStruct(s, F32) for s in acc_outs]
    grid = (T // tr,)
    body, in_specs, out_specs, out_shape, scratch, extra = _ride(rider, body, grid, in_specs, out_specs, out_shape, [])
    return pl.pallas_call(
        body, name=name, grid=grid, in_specs=in_specs, out_specs=out_specs, out_shape=out_shape, scratch_shapes=scratch,
        compiler_params=_params("arbitrary" if acc_outs or rider is not None else "parallel"),
    )(*row_ins, *full_ins, *extra)


def _rmsnorm_fwd(x, g, name):
    T, C = x.shape

    def body(x_ref, g_ref, h_ref):
        xv = x_ref[...]
        r = lax.rsqrt(jnp.mean(xv * xv, axis=-1, keepdims=True) + RMS_EPS)
        h_ref[...] = ((xv * r) * g_ref[...]).astype(BF16)

    return _rows_call(body, name, T, C, [x], [g.reshape(1, C)], [BF16])[0]


def _rmsnorm_bwd(x, g, dh, dres, name, rider=None):
    T, C = x.shape

    def body(x_ref, dh_ref, dres_ref, g_ref, dx_ref, dxb_ref, dg_ref):
        @pl.when(pl.program_id(0) == 0)
        def _():
            dg_ref[...] = jnp.zeros_like(dg_ref)

        xv = x_ref[...]
        dhv = dh_ref[...]
        r = lax.rsqrt(jnp.mean(xv * xv, axis=-1, keepdims=True) + RMS_EPS)
        xn = xv * r
        dg_ref[...] += jnp.sum(dhv * xn, axis=0, keepdims=True)
        dxn = dhv * g_ref[...]
        dx = dres_ref[...] + r * (dxn - xn * jnp.mean(dxn * xn, axis=-1, keepdims=True))
        dx_ref[...] = dx
        dxb_ref[...] = dx.astype(BF16)

    return _rows_call(body, name, T, C, [x, dh, dres], [g.reshape(1, C)], [F32, BF16], [(1, C)], rider=rider)


def _heads(C):
    return [slice(h * HEAD_DIM, (h + 1) * HEAD_DIM) for h in range(C // HEAD_DIM)]


def _qknorm_fwd(q, g, name):
    T, C = q.shape

    def body(q_ref, g_ref, o_ref):
        for head in _heads(C):
            xv = q_ref[:, head]
            r = lax.rsqrt(jnp.mean(xv * xv, axis=-1, keepdims=True) + RMS_EPS)
            o_ref[:, head] = ((xv * r) * g_ref[...]).astype(BF16)

    return _rows_call(body, name, T, C, [q], [g.reshape(1, HEAD_DIM)], [BF16])[0]


def _qknorm_bwd(q, g, dqn, name):
    T, C = q.shape

    def body(q_ref, d_ref, g_ref, dq_ref, dg_ref):
        @pl.when(pl.program_id(0) == 0)
        def _():
            dg_ref[...] = jnp.zeros_like(dg_ref)

        dg = jnp.zeros((1, HEAD_DIM), F32)
        for head in _heads(C):
            xv = q_ref[:, head]
            dv = d_ref[:, head]
            r = lax.rsqrt(jnp.mean(xv * xv, axis=-1, keepdims=True) + RMS_EPS)
            xn = xv * r
            dg = dg + jnp.sum(dv * xn, axis=0, keepdims=True)
            dxn = dv * g_ref[...]
            dq_ref[:, head] = (r * (dxn - xn * jnp.mean(dxn * xn, axis=-1, keepdims=True))).astype(BF16)
        dg_ref[...] += dg

    return _rows_call(body, name, T, C, [q, dqn], [g.reshape(1, HEAD_DIM)], [BF16], [(1, HEAD_DIM)])


ATT_TQ = 512
ATT_TK = 256
ATT_HEADS = 2


def _logsig_pair(z):
    a = jnp.minimum(z, 0.0) - jnp.log(1.0 + jnp.exp(-jnp.abs(z)))
    return a, a - z


def _tri(n, strict_upper_src):
    j = lax.broadcasted_iota(jnp.int32, (n, n), 0)
    s = lax.broadcasted_iota(jnp.int32, (n, n), 1)
    if strict_upper_src == "gt":
        m = j > s
    elif strict_upper_src == "le":
        m = j <= s
    else:
        m = j < s
    return jnp.where(m, 1.0, 0.0).astype(BF16)


def _cumdot(x, tri):
    hi, lo = _split(x)
    return _dot(hi, tri) + _dot(lo, tri)


def _attn_tiles(S):
    tq, tk = min(ATT_TQ, S), min(ATT_TK, S)
    return tq, tk, S // tq, tq // tk


def _attn_fwd(qn, kn, v, gate, Bl, S, rider=None):
    T, C = qn.shape
    H = C // HEAD_DIM
    tq, tk, nq, kpq = _attn_tiles(S)
    hp = min(ATT_HEADS, H)
    scale = 1.0 / math.sqrt(HEAD_DIM)

    def body(q_ref, k_ref, v_ref, g_ref, o_ref, bt_ref, og_ref):
        i = pl.program_id(2)
        tri = _tri(tk, "gt")
        rowpos = lax.broadcasted_iota(jnp.int32, (tq, tk), 0) + i * tq
        colpos = lax.broadcasted_iota(jnp.int32, (tq, tk), 1)
        o_ref[...] = jnp.zeros_like(o_ref)
        bt_ref[...] = jnp.zeros_like(bt_ref)

        heads = _heads(hp * HEAD_DIM)

        def diagonal():
            work = []
            for p in reversed(range(kpq)):
                j = i * kpq + p
                rows = pl.ds(pl.multiple_of(j * tk, tk), tk)
                live = slice(p * tk, tq)
                mask = (colpos[live] + j * tk) < rowpos[live]
                for head in heads:
                    a, b = _logsig_pair(_dot(q_ref[live, head], k_ref[rows, head], NT) * scale)
                    b = jnp.where(mask, b, 0.0)
                    work.append((head, rows, live, a, b, _cumdot(b, tri), mask))
            for n, head in enumerate(heads):
                for rows, live, a, b, suffix, mask in [w[1:] for w in work if w[0] == head]:
                    w = jnp.where(mask, jnp.exp(a + suffix + bt_ref[n, live]), 0.0)
                    o_ref[live, head] += _dot(w.astype(BF16), v_ref[rows, head])
                    bt_ref[n, live] += jnp.sum(b, axis=-1, keepdims=True)

        def group(g):
            work = []
            for p in reversed(range(kpq)):
                rows = pl.ds(pl.multiple_of((g * kpq + p) * tk, tk), tk)
                for head in heads:
                    a, b = _logsig_pair(_dot(q_ref[:, head], k_ref[rows, head], NT) * scale)
                    work.append((head, rows, a, b, _cumdot(b, tri)))
            for n, head in enumerate(heads):
                total = bt_ref[n]
                out = None
                for rows, a, b, suffix in [w[1:] for w in work if w[0] == head]:
                    term = _dot(jnp.exp(a + suffix + total).astype(BF16), v_ref[rows, head])
                    out = term if out is None else out + term
                    total = total + jnp.sum(b, axis=-1, keepdims=True)
                o_ref[:, head] += out
                bt_ref[n] = total

        diagonal()

        def step(n, carry):
            group(i - 1 - n)
            return carry

        lax.fori_loop(0, i, step, 0)
        og_ref[...] = (o_ref[...] * _silu_parts(g_ref[...])[0]).astype(BF16)

    qspec = pl.BlockSpec((tq, hp * HEAD_DIM), lambda b, h, i: (b * nq + i, h))
    kspec = pl.BlockSpec((S, hp * HEAD_DIM), lambda b, h, i: (b, h))
    btspec = pl.BlockSpec((None, hp, tq, 1), lambda b, h, i: (b, h, i, 0))
    grid = (Bl, H // hp, nq)
    body, in_specs, out_specs, out_shape, scratch, extra = _ride(
        rider, body, grid, [qspec, kspec, kspec, qspec], [qspec, btspec, qspec],
        [jax.ShapeDtypeStruct((T, C), F32), jax.ShapeDtypeStruct((Bl, H, S, 1), F32),
         jax.ShapeDtypeStruct((T, C), BF16)], [])
    return pl.pallas_call(
        body, name="attn_fwd", grid=grid, in_specs=in_specs, out_specs=out_specs, out_shape=out_shape,
        scratch_shapes=scratch, compiler_params=_params("arbitrary", "arbitrary", "arbitrary"),
    )(qn, kn, v, gate, *extra)


def _attn_bwd(qn, kn, v, do, btot, Bl, S, rider=None):
    T, C = qn.shape
    H = C // HEAD_DIM
    tq, tk, nq, kpq = _attn_tiles(S)
    hp = min(ATT_HEADS, H)
    scale = 1.0 / math.sqrt(HEAD_DIM)

    def body(q_ref, k_ref, v_ref, do_ref, bt_ref, dq_ref, dk_ref, dvb_ref, pb_ref, pdl_ref, dv_ref):
        i = pl.program_id(2)

        @pl.when(i == 0)
        def _():
            dk_ref[...] = jnp.zeros_like(dk_ref)
            dv_ref[...] = jnp.zeros_like(dv_ref)

        tri_le = _tri(tk, "le")
        tri_lt = _tri(tk, "lt")
        rowpos = lax.broadcasted_iota(jnp.int32, (tq, tk), 0) + i * tq
        colpos = lax.broadcasted_iota(jnp.int32, (tq, tk), 1)
        dq_ref[...] = jnp.zeros_like(dq_ref)
        pb_ref[...] = bt_ref[...]
        pdl_ref[...] = jnp.zeros_like(pdl_ref)

        heads = _heads(hp * HEAD_DIM)

        def group(g):
            work = []
            for p in range(kpq):
                rows = pl.ds(pl.multiple_of((g * kpq + p) * tk, tk), tk)
                for head in heads:
                    a, b = _logsig_pair(_dot(q_ref[:, head], k_ref[rows, head], NT) * scale)
                    work.append((head, rows, a, b, _cumdot(b, tri_le), _dot(do_ref[:, head], v_ref[rows, head], NT)))
            for n, head in enumerate(heads):
                remaining = pb_ref[n]
                swept = pdl_ref[n]
                dq = None
                for rows, a, b, cum, dw in [w[1:] for w in work if w[0] == head]:
                    w = jnp.exp(a + (remaining - cum))
                    dl = dw * w
                    prefix = swept + _cumdot(dl, tri_lt)
                    beta = jnp.exp(a)
                    dzb = ((dl * (1.0 - beta) - beta * prefix) * scale).astype(BF16)
                    term = _dot(dzb, k_ref[rows, head])
                    dq = term if dq is None else dq + term
                    dk_ref[rows, head] += _dot(dzb, q_ref[:, head], TN)
                    dv_ref[rows, head] += _dot(w.astype(BF16), do_ref[:, head], TN)
                    remaining = remaining - jnp.sum(b, axis=-1, keepdims=True)
                    swept = swept + jnp.sum(dl, axis=-1, keepdims=True)
                dq_ref[:, head] += dq
                pb_ref[n] = remaining
                pdl_ref[n] = swept

        def diagonal():
            work = []
            for p in range(kpq):
                j = i * kpq + p
                rows = pl.ds(pl.multiple_of(j * tk, tk), tk)
                live = slice(p * tk, tq)
                mask = (colpos[live] + j * tk) < rowpos[live]
                for head in heads:
                    a, b = _logsig_pair(_dot(q_ref[live, head], k_ref[rows, head], NT) * scale)
                    b = jnp.where(mask, b, 0.0)
                    work.append((head, rows, live, a, b, _cumdot(b, tri_le),
                                 _dot(do_ref[live, head], v_ref[rows, head], NT), mask))
            for n, head in enumerate(heads):
                for rows, live, a, b, cum, dw, mask in [w[1:] for w in work if w[0] == head]:
                    w = jnp.where(mask, jnp.exp(a + (pb_ref[n, live] - cum)), 0.0)
                    dl = dw * w
                    prefix = pdl_ref[n, live] + _cumdot(dl, tri_lt)
                    beta = jnp.exp(a)
                    dzb = (jnp.where(mask, dl * (1.0 - beta) - beta * prefix, 0.0) * scale).astype(BF16)
                    dq_ref[live, head] += _dot(dzb, k_ref[rows, head])
                    dk_ref[rows, head] += _dot(dzb, q_ref[live, head], TN)
                    dv_ref[rows, head] += _dot(w.astype(BF16), do_ref[live, head], TN)
                    pb_ref[n, live] -= jnp.sum(b, axis=-1, keepdims=True)
                    pdl_ref[n, live] += jnp.sum(dl, axis=-1, keepdims=True)

        def step(g, carry):
            group(g)
            return carry

        lax.fori_loop(0, i, step, 0)
        diagonal()

        @pl.when(i == nq - 1)
        def _():
            dvb_ref[...] = dv_ref[...].astype(BF16)

    qspec = pl.BlockSpec((tq, hp * HEAD_DIM), lambda b, h, i: (b * nq + i, h))
    kspec = pl.BlockSpec((S, hp * HEAD_DIM), lambda b, h, i: (b, h))
    btspec = pl.BlockSpec((None, hp, tq, 1), lambda b, h, i: (b, h, i, 0))
    grid = (Bl, H // hp, nq)
    body, in_specs, out_specs, out_shape, scratch, extra = _ride(
        rider, body, grid, [qspec, kspec, kspec, qspec, btspec], [qspec, kspec, kspec],
        [jax.ShapeDtypeStruct((T, C), F32), jax.ShapeDtypeStruct((T, C), F32), jax.ShapeDtypeStruct((T, C), BF16)],
        [pltpu.VMEM((hp, tq, 1), F32), pltpu.VMEM((hp, tq, 1), F32), pltpu.VMEM((S, hp * HEAD_DIM), F32)])
    return pl.pallas_call(
        body, name="attn_bwd", grid=grid, in_specs=in_specs, out_specs=out_specs, out_shape=out_shape,
        scratch_shapes=scratch, compiler_params=_params("arbitrary", "arbitrary", "arbitrary"),
    )(qn, kn, v, do, btot, *extra)


SSM_TIME_BLOCK = 512
CHUNK = SUBLANES


def _cmadd(xr, xi, ar, ai, sr, si):
    return xr + ar * sr - ai * si, xi + ar * si + ai * sr


def _chunk_scan(xr, xi, tab_ref, cr, ci, reverse):
    for lvl, d in enumerate((1, 2, 4)):
        shift = (CHUNK - d) if reverse else d
        sr = pltpu.roll(xr, shift, 0)
        si = pltpu.roll(xi, shift, 0)
        ar = tab_ref[pl.ds((2 * lvl) * CHUNK, CHUNK), :]
        ai = tab_ref[pl.ds((2 * lvl + 1) * CHUNK, CHUNK), :]
        xr, xi = _cmadd(xr, xi, ar, ai, sr, si)
    pr = tab_ref[pl.ds(6 * CHUNK, CHUNK), :]
    pi = tab_ref[pl.ds(7 * CHUNK, CHUNK), :]
    return _cmadd(xr, xi, pr, pi, cr, ci)


def _ssm_dims(S, C):
    G = C // GROUP
    GT = min(16, G)
    return G, GT, G // GT, GT * GROUP, GT * STATE, min(SSM_TIME_BLOCK, S)


def _ssm_fwd(u, mats, Bl, S):
    T, C = u.shape
    G, GT, ngt, cw, sw, TB = _ssm_dims(S, C)
    ntb = S // TB
    nch = TB // CHUNK

    def body(u_ref, bre_ref, bim_ref, cre_ref, cim_ref, d_ref, tab_ref, y_ref, hr_ref, hi_ref, yg_ref, car_r, car_i):
        @pl.when(pl.program_id(2) == 0)
        def _():
            car_r[...] = jnp.zeros_like(car_r)
            car_i[...] = jnp.zeros_like(car_i)

        uv = u_ref[...]
        hr_ref[...] = _dot_f32(uv, bre_ref[...])
        hi_ref[...] = _dot_f32(uv, bim_ref[...])

        def step(n, carry):
            cr, ci = carry
            rows = pl.ds(pl.multiple_of(n * CHUNK, CHUNK), CHUNK)
            xr, xi = _chunk_scan(hr_ref[rows, :], hi_ref[rows, :], tab_ref, cr, ci, False)
            hr_ref[rows, :] = xr
            hi_ref[rows, :] = xi
            last = (CHUNK - 1, CHUNK)
            return (jnp.broadcast_to(xr[last[0]:last[1], :], xr.shape), jnp.broadcast_to(xi[last[0]:last[1], :], xi.shape))

        cr, ci = lax.fori_loop(0, nch, step, (car_r[...], car_i[...]))
        car_r[...] = cr
        car_i[...] = ci
        y = _dot_f32(hr_ref[...], cre_ref[...]) - _dot_f32(hi_ref[...], cim_ref[...]) + d_ref[...] * uv
        y_ref[...] = y
        yg_ref[...] = _gelu_parts(y)[0].astype(BF16)

    uspec = pl.BlockSpec((TB, cw), lambda g, b, t: (b * ntb + t, g))
    hspec = pl.BlockSpec((TB, sw), lambda g, b, t: (b * ntb + t, g))

    def gspec(r, c):
        return pl.BlockSpec((None, r, c), lambda g, b, t: (g, 0, 0))

    return pl.pallas_call(
        body, name="ssm_fwd", grid=(ngt, Bl, ntb),
        in_specs=[uspec, gspec(cw, sw), gspec(cw, sw), gspec(sw, cw), gspec(sw, cw), gspec(1, cw), gspec(8 * CHUNK, sw)],
        out_specs=[uspec, hspec, hspec, uspec],
        out_shape=[jax.ShapeDtypeStruct((T, C), F32), jax.ShapeDtypeStruct((T, G * STATE), F32),
                   jax.ShapeDtypeStruct((T, G * STATE), F32), jax.ShapeDtypeStruct((T, C), BF16)],
        scratch_shapes=[pltpu.VMEM((CHUNK, sw), F32), pltpu.VMEM((CHUNK, sw), F32)],
        compiler_params=_params("parallel", "arbitrary", "arbitrary"),
    )(u, mats["bbd_re"], mats["bbd_im"], mats["cbd_re"], mats["cbd_im"], mats["d"], mats["tab_fwd"])


def _ssm_bwd(u, dy, h_re, h_im, mats, Bl, S):
    T, C = u.shape
    G, GT, ngt, cw, sw, TB = _ssm_dims(S, C)
    ntb = S // TB
    nch = TB // CHUNK
    rpb = TB // CHUNK

    def body(u_ref, dy_ref, hr_ref, hi_ref, hpr_ref, hpi_ref, cre_ref, cim_ref, bre_ref, bim_ref, d_ref, tab_ref,
             du_ref, dbr_ref, dbi_ref, dcr_ref, dci_ref, dlr_ref, dli_ref, dd_ref, gr_ref, gi_ref, car_r, car_i):
        b = pl.program_id(1)
        t = pl.program_id(2)

        @pl.when((b == 0) & (t == 0))
        def _():
            for ref in (dbr_ref, dbi_ref, dcr_ref, dci_ref, dlr_ref, dli_ref, dd_ref):
                ref[...] = jnp.zeros_like(ref)

        @pl.when(t == 0)
        def _():
            car_r[...] = jnp.zeros_like(car_r)
            car_i[...] = jnp.zeros_like(car_i)

        uv = u_ref[...]
        dyv = dy_ref[...]
        gr_ref[...] = _dot_f32(dyv, cre_ref[...], NT)
        gi_ref[...] = -_dot_f32(dyv, cim_ref[...], NT)
        alive = jnp.where(t == ntb - 1, 0.0, 1.0)
        row0 = lax.broadcasted_iota(jnp.int32, (CHUNK, sw), 0) == 0

        def step(m, carry):
            cr, ci, ar, ai = carry
            n = nch - 1 - m
            rows = pl.ds(pl.multiple_of(n * CHUNK, CHUNK), CHUNK)
            prow = pl.ds(pl.multiple_of(jnp.maximum(n - 1, 0) * CHUNK, CHUNK), CHUNK)
            xr, xi = _chunk_scan(gr_ref[rows, :], gi_ref[rows, :], tab_ref, cr, ci, True)
            gr_ref[rows, :] = xr
            gi_ref[rows, :] = xi
            first = n == 0
            pr = jnp.where(first, hpr_ref[...] * alive, hr_ref[prow, :])
            pi = jnp.where(first, hpi_ref[...] * alive, hi_ref[prow, :])
            sr = jnp.where(row0, pltpu.roll(pr, 1, 0), pltpu.roll(hr_ref[rows, :], 1, 0))
            si = jnp.where(row0, pltpu.roll(pi, 1, 0), pltpu.roll(hi_ref[rows, :], 1, 0))
            ar = ar + xr * sr + xi * si
            ai = ai + xi * sr - xr * si
            return (jnp.broadcast_to(xr[0:1, :], xr.shape), jnp.broadcast_to(xi[0:1, :], xi.shape), ar, ai)

        zero = jnp.zeros((CHUNK, sw), F32)
        cr, ci, ar, ai = lax.fori_loop(0, nch, step, (car_r[...], car_i[...], zero, zero))
        car_r[...] = cr
        car_i[...] = ci
        dlr_ref[...] += ar
        dli_ref[...] += ai
        gr = gr_ref[...]
        gi = gi_ref[...]
        dbr_ref[...] += _dot_f32(uv, gr, TN)
        dbi_ref[...] += _dot_f32(uv, gi, TN)
        dcr_ref[...] += _dot_f32(hr_ref[...], dyv, TN)
        dci_ref[...] -= _dot_f32(hi_ref[...], dyv, TN)
        dd_ref[...] += jnp.sum(dyv * uv, axis=0, keepdims=True)
        du_ref[...] = (_dot_f32(gr, bre_ref[...], NT) + _dot_f32(gi, bim_ref[...], NT) + d_ref[...] * dyv).astype(BF16)

    def tblk(b, t):
        return b * ntb + (ntb - 1 - t)

    uspec = pl.BlockSpec((TB, cw), lambda g, b, t: (tblk(b, t), g))
    hspec = pl.BlockSpec((TB, sw), lambda g, b, t: (tblk(b, t), g))
    hpspec = pl.BlockSpec((CHUNK, sw), lambda g, b, t: (jnp.maximum(tblk(b, t) * rpb - 1, 0), g))

    def gspec(r, c):
        return pl.BlockSpec((None, r, c), lambda g, b, t: (g, 0, 0))

    def gshape(r, c):
        return jax.ShapeDtypeStruct((ngt, r, c), F32)

    return pl.pallas_call(
        body, name="ssm_bwd", grid=(ngt, Bl, ntb),
        in_specs=[uspec, uspec, hspec, hspec, hpspec, hpspec, gspec(sw, cw), gspec(sw, cw), gspec(cw, sw), gspec(cw, sw),
                  gspec(1, cw), gspec(8 * CHUNK, sw)],
        out_specs=[uspec, gspec(cw, sw), gspec(cw, sw), gspec(sw, cw), gspec(sw, cw), gspec(CHUNK, sw), gspec(CHUNK, sw),
                   gspec(1, cw)],
        out_shape=[jax.ShapeDtypeStruct((T, C), BF16), gshape(cw, sw), gshape(cw, sw), gshape(sw, cw), gshape(sw, cw),
                   gshape(CHUNK, sw), gshape(CHUNK, sw), gshape(1, cw)],
        scratch_shapes=[pltpu.VMEM((TB, sw), F32), pltpu.VMEM((TB, sw), F32), pltpu.VMEM((CHUNK, sw), F32),
                        pltpu.VMEM((CHUNK, sw), F32)],
        compiler_params=_params("arbitrary", "arbitrary", "arbitrary"),
    )(u, dy, h_re, h_im, h_re, h_im, mats["cbd_re"], mats["cbd_im"], mats["bbd_re"], mats["bbd_im"], mats["d"],
      mats["tab_rev"])


def _zoh(a_re, a_im, log_dt, b_re, b_im):
    dt = jnp.exp(log_dt)[:, None]
    mag = jnp.exp(a_re * dt)
    l_re = mag * jnp.cos(a_im * dt)
    l_im = mag * jnp.sin(a_im * dt)
    den = a_re * a_re + a_im * a_im
    f_re = ((l_re - 1.0) * a_re + l_im * a_im) / den
    f_im = (l_im * a_re - (l_re - 1.0) * a_im) / den
    bb_re = f_re[..., None] * b_re - f_im[..., None] * b_im
    bb_im = f_re[..., None] * b_im + f_im[..., None] * b_re
    return l_re, l_im, bb_re, bb_im


def _ssm_matrices(a_re, a_im, log_dt, b_re, b_im, c_re, c_im, d, S):
    G = a_re.shape[0]
    _, GT, ngt, cw, sw, _ = _ssm_dims(S, G * GROUP)
    _, _, bb_re, bb_im = _zoh(a_re, a_im, log_dt, b_re, b_im)
    eye = jnp.eye(GT, dtype=BF16)

    def bd_b(bb):
        return jnp.einsum("tgpi,gh->tgihp", bb.astype(BF16).reshape(ngt, GT, STATE, GROUP), eye).reshape(ngt, cw, sw)

    def bd_c(c):
        return jnp.einsum("tgip,gh->tgphi", c.astype(BF16).reshape(ngt, GT, GROUP, STATE), eye).reshape(ngt, sw, cw)

    dt = jnp.exp(log_dt)[:, None]

    def power(k, conj):
        mag = jnp.exp(k * a_re * dt)
        ang = k * a_im * dt
        return (mag * jnp.cos(ang)).reshape(ngt, 1, sw), ((-1.0 if conj else 1.0) * mag * jnp.sin(ang)).reshape(ngt, 1, sw)

    r = jnp.arange(CHUNK)[None, :, None]

    def table(reverse):
        parts = []
        for dd in (1, 2, 4):
            pr, pi = power(float(dd), reverse)
            keep = (r <= CHUNK - 1 - dd) if reverse else (r >= dd)
            parts += [jnp.where(keep, pr, 0.0), jnp.where(keep, pi, 0.0)]
        exps = [(CHUNK - k) if reverse else (k + 1) for k in range(CHUNK)]
        pw = [power(float(e), reverse) for e in exps]
        parts += [jnp.concatenate([p[0] for p in pw], axis=1), jnp.concatenate([p[1] for p in pw], axis=1)]
        return jnp.concatenate([jnp.broadcast_to(p, (ngt, CHUNK, sw)) for p in parts], axis=1)

    return dict(bbd_re=bd_b(bb_re), bbd_im=bd_b(bb_im), cbd_re=bd_c(c_re), cbd_im=bd_c(c_im), d=d.reshape(ngt, 1, cw),
                tab_fwd=table(False), tab_rev=table(True))


def _ssm_unblock(dbr, dbi, dcr, dci, dlr, dli, dd, G):
    ngt = dbr.shape[0]
    GT = G // ngt
    eye = jnp.eye(GT, dtype=F32)

    def ub(x):
        return jnp.einsum("tgihp,gh->tgpi", x.reshape(ngt, GT, GROUP, GT, STATE), eye).reshape(G, STATE, GROUP)

    def uc(x):
        return jnp.einsum("tgphi,gh->tgip", x.reshape(ngt, GT, STATE, GT, GROUP), eye).reshape(G, GROUP, STATE)

    return (dlr.sum(axis=1).reshape(G, STATE), dli.sum(axis=1).reshape(G, STATE), ub(dbr), ub(dbi), uc(dcr), uc(dci),
            dd.reshape(G * GROUP))


def _attn_gate_bwd(o, gate):
    def fn(dog, o_blk, g_blk):
        val, der = _silu_parts(g_blk)
        return dog * val, dog * o_blk * der

    return _Epilogue(fn, [o, gate], [], [BF16, BF16])


def _glu_fwd(y, gate, glu_b):
    def fn(gl, y_blk, g_blk, b_blk):
        return gl, _gelu_parts(y_blk)[0] * _sigmoid(gl + b_blk) * _silu_parts(g_blk)[0]

    return _Epilogue(fn, [y, gate], [glu_b.reshape(1, -1)], [F32, BF16])


def _glu_bwd(dy3, y, gl, gate, glu_b):
    T, C = y.shape

    def body(d_ref, y_ref, gl_ref, g_ref, b_ref, dgl_ref, dgate_ref, t1_ref, db_ref):
        @pl.when(pl.program_id(0) == 0)
        def _():
            db_ref[...] = jnp.zeros_like(db_ref)

        yg = _gelu_parts(y_ref[...])[0]
        sg = _sigmoid(gl_ref[...] + b_ref[...])
        sl, sld = _silu_parts(g_ref[...])
        dv = d_ref[...]
        dy2 = dv * sl
        dgl = dy2 * yg * sg * (1.0 - sg)
        dgl_ref[...] = dgl.astype(BF16)
        dgate_ref[...] = (dv * (yg * sg) * sld).astype(BF16)
        t1_ref[...] = dy2 * sg
        db_ref[...] += jnp.sum(dgl, axis=0, keepdims=True)

    return _rows_call(body, "glu_bwd", T, C, [dy3, y, gl, gate], [glu_b.reshape(1, C)], [BF16, BF16, F32], [(1, C)])


def _gelu_bwd(t1, y):
    return _Epilogue(lambda t2, t1_blk, y_blk: ((t1_blk + t2) * _gelu_parts(y_blk)[1],), [t1, y], [], [F32])


def _loss_head(x2, target):
    T, C = x2.shape

    def body(x_ref, t_ref, d_ref, db_ref, l_ref):
        @pl.when(pl.program_id(0) == 0)
        def _():
            l_ref[...] = jnp.zeros_like(l_ref)

        e = x_ref[...] - t_ref[...]
        d = e * (1.0 / C)
        d_ref[...] = d
        db_ref[...] = d.astype(BF16)
        l_ref[...] += 0.5 * jnp.sum(jnp.sum(e * e, axis=-1, keepdims=True) * (1.0 / C), axis=0, keepdims=True)

    return _rows_call(body, "loss_head", T, C, [x2, target], [], [F32, BF16], [(1, 1)])


def _adamw(w, g, m, v, name):
    shape = w.shape
    C = shape[-1]
    R = w.size // C
    bc1 = 1.0 - ADAM_B1 ** ADAM_STEP
    bc2 = 1.0 - ADAM_B2 ** ADAM_STEP

    def body(w_ref, g_ref, m_ref, v_ref, d_ref, nm_ref, nv_ref):
        gv = g_ref[...]
        mn = ADAM_B1 * m_ref[...] + (1.0 - ADAM_B1) * gv
        vn = ADAM_B2 * v_ref[...] + (1.0 - ADAM_B2) * (gv * gv)
        d_ref[...] = -ADAM_LR * ((mn / bc1) / (jnp.sqrt(vn / bc2) + ADAM_EPS) + ADAM_WD * w_ref[...])
        nm_ref[...] = mn
        nv_ref[...] = vn

    outs = _rows_call(body, name, R, C, [a.reshape(R, C) for a in (w, g, m, v)], [], [F32, F32, F32])
    return [o.reshape(shape) for o in outs]


def _sum_leading(x, name, half=None):
    n, R, C = x.shape
    tr = _row_tile(R, C * n)
    nb = R // tr

    def body(*refs):
        x_ref, o_ref = refs[-2:]
        acc = x_ref[0].astype(F32)
        for k in range(1, n):
            acc = acc + x_ref[k].astype(F32)
        o_ref[...] = acc

    if half is None:
        return pl.pallas_call(
            body, name=name, grid=(nb,), in_specs=[pl.BlockSpec((n, tr, C), lambda i: (0, i, 0))],
            out_specs=pl.BlockSpec((tr, C), lambda i: (i, 0)), out_shape=jax.ShapeDtypeStruct((R, C), F32),
            compiler_params=_params("parallel"),
        )(x)
    grid_spec = pltpu.PrefetchScalarGridSpec(
        num_scalar_prefetch=1, grid=(nb,), in_specs=[pl.BlockSpec((n, tr, C), lambda i, c_ref: (0, i, 0))],
        out_specs=pl.BlockSpec((tr, C), lambda i, c_ref: (c_ref[0] * nb + i, 0)))
    return pl.pallas_call(
        body, name=name, grid_spec=grid_spec, out_shape=jax.ShapeDtypeStruct((2 * R, C), F32),
        compiler_params=_params("parallel"),
    )(half.reshape(1).astype(jnp.int32), x)


def _add_halves(g, c, name):
    full, recv = g
    n, R, C = full.shape
    half = R // 2
    tr = _row_tile(half, C)
    nb = half // tr

    def body(c_ref, a_ref, b_ref, o_ref):
        o_ref[...] = (a_ref[...] + b_ref[...]).astype(BF16)

    grid_spec = pltpu.PrefetchScalarGridSpec(
        num_scalar_prefetch=1, grid=(n, nb),
        in_specs=[pl.BlockSpec((None, tr, C), lambda j, i, c_ref: (j, c_ref[0] * nb + i, 0)),
                  pl.BlockSpec((None, tr, C), lambda j, i, c_ref: (j, i, 0))],
        out_specs=pl.BlockSpec((None, tr, C), lambda j, i, c_ref: (j, i, 0)))
    return pl.pallas_call(
        body, name=name, grid_spec=grid_spec, out_shape=jax.ShapeDtypeStruct((n, half, C), BF16),
        compiler_params=_params("parallel", "parallel"),
    )(c.reshape(1).astype(jnp.int32), full, recv)


ANY = pl.BlockSpec(memory_space=pl.ANY)


def _position():
    return lax.axis_index("x"), lax.axis_index("y"), lax.axis_index("c")


def _all_gather8(blk, name):
    return _standalone(_all_gather_rider(blk), name)[0]


class _Rider:
    def __init__(self, arrays, out_shapes, sems, start, finish):
        self.arrays, self.out_shapes, self.sems, self.start, self.finish = arrays, out_shapes, sems, start, finish


def _join_riders(*riders):
    def split(refs):
        ins, outs, sems = [], [], []
        for r in riders:
            ins.append(refs[:len(r.arrays)])
            refs = refs[len(r.arrays):]
        for r in riders:
            outs.append(refs[:len(r.out_shapes)])
            refs = refs[len(r.out_shapes):]
        for r in riders:
            sems.append(refs[:len(r.sems)])
            refs = refs[len(r.sems):]
        return [i + o + s for i, o, s in zip(ins, outs, sems)]

    def start(*refs):
        for r, own in zip(riders, split(refs)):
            r.start(*own)

    def finish(*refs):
        for r, own in zip(riders, split(refs)):
            r.finish(*own)

    return _Rider([a for r in riders for a in r.arrays], [s for r in riders for s in r.out_shapes],
                  [s for r in riders for s in r.sems], start, finish)


def _standalone(rider, name):
    def body(*refs):
        rider.start(*refs)
        rider.finish(*refs)

    return pl.pallas_call(
        body, name=name, in_specs=[ANY] * len(rider.arrays), out_specs=[ANY] * len(rider.out_shapes),
        out_shape=rider.out_shapes, scratch_shapes=rider.sems,
    )(*rider.arrays)


def _ride(rider, body, grid, in_specs, out_specs, out_shape, scratch):
    if rider is None:
        return body, in_specs, out_specs, out_shape, scratch, []
    ni, no, ns = len(in_specs), len(out_specs), len(scratch)
    ri, ro = len(rider.arrays), len(rider.out_shapes)

    def full(*refs):
        ins, refs = refs[:ni], refs[ni:]
        r_ins, refs = refs[:ri], refs[ri:]
        outs, refs = refs[:no], refs[no:]
        r_outs, refs = refs[:ro], refs[ro:]
        scr, r_sems = refs[:ns], refs[ns:]
        ids = [pl.program_id(a) for a in range(len(grid))]
        first = functools.reduce(jnp.logical_and, [i == 0 for i in ids])
        last = functools.reduce(jnp.logical_and, [i == g - 1 for i, g in zip(ids, grid)])

        @pl.when(first)
        def _():
            rider.start(*r_ins, *r_outs, *r_sems)

        body(*ins, *outs, *scr)

        @pl.when(last)
        def _():
            rider.finish(*r_ins, *r_outs, *r_sems)

    return (full, in_specs + [ANY] * ri, out_specs + [ANY] * ro, out_shape + rider.out_shapes, scratch + rider.sems,
            rider.arrays)


def _all_gather_rider(blk):
    M, N = blk.shape

    def copies(x_ref, out_ref, send_sems, recv_sems, local_sem):
        x, y, c = _position()
        me, sibling = (x, y, c), (x, y, 1 - c)
        chips = [(1 - x, y), (x, 1 - y), (1 - x, 1 - y)]

        def slab(px, py, pc):
            return out_ref.at[4 * px + 2 * py + pc]

        def copy(k, block, to, src=None):
            return pltpu.make_async_remote_copy(
                src_ref=slab(*block) if src is None else src, dst_ref=slab(*block),
                send_sem=send_sems.at[k], recv_sem=recv_sems.at[k], device_id=to, device_id_type=MESH)

        mine = pltpu.make_async_copy(x_ref, slab(*me), local_sem)
        first = [copy(0, me, sibling, src=x_ref)]
        first += [copy(1 + j, me, (*chip, c), src=x_ref) for j, chip in enumerate(chips)]
        passed = [copy(4 + j, (*chip, c), sibling) for j, chip in enumerate(chips)]
        arrivals = [copy(1 + j, (*chip, c), me) for j, chip in enumerate(chips)]
        from_sibling = [copy(0, sibling, me)] + [copy(4 + j, (*chip, 1 - c), me) for j, chip in enumerate(chips)]
        return mine, first, passed, arrivals, from_sibling

    def start(*refs):
        mine, first, _, _, _ = copies(*refs)
        mine.start()
        for cp in first:
            cp.start()

    def finish(*refs):
        mine, first, passed, arrivals, from_sibling = copies(*refs)
        for arrival, onward in zip(arrivals, passed):
            arrival.wait_recv()
            onward.start()
        for cp in from_sibling:
            cp.wait_recv()
        for cp in first + passed:
            cp.wait_send()
        mine.wait()

    return _Rider([blk], [jax.ShapeDtypeStruct((N_DEV, M, N), blk.dtype)],
                  [pltpu.SemaphoreType.DMA((7,)), pltpu.SemaphoreType.DMA((7,)), pltpu.SemaphoreType.DMA], start, finish)


def _all_gather_forwarding(blk, name):
    M, N = blk.shape
    half = M // 2

    def body(x_ref, out_ref, send_sems, recv_sems, local_sem):
        x, y, c = _position()
        me, sibling = (x, y, c), (x, y, 1 - c)
        xn, yn, dg = (1 - x, y), (x, 1 - y), (1 - x, 1 - y)

        def slab(chip, pc, part=None):
            ref = out_ref.at[4 * chip[0] + 2 * chip[1] + pc]
            return ref if part is None else ref.at[pl.ds(part * half, half), :]

        def copy(k, place, to, src=None):
            return pltpu.make_async_remote_copy(src_ref=place if src is None else src, dst_ref=place,
                                                send_sem=send_sems.at[k], recv_sem=recv_sems.at[k], device_id=to,
                                                device_id_type=MESH)

        mine = pltpu.make_async_copy(x_ref, slab((x, y), c), local_sem)
        own = [copy(0, slab((x, y), c), sibling, src=x_ref), copy(1, slab((x, y), c), (*xn, c), src=x_ref),
               copy(2, slab((x, y), c), (*yn, c), src=x_ref)]
        mine.start()
        for cp in own:
            cp.start()
        copy(1, slab(xn, c), me).wait_recv()
        onward = [copy(3, slab(xn, c, 0), (*yn, c)), copy(5, slab(xn, c), sibling)]
        for cp in onward:
            cp.start()
        copy(2, slab(yn, c), me).wait_recv()
        onward += [copy(4, slab(yn, c, 1), (*xn, c)), copy(6, slab(yn, c), sibling)]
        for cp in onward[2:]:
            cp.start()
        copy(3, slab(dg, c, 0), me).wait_recv()
        onward.append(copy(7, slab(dg, c, 0), sibling))
        onward[-1].start()
        copy(4, slab(dg, c, 1), me).wait_recv()
        onward.append(copy(8, slab(dg, c, 1), sibling))
        onward[-1].start()
        for k, place in ((0, slab((x, y), 1 - c)), (5, slab(xn, 1 - c)), (6, slab(yn, 1 - c)), (7, slab(dg, 1 - c, 0)),
                         (8, slab(dg, 1 - c, 1))):
            copy(k, place, me).wait_recv()
        for cp in own + onward:
            cp.wait_send()
        mine.wait()

    return pl.pallas_call(
        body, name=name, in_specs=[ANY], out_specs=ANY, out_shape=jax.ShapeDtypeStruct((N_DEV, M, N), blk.dtype),
        scratch_shapes=[pltpu.SemaphoreType.DMA((9,)), pltpu.SemaphoreType.DMA((9,)), pltpu.SemaphoreType.DMA],
    )(blk)


def _sibling_half_rider(g):
    n, R, C = g.shape
    half = R // 2

    def copy(g_ref, out_ref, send_sem, recv_sem):
        x, y, c = _position()
        return pltpu.make_async_remote_copy(
            src_ref=g_ref.at[:, pl.ds((1 - c) * half, half), :], dst_ref=out_ref, send_sem=send_sem, recv_sem=recv_sem,
            device_id=(x, y, 1 - c), device_id_type=MESH)

    return _Rider([g], [jax.ShapeDtypeStruct((n, half, C), g.dtype)], [pltpu.SemaphoreType.DMA, pltpu.SemaphoreType.DMA],
                  lambda *refs: copy(*refs).start(), lambda *refs: copy(*refs).wait())


def _sibling_send_half(g, name):
    return _standalone(_sibling_half_rider(g), name)[0]


def _chip_exchange_rider(p, row_off=0, rows=None):
    rows = p.shape[1] if rows is None else rows

    def copies(p_ref, out_ref, send_sems, recv_sems, local_sem):
        x, y, c = _position()
        my = 2 * x + y
        chips = [(1 - x, y), (x, 1 - y), (1 - x, 1 - y)]

        def src(slab):
            return p_ref.at[slab, pl.ds(row_off, rows), :]

        mine = pltpu.make_async_copy(src(my), out_ref.at[my], local_sem)
        sends = [pltpu.make_async_remote_copy(
            src_ref=src(2 * px + py), dst_ref=out_ref.at[my], send_sem=send_sems.at[k], recv_sem=recv_sems.at[k],
            device_id=(px, py, c), device_id_type=MESH) for k, (px, py) in enumerate(chips)]
        arrivals = [pltpu.make_async_remote_copy(
            src_ref=src(my), dst_ref=out_ref.at[2 * px + py], send_sem=send_sems.at[k], recv_sem=recv_sems.at[k],
            device_id=(px, py, c), device_id_type=MESH) for k, (px, py) in enumerate(chips)]
        return mine, sends, arrivals

    def start(*refs):
        mine, sends, _ = copies(*refs)
        mine.start()
        for cp in sends:
            cp.start()

    def finish(*refs):
        mine, sends, arrivals = copies(*refs)
        for cp in arrivals:
            cp.wait_recv()
        for cp in sends:
            cp.wait_send()
        mine.wait()

    return _Rider([p], [jax.ShapeDtypeStruct((p.shape[0], rows, p.shape[2]), p.dtype)],
                  [pltpu.SemaphoreType.DMA((3,)), pltpu.SemaphoreType.DMA((3,)), pltpu.SemaphoreType.DMA], start, finish)


def _sibling_fill(buf, name):
    M = buf.shape[0] // 2

    def body(buf_ref, out_ref, send_sem, recv_sem):
        x, y, c = _position()
        mine = pl.ds(c * M, M)
        theirs = pl.ds((1 - c) * M, M)
        pltpu.make_async_remote_copy(src_ref=buf_ref.at[mine], dst_ref=out_ref.at[mine], send_sem=send_sem,
                                     recv_sem=recv_sem, device_id=(x, y, 1 - c), device_id_type=MESH).start()
        pltpu.make_async_remote_copy(src_ref=buf_ref.at[mine], dst_ref=out_ref.at[theirs], send_sem=send_sem,
                                     recv_sem=recv_sem, device_id=(x, y, 1 - c), device_id_type=MESH).wait()

    return pl.pallas_call(
        body, name=name, in_specs=[ANY], out_specs=ANY, out_shape=jax.ShapeDtypeStruct(buf.shape, buf.dtype),
        scratch_shapes=[pltpu.SemaphoreType.DMA, pltpu.SemaphoreType.DMA], input_output_aliases={0: 0},
    )(buf)


def _pack_rest(attn_out, ssm_in, glu_w, ssm_out):
    hd = ssm_in.shape[0] // 2
    return jnp.concatenate([attn_out, jnp.concatenate([ssm_in[:hd], ssm_in[hd:]], axis=1), glu_w, ssm_out], axis=0)


def _unpack_rest(p):
    D = p.shape[-1]
    q, hd = D // N_CHIPS, D // 2
    o = [0, q, q + hd, 2 * q + hd, 3 * q + hd]
    ssm_in = p[o[1]:o[2]]
    return p[o[0]:o[1]], jnp.concatenate([ssm_in[:, :hd], ssm_in[:, hd:]], axis=0), p[o[2]:o[3]], p[o[3]:o[4]]


def _attn_in_views(buf_a):
    D = buf_a.shape[-1]
    return [_View(buf_a, (D, D), lambda r, c, j=j: (j, r, c), D, D) for j in range(N_CHIPS)]


def _rest_views(buf_b):
    D = buf_b.shape[-1]
    q, hd = D // N_CHIPS, D // 2
    o_out, o_in, o_glu, o_sout = 0, q, q + hd, 2 * q + hd

    def row_sharded(off):
        return _View(buf_b, (D, D), lambda r, c: (r // q, off + r % q, c), math.gcd(q, off), D)

    def ssm_in(part):
        return _View(buf_b, (D, D), lambda r, c: (2 * part + c // hd, o_in + r % hd, (r // hd) * hd + c % hd),
                     math.gcd(hd, o_in), hd)

    return dict(attn_out=row_sharded(o_out), ssm_in_u=ssm_in(0), ssm_in_gate=ssm_in(1), glu=row_sharded(o_glu),
                ssm_out=row_sharded(o_sout))


def _pack_small(parts):
    flat = jnp.concatenate([p.reshape(-1) for p in parts])
    pad = (-flat.size) % (2 * SUBLANES * LANES)
    return jnp.pad(flat, (0, pad)).reshape(-1, LANES)


def _unpack_small(buf, shapes):
    flat = buf.reshape(-1)
    out, off = [], 0
    for s in shapes:
        n = math.prod(s)
        out.append(flat[off:off + n].reshape(s))
        off += n
    return out


def _local_step(x, target, norm_g, q_g, k_g, wa, wb, ssm_small, core=None):
    Bl, S, D = x.shape
    T = Bl * S
    x0 = x.reshape(T, D)
    tgt = target.reshape(T, D)
    a_re, a_im, log_dt, b_re, b_im, c_re, c_im, d_skip, glu_b = ssm_small
    G = a_re.shape[0]
    mats = _ssm_matrices(a_re, a_im, log_dt, b_re, b_im, c_re, c_im, d_skip, S)
    w_in = _attn_in_views(wa)
    rows_b = wb.shape[-2] * (1 if core is None else 2)
    ga = lax.empty(wa.shape, F32)
    gb = lax.empty((N_CHIPS, rows_b, D), F32)

    def wgrad(a, b, key, name):
        return _matmul(a, b, name=name, ta=True, out=_rest_views(gb)[key])

    h0 = _rmsnorm_fwd(x0, norm_g[0], "norm0_fwd")
    q, k, v, gate = [_matmul(h0, w_in[j], name=f"attn_in_{j}", out_dtype=(BF16 if j == 2 else F32)) for j in range(4)]
    qn = _qknorm_fwd(q, q_g, "qnorm_fwd")
    kn = _qknorm_fwd(k, k_g, "knorm_fwd")
    if core is None:
        o, btot, og = _attn_fwd(qn, kn, v, gate, Bl, S)
    else:
        o, btot, og, wb = _attn_fwd(qn, kn, v, gate, Bl, S, rider=_all_gather_rider(wb))
        wb = wb.reshape(N_CHIPS, rows_b, D)
    w = _rest_views(wb)
    x1 = _matmul(og, w["attn_out"], name="attn_out", residual=x0)

    h1 = _rmsnorm_fwd(x1, norm_g[1], "norm1_fwd")
    u = _matmul(h1, w["ssm_in_u"], name="ssm_in_u")
    gate2 = _matmul(h1, w["ssm_in_gate"], name="ssm_in_gate")
    y, hs_re, hs_im, yg = _ssm_fwd(u, mats, Bl, S)
    gl, y3 = _matmul(yg, w["glu"], name="glu_mm", epilogue=_glu_fwd(y, gate2, glu_b))
    x2 = _matmul(y3, w["ssm_out"], name="ssm_out", residual=x1)

    dx2, dx2b, loss = _loss_head(x2, tgt)

    dy3 = _matmul(dx2b, w["ssm_out"], name="ssm_out_dgrad", tb=True)
    gb = wgrad(y3, dx2b, "ssm_out", "ssm_out_wgrad")
    dgl, dgate2, t1, dglu_b = _glu_bwd(dy3, y, gl, gate2, glu_b)
    dy = _matmul(dgl, w["glu"], name="glu_dgrad", tb=True, epilogue=_gelu_bwd(t1, y))
    gb = wgrad(yg, dgl, "glu", "glu_wgrad")
    du, dbr, dbi, dcr, dci, dlr, dli, dd = _ssm_bwd(u, dy, hs_re, hs_im, mats, Bl, S)
    dh1 = _matmul(du, w["ssm_in_u"], name="ssm_in_dgrad_u", tb=True)
    dh1 = _matmul(dgate2, w["ssm_in_gate"], name="ssm_in_dgrad_gate", tb=True, residual=dh1)
    gb = wgrad(h1, du, "ssm_in_u", "ssm_in_wgrad_u")
    gb = wgrad(h1, dgate2, "ssm_in_gate", "ssm_in_wgrad_gate")
    dx1, dx1b, dng1 = _rmsnorm_bwd(x1, norm_g[1], dh1, dx2, "norm1_bwd")
    small_early = (dng1,) + _ssm_unblock(dbr, dbi, dcr, dci, dlr, dli, dd, G) + (dglu_b.reshape(D),)

    gb = wgrad(og, dx1b, "attn_out", "attn_out_wgrad")
    if core is None:
        do, dgate = _matmul(dx1b, w["attn_out"], name="attn_out_dgrad", tb=True, epilogue=_attn_gate_bwd(o, gate))
        dqn, dkn, dv = _attn_bwd(qn, kn, v, do, btot, Bl, S)
    else:
        do, dgate, from_sibling = _matmul(dx1b, w["attn_out"], name="attn_out_dgrad", tb=True,
                                          epilogue=_attn_gate_bwd(o, gate), rider=_sibling_half_rider(gb))
        chip_sum_b = _add_halves((gb, from_sibling), core, "grads_b_add_halves")
        dqn, dkn, dv, gb, small_early = _attn_bwd(
            qn, kn, v, do, btot, Bl, S,
            rider=_join_riders(_chip_exchange_rider(chip_sum_b), _all_gather_rider(_pack_small(small_early))))
    dq, dqg = _qknorm_bwd(q, q_g, dqn, "qnorm_bwd")
    dk, dkg = _qknorm_bwd(k, k_g, dkn, "knorm_bwd")
    dproj = [dq, dk, dv, dgate]
    for j in range(4):
        ga = _matmul(h0, dproj[j], name=f"attn_in_wgrad_{j}", ta=True, out=_attn_in_views(ga)[j])
    dh0 = None
    if core is None:
        for j in range(4):
            dh0 = _matmul(dproj[j], w_in[j], name=f"attn_in_dgrad_{j}", tb=True, residual=dh0)
        dx0, _, dng0 = _rmsnorm_bwd(x0, norm_g[0], dh0, dx1, "norm0_bwd")
    else:
        chip_sum_a = _add_halves((ga, _sibling_send_half(ga, "grads_a_sibling_half")), core, "grads_a_add_halves")
        total = chip_sum_a.shape[1]
        rows = min(-(-(total * 7 // 32) // 16) * 16, (total - 16) // 4 // 16 * 16)
        parts = []
        for j in range(4):
            dh0, part = _matmul(dproj[j], w_in[j], name=f"attn_in_dgrad_{j}", tb=True, residual=dh0,
                                rider=_chip_exchange_rider(chip_sum_a, j * rows, rows))
            parts.append(part)
        dx0, _, dng0, part = _rmsnorm_bwd(x0, norm_g[0], dh0, dx1, "norm0_bwd",
                                          rider=_chip_exchange_rider(chip_sum_a, 4 * rows, total - 4 * rows))
        ga = jnp.concatenate(parts + [part], axis=1)

    return loss, dx0.reshape(Bl, S, D), ga, gb, small_early, (dng0, dqg, dkg)


def _small_early_shapes(D):
    G = D // GROUP
    return [(1, D), (G, STATE), (G, STATE), (G, STATE, GROUP), (G, STATE, GROUP), (G, GROUP, STATE), (G, GROUP, STATE),
            (D,), (D,)]


def _chip_rows(a, chip, n_per):
    return lax.dynamic_slice_in_dim(a, chip * n_per, n_per, axis=0)


def kernel(x, norm_g, attn_w_in, attn_q_g, attn_k_g, attn_w_out, ssm_w_in, ssm_A_re, ssm_A_im, ssm_log_dt, ssm_B_re, ssm_B_im, ssm_C_re, ssm_C_im, ssm_D, ssm_glu_w, ssm_glu_b, ssm_w_out, loss_target, m_norm_g, m_attn_w_in, m_attn_q_g, m_attn_k_g, m_attn_w_out, m_ssm_w_in, m_ssm_A_re, m_ssm_A_im, m_ssm_log_dt, m_ssm_B_re, m_ssm_B_im, m_ssm_C_re, m_ssm_C_im, m_ssm_D, m_ssm_glu_w, m_ssm_glu_b, m_ssm_w_out, v_norm_g, v_attn_w_in, v_attn_q_g, v_attn_k_g, v_attn_w_out, v_ssm_w_in, v_ssm_A_re, v_ssm_A_im, v_ssm_log_dt, v_ssm_B_re, v_ssm_B_im, v_ssm_C_re, v_ssm_C_im, v_ssm_D, v_ssm_glu_w, v_ssm_glu_b, v_ssm_w_out):
    D = x.shape[-1]
    cx, cy, cc = _position()
    chip = 2 * cx + cy
    G = D // GROUP
    Gl = G // N_CHIPS

    def my_half(a):
        return lax.dynamic_slice_in_dim(a, cc * (a.shape[0] // 2), a.shape[0] // 2, axis=0)

    wa = _all_gather_forwarding(my_half(attn_w_in[0].astype(BF16)), "attn_in_all_gather").reshape(N_CHIPS, D, D)
    wb_half = my_half(_pack_rest(attn_w_out[0], ssm_w_in[0], ssm_glu_w[0], ssm_w_out[0]).astype(BF16))

    ssm_local = [ssm_A_re[0], ssm_A_im[0], ssm_log_dt[0], ssm_B_re[0], ssm_B_im[0], ssm_C_re[0], ssm_C_im[0], ssm_D[0],
                 ssm_glu_b[0]]
    small_local = _pack_small(ssm_local)
    half_rows = small_local.shape[0] // 2
    small_half = lax.dynamic_slice_in_dim(small_local, cc * half_rows, half_rows, axis=0)
    small_all = _all_gather8(small_half, "ssm_params_all_gather").reshape(N_CHIPS, 2 * half_rows, LANES)
    per_chip = [_unpack_small(small_all[j], [p.shape for p in ssm_local]) for j in range(N_CHIPS)]
    ssm_full = [jnp.concatenate([per_chip[j][i] for j in range(N_CHIPS)], axis=0) for i in range(len(ssm_local))]

    loss, grad_x, gathered_a, gathered_b, small_early, small_late = _local_step(
        x, loss_target, norm_g, attn_q_g[0], attn_k_g[0], wa, wb_half, ssm_full, core=cc)
    loss = lax.psum(loss[0, 0], ("x", "y", "c"))

    q4 = D // N_CHIPS
    def both_halves(gathered, tag):
        return _sibling_fill(_sum_leading(gathered, f"grads_{tag}_sum_chips", half=cc), f"grads_{tag}_sibling_fill")

    gd = (both_halves(gathered_a, "a"),) + _unpack_rest(both_halves(gathered_b, "b"))

    (dng1, dl_re, dl_im, dbb_re, dbb_im, dc_re, dc_im, dd_skip, dglu_b) = _unpack_small(
        _sum_leading(small_early, "small_early_sum"), _small_early_shapes(D))
    late_all = _all_gather8(_pack_small(small_late), "small_late_all_gather")
    dng0, dqg, dkg = _unpack_small(_sum_leading(late_all, "small_late_sum"), [s.shape for s in small_late])
    dng = jnp.concatenate([dng0, dng1], axis=0)
    a_re, a_im, log_dt, b_re, b_im = ssm_local[:5]
    _, zoh_vjp = jax.vjp(_zoh, a_re, a_im, log_dt, b_re, b_im)
    da_re, da_im, dlog_dt, db_re, db_im = zoh_vjp((_chip_rows(dl_re, chip, Gl), _chip_rows(dl_im, chip, Gl),
                                                   _chip_rows(dbb_re, chip, Gl), _chip_rows(dbb_im, chip, Gl)))
    grads = {
        "norm_g": dng, "attn_w_in": gd[0][None], "attn_q_g": dqg, "attn_k_g": dkg, "attn_w_out": gd[1][None],
        "ssm_w_in": gd[2][None], "ssm_A_re": da_re[None], "ssm_A_im": da_im[None], "ssm_log_dt": dlog_dt[None],
        "ssm_B_re": db_re[None], "ssm_B_im": db_im[None], "ssm_C_re": _chip_rows(dc_re, chip, Gl)[None],
        "ssm_C_im": _chip_rows(dc_im, chip, Gl)[None], "ssm_D": _chip_rows(dd_skip, chip, q4)[None],
        "ssm_glu_w": gd[3][None], "ssm_glu_b": _chip_rows(dglu_b, chip, q4)[None], "ssm_w_out": gd[4][None],
    }
    weights = dict(norm_g=norm_g, attn_w_in=attn_w_in, attn_q_g=attn_q_g, attn_k_g=attn_k_g, attn_w_out=attn_w_out,
                   ssm_w_in=ssm_w_in, ssm_A_re=ssm_A_re, ssm_A_im=ssm_A_im, ssm_log_dt=ssm_log_dt, ssm_B_re=ssm_B_re,
                   ssm_B_im=ssm_B_im, ssm_C_re=ssm_C_re, ssm_C_im=ssm_C_im, ssm_D=ssm_D, ssm_glu_w=ssm_glu_w,
                   ssm_glu_b=ssm_glu_b, ssm_w_out=ssm_w_out)
    m = dict(norm_g=m_norm_g, attn_w_in=m_attn_w_in, attn_q_g=m_attn_q_g, attn_k_g=m_attn_k_g, attn_w_out=m_attn_w_out,
             ssm_w_in=m_ssm_w_in, ssm_A_re=m_ssm_A_re, ssm_A_im=m_ssm_A_im, ssm_log_dt=m_ssm_log_dt, ssm_B_re=m_ssm_B_re,
             ssm_B_im=m_ssm_B_im, ssm_C_re=m_ssm_C_re, ssm_C_im=m_ssm_C_im, ssm_D=m_ssm_D, ssm_glu_w=m_ssm_glu_w,
             ssm_glu_b=m_ssm_glu_b, ssm_w_out=m_ssm_w_out)
    v = dict(norm_g=v_norm_g, attn_w_in=v_attn_w_in, attn_q_g=v_attn_q_g, attn_k_g=v_attn_k_g, attn_w_out=v_attn_w_out,
             ssm_w_in=v_ssm_w_in, ssm_A_re=v_ssm_A_re, ssm_A_im=v_ssm_A_im, ssm_log_dt=v_ssm_log_dt, ssm_B_re=v_ssm_B_re,
             ssm_B_im=v_ssm_B_im, ssm_C_re=v_ssm_C_re, ssm_C_im=v_ssm_C_im, ssm_D=v_ssm_D, ssm_glu_w=v_ssm_glu_w,
             ssm_glu_b=v_ssm_glu_b, ssm_w_out=v_ssm_w_out)
    names = list(weights)
    dense_names = ("attn_w_in", "attn_w_out", "ssm_w_in", "ssm_glu_w", "ssm_w_out")
    delta, new_m, new_v = {}, {}, {}
    for n in dense_names:
        delta[n], new_m[n], new_v[n] = _adamw(weights[n], grads[n], m[n], v[n], "adamw_" + n)
    small_names = [n for n in names if n not in dense_names]
    small_shapes = [weights[n].shape for n in small_names]
    packs = [_pack_small([d[n] for n in small_names]) for d in (weights, grads, m, v)]
    outs = _adamw(*packs, "adamw_small")
    for res, out in zip((delta, new_m, new_v), outs):
        for n, val in zip(small_names, _unpack_small(out, small_shapes)):
            res[n] = val
    return (loss, grad_x, *[grads[n] for n in names], *[delta[n] for n in names], *[new_m[n] for n in names],
            *[new_v[n] for n in names])
```

```python
import functools
import math

import jax
import jax.numpy as jnp
from jax import lax
from jax.experimental import pallas as pl
from jax.experimental.pallas import tpu as pltpu

F32 = jnp.float32
BF16 = jnp.bfloat16

HEAD_DIM = 128
GROUP = 16
STATE = 64
RMS_EPS = 1e-6
ADAM_LR = 0.001
ADAM_B1 = 0.9
ADAM_B2 = 0.999
ADAM_EPS = 1e-08
ADAM_WD = 0.01
ADAM_STEP = 10

N_CHIPS = 4
N_DEV = 8
SUBLANES = 8
BF16_ROWS = 16
LANES = 128
VMEM_LIMIT = 56 * 1024 * 1024
ROW_BLOCK_ELEMS = 1 << 19
MATMUL_TILE = 1024
MATMUL_PANEL_BYTES = 8 * 1024 * 1024
MESH = pl.DeviceIdType.MESH

NN = (((1,), (0,)), ((), ()))
NT = (((1,), (1,)), ((), ()))
TN = (((0,), (0,)), ((), ()))


def _params(*sem):
    return pltpu.CompilerParams(dimension_semantics=sem, vmem_limit_bytes=VMEM_LIMIT)


def _dot(a, b, dims=NN):
    return lax.dot_general(a, b, dims, preferred_element_type=F32)


def _split(a):
    hi = a.astype(BF16)
    lo = (a - hi.astype(F32)).astype(BF16)
    return hi, lo


def _dot_f32(a, b, dims=NN):
    return _dot(a.astype(BF16), b.astype(BF16), dims)


def _sigmoid(x):
    return 1.0 / (1.0 + jnp.exp(-x))


def _silu_parts(x):
    s = _sigmoid(x)
    return x * s, s * (1.0 + x * (1.0 - s))


_GELU_C = math.sqrt(2.0 / math.pi)


def _gelu_parts(x):
    x2 = x * x
    t = jnp.tanh(_GELU_C * (x + 0.044715 * x * x2))
    val = 0.5 * x * (1.0 + t)
    der = 0.5 * (1.0 + t) + 0.5 * x * (1.0 - t * t) * _GELU_C * (1.0 + 3.0 * 0.044715 * x2)
    return val, der


class _View:
    def __init__(self, buf, shape, locate, row_tile, col_tile):
        self.buf, self.shape, self.locate, self.row_tile, self.col_tile = buf, shape, locate, row_tile, col_tile

    def spec(self, t0, t1, block_of):
        def index(i, j, k):
            rb, cb = block_of(i, j, k)
            slab, r, c = self.locate(rb * t0, cb * t1)
            return slab, r // t0, c // t1
        return pl.BlockSpec((None, t0, t1), index)


def _operand(x):
    return (x.buf, x.shape, x.row_tile, x.col_tile) if isinstance(x, _View) else (x, x.shape, x.shape[0], x.shape[1])


class _Epilogue:
    def __init__(self, fn, tiles, rows, out_dtypes):
        self.fn, self.tiles, self.rows, self.out_dtypes = fn, tiles, rows, out_dtypes


def _matmul(a, b, *, name, ta=False, tb=False, residual=None, out_dtype=F32, out=None, rider=None, epilogue=None):
    if residual is not None:
        epilogue = _Epilogue(lambda r, res: (r + res.astype(F32),), [residual], [], [out_dtype])
    elif epilogue is None:
        epilogue = _Epilogue(lambda r: (r,), [], [], [out_dtype])
    n_extra, n_out = len(epilogue.tiles) + len(epilogue.rows), len(epilogue.out_dtypes)
    a_arr, a_shape, a_rt, a_ct = _operand(a)
    b_arr, b_shape, b_rt, b_ct = _operand(b)
    (K, M) = a_shape if ta else a_shape[::-1]
    N = b_shape[0] if tb else b_shape[1]
    a_mt, a_kt = (a_ct, a_rt) if ta else (a_rt, a_ct)
    b_nt, b_kt = (b_rt, b_ct) if tb else (b_ct, b_rt)
    k_cap = MATMUL_PANEL_BYTES // (MATMUL_TILE * max(a_arr.dtype.itemsize, b_arr.dtype.itemsize))
    n_cap = MATMUL_TILE if n_extra + n_out <= 2 else MATMUL_TILE // 2
    tm, tn, tk = min(M, MATMUL_TILE, a_mt), min(N, n_cap, b_nt), min(K, k_cap, a_kt)
    if out is not None:
        tm, tn = min(tm, out.row_tile), min(tn, out.col_tile)
    pk = min(tk, b_kt)
    pieces = tk // pk
    nk = K // tk
    dims = ((((0,) if ta else (1,)), ((1,) if tb else (0,))), ((), ()))
    n_in = 1 + pieces + n_extra + (out is not None)

    def body(*refs):
        a_ref, b_refs = refs[0], refs[1:1 + pieces]
        e_refs = refs[1 + pieces:1 + pieces + n_extra]
        o_refs = refs[n_in:n_in + n_out]

        def finish(r):
            for o_ref, val in zip(o_refs, epilogue.fn(r, *[e[...] for e in e_refs])):
                o_ref[...] = val.astype(o_ref.dtype)

        part = None
        for p, b_ref in enumerate(b_refs):
            ks = slice(p * pk, (p + 1) * pk)
            a_blk = a_ref[...] if pieces == 1 else (a_ref[ks, :] if ta else a_ref[:, ks])
            term = _dot(a_blk.astype(BF16), b_ref[...].astype(BF16), dims)
            part = term if part is None else part + term
        if nk == 1:
            finish(part)
            return
        acc = refs[n_in + n_out]
        k = pl.program_id(2)

        @pl.when(k == 0)
        def _():
            acc[...] = part

        @pl.when(k > 0)
        def _():
            acc[...] += part

        @pl.when(k == nk - 1)
        def _():
            finish(acc[...])

    def spec(x, t0, t1, block_of):
        if isinstance(x, _View):
            return x.spec(t0, t1, block_of)
        return pl.BlockSpec((t0, t1), block_of)

    a_spec = spec(a, tk, tm, lambda i, j, k: (k, i)) if ta else spec(a, tm, tk, lambda i, j, k: (i, k))
    b_specs = [spec(b, tn, pk, lambda i, j, k, p=p: (j, k * pieces + p)) if tb else
               spec(b, pk, tn, lambda i, j, k, p=p: (k * pieces + p, j)) for p in range(pieces)]
    tile_spec = pl.BlockSpec((tm, tn), lambda i, j, k: (i, j))
    in_specs = [a_spec] + b_specs + [tile_spec] * len(epilogue.tiles) + [
        pl.BlockSpec((1, tn), lambda i, j, k: (0, j))] * len(epilogue.rows)
    args = [a_arr] + [b_arr] * pieces + list(epilogue.tiles) + list(epilogue.rows)
    aliases = {}
    if out is None:
        out_specs = [tile_spec] * n_out
        out_shape = [jax.ShapeDtypeStruct((M, N), d) for d in epilogue.out_dtypes]
    else:
        out_specs = [out.spec(tm, tn, lambda i, j, k: (i, j))]
        out_shape = [jax.ShapeDtypeStruct(out.buf.shape, out.buf.dtype)]
        in_specs.append(pl.BlockSpec(memory_space=pl.ANY))
        args.append(out.buf)
        aliases = {len(args) - 1: 0}
    grid = (M // tm, N // tn, nk)
    scratch = [pltpu.VMEM((tm, tn), F32)] if nk > 1 else []
    body, in_specs, out_specs, out_shape, scratch, extra = _ride(rider, body, grid, in_specs, out_specs, out_shape,
                                                                 scratch)
    results = pl.pallas_call(
        body, name=name, grid=grid, in_specs=in_specs, out_specs=out_specs, out_shape=out_shape,
        scratch_shapes=scratch, input_output_aliases=aliases,
        compiler_params=_params(*(("parallel", "parallel", "arbitrary") if rider is None else ("arbitrary",) * 3)),
    )(*args, *extra)
    return results[0] if len(results) == 1 else results


def _row_tile(T, C):
    tr = max(SUBLANES, min(T, ROW_BLOCK_ELEMS // C) // SUBLANES * SUBLANES)
    while T % tr:
        tr -= SUBLANES
    return tr


def _rows_call(body, name, T, C, row_ins, full_ins, row_outs, acc_outs=(), rider=None):
    tr = _row_tile(T, C)
    row_spec = pl.BlockSpec((tr, C), lambda i: (i, 0))
    in_specs = [row_spec] * len(row_ins) + [pl.BlockSpec(f.shape, lambda i, n=f.ndim: (0,) * n) for f in full_ins]
    out_specs = [row_spec] * len(row_outs) + [pl.BlockSpec(s, lambda i, n=len(s): (0,) * n) for s in acc_outs]
    out_shape = [jax.ShapeDtypeStruct((T, C), d) for d in row_outs] + [jax.ShapeDtypeStruct(s, F32) for s in acc_outs]
    grid = (T // tr,)
    body, in_specs, out_specs, out_shape, scratch, extra = _ride(rider, body, grid, in_specs, out_specs, out_shape, [])
    return pl.pallas_call(
        body, name=name, grid=grid, in_specs=in_specs, out_specs=out_specs, out_shape=out_shape, scratch_shapes=scratch,
        compiler_params=_params("arbitrary" if acc_outs or rider is not None else "parallel"),
    )(*row_ins, *full_ins, *extra)


def _rmsnorm_fwd(x, g, name):
    T, C = x.shape

    def body(x_ref, g_ref, h_ref):
        xv = x_ref[...]
        r = lax.rsqrt(jnp.mean(xv * xv, axis=-1, keepdims=True) + RMS_EPS)
        h_ref[...] = ((xv * r) * g_ref[...]).astype(BF16)

    return _rows_call(body, name, T, C, [x], [g.reshape(1, C)], [BF16])[0]


def _rmsnorm_bwd(x, g, dh, dres, name, rider=None):
    T, C = x.shape

    def body(x_ref, dh_ref, dres_ref, g_ref, dx_ref, dxb_ref, dg_ref):
        @pl.when(pl.program_id(0) == 0)
        def _():
            dg_ref[...] = jnp.zeros_like(dg_ref)

        xv = x_ref[...]
        dhv = dh_ref[...]
        r = lax.rsqrt(jnp.mean(xv * xv, axis=-1, keepdims=True) + RMS_EPS)
        xn = xv * r
        dg_ref[...] += jnp.sum(dhv * xn, axis=0, keepdims=True)
        dxn = dhv * g_ref[...]
        dx = dres_ref[...] + r * (dxn - xn * jnp.mean(dxn * xn, axis=-1, keepdims=True))
        dx_ref[...] = dx
        dxb_ref[...] = dx.astype(BF16)

    return _rows_call(body, name, T, C, [x, dh, dres], [g.reshape(1, C)], [F32, BF16], [(1, C)], rider=rider)


def _heads(C):
    return [slice(h * HEAD_DIM, (h + 1) * HEAD_DIM) for h in range(C // HEAD_DIM)]


def _qknorm_fwd(q, g, name):
    T, C = q.shape

    def body(q_ref, g_ref, o_ref):
        for head in _heads(C):
            xv = q_ref[:, head]
            r = lax.rsqrt(jnp.mean(xv * xv, axis=-1, keepdims=True) + RMS_EPS)
            o_ref[:, head] = ((xv * r) * g_ref[...]).astype(BF16)

    return _rows_call(body, name, T, C, [q], [g.reshape(1, HEAD_DIM)], [BF16])[0]


def _qknorm_bwd(q, g, dqn, name):
    T, C = q.shape

    def body(q_ref, d_ref, g_ref, dq_ref, dg_ref):
        @pl.when(pl.program_id(0) == 0)
        def _():
            dg_ref[...] = jnp.zeros_like(dg_ref)

        dg = jnp.zeros((1, HEAD_DIM), F32)
        for head in _heads(C):
            xv = q_ref[:, head]
            dv = d_ref[:, head]
            r = lax.rsqrt(jnp.mean(xv * xv, axis=-1, keepdims=True) + RMS_EPS)
            xn = xv * r
            dg = dg + jnp.sum(dv * xn, axis=0, keepdims=True)
            dxn = dv * g_ref[...]
            dq_ref[:, head] = (r * (dxn - xn * jnp.mean(dxn * xn, axis=-1, keepdims=True))).astype(BF16)
        dg_ref[...] += dg

    return _rows_call(body, name, T, C, [q, dqn], [g.reshape(1, HEAD_DIM)], [BF16], [(1, HEAD_DIM)])


ATT_TQ = 512
ATT_TK = 256
ATT_HEADS = 2


def _logsig_pair(z):
    a = jnp.minimum(z, 0.0) - jnp.log(1.0 + jnp.exp(-jnp.abs(z)))
    return a, a - z


def _tri(n, strict_upper_src):
    j = lax.broadcasted_iota(jnp.int32, (n, n), 0)
    s = lax.broadcasted_iota(jnp.int32, (n, n), 1)
    if strict_upper_src == "gt":
        m = j > s
    elif strict_upper_src == "le":
        m = j <= s
    else:
        m = j < s
    return jnp.where(m, 1.0, 0.0).astype(BF16)


def _cumdot(x, tri):
    hi, lo = _split(x)
    return _dot(hi, tri) + _dot(lo, tri)


def _attn_tiles(S):
    tq, tk = min(ATT_TQ, S), min(ATT_TK, S)
    return tq, tk, S // tq, tq // tk


def _attn_fwd(qn, kn, v, gate, Bl, S, rider=None):
    T, C = qn.shape
    H = C // HEAD_DIM
    tq, tk, nq, kpq = _attn_tiles(S)
    hp = min(ATT_HEADS, H)
    scale = 1.0 / math.sqrt(HEAD_DIM)

    def body(q_ref, k_ref, v_ref, g_ref, o_ref, bt_ref, og_ref):
        i = pl.program_id(2)
        tri = _tri(tk, "gt")
        rowpos = lax.broadcasted_iota(jnp.int32, (tq, tk), 0) + i * tq
        colpos = lax.broadcasted_iota(jnp.int32, (tq, tk), 1)
        o_ref[...] = jnp.zeros_like(o_ref)
        bt_ref[...] = jnp.zeros_like(bt_ref)

        heads = _heads(hp * HEAD_DIM)

        def diagonal():
            work = []
            for p in reversed(range(kpq)):
                j = i * kpq + p
                rows = pl.ds(pl.multiple_of(j * tk, tk), tk)
                live = slice(p * tk, tq)
                mask = (colpos[live] + j * tk) < rowpos[live]
                for head in heads:
                    a, b = _logsig_pair(_dot(q_ref[live, head], k_ref[rows, head], NT) * scale)
                    b = jnp.where(mask, b, 0.0)
                    work.append((head, rows, live, a, b, _cumdot(b, tri), mask))
            for n, head in enumerate(heads):
                for rows, live, a, b, suffix, mask in [w[1:] for w in work if w[0] == head]:
                    w = jnp.where(mask, jnp.exp(a + suffix + bt_ref[n, live]), 0.0)
                    o_ref[live, head] += _dot(w.astype(BF16), v_ref[rows, head])
                    bt_ref[n, live] += jnp.sum(b, axis=-1, keepdims=True)

        def group(g):
            work = []
            for p in reversed(range(kpq)):
                rows = pl.ds(pl.multiple_of((g * kpq + p) * tk, tk), tk)
                for head in heads:
                    a, b = _logsig_pair(_dot(q_ref[:, head], k_ref[rows, head], NT) * scale)
                    work.append((head, rows, a, b, _cumdot(b, tri)))
            for n, head in enumerate(heads):
                total = bt_ref[n]
                out = None
                for rows, a, b, suffix in [w[1:] for w in work if w[0] == head]:
                    term = _dot(jnp.exp(a + suffix + total).astype(BF16), v_ref[rows, head])
                    out = term if out is None else out + term
                    total = total + jnp.sum(b, axis=-1, keepdims=True)
                o_ref[:, head] += out
                bt_ref[n] = total

        diagonal()

        def step(n, carry):
            group(i - 1 - n)
            return carry

        lax.fori_loop(0, i, step, 0)
        og_ref[...] = (o_ref[...] * _silu_parts(g_ref[...])[0]).astype(BF16)

    qspec = pl.BlockSpec((tq, hp * HEAD_DIM), lambda b, h, i: (b * nq + i, h))
    kspec = pl.BlockSpec((S, hp * HEAD_DIM), lambda b, h, i: (b, h))
    btspec = pl.BlockSpec((None, hp, tq, 1), lambda b, h, i: (b, h, i, 0))
    grid = (Bl, H // hp, nq)
    body, in_specs, out_specs, out_shape, scratch, extra = _ride(
        rider, body, grid, [qspec, kspec, kspec, qspec], [qspec, btspec, qspec],
        [jax.ShapeDtypeStruct((T, C), F32), jax.ShapeDtypeStruct((Bl, H, S, 1), F32),
         jax.ShapeDtypeStruct((T, C), BF16)], [])
    return pl.pallas_call(
        body, name="attn_fwd", grid=grid, in_specs=in_specs, out_specs=out_specs, out_shape=out_shape,
        scratch_shapes=scratch, compiler_params=_params("arbitrary", "arbitrary", "arbitrary"),
    )(qn, kn, v, gate, *extra)


def _attn_bwd(qn, kn, v, do, btot, Bl, S, rider=None):
    T, C = qn.shape
    H = C // HEAD_DIM
    tq, tk, nq, kpq = _attn_tiles(S)
    hp = min(ATT_HEADS, H)
    scale = 1.0 / math.sqrt(HEAD_DIM)

    def body(q_ref, k_ref, v_ref, do_ref, bt_ref, dq_ref, dk_ref, dvb_ref, pb_ref, pdl_ref, dv_ref):
        i = pl.program_id(2)

        @pl.when(i == 0)
        def _():
            dk_ref[...] = jnp.zeros_like(dk_ref)
            dv_ref[...] = jnp.zeros_like(dv_ref)

        tri_le = _tri(tk, "le")
        tri_lt = _tri(tk, "lt")
        rowpos = lax.broadcasted_iota(jnp.int32, (tq, tk), 0) + i * tq
        colpos = lax.broadcasted_iota(jnp.int32, (tq, tk), 1)
        dq_ref[...] = jnp.zeros_like(dq_ref)
        pb_ref[...] = bt_ref[...]
        pdl_ref[...] = jnp.zeros_like(pdl_ref)

        heads = _heads(hp * HEAD_DIM)

        def group(g):
            work = []
            for p in range(kpq):
                rows = pl.ds(pl.multiple_of((g * kpq + p) * tk, tk), tk)
                for head in heads:
                    a, b = _logsig_pair(_dot(q_ref[:, head], k_ref[rows, head], NT) * scale)
                    work.append((head, rows, a, b, _cumdot(b, tri_le), _dot(do_ref[:, head], v_ref[rows, head], NT)))
            for n, head in enumerate(heads):
                remaining = pb_ref[n]
                swept = pdl_ref[n]
                dq = None
                for rows, a, b, cum, dw in [w[1:] for w in work if w[0] == head]:
                    w = jnp.exp(a + (remaining - cum))
                    dl = dw * w
                    prefix = swept + _cumdot(dl, tri_lt)
                    beta = jnp.exp(a)
                    dzb = ((dl * (1.0 - beta) - beta * prefix) * scale).astype(BF16)
                    term = _dot(dzb, k_ref[rows, head])
                    dq = term if dq is None else dq + term
                    dk_ref[rows, head] += _dot(dzb, q_ref[:, head], TN)
                    dv_ref[rows, head] += _dot(w.astype(BF16), do_ref[:, head], TN)
                    remaining = remaining - jnp.sum(b, axis=-1, keepdims=True)
                    swept = swept + jnp.sum(dl, axis=-1, keepdims=True)
                dq_ref[:, head] += dq
                pb_ref[n] = remaining
                pdl_ref[n] = swept

        def diagonal():
            work = []
            for p in range(kpq):
                j = i * kpq + p
                rows = pl.ds(pl.multiple_of(j * tk, tk), tk)
                live = slice(p * tk, tq)
                mask = (colpos[live] + j * tk) < rowpos[live]
                for head in heads:
                    a, b = _logsig_pair(_dot(q_ref[live, head], k_ref[rows, head], NT) * scale)
                    b = jnp.where(mask, b, 0.0)
                    work.append((head, rows, live, a, b, _cumdot(b, tri_le),
                                 _dot(do_ref[live, head], v_ref[rows, head], NT), mask))
            for n, head in enumerate(heads):
                for rows, live, a, b, cum, dw, mask in [w[1:] for w in work if w[0] == head]:
                    w = jnp.where(mask, jnp.exp(a + (pb_ref[n, live] - cum)), 0.0)
                    dl = dw * w
                    prefix = pdl_ref[n, live] + _cumdot(dl, tri_lt)
                    beta = jnp.exp(a)
                    dzb = (jnp.where(mask, dl * (1.0 - beta) - beta * prefix, 0.0) * scale).astype(BF16)
                    dq_ref[live, head] += _dot(dzb, k_ref[rows, head])
                    dk_ref[rows, head] += _dot(dzb, q_ref[live, head], TN)
                    dv_ref[rows, head] += _dot(w.astype(BF16), do_ref[live, head], TN)
                    pb_ref[n, live] -= jnp.sum(b, axis=-1, keepdims=True)
                    pdl_ref[n, live] += jnp.sum(dl, axis=-1, keepdims=True)

        def step(g, carry):
            group(g)
            return carry

        lax.fori_loop(0, i, step, 0)
        diagonal()

        @pl.when(i == nq - 1)
        def _():
            dvb_ref[...] = dv_ref[...].astype(BF16)

    qspec = pl.BlockSpec((tq, hp * HEAD_DIM), lambda b, h, i: (b * nq + i, h))
    kspec = pl.BlockSpec((S, hp * HEAD_DIM), lambda b, h, i: (b, h))
    btspec = pl.BlockSpec((None, hp, tq, 1), lambda b, h, i: (b, h, i, 0))
    grid = (Bl, H // hp, nq)
    body, in_specs, out_specs, out_shape, scratch, extra = _ride(
        rider, body, grid, [qspec, kspec, kspec, qspec, btspec], [qspec, kspec, kspec],
        [jax.ShapeDtypeStruct((T, C), F32), jax.ShapeDtypeStruct((T, C), F32), jax.ShapeDtypeStruct((T, C), BF16)],
        [pltpu.VMEM((hp, tq, 1), F32), pltpu.VMEM((hp, tq, 1), F32), pltpu.VMEM((S, hp * HEAD_DIM), F32)])
    return pl.pallas_call(
        body, name="attn_bwd", grid=grid, in_specs=in_specs, out_specs=out_specs, out_shape=out_shape,
        scratch_shapes=scratch, compiler_params=_params("arbitrary", "arbitrary", "arbitrary"),
    )(qn, kn, v, do, btot, *extra)


SSM_TIME_BLOCK = 1024
CHUNK = SUBLANES


def _cmadd(xr, xi, ar, ai, sr, si):
    return xr + ar * sr - ai * si, xi + ar * si + ai * sr


def _chunk_scan(xr, xi, tab_ref, cr, ci, reverse):
    for lvl, d in enumerate((1, 2, 4)):
        shift = (CHUNK - d) if reverse else d
        sr = pltpu.roll(xr, shift, 0)
        si = pltpu.roll(xi, shift, 0)
        ar = tab_ref[pl.ds((2 * lvl) * CHUNK, CHUNK), :]
        ai = tab_ref[pl.ds((2 * lvl + 1) * CHUNK, CHUNK), :]
        xr, xi = _cmadd(xr, xi, ar, ai, sr, si)
    pr = tab_ref[pl.ds(6 * CHUNK, CHUNK), :]
    pi = tab_ref[pl.ds(7 * CHUNK, CHUNK), :]
    return _cmadd(xr, xi, pr, pi, cr, ci)


def _ssm_dims(S, C):
    G = C // GROUP
    GT = min(16, G)
    return G, GT, G // GT, GT * GROUP, GT * STATE, min(SSM_TIME_BLOCK, S)


def _ssm_fwd(u, mats, Bl, S):
    T, C = u.shape
    G, GT, ngt, cw, sw, TB = _ssm_dims(S, C)
    ntb = S // TB
    nch = TB // CHUNK

    def body(u_ref, bre_ref, bim_ref, cre_ref, cim_ref, d_ref, tab_ref, y_ref, hr_ref, hi_ref, yg_ref, car_r, car_i):
        @pl.when(pl.program_id(2) == 0)
        def _():
            car_r[...] = jnp.zeros_like(car_r)
            car_i[...] = jnp.zeros_like(car_i)

        uv = u_ref[...]
        hr_ref[...] = _dot_f32(uv, bre_ref[...])
        hi_ref[...] = _dot_f32(uv, bim_ref[...])

        def step(n, carry):
            cr, ci = carry
            rows = pl.ds(pl.multiple_of(n * CHUNK, CHUNK), CHUNK)
            xr, xi = _chunk_scan(hr_ref[rows, :], hi_ref[rows, :], tab_ref, cr, ci, False)
            hr_ref[rows, :] = xr
            hi_ref[rows, :] = xi
            last = (CHUNK - 1, CHUNK)
            return (jnp.broadcast_to(xr[last[0]:last[1], :], xr.shape), jnp.broadcast_to(xi[last[0]:last[1], :], xi.shape))

        cr, ci = lax.fori_loop(0, nch, step, (car_r[...], car_i[...]))
        car_r[...] = cr
        car_i[...] = ci
        y = _dot_f32(hr_ref[...], cre_ref[...]) - _dot_f32(hi_ref[...], cim_ref[...]) + d_ref[...] * uv
        y_ref[...] = y
        yg_ref[...] = _gelu_parts(y)[0].astype(BF16)

    uspec = pl.BlockSpec((TB, cw), lambda g, b, t: (b * ntb + t, g))
    hspec = pl.BlockSpec((TB, sw), lambda g, b, t: (b * ntb + t, g))

    def gspec(r, c):
        return pl.BlockSpec((None, r, c), lambda g, b, t: (g, 0, 0))

    return pl.pallas_call(
        body, name="ssm_fwd", grid=(ngt, Bl, ntb),
        in_specs=[uspec, gspec(cw, sw), gspec(cw, sw), gspec(sw, cw), gspec(sw, cw), gspec(1, cw), gspec(8 * CHUNK, sw)],
        out_specs=[uspec, hspec, hspec, uspec],
        out_shape=[jax.ShapeDtypeStruct((T, C), F32), jax.ShapeDtypeStruct((T, G * STATE), F32),
                   jax.ShapeDtypeStruct((T, G * STATE), F32), jax.ShapeDtypeStruct((T, C), BF16)],
        scratch_shapes=[pltpu.VMEM((CHUNK, sw), F32), pltpu.VMEM((CHUNK, sw), F32)],
        compiler_params=_params("parallel", "arbitrary", "arbitrary"),
    )(u, mats["bbd_re"], mats["bbd_im"], mats["cbd_re"], mats["cbd_im"], mats["d"], mats["tab_fwd"])


def _ssm_bwd(u, dy, h_re, h_im, mats, Bl, S):
    T, C = u.shape
    G, GT, ngt, cw, sw, TB = _ssm_dims(S, C)
    ntb = S // TB
    nch = TB // CHUNK
    rpb = TB // CHUNK

    def body(u_ref, dy_ref, hr_ref, hi_ref, hpr_ref, hpi_ref, cre_ref, cim_ref, bre_ref, bim_ref, d_ref, tab_ref,
             du_ref, dbr_ref, dbi_ref, dcr_ref, dci_ref, dlr_ref, dli_ref, dd_ref, gr_ref, gi_ref, car_r, car_i):
        b = pl.program_id(1)
        t = pl.program_id(2)

        @pl.when((b == 0) & (t == 0))
        def _():
            for ref in (dbr_ref, dbi_ref, dcr_ref, dci_ref, dlr_ref, dli_ref, dd_ref):
                ref[...] = jnp.zeros_like(ref)

        @pl.when(t == 0)
        def _():
            car_r[...] = jnp.zeros_like(car_r)
            car_i[...] = jnp.zeros_like(car_i)

        uv = u_ref[...]
        dyv = dy_ref[...]
        gr_ref[...] = _dot_f32(dyv, cre_ref[...], NT)
        gi_ref[...] = -_dot_f32(dyv, cim_ref[...], NT)
        alive = jnp.where(t == ntb - 1, 0.0, 1.0)
        row0 = lax.broadcasted_iota(jnp.int32, (CHUNK, sw), 0) == 0

        def step(m, carry):
            cr, ci, ar, ai = carry
            n = nch - 1 - m
            rows = pl.ds(pl.multiple_of(n * CHUNK, CHUNK), CHUNK)
            prow = pl.ds(pl.multiple_of(jnp.maximum(n - 1, 0) * CHUNK, CHUNK), CHUNK)
            xr, xi = _chunk_scan(gr_ref[rows, :], gi_ref[rows, :], tab_ref, cr, ci, True)
            gr_ref[rows, :] = xr
            gi_ref[rows, :] = xi
            first = n == 0
            pr = jnp.where(first, hpr_ref[...] * alive, hr_ref[prow, :])
            pi = jnp.where(first, hpi_ref[...] * alive, hi_ref[prow, :])
            sr = jnp.where(row0, pltpu.roll(pr, 1, 0), pltpu.roll(hr_ref[rows, :], 1, 0))
            si = jnp.where(row0, pltpu.roll(pi, 1, 0), pltpu.roll(hi_ref[rows, :], 1, 0))
            ar = ar + xr * sr + xi * si
            ai = ai + xi * sr - xr * si
            return (jnp.broadcast_to(xr[0:1, :], xr.shape), jnp.broadcast_to(xi[0:1, :], xi.shape), ar, ai)

        zero = jnp.zeros((CHUNK, sw), F32)
        cr, ci, ar, ai = lax.fori_loop(0, nch, step, (car_r[...], car_i[...], zero, zero))
        car_r[...] = cr
        car_i[...] = ci
        dlr_ref[...] += ar
        dli_ref[...] += ai
        gr = gr_ref[...]
        gi = gi_ref[...]
        dbr_ref[...] += _dot_f32(uv, gr, TN)
        dbi_ref[...] += _dot_f32(uv, gi, TN)
        dcr_ref[...] += _dot_f32(hr_ref[...], dyv, TN)
        dci_ref[...] -= _dot_f32(hi_ref[...], dyv, TN)
        dd_ref[...] += jnp.sum(dyv * uv, axis=0, keepdims=True)
        du_ref[...] = (_dot_f32(gr, bre_ref[...], NT) + _dot_f32(gi, bim_ref[...], NT) + d_ref[...] * dyv).astype(BF16)

    def tblk(b, t):
        return b * ntb + (ntb - 1 - t)

    uspec = pl.BlockSpec((TB, cw), lambda g, b, t: (tblk(b, t), g))
    hspec = pl.BlockSpec((TB, sw), lambda g, b, t: (tblk(b, t), g))
    hpspec = pl.BlockSpec((CHUNK, sw), lambda g, b, t: (jnp.maximum(tblk(b, t) * rpb - 1, 0), g))

    def gspec(r, c):
        return pl.BlockSpec((None, r, c), lambda g, b, t: (g, 0, 0))

    def gshape(r, c):
        return jax.ShapeDtypeStruct((ngt, r, c), F32)

    return pl.pallas_call(
        body, name="ssm_bwd", grid=(ngt, Bl, ntb),
        in_specs=[uspec, uspec, hspec, hspec, hpspec, hpspec, gspec(sw, cw), gspec(sw, cw), gspec(cw, sw), gspec(cw, sw),
                  gspec(1, cw), gspec(8 * CHUNK, sw)],
        out_specs=[uspec, gspec(cw, sw), gspec(cw, sw), gspec(sw, cw), gspec(sw, cw), gspec(CHUNK, sw), gspec(CHUNK, sw),
                   gspec(1, cw)],
        out_shape=[jax.ShapeDtypeStruct((T, C), BF16), gshape(cw, sw), gshape(cw, sw), gshape(sw, cw), gshape(sw, cw),
                   gshape(CHUNK, sw), gshape(CHUNK, sw), gshape(1, cw)],
        scratch_shapes=[pltpu.VMEM((TB, sw), F32), pltpu.VMEM((TB, sw), F32), pltpu.VMEM((CHUNK, sw), F32),
                        pltpu.VMEM((CHUNK, sw), F32)],
        compiler_params=_params("arbitrary", "arbitrary", "arbitrary"),
    )(u, dy, h_re, h_im, h_re, h_im, mats["cbd_re"], mats["cbd_im"], mats["bbd_re"], mats["bbd_im"], mats["d"],
      mats["tab_rev"])


def _zoh(a_re, a_im, log_dt, b_re, b_im):
    dt = jnp.exp(log_dt)[:, None]
    mag = jnp.exp(a_re * dt)
    l_re = mag * jnp.cos(a_im * dt)
    l_im = mag * jnp.sin(a_im * dt)
    den = a_re * a_re + a_im * a_im
    f_re = ((l_re - 1.0) * a_re + l_im * a_im) / den
    f_im = (l_im * a_re - (l_re - 1.0) * a_im) / den
    bb_re = f_re[..., None] * b_re - f_im[..., None] * b_im
    bb_im = f_re[..., None] * b_im + f_im[..., None] * b_re
    return l_re, l_im, bb_re, bb_im


def _ssm_matrices(a_re, a_im, log_dt, b_re, b_im, c_re, c_im, d, S):
    G = a_re.shape[0]
    _, GT, ngt, cw, sw, _ = _ssm_dims(S, G * GROUP)
    _, _, bb_re, bb_im = _zoh(a_re, a_im, log_dt, b_re, b_im)
    eye = jnp.eye(GT, dtype=BF16)

    def bd_b(bb):
        return jnp.einsum("tgpi,gh->tgihp", bb.astype(BF16).reshape(ngt, GT, STATE, GROUP), eye).reshape(ngt, cw, sw)

    def bd_c(c):
        return jnp.einsum("tgip,gh->tgphi", c.astype(BF16).reshape(ngt, GT, GROUP, STATE), eye).reshape(ngt, sw, cw)

    dt = jnp.exp(log_dt)[:, None]

    def power(k, conj):
        mag = jnp.exp(k * a_re * dt)
        ang = k * a_im * dt
        return (mag * jnp.cos(ang)).reshape(ngt, 1, sw), ((-1.0 if conj else 1.0) * mag * jnp.sin(ang)).reshape(ngt, 1, sw)

    r = jnp.arange(CHUNK)[None, :, None]

    def table(reverse):
        parts = []
        for dd in (1, 2, 4):
            pr, pi = power(float(dd), reverse)
            keep = (r <= CHUNK - 1 - dd) if reverse else (r >= dd)
            parts += [jnp.where(keep, pr, 0.0), jnp.where(keep, pi, 0.0)]
        exps = [(CHUNK - k) if reverse else (k + 1) for k in range(CHUNK)]
        pw = [power(float(e), reverse) for e in exps]
        parts += [jnp.concatenate([p[0] for p in pw], axis=1), jnp.concatenate([p[1] for p in pw], axis=1)]
        return jnp.concatenate([jnp.broadcast_to(p, (ngt, CHUNK, sw)) for p in parts], axis=1)

    return dict(bbd_re=bd_b(bb_re), bbd_im=bd_b(bb_im), cbd_re=bd_c(c_re), cbd_im=bd_c(c_im), d=d.reshape(ngt, 1, cw),
                tab_fwd=table(False), tab_rev=table(True))


def _ssm_unblock(dbr, dbi, dcr, dci, dlr, dli, dd, G):
    ngt = dbr.shape[0]
    GT = G // ngt
    eye = jnp.eye(GT, dtype=F32)

    def ub(x):
        return jnp.einsum("tgihp,gh->tgpi", x.reshape(ngt, GT, GROUP, GT, STATE), eye).reshape(G, STATE, GROUP)

    def uc(x):
        return jnp.einsum("tgphi,gh->tgip", x.reshape(ngt, GT, STATE, GT, GROUP), eye).reshape(G, GROUP, STATE)

    return (dlr.sum(axis=1).reshape(G, STATE), dli.sum(axis=1).reshape(G, STATE), ub(dbr), ub(dbi), uc(dcr), uc(dci),
            dd.reshape(G * GROUP))


def _attn_gate_bwd(o, gate):
    def fn(dog, o_blk, g_blk):
        val, der = _silu_parts(g_blk)
        return dog * val, dog * o_blk * der

    return _Epilogue(fn, [o, gate], [], [BF16, BF16])


def _glu_fwd(y, gate, glu_b):
    def fn(gl, y_blk, g_blk, b_blk):
        return gl, _gelu_parts(y_blk)[0] * _sigmoid(gl + b_blk) * _silu_parts(g_blk)[0]

    return _Epilogue(fn, [y, gate], [glu_b.reshape(1, -1)], [F32, BF16])


def _glu_bwd(dy3, y, gl, gate, glu_b):
    T, C = y.shape

    def body(d_ref, y_ref, gl_ref, g_ref, b_ref, dgl_ref, dgate_ref, t1_ref, db_ref):
        @pl.when(pl.program_id(0) == 0)
        def _():
            db_ref[...] = jnp.zeros_like(db_ref)

        yg = _gelu_parts(y_ref[...])[0]
        sg = _sigmoid(gl_ref[...] + b_ref[...])
        sl, sld = _silu_parts(g_ref[...])
        dv = d_ref[...]
        dy2 = dv * sl
        dgl = dy2 * yg * sg * (1.0 - sg)
        dgl_ref[...] = dgl.astype(BF16)
        dgate_ref[...] = (dv * (yg * sg) * sld).astype(BF16)
        t1_ref[...] = dy2 * sg
        db_ref[...] += jnp.sum(dgl, axis=0, keepdims=True)

    return _rows_call(body, "glu_bwd", T, C, [dy3, y, gl, gate], [glu_b.reshape(1, C)], [BF16, BF16, F32], [(1, C)])


def _gelu_bwd(t1, y):
    return _Epilogue(lambda t2, t1_blk, y_blk: ((t1_blk + t2) * _gelu_parts(y_blk)[1],), [t1, y], [], [F32])


def _loss_head(x2, target):
    T, C = x2.shape

    def body(x_ref, t_ref, d_ref, db_ref, l_ref):
        @pl.when(pl.program_id(0) == 0)
        def _():
            l_ref[...] = jnp.zeros_like(l_ref)

        e = x_ref[...] - t_ref[...]
        d = e * (1.0 / C)
        d_ref[...] = d
        db_ref[...] = d.astype(BF16)
        l_ref[...] += 0.5 * jnp.sum(jnp.sum(e * e, axis=-1, keepdims=True) * (1.0 / C), axis=0, keepdims=True)

    return _rows_call(body, "loss_head", T, C, [x2, target], [], [F32, BF16], [(1, 1)])


def _adamw(w, g, m, v, name):
    shape = w.shape
    C = shape[-1]
    R = w.size // C
    bc1 = 1.0 - ADAM_B1 ** ADAM_STEP
    bc2 = 1.0 - ADAM_B2 ** ADAM_STEP

    def body(w_ref, g_ref, m_ref, v_ref, d_ref, nm_ref, nv_ref):
        gv = g_ref[...]
        mn = ADAM_B1 * m_ref[...] + (1.0 - ADAM_B1) * gv
        vn = ADAM_B2 * v_ref[...] + (1.0 - ADAM_B2) * (gv * gv)
        d_ref[...] = -ADAM_LR * ((mn / bc1) / (jnp.sqrt(vn / bc2) + ADAM_EPS) + ADAM_WD * w_ref[...])
        nm_ref[...] = mn
        nv_ref[...] = vn

    outs = _rows_call(body, name, R, C, [a.reshape(R, C) for a in (w, g, m, v)], [], [F32, F32, F32])
    return [o.reshape(shape) for o in outs]


def _sum_leading(x, name, half=None):
    n, R, C = x.shape
    tr = _row_tile(R, C * n)
    nb = R // tr

    def body(*refs):
        x_ref, o_ref = refs[-2:]
        acc = x_ref[0].astype(F32)
        for k in range(1, n):
            acc = acc + x_ref[k].astype(F32)
        o_ref[...] = acc

    if half is None:
        return pl.pallas_call(
            body, name=name, grid=(nb,), in_specs=[pl.BlockSpec((n, tr, C), lambda i: (0, i, 0))],
            out_specs=pl.BlockSpec((tr, C), lambda i: (i, 0)), out_shape=jax.ShapeDtypeStruct((R, C), F32),
            compiler_params=_params("parallel"),
        )(x)
    grid_spec = pltpu.PrefetchScalarGridSpec(
        num_scalar_prefetch=1, grid=(nb,), in_specs=[pl.BlockSpec((n, tr, C), lambda i, c_ref: (0, i, 0))],
        out_specs=pl.BlockSpec((tr, C), lambda i, c_ref: (c_ref[0] * nb + i, 0)))
    return pl.pallas_call(
        body, name=name, grid_spec=grid_spec, out_shape=jax.ShapeDtypeStruct((2 * R, C), F32),
        compiler_params=_params("parallel"),
    )(half.reshape(1).astype(jnp.int32), x)


def _add_halves(g, c, name):
    full, recv = g
    n, R, C = full.shape
    half = R // 2
    tr = _row_tile(half, C)
    nb = half // tr

    def body(c_ref, a_ref, b_ref, o_ref):
        o_ref[...] = (a_ref[...] + b_ref[...]).astype(BF16)

    grid_spec = pltpu.PrefetchScalarGridSpec(
        num_scalar_prefetch=1, grid=(n, nb),
        in_specs=[pl.BlockSpec((None, tr, C), lambda j, i, c_ref: (j, c_ref[0] * nb + i, 0)),
                  pl.BlockSpec((None, tr, C), lambda j, i, c_ref: (j, i, 0))],
        out_specs=pl.BlockSpec((None, tr, C), lambda j, i, c_ref: (j, i, 0)))
    return pl.pallas_call(
        body, name=name, grid_spec=grid_spec, out_shape=jax.ShapeDtypeStruct((n, half, C), BF16),
        compiler_params=_params("parallel", "parallel"),
    )(c.reshape(1).astype(jnp.int32), full, recv)


ANY = pl.BlockSpec(memory_space=pl.ANY)


def _position():
    return lax.axis_index("x"), lax.axis_index("y"), lax.axis_index("c")


def _all_gather8(blk, name):
    return _standalone(_all_gather_rider(blk), name)[0]


class _Rider:
    def __init__(self, arrays, out_shapes, sems, start, finish):
        self.arrays, self.out_shapes, self.sems, self.start, self.finish = arrays, out_shapes, sems, start, finish


def _join_riders(*riders):
    def split(refs):
        ins, outs, sems = [], [], []
        for r in riders:
            ins.append(refs[:len(r.arrays)])
            refs = refs[len(r.arrays):]
        for r in riders:
            outs.append(refs[:len(r.out_shapes)])
            refs = refs[len(r.out_shapes):]
        for r in riders:
            sems.append(refs[:len(r.sems)])
            refs = refs[len(r.sems):]
        return [i + o + s for i, o, s in zip(ins, outs, sems)]

    def start(*refs):
        for r, own in zip(riders, split(refs)):
            r.start(*own)

    def finish(*refs):
        for r, own in zip(riders, split(refs)):
            r.finish(*own)

    return _Rider([a for r in riders for a in r.arrays], [s for r in riders for s in r.out_shapes],
                  [s for r in riders for s in r.sems], start, finish)


def _standalone(rider, name):
    def body(*refs):
        rider.start(*refs)
        rider.finish(*refs)

    return pl.pallas_call(
        body, name=name, in_specs=[ANY] * len(rider.arrays), out_specs=[ANY] * len(rider.out_shapes),
        out_shape=rider.out_shapes, scratch_shapes=rider.sems,
    )(*rider.arrays)


def _ride(rider, body, grid, in_specs, out_specs, out_shape, scratch):
    if rider is None:
        return body, in_specs, out_specs, out_shape, scratch, []
    ni, no, ns = len(in_specs), len(out_specs), len(scratch)
    ri, ro = len(rider.arrays), len(rider.out_shapes)

    def full(*refs):
        ins, refs = refs[:ni], refs[ni:]
        r_ins, refs = refs[:ri], refs[ri:]
        outs, refs = refs[:no], refs[no:]
        r_outs, refs = refs[:ro], refs[ro:]
        scr, r_sems = refs[:ns], refs[ns:]
        ids = [pl.program_id(a) for a in range(len(grid))]
        first = functools.reduce(jnp.logical_and, [i == 0 for i in ids])
        last = functools.reduce(jnp.logical_and, [i == g - 1 for i, g in zip(ids, grid)])

        @pl.when(first)
        def _():
            rider.start(*r_ins, *r_outs, *r_sems)

        body(*ins, *outs, *scr)

        @pl.when(last)
        def _():
            rider.finish(*r_ins, *r_outs, *r_sems)

    return (full, in_specs + [ANY] * ri, out_specs + [ANY] * ro, out_shape + rider.out_shapes, scratch + rider.sems,
            rider.arrays)


def _all_gather_rider(blk):
    M, N = blk.shape

    def copies(x_ref, out_ref, send_sems, recv_sems, local_sem):
        x, y, c = _position()
        me, sibling = (x, y, c), (x, y, 1 - c)
        chips = [(1 - x, y), (x, 1 - y), (1 - x, 1 - y)]

        def slab(px, py, pc):
            return out_ref.at[4 * px + 2 * py + pc]

        def copy(k, block, to, src=None):
            return pltpu.make_async_remote_copy(
                src_ref=slab(*block) if src is None else src, dst_ref=slab(*block),
                send_sem=send_sems.at[k], recv_sem=recv_sems.at[k], device_id=to, device_id_type=MESH)

        mine = pltpu.make_async_copy(x_ref, slab(*me), local_sem)
        first = [copy(0, me, sibling, src=x_ref)]
        first += [copy(1 + j, me, (*chip, c), src=x_ref) for j, chip in enumerate(chips)]
        passed = [copy(4 + j, (*chip, c), sibling) for j, chip in enumerate(chips)]
        arrivals = [copy(1 + j, (*chip, c), me) for j, chip in enumerate(chips)]
        from_sibling = [copy(0, sibling, me)] + [copy(4 + j, (*chip, 1 - c), me) for j, chip in enumerate(chips)]
        return mine, first, passed, arrivals, from_sibling

    def start(*refs):
        mine, first, _, _, _ = copies(*refs)
        mine.start()
        for cp in first:
            cp.start()

    def finish(*refs):
        mine, first, passed, arrivals, from_sibling = copies(*refs)
        for arrival, onward in zip(arrivals, passed):
            arrival.wait_recv()
            onward.start()
        for cp in from_sibling:
            cp.wait_recv()
        for cp in first + passed:
            cp.wait_send()
        mine.wait()

    return _Rider([blk], [jax.ShapeDtypeStruct((N_DEV, M, N), blk.dtype)],
                  [pltpu.SemaphoreType.DMA((7,)), pltpu.SemaphoreType.DMA((7,)), pltpu.SemaphoreType.DMA], start, finish)


def _all_gather_forwarding(blk, name):
    M, N = blk.shape
    half = M // 2

    def body(x_ref, out_ref, send_sems, recv_sems, local_sem):
        x, y, c = _position()
        me, sibling = (x, y, c), (x, y, 1 - c)
        xn, yn, dg = (1 - x, y), (x, 1 - y), (1 - x, 1 - y)

        def slab(chip, pc, part=None):
            ref = out_ref.at[4 * chip[0] + 2 * chip[1] + pc]
            return ref if part is None else ref.at[pl.ds(part * half, half), :]

        def copy(k, place, to, src=None):
            return pltpu.make_async_remote_copy(src_ref=place if src is None else src, dst_ref=place,
                                                send_sem=send_sems.at[k], recv_sem=recv_sems.at[k], device_id=to,
                                                device_id_type=MESH)

        mine = pltpu.make_async_copy(x_ref, slab((x, y), c), local_sem)
        own = [copy(0, slab((x, y), c), sibling, src=x_ref), copy(1, slab((x, y), c), (*xn, c), src=x_ref),
               copy(2, slab((x, y), c), (*yn, c), src=x_ref)]
        mine.start()
        for cp in own:
            cp.start()
        copy(1, slab(xn, c), me).wait_recv()
        onward = [copy(3, slab(xn, c, 0), (*yn, c)), copy(5, slab(xn, c), sibling)]
        for cp in onward:
            cp.start()
        copy(2, slab(yn, c), me).wait_recv()
        onward += [copy(4, slab(yn, c, 1), (*xn, c)), copy(6, slab(yn, c), sibling)]
        for cp in onward[2:]:
            cp.start()
        copy(3, slab(dg, c, 0), me).wait_recv()
        onward.append(copy(7, slab(dg, c, 0), sibling))
        onward[-1].start()
        copy(4, slab(dg, c, 1), me).wait_recv()
        onward.append(copy(8, slab(dg, c, 1), sibling))
        onward[-1].start()
        for k, place in ((0, slab((x, y), 1 - c)), (5, slab(xn, 1 - c)), (6, slab(yn, 1 - c)), (7, slab(dg, 1 - c, 0)),
                         (8, slab(dg, 1 - c, 1))):
            copy(k, place, me).wait_recv()
        for cp in own + onward:
            cp.wait_send()
        mine.wait()

    return pl.pallas_call(
        body, name=name, in_specs=[ANY], out_specs=ANY, out_shape=jax.ShapeDtypeStruct((N_DEV, M, N), blk.dtype),
        scratch_shapes=[pltpu.SemaphoreType.DMA((9,)), pltpu.SemaphoreType.DMA((9,)), pltpu.SemaphoreType.DMA],
    )(blk)


def _sibling_half_rider(g):
    n, R, C = g.shape
    half = R // 2

    def copy(g_ref, out_ref, send_sem, recv_sem):
        x, y, c = _position()
        return pltpu.make_async_remote_copy(
            src_ref=g_ref.at[:, pl.ds((1 - c) * half, half), :], dst_ref=out_ref, send_sem=send_sem, recv_sem=recv_sem,
            device_id=(x, y, 1 - c), device_id_type=MESH)

    return _Rider([g], [jax.ShapeDtypeStruct((n, half, C), g.dtype)], [pltpu.SemaphoreType.DMA, pltpu.SemaphoreType.DMA],
                  lambda *refs: copy(*refs).start(), lambda *refs: copy(*refs).wait())


def _sibling_send_half(g, name):
    return _standalone(_sibling_half_rider(g), name)[0]


def _chip_exchange_rider(p, row_off=0, rows=None):
    rows = p.shape[1] if rows is None else rows

    def copies(p_ref, out_ref, send_sems, recv_sems, local_sem):
        x, y, c = _position()
        my = 2 * x + y
        chips = [(1 - x, y), (x, 1 - y), (1 - x, 1 - y)]

        def src(slab):
            return p_ref.at[slab, pl.ds(row_off, rows), :]

        mine = pltpu.make_async_copy(src(my), out_ref.at[my], local_sem)
        sends = [pltpu.make_async_remote_copy(
            src_ref=src(2 * px + py), dst_ref=out_ref.at[my], send_sem=send_sems.at[k], recv_sem=recv_sems.at[k],
            device_id=(px, py, c), device_id_type=MESH) for k, (px, py) in enumerate(chips)]
        arrivals = [pltpu.make_async_remote_copy(
            src_ref=src(my), dst_ref=out_ref.at[2 * px + py], send_sem=send_sems.at[k], recv_sem=recv_sems.at[k],
            device_id=(px, py, c), device_id_type=MESH) for k, (px, py) in enumerate(chips)]
        return mine, sends, arrivals

    def start(*refs):
        mine, sends, _ = copies(*refs)
        mine.start()
        for cp in sends:
            cp.start()

    def finish(*refs):
        mine, sends, arrivals = copies(*refs)
        for cp in arrivals:
            cp.wait_recv()
        for cp in sends:
            cp.wait_send()
        mine.wait()

    return _Rider([p], [jax.ShapeDtypeStruct((p.shape[0], rows, p.shape[2]), p.dtype)],
                  [pltpu.SemaphoreType.DMA((3,)), pltpu.SemaphoreType.DMA((3,)), pltpu.SemaphoreType.DMA], start, finish)


def _sibling_fill(buf, name):
    M = buf.shape[0] // 2

    def body(buf_ref, out_ref, send_sem, recv_sem):
        x, y, c = _position()
        mine = pl.ds(c * M, M)
        theirs = pl.ds((1 - c) * M, M)
        pltpu.make_async_remote_copy(src_ref=buf_ref.at[mine], dst_ref=out_ref.at[mine], send_sem=send_sem,
                                     recv_sem=recv_sem, device_id=(x, y, 1 - c), device_id_type=MESH).start()
        pltpu.make_async_remote_copy(src_ref=buf_ref.at[mine], dst_ref=out_ref.at[theirs], send_sem=send_sem,
                                     recv_sem=recv_sem, device_id=(x, y, 1 - c), device_id_type=MESH).wait()

    return pl.pallas_call(
        body, name=name, in_specs=[ANY], out_specs=ANY, out_shape=jax.ShapeDtypeStruct(buf.shape, buf.dtype),
        scratch_shapes=[pltpu.SemaphoreType.DMA, pltpu.SemaphoreType.DMA], input_output_aliases={0: 0},
    )(buf)


def _pack_rest(attn_out, ssm_in, glu_w, ssm_out):
    hd = ssm_in.shape[0] // 2
    return jnp.concatenate([attn_out, jnp.concatenate([ssm_in[:hd], ssm_in[hd:]], axis=1), glu_w, ssm_out], axis=0)


def _unpack_rest(p):
    D = p.shape[-1]
    q, hd = D // N_CHIPS, D // 2
    o = [0, q, q + hd, 2 * q + hd, 3 * q + hd]
    ssm_in = p[o[1]:o[2]]
    return p[o[0]:o[1]], jnp.concatenate([ssm_in[:, :hd], ssm_in[:, hd:]], axis=0), p[o[2]:o[3]], p[o[3]:o[4]]


def _attn_in_views(buf_a):
    D = buf_a.shape[-1]
    return [_View(buf_a, (D, D), lambda r, c, j=j: (j, r, c), D, D) for j in range(N_CHIPS)]


def _rest_views(buf_b):
    D = buf_b.shape[-1]
    q, hd = D // N_CHIPS, D // 2
    o_out, o_in, o_glu, o_sout = 0, q, q + hd, 2 * q + hd

    def row_sharded(off):
        return _View(buf_b, (D, D), lambda r, c: (r // q, off + r % q, c), math.gcd(q, off), D)

    def ssm_in(part):
        return _View(buf_b, (D, D), lambda r, c: (2 * part + c // hd, o_in + r % hd, (r // hd) * hd + c % hd),
                     math.gcd(hd, o_in), hd)

    return dict(attn_out=row_sharded(o_out), ssm_in_u=ssm_in(0), ssm_in_gate=ssm_in(1), glu=row_sharded(o_glu),
                ssm_out=row_sharded(o_sout))


def _pack_small(parts):
    flat = jnp.concatenate([p.reshape(-1) for p in parts])
    pad = (-flat.size) % (2 * SUBLANES * LANES)
    return jnp.pad(flat, (0, pad)).reshape(-1, LANES)


def _unpack_small(buf, shapes):
    flat = buf.reshape(-1)
    out, off = [], 0
    for s in shapes:
        n = math.prod(s)
        out.append(flat[off:off + n].reshape(s))
        off += n
    return out


def _local_step(x, target, norm_g, q_g, k_g, wa, wb, ssm_small, core=None):
    Bl, S, D = x.shape
    T = Bl * S
    x0 = x.reshape(T, D)
    tgt = target.reshape(T, D)
    a_re, a_im, log_dt, b_re, b_im, c_re, c_im, d_skip, glu_b = ssm_small
    G = a_re.shape[0]
    mats = _ssm_matrices(a_re, a_im, log_dt, b_re, b_im, c_re, c_im, d_skip, S)
    w_in = _attn_in_views(wa)
    rows_b = wb.shape[-2] * (1 if core is None else 2)
    ga = lax.empty(wa.shape, F32)
    gb = lax.empty((N_CHIPS, rows_b, D), F32)

    def wgrad(a, b, key, name):
        return _matmul(a, b, name=name, ta=True, out=_rest_views(gb)[key])

    h0 = _rmsnorm_fwd(x0, norm_g[0], "norm0_fwd")
    q, k, v, gate = [_matmul(h0, w_in[j], name=f"attn_in_{j}", out_dtype=(BF16 if j == 2 else F32)) for j in range(4)]
    qn = _qknorm_fwd(q, q_g, "qnorm_fwd")
    kn = _qknorm_fwd(k, k_g, "knorm_fwd")
    if core is None:
        o, btot, og = _attn_fwd(qn, kn, v, gate, Bl, S)
    else:
        o, btot, og, wb = _attn_fwd(qn, kn, v, gate, Bl, S, rider=_all_gather_rider(wb))
        wb = wb.reshape(N_CHIPS, rows_b, D)
    w = _rest_views(wb)
    x1 = _matmul(og, w["attn_out"], name="attn_out", residual=x0)

    h1 = _rmsnorm_fwd(x1, norm_g[1], "norm1_fwd")
    u = _matmul(h1, w["ssm_in_u"], name="ssm_in_u")
    gate2 = _matmul(h1, w["ssm_in_gate"], name="ssm_in_gate")
    y, hs_re, hs_im, yg = _ssm_fwd(u, mats, Bl, S)
    gl, y3 = _matmul(yg, w["glu"], name="glu_mm", epilogue=_glu_fwd(y, gate2, glu_b))
    x2 = _matmul(y3, w["ssm_out"], name="ssm_out", residual=x1)

    dx2, dx2b, loss = _loss_head(x2, tgt)

    dy3 = _matmul(dx2b, w["ssm_out"], name="ssm_out_dgrad", tb=True)
    gb = wgrad(y3, dx2b, "ssm_out", "ssm_out_wgrad")
    dgl, dgate2, t1, dglu_b = _glu_bwd(dy3, y, gl, gate2, glu_b)
    dy = _matmul(dgl, w["glu"], name="glu_dgrad", tb=True, epilogue=_gelu_bwd(t1, y))
    gb = wgrad(yg, dgl, "glu", "glu_wgrad")
    du, dbr, dbi, dcr, dci, dlr, dli, dd = _ssm_bwd(u, dy, hs_re, hs_im, mats, Bl, S)
    dh1 = _matmul(du, w["ssm_in_u"], name="ssm_in_dgrad_u", tb=True)
    dh1 = _matmul(dgate2, w["ssm_in_gate"], name="ssm_in_dgrad_gate", tb=True, residual=dh1)
    gb = wgrad(h1, du, "ssm_in_u", "ssm_in_wgrad_u")
    gb = wgrad(h1, dgate2, "ssm_in_gate", "ssm_in_wgrad_gate")
    dx1, dx1b, dng1 = _rmsnorm_bwd(x1, norm_g[1], dh1, dx2, "norm1_bwd")
    small_early = (dng1,) + _ssm_unblock(dbr, dbi, dcr, dci, dlr, dli, dd, G) + (dglu_b.reshape(D),)

    gb = wgrad(og, dx1b, "attn_out", "attn_out_wgrad")
    if core is None:
        do, dgate = _matmul(dx1b, w["attn_out"], name="attn_out_dgrad", tb=True, epilogue=_attn_gate_bwd(o, gate))
        dqn, dkn, dv = _attn_bwd(qn, kn, v, do, btot, Bl, S)
    else:
        do, dgate, from_sibling = _matmul(dx1b, w["attn_out"], name="attn_out_dgrad", tb=True,
                                          epilogue=_attn_gate_bwd(o, gate), rider=_sibling_half_rider(gb))
        chip_sum_b = _add_halves((gb, from_sibling), core, "grads_b_add_halves")
        dqn, dkn, dv, gb, small_early = _attn_bwd(
            qn, kn, v, do, btot, Bl, S,
            rider=_join_riders(_chip_exchange_rider(chip_sum_b), _all_gather_rider(_pack_small(small_early))))
    dq, dqg = _qknorm_bwd(q, q_g, dqn, "qnorm_bwd")
    dk, dkg = _qknorm_bwd(k, k_g, dkn, "knorm_bwd")
    dproj = [dq, dk, dv, dgate]
    for j in range(4):
        ga = _matmul(h0, dproj[j], name=f"attn_in_wgrad_{j}", ta=True, out=_attn_in_views(ga)[j])
    dh0 = None
    if core is None:
        for j in range(4):
            dh0 = _matmul(dproj[j], w_in[j], name=f"attn_in_dgrad_{j}", tb=True, residual=dh0)
        dx0, _, dng0 = _rmsnorm_bwd(x0, norm_g[0], dh0, dx1, "norm0_bwd")
    else:
        chip_sum_a = _add_halves((ga, _sibling_send_half(ga, "grads_a_sibling_half")), core, "grads_a_add_halves")
        total = chip_sum_a.shape[1]
        rows = min(pl.cdiv(total * 7 // 32, BF16_ROWS) * BF16_ROWS, (total - BF16_ROWS) // 4 // BF16_ROWS * BF16_ROWS)
        parts = []
        for j in range(4):
            dh0, part = _matmul(dproj[j], w_in[j], name=f"attn_in_dgrad_{j}", tb=True, residual=dh0,
                                rider=_chip_exchange_rider(chip_sum_a, j * rows, rows))
            parts.append(part)
        dx0, _, dng0, part = _rmsnorm_bwd(x0, norm_g[0], dh0, dx1, "norm0_bwd",
                                          rider=_chip_exchange_rider(chip_sum_a, 4 * rows, total - 4 * rows))
        ga = jnp.concatenate(parts + [part], axis=1)

    return loss, dx0.reshape(Bl, S, D), ga, gb, small_early, (dng0, dqg, dkg)


def _small_early_shapes(D):
    G = D // GROUP
    return [(1, D), (G, STATE), (G, STATE), (G, STATE, GROUP), (G, STATE, GROUP), (G, GROUP, STATE), (G, GROUP, STATE),
            (D,), (D,)]


def _chip_rows(a, chip, n_per):
    return lax.dynamic_slice_in_dim(a, chip * n_per, n_per, axis=0)


def kernel(x, norm_g, attn_w_in, attn_q_g, attn_k_g, attn_w_out, ssm_w_in, ssm_A_re, ssm_A_im, ssm_log_dt, ssm_B_re, ssm_B_im, ssm_C_re, ssm_C_im, ssm_D, ssm_glu_w, ssm_glu_b, ssm_w_out, loss_target, m_norm_g, m_attn_w_in, m_attn_q_g, m_attn_k_g, m_attn_w_out, m_ssm_w_in, m_ssm_A_re, m_ssm_A_im, m_ssm_log_dt, m_ssm_B_re, m_ssm_B_im, m_ssm_C_re, m_ssm_C_im, m_ssm_D, m_ssm_glu_w, m_ssm_glu_b, m_ssm_w_out, v_norm_g, v_attn_w_in, v_attn_q_g, v_attn_k_g, v_attn_w_out, v_ssm_w_in, v_ssm_A_re, v_ssm_A_im, v_ssm_log_dt, v_ssm_B_re, v_ssm_B_im, v_ssm_C_re, v_ssm_C_im, v_ssm_D, v_ssm_glu_w, v_ssm_glu_b, v_ssm_w_out):
    D = x.shape[-1]
    cx, cy, cc = _position()
    chip = 2 * cx + cy
    G = D // GROUP
    Gl = G // N_CHIPS

    def my_half(a):
        return lax.dynamic_slice_in_dim(a, cc * (a.shape[0] // 2), a.shape[0] // 2, axis=0)

    wa = _all_gather_forwarding(my_half(attn_w_in[0].astype(BF16)), "attn_in_all_gather").reshape(N_CHIPS, D, D)
    wb_half = my_half(_pack_rest(attn_w_out[0], ssm_w_in[0], ssm_glu_w[0], ssm_w_out[0]).astype(BF16))

    ssm_local = [ssm_A_re[0], ssm_A_im[0], ssm_log_dt[0], ssm_B_re[0], ssm_B_im[0], ssm_C_re[0], ssm_C_im[0], ssm_D[0],
                 ssm_glu_b[0]]
    small_local = _pack_small(ssm_local)
    half_rows = small_local.shape[0] // 2
    small_half = lax.dynamic_slice_in_dim(small_local, cc * half_rows, half_rows, axis=0)
    small_all = _all_gather8(small_half, "ssm_params_all_gather").reshape(N_CHIPS, 2 * half_rows, LANES)
    per_chip = [_unpack_small(small_all[j], [p.shape for p in ssm_local]) for j in range(N_CHIPS)]
    ssm_full = [jnp.concatenate([per_chip[j][i] for j in range(N_CHIPS)], axis=0) for i in range(len(ssm_local))]

    loss, grad_x, gathered_a, gathered_b, small_early, small_late = _local_step(
        x, loss_target, norm_g, attn_q_g[0], attn_k_g[0], wa, wb_half, ssm_full, core=cc)
    loss = lax.psum(loss[0, 0], ("x", "y", "c"))

    q4 = D // N_CHIPS
    def both_halves(gathered, tag):
        return _sibling_fill(_sum_leading(gathered, f"grads_{tag}_sum_chips", half=cc), f"grads_{tag}_sibling_fill")

    gd = (both_halves(gathered_a, "a"),) + _unpack_rest(both_halves(gathered_b, "b"))

    (dng1, dl_re, dl_im, dbb_re, dbb_im, dc_re, dc_im, dd_skip, dglu_b) = _unpack_small(
        _sum_leading(small_early, "small_early_sum"), _small_early_shapes(D))
    late_all = _all_gather8(_pack_small(small_late), "small_late_all_gather")
    dng0, dqg, dkg = _unpack_small(_sum_leading(late_all, "small_late_sum"), [s.shape for s in small_late])
    dng = jnp.concatenate([dng0, dng1], axis=0)
    a_re, a_im, log_dt, b_re, b_im = ssm_local[:5]
    _, zoh_vjp = jax.vjp(_zoh, a_re, a_im, log_dt, b_re, b_im)
    da_re, da_im, dlog_dt, db_re, db_im = zoh_vjp((_chip_rows(dl_re, chip, Gl), _chip_rows(dl_im, chip, Gl),
                                                   _chip_rows(dbb_re, chip, Gl), _chip_rows(dbb_im, chip, Gl)))
    grads = {
        "norm_g": dng, "attn_w_in": gd[0][None], "attn_q_g": dqg, "attn_k_g": dkg, "attn_w_out": gd[1][None],
        "ssm_w_in": gd[2][None], "ssm_A_re": da_re[None], "ssm_A_im": da_im[None], "ssm_log_dt": dlog_dt[None],
        "ssm_B_re": db_re[None], "ssm_B_im": db_im[None], "ssm_C_re": _chip_rows(dc_re, chip, Gl)[None],
        "ssm_C_im": _chip_rows(dc_im, chip, Gl)[None], "ssm_D": _chip_rows(dd_skip, chip, q4)[None],
        "ssm_glu_w": gd[3][None], "ssm_glu_b": _chip_rows(dglu_b, chip, q4)[None], "ssm_w_out": gd[4][None],
    }
    weights = dict(norm_g=norm_g, attn_w_in=attn_w_in, attn_q_g=attn_q_g, attn_k_g=attn_k_g, attn_w_out=attn_w_out,
                   ssm_w_in=ssm_w_in, ssm_A_re=ssm_A_re, ssm_A_im=ssm_A_im, ssm_log_dt=ssm_log_dt, ssm_B_re=ssm_B_re,
                   ssm_B_im=ssm_B_im, ssm_C_re=ssm_C_re, ssm_C_im=ssm_C_im, ssm_D=ssm_D, ssm_glu_w=ssm_glu_w,
                   ssm_glu_b=ssm_glu_b, ssm_w_out=ssm_w_out)
    m = dict(norm_g=m_norm_g, attn_w_in=m_attn_w_in, attn_q_g=m_attn_q_g, attn_k_g=m_attn_k_g, attn_w_out=m_attn_w_out,
             ssm_w_in=m_ssm_w_in, ssm_A_re=m_ssm_A_re, ssm_A_im=m_ssm_A_im, ssm_log_dt=m_ssm_log_dt, ssm_B_re=m_ssm_B_re,
             ssm_B_im=m_ssm_B_im, ssm_C_re=m_ssm_C_re, ssm_C_im=m_ssm_C_im, ssm_D=m_ssm_D, ssm_glu_w=m_ssm_glu_w,
             ssm_glu_b=m_ssm_glu_b, ssm_w_out=m_ssm_w_out)
    v = dict(norm_g=v_norm_g, attn_w_in=v_attn_w_in, attn_q_g=v_attn_q_g, attn_k_g=v_attn_k_g, attn_w_out=v_attn_w_out,
             ssm_w_in=v_ssm_w_in, ssm_A_re=v_ssm_A_re, ssm_A_im=v_ssm_A_im, ssm_log_dt=v_ssm_log_dt, ssm_B_re=v_ssm_B_re,
             ssm_B_im=v_ssm_B_im, ssm_C_re=v_ssm_C_re, ssm_C_im=v_ssm_C_im, ssm_D=v_ssm_D, ssm_glu_w=v_ssm_glu_w,
             ssm_glu_b=v_ssm_glu_b, ssm_w_out=v_ssm_w_out)
    names = list(weights)
    dense_names = ("attn_w_in", "attn_w_out", "ssm_w_in", "ssm_glu_w", "ssm_w_out")
    delta, new_m, new_v = {}, {}, {}
    for n in dense_names:
        delta[n], new_m[n], new_v[n] = _adamw(weights[n], grads[n], m[n], v[n], "adamw_" + n)
    small_names = [n for n in names if n not in dense_names]
    small_shapes = [weights[n].shape for n in small_names]
    packs = [_pack_small([d[n] for n in small_names]) for d in (weights, grads, m, v)]
    outs = _adamw(*packs, "adamw_small")
    for res, out in zip((delta, new_m, new_v), outs):
        for n, val in zip(small_names, _unpack_small(out, small_shapes)):
            res[n] = val
    return (loss, grad_x, *[grads[n] for n in names], *[delta[n] for n in names], *[new_m[n] for n in names],
            *[new_v[n] for n in names])
```

```python
import functools
import math

import jax
import jax.numpy as jnp
from jax import lax
from jax.experimental import pallas as pl
from jax.experimental.pallas import tpu as pltpu

F32 = jnp.float32
BF16 = jnp.bfloat16

HEAD_DIM = 128
GROUP = 16
STATE = 64
RMS_EPS = 1e-6
ADAM_LR = 0.001
ADAM_B1 = 0.9
ADAM_B2 = 0.999
ADAM_EPS = 1e-08
ADAM_WD = 0.01
ADAM_STEP = 10

N_CHIPS = 4
N_DEV = 8
SUBLANES = 8
BF16_ROWS = 16
LANES = 128
VMEM_LIMIT = 56 * 1024 * 1024
ROW_BLOCK_ELEMS = 1 << 19
MATMUL_TILE = 1024
MATMUL_PANEL_BYTES = 8 * 1024 * 1024
MESH = pl.DeviceIdType.MESH

NN = (((1,), (0,)), ((), ()))
NT = (((1,), (1,)), ((), ()))
TN = (((0,), (0,)), ((), ()))


def _params(*sem):
    return pltpu.CompilerParams(dimension_semantics=sem, vmem_limit_bytes=VMEM_LIMIT)


def _dot(a, b, dims=NN):
    return lax.dot_general(a, b, dims, preferred_element_type=F32)


def _split(a):
    hi = a.astype(BF16)
    lo = (a - hi.astype(F32)).astype(BF16)
    return hi, lo


def _dot_f32(a, b, dims=NN):
    return _dot(a.astype(BF16), b.astype(BF16), dims)


def _sigmoid(x):
    return 1.0 / (1.0 + jnp.exp(-x))


def _silu_parts(x):
    s = _sigmoid(x)
    return x * s, s * (1.0 + x * (1.0 - s))


_GELU_C = math.sqrt(2.0 / math.pi)


def _gelu_parts(x):
    x2 = x * x
    t = jnp.tanh(_GELU_C * (x + 0.044715 * x * x2))
    val = 0.5 * x * (1.0 + t)
    der = 0.5 * (1.0 + t) + 0.5 * x * (1.0 - t * t) * _GELU_C * (1.0 + 3.0 * 0.044715 * x2)
    return val, der


class _View:
    def __init__(self, buf, shape, locate, row_tile, col_tile):
        self.buf, self.shape, self.locate, self.row_tile, self.col_tile = buf, shape, locate, row_tile, col_tile

    def spec(self, t0, t1, block_of):
        def index(i, j, k):
            rb, cb = block_of(i, j, k)
            slab, r, c = self.locate(rb * t0, cb * t1)
            return slab, r // t0, c // t1
        return pl.BlockSpec((None, t0, t1), index)


def _operand(x):
    return (x.buf, x.shape, x.row_tile, x.col_tile) if isinstance(x, _View) else (x, x.shape, x.shape[0], x.shape[1])


class _Epilogue:
    def __init__(self, fn, tiles, rows, out_dtypes):
        self.fn, self.tiles, self.rows, self.out_dtypes = fn, tiles, rows, out_dtypes


def _matmul(a, b, *, name, ta=False, tb=False, residual=None, out_dtype=F32, out=None, rider=None, epilogue=None):
    if residual is not None:
        epilogue = _Epilogue(lambda r, res: (r + res.astype(F32),), [residual], [], [out_dtype])
    elif epilogue is None:
        epilogue = _Epilogue(lambda r: (r,), [], [], [out_dtype])
    n_extra, n_out = len(epilogue.tiles) + len(epilogue.rows), len(epilogue.out_dtypes)
    a_arr, a_shape, a_rt, a_ct = _operand(a)
    b_arr, b_shape, b_rt, b_ct = _operand(b)
    (K, M) = a_shape if ta else a_shape[::-1]
    N = b_shape[0] if tb else b_shape[1]
    a_mt, a_kt = (a_ct, a_rt) if ta else (a_rt, a_ct)
    b_nt, b_kt = (b_rt, b_ct) if tb else (b_ct, b_rt)
    k_cap = MATMUL_PANEL_BYTES // (MATMUL_TILE * max(a_arr.dtype.itemsize, b_arr.dtype.itemsize))
    n_cap = MATMUL_TILE if n_extra + n_out <= 2 else MATMUL_TILE // 2
    tm, tn, tk = min(M, MATMUL_TILE, a_mt), min(N, n_cap, b_nt), min(K, k_cap, a_kt)
    if out is not None:
        tm, tn = min(tm, out.row_tile), min(tn, out.col_tile)
    pk = min(tk, b_kt)
    pieces = tk // pk
    nk = K // tk
    dims = ((((0,) if ta else (1,)), ((1,) if tb else (0,))), ((), ()))
    n_in = 1 + pieces + n_extra + (out is not None)

    def body(*refs):
        a_ref, b_refs = refs[0], refs[1:1 + pieces]
        e_refs = refs[1 + pieces:1 + pieces + n_extra]
        o_refs = refs[n_in:n_in + n_out]

        def finish(r):
            for o_ref, val in zip(o_refs, epilogue.fn(r, *[e[...] for e in e_refs])):
                o_ref[...] = val.astype(o_ref.dtype)

        part = None
        for p, b_ref in enumerate(b_refs):
            ks = slice(p * pk, (p + 1) * pk)
            a_blk = a_ref[...] if pieces == 1 else (a_ref[ks, :] if ta else a_ref[:, ks])
            term = _dot(a_blk.astype(BF16), b_ref[...].astype(BF16), dims)
            part = term if part is None else part + term
        if nk == 1:
            finish(part)
            return
        acc = refs[n_in + n_out]
        k = pl.program_id(2)

        @pl.when(k == 0)
        def _():
            acc[...] = part

        @pl.when(k > 0)
        def _():
            acc[...] += part

        @pl.when(k == nk - 1)
        def _():
            finish(acc[...])

    def spec(x, t0, t1, block_of):
        if isinstance(x, _View):
            return x.spec(t0, t1, block_of)
        return pl.BlockSpec((t0, t1), block_of)

    a_spec = spec(a, tk, tm, lambda i, j, k: (k, i)) if ta else spec(a, tm, tk, lambda i, j, k: (i, k))
    b_specs = [spec(b, tn, pk, lambda i, j, k, p=p: (j, k * pieces + p)) if tb else
               spec(b, pk, tn, lambda i, j, k, p=p: (k * pieces + p, j)) for p in range(pieces)]
    tile_spec = pl.BlockSpec((tm, tn), lambda i, j, k: (i, j))
    in_specs = [a_spec] + b_specs + [tile_spec] * len(epilogue.tiles) + [
        pl.BlockSpec((1, tn), lambda i, j, k: (0, j))] * len(epilogue.rows)
    args = [a_arr] + [b_arr] * pieces + list(epilogue.tiles) + list(epilogue.rows)
    aliases = {}
    if out is None:
        out_specs = [tile_spec] * n_out
        out_shape = [jax.ShapeDtypeStruct((M, N), d) for d in epilogue.out_dtypes]
    else:
        out_specs = [out.spec(tm, tn, lambda i, j, k: (i, j))]
        out_shape = [jax.ShapeDtypeStruct(out.buf.shape, out.buf.dtype)]
        in_specs.append(pl.BlockSpec(memory_space=pl.ANY))
        args.append(out.buf)
        aliases = {len(args) - 1: 0}
    grid = (M // tm, N // tn, nk)
    scratch = [pltpu.VMEM((tm, tn), F32)] if nk > 1 else []
    body, in_specs, out_specs, out_shape, scratch, extra = _ride(rider, body, grid, in_specs, out_specs, out_shape,
                                                                 scratch)
    results = pl.pallas_call(
        body, name=name, grid=grid, in_specs=in_specs, out_specs=out_specs, out_shape=out_shape,
        scratch_shapes=scratch, input_output_aliases=aliases,
        compiler_params=_params(*(("parallel", "parallel", "arbitrary") if rider is None else ("arbitrary",) * 3)),
    )(*args, *extra)
    return results[0] if len(results) == 1 else results


def _row_tile(T, C):
    tr = max(SUBLANES, min(T, ROW_BLOCK_ELEMS // C) // SUBLANES * SUBLANES)
    while T % tr:
        tr -= SUBLANES
    return tr


def _rows_call(body, name, T, C, row_ins, full_ins, row_outs, acc_outs=(), rider=None):
    tr = _row_tile(T, C)
    row_spec = pl.BlockSpec((tr, C), lambda i: (i, 0))
    in_specs = [row_spec] * len(row_ins) + [pl.BlockSpec(f.shape, lambda i, n=f.ndim: (0,) * n) for f in full_ins]
    out_specs = [row_spec] * len(row_outs) + [pl.BlockSpec(s, lambda i, n=len(s): (0,) * n) for s in acc_outs]
    out_shape = [jax.ShapeDtypeStruct((T, C), d) for d in row_outs] + [jax.ShapeDtypeStruct(s, F32) for s in acc_outs]
    grid = (T // tr,)
    body, in_specs, out_specs, out_shape, scratch, extra = _ride(rider, body, grid, in_specs, out_specs, out_shape, [])
    return pl.pallas_call(
        body, name=name, grid=grid, in_specs=in_specs, out_specs=out_specs, out_shape=out_shape, scratch_shapes=scratch,
        compiler_params=_params("arbitrary" if acc_outs or rider is not None else "parallel"),
    )(*row_ins, *full_ins, *extra)


def _rmsnorm_fwd(x, g, name):
    T, C = x.shape

    def body(x_ref, g_ref, h_ref):
        xv = x_ref[...]
        r = lax.rsqrt(jnp.mean(xv * xv, axis=-1, keepdims=True) + RMS_EPS)
        h_ref[...] = ((xv * r) * g_ref[...]).astype(BF16)

    return _rows_call(body, name, T, C, [x], [g.reshape(1, C)], [BF16])[0]


def _rmsnorm_bwd(x, g, dh, dres, name, rider=None):
    T, C = x.shape

    def body(x_ref, dh_ref, dres_ref, g_ref, dx_ref, dxb_ref, dg_ref):
        @pl.when(pl.program_id(0) == 0)
        def _():
            dg_ref[...] = jnp.zeros_like(dg_ref)

        xv = x_ref[...]
        dhv = dh_ref[...]
        r = lax.rsqrt(jnp.mean(xv * xv, axis=-1, keepdims=True) + RMS_EPS)
        xn = xv * r
        dg_ref[...] += jnp.sum(dhv * xn, axis=0, keepdims=True)
        dxn = dhv * g_ref[...]
        dx = dres_ref[...] + r * (dxn - xn * jnp.mean(dxn * xn, axis=-1, keepdims=True))
        dx_ref[...] = dx
        dxb_ref[...] = dx.astype(BF16)

    return _rows_call(body, name, T, C, [x, dh, dres], [g.reshape(1, C)], [F32, BF16], [(1, C)], rider=rider)


def _heads(C):
    return [slice(h * HEAD_DIM, (h + 1) * HEAD_DIM) for h in range(C // HEAD_DIM)]


def _qknorm_fwd(q, g, name):
    T, C = q.shape

    def body(q_ref, g_ref, o_ref):
        for head in _heads(C):
            xv = q_ref[:, head]
            r = lax.rsqrt(jnp.mean(xv * xv, axis=-1, keepdims=True) + RMS_EPS)
            o_ref[:, head] = ((xv * r) * g_ref[...]).astype(BF16)

    return _rows_call(body, name, T, C, [q], [g.reshape(1, HEAD_DIM)], [BF16])[0]


def _qknorm_bwd(q, g, dqn, name):
    T, C = q.shape

    def body(q_ref, d_ref, g_ref, dq_ref, dg_ref):
        @pl.when(pl.program_id(0) == 0)
        def _():
            dg_ref[...] = jnp.zeros_like(dg_ref)

        dg = jnp.zeros((1, HEAD_DIM), F32)
        for head in _heads(C):
            xv = q_ref[:, head]
            dv = d_ref[:, head]
            r = lax.rsqrt(jnp.mean(xv * xv, axis=-1, keepdims=True) + RMS_EPS)
            xn = xv * r
            dg = dg + jnp.sum(dv * xn, axis=0, keepdims=True)
            dxn = dv * g_ref[...]
            dq_ref[:, head] = (r * (dxn - xn * jnp.mean(dxn * xn, axis=-1, keepdims=True))).astype(BF16)
        dg_ref[...] += dg

    return _rows_call(body, name, T, C, [q, dqn], [g.reshape(1, HEAD_DIM)], [BF16], [(1, HEAD_DIM)])


ATT_TQ = 512
ATT_TK = 256
ATT_HEADS = 2


def _logsig_pair(z):
    a = jnp.minimum(z, 0.0) - jnp.log(1.0 + jnp.exp(-jnp.abs(z)))
    return a, a - z


def _tri(n, strict_upper_src):
    j = lax.broadcasted_iota(jnp.int32, (n, n), 0)
    s = lax.broadcasted_iota(jnp.int32, (n, n), 1)
    if strict_upper_src == "gt":
        m = j > s
    elif strict_upper_src == "le":
        m = j <= s
    else:
        m = j < s
    return jnp.where(m, 1.0, 0.0).astype(BF16)


def _cumdot(x, tri):
    hi, lo = _split(x)
    return _dot(hi, tri) + _dot(lo, tri)


def _attn_tiles(S):
    tq, tk = min(ATT_TQ, S), min(ATT_TK, S)
    return tq, tk, S // tq, tq // tk


def _attn_fwd(qn, kn, v, gate, Bl, S, rider=None):
    T, C = qn.shape
    H = C // HEAD_DIM
    tq, tk, nq, kpq = _attn_tiles(S)
    hp = min(ATT_HEADS, H)
    scale = 1.0 / math.sqrt(HEAD_DIM)

    def body(q_ref, k_ref, v_ref, g_ref, o_ref, bt_ref, og_ref):
        i = pl.program_id(2)
        tri = _tri(tk, "gt")
        rowpos = lax.broadcasted_iota(jnp.int32, (tq, tk), 0) + i * tq
        colpos = lax.broadcasted_iota(jnp.int32, (tq, tk), 1)
        o_ref[...] = jnp.zeros_like(o_ref)
        bt_ref[...] = jnp.zeros_like(bt_ref)

        heads = _heads(hp * HEAD_DIM)

        def diagonal():
            work = []
            for p in reversed(range(kpq)):
                j = i * kpq + p
                rows = pl.ds(pl.multiple_of(j * tk, tk), tk)
                live = slice(p * tk, tq)
                mask = (colpos[live] + j * tk) < rowpos[live]
                for head in heads:
                    a, b = _logsig_pair(_dot(q_ref[live, head], k_ref[rows, head], NT) * scale)
                    b = jnp.where(mask, b, 0.0)
                    work.append((head, rows, live, a, b, _cumdot(b, tri), mask))
            for n, head in enumerate(heads):
                for rows, live, a, b, suffix, mask in [w[1:] for w in work if w[0] == head]:
                    w = jnp.where(mask, jnp.exp(a + suffix + bt_ref[n, live]), 0.0)
                    o_ref[live, head] += _dot(w.astype(BF16), v_ref[rows, head])
                    bt_ref[n, live] += jnp.sum(b, axis=-1, keepdims=True)

        def group(g):
            work = []
            for p in reversed(range(kpq)):
                rows = pl.ds(pl.multiple_of((g * kpq + p) * tk, tk), tk)
                for head in heads:
                    a, b = _logsig_pair(_dot(q_ref[:, head], k_ref[rows, head], NT) * scale)
                    work.append((head, rows, a, b, _cumdot(b, tri)))
            for n, head in enumerate(heads):
                total = bt_ref[n]
                out = None
                for rows, a, b, suffix in [w[1:] for w in work if w[0] == head]:
                    term = _dot(jnp.exp(a + suffix + total).astype(BF16), v_ref[rows, head])
                    out = term if out is None else out + term
                    total = total + jnp.sum(b, axis=-1, keepdims=True)
                o_ref[:, head] += out
                bt_ref[n] = total

        diagonal()

        def step(n, carry):
            group(i - 1 - n)
            return carry

        lax.fori_loop(0, i, step, 0)
        og_ref[...] = (o_ref[...] * _silu_parts(g_ref[...])[0]).astype(BF16)

    qspec = pl.BlockSpec((tq, hp * HEAD_DIM), lambda b, h, i: (b * nq + i, h))
    kspec = pl.BlockSpec((S, hp * HEAD_DIM), lambda b, h, i: (b, h))
    btspec = pl.BlockSpec((None, hp, tq, 1), lambda b, h, i: (b, h, i, 0))
    grid = (Bl, H // hp, nq)
    body, in_specs, out_specs, out_shape, scratch, extra = _ride(
        rider, body, grid, [qspec, kspec, kspec, qspec], [qspec, btspec, qspec],
        [jax.ShapeDtypeStruct((T, C), F32), jax.ShapeDtypeStruct((Bl, H, S, 1), F32),
         jax.ShapeDtypeStruct((T, C), BF16)], [])
    return pl.pallas_call(
        body, name="attn_fwd", grid=grid, in_specs=in_specs, out_specs=out_specs, out_shape=out_shape,
        scratch_shapes=scratch, compiler_params=_params("arbitrary", "arbitrary", "arbitrary"),
    )(qn, kn, v, gate, *extra)


def _attn_bwd(qn, kn, v, do, btot, Bl, S, rider=None):
    T, C = qn.shape
    H = C // HEAD_DIM
    tq, tk, nq, kpq = _attn_tiles(S)
    hp = min(ATT_HEADS, H)
    scale = 1.0 / math.sqrt(HEAD_DIM)

    def body(q_ref, k_ref, v_ref, do_ref, bt_ref, dq_ref, dk_ref, dvb_ref, pb_ref, pdl_ref, dv_ref):
        i = pl.program_id(2)

        @pl.when(i == 0)
        def _():
            dk_ref[...] = jnp.zeros_like(dk_ref)
            dv_ref[...] = jnp.zeros_like(dv_ref)

        tri_le = _tri(tk, "le")
        tri_lt = _tri(tk, "lt")
        rowpos = lax.broadcasted_iota(jnp.int32, (tq, tk), 0) + i * tq
        colpos = lax.broadcasted_iota(jnp.int32, (tq, tk), 1)
        dq_ref[...] = jnp.zeros_like(dq_ref)
        pb_ref[...] = bt_ref[...]
        pdl_ref[...] = jnp.zeros_like(pdl_ref)

        heads = _heads(hp * HEAD_DIM)

        def group(g):
            work = []
            for p in range(kpq):
                rows = pl.ds(pl.multiple_of((g * kpq + p) * tk, tk), tk)
                for head in heads:
                    a, b = _logsig_pair(_dot(q_ref[:, head], k_ref[rows, head], NT) * scale)
                    work.append((head, rows, a, b, _cumdot(b, tri_le), _dot(do_ref[:, head], v_ref[rows, head], NT)))
            for n, head in enumerate(heads):
                remaining = pb_ref[n]
                swept = pdl_ref[n]
                dq = None
                for rows, a, b, cum, dw in [w[1:] for w in work if w[0] == head]:
                    w = jnp.exp(a + (remaining - cum))
                    dl = dw * w
                    prefix = swept + _cumdot(dl, tri_lt)
                    beta = jnp.exp(a)
                    dzb = ((dl * (1.0 - beta) - beta * prefix) * scale).astype(BF16)
                    term = _dot(dzb, k_ref[rows, head])
                    dq = term if dq is None else dq + term
                    dk_ref[rows, head] += _dot(dzb, q_ref[:, head], TN)
                    dv_ref[rows, head] += _dot(w.astype(BF16), do_ref[:, head], TN)
                    remaining = remaining - jnp.sum(b, axis=-1, keepdims=True)
                    swept = swept + jnp.sum(dl, axis=-1, keepdims=True)
                dq_ref[:, head] += dq
                pb_ref[n] = remaining
                pdl_ref[n] = swept

        def diagonal():
            work = []
            for p in range(kpq):
                j = i * kpq + p
                rows = pl.ds(pl.multiple_of(j * tk, tk), tk)
                live = slice(p * tk, tq)
                mask = (colpos[live] + j * tk) < rowpos[live]
                for head in heads:
                    a, b = _logsig_pair(_dot(q_ref[live, head], k_ref[rows, head], NT) * scale)
                    b = jnp.where(mask, b, 0.0)
                    work.append((head, rows, live, a, b, _cumdot(b, tri_le),
                                 _dot(do_ref[live, head], v_ref[rows, head], NT), mask))
            for n, head in enumerate(heads):
                for rows, live, a, b, cum, dw, mask in [w[1:] for w in work if w[0] == head]:
                    w = jnp.where(mask, jnp.exp(a + (pb_ref[n, live] - cum)), 0.0)
                    dl = dw * w
                    prefix = pdl_ref[n, live] + _cumdot(dl, tri_lt)
                    beta = jnp.exp(a)
                    dzb = (jnp.where(mask, dl * (1.0 - beta) - beta * prefix, 0.0) * scale).astype(BF16)
                    dq_ref[live, head] += _dot(dzb, k_ref[rows, head])
                    dk_ref[rows, head] += _dot(dzb, q_ref[live, head], TN)
                    dv_ref[rows, head] += _dot(w.astype(BF16), do_ref[live, head], TN)
                    pb_ref[n, live] -= jnp.sum(b, axis=-1, keepdims=True)
                    pdl_ref[n, live] += jnp.sum(dl, axis=-1, keepdims=True)

        def step(g, carry):
            group(g)
            return carry

        lax.fori_loop(0, i, step, 0)
        diagonal()

        @pl.when(i == nq - 1)
        def _():
            dvb_ref[...] = dv_ref[...].astype(BF16)

    qspec = pl.BlockSpec((tq, hp * HEAD_DIM), lambda b, h, i: (b * nq + i, h))
    kspec = pl.BlockSpec((S, hp * HEAD_DIM), lambda b, h, i: (b, h))
    btspec = pl.BlockSpec((None, hp, tq, 1), lambda b, h, i: (b, h, i, 0))
    grid = (Bl, H // hp, nq)
    body, in_specs, out_specs, out_shape, scratch, extra = _ride(
        rider, body, grid, [qspec, kspec, kspec, qspec, btspec], [qspec, kspec, kspec],
        [jax.ShapeDtypeStruct((T, C), F32), jax.ShapeDtypeStruct((T, C), F32), jax.ShapeDtypeStruct((T, C), BF16)],
        [pltpu.VMEM((hp, tq, 1), F32), pltpu.VMEM((hp, tq, 1), F32), pltpu.VMEM((S, hp * HEAD_DIM), F32)])
    return pl.pallas_call(
        body, name="attn_bwd", grid=grid, in_specs=in_specs, out_specs=out_specs, out_shape=out_shape,
        scratch_shapes=scratch, compiler_params=_params("arbitrary", "arbitrary", "arbitrary"),
    )(qn, kn, v, do, btot, *extra)


SSM_TIME_BLOCK_FWD = 1024
SSM_TIME_BLOCK_BWD = 512
CHUNK = SUBLANES


def _cmadd(xr, xi, ar, ai, sr, si):
    return xr + ar * sr - ai * si, xi + ar * si + ai * sr


def _chunk_scan(xr, xi, tab_ref, cr, ci, reverse):
    for lvl, d in enumerate((1, 2, 4)):
        shift = (CHUNK - d) if reverse else d
        sr = pltpu.roll(xr, shift, 0)
        si = pltpu.roll(xi, shift, 0)
        ar = tab_ref[pl.ds((2 * lvl) * CHUNK, CHUNK), :]
        ai = tab_ref[pl.ds((2 * lvl + 1) * CHUNK, CHUNK), :]
        xr, xi = _cmadd(xr, xi, ar, ai, sr, si)
    pr = tab_ref[pl.ds(6 * CHUNK, CHUNK), :]
    pi = tab_ref[pl.ds(7 * CHUNK, CHUNK), :]
    return _cmadd(xr, xi, pr, pi, cr, ci)


def _ssm_dims(S, C, time_block=SSM_TIME_BLOCK_BWD):
    G = C // GROUP
    GT = min(16, G)
    return G, GT, G // GT, GT * GROUP, GT * STATE, min(time_block, S)


def _ssm_fwd(u, mats, Bl, S):
    T, C = u.shape
    G, GT, ngt, cw, sw, TB = _ssm_dims(S, C, SSM_TIME_BLOCK_FWD)
    ntb = S // TB
    nch = TB // CHUNK

    def body(u_ref, bre_ref, bim_ref, cre_ref, cim_ref, d_ref, tab_ref, y_ref, hr_ref, hi_ref, yg_ref, car_r, car_i):
        @pl.when(pl.program_id(2) == 0)
        def _():
            car_r[...] = jnp.zeros_like(car_r)
            car_i[...] = jnp.zeros_like(car_i)

        uv = u_ref[...]
        hr_ref[...] = _dot_f32(uv, bre_ref[...])
        hi_ref[...] = _dot_f32(uv, bim_ref[...])

        def step(n, carry):
            cr, ci = carry
            rows = pl.ds(pl.multiple_of(n * CHUNK, CHUNK), CHUNK)
            xr, xi = _chunk_scan(hr_ref[rows, :], hi_ref[rows, :], tab_ref, cr, ci, False)
            hr_ref[rows, :] = xr
            hi_ref[rows, :] = xi
            last = (CHUNK - 1, CHUNK)
            return (jnp.broadcast_to(xr[last[0]:last[1], :], xr.shape), jnp.broadcast_to(xi[last[0]:last[1], :], xi.shape))

        cr, ci = lax.fori_loop(0, nch, step, (car_r[...], car_i[...]))
        car_r[...] = cr
        car_i[...] = ci
        y = _dot_f32(hr_ref[...], cre_ref[...]) - _dot_f32(hi_ref[...], cim_ref[...]) + d_ref[...] * uv
        y_ref[...] = y
        yg_ref[...] = _gelu_parts(y)[0].astype(BF16)

    uspec = pl.BlockSpec((TB, cw), lambda g, b, t: (b * ntb + t, g))
    hspec = pl.BlockSpec((TB, sw), lambda g, b, t: (b * ntb + t, g))

    def gspec(r, c):
        return pl.BlockSpec((None, r, c), lambda g, b, t: (g, 0, 0))

    return pl.pallas_call(
        body, name="ssm_fwd", grid=(ngt, Bl, ntb),
        in_specs=[uspec, gspec(cw, sw), gspec(cw, sw), gspec(sw, cw), gspec(sw, cw), gspec(1, cw), gspec(8 * CHUNK, sw)],
        out_specs=[uspec, hspec, hspec, uspec],
        out_shape=[jax.ShapeDtypeStruct((T, C), F32), jax.ShapeDtypeStruct((T, G * STATE), F32),
                   jax.ShapeDtypeStruct((T, G * STATE), F32), jax.ShapeDtypeStruct((T, C), BF16)],
        scratch_shapes=[pltpu.VMEM((CHUNK, sw), F32), pltpu.VMEM((CHUNK, sw), F32)],
        compiler_params=_params("parallel", "arbitrary", "arbitrary"),
    )(u, mats["bbd_re"], mats["bbd_im"], mats["cbd_re"], mats["cbd_im"], mats["d"], mats["tab_fwd"])


def _ssm_bwd(u, dy, h_re, h_im, mats, Bl, S):
    T, C = u.shape
    G, GT, ngt, cw, sw, TB = _ssm_dims(S, C)
    ntb = S // TB
    nch = TB // CHUNK
    rpb = TB // CHUNK

    def body(u_ref, dy_ref, hr_ref, hi_ref, hpr_ref, hpi_ref, cre_ref, cim_ref, bre_ref, bim_ref, d_ref, tab_ref,
             du_ref, dbr_ref, dbi_ref, dcr_ref, dci_ref, dlr_ref, dli_ref, dd_ref, gr_ref, gi_ref, car_r, car_i):
        b = pl.program_id(1)
        t = pl.program_id(2)

        @pl.when((b == 0) & (t == 0))
        def _():
            for ref in (dbr_ref, dbi_ref, dcr_ref, dci_ref, dlr_ref, dli_ref, dd_ref):
                ref[...] = jnp.zeros_like(ref)

        @pl.when(t == 0)
        def _():
            car_r[...] = jnp.zeros_like(car_r)
            car_i[...] = jnp.zeros_like(car_i)

        uv = u_ref[...]
        dyv = dy_ref[...]
        gr_ref[...] = _dot_f32(dyv, cre_ref[...], NT)
        gi_ref[...] = -_dot_f32(dyv, cim_ref[...], NT)
        alive = jnp.where(t == ntb - 1, 0.0, 1.0)
        row0 = lax.broadcasted_iota(jnp.int32, (CHUNK, sw), 0) == 0

        def step(m, carry):
            cr, ci, ar, ai = carry
            n = nch - 1 - m
            rows = pl.ds(pl.multiple_of(n * CHUNK, CHUNK), CHUNK)
            prow = pl.ds(pl.multiple_of(jnp.maximum(n - 1, 0) * CHUNK, CHUNK), CHUNK)
            xr, xi = _chunk_scan(gr_ref[rows, :], gi_ref[rows, :], tab_ref, cr, ci, True)
            gr_ref[rows, :] = xr
            gi_ref[rows, :] = xi
            first = n == 0
            pr = jnp.where(first, hpr_ref[...] * alive, hr_ref[prow, :])
            pi = jnp.where(first, hpi_ref[...] * alive, hi_ref[prow, :])
            sr = jnp.where(row0, pltpu.roll(pr, 1, 0), pltpu.roll(hr_ref[rows, :], 1, 0))
            si = jnp.where(row0, pltpu.roll(pi, 1, 0), pltpu.roll(hi_ref[rows, :], 1, 0))
            ar = ar + xr * sr + xi * si
            ai = ai + xi * sr - xr * si
            return (jnp.broadcast_to(xr[0:1, :], xr.shape), jnp.broadcast_to(xi[0:1, :], xi.shape), ar, ai)

        zero = jnp.zeros((CHUNK, sw), F32)
        cr, ci, ar, ai = lax.fori_loop(0, nch, step, (car_r[...], car_i[...], zero, zero))
        car_r[...] = cr
        car_i[...] = ci
        dlr_ref[...] += ar
        dli_ref[...] += ai
        gr = gr_ref[...]
        gi = gi_ref[...]
        dbr_ref[...] += _dot_f32(uv, gr, TN)
        dbi_ref[...] += _dot_f32(uv, gi, TN)
        dcr_ref[...] += _dot_f32(hr_ref[...], dyv, TN)
        dci_ref[...] -= _dot_f32(hi_ref[...], dyv, TN)
        dd_ref[...] += jnp.sum(dyv * uv, axis=0, keepdims=True)
        du_ref[...] = (_dot_f32(gr, bre_ref[...], NT) + _dot_f32(gi, bim_ref[...], NT) + d_ref[...] * dyv).astype(BF16)

    def tblk(b, t):
        return b * ntb + (ntb - 1 - t)

    uspec = pl.BlockSpec((TB, cw), lambda g, b, t: (tblk(b, t), g))
    hspec = pl.BlockSpec((TB, sw), lambda g, b, t: (tblk(b, t), g))
    hpspec = pl.BlockSpec((CHUNK, sw), lambda g, b, t: (jnp.maximum(tblk(b, t) * rpb - 1, 0), g))

    def gspec(r, c):
        return pl.BlockSpec((None, r, c), lambda g, b, t: (g, 0, 0))

    def gshape(r, c):
        return jax.ShapeDtypeStruct((ngt, r, c), F32)

    return pl.pallas_call(
        body, name="ssm_bwd", grid=(ngt, Bl, ntb),
        in_specs=[uspec, uspec, hspec, hspec, hpspec, hpspec, gspec(sw, cw), gspec(sw, cw), gspec(cw, sw), gspec(cw, sw),
                  gspec(1, cw), gspec(8 * CHUNK, sw)],
        out_specs=[uspec, gspec(cw, sw), gspec(cw, sw), gspec(sw, cw), gspec(sw, cw), gspec(CHUNK, sw), gspec(CHUNK, sw),
                   gspec(1, cw)],
        out_shape=[jax.ShapeDtypeStruct((T, C), BF16), gshape(cw, sw), gshape(cw, sw), gshape(sw, cw), gshape(sw, cw),
                   gshape(CHUNK, sw), gshape(CHUNK, sw), gshape(1, cw)],
        scratch_shapes=[pltpu.VMEM((TB, sw), F32), pltpu.VMEM((TB, sw), F32), pltpu.VMEM((CHUNK, sw), F32),
                        pltpu.VMEM((CHUNK, sw), F32)],
        compiler_params=_params("arbitrary", "arbitrary", "arbitrary"),
    )(u, dy, h_re, h_im, h_re, h_im, mats["cbd_re"], mats["cbd_im"], mats["bbd_re"], mats["bbd_im"], mats["d"],
      mats["tab_rev"])


def _zoh(a_re, a_im, log_dt, b_re, b_im):
    dt = jnp.exp(log_dt)[:, None]
    mag = jnp.exp(a_re * dt)
    l_re = mag * jnp.cos(a_im * dt)
    l_im = mag * jnp.sin(a_im * dt)
    den = a_re * a_re + a_im * a_im
    f_re = ((l_re - 1.0) * a_re + l_im * a_im) / den
    f_im = (l_im * a_re - (l_re - 1.0) * a_im) / den
    bb_re = f_re[..., None] * b_re - f_im[..., None] * b_im
    bb_im = f_re[..., None] * b_im + f_im[..., None] * b_re
    return l_re, l_im, bb_re, bb_im


def _ssm_matrices(a_re, a_im, log_dt, b_re, b_im, c_re, c_im, d, S):
    G = a_re.shape[0]
    _, GT, ngt, cw, sw, _ = _ssm_dims(S, G * GROUP)
    _, _, bb_re, bb_im = _zoh(a_re, a_im, log_dt, b_re, b_im)
    eye = jnp.eye(GT, dtype=BF16)

    def bd_b(bb):
        return jnp.einsum("tgpi,gh->tgihp", bb.astype(BF16).reshape(ngt, GT, STATE, GROUP), eye).reshape(ngt, cw, sw)

    def bd_c(c):
        return jnp.einsum("tgip,gh->tgphi", c.astype(BF16).reshape(ngt, GT, GROUP, STATE), eye).reshape(ngt, sw, cw)

    dt = jnp.exp(log_dt)[:, None]

    def power(k, conj):
        mag = jnp.exp(k * a_re * dt)
        ang = k * a_im * dt
        return (mag * jnp.cos(ang)).reshape(ngt, 1, sw), ((-1.0 if conj else 1.0) * mag * jnp.sin(ang)).reshape(ngt, 1, sw)

    r = jnp.arange(CHUNK)[None, :, None]

    def table(reverse):
        parts = []
        for dd in (1, 2, 4):
            pr, pi = power(float(dd), reverse)
            keep = (r <= CHUNK - 1 - dd) if reverse else (r >= dd)
            parts += [jnp.where(keep, pr, 0.0), jnp.where(keep, pi, 0.0)]
        exps = [(CHUNK - k) if reverse else (k + 1) for k in range(CHUNK)]
        pw = [power(float(e), reverse) for e in exps]
        parts += [jnp.concatenate([p[0] for p in pw], axis=1), jnp.concatenate([p[1] for p in pw], axis=1)]
        return jnp.concatenate([jnp.broadcast_to(p, (ngt, CHUNK, sw)) for p in parts], axis=1)

    return dict(bbd_re=bd_b(bb_re), bbd_im=bd_b(bb_im), cbd_re=bd_c(c_re), cbd_im=bd_c(c_im), d=d.reshape(ngt, 1, cw),
                tab_fwd=table(False), tab_rev=table(True))


def _ssm_unblock(dbr, dbi, dcr, dci, dlr, dli, dd, G):
    ngt = dbr.shape[0]
    GT = G // ngt
    eye = jnp.eye(GT, dtype=F32)

    def ub(x):
        return jnp.einsum("tgihp,gh->tgpi", x.reshape(ngt, GT, GROUP, GT, STATE), eye).reshape(G, STATE, GROUP)

    def uc(x):
        return jnp.einsum("tgphi,gh->tgip", x.reshape(ngt, GT, STATE, GT, GROUP), eye).reshape(G, GROUP, STATE)

    return (dlr.sum(axis=1).reshape(G, STATE), dli.sum(axis=1).reshape(G, STATE), ub(dbr), ub(dbi), uc(dcr), uc(dci),
            dd.reshape(G * GROUP))


def _attn_gate_bwd(o, gate):
    def fn(dog, o_blk, g_blk):
        val, der = _silu_parts(g_blk)
        return dog * val, dog * o_blk * der

    return _Epilogue(fn, [o, gate], [], [BF16, BF16])


def _glu_fwd(y, gate, glu_b):
    def fn(gl, y_blk, g_blk, b_blk):
        return gl, _gelu_parts(y_blk)[0] * _sigmoid(gl + b_blk) * _silu_parts(g_blk)[0]

    return _Epilogue(fn, [y, gate], [glu_b.reshape(1, -1)], [F32, BF16])


def _glu_bwd(dy3, y, gl, gate, glu_b):
    T, C = y.shape

    def body(d_ref, y_ref, gl_ref, g_ref, b_ref, dgl_ref, dgate_ref, t1_ref, db_ref):
        @pl.when(pl.program_id(0) == 0)
        def _():
            db_ref[...] = jnp.zeros_like(db_ref)

        yg = _gelu_parts(y_ref[...])[0]
        sg = _sigmoid(gl_ref[...] + b_ref[...])
        sl, sld = _silu_parts(g_ref[...])
        dv = d_ref[...]
        dy2 = dv * sl
        dgl = dy2 * yg * sg * (1.0 - sg)
        dgl_ref[...] = dgl.astype(BF16)
        dgate_ref[...] = (dv * (yg * sg) * sld).astype(BF16)
        t1_ref[...] = dy2 * sg
        db_ref[...] += jnp.sum(dgl, axis=0, keepdims=True)

    return _rows_call(body, "glu_bwd", T, C, [dy3, y, gl, gate], [glu_b.reshape(1, C)], [BF16, BF16, F32], [(1, C)])


def _gelu_bwd(t1, y):
    return _Epilogue(lambda t2, t1_blk, y_blk: ((t1_blk + t2) * _gelu_parts(y_blk)[1],), [t1, y], [], [F32])


def _loss_head(x2, target):
    T, C = x2.shape

    def body(x_ref, t_ref, d_ref, db_ref, l_ref):
        @pl.when(pl.program_id(0) == 0)
        def _():
            l_ref[...] = jnp.zeros_like(l_ref)

        e = x_ref[...] - t_ref[...]
        d = e * (1.0 / C)
        d_ref[...] = d
        db_ref[...] = d.astype(BF16)
        l_ref[...] += 0.5 * jnp.sum(jnp.sum(e * e, axis=-1, keepdims=True) * (1.0 / C), axis=0, keepdims=True)

    return _rows_call(body, "loss_head", T, C, [x2, target], [], [F32, BF16], [(1, 1)])


def _adamw(w, g, m, v, name):
    shape = w.shape
    C = shape[-1]
    R = w.size // C
    bc1 = 1.0 - ADAM_B1 ** ADAM_STEP
    bc2 = 1.0 - ADAM_B2 ** ADAM_STEP

    def body(w_ref, g_ref, m_ref, v_ref, d_ref, nm_ref, nv_ref):
        gv = g_ref[...]
        mn = ADAM_B1 * m_ref[...] + (1.0 - ADAM_B1) * gv
        vn = ADAM_B2 * v_ref[...] + (1.0 - ADAM_B2) * (gv * gv)
        d_ref[...] = -ADAM_LR * ((mn / bc1) / (jnp.sqrt(vn / bc2) + ADAM_EPS) + ADAM_WD * w_ref[...])
        nm_ref[...] = mn
        nv_ref[...] = vn

    outs = _rows_call(body, name, R, C, [a.reshape(R, C) for a in (w, g, m, v)], [], [F32, F32, F32])
    return [o.reshape(shape) for o in outs]


def _sum_leading(x, name, half=None):
    n, R, C = x.shape
    tr = _row_tile(R, C * n)
    nb = R // tr

    def body(*refs):
        x_ref, o_ref = refs[-2:]
        acc = x_ref[0].astype(F32)
        for k in range(1, n):
            acc = acc + x_ref[k].astype(F32)
        o_ref[...] = acc

    if half is None:
        return pl.pallas_call(
            body, name=name, grid=(nb,), in_specs=[pl.BlockSpec((n, tr, C), lambda i: (0, i, 0))],
            out_specs=pl.BlockSpec((tr, C), lambda i: (i, 0)), out_shape=jax.ShapeDtypeStruct((R, C), F32),
            compiler_params=_params("parallel"),
        )(x)
    grid_spec = pltpu.PrefetchScalarGridSpec(
        num_scalar_prefetch=1, grid=(nb,), in_specs=[pl.BlockSpec((n, tr, C), lambda i, c_ref: (0, i, 0))],
        out_specs=pl.BlockSpec((tr, C), lambda i, c_ref: (c_ref[0] * nb + i, 0)))
    return pl.pallas_call(
        body, name=name, grid_spec=grid_spec, out_shape=jax.ShapeDtypeStruct((2 * R, C), F32),
        compiler_params=_params("parallel"),
    )(half.reshape(1).astype(jnp.int32), x)


def _add_halves(g, c, name):
    full, recv = g
    n, R, C = full.shape
    half = R // 2
    tr = _row_tile(half, C)
    nb = half // tr

    def body(c_ref, a_ref, b_ref, o_ref):
        o_ref[...] = (a_ref[...] + b_ref[...]).astype(BF16)

    grid_spec = pltpu.PrefetchScalarGridSpec(
        num_scalar_prefetch=1, grid=(n, nb),
        in_specs=[pl.BlockSpec((None, tr, C), lambda j, i, c_ref: (j, c_ref[0] * nb + i, 0)),
                  pl.BlockSpec((None, tr, C), lambda j, i, c_ref: (j, i, 0))],
        out_specs=pl.BlockSpec((None, tr, C), lambda j, i, c_ref: (j, i, 0)))
    return pl.pallas_call(
        body, name=name, grid_spec=grid_spec, out_shape=jax.ShapeDtypeStruct((n, half, C), BF16),
        compiler_params=_params("parallel", "parallel"),
    )(c.reshape(1).astype(jnp.int32), full, recv)


ANY = pl.BlockSpec(memory_space=pl.ANY)


def _position():
    return lax.axis_index("x"), lax.axis_index("y"), lax.axis_index("c")


def _all_gather8(blk, name):
    return _standalone(_all_gather_rider(blk), name)[0]


class _Rider:
    def __init__(self, arrays, out_shapes, sems, start, finish):
        self.arrays, self.out_shapes, self.sems, self.start, self.finish = arrays, out_shapes, sems, start, finish


def _join_riders(*riders):
    def split(refs):
        ins, outs, sems = [], [], []
        for r in riders:
            ins.append(refs[:len(r.arrays)])
            refs = refs[len(r.arrays):]
        for r in riders:
            outs.append(refs[:len(r.out_shapes)])
            refs = refs[len(r.out_shapes):]
        for r in riders:
            sems.append(refs[:len(r.sems)])
            refs = refs[len(r.sems):]
        return [i + o + s for i, o, s in zip(ins, outs, sems)]

    def start(*refs):
        for r, own in zip(riders, split(refs)):
            r.start(*own)

    def finish(*refs):
        for r, own in zip(riders, split(refs)):
            r.finish(*own)

    return _Rider([a for r in riders for a in r.arrays], [s for r in riders for s in r.out_shapes],
                  [s for r in riders for s in r.sems], start, finish)


def _standalone(rider, name):
    def body(*refs):
        rider.start(*refs)
        rider.finish(*refs)

    return pl.pallas_call(
        body, name=name, in_specs=[ANY] * len(rider.arrays), out_specs=[ANY] * len(rider.out_shapes),
        out_shape=rider.out_shapes, scratch_shapes=rider.sems,
    )(*rider.arrays)


def _ride(rider, body, grid, in_specs, out_specs, out_shape, scratch):
    if rider is None:
        return body, in_specs, out_specs, out_shape, scratch, []
    ni, no, ns = len(in_specs), len(out_specs), len(scratch)
    ri, ro = len(rider.arrays), len(rider.out_shapes)

    def full(*refs):
        ins, refs = refs[:ni], refs[ni:]
        r_ins, refs = refs[:ri], refs[ri:]
        outs, refs = refs[:no], refs[no:]
        r_outs, refs = refs[:ro], refs[ro:]
        scr, r_sems = refs[:ns], refs[ns:]
        ids = [pl.program_id(a) for a in range(len(grid))]
        first = functools.reduce(jnp.logical_and, [i == 0 for i in ids])
        last = functools.reduce(jnp.logical_and, [i == g - 1 for i, g in zip(ids, grid)])

        @pl.when(first)
        def _():
            rider.start(*r_ins, *r_outs, *r_sems)

        body(*ins, *outs, *scr)

        @pl.when(last)
        def _():
            rider.finish(*r_ins, *r_outs, *r_sems)

    return (full, in_specs + [ANY] * ri, out_specs + [ANY] * ro, out_shape + rider.out_shapes, scratch + rider.sems,
            rider.arrays)


def _all_gather_rider(blk):
    M, N = blk.shape

    def copies(x_ref, out_ref, send_sems, recv_sems, local_sem):
        x, y, c = _position()
        me, sibling = (x, y, c), (x, y, 1 - c)
        chips = [(1 - x, y), (x, 1 - y), (1 - x, 1 - y)]

        def slab(px, py, pc):
            return out_ref.at[4 * px + 2 * py + pc]

        def copy(k, block, to, src=None):
            return pltpu.make_async_remote_copy(
                src_ref=slab(*block) if src is None else src, dst_ref=slab(*block),
                send_sem=send_sems.at[k], recv_sem=recv_sems.at[k], device_id=to, device_id_type=MESH)

        mine = pltpu.make_async_copy(x_ref, slab(*me), local_sem)
        first = [copy(0, me, sibling, src=x_ref)]
        first += [copy(1 + j, me, (*chip, c), src=x_ref) for j, chip in enumerate(chips)]
        passed = [copy(4 + j, (*chip, c), sibling) for j, chip in enumerate(chips)]
        arrivals = [copy(1 + j, (*chip, c), me) for j, chip in enumerate(chips)]
        from_sibling = [copy(0, sibling, me)] + [copy(4 + j, (*chip, 1 - c), me) for j, chip in enumerate(chips)]
        return mine, first, passed, arrivals, from_sibling

    def start(*refs):
        mine, first, _, _, _ = copies(*refs)
        mine.start()
        for cp in first:
            cp.start()

    def finish(*refs):
        mine, first, passed, arrivals, from_sibling = copies(*refs)
        for arrival, onward in zip(arrivals, passed):
            arrival.wait_recv()
            onward.start()
        for cp in from_sibling:
            cp.wait_recv()
        for cp in first + passed:
            cp.wait_send()
        mine.wait()

    return _Rider([blk], [jax.ShapeDtypeStruct((N_DEV, M, N), blk.dtype)],
                  [pltpu.SemaphoreType.DMA((7,)), pltpu.SemaphoreType.DMA((7,)), pltpu.SemaphoreType.DMA], start, finish)


def _all_gather_forwarding(blk, name):
    M, N = blk.shape
    half = M // 2

    def body(x_ref, out_ref, send_sems, recv_sems, local_sem):
        x, y, c = _position()
        me, sibling = (x, y, c), (x, y, 1 - c)
        xn, yn, dg = (1 - x, y), (x, 1 - y), (1 - x, 1 - y)

        def slab(chip, pc, part=None):
            ref = out_ref.at[4 * chip[0] + 2 * chip[1] + pc]
            return ref if part is None else ref.at[pl.ds(part * half, half), :]

        def copy(k, place, to, src=None):
            return pltpu.make_async_remote_copy(src_ref=place if src is None else src, dst_ref=place,
                                                send_sem=send_sems.at[k], recv_sem=recv_sems.at[k], device_id=to,
                                                device_id_type=MESH)

        mine = pltpu.make_async_copy(x_ref, slab((x, y), c), local_sem)
        own = [copy(0, slab((x, y), c), sibling, src=x_ref), copy(1, slab((x, y), c), (*xn, c), src=x_ref),
               copy(2, slab((x, y), c), (*yn, c), src=x_ref)]
        mine.start()
        for cp in own:
            cp.start()
        copy(1, slab(xn, c), me).wait_recv()
        onward = [copy(3, slab(xn, c, 0), (*yn, c)), copy(5, slab(xn, c), sibling)]
        for cp in onward:
            cp.start()
        copy(2, slab(yn, c), me).wait_recv()
        onward += [copy(4, slab(yn, c, 1), (*xn, c)), copy(6, slab(yn, c), sibling)]
        for cp in onward[2:]:
            cp.start()
        copy(3, slab(dg, c, 0), me).wait_recv()
        onward.append(copy(7, slab(dg, c, 0), sibling))
        onward[-1].start()
        copy(4, slab(dg, c, 1), me).wait_recv()
        onward.append(copy(8, slab(dg, c, 1), sibling))
        onward[-1].start()
        for k, place in ((0, slab((x, y), 1 - c)), (5, slab(xn, 1 - c)), (6, slab(yn, 1 - c)), (7, slab(dg, 1 - c, 0)),
                         (8, slab(dg, 1 - c, 1))):
            copy(k, place, me).wait_recv()
        for cp in own + onward:
            cp.wait_send()
        mine.wait()

    return pl.pallas_call(
        body, name=name, in_specs=[ANY], out_specs=ANY, out_shape=jax.ShapeDtypeStruct((N_DEV, M, N), blk.dtype),
        scratch_shapes=[pltpu.SemaphoreType.DMA((9,)), pltpu.SemaphoreType.DMA((9,)), pltpu.SemaphoreType.DMA],
    )(blk)


def _sibling_half_rider(g):
    n, R, C = g.shape
    half = R // 2

    def copy(g_ref, out_ref, send_sem, recv_sem):
        x, y, c = _position()
        return pltpu.make_async_remote_copy(
            src_ref=g_ref.at[:, pl.ds((1 - c) * half, half), :], dst_ref=out_ref, send_sem=send_sem, recv_sem=recv_sem,
            device_id=(x, y, 1 - c), device_id_type=MESH)

    return _Rider([g], [jax.ShapeDtypeStruct((n, half, C), g.dtype)], [pltpu.SemaphoreType.DMA, pltpu.SemaphoreType.DMA],
                  lambda *refs: copy(*refs).start(), lambda *refs: copy(*refs).wait())


def _sibling_send_half(g, name):
    return _standalone(_sibling_half_rider(g), name)[0]


def _chip_exchange_rider(p, row_off=0, rows=None):
    rows = p.shape[1] if rows is None else rows

    def copies(p_ref, out_ref, send_sems, recv_sems, local_sem):
        x, y, c = _position()
        my = 2 * x + y
        chips = [(1 - x, y), (x, 1 - y), (1 - x, 1 - y)]

        def src(slab):
            return p_ref.at[slab, pl.ds(row_off, rows), :]

        mine = pltpu.make_async_copy(src(my), out_ref.at[my], local_sem)
        sends = [pltpu.make_async_remote_copy(
            src_ref=src(2 * px + py), dst_ref=out_ref.at[my], send_sem=send_sems.at[k], recv_sem=recv_sems.at[k],
            device_id=(px, py, c), device_id_type=MESH) for k, (px, py) in enumerate(chips)]
        arrivals = [pltpu.make_async_remote_copy(
            src_ref=src(my), dst_ref=out_ref.at[2 * px + py], send_sem=send_sems.at[k], recv_sem=recv_sems.at[k],
            device_id=(px, py, c), device_id_type=MESH) for k, (px, py) in enumerate(chips)]
        return mine, sends, arrivals

    def start(*refs):
        mine, sends, _ = copies(*refs)
        mine.start()
        for cp in sends:
            cp.start()

    def finish(*refs):
        mine, sends, arrivals = copies(*refs)
        for cp in arrivals:
            cp.wait_recv()
        for cp in sends:
            cp.wait_send()
        mine.wait()

    return _Rider([p], [jax.ShapeDtypeStruct((p.shape[0], rows, p.shape[2]), p.dtype)],
                  [pltpu.SemaphoreType.DMA((3,)), pltpu.SemaphoreType.DMA((3,)), pltpu.SemaphoreType.DMA], start, finish)


def _sibling_fill(buf, name):
    M = buf.shape[0] // 2

    def body(buf_ref, out_ref, send_sem, recv_sem):
        x, y, c = _position()
        mine = pl.ds(c * M, M)
        theirs = pl.ds((1 - c) * M, M)
        pltpu.make_async_remote_copy(src_ref=buf_ref.at[mine], dst_ref=out_ref.at[mine], send_sem=send_sem,
                                     recv_sem=recv_sem, device_id=(x, y, 1 - c), device_id_type=MESH).start()
        pltpu.make_async_remote_copy(src_ref=buf_ref.at[mine], dst_ref=out_ref.at[theirs], send_sem=send_sem,
                                     recv_sem=recv_sem, device_id=(x, y, 1 - c), device_id_type=MESH).wait()

    return pl.pallas_call(
        body, name=name, in_specs=[ANY], out_specs=ANY, out_shape=jax.ShapeDtypeStruct(buf.shape, buf.dtype),
        scratch_shapes=[pltpu.SemaphoreType.DMA, pltpu.SemaphoreType.DMA], input_output_aliases={0: 0},
    )(buf)


def _pack_rest(attn_out, ssm_in, glu_w, ssm_out):
    hd = ssm_in.shape[0] // 2
    return jnp.concatenate([attn_out, jnp.concatenate([ssm_in[:hd], ssm_in[hd:]], axis=1), glu_w, ssm_out], axis=0)


def _unpack_rest(p):
    D = p.shape[-1]
    q, hd = D // N_CHIPS, D // 2
    o = [0, q, q + hd, 2 * q + hd, 3 * q + hd]
    ssm_in = p[o[1]:o[2]]
    return p[o[0]:o[1]], jnp.concatenate([ssm_in[:, :hd], ssm_in[:, hd:]], axis=0), p[o[2]:o[3]], p[o[3]:o[4]]


def _attn_in_views(buf_a):
    D = buf_a.shape[-1]
    return [_View(buf_a, (D, D), lambda r, c, j=j: (j, r, c), D, D) for j in range(N_CHIPS)]


def _rest_views(buf_b):
    D = buf_b.shape[-1]
    q, hd = D // N_CHIPS, D // 2
    o_out, o_in, o_glu, o_sout = 0, q, q + hd, 2 * q + hd

    def row_sharded(off):
        return _View(buf_b, (D, D), lambda r, c: (r // q, off + r % q, c), math.gcd(q, off), D)

    def ssm_in(part):
        return _View(buf_b, (D, D), lambda r, c: (2 * part + c // hd, o_in + r % hd, (r // hd) * hd + c % hd),
                     math.gcd(hd, o_in), hd)

    return dict(attn_out=row_sharded(o_out), ssm_in_u=ssm_in(0), ssm_in_gate=ssm_in(1), glu=row_sharded(o_glu),
                ssm_out=row_sharded(o_sout))


def _pack_small(parts):
    flat = jnp.concatenate([p.reshape(-1) for p in parts])
    pad = (-flat.size) % (2 * SUBLANES * LANES)
    return jnp.pad(flat, (0, pad)).reshape(-1, LANES)


def _unpack_small(buf, shapes):
    flat = buf.reshape(-1)
    out, off = [], 0
    for s in shapes:
        n = math.prod(s)
        out.append(flat[off:off + n].reshape(s))
        off += n
    return out


def _local_step(x, target, norm_g, q_g, k_g, wa, wb, ssm_small, core=None):
    Bl, S, D = x.shape
    T = Bl * S
    x0 = x.reshape(T, D)
    tgt = target.reshape(T, D)
    a_re, a_im, log_dt, b_re, b_im, c_re, c_im, d_skip, glu_b = ssm_small
    G = a_re.shape[0]
    mats = _ssm_matrices(a_re, a_im, log_dt, b_re, b_im, c_re, c_im, d_skip, S)
    w_in = _attn_in_views(wa)
    rows_b = wb.shape[-2] * (1 if core is None else 2)
    ga = lax.empty(wa.shape, F32)
    gb = lax.empty((N_CHIPS, rows_b, D), F32)

    def wgrad(a, b, key, name):
        return _matmul(a, b, name=name, ta=True, out=_rest_views(gb)[key])

    h0 = _rmsnorm_fwd(x0, norm_g[0], "norm0_fwd")
    q, k, v, gate = [_matmul(h0, w_in[j], name=f"attn_in_{j}", out_dtype=(BF16 if j == 2 else F32)) for j in range(4)]
    qn = _qknorm_fwd(q, q_g, "qnorm_fwd")
    kn = _qknorm_fwd(k, k_g, "knorm_fwd")
    if core is None:
        o, btot, og = _attn_fwd(qn, kn, v, gate, Bl, S)
    else:
        o, btot, og, wb = _attn_fwd(qn, kn, v, gate, Bl, S, rider=_all_gather_rider(wb))
        wb = wb.reshape(N_CHIPS, rows_b, D)
    w = _rest_views(wb)
    x1 = _matmul(og, w["attn_out"], name="attn_out", residual=x0)

    h1 = _rmsnorm_fwd(x1, norm_g[1], "norm1_fwd")
    u = _matmul(h1, w["ssm_in_u"], name="ssm_in_u")
    gate2 = _matmul(h1, w["ssm_in_gate"], name="ssm_in_gate")
    y, hs_re, hs_im, yg = _ssm_fwd(u, mats, Bl, S)
    gl, y3 = _matmul(yg, w["glu"], name="glu_mm", epilogue=_glu_fwd(y, gate2, glu_b))
    x2 = _matmul(y3, w["ssm_out"], name="ssm_out", residual=x1)

    dx2, dx2b, loss = _loss_head(x2, tgt)

    dy3 = _matmul(dx2b, w["ssm_out"], name="ssm_out_dgrad", tb=True)
    gb = wgrad(y3, dx2b, "ssm_out", "ssm_out_wgrad")
    dgl, dgate2, t1, dglu_b = _glu_bwd(dy3, y, gl, gate2, glu_b)
    dy = _matmul(dgl, w["glu"], name="glu_dgrad", tb=True, epilogue=_gelu_bwd(t1, y))
    gb = wgrad(yg, dgl, "glu", "glu_wgrad")
    du, dbr, dbi, dcr, dci, dlr, dli, dd = _ssm_bwd(u, dy, hs_re, hs_im, mats, Bl, S)
    dh1 = _matmul(du, w["ssm_in_u"], name="ssm_in_dgrad_u", tb=True)
    dh1 = _matmul(dgate2, w["ssm_in_gate"], name="ssm_in_dgrad_gate", tb=True, residual=dh1)
    gb = wgrad(h1, du, "ssm_in_u", "ssm_in_wgrad_u")
    gb = wgrad(h1, dgate2, "ssm_in_gate", "ssm_in_wgrad_gate")
    dx1, dx1b, dng1 = _rmsnorm_bwd(x1, norm_g[1], dh1, dx2, "norm1_bwd")
    small_early = (dng1,) + _ssm_unblock(dbr, dbi, dcr, dci, dlr, dli, dd, G) + (dglu_b.reshape(D),)

    gb = wgrad(og, dx1b, "attn_out", "attn_out_wgrad")
    if core is None:
        do, dgate = _matmul(dx1b, w["attn_out"], name="attn_out_dgrad", tb=True, epilogue=_attn_gate_bwd(o, gate))
        dqn, dkn, dv = _attn_bwd(qn, kn, v, do, btot, Bl, S)
    else:
        do, dgate, from_sibling = _matmul(dx1b, w["attn_out"], name="attn_out_dgrad", tb=True,
                                          epilogue=_attn_gate_bwd(o, gate), rider=_sibling_half_rider(gb))
        chip_sum_b = _add_halves((gb, from_sibling), core, "grads_b_add_halves")
        dqn, dkn, dv, gb, small_early = _attn_bwd(
            qn, kn, v, do, btot, Bl, S,
            rider=_join_riders(_chip_exchange_rider(chip_sum_b), _all_gather_rider(_pack_small(small_early))))
    dq, dqg = _qknorm_bwd(q, q_g, dqn, "qnorm_bwd")
    dk, dkg = _qknorm_bwd(k, k_g, dkn, "knorm_bwd")
    dproj = [dq, dk, dv, dgate]
    for j in range(4):
        ga = _matmul(h0, dproj[j], name=f"attn_in_wgrad_{j}", ta=True, out=_attn_in_views(ga)[j])
    dh0 = None
    if core is None:
        for j in range(4):
            dh0 = _matmul(dproj[j], w_in[j], name=f"attn_in_dgrad_{j}", tb=True, residual=dh0)
        dx0, _, dng0 = _rmsnorm_bwd(x0, norm_g[0], dh0, dx1, "norm0_bwd")
    else:
        chip_sum_a = _add_halves((ga, _sibling_send_half(ga, "grads_a_sibling_half")), core, "grads_a_add_halves")
        total = chip_sum_a.shape[1]
        rows = min(pl.cdiv(total * 7 // 32, BF16_ROWS) * BF16_ROWS, (total - BF16_ROWS) // 4 // BF16_ROWS * BF16_ROWS)
        parts = []
        for j in range(4):
            dh0, part = _matmul(dproj[j], w_in[j], name=f"attn_in_dgrad_{j}", tb=True, residual=dh0,
                                rider=_chip_exchange_rider(chip_sum_a, j * rows, rows))
            parts.append(part)
        dx0, _, dng0, part = _rmsnorm_bwd(x0, norm_g[0], dh0, dx1, "norm0_bwd",
                                          rider=_chip_exchange_rider(chip_sum_a, 4 * rows, total - 4 * rows))
        ga = jnp.concatenate(parts + [part], axis=1)

    return loss, dx0.reshape(Bl, S, D), ga, gb, small_early, (dng0, dqg, dkg)


def _small_early_shapes(D):
    G = D // GROUP
    return [(1, D), (G, STATE), (G, STATE), (G, STATE, GROUP), (G, STATE, GROUP), (G, GROUP, STATE), (G, GROUP, STATE),
            (D,), (D,)]


def _chip_rows(a, chip, n_per):
    return lax.dynamic_slice_in_dim(a, chip * n_per, n_per, axis=0)


def kernel(x, norm_g, attn_w_in, attn_q_g, attn_k_g, attn_w_out, ssm_w_in, ssm_A_re, ssm_A_im, ssm_log_dt, ssm_B_re, ssm_B_im, ssm_C_re, ssm_C_im, ssm_D, ssm_glu_w, ssm_glu_b, ssm_w_out, loss_target, m_norm_g, m_attn_w_in, m_attn_q_g, m_attn_k_g, m_attn_w_out, m_ssm_w_in, m_ssm_A_re, m_ssm_A_im, m_ssm_log_dt, m_ssm_B_re, m_ssm_B_im, m_ssm_C_re, m_ssm_C_im, m_ssm_D, m_ssm_glu_w, m_ssm_glu_b, m_ssm_w_out, v_norm_g, v_attn_w_in, v_attn_q_g, v_attn_k_g, v_attn_w_out, v_ssm_w_in, v_ssm_A_re, v_ssm_A_im, v_ssm_log_dt, v_ssm_B_re, v_ssm_B_im, v_ssm_C_re, v_ssm_C_im, v_ssm_D, v_ssm_glu_w, v_ssm_glu_b, v_ssm_w_out):
    D = x.shape[-1]
    cx, cy, cc = _position()
    chip = 2 * cx + cy
    G = D // GROUP
    Gl = G // N_CHIPS

    def my_half(a):
        return lax.dynamic_slice_in_dim(a, cc * (a.shape[0] // 2), a.shape[0] // 2, axis=0)

    wa = _all_gather_forwarding(my_half(attn_w_in[0].astype(BF16)), "attn_in_all_gather").reshape(N_CHIPS, D, D)
    wb_half = my_half(_pack_rest(attn_w_out[0], ssm_w_in[0], ssm_glu_w[0], ssm_w_out[0]).astype(BF16))

    ssm_local = [ssm_A_re[0], ssm_A_im[0], ssm_log_dt[0], ssm_B_re[0], ssm_B_im[0], ssm_C_re[0], ssm_C_im[0], ssm_D[0],
                 ssm_glu_b[0]]
    small_local = _pack_small(ssm_local)
    half_rows = small_local.shape[0] // 2
    small_half = lax.dynamic_slice_in_dim(small_local, cc * half_rows, half_rows, axis=0)
    small_all = _all_gather8(small_half, "ssm_params_all_gather").reshape(N_CHIPS, 2 * half_rows, LANES)
    per_chip = [_unpack_small(small_all[j], [p.shape for p in ssm_local]) for j in range(N_CHIPS)]
    ssm_full = [jnp.concatenate([per_chip[j][i] for j in range(N_CHIPS)], axis=0) for i in range(len(ssm_local))]

    loss, grad_x, gathered_a, gathered_b, small_early, small_late = _local_step(
        x, loss_target, norm_g, attn_q_g[0], attn_k_g[0], wa, wb_half, ssm_full, core=cc)
    loss = lax.psum(loss[0, 0], ("x", "y", "c"))

    q4 = D // N_CHIPS
    def both_halves(gathered, tag):
        return _sibling_fill(_sum_leading(gathered, f"grads_{tag}_sum_chips", half=cc), f"grads_{tag}_sibling_fill")

    gd = (both_halves(gathered_a, "a"),) + _unpack_rest(both_halves(gathered_b, "b"))

    (dng1, dl_re, dl_im, dbb_re, dbb_im, dc_re, dc_im, dd_skip, dglu_b) = _unpack_small(
        _sum_leading(small_early, "small_early_sum"), _small_early_shapes(D))
    late_all = _all_gather8(_pack_small(small_late), "small_late_all_gather")
    dng0, dqg, dkg = _unpack_small(_sum_leading(late_all, "small_late_sum"), [s.shape for s in small_late])
    dng = jnp.concatenate([dng0, dng1], axis=0)
    a_re, a_im, log_dt, b_re, b_im = ssm_local[:5]
    _, zoh_vjp = jax.vjp(_zoh, a_re, a_im, log_dt, b_re, b_im)
    da_re, da_im, dlog_dt, db_re, db_im = zoh_vjp((_chip_rows(dl_re, chip, Gl), _chip_rows(dl_im, chip, Gl),
                                                   _chip_rows(dbb_re, chip, Gl), _chip_rows(dbb_im, chip, Gl)))
    grads = {
        "norm_g": dng, "attn_w_in": gd[0][None], "attn_q_g": dqg, "attn_k_g": dkg, "attn_w_out": gd[1][None],
        "ssm_w_in": gd[2][None], "ssm_A_re": da_re[None], "ssm_A_im": da_im[None], "ssm_log_dt": dlog_dt[None],
        "ssm_B_re": db_re[None], "ssm_B_im": db_im[None], "ssm_C_re": _chip_rows(dc_re, chip, Gl)[None],
        "ssm_C_im": _chip_rows(dc_im, chip, Gl)[None], "ssm_D": _chip_rows(dd_skip, chip, q4)[None],
        "ssm_glu_w": gd[3][None], "ssm_glu_b": _chip_rows(dglu_b, chip, q4)[None], "ssm_w_out": gd[4][None],
    }
    weights = dict(norm_g=norm_g, attn_w_in=attn_w_in, attn_q_g=attn_q_g, attn_k_g=attn_k_g, attn_w_out=attn_w_out,
                   ssm_w_in=ssm_w_in, ssm_A_re=ssm_A_re, ssm_A_im=ssm_A_im, ssm_log_dt=ssm_log_dt, ssm_B_re=ssm_B_re,
                   ssm_B_im=ssm_B_im, ssm_C_re=ssm_C_re, ssm_C_im=ssm_C_im, ssm_D=ssm_D, ssm_glu_w=ssm_glu_w,
                   ssm_glu_b=ssm_glu_b, ssm_w_out=ssm_w_out)
    m = dict(norm_g=m_norm_g, attn_w_in=m_attn_w_in, attn_q_g=m_attn_q_g, attn_k_g=m_attn_k_g, attn_w_out=m_attn_w_out,
             ssm_w_in=m_ssm_w_in, ssm_A_re=m_ssm_A_re, ssm_A_im=m_ssm_A_im, ssm_log_dt=m_ssm_log_dt, ssm_B_re=m_ssm_B_re,
             ssm_B_im=m_ssm_B_im, ssm_C_re=m_ssm_C_re, ssm_C_im=m_ssm_C_im, ssm_D=m_ssm_D, ssm_glu_w=m_ssm_glu_w,
             ssm_glu_b=m_ssm_glu_b, ssm_w_out=m_ssm_w_out)
    v = dict(norm_g=v_norm_g, attn_w_in=v_attn_w_in, attn_q_g=v_attn_q_g, attn_k_g=v_attn_k_g, attn_w_out=v_attn_w_out,
             ssm_w_in=v_ssm_w_in, ssm_A_re=v_ssm_A_re, ssm_A_im=v_ssm_A_im, ssm_log_dt=v_ssm_log_dt, ssm_B_re=v_ssm_B_re,
             ssm_B_im=v_ssm_B_im, ssm_C_re=v_ssm_C_re, ssm_C_im=v_ssm_C_im, ssm_D=v_ssm_D, ssm_glu_w=v_ssm_glu_w,
             ssm_glu_b=v_ssm_glu_b, ssm_w_out=v_ssm_w_out)
    names = list(weights)
    dense_names = ("attn_w_in", "attn_w_out", "ssm_w_in", "ssm_glu_w", "ssm_w_out")
    delta, new_m, new_v = {}, {}, {}
    for n in dense_names:
        delta[n], new_m[n], new_v[n] = _adamw(weights[n], grads[n], m[n], v[n], "adamw_" + n)
    small_names = [n for n in names if n not in dense_names]
    small_shapes = [weights[n].shape for n in small_names]
    packs = [_pack_small([d[n] for n in small_names]) for d in (weights, grads, m, v)]
    outs = _adamw(*packs, "adamw_small")
    for res, out in zip((delta, new_m, new_v), outs):
        for n, val in zip(small_names, _unpack_small(out, small_shapes)):
            res[n] = val
    return (loss, grad_x, *[grads[n] for n in names], *[delta[n] for n in names], *[new_m[n] for n in names],
            *[new_v[n] for n in names])
```

```python
import functools
import math

import jax
import jax.numpy as jnp
from jax import lax
from jax.experimental import pallas as pl
from jax.experimental.pallas import tpu as pltpu

F32 = jnp.float32
BF16 = jnp.bfloat16

HEAD_DIM = 128
GROUP = 16
STATE = 64
RMS_EPS = 1e-6
ADAM_LR = 0.001
ADAM_B1 = 0.9
ADAM_B2 = 0.999
ADAM_EPS = 1e-08
ADAM_WD = 0.01
ADAM_STEP = 10

N_CHIPS = 4
N_DEV = 8
SUBLANES = 8
BF16_ROWS = 16
LANES = 128
VMEM_LIMIT = 56 * 1024 * 1024
ROW_BLOCK_ELEMS = 1 << 19
MATMUL_TILE = 1024
MATMUL_PANEL_BYTES = 8 * 1024 * 1024
MESH = pl.DeviceIdType.MESH

NN = (((1,), (0,)), ((), ()))
NT = (((1,), (1,)), ((), ()))
TN = (((0,), (0,)), ((), ()))


def _params(*sem):
    return pltpu.CompilerParams(dimension_semantics=sem, vmem_limit_bytes=VMEM_LIMIT)


def _dot(a, b, dims=NN):
    return lax.dot_general(a, b, dims, preferred_element_type=F32)


def _split(a):
    hi = a.astype(BF16)
    lo = (a - hi.astype(F32)).astype(BF16)
    return hi, lo


def _dot_f32(a, b, dims=NN):
    return _dot(a.astype(BF16), b.astype(BF16), dims)


def _sigmoid(x):
    return 1.0 / (1.0 + jnp.exp(-x))


def _silu_parts(x):
    s = _sigmoid(x)
    return x * s, s * (1.0 + x * (1.0 - s))


_GELU_C = math.sqrt(2.0 / math.pi)


def _gelu_parts(x):
    x2 = x * x
    t = jnp.tanh(_GELU_C * (x + 0.044715 * x * x2))
    val = 0.5 * x * (1.0 + t)
    der = 0.5 * (1.0 + t) + 0.5 * x * (1.0 - t * t) * _GELU_C * (1.0 + 3.0 * 0.044715 * x2)
    return val, der


class _View:
    def __init__(self, buf, shape, locate, row_tile, col_tile):
        self.buf, self.shape, self.locate, self.row_tile, self.col_tile = buf, shape, locate, row_tile, col_tile

    def spec(self, t0, t1, block_of):
        def index(i, j, k):
            rb, cb = block_of(i, j, k)
            slab, r, c = self.locate(rb * t0, cb * t1)
            return slab, r // t0, c // t1
        return pl.BlockSpec((None, t0, t1), index)


def _operand(x):
    return (x.buf, x.shape, x.row_tile, x.col_tile) if isinstance(x, _View) else (x, x.shape, x.shape[0], x.shape[1])


class _Epilogue:
    def __init__(self, fn, tiles, rows, out_dtypes):
        self.fn, self.tiles, self.rows, self.out_dtypes = fn, tiles, rows, out_dtypes


def _matmul(a, b, *, name, ta=False, tb=False, residual=None, out_dtype=F32, out=None, rider=None, epilogue=None):
    if residual is not None:
        epilogue = _Epilogue(lambda r, res: (r + res.astype(F32),), [residual], [], [out_dtype])
    elif epilogue is None:
        epilogue = _Epilogue(lambda r: (r,), [], [], [out_dtype])
    n_extra, n_out = len(epilogue.tiles) + len(epilogue.rows), len(epilogue.out_dtypes)
    a_arr, a_shape, a_rt, a_ct = _operand(a)
    b_arr, b_shape, b_rt, b_ct = _operand(b)
    (K, M) = a_shape if ta else a_shape[::-1]
    N = b_shape[0] if tb else b_shape[1]
    a_mt, a_kt = (a_ct, a_rt) if ta else (a_rt, a_ct)
    b_nt, b_kt = (b_rt, b_ct) if tb else (b_ct, b_rt)
    k_cap = MATMUL_PANEL_BYTES // (MATMUL_TILE * max(a_arr.dtype.itemsize, b_arr.dtype.itemsize))
    n_cap = MATMUL_TILE if n_extra + n_out <= 2 else MATMUL_TILE // 2
    tm, tn, tk = min(M, MATMUL_TILE, a_mt), min(N, n_cap, b_nt), min(K, k_cap, a_kt)
    if out is not None:
        tm, tn = min(tm, out.row_tile), min(tn, out.col_tile)
    pk = min(tk, b_kt)
    pieces = tk // pk
    nk = K // tk
    dims = ((((0,) if ta else (1,)), ((1,) if tb else (0,))), ((), ()))
    n_in = 1 + pieces + n_extra + (out is not None)

    def body(*refs):
        a_ref, b_refs = refs[0], refs[1:1 + pieces]
        e_refs = refs[1 + pieces:1 + pieces + n_extra]
        o_refs = refs[n_in:n_in + n_out]

        def finish(r):
            for o_ref, val in zip(o_refs, epilogue.fn(r, *[e[...] for e in e_refs])):
                o_ref[...] = val.astype(o_ref.dtype)

        part = None
        for p, b_ref in enumerate(b_refs):
            ks = slice(p * pk, (p + 1) * pk)
            a_blk = a_ref[...] if pieces == 1 else (a_ref[ks, :] if ta else a_ref[:, ks])
            term = _dot(a_blk.astype(BF16), b_ref[...].astype(BF16), dims)
            part = term if part is None else part + term
        if nk == 1:
            finish(part)
            return
        acc = refs[n_in + n_out]
        k = pl.program_id(2)

        @pl.when(k == 0)
        def _():
            acc[...] = part

        @pl.when(k > 0)
        def _():
            acc[...] += part

        @pl.when(k == nk - 1)
        def _():
            finish(acc[...])

    def spec(x, t0, t1, block_of):
        if isinstance(x, _View):
            return x.spec(t0, t1, block_of)
        return pl.BlockSpec((t0, t1), block_of)

    a_spec = spec(a, tk, tm, lambda i, j, k: (k, i)) if ta else spec(a, tm, tk, lambda i, j, k: (i, k))
    b_specs = [spec(b, tn, pk, lambda i, j, k, p=p: (j, k * pieces + p)) if tb else
               spec(b, pk, tn, lambda i, j, k, p=p: (k * pieces + p, j)) for p in range(pieces)]
    tile_spec = pl.BlockSpec((tm, tn), lambda i, j, k: (i, j))
    in_specs = [a_spec] + b_specs + [tile_spec] * len(epilogue.tiles) + [
        pl.BlockSpec((1, tn), lambda i, j, k: (0, j))] * len(epilogue.rows)
    args = [a_arr] + [b_arr] * pieces + list(epilogue.tiles) + list(epilogue.rows)
    aliases = {}
    if out is None:
        out_specs = [tile_spec] * n_out
        out_shape = [jax.ShapeDtypeStruct((M, N), d) for d in epilogue.out_dtypes]
    else:
        out_specs = [out.spec(tm, tn, lambda i, j, k: (i, j))]
        out_shape = [jax.ShapeDtypeStruct(out.buf.shape, out.buf.dtype)]
        in_specs.append(pl.BlockSpec(memory_space=pl.ANY))
        args.append(out.buf)
        aliases = {len(args) - 1: 0}
    grid = (M // tm, N // tn, nk)
    scratch = [pltpu.VMEM((tm, tn), F32)] if nk > 1 else []
    body, in_specs, out_specs, out_shape, scratch, extra = _ride(rider, body, grid, in_specs, out_specs, out_shape,
                                                                 scratch)
    results = pl.pallas_call(
        body, name=name, grid=grid, in_specs=in_specs, out_specs=out_specs, out_shape=out_shape,
        scratch_shapes=scratch, input_output_aliases=aliases,
        compiler_params=_params(*(("parallel", "parallel", "arbitrary") if rider is None else ("arbitrary",) * 3)),
    )(*args, *extra)
    return results[0] if len(results) == 1 else results


def _row_tile(T, C):
    tr = max(SUBLANES, min(T, ROW_BLOCK_ELEMS // C) // SUBLANES * SUBLANES)
    while T % tr:
        tr -= SUBLANES
    return tr


def _rows_call(body, name, T, C, row_ins, full_ins, row_outs, acc_outs=(), rider=None):
    tr = _row_tile(T, C)
    row_spec = pl.BlockSpec((tr, C), lambda i: (i, 0))
    in_specs = [row_spec] * len(row_ins) + [pl.BlockSpec(f.shape, lambda i, n=f.ndim: (0,) * n) for f in full_ins]
    out_specs = [row_spec] * len(row_outs) + [pl.BlockSpec(s, lambda i, n=len(s): (0,) * n) for s in acc_outs]
    out_shape = [jax.ShapeDtypeStruct((T, C), d) for d in row_outs] + [jax.ShapeDtypeStruct(s, F32) for s in acc_outs]
    grid = (T // tr,)
    body, in_specs, out_specs, out_shape, scratch, extra = _ride(rider, body, grid, in_specs, out_specs, out_shape, [])
    return pl.pallas_call(
        body, name=name, grid=grid, in_specs=in_specs, out_specs=out_specs, out_shape=out_shape, scratch_shapes=scratch,
        compiler_params=_params("arbitrary" if acc_outs or rider is not None else "parallel"),
    )(*row_ins, *full_ins, *extra)


def _rmsnorm_fwd(x, g, name):
    T, C = x.shape

    def body(x_ref, g_ref, h_ref):
        xv = x_ref[...]
        r = lax.rsqrt(jnp.mean(xv * xv, axis=-1, keepdims=True) + RMS_EPS)
        h_ref[...] = ((xv * r) * g_ref[...]).astype(BF16)

    return _rows_call(body, name, T, C, [x], [g.reshape(1, C)], [BF16])[0]


def _rmsnorm_bwd(x, g, dh, dres, name, rider=None):
    T, C = x.shape

    def body(x_ref, dh_ref, dres_ref, g_ref, dx_ref, dxb_ref, dg_ref):
        @pl.when(pl.program_id(0) == 0)
        def _():
            dg_ref[...] = jnp.zeros_like(dg_ref)

        xv = x_ref[...]
        dhv = dh_ref[...]
        r = lax.rsqrt(jnp.mean(xv * xv, axis=-1, keepdims=True) + RMS_EPS)
        xn = xv * r
        dg_ref[...] += jnp.sum(dhv * xn, axis=0, keepdims=True)
        dxn = dhv * g_ref[...]
        dx = dres_ref[...] + r * (dxn - xn * jnp.mean(dxn * xn, axis=-1, keepdims=True))
        dx_ref[...] = dx
        dxb_ref[...] = dx.astype(BF16)

    return _rows_call(body, name, T, C, [x, dh, dres], [g.reshape(1, C)], [F32, BF16], [(1, C)], rider=rider)


def _heads(C):
    return [slice(h * HEAD_DIM, (h + 1) * HEAD_DIM) for h in range(C // HEAD_DIM)]


def _qknorm_fwd(q, g, name):
    T, C = q.shape

    def body(q_ref, g_ref, o_ref):
        for head in _heads(C):
            xv = q_ref[:, head]
            r = lax.rsqrt(jnp.mean(xv * xv, axis=-1, keepdims=True) + RMS_EPS)
            o_ref[:, head] = ((xv * r) * g_ref[...]).astype(BF16)

    return _rows_call(body, name, T, C, [q], [g.reshape(1, HEAD_DIM)], [BF16])[0]


def _qknorm_bwd(q, g, dqn, name):
    T, C = q.shape

    def body(q_ref, d_ref, g_ref, dq_ref, dg_ref):
        @pl.when(pl.program_id(0) == 0)
        def _():
            dg_ref[...] = jnp.zeros_like(dg_ref)

        dg = jnp.zeros((1, HEAD_DIM), F32)
        for head in _heads(C):
            xv = q_ref[:, head]
            dv = d_ref[:, head]
            r = lax.rsqrt(jnp.mean(xv * xv, axis=-1, keepdims=True) + RMS_EPS)
            xn = xv * r
            dg = dg + jnp.sum(dv * xn, axis=0, keepdims=True)
            dxn = dv * g_ref[...]
            dq_ref[:, head] = (r * (dxn - xn * jnp.mean(dxn * xn, axis=-1, keepdims=True))).astype(BF16)
        dg_ref[...] += dg

    return _rows_call(body, name, T, C, [q, dqn], [g.reshape(1, HEAD_DIM)], [BF16], [(1, HEAD_DIM)])


ATT_TQ = 512
ATT_TK = 256
ATT_HEADS = 2


def _logsig_pair(z):
    a = jnp.minimum(z, 0.0) - jnp.log(1.0 + jnp.exp(-jnp.abs(z)))
    return a, a - z


def _tri(n, strict_upper_src):
    j = lax.broadcasted_iota(jnp.int32, (n, n), 0)
    s = lax.broadcasted_iota(jnp.int32, (n, n), 1)
    if strict_upper_src == "gt":
        m = j > s
    elif strict_upper_src == "le":
        m = j <= s
    else:
        m = j < s
    return jnp.where(m, 1.0, 0.0).astype(BF16)


def _cumdot(x, tri):
    hi, lo = _split(x)
    return _dot(hi, tri) + _dot(lo, tri)


ATT_TQ_BWD = 256


def _attn_tiles(S, tq_cap=ATT_TQ):
    tq, tk = min(tq_cap, S), min(ATT_TK, S)
    return tq, tk, S // tq, tq // tk


def _attn_fwd(qn, kn, v, gate, Bl, S, rider=None):
    T, C = qn.shape
    H = C // HEAD_DIM
    tq, tk, nq, kpq = _attn_tiles(S)
    hp = min(ATT_HEADS, H)
    scale = 1.0 / math.sqrt(HEAD_DIM)

    def body(q_ref, k_ref, v_ref, g_ref, o_ref, bt_ref, og_ref):
        i = pl.program_id(2)
        tri = _tri(tk, "gt")
        rowpos = lax.broadcasted_iota(jnp.int32, (tq, tk), 0) + i * tq
        colpos = lax.broadcasted_iota(jnp.int32, (tq, tk), 1)
        o_ref[...] = jnp.zeros_like(o_ref)
        bt_ref[...] = jnp.zeros_like(bt_ref)

        heads = _heads(hp * HEAD_DIM)

        def diagonal():
            work = []
            for p in reversed(range(kpq)):
                j = i * kpq + p
                rows = pl.ds(pl.multiple_of(j * tk, tk), tk)
                live = slice(p * tk, tq)
                mask = (colpos[live] + j * tk) < rowpos[live]
                for head in heads:
                    a, b = _logsig_pair(_dot(q_ref[live, head], k_ref[rows, head], NT) * scale)
                    b = jnp.where(mask, b, 0.0)
                    work.append((head, rows, live, a, b, _cumdot(b, tri), mask))
            for n, head in enumerate(heads):
                for rows, live, a, b, suffix, mask in [w[1:] for w in work if w[0] == head]:
                    w = jnp.where(mask, jnp.exp(a + suffix + bt_ref[n, live]), 0.0)
                    o_ref[live, head] += _dot(w.astype(BF16), v_ref[rows, head])
                    bt_ref[n, live] += jnp.sum(b, axis=-1, keepdims=True)

        def group(g):
            work = []
            for p in reversed(range(kpq)):
                rows = pl.ds(pl.multiple_of((g * kpq + p) * tk, tk), tk)
                for head in heads:
                    a, b = _logsig_pair(_dot(q_ref[:, head], k_ref[rows, head], NT) * scale)
                    work.append((head, rows, a, b, _cumdot(b, tri)))
            for n, head in enumerate(heads):
                total = bt_ref[n]
                out = None
                for rows, a, b, suffix in [w[1:] for w in work if w[0] == head]:
                    term = _dot(jnp.exp(a + suffix + total).astype(BF16), v_ref[rows, head])
                    out = term if out is None else out + term
                    total = total + jnp.sum(b, axis=-1, keepdims=True)
                o_ref[:, head] += out
                bt_ref[n] = total

        diagonal()

        def step(n, carry):
            group(i - 1 - n)
            return carry

        lax.fori_loop(0, i, step, 0)
        og_ref[...] = (o_ref[...] * _silu_parts(g_ref[...])[0]).astype(BF16)

    qspec = pl.BlockSpec((tq, hp * HEAD_DIM), lambda b, h, i: (b * nq + i, h))
    kspec = pl.BlockSpec((S, hp * HEAD_DIM), lambda b, h, i: (b, h))
    btspec = pl.BlockSpec((None, hp, tq, 1), lambda b, h, i: (b, h, i, 0))
    grid = (Bl, H // hp, nq)
    body, in_specs, out_specs, out_shape, scratch, extra = _ride(
        rider, body, grid, [qspec, kspec, kspec, qspec], [qspec, btspec, qspec],
        [jax.ShapeDtypeStruct((T, C), F32), jax.ShapeDtypeStruct((Bl, H, S, 1), F32),
         jax.ShapeDtypeStruct((T, C), BF16)], [])
    return pl.pallas_call(
        body, name="attn_fwd", grid=grid, in_specs=in_specs, out_specs=out_specs, out_shape=out_shape,
        scratch_shapes=scratch, compiler_params=_params("arbitrary", "arbitrary", "arbitrary"),
    )(qn, kn, v, gate, *extra)


def _attn_bwd(qn, kn, v, do, btot, Bl, S, rider=None):
    T, C = qn.shape
    H = C // HEAD_DIM
    tq, tk, nq, kpq = _attn_tiles(S, ATT_TQ_BWD)
    hp = min(ATT_HEADS, H)
    scale = 1.0 / math.sqrt(HEAD_DIM)

    def body(q_ref, k_ref, v_ref, do_ref, bt_ref, dq_ref, dk_ref, dvb_ref, pb_ref, pdl_ref, dv_ref):
        i = pl.program_id(2)

        @pl.when(i == 0)
        def _():
            dk_ref[...] = jnp.zeros_like(dk_ref)
            dv_ref[...] = jnp.zeros_like(dv_ref)

        tri_le = _tri(tk, "le")
        tri_lt = _tri(tk, "lt")
        rowpos = lax.broadcasted_iota(jnp.int32, (tq, tk), 0) + i * tq
        colpos = lax.broadcasted_iota(jnp.int32, (tq, tk), 1)
        dq_ref[...] = jnp.zeros_like(dq_ref)
        pb_ref[...] = bt_ref[...]
        pdl_ref[...] = jnp.zeros_like(pdl_ref)

        heads = _heads(hp * HEAD_DIM)

        def group(g):
            work = []
            for p in range(kpq):
                rows = pl.ds(pl.multiple_of((g * kpq + p) * tk, tk), tk)
                for head in heads:
                    a, b = _logsig_pair(_dot(q_ref[:, head], k_ref[rows, head], NT) * scale)
                    work.append((head, rows, a, b, _cumdot(b, tri_le), _dot(do_ref[:, head], v_ref[rows, head], NT)))
            for n, head in enumerate(heads):
                remaining = pb_ref[n]
                swept = pdl_ref[n]
                dq = None
                for rows, a, b, cum, dw in [w[1:] for w in work if w[0] == head]:
                    w = jnp.exp(a + (remaining - cum))
                    dl = dw * w
                    prefix = swept + _cumdot(dl, tri_lt)
                    beta = jnp.exp(a)
                    dzb = ((dl * (1.0 - beta) - beta * prefix) * scale).astype(BF16)
                    term = _dot(dzb, k_ref[rows, head])
                    dq = term if dq is None else dq + term
                    dk_ref[rows, head] += _dot(dzb, q_ref[:, head], TN)
                    dv_ref[rows, head] += _dot(w.astype(BF16), do_ref[:, head], TN)
                    remaining = remaining - jnp.sum(b, axis=-1, keepdims=True)
                    swept = swept + jnp.sum(dl, axis=-1, keepdims=True)
                dq_ref[:, head] += dq
                pb_ref[n] = remaining
                pdl_ref[n] = swept

        def diagonal():
            work = []
            for p in range(kpq):
                j = i * kpq + p
                rows = pl.ds(pl.multiple_of(j * tk, tk), tk)
                live = slice(p * tk, tq)
                mask = (colpos[live] + j * tk) < rowpos[live]
                for head in heads:
                    a, b = _logsig_pair(_dot(q_ref[live, head], k_ref[rows, head], NT) * scale)
                    b = jnp.where(mask, b, 0.0)
                    work.append((head, rows, live, a, b, _cumdot(b, tri_le),
                                 _dot(do_ref[live, head], v_ref[rows, head], NT), mask))
            for n, head in enumerate(heads):
                for rows, live, a, b, cum, dw, mask in [w[1:] for w in work if w[0] == head]:
                    w = jnp.where(mask, jnp.exp(a + (pb_ref[n, live] - cum)), 0.0)
                    dl = dw * w
                    prefix = pdl_ref[n, live] + _cumdot(dl, tri_lt)
                    beta = jnp.exp(a)
                    dzb = (jnp.where(mask, dl * (1.0 - beta) - beta * prefix, 0.0) * scale).astype(BF16)
                    dq_ref[live, head] += _dot(dzb, k_ref[rows, head])
                    dk_ref[rows, head] += _dot(dzb, q_ref[live, head], TN)
                    dv_ref[rows, head] += _dot(w.astype(BF16), do_ref[live, head], TN)
                    pb_ref[n, live] -= jnp.sum(b, axis=-1, keepdims=True)
                    pdl_ref[n, live] += jnp.sum(dl, axis=-1, keepdims=True)

        def step(g, carry):
            group(g)
            return carry

        lax.fori_loop(0, i, step, 0)
        diagonal()

        @pl.when(i == nq - 1)
        def _():
            dvb_ref[...] = dv_ref[...].astype(BF16)

    qspec = pl.BlockSpec((tq, hp * HEAD_DIM), lambda b, h, i: (b * nq + i, h))
    kspec = pl.BlockSpec((S, hp * HEAD_DIM), lambda b, h, i: (b, h))
    btspec = pl.BlockSpec((None, hp, tq, 1), lambda b, h, i: (b, h, i, 0))
    grid = (Bl, H // hp, nq)
    body, in_specs, out_specs, out_shape, scratch, extra = _ride(
        rider, body, grid, [qspec, kspec, kspec, qspec, btspec], [qspec, kspec, kspec],
        [jax.ShapeDtypeStruct((T, C), F32), jax.ShapeDtypeStruct((T, C), F32), jax.ShapeDtypeStruct((T, C), BF16)],
        [pltpu.VMEM((hp, tq, 1), F32), pltpu.VMEM((hp, tq, 1), F32), pltpu.VMEM((S, hp * HEAD_DIM), F32)])
    return pl.pallas_call(
        body, name="attn_bwd", grid=grid, in_specs=in_specs, out_specs=out_specs, out_shape=out_shape,
        scratch_shapes=scratch, compiler_params=_params("arbitrary", "arbitrary", "arbitrary"),
    )(qn, kn, v, do, btot, *extra)


SSM_TIME_BLOCK_FWD = 1024
SSM_TIME_BLOCK_BWD = 512
CHUNK = SUBLANES


def _cmadd(xr, xi, ar, ai, sr, si):
    return xr + ar * sr - ai * si, xi + ar * si + ai * sr


def _chunk_scan(xr, xi, tab_ref, cr, ci, reverse):
    for lvl, d in enumerate((1, 2, 4)):
        shift = (CHUNK - d) if reverse else d
        sr = pltpu.roll(xr, shift, 0)
        si = pltpu.roll(xi, shift, 0)
        ar = tab_ref[pl.ds((2 * lvl) * CHUNK, CHUNK), :]
        ai = tab_ref[pl.ds((2 * lvl + 1) * CHUNK, CHUNK), :]
        xr, xi = _cmadd(xr, xi, ar, ai, sr, si)
    pr = tab_ref[pl.ds(6 * CHUNK, CHUNK), :]
    pi = tab_ref[pl.ds(7 * CHUNK, CHUNK), :]
    return _cmadd(xr, xi, pr, pi, cr, ci)


def _ssm_dims(S, C, time_block=SSM_TIME_BLOCK_BWD):
    G = C // GROUP
    GT = min(16, G)
    return G, GT, G // GT, GT * GROUP, GT * STATE, min(time_block, S)


def _ssm_fwd(u, mats, Bl, S):
    T, C = u.shape
    G, GT, ngt, cw, sw, TB = _ssm_dims(S, C, SSM_TIME_BLOCK_FWD)
    ntb = S // TB
    nch = TB // CHUNK

    def body(u_ref, bre_ref, bim_ref, cre_ref, cim_ref, d_ref, tab_ref, y_ref, hr_ref, hi_ref, yg_ref, car_r, car_i):
        @pl.when(pl.program_id(2) == 0)
        def _():
            car_r[...] = jnp.zeros_like(car_r)
            car_i[...] = jnp.zeros_like(car_i)

        uv = u_ref[...]
        hr_ref[...] = _dot_f32(uv, bre_ref[...])
        hi_ref[...] = _dot_f32(uv, bim_ref[...])

        def step(n, carry):
            cr, ci = carry
            rows = pl.ds(pl.multiple_of(n * CHUNK, CHUNK), CHUNK)
            xr, xi = _chunk_scan(hr_ref[rows, :], hi_ref[rows, :], tab_ref, cr, ci, False)
            hr_ref[rows, :] = xr
            hi_ref[rows, :] = xi
            last = (CHUNK - 1, CHUNK)
            return (jnp.broadcast_to(xr[last[0]:last[1], :], xr.shape), jnp.broadcast_to(xi[last[0]:last[1], :], xi.shape))

        cr, ci = lax.fori_loop(0, nch, step, (car_r[...], car_i[...]))
        car_r[...] = cr
        car_i[...] = ci
        y = _dot_f32(hr_ref[...], cre_ref[...]) - _dot_f32(hi_ref[...], cim_ref[...]) + d_ref[...] * uv
        y_ref[...] = y
        yg_ref[...] = _gelu_parts(y)[0].astype(BF16)

    uspec = pl.BlockSpec((TB, cw), lambda g, b, t: (b * ntb + t, g))
    hspec = pl.BlockSpec((TB, sw), lambda g, b, t: (b * ntb + t, g))

    def gspec(r, c):
        return pl.BlockSpec((None, r, c), lambda g, b, t: (g, 0, 0))

    return pl.pallas_call(
        body, name="ssm_fwd", grid=(ngt, Bl, ntb),
        in_specs=[uspec, gspec(cw, sw), gspec(cw, sw), gspec(sw, cw), gspec(sw, cw), gspec(1, cw), gspec(8 * CHUNK, sw)],
        out_specs=[uspec, hspec, hspec, uspec],
        out_shape=[jax.ShapeDtypeStruct((T, C), F32), jax.ShapeDtypeStruct((T, G * STATE), F32),
                   jax.ShapeDtypeStruct((T, G * STATE), F32), jax.ShapeDtypeStruct((T, C), BF16)],
        scratch_shapes=[pltpu.VMEM((CHUNK, sw), F32), pltpu.VMEM((CHUNK, sw), F32)],
        compiler_params=_params("parallel", "arbitrary", "arbitrary"),
    )(u, mats["bbd_re"], mats["bbd_im"], mats["cbd_re"], mats["cbd_im"], mats["d"], mats["tab_fwd"])


def _ssm_bwd(u, dy, h_re, h_im, mats, Bl, S):
    T, C = u.shape
    G, GT, ngt, cw, sw, TB = _ssm_dims(S, C)
    ntb = S // TB
    nch = TB // CHUNK
    rpb = TB // CHUNK

    def body(u_ref, dy_ref, hr_ref, hi_ref, hpr_ref, hpi_ref, cre_ref, cim_ref, bre_ref, bim_ref, d_ref, tab_ref,
             du_ref, dbr_ref, dbi_ref, dcr_ref, dci_ref, dlr_ref, dli_ref, dd_ref, gr_ref, gi_ref, car_r, car_i):
        b = pl.program_id(1)
        t = pl.program_id(2)

        @pl.when((b == 0) & (t == 0))
        def _():
            for ref in (dbr_ref, dbi_ref, dcr_ref, dci_ref, dlr_ref, dli_ref, dd_ref):
                ref[...] = jnp.zeros_like(ref)

        @pl.when(t == 0)
        def _():
            car_r[...] = jnp.zeros_like(car_r)
            car_i[...] = jnp.zeros_like(car_i)

        uv = u_ref[...]
        dyv = dy_ref[...]
        gr_ref[...] = _dot_f32(dyv, cre_ref[...], NT)
        gi_ref[...] = -_dot_f32(dyv, cim_ref[...], NT)
        alive = jnp.where(t == ntb - 1, 0.0, 1.0)
        row0 = lax.broadcasted_iota(jnp.int32, (CHUNK, sw), 0) == 0

        def step(m, carry):
            cr, ci, ar, ai = carry
            n = nch - 1 - m
            rows = pl.ds(pl.multiple_of(n * CHUNK, CHUNK), CHUNK)
            prow = pl.ds(pl.multiple_of(jnp.maximum(n - 1, 0) * CHUNK, CHUNK), CHUNK)
            xr, xi = _chunk_scan(gr_ref[rows, :], gi_ref[rows, :], tab_ref, cr, ci, True)
            gr_ref[rows, :] = xr
            gi_ref[rows, :] = xi
            first = n == 0
            pr = jnp.where(first, hpr_ref[...] * alive, hr_ref[prow, :])
            pi = jnp.where(first, hpi_ref[...] * alive, hi_ref[prow, :])
            sr = jnp.where(row0, pltpu.roll(pr, 1, 0), pltpu.roll(hr_ref[rows, :], 1, 0))
            si = jnp.where(row0, pltpu.roll(pi, 1, 0), pltpu.roll(hi_ref[rows, :], 1, 0))
            ar = ar + xr * sr + xi * si
            ai = ai + xi * sr - xr * si
            return (jnp.broadcast_to(xr[0:1, :], xr.shape), jnp.broadcast_to(xi[0:1, :], xi.shape), ar, ai)

        zero = jnp.zeros((CHUNK, sw), F32)
        cr, ci, ar, ai = lax.fori_loop(0, nch, step, (car_r[...], car_i[...], zero, zero))
        car_r[...] = cr
        car_i[...] = ci
        dlr_ref[...] += ar
        dli_ref[...] += ai
        gr = gr_ref[...]
        gi = gi_ref[...]
        dbr_ref[...] += _dot_f32(uv, gr, TN)
        dbi_ref[...] += _dot_f32(uv, gi, TN)
        dcr_ref[...] += _dot_f32(hr_ref[...], dyv, TN)
        dci_ref[...] -= _dot_f32(hi_ref[...], dyv, TN)
        dd_ref[...] += jnp.sum(dyv * uv, axis=0, keepdims=True)
        du_ref[...] = (_dot_f32(gr, bre_ref[...], NT) + _dot_f32(gi, bim_ref[...], NT) + d_ref[...] * dyv).astype(BF16)

    def tblk(b, t):
        return b * ntb + (ntb - 1 - t)

    uspec = pl.BlockSpec((TB, cw), lambda g, b, t: (tblk(b, t), g))
    hspec = pl.BlockSpec((TB, sw), lambda g, b, t: (tblk(b, t), g))
    hpspec = pl.BlockSpec((CHUNK, sw), lambda g, b, t: (jnp.maximum(tblk(b, t) * rpb - 1, 0), g))

    def gspec(r, c):
        return pl.BlockSpec((None, r, c), lambda g, b, t: (g, 0, 0))

    def gshape(r, c):
        return jax.ShapeDtypeStruct((ngt, r, c), F32)

    return pl.pallas_call(
        body, name="ssm_bwd", grid=(ngt, Bl, ntb),
        in_specs=[uspec, uspec, hspec, hspec, hpspec, hpspec, gspec(sw, cw), gspec(sw, cw), gspec(cw, sw), gspec(cw, sw),
                  gspec(1, cw), gspec(8 * CHUNK, sw)],
        out_specs=[uspec, gspec(cw, sw), gspec(cw, sw), gspec(sw, cw), gspec(sw, cw), gspec(CHUNK, sw), gspec(CHUNK, sw),
                   gspec(1, cw)],
        out_shape=[jax.ShapeDtypeStruct((T, C), BF16), gshape(cw, sw), gshape(cw, sw), gshape(sw, cw), gshape(sw, cw),
                   gshape(CHUNK, sw), gshape(CHUNK, sw), gshape(1, cw)],
        scratch_shapes=[pltpu.VMEM((TB, sw), F32), pltpu.VMEM((TB, sw), F32), pltpu.VMEM((CHUNK, sw), F32),
                        pltpu.VMEM((CHUNK, sw), F32)],
        compiler_params=_params("arbitrary", "arbitrary", "arbitrary"),
    )(u, dy, h_re, h_im, h_re, h_im, mats["cbd_re"], mats["cbd_im"], mats["bbd_re"], mats["bbd_im"], mats["d"],
      mats["tab_rev"])


def _zoh(a_re, a_im, log_dt, b_re, b_im):
    dt = jnp.exp(log_dt)[:, None]
    mag = jnp.exp(a_re * dt)
    l_re = mag * jnp.cos(a_im * dt)
    l_im = mag * jnp.sin(a_im * dt)
    den = a_re * a_re + a_im * a_im
    f_re = ((l_re - 1.0) * a_re + l_im * a_im) / den
    f_im = (l_im * a_re - (l_re - 1.0) * a_im) / den
    bb_re = f_re[..., None] * b_re - f_im[..., None] * b_im
    bb_im = f_re[..., None] * b_im + f_im[..., None] * b_re
    return l_re, l_im, bb_re, bb_im


def _ssm_matrices(a_re, a_im, log_dt, b_re, b_im, c_re, c_im, d, S):
    G = a_re.shape[0]
    _, GT, ngt, cw, sw, _ = _ssm_dims(S, G * GROUP)
    _, _, bb_re, bb_im = _zoh(a_re, a_im, log_dt, b_re, b_im)
    eye = jnp.eye(GT, dtype=BF16)

    def bd_b(bb):
        return jnp.einsum("tgpi,gh->tgihp", bb.astype(BF16).reshape(ngt, GT, STATE, GROUP), eye).reshape(ngt, cw, sw)

    def bd_c(c):
        return jnp.einsum("tgip,gh->tgphi", c.astype(BF16).reshape(ngt, GT, GROUP, STATE), eye).reshape(ngt, sw, cw)

    dt = jnp.exp(log_dt)[:, None]

    def power(k, conj):
        mag = jnp.exp(k * a_re * dt)
        ang = k * a_im * dt
        return (mag * jnp.cos(ang)).reshape(ngt, 1, sw), ((-1.0 if conj else 1.0) * mag * jnp.sin(ang)).reshape(ngt, 1, sw)

    r = jnp.arange(CHUNK)[None, :, None]

    def table(reverse):
        parts = []
        for dd in (1, 2, 4):
            pr, pi = power(float(dd), reverse)
            keep = (r <= CHUNK - 1 - dd) if reverse else (r >= dd)
            parts += [jnp.where(keep, pr, 0.0), jnp.where(keep, pi, 0.0)]
        exps = [(CHUNK - k) if reverse else (k + 1) for k in range(CHUNK)]
        pw = [power(float(e), reverse) for e in exps]
        parts += [jnp.concatenate([p[0] for p in pw], axis=1), jnp.concatenate([p[1] for p in pw], axis=1)]
        return jnp.concatenate([jnp.broadcast_to(p, (ngt, CHUNK, sw)) for p in parts], axis=1)

    return dict(bbd_re=bd_b(bb_re), bbd_im=bd_b(bb_im), cbd_re=bd_c(c_re), cbd_im=bd_c(c_im), d=d.reshape(ngt, 1, cw),
                tab_fwd=table(False), tab_rev=table(True))


def _ssm_unblock(dbr, dbi, dcr, dci, dlr, dli, dd, G):
    ngt = dbr.shape[0]
    GT = G // ngt
    eye = jnp.eye(GT, dtype=F32)

    def ub(x):
        return jnp.einsum("tgihp,gh->tgpi", x.reshape(ngt, GT, GROUP, GT, STATE), eye).reshape(G, STATE, GROUP)

    def uc(x):
        return jnp.einsum("tgphi,gh->tgip", x.reshape(ngt, GT, STATE, GT, GROUP), eye).reshape(G, GROUP, STATE)

    return (dlr.sum(axis=1).reshape(G, STATE), dli.sum(axis=1).reshape(G, STATE), ub(dbr), ub(dbi), uc(dcr), uc(dci),
            dd.reshape(G * GROUP))


def _attn_gate_bwd(o, gate):
    def fn(dog, o_blk, g_blk):
        val, der = _silu_parts(g_blk)
        return dog * val, dog * o_blk * der

    return _Epilogue(fn, [o, gate], [], [BF16, BF16])


def _glu_fwd(y, gate, glu_b):
    def fn(gl, y_blk, g_blk, b_blk):
        return gl, _gelu_parts(y_blk)[0] * _sigmoid(gl + b_blk) * _silu_parts(g_blk)[0]

    return _Epilogue(fn, [y, gate], [glu_b.reshape(1, -1)], [F32, BF16])


def _glu_bwd(dy3, y, gl, gate, glu_b):
    T, C = y.shape

    def body(d_ref, y_ref, gl_ref, g_ref, b_ref, dgl_ref, dgate_ref, t1_ref, db_ref):
        @pl.when(pl.program_id(0) == 0)
        def _():
            db_ref[...] = jnp.zeros_like(db_ref)

        yg = _gelu_parts(y_ref[...])[0]
        sg = _sigmoid(gl_ref[...] + b_ref[...])
        sl, sld = _silu_parts(g_ref[...])
        dv = d_ref[...]
        dy2 = dv * sl
        dgl = dy2 * yg * sg * (1.0 - sg)
        dgl_ref[...] = dgl.astype(BF16)
        dgate_ref[...] = (dv * (yg * sg) * sld).astype(BF16)
        t1_ref[...] = dy2 * sg
        db_ref[...] += jnp.sum(dgl, axis=0, keepdims=True)

    return _rows_call(body, "glu_bwd", T, C, [dy3, y, gl, gate], [glu_b.reshape(1, C)], [BF16, BF16, F32], [(1, C)])


def _gelu_bwd(t1, y):
    return _Epilogue(lambda t2, t1_blk, y_blk: ((t1_blk + t2) * _gelu_parts(y_blk)[1],), [t1, y], [], [F32])


def _loss_head(x2, target):
    T, C = x2.shape

    def body(x_ref, t_ref, d_ref, db_ref, l_ref):
        @pl.when(pl.program_id(0) == 0)
        def _():
            l_ref[...] = jnp.zeros_like(l_ref)

        e = x_ref[...] - t_ref[...]
        d = e * (1.0 / C)
        d_ref[...] = d
        db_ref[...] = d.astype(BF16)
        l_ref[...] += 0.5 * jnp.sum(jnp.sum(e * e, axis=-1, keepdims=True) * (1.0 / C), axis=0, keepdims=True)

    return _rows_call(body, "loss_head", T, C, [x2, target], [], [F32, BF16], [(1, 1)])


def _adamw(w, g, m, v, name):
    shape = w.shape
    C = shape[-1]
    R = w.size // C
    bc1 = 1.0 - ADAM_B1 ** ADAM_STEP
    bc2 = 1.0 - ADAM_B2 ** ADAM_STEP

    def body(w_ref, g_ref, m_ref, v_ref, d_ref, nm_ref, nv_ref):
        gv = g_ref[...]
        mn = ADAM_B1 * m_ref[...] + (1.0 - ADAM_B1) * gv
        vn = ADAM_B2 * v_ref[...] + (1.0 - ADAM_B2) * (gv * gv)
        d_ref[...] = -ADAM_LR * ((mn / bc1) / (jnp.sqrt(vn / bc2) + ADAM_EPS) + ADAM_WD * w_ref[...])
        nm_ref[...] = mn
        nv_ref[...] = vn

    outs = _rows_call(body, name, R, C, [a.reshape(R, C) for a in (w, g, m, v)], [], [F32, F32, F32])
    return [o.reshape(shape) for o in outs]


def _sum_leading(x, name, half=None):
    n, R, C = x.shape
    tr = _row_tile(R, C * n)
    nb = R // tr

    def body(*refs):
        x_ref, o_ref = refs[-2:]
        acc = x_ref[0].astype(F32)
        for k in range(1, n):
            acc = acc + x_ref[k].astype(F32)
        o_ref[...] = acc

    if half is None:
        return pl.pallas_call(
            body, name=name, grid=(nb,), in_specs=[pl.BlockSpec((n, tr, C), lambda i: (0, i, 0))],
            out_specs=pl.BlockSpec((tr, C), lambda i: (i, 0)), out_shape=jax.ShapeDtypeStruct((R, C), F32),
            compiler_params=_params("parallel"),
        )(x)
    grid_spec = pltpu.PrefetchScalarGridSpec(
        num_scalar_prefetch=1, grid=(nb,), in_specs=[pl.BlockSpec((n, tr, C), lambda i, c_ref: (0, i, 0))],
        out_specs=pl.BlockSpec((tr, C), lambda i, c_ref: (c_ref[0] * nb + i, 0)))
    return pl.pallas_call(
        body, name=name, grid_spec=grid_spec, out_shape=jax.ShapeDtypeStruct((2 * R, C), F32),
        compiler_params=_params("parallel"),
    )(half.reshape(1).astype(jnp.int32), x)


def _add_halves(g, c, name):
    full, recv = g
    n, R, C = full.shape
    half = R // 2
    tr = _row_tile(half, C)
    nb = half // tr

    def body(c_ref, a_ref, b_ref, o_ref):
        o_ref[...] = (a_ref[...] + b_ref[...]).astype(BF16)

    grid_spec = pltpu.PrefetchScalarGridSpec(
        num_scalar_prefetch=1, grid=(n, nb),
        in_specs=[pl.BlockSpec((None, tr, C), lambda j, i, c_ref: (j, c_ref[0] * nb + i, 0)),
                  pl.BlockSpec((None, tr, C), lambda j, i, c_ref: (j, i, 0))],
        out_specs=pl.BlockSpec((None, tr, C), lambda j, i, c_ref: (j, i, 0)))
    return pl.pallas_call(
        body, name=name, grid_spec=grid_spec, out_shape=jax.ShapeDtypeStruct((n, half, C), BF16),
        compiler_params=_params("parallel", "parallel"),
    )(c.reshape(1).astype(jnp.int32), full, recv)


ANY = pl.BlockSpec(memory_space=pl.ANY)


def _position():
    return lax.axis_index("x"), lax.axis_index("y"), lax.axis_index("c")


def _all_gather8(blk, name):
    return _standalone(_all_gather_rider(blk), name)[0]


class _Rider:
    def __init__(self, arrays, out_shapes, sems, start, finish):
        self.arrays, self.out_shapes, self.sems, self.start, self.finish = arrays, out_shapes, sems, start, finish


def _join_riders(*riders):
    def split(refs):
        ins, outs, sems = [], [], []
        for r in riders:
            ins.append(refs[:len(r.arrays)])
            refs = refs[len(r.arrays):]
        for r in riders:
            outs.append(refs[:len(r.out_shapes)])
            refs = refs[len(r.out_shapes):]
        for r in riders:
            sems.append(refs[:len(r.sems)])
            refs = refs[len(r.sems):]
        return [i + o + s for i, o, s in zip(ins, outs, sems)]

    def start(*refs):
        for r, own in zip(riders, split(refs)):
            r.start(*own)

    def finish(*refs):
        for r, own in zip(riders, split(refs)):
            r.finish(*own)

    return _Rider([a for r in riders for a in r.arrays], [s for r in riders for s in r.out_shapes],
                  [s for r in riders for s in r.sems], start, finish)


def _standalone(rider, name):
    def body(*refs):
        rider.start(*refs)
        rider.finish(*refs)

    return pl.pallas_call(
        body, name=name, in_specs=[ANY] * len(rider.arrays), out_specs=[ANY] * len(rider.out_shapes),
        out_shape=rider.out_shapes, scratch_shapes=rider.sems,
    )(*rider.arrays)


def _ride(rider, body, grid, in_specs, out_specs, out_shape, scratch):
    if rider is None:
        return body, in_specs, out_specs, out_shape, scratch, []
    ni, no, ns = len(in_specs), len(out_specs), len(scratch)
    ri, ro = len(rider.arrays), len(rider.out_shapes)

    def full(*refs):
        ins, refs = refs[:ni], refs[ni:]
        r_ins, refs = refs[:ri], refs[ri:]
        outs, refs = refs[:no], refs[no:]
        r_outs, refs = refs[:ro], refs[ro:]
        scr, r_sems = refs[:ns], refs[ns:]
        ids = [pl.program_id(a) for a in range(len(grid))]
        first = functools.reduce(jnp.logical_and, [i == 0 for i in ids])
        last = functools.reduce(jnp.logical_and, [i == g - 1 for i, g in zip(ids, grid)])

        @pl.when(first)
        def _():
            rider.start(*r_ins, *r_outs, *r_sems)

        body(*ins, *outs, *scr)

        @pl.when(last)
        def _():
            rider.finish(*r_ins, *r_outs, *r_sems)

    return (full, in_specs + [ANY] * ri, out_specs + [ANY] * ro, out_shape + rider.out_shapes, scratch + rider.sems,
            rider.arrays)


def _all_gather_rider(blk):
    M, N = blk.shape

    def copies(x_ref, out_ref, send_sems, recv_sems, local_sem):
        x, y, c = _position()
        me, sibling = (x, y, c), (x, y, 1 - c)
        chips = [(1 - x, y), (x, 1 - y), (1 - x, 1 - y)]

        def slab(px, py, pc):
            return out_ref.at[4 * px + 2 * py + pc]

        def copy(k, block, to, src=None):
            return pltpu.make_async_remote_copy(
                src_ref=slab(*block) if src is None else src, dst_ref=slab(*block),
                send_sem=send_sems.at[k], recv_sem=recv_sems.at[k], device_id=to, device_id_type=MESH)

        mine = pltpu.make_async_copy(x_ref, slab(*me), local_sem)
        first = [copy(0, me, sibling, src=x_ref)]
        first += [copy(1 + j, me, (*chip, c), src=x_ref) for j, chip in enumerate(chips)]
        passed = [copy(4 + j, (*chip, c), sibling) for j, chip in enumerate(chips)]
        arrivals = [copy(1 + j, (*chip, c), me) for j, chip in enumerate(chips)]
        from_sibling = [copy(0, sibling, me)] + [copy(4 + j, (*chip, 1 - c), me) for j, chip in enumerate(chips)]
        return mine, first, passed, arrivals, from_sibling

    def start(*refs):
        mine, first, _, _, _ = copies(*refs)
        mine.start()
        for cp in first:
            cp.start()

    def finish(*refs):
        mine, first, passed, arrivals, from_sibling = copies(*refs)
        for arrival, onward in zip(arrivals, passed):
            arrival.wait_recv()
            onward.start()
        for cp in from_sibling:
            cp.wait_recv()
        for cp in first + passed:
            cp.wait_send()
        mine.wait()

    return _Rider([blk], [jax.ShapeDtypeStruct((N_DEV, M, N), blk.dtype)],
                  [pltpu.SemaphoreType.DMA((7,)), pltpu.SemaphoreType.DMA((7,)), pltpu.SemaphoreType.DMA], start, finish)


def _all_gather_forwarding(blk, name):
    M, N = blk.shape
    half = M // 2

    def body(x_ref, out_ref, send_sems, recv_sems, local_sem):
        x, y, c = _position()
        me, sibling = (x, y, c), (x, y, 1 - c)
        xn, yn, dg = (1 - x, y), (x, 1 - y), (1 - x, 1 - y)

        def slab(chip, pc, part=None):
            ref = out_ref.at[4 * chip[0] + 2 * chip[1] + pc]
            return ref if part is None else ref.at[pl.ds(part * half, half), :]

        def copy(k, place, to, src=None):
            return pltpu.make_async_remote_copy(src_ref=place if src is None else src, dst_ref=place,
                                                send_sem=send_sems.at[k], recv_sem=recv_sems.at[k], device_id=to,
                                                device_id_type=MESH)

        mine = pltpu.make_async_copy(x_ref, slab((x, y), c), local_sem)
        own = [copy(0, slab((x, y), c), sibling, src=x_ref), copy(1, slab((x, y), c), (*xn, c), src=x_ref),
               copy(2, slab((x, y), c), (*yn, c), src=x_ref)]
        mine.start()
        for cp in own:
            cp.start()
        copy(1, slab(xn, c), me).wait_recv()
        onward = [copy(3, slab(xn, c, 0), (*yn, c)), copy(5, slab(xn, c), sibling)]
        for cp in onward:
            cp.start()
        copy(2, slab(yn, c), me).wait_recv()
        onward += [copy(4, slab(yn, c, 1), (*xn, c)), copy(6, slab(yn, c), sibling)]
        for cp in onward[2:]:
            cp.start()
        copy(3, slab(dg, c, 0), me).wait_recv()
        onward.append(copy(7, slab(dg, c, 0), sibling))
        onward[-1].start()
        copy(4, slab(dg, c, 1), me).wait_recv()
        onward.append(copy(8, slab(dg, c, 1), sibling))
        onward[-1].start()
        for k, place in ((0, slab((x, y), 1 - c)), (5, slab(xn, 1 - c)), (6, slab(yn, 1 - c)), (7, slab(dg, 1 - c, 0)),
                         (8, slab(dg, 1 - c, 1))):
            copy(k, place, me).wait_recv()
        for cp in own + onward:
            cp.wait_send()
        mine.wait()

    return pl.pallas_call(
        body, name=name, in_specs=[ANY], out_specs=ANY, out_shape=jax.ShapeDtypeStruct((N_DEV, M, N), blk.dtype),
        scratch_shapes=[pltpu.SemaphoreType.DMA((9,)), pltpu.SemaphoreType.DMA((9,)), pltpu.SemaphoreType.DMA],
    )(blk)


def _sibling_half_rider(g):
    n, R, C = g.shape
    half = R // 2

    def copy(g_ref, out_ref, send_sem, recv_sem):
        x, y, c = _position()
        return pltpu.make_async_remote_copy(
            src_ref=g_ref.at[:, pl.ds((1 - c) * half, half), :], dst_ref=out_ref, send_sem=send_sem, recv_sem=recv_sem,
            device_id=(x, y, 1 - c), device_id_type=MESH)

    return _Rider([g], [jax.ShapeDtypeStruct((n, half, C), g.dtype)], [pltpu.SemaphoreType.DMA, pltpu.SemaphoreType.DMA],
                  lambda *refs: copy(*refs).start(), lambda *refs: copy(*refs).wait())


def _sibling_send_half(g, name):
    return _standalone(_sibling_half_rider(g), name)[0]


def _chip_exchange_rider(p, row_off=0, rows=None):
    rows = p.shape[1] if rows is None else rows

    def copies(p_ref, out_ref, send_sems, recv_sems, local_sem):
        x, y, c = _position()
        my = 2 * x + y
        chips = [(1 - x, y), (x, 1 - y), (1 - x, 1 - y)]

        def src(slab):
            return p_ref.at[slab, pl.ds(row_off, rows), :]

        mine = pltpu.make_async_copy(src(my), out_ref.at[my], local_sem)
        sends = [pltpu.make_async_remote_copy(
            src_ref=src(2 * px + py), dst_ref=out_ref.at[my], send_sem=send_sems.at[k], recv_sem=recv_sems.at[k],
            device_id=(px, py, c), device_id_type=MESH) for k, (px, py) in enumerate(chips)]
        arrivals = [pltpu.make_async_remote_copy(
            src_ref=src(my), dst_ref=out_ref.at[2 * px + py], send_sem=send_sems.at[k], recv_sem=recv_sems.at[k],
            device_id=(px, py, c), device_id_type=MESH) for k, (px, py) in enumerate(chips)]
        return mine, sends, arrivals

    def start(*refs):
        mine, sends, _ = copies(*refs)
        mine.start()
        for cp in sends:
            cp.start()

    def finish(*refs):
        mine, sends, arrivals = copies(*refs)
        for cp in arrivals:
            cp.wait_recv()
        for cp in sends:
            cp.wait_send()
        mine.wait()

    return _Rider([p], [jax.ShapeDtypeStruct((p.shape[0], rows, p.shape[2]), p.dtype)],
                  [pltpu.SemaphoreType.DMA((3,)), pltpu.SemaphoreType.DMA((3,)), pltpu.SemaphoreType.DMA], start, finish)


def _sibling_fill(buf, name):
    M = buf.shape[0] // 2

    def body(buf_ref, out_ref, send_sem, recv_sem):
        x, y, c = _position()
        mine = pl.ds(c * M, M)
        theirs = pl.ds((1 - c) * M, M)
        pltpu.make_async_remote_copy(src_ref=buf_ref.at[mine], dst_ref=out_ref.at[mine], send_sem=send_sem,
                                     recv_sem=recv_sem, device_id=(x, y, 1 - c), device_id_type=MESH).start()
        pltpu.make_async_remote_copy(src_ref=buf_ref.at[mine], dst_ref=out_ref.at[theirs], send_sem=send_sem,
                                     recv_sem=recv_sem, device_id=(x, y, 1 - c), device_id_type=MESH).wait()

    return pl.pallas_call(
        body, name=name, in_specs=[ANY], out_specs=ANY, out_shape=jax.ShapeDtypeStruct(buf.shape, buf.dtype),
        scratch_shapes=[pltpu.SemaphoreType.DMA, pltpu.SemaphoreType.DMA], input_output_aliases={0: 0},
    )(buf)


def _pack_rest(attn_out, ssm_in, glu_w, ssm_out):
    hd = ssm_in.shape[0] // 2
    return jnp.concatenate([attn_out, jnp.concatenate([ssm_in[:hd], ssm_in[hd:]], axis=1), glu_w, ssm_out], axis=0)


def _unpack_rest(p):
    D = p.shape[-1]
    q, hd = D // N_CHIPS, D // 2
    o = [0, q, q + hd, 2 * q + hd, 3 * q + hd]
    ssm_in = p[o[1]:o[2]]
    return p[o[0]:o[1]], jnp.concatenate([ssm_in[:, :hd], ssm_in[:, hd:]], axis=0), p[o[2]:o[3]], p[o[3]:o[4]]


def _attn_in_views(buf_a):
    D = buf_a.shape[-1]
    return [_View(buf_a, (D, D), lambda r, c, j=j: (j, r, c), D, D) for j in range(N_CHIPS)]


def _rest_views(buf_b):
    D = buf_b.shape[-1]
    q, hd = D // N_CHIPS, D // 2
    o_out, o_in, o_glu, o_sout = 0, q, q + hd, 2 * q + hd

    def row_sharded(off):
        return _View(buf_b, (D, D), lambda r, c: (r // q, off + r % q, c), math.gcd(q, off), D)

    def ssm_in(part):
        return _View(buf_b, (D, D), lambda r, c: (2 * part + c // hd, o_in + r % hd, (r // hd) * hd + c % hd),
                     math.gcd(hd, o_in), hd)

    return dict(attn_out=row_sharded(o_out), ssm_in_u=ssm_in(0), ssm_in_gate=ssm_in(1), glu=row_sharded(o_glu),
                ssm_out=row_sharded(o_sout))


def _pack_small(parts):
    flat = jnp.concatenate([p.reshape(-1) for p in parts])
    pad = (-flat.size) % (2 * SUBLANES * LANES)
    return jnp.pad(flat, (0, pad)).reshape(-1, LANES)


def _unpack_small(buf, shapes):
    flat = buf.reshape(-1)
    out, off = [], 0
    for s in shapes:
        n = math.prod(s)
        out.append(flat[off:off + n].reshape(s))
        off += n
    return out


def _local_step(x, target, norm_g, q_g, k_g, wa, wb, ssm_small, core=None):
    Bl, S, D = x.shape
    T = Bl * S
    x0 = x.reshape(T, D)
    tgt = target.reshape(T, D)
    a_re, a_im, log_dt, b_re, b_im, c_re, c_im, d_skip, glu_b = ssm_small
    G = a_re.shape[0]
    mats = _ssm_matrices(a_re, a_im, log_dt, b_re, b_im, c_re, c_im, d_skip, S)
    w_in = _attn_in_views(wa)
    rows_b = wb.shape[-2] * (1 if core is None else 2)
    ga = lax.empty(wa.shape, F32)
    gb = lax.empty((N_CHIPS, rows_b, D), F32)

    def wgrad(a, b, key, name):
        return _matmul(a, b, name=name, ta=True, out=_rest_views(gb)[key])

    h0 = _rmsnorm_fwd(x0, norm_g[0], "norm0_fwd")
    q, k, v, gate = [_matmul(h0, w_in[j], name=f"attn_in_{j}", out_dtype=(BF16 if j == 2 else F32)) for j in range(4)]
    qn = _qknorm_fwd(q, q_g, "qnorm_fwd")
    kn = _qknorm_fwd(k, k_g, "knorm_fwd")
    if core is None:
        o, btot, og = _attn_fwd(qn, kn, v, gate, Bl, S)
    else:
        o, btot, og, wb = _attn_fwd(qn, kn, v, gate, Bl, S, rider=_all_gather_rider(wb))
        wb = wb.reshape(N_CHIPS, rows_b, D)
    w = _rest_views(wb)
    x1 = _matmul(og, w["attn_out"], name="attn_out", residual=x0)

    h1 = _rmsnorm_fwd(x1, norm_g[1], "norm1_fwd")
    u = _matmul(h1, w["ssm_in_u"], name="ssm_in_u")
    gate2 = _matmul(h1, w["ssm_in_gate"], name="ssm_in_gate")
    y, hs_re, hs_im, yg = _ssm_fwd(u, mats, Bl, S)
    gl, y3 = _matmul(yg, w["glu"], name="glu_mm", epilogue=_glu_fwd(y, gate2, glu_b))
    x2 = _matmul(y3, w["ssm_out"], name="ssm_out", residual=x1)

    dx2, dx2b, loss = _loss_head(x2, tgt)

    dy3 = _matmul(dx2b, w["ssm_out"], name="ssm_out_dgrad", tb=True)
    gb = wgrad(y3, dx2b, "ssm_out", "ssm_out_wgrad")
    dgl, dgate2, t1, dglu_b = _glu_bwd(dy3, y, gl, gate2, glu_b)
    dy = _matmul(dgl, w["glu"], name="glu_dgrad", tb=True, epilogue=_gelu_bwd(t1, y))
    gb = wgrad(yg, dgl, "glu", "glu_wgrad")
    du, dbr, dbi, dcr, dci, dlr, dli, dd = _ssm_bwd(u, dy, hs_re, hs_im, mats, Bl, S)
    dh1 = _matmul(du, w["ssm_in_u"], name="ssm_in_dgrad_u", tb=True)
    dh1 = _matmul(dgate2, w["ssm_in_gate"], name="ssm_in_dgrad_gate", tb=True, residual=dh1)
    gb = wgrad(h1, du, "ssm_in_u", "ssm_in_wgrad_u")
    gb = wgrad(h1, dgate2, "ssm_in_gate", "ssm_in_wgrad_gate")
    dx1, dx1b, dng1 = _rmsnorm_bwd(x1, norm_g[1], dh1, dx2, "norm1_bwd")
    small_early = (dng1,) + _ssm_unblock(dbr, dbi, dcr, dci, dlr, dli, dd, G) + (dglu_b.reshape(D),)

    gb = wgrad(og, dx1b, "attn_out", "attn_out_wgrad")
    if core is None:
        do, dgate = _matmul(dx1b, w["attn_out"], name="attn_out_dgrad", tb=True, epilogue=_attn_gate_bwd(o, gate))
        dqn, dkn, dv = _attn_bwd(qn, kn, v, do, btot, Bl, S)
    else:
        do, dgate, from_sibling = _matmul(dx1b, w["attn_out"], name="attn_out_dgrad", tb=True,
                                          epilogue=_attn_gate_bwd(o, gate), rider=_sibling_half_rider(gb))
        chip_sum_b = _add_halves((gb, from_sibling), core, "grads_b_add_halves")
        dqn, dkn, dv, gb, small_early = _attn_bwd(
            qn, kn, v, do, btot, Bl, S,
            rider=_join_riders(_chip_exchange_rider(chip_sum_b), _all_gather_rider(_pack_small(small_early))))
    dq, dqg = _qknorm_bwd(q, q_g, dqn, "qnorm_bwd")
    dk, dkg = _qknorm_bwd(k, k_g, dkn, "knorm_bwd")
    dproj = [dq, dk, dv, dgate]
    for j in range(4):
        ga = _matmul(h0, dproj[j], name=f"attn_in_wgrad_{j}", ta=True, out=_attn_in_views(ga)[j])
    dh0 = None
    if core is None:
        for j in range(4):
            dh0 = _matmul(dproj[j], w_in[j], name=f"attn_in_dgrad_{j}", tb=True, residual=dh0)
        dx0, _, dng0 = _rmsnorm_bwd(x0, norm_g[0], dh0, dx1, "norm0_bwd")
    else:
        chip_sum_a = _add_halves((ga, _sibling_send_half(ga, "grads_a_sibling_half")), core, "grads_a_add_halves")
        total = chip_sum_a.shape[1]
        rows = min(pl.cdiv(total * 7 // 32, BF16_ROWS) * BF16_ROWS, (total - BF16_ROWS) // 4 // BF16_ROWS * BF16_ROWS)
        parts = []
        for j in range(4):
            dh0, part = _matmul(dproj[j], w_in[j], name=f"attn_in_dgrad_{j}", tb=True, residual=dh0,
                                rider=_chip_exchange_rider(chip_sum_a, j * rows, rows))
            parts.append(part)
        dx0, _, dng0, part = _rmsnorm_bwd(x0, norm_g[0], dh0, dx1, "norm0_bwd",
                                          rider=_chip_exchange_rider(chip_sum_a, 4 * rows, total - 4 * rows))
        ga = jnp.concatenate(parts + [part], axis=1)

    return loss, dx0.reshape(Bl, S, D), ga, gb, small_early, (dng0, dqg, dkg)


def _small_early_shapes(D):
    G = D // GROUP
    return [(1, D), (G, STATE), (G, STATE), (G, STATE, GROUP), (G, STATE, GROUP), (G, GROUP, STATE), (G, GROUP, STATE),
            (D,), (D,)]


def _chip_rows(a, chip, n_per):
    return lax.dynamic_slice_in_dim(a, chip * n_per, n_per, axis=0)


def kernel(x, norm_g, attn_w_in, attn_q_g, attn_k_g, attn_w_out, ssm_w_in, ssm_A_re, ssm_A_im, ssm_log_dt, ssm_B_re, ssm_B_im, ssm_C_re, ssm_C_im, ssm_D, ssm_glu_w, ssm_glu_b, ssm_w_out, loss_target, m_norm_g, m_attn_w_in, m_attn_q_g, m_attn_k_g, m_attn_w_out, m_ssm_w_in, m_ssm_A_re, m_ssm_A_im, m_ssm_log_dt, m_ssm_B_re, m_ssm_B_im, m_ssm_C_re, m_ssm_C_im, m_ssm_D, m_ssm_glu_w, m_ssm_glu_b, m_ssm_w_out, v_norm_g, v_attn_w_in, v_attn_q_g, v_attn_k_g, v_attn_w_out, v_ssm_w_in, v_ssm_A_re, v_ssm_A_im, v_ssm_log_dt, v_ssm_B_re, v_ssm_B_im, v_ssm_C_re, v_ssm_C_im, v_ssm_D, v_ssm_glu_w, v_ssm_glu_b, v_ssm_w_out):
    D = x.shape[-1]
    cx, cy, cc = _position()
    chip = 2 * cx + cy
    G = D // GROUP
    Gl = G // N_CHIPS

    def my_half(a):
        return lax.dynamic_slice_in_dim(a, cc * (a.shape[0] // 2), a.shape[0] // 2, axis=0)

    wa = _all_gather_forwarding(my_half(attn_w_in[0].astype(BF16)), "attn_in_all_gather").reshape(N_CHIPS, D, D)
    wb_half = my_half(_pack_rest(attn_w_out[0], ssm_w_in[0], ssm_glu_w[0], ssm_w_out[0]).astype(BF16))

    ssm_local = [ssm_A_re[0], ssm_A_im[0], ssm_log_dt[0], ssm_B_re[0], ssm_B_im[0], ssm_C_re[0], ssm_C_im[0], ssm_D[0],
                 ssm_glu_b[0]]
    small_local = _pack_small(ssm_local)
    half_rows = small_local.shape[0] // 2
    small_half = lax.dynamic_slice_in_dim(small_local, cc * half_rows, half_rows, axis=0)
    small_all = _all_gather8(small_half, "ssm_params_all_gather").reshape(N_CHIPS, 2 * half_rows, LANES)
    per_chip = [_unpack_small(small_all[j], [p.shape for p in ssm_local]) for j in range(N_CHIPS)]
    ssm_full = [jnp.concatenate([per_chip[j][i] for j in range(N_CHIPS)], axis=0) for i in range(len(ssm_local))]

    loss, grad_x, gathered_a, gathered_b, small_early, small_late = _local_step(
        x, loss_target, norm_g, attn_q_g[0], attn_k_g[0], wa, wb_half, ssm_full, core=cc)
    loss = lax.psum(loss[0, 0], ("x", "y", "c"))

    q4 = D // N_CHIPS
    def both_halves(gathered, tag):
        return _sibling_fill(_sum_leading(gathered, f"grads_{tag}_sum_chips", half=cc), f"grads_{tag}_sibling_fill")

    gd = (both_halves(gathered_a, "a"),) + _unpack_rest(both_halves(gathered_b, "b"))

    (dng1, dl_re, dl_im, dbb_re, dbb_im, dc_re, dc_im, dd_skip, dglu_b) = _unpack_small(
        _sum_leading(small_early, "small_early_sum"), _small_early_shapes(D))
    late_all = _all_gather8(_pack_small(small_late), "small_late_all_gather")
    dng0, dqg, dkg = _unpack_small(_sum_leading(late_all, "small_late_sum"), [s.shape for s in small_late])
    dng = jnp.concatenate([dng0, dng1], axis=0)
    a_re, a_im, log_dt, b_re, b_im = ssm_local[:5]
    _, zoh_vjp = jax.vjp(_zoh, a_re, a_im, log_dt, b_re, b_im)
    da_re, da_im, dlog_dt, db_re, db_im = zoh_vjp((_chip_rows(dl_re, chip, Gl), _chip_rows(dl_im, chip, Gl),
                                                   _chip_rows(dbb_re, chip, Gl), _chip_rows(dbb_im, chip, Gl)))
    grads = {
        "norm_g": dng, "attn_w_in": gd[0][None], "attn_q_g": dqg, "attn_k_g": dkg, "attn_w_out": gd[1][None],
        "ssm_w_in": gd[2][None], "ssm_A_re": da_re[None], "ssm_A_im": da_im[None], "ssm_log_dt": dlog_dt[None],
        "ssm_B_re": db_re[None], "ssm_B_im": db_im[None], "ssm_C_re": _chip_rows(dc_re, chip, Gl)[None],
        "ssm_C_im": _chip_rows(dc_im, chip, Gl)[None], "ssm_D": _chip_rows(dd_skip, chip, q4)[None],
        "ssm_glu_w": gd[3][None], "ssm_glu_b": _chip_rows(dglu_b, chip, q4)[None], "ssm_w_out": gd[4][None],
    }
    weights = dict(norm_g=norm_g, attn_w_in=attn_w_in, attn_q_g=attn_q_g, attn_k_g=attn_k_g, attn_w_out=attn_w_out,
                   ssm_w_in=ssm_w_in, ssm_A_re=ssm_A_re, ssm_A_im=ssm_A_im, ssm_log_dt=ssm_log_dt, ssm_B_re=ssm_B_re,
                   ssm_B_im=ssm_B_im, ssm_C_re=ssm_C_re, ssm_C_im=ssm_C_im, ssm_D=ssm_D, ssm_glu_w=ssm_glu_w,
                   ssm_glu_b=ssm_glu_b, ssm_w_out=ssm_w_out)
    m = dict(norm_g=m_norm_g, attn_w_in=m_attn_w_in, attn_q_g=m_attn_q_g, attn_k_g=m_attn_k_g, attn_w_out=m_attn_w_out,
             ssm_w_in=m_ssm_w_in, ssm_A_re=m_ssm_A_re, ssm_A_im=m_ssm_A_im, ssm_log_dt=m_ssm_log_dt, ssm_B_re=m_ssm_B_re,
             ssm_B_im=m_ssm_B_im, ssm_C_re=m_ssm_C_re, ssm_C_im=m_ssm_C_im, ssm_D=m_ssm_D, ssm_glu_w=m_ssm_glu_w,
             ssm_glu_b=m_ssm_glu_b, ssm_w_out=m_ssm_w_out)
    v = dict(norm_g=v_norm_g, attn_w_in=v_attn_w_in, attn_q_g=v_attn_q_g, attn_k_g=v_attn_k_g, attn_w_out=v_attn_w_out,
             ssm_w_in=v_ssm_w_in, ssm_A_re=v_ssm_A_re, ssm_A_im=v_ssm_A_im, ssm_log_dt=v_ssm_log_dt, ssm_B_re=v_ssm_B_re,
             ssm_B_im=v_ssm_B_im, ssm_C_re=v_ssm_C_re, ssm_C_im=v_ssm_C_im, ssm_D=v_ssm_D, ssm_glu_w=v_ssm_glu_w,
             ssm_glu_b=v_ssm_glu_b, ssm_w_out=v_ssm_w_out)
    names = list(weights)
    dense_names = ("attn_w_in", "attn_w_out", "ssm_w_in", "ssm_glu_w", "ssm_w_out")
    delta, new_m, new_v = {}, {}, {}
    for n in dense_names:
        delta[n], new_m[n], new_v[n] = _adamw(weights[n], grads[n], m[n], v[n], "adamw_" + n)
    small_names = [n for n in names if n not in dense_names]
    small_shapes = [weights[n].shape for n in small_names]
    packs = [_pack_small([d[n] for n in small_names]) for d in (weights, grads, m, v)]
    outs = _adamw(*packs, "adamw_small")
    for res, out in zip((delta, new_m, new_v), outs):
        for n, val in zip(small_names, _unpack_small(out, small_shapes)):
            res[n] = val
    return (loss, grad_x, *[grads[n] for n in names], *[delta[n] for n in names], *[new_m[n] for n in names],
            *[new_v[n] for n in names])
```
